```python
import math
import jax, jax.numpy as jnp
from jax import lax
import numpy as np

D_MODEL = 1024
BATCH = 8
SEQ = 4096
DEPTH = 2

MIX_A = 512
MLSTM_HEADS = 4
MLSTM_DH = MIX_A // MLSTM_HEADS
MLSTM_CHUNK = 128
CONV_K = 4
MIX_B = D_MODEL - MIX_A
S5_GROUP = 16
S5_GROUPS = MIX_B // S5_GROUP
S5_STATE = 64
DT_MIN = 1e-3
DT_MAX = 1e-1
IN_A = 4 * MIX_A + 2 * MLSTM_HEADS + MIX_B

NSA_HEADS = 16
NSA_KV = 4
NSA_DH = D_MODEL // NSA_HEADS
KV_W = NSA_KV * NSA_DH
CMP_BLOCK = 32
CMP_STRIDE = 16
CMP_HIDDEN = 256
SEL_BLOCK = 64
SEL_TOPK = 16
WINDOW = 512
NSA_QBLOCK = 32
IN_C = D_MODEL + 6 * KV_W + 3 * NSA_HEADS
FORCE = 1e9

REL_BUCKETS = 32
REL_MAX_DIST = 128

MOE_GROUPS = 4
MOE_PER_GROUP = 4
MOE_EXPERTS = MOE_GROUPS * MOE_PER_GROUP
MOE_HIDDEN = 256
MOE_TOPK = 2

N_EVEN = (DEPTH + 1) // 2
N_ODD = DEPTH // 2
EPS = 1e-6
NEG = -1e30

kernel_name = 'hybrid_mlstm_s5_nsa_hmoe'


def _rmsnorm(x, g):
    xf = x.astype(jnp.float32)
    y = xf * lax.rsqrt(jnp.mean(xf * xf, axis=-1, keepdims=True) + EPS)
    return (y * g.astype(jnp.float32)).astype(x.dtype)


def _modulate(x, g, shift, scale):
    return _rmsnorm(x, g) * (1 + scale[:, None, :]) + shift[:, None, :]


def _masked_softmax(s, mask, axis):
    p = jax.nn.softmax(jnp.where(mask, s, NEG), axis=axis)
    return jnp.where(mask, p, 0.0)


def _t5_bucket(dist):
    dist = jnp.maximum(dist, 0)
    max_exact = REL_BUCKETS // 2
    log_ratio = jnp.log(jnp.maximum(dist, 1).astype(jnp.float32) / max_exact) / math.log(REL_MAX_DIST / max_exact)
    large = jnp.minimum(max_exact + (log_ratio * (REL_BUCKETS - max_exact)).astype(jnp.int32), REL_BUCKETS - 1)
    return jnp.where(dist < max_exact, dist, large)


def _causal_conv(x, w):
    return lax.conv_general_dilated(x, w[:, None, :].astype(x.dtype), window_strides=(1,),
                                    padding=[(CONV_K - 1, 0)],
                                    dimension_numbers=('NWC', 'WIO', 'NWC'),
                                    feature_group_count=x.shape[-1])


def _mlstm_chunkwise(q, k, v, i_pre, f_pre):
    f32 = jnp.float32
    B, S, H, DH = q.shape
    L = MLSTM_CHUNK
    NC = S // L
    ch = lambda t: t.reshape((B, NC, L) + t.shape[2:])
    q, k, v = ch(q), ch(k) * DH ** -0.5, ch(v)
    li = ch(i_pre)
    b = jnp.cumsum(jax.nn.log_sigmoid(ch(f_pre)), axis=2)
    b_last = b[:, :, -1]
    w = b_last[:, :, None] - b + li
    m_loc = jnp.max(w, axis=2)
    e = jnp.exp(w - m_loc[:, :, None])
    c_loc = jnp.einsum('bclh,bclhd,bclhe->bchde', e, v, k)
    n_loc = jnp.einsum('bclh,bclhe->bche', e, k)

    def step(carry, inp):
        c_st, n_st, m_st = carry
        c_l, n_l, m_l, b_l = inp
        m_new = jnp.maximum(b_l + m_st, m_l)
        a = jnp.exp(b_l + m_st - m_new)
        s = jnp.exp(m_l - m_new)
        new = (a[..., None, None] * c_st + s[..., None, None] * c_l,
               a[..., None] * n_st + s[..., None] * n_l, m_new)
        return new, (c_st, n_st, m_st)

    init = (jnp.zeros((B, H, DH, DH), f32), jnp.zeros((B, H, DH), f32), jnp.zeros((B, H), f32))
    xs = tuple(jnp.moveaxis(t, 1, 0) for t in (c_loc, n_loc, m_loc, b_last))
    _, (c0, n0, m0) = lax.scan(step, init, xs)
    c0, n0, m0 = (jnp.moveaxis(t, 0, 1) for t in (c0, n0, m0))

    causal = jnp.tril(jnp.ones((L, L), bool))
    log_d = jnp.where(causal[:, :, None], b[:, :, :, None] - b[:, :, None] + li[:, :, None], -jnp.inf)
    log_inter = b + m0[:, :, None]
    m_t = jnp.maximum(log_inter, jnp.max(log_d, axis=3))
    dmat = jnp.exp(log_d - m_t[:, :, :, None])
    a_inter = jnp.exp(log_inter - m_t)
    s = jnp.einsum('bcthd,bcshd->bctsh', q, k) * dmat
    num = jnp.einsum('bctsh,bcshd->bcthd', s, v) + a_inter[..., None] * jnp.einsum('bchde,bcthe->bcthd', c0, q)
    den = jnp.sum(s, axis=3) + a_inter * jnp.einsum('bche,bcthe->bcth', n0, q)
    h = num / jnp.maximum(jnp.abs(den), jnp.exp(-m_t))[..., None]
    return h.reshape(B, S, H, DH)


def _s5_ssm(u, lam_re, lam_im, log_dt, b_re, b_im, c_re, c_im, d_skip):
    f32 = jnp.float32
    lam = lax.complex(lam_re.astype(f32), lam_im.astype(f32))
    dt = jnp.exp(log_dt.astype(f32))[:, None]
    lam_bar = jnp.exp(lam * dt)
    b_bar = ((lam_bar - 1.0) / lam)[..., None] * lax.complex(b_re.astype(f32), b_im.astype(f32))
    c_mat = lax.complex(c_re.astype(f32), c_im.astype(f32))
    bu = jnp.einsum('gpc,bsgc->sbgp', b_bar, u.astype(jnp.complex64))
    a = jnp.broadcast_to(lam_bar, (u.shape[1],) + lam_bar.shape)

    def combine(e1, e2):
        a1, x1 = e1
        a2, x2 = e2
        return a1 * a2, a2[:, None] * x1 + x2

    _, states = lax.associative_scan(combine, (a, bu), axis=0)
    y = jnp.einsum('gcp,sbgp->bsgc', c_mat, states).real
    return y + d_skip.astype(f32) * u


def _mlstm_s5_mixer(h, w_in, conv_w, b_i, b_f, head_g, lam_re, lam_im, log_dt,
                    b_re, b_im, c_re, c_im, d_skip, glu_w, glu_b, w_out):
    f32 = jnp.float32
    B, S, _ = h.shape
    H, DH = MLSTM_HEADS, MLSTM_DH
    proj = (h @ w_in).astype(f32)
    qk, v, o_pre, i_pre, f_pre, u = jnp.split(
        proj, [2 * MIX_A, 3 * MIX_A, 4 * MIX_A, 4 * MIX_A + H, 4 * MIX_A + 2 * H], axis=-1)
    qk = jax.nn.silu(_causal_conv(qk, conv_w.astype(f32)))
    q, k = jnp.split(qk, 2, axis=-1)
    hm = _mlstm_chunkwise(q.reshape(B, S, H, DH), k.reshape(B, S, H, DH), v.reshape(B, S, H, DH),
                          i_pre + b_i.astype(f32), f_pre + b_f.astype(f32))
    hm = hm * lax.rsqrt(jnp.mean(hm * hm, axis=-1, keepdims=True) + EPS)
    hm = jax.nn.sigmoid(o_pre) * (hm.reshape(B, S, MIX_A) * head_g.astype(f32))
    ys = jax.nn.gelu(_s5_ssm(u.reshape(B, S, S5_GROUPS, S5_GROUP), lam_re, lam_im, log_dt,
                             b_re, b_im, c_re, c_im, d_skip))
    ys = ys * jax.nn.sigmoid(jnp.einsum('bsgc,gce->bsge', ys, glu_w.astype(f32)) + glu_b.astype(f32))
    mix = jnp.concatenate([hm, ys.reshape(B, S, MIX_B)], axis=-1)
    return mix.astype(h.dtype) @ w_out


def _nsa_mixer(h, w_in, b_gate, cmp_pos, cmp_w1, cmp_b1, cmp_w2, cmp_b2, rel_bias, w_out):
    f32 = jnp.float32
    B, S, _ = h.shape
    R = NSA_HEADS // NSA_KV
    proj = (h @ w_in).astype(f32)
    q, kc, vc, ks, vs, kw, vw, g = jnp.split(proj, [D_MODEL + i * KV_W for i in range(7)], axis=-1)
    q = q.reshape(B, S, NSA_KV, R, NSA_DH).transpose(0, 2, 3, 1, 4) * NSA_DH ** -0.5
    heads = lambda t: t.reshape(B, S, NSA_KV, NSA_DH).transpose(0, 2, 1, 3)
    kc, vc, ks, vs, kw, vw = (heads(t) for t in (kc, vc, ks, vs, kw, vw))
    gates = jax.nn.sigmoid(g + b_gate.astype(f32)).reshape(B, S, NSA_KV, R, 3).transpose(0, 2, 3, 1, 4)

    n_cmp = (S - CMP_BLOCK) // CMP_STRIDE + 1
    cmp_start = jnp.arange(n_cmp) * CMP_STRIDE
    cmp_end = cmp_start + CMP_BLOCK - 1
    cmp_idx = cmp_start[:, None] + jnp.arange(CMP_BLOCK)[None]

    def compress(t, j):
        blk = t[:, :, cmp_idx] + cmp_pos[j].astype(f32)
        flat = blk.reshape(B, NSA_KV, n_cmp, CMP_BLOCK * NSA_DH)
        hid = jax.nn.gelu(flat @ cmp_w1[j].astype(f32) + cmp_b1[j].astype(f32))
        return hid @ cmp_w2[j].astype(f32) + cmp_b2[j].astype(f32)

    k_cmp = compress(kc, 0)
    v_cmp = compress(vc, 1)

    n_sel = S // SEL_BLOCK
    sel_start = jnp.arange(n_sel) * SEL_BLOCK
    overlap = jnp.clip(jnp.minimum(cmp_start[:, None] + CMP_BLOCK, sel_start[None] + SEL_BLOCK)
                       - jnp.maximum(cmp_start[:, None], sel_start[None]), 0).astype(f32) / CMP_BLOCK
    k_blocks = ks.reshape(B, NSA_KV, n_sel, SEL_BLOCK, NSA_DH)
    v_blocks = vs.reshape(B, NSA_KV, n_sel, SEL_BLOCK, NSA_DH)
    top_n = min(SEL_TOPK, n_sel)

    pad = ((0, 0), (0, 0), (WINDOW, 0), (0, 0))
    kw_pad = jnp.pad(kw, pad)
    vw_pad = jnp.pad(vw, pad)
    table = rel_bias.astype(f32).reshape(REL_BUCKETS, NSA_KV, R)
    b_idx = jnp.arange(B)[:, None, None, None]
    g_idx = jnp.arange(NSA_KV)[None, :, None, None]

    def block(jq):
        q0 = jq * NSA_QBLOCK
        t = q0 + jnp.arange(NSA_QBLOCK)
        qb = lax.dynamic_slice_in_dim(q, q0, NSA_QBLOCK, axis=3)
        gb = lax.dynamic_slice_in_dim(gates, q0, NSA_QBLOCK, axis=3)
        dist = t[:, None] - cmp_end[None]
        bias = table[_t5_bucket(dist)].transpose(2, 3, 0, 1)
        s = jnp.einsum('bgrtd,bgnd->bgrtn', qb, k_cmp) + bias
        p_cmp = _masked_softmax(s, dist >= 0, axis=-1)
        o_cmp = jnp.einsum('bgrtn,bgnd->bgrtd', p_cmp, v_cmp)
        imp = jnp.einsum('bgrtn,nj->bgtj', p_cmp, overlap)
        blk_t = (t // SEL_BLOCK)[:, None]
        jj = jnp.arange(n_sel)[None]
        forced = (jj == 0) | (jj == blk_t) | (jj == blk_t - 1)
        score = jnp.where(forced, FORCE, jnp.where(jj <= blk_t, imp, -1.0))
        _, sel = lax.top_k(score, top_n)
        ksel = k_blocks[b_idx, g_idx, sel]
        vsel = v_blocks[b_idx, g_idx, sel]
        kpos = sel[..., None] * SEL_BLOCK + jnp.arange(SEL_BLOCK)
        dist = t[:, None, None] - kpos
        bias = jnp.moveaxis(table[_t5_bucket(dist), g_idx[..., None]], -1, 2)
        s = jnp.einsum('bgrtd,bgtnsd->bgrtns', qb, ksel) + bias
        p = _masked_softmax(s, (dist >= 0)[:, :, None], axis=(-2, -1))
        o_sel = jnp.einsum('bgrtns,bgtnsd->bgrtd', p, vsel)
        kwin = lax.dynamic_slice_in_dim(kw_pad, q0, WINDOW + NSA_QBLOCK, axis=2)
        vwin = lax.dynamic_slice_in_dim(vw_pad, q0, WINDOW + NSA_QBLOCK, axis=2)
        kpos = q0 - WINDOW + jnp.arange(WINDOW + NSA_QBLOCK)
        dist = t[:, None] - kpos[None]
        mask = (dist >= 0) & (dist < WINDOW) & (kpos[None] >= 0)
        bias = table[_t5_bucket(dist)].transpose(2, 3, 0, 1)
        s = jnp.einsum('bgrtd,bgkd->bgrtk', qb, kwin) + bias
        p = _masked_softmax(s, mask, axis=-1)
        o_win = jnp.einsum('bgrtk,bgkd->bgrtd', p, vwin)
        return gb[..., 0:1] * o_cmp + gb[..., 1:2] * o_sel + gb[..., 2:3] * o_win

    out = lax.map(block, jnp.arange(S // NSA_QBLOCK))
    out = out.transpose(1, 0, 4, 2, 3, 5).reshape(B, S, D_MODEL)
    return out.astype(h.dtype) @ w_out


def _hier_moe(h, wg, bg, we, be, w_gate, w_up, w_down):
    f32 = jnp.float32
    B, S, D = h.shape
    hf = h.reshape(B * S, D)
    grp_logits = (hf @ wg).astype(f32) + bg.astype(f32)
    _, g_top = lax.top_k(grp_logits, 1)
    p_g = jnp.take_along_axis(jax.nn.softmax(grp_logits, axis=-1), g_top, axis=1)
    exp_logits = jnp.einsum('nd,dge->nge', hf, we).astype(f32) + be.astype(f32)
    in_grp = jnp.take_along_axis(exp_logits, g_top[:, :, None], axis=1)[:, 0]
    top_v, top_i = lax.top_k(in_grp, MOE_TOPK)
    w = jax.nn.softmax(top_v, axis=-1) * p_g
    expert_id = g_top * MOE_PER_GROUP + top_i
    gate = jnp.sum(jax.nn.one_hot(expert_id, MOE_EXPERTS, dtype=f32) * w[..., None], axis=1)
    hid = jax.nn.silu(jnp.einsum('nd,edf->nef', hf, w_gate)) * jnp.einsum('nd,edf->nef', hf, w_up)
    y = jnp.einsum('nef,efd->nd', hid * gate[:, :, None].astype(hid.dtype), w_down)
    return y.reshape(B, S, D)


def setup_inputs(seed: int = 0) -> dict:
    key = jax.random.key(seed)
    ks = iter(jax.random.split(key, 40))
    f32 = jnp.float32
    nrm = lambda shape, std: jax.random.normal(next(ks), shape, f32) * std
    D = D_MODEL
    G, P, C = S5_GROUPS, S5_STATE, S5_GROUP
    inp = {}
    inp['x'] = nrm((BATCH, SEQ, D), 1.0)
    inp['c'] = nrm((BATCH, D), 1.0)
    inp['rel_bias'] = nrm((REL_BUCKETS, NSA_HEADS), 0.1)
    inp['ada_w'] = nrm((DEPTH, 2, D, 3 * D), 0.5 * D ** -0.5)
    inp['ada_b'] = nrm((DEPTH, 2, 3 * D), 0.02)
    inp['norm_g'] = 1.0 + nrm((DEPTH, 2, D), 0.05)
    inp['final_g'] = 1.0 + nrm((D,), 0.05)
    inp['a_w_in'] = nrm((N_EVEN, D, IN_A), D ** -0.5)
    inp['a_conv'] = nrm((N_EVEN, CONV_K, 2 * MIX_A), CONV_K ** -0.5)
    inp['a_b_i'] = nrm((N_EVEN, MLSTM_HEADS), 0.1)
    inp['a_b_f'] = jnp.linspace(3.0, 6.0, MLSTM_HEADS, dtype=f32) + nrm((N_EVEN, MLSTM_HEADS), 0.1)
    inp['a_head_g'] = 1.0 + nrm((N_EVEN, MIX_A), 0.05)
    inp['s5_lam_re'] = -0.5 + nrm((N_EVEN, G, P), 0.01)
    inp['s5_lam_im'] = jnp.pi * jnp.arange(P, dtype=f32) + nrm((N_EVEN, G, P), 0.01)
    inp['s5_log_dt'] = jax.random.uniform(next(ks), (N_EVEN, G), f32, math.log(DT_MIN), math.log(DT_MAX))
    inp['s5_b_re'] = nrm((N_EVEN, G, P, C), (2 * C) ** -0.5)
    inp['s5_b_im'] = nrm((N_EVEN, G, P, C), (2 * C) ** -0.5)
    inp['s5_c_re'] = nrm((N_EVEN, G, C, P), P ** -0.5)
    inp['s5_c_im'] = nrm((N_EVEN, G, C, P), P ** -0.5)
    inp['s5_d'] = nrm((N_EVEN, G, C), 1.0)
    inp['s5_glu_w'] = nrm((N_EVEN, G, C, C), C ** -0.5)
    inp['s5_glu_b'] = nrm((N_EVEN, G, C), 0.02)
    inp['a_w_out'] = nrm((N_EVEN, D, D), D ** -0.5)
    inp['n_w_in'] = nrm((N_ODD, D, IN_C), D ** -0.5)
    inp['n_b_gate'] = nrm((N_ODD, 3 * NSA_HEADS), 0.1)
    inp['n_cmp_pos'] = nrm((N_ODD, 2, CMP_BLOCK, NSA_DH), 0.1)
    inp['n_cmp_w1'] = nrm((N_ODD, 2, CMP_BLOCK * NSA_DH, CMP_HIDDEN), (CMP_BLOCK * NSA_DH) ** -0.5)
    inp['n_cmp_b1'] = nrm((N_ODD, 2, CMP_HIDDEN), 0.02)
    inp['n_cmp_w2'] = nrm((N_ODD, 2, CMP_HIDDEN, NSA_DH), CMP_HIDDEN ** -0.5)
    inp['n_cmp_b2'] = nrm((N_ODD, 2, NSA_DH), 0.02)
    inp['n_w_out'] = nrm((N_ODD, D, D), D ** -0.5)
    inp['r_grp_w'] = nrm((DEPTH, D, MOE_GROUPS), D ** -0.5)
    inp['r_grp_b'] = nrm((DEPTH, MOE_GROUPS), 0.01)
    inp['r_exp_w'] = nrm((DEPTH, D, MOE_GROUPS, MOE_PER_GROUP), D ** -0.5)
    inp['r_exp_b'] = nrm((DEPTH, MOE_GROUPS, MOE_PER_GROUP), 0.01)
    inp['e_w_gate'] = nrm((DEPTH, MOE_EXPERTS, D, MOE_HIDDEN), D ** -0.5)
    inp['e_w_up'] = nrm((DEPTH, MOE_EXPERTS, D, MOE_HIDDEN), D ** -0.5)
    inp['e_w_down'] = nrm((DEPTH, MOE_EXPERTS, MOE_HIDDEN, D), MOE_HIDDEN ** -0.5)
    return inp


def reference(x, c, rel_bias, ada_w, ada_b, norm_g, final_g,
              a_w_in, a_conv, a_b_i, a_b_f, a_head_g,
              s5_lam_re, s5_lam_im, s5_log_dt, s5_b_re, s5_b_im, s5_c_re, s5_c_im,
              s5_d, s5_glu_w, s5_glu_b, a_w_out,
              n_w_in, n_b_gate, n_cmp_pos, n_cmp_w1, n_cmp_b1, n_cmp_w2, n_cmp_b2, n_w_out,
              r_grp_w, r_grp_b, r_exp_w, r_exp_b, e_w_gate, e_w_up, e_w_down):
    mod = jnp.einsum('bd,lkde->lkbe', jax.nn.silu(c), ada_w) + ada_b[:, :, None, :]
    for layer in range(DEPTH):
        shift, scale, gate = jnp.split(mod[layer, 0], 3, axis=-1)
        h = _modulate(x, norm_g[layer, 0], shift, scale)
        if layer % 2 == 0:
            j = layer // 2
            y = _mlstm_s5_mixer(h, a_w_in[j], a_conv[j], a_b_i[j], a_b_f[j], a_head_g[j],
                                s5_lam_re[j], s5_lam_im[j], s5_log_dt[j], s5_b_re[j], s5_b_im[j],
                                s5_c_re[j], s5_c_im[j], s5_d[j], s5_glu_w[j], s5_glu_b[j], a_w_out[j])
        else:
            j = layer // 2
            y = _nsa_mixer(h, n_w_in[j], n_b_gate[j], n_cmp_pos[j], n_cmp_w1[j], n_cmp_b1[j],
                           n_cmp_w2[j], n_cmp_b2[j], rel_bias, n_w_out[j])
        x = x + gate[:, None, :] * y
        shift, scale, gate = jnp.split(mod[layer, 1], 3, axis=-1)
        h = _modulate(x, norm_g[layer, 1], shift, scale)
        x = x + gate[:, None, :] * _hier_moe(h, r_grp_w[layer], r_grp_b[layer], r_exp_w[layer], r_exp_b[layer],
                                             e_w_gate[layer], e_w_up[layer], e_w_down[layer])
    return _rmsnorm(x, final_g)
```

```python
import functools
import math

import jax
import jax.numpy as jnp
from jax import lax
from jax.experimental import pallas as pl
from jax.experimental.pallas import tpu as pltpu

F32 = jnp.float32
BF16 = jnp.bfloat16
HIGHEST = lax.Precision.HIGHEST

D_MODEL = 1024
DEPTH = 2
MIX_A = 512
MLSTM_HEADS = 4
MLSTM_DH = MIX_A // MLSTM_HEADS
MLSTM_CHUNK = 128
CONV_K = 4
MIX_B = D_MODEL - MIX_A
S5_GROUP = 16
S5_GROUPS = MIX_B // S5_GROUP
S5_STATE = 64
S5_CHUNK = 16
NSA_HEADS = 16
NSA_KV = 4
NSA_R = NSA_HEADS // NSA_KV
NSA_DH = D_MODEL // NSA_HEADS
KV_W = NSA_KV * NSA_DH
CMP_BLOCK = 32
CMP_STRIDE = 16
CMP_HIDDEN = 256
SEL_BLOCK = 64
SEL_TOPK = 16
WINDOW = 512
FORCE = 1e9
REL_BUCKETS = 32
REL_MAX_DIST = 128
MOE_GROUPS = 4
MOE_PER_GROUP = 4
MOE_HIDDEN = 256
EPS = 1e-6
NEG = -1e30

LANES = 128
SUBLANES = 8
ATT_TILE = 128
VMEM_LIMIT = 56 * 1024 * 1024


def _cparams(sem):
    return pltpu.CompilerParams(dimension_semantics=sem, vmem_limit_bytes=VMEM_LIMIT)


def _dot(a, b, precision=None):
    return jnp.dot(a, b, preferred_element_type=F32, precision=precision)


def _dot_nt(a, b):
    return lax.dot_general(a, b, (((1,), (1,)), ((), ())), preferred_element_type=F32)


def _sigmoid(x):
    return 1.0 / (1.0 + jnp.exp(-x))


def _silu(x):
    return x * _sigmoid(x)


def _gelu_tanh(x):
    return 0.5 * x * (1.0 + jnp.tanh(math.sqrt(2.0 / math.pi) * (x + 0.044715 * (x * x * x))))


def _modulated_norm(x, g, shift, scale):
    y = x * lax.rsqrt(jnp.mean(x * x, axis=-1, keepdims=True) + EPS) * g
    return y * (1.0 + scale) + shift


def _ada_kernel(c_ref, w_ref, b_ref, o_ref):
    c = c_ref[...]
    o_ref[0] = _dot(_silu(c), w_ref[0]) + b_ref[0]


def _ada_mod(c, ada_w, ada_b):
    B, D = c.shape
    n_mod = ada_w.shape[0] * ada_w.shape[1]
    w = ada_w.reshape(n_mod, D, 3 * D)
    b = ada_b.reshape(n_mod, 1, 3 * D)
    tn = 1024
    return pl.pallas_call(
        _ada_kernel,
        out_shape=jax.ShapeDtypeStruct((n_mod, B, 3 * D), F32),
        grid=(n_mod, 3 * D // tn),
        in_specs=[pl.BlockSpec((B, D), lambda i, j: (0, 0)),
                  pl.BlockSpec((1, D, tn), lambda i, j: (i, 0, j)),
                  pl.BlockSpec((1, 1, tn), lambda i, j: (i, 0, j))],
        out_specs=pl.BlockSpec((1, B, tn), lambda i, j: (i, 0, j)),
        compiler_params=_cparams(("parallel", "parallel")),
        name="ada_mod",
    )(c, w, b)


def _norm_mm_kernel(*refs, n_w):
    x_ref, g_ref, sh_ref, sc_ref = refs[:4]
    w_refs = refs[4:4 + n_w]
    o_refs = refs[4 + n_w:]
    h = _modulated_norm(x_ref[0], g_ref[...], sh_ref[0], sc_ref[0]).astype(BF16)
    for w_ref, o_ref in zip(w_refs, o_refs):
        o_ref[0] = _dot(h, w_ref[...]).astype(o_ref.dtype)


def _norm_matmul(x, g, shift, scale, weights, out_dtypes, tm=512):
    B, S, D = x.shape
    n_w = len(weights)
    vec = pl.BlockSpec((1, 1, D), lambda b, i: (b, 0, 0))
    in_specs = [pl.BlockSpec((1, tm, D), lambda b, i: (b, i, 0)),
                pl.BlockSpec((1, D), lambda b, i: (0, 0)), vec, vec]
    in_specs += [pl.BlockSpec(w.shape, lambda b, i: (0, 0)) for w in weights]
    return pl.pallas_call(
        functools.partial(_norm_mm_kernel, n_w=n_w),
        out_shape=[jax.ShapeDtypeStruct((B, S, w.shape[1]), dt) for w, dt in zip(weights, out_dtypes)],
        grid=(B, S // tm),
        in_specs=in_specs,
        out_specs=[pl.BlockSpec((1, tm, w.shape[1]), lambda b, i: (b, i, 0)) for w in weights],
        compiler_params=_cparams(("parallel", "parallel")),
        name="norm_matmul",
    )(x, g.reshape(1, D), shift, scale, *weights)


def _out_res_kernel(*refs, n_in):
    x_ref, gate_ref = refs[:2]
    a_refs = refs[2:2 + n_in]
    w_refs = refs[2 + n_in:2 + 2 * n_in]
    o_ref = refs[2 + 2 * n_in]
    acc = None
    for a_ref, w_ref in zip(a_refs, w_refs):
        t = _dot(a_ref[0].astype(BF16), w_ref[...])
        acc = t if acc is None else acc + t
    o_ref[0] = x_ref[0] + gate_ref[0] * acc


def _out_residual(x, gate, acts, weights, tm=512):
    B, S, D = x.shape
    n_in = len(acts)
    in_specs = [pl.BlockSpec((1, tm, D), lambda b, i: (b, i, 0)),
                pl.BlockSpec((1, 1, D), lambda b, i: (b, 0, 0))]
    in_specs += [pl.BlockSpec((1, tm, a.shape[2]), lambda b, i: (b, i, 0)) for a in acts]
    in_specs += [pl.BlockSpec(w.shape, lambda b, i: (0, 0)) for w in weights]
    return pl.pallas_call(
        functools.partial(_out_res_kernel, n_in=n_in),
        out_shape=jax.ShapeDtypeStruct((B, S, D), F32),
        grid=(B, S // tm),
        in_specs=in_specs,
        out_specs=pl.BlockSpec((1, tm, D), lambda b, i: (b, i, 0)),
        compiler_params=_cparams(("parallel", "parallel")),
        name="out_residual",
    )(x, gate, *acts, *weights)


def _mlstm_kernel(qk_ref, vo_ref, if_ref, cw_ref, gb_ref, hg_ref, tril_ref, o_ref,
                  xbuf, c_scr, n_scr, m_scr):
    L, H, DH = MLSTM_CHUNK, MLSTM_HEADS, MLSTM_DH
    pad = SUBLANES

    @pl.when(pl.program_id(1) == 0)
    def _():
        xbuf[0:pad, :] = jnp.zeros((pad, 2 * MIX_A), F32)
        c_scr[...] = jnp.zeros_like(c_scr)
        n_scr[...] = jnp.zeros_like(n_scr)
        m_scr[...] = jnp.zeros_like(m_scr)

    xbuf[pad:pad + L, :] = qk_ref[0]
    cw = cw_ref[...]
    conv = None
    for j in range(CONV_K):
        lo = pad - (CONV_K - 1) + j
        t = xbuf[lo:lo + L, :] * cw[j:j + 1, :]
        conv = t if conv is None else conv + t
    xbuf[0:pad, :] = xbuf[L:L + pad, :]
    qk = _silu(conv)
    q = qk[:, :MIX_A]
    k = qk[:, MIX_A:] * (DH ** -0.5)
    vo = vo_ref[0]
    v = vo[:, :MIX_A]
    o_pre = vo[:, MIX_A:]

    ifb = if_ref[0] + gb_ref[...]
    lf = jnp.minimum(ifb, 0.0) - jnp.log1p(jnp.exp(-jnp.abs(ifb)))
    bcs = _dot(tril_ref[...], lf, precision=HIGHEST)
    ifb_t = ifb.T
    bcs_t = bcs.T
    row = lax.broadcasted_iota(jnp.int32, (L, L), 0)
    col = lax.broadcasted_iota(jnp.int32, (L, L), 1)
    causal = col <= row

    outs = []
    for h in range(H):
        sl = slice(h * DH, (h + 1) * DH)
        qh, kh, vh = q[:, sl], k[:, sl], v[:, sl]
        qb, kb = qh.astype(BF16), kh.astype(BF16)
        b_col = bcs[:, H + h:H + h + 1]
        b_row = bcs_t[H + h:H + h + 1, :]
        li_col = ifb[:, h:h + 1]
        li_row = ifb_t[h:h + 1, :]
        b_last = b_col[L - 1:L, :]
        m0 = m_scr[h][:, 0:1]
        c0 = c_scr[h]
        n0 = n_scr[h]

        log_d = jnp.where(causal, b_col - b_row + li_row, NEG)
        log_inter = b_col + m0
        m_t = jnp.maximum(log_inter, jnp.max(log_d, axis=1, keepdims=True))
        dmat = jnp.exp(log_d - m_t)
        a_inter = jnp.exp(log_inter - m_t)
        s = _dot_nt(qb, kb) * dmat
        num = _dot(s.astype(BF16), vh.astype(BF16)) + a_inter * _dot_nt(qb, c0.astype(BF16))
        den = jnp.sum(s, axis=1, keepdims=True) + a_inter * jnp.sum(qh * n0, axis=1, keepdims=True)
        hh = num / jnp.maximum(jnp.abs(den), jnp.exp(-m_t))

        w_col = b_last - b_col + li_col
        m_loc = jnp.max(w_col, axis=0, keepdims=True)
        e = jnp.exp(w_col - m_loc)
        c_loc = _dot((vh * e).T.astype(BF16), kb)
        n_loc = jnp.sum(kh * e, axis=0, keepdims=True)
        m_new = jnp.maximum(b_last + m0, m_loc)
        a = jnp.exp(b_last + m0 - m_new)
        sc = jnp.exp(m_loc - m_new)
        c_scr[h] = a * c0 + sc * c_loc
        n_scr[h] = a * n0 + sc * n_loc
        m_scr[h] = jnp.broadcast_to(m_new, (1, LANES))

        outs.append(hh * lax.rsqrt(jnp.mean(hh * hh, axis=1, keepdims=True) + EPS))
    hm = jnp.concatenate(outs, axis=1)
    o_ref[0] = _sigmoid(o_pre) * (hm * hg_ref[...])


def _mlstm(qk, vo, ifg, conv_w, gate_bias, head_g):
    B, S, _ = qk.shape
    L, H, DH = MLSTM_CHUNK, MLSTM_HEADS, MLSTM_DH
    tril = jnp.tril(jnp.ones((L, L), F32))
    return pl.pallas_call(
        _mlstm_kernel,
        out_shape=jax.ShapeDtypeStruct((B, S, MIX_A), F32),
        grid=(B, S // L),
        in_specs=[pl.BlockSpec((1, L, 2 * MIX_A), lambda b, c: (b, c, 0)),
                  pl.BlockSpec((1, L, 2 * MIX_A), lambda b, c: (b, c, 0)),
                  pl.BlockSpec((1, L, LANES), lambda b, c: (b, c, 0)),
                  pl.BlockSpec((CONV_K, 2 * MIX_A), lambda b, c: (0, 0)),
                  pl.BlockSpec((1, LANES), lambda b, c: (0, 0)),
                  pl.BlockSpec((1, MIX_A), lambda b, c: (0, 0)),
                  pl.BlockSpec((L, L), lambda b, c: (0, 0))],
        out_specs=pl.BlockSpec((1, L, MIX_A), lambda b, c: (b, c, 0)),
        scratch_shapes=[pltpu.VMEM((L + SUBLANES, 2 * MIX_A), F32),
                        pltpu.VMEM((H, DH, DH), F32),
                        pltpu.VMEM((H, 1, DH), F32),
                        pltpu.VMEM((H, 1, LANES), F32)],
        compiler_params=_cparams(("parallel", "arbitrary")),
        name="mlstm",
    )(qk, vo, ifg, conv_w, gate_bias, head_g, tril)


def _s5_kernel(u_ref, m_ref, h_ref, e_ref, a1_ref, a2_ref, d_ref, gw_ref, gb_ref, o_ref,
               xl_scr, x0_scr, *, n_chunks, batch):
    u = u_ref[0]
    xl_scr[...] = _dot(u, h_ref[0], precision=HIGHEST)
    a1 = a1_ref[0]
    a2 = a2_ref[0]
    half = S5_STATE

    def body(i, x):
        r = pl.multiple_of(i * batch, batch)
        x0_scr[pl.ds(r, batch), :] = x
        return a1 * x + a2 * pltpu.roll(x, half, 1) + xl_scr[pl.ds(r, batch), :]

    lax.fori_loop(0, n_chunks, body, jnp.zeros((batch, 2 * S5_STATE), F32), unroll=8)
    y = (_dot(u, m_ref[0], precision=HIGHEST) + _dot(x0_scr[...], e_ref[0], precision=HIGHEST)
         + u * d_ref[0])
    ys = _gelu_tanh(y)
    z = _dot(ys.astype(BF16), gw_ref[0]) + gb_ref[0]
    o_ref[0] = ys * _sigmoid(z)


def _s5_tables(lam_re, lam_im, log_dt, b_re, b_im, c_re, c_im, d_skip, glu_w, glu_b):
    T, C, P = S5_CHUNK, S5_GROUP, S5_STATE
    G = lam_re.shape[0]
    lam = lax.complex(lam_re.astype(F32), lam_im.astype(F32))
    dt = jnp.exp(log_dt.astype(F32))[:, None]
    lam_bar = jnp.exp(lam * dt)
    b_bar = ((lam_bar - 1.0) / lam)[..., None] * lax.complex(b_re.astype(F32), b_im.astype(F32))
    c_mat = lax.complex(c_re.astype(F32), c_im.astype(F32))
    taus = jnp.arange(T + 1, dtype=F32)
    pw = jnp.exp((lam * dt)[:, None, :] * taus[None, :, None])
    kern = jnp.einsum('gcp,gtp,gpd->gtcd', c_mat, pw[:, :T], b_bar,
                      precision=HIGHEST).real
    tt = jnp.arange(T)
    diff = tt[:, None] - tt[None, :]
    toe = jnp.where((diff >= 0)[None, :, :, None, None], kern[:, jnp.clip(diff, 0)], 0.0)
    m_t = toe.transpose(0, 2, 4, 1, 3).reshape(G, T * C, T * C)
    hmat = pw[:, :T][:, ::-1, :, None] * b_bar[:, None]
    h_t = jnp.concatenate([hmat.real, hmat.imag], axis=2).transpose(0, 1, 3, 2).reshape(G, T * C, 2 * P)
    emat = c_mat[:, None] * pw[:, 1:][:, :, None, :]
    e_t = jnp.concatenate([emat.real, -emat.imag], axis=3).reshape(G, T * C, 2 * P).transpose(0, 2, 1)
    a_re, a_im = pw[:, T].real, pw[:, T].imag
    a1 = jnp.broadcast_to(jnp.concatenate([a_re, a_re], axis=1)[:, None], (G, SUBLANES, 2 * P))
    a2 = jnp.broadcast_to(jnp.concatenate([-a_im, a_im], axis=1)[:, None], (G, SUBLANES, 2 * P))
    d_t = jnp.tile(d_skip.astype(F32), (1, T))[:, None]
    eye = jnp.eye(T, dtype=F32)
    gw = jnp.einsum('ts,gce->gtcse', eye, glu_w.astype(F32)).reshape(G, T * C, T * C).astype(BF16)
    gb = jnp.tile(glu_b.astype(F32), (1, T))[:, None]
    return m_t, h_t, e_t, a1, a2, d_t, gw, gb


def _s5(u, tables):
    B, S, _ = u.shape
    T, C, P, G = S5_CHUNK, S5_GROUP, S5_STATE, S5_GROUPS
    assert B == SUBLANES
    n_chunks = S // T
    rows = n_chunks * B
    ug = u.reshape(B, n_chunks, T, G, C).transpose(3, 1, 0, 2, 4).reshape(G, rows, T * C)
    m_t, h_t, e_t, a1, a2, d_t, gw, gb = tables
    per_g = lambda shape: pl.BlockSpec((1,) + shape, lambda g: (g, 0, 0))
    out = pl.pallas_call(
        functools.partial(_s5_kernel, n_chunks=n_chunks, batch=B),
        out_shape=jax.ShapeDtypeStruct((G, rows, T * C), F32),
        grid=(G,),
        in_specs=[per_g((rows, T * C)), per_g((T * C, T * C)), per_g((T * C, 2 * P)),
                  per_g((2 * P, T * C)), per_g((SUBLANES, 2 * P)), per_g((SUBLANES, 2 * P)),
                  per_g((1, T * C)), per_g((T * C, T * C)), per_g((1, T * C))],
        out_specs=per_g((rows, T * C)),
        scratch_shapes=[pltpu.VMEM((rows, 2 * P), F32), pltpu.VMEM((rows, 2 * P), F32)],
        compiler_params=_cparams(("parallel",)),
        name="s5",
    )(ug, m_t, h_t, e_t, a1, a2, d_t, gw, gb)
    return out.reshape(G, n_chunks, B, T, C).transpose(2, 1, 3, 0, 4).reshape(B, S, G * C)


def _compress_kernel(x_ref, plo_ref, phi_ref, w1_ref, b1_ref, w2_ref, b2_ref, o_ref):
    x = x_ref[0, 0]
    half = x.shape[1]
    w1 = w1_ref[0]
    lo = _dot((x + plo_ref[0]).astype(BF16), w1[:half])
    hi = _dot((x + phi_ref[0]).astype(BF16), w1[half:])
    rows = x.shape[0]
    hid = _gelu_tanh(lo + pltpu.roll(hi, rows - 1, 0) + b1_ref[0])
    o_ref[0, 0] = _dot(hid.astype(BF16), w2_ref[0]) + b2_ref[0]


def _compress(xg, pos, w1, b1, w2, b2):
    _, B, rows, width = xg.shape
    pos_flat = pos.reshape(2, 2, 1, width).astype(F32)
    sel = lambda shape: pl.BlockSpec((1,) + shape, lambda j, b: (j, 0, 0))
    return pl.pallas_call(
        _compress_kernel,
        out_shape=jax.ShapeDtypeStruct((2, B, rows, NSA_DH), F32),
        grid=(2, B),
        in_specs=[pl.BlockSpec((1, 1, rows, width), lambda j, b: (j, b, 0, 0)),
                  sel((1, width)), sel((1, width)),
                  sel((2 * width, CMP_HIDDEN)), sel((1, CMP_HIDDEN)),
                  sel((CMP_HIDDEN, NSA_DH)), sel((1, NSA_DH))],
        out_specs=pl.BlockSpec((1, 1, rows, NSA_DH), lambda j, b: (j, b, 0, 0)),
        compiler_params=_cparams(("parallel", "parallel")),
        name="nsa_compress",
    )(xg, pos_flat[:, 0], pos_flat[:, 1], w1.astype(BF16), b1[:, None].astype(F32),
      w2.astype(BF16), b2[:, None].astype(F32))


def _nsa_kernel(q_ref, gp_ref, bg_ref, kc_ref, vc_ref, ks_ref, vs_ref, kw_ref, vw_ref,
                grev_ref, selb_ref, winb_ref, ov_ref, ex_ref, o_ref, mask_scr):
    T = ATT_TILE
    R, DH = NSA_R, NSA_DH
    qi = pl.program_id(2)
    q0 = qi * T
    qall = q_ref[0] * (DH ** -0.5)
    t_col = q0 + lax.broadcasted_iota(jnp.int32, (T, 1), 0)
    n_cmp_pad = kc_ref.shape[2]
    n_sel = ov_ref.shape[1]
    grp_rows = CMP_STRIDE

    n_row = lax.broadcasted_iota(jnp.int32, (1, n_cmp_pad), 1)
    cmask = t_col >= n_row * CMP_STRIDE + (CMP_BLOCK - 1)
    kc = kc_ref[0, 0].astype(BF16)
    vc = vc_ref[0, 0].astype(BF16)
    qs = [qall[:, r * DH:(r + 1) * DH].astype(BF16) for r in range(R)]
    psum = None
    o_cmp = []
    for r in range(R):
        grev = grev_ref[r]
        bias = jnp.concatenate(
            [pltpu.roll(grev, (qi * (T // grp_rows) + al + 1) % n_cmp_pad, 1) for al in range(T // grp_rows)],
            axis=0)
        s = jnp.where(cmask, _dot_nt(qs[r], kc) + bias, NEG)
        p = jnp.exp(s - jnp.max(s, axis=1, keepdims=True))
        p = p / jnp.sum(p, axis=1, keepdims=True)
        p = jnp.where(cmask, p, 0.0)
        psum = p if psum is None else psum + p
        o_cmp.append(_dot(p.astype(BF16), vc))

    imp = _dot(psum, ov_ref[...], precision=HIGHEST)
    jj = lax.broadcasted_iota(jnp.int32, (T, n_sel), 1)
    blk_t = t_col // SEL_BLOCK
    forced = (jj == 0) | (jj == blk_t) | (jj == blk_t - 1)
    score = jnp.where(forced, FORCE, jnp.where(jj <= blk_t, imp, -1.0))
    cnt = jnp.zeros((T, n_sel), F32)
    for j2 in range(n_sel):
        c2 = score[:, j2:j2 + 1]
        beats = (c2 > score) | ((c2 == score) & (jj > j2))
        cnt = cnt + jnp.where(beats, 1.0, 0.0)
    selm = jnp.where(cnt < float(min(SEL_TOPK, n_sel)), 1.0, 0.0).astype(BF16)

    def mk_mask(kt, carry):
        mexp = _dot(selm, ex_ref[kt])
        kpos = kt * T + lax.broadcasted_iota(jnp.int32, (1, T), 1)
        mask_scr[kt] = jnp.where((mexp > 0.5) & (t_col >= kpos), 0.0, NEG)
        return carry

    lax.fori_loop(0, qi + 1, mk_mask, 0)

    gates = _sigmoid(gp_ref[0] + bg_ref[...])
    n_far = selb_ref.shape[1] - 1
    n_win = winb_ref.shape[1] - 1
    init = (jnp.full((T, 1), NEG, F32), jnp.zeros((T, 1), F32), jnp.zeros((T, DH), F32))
    outs = []
    for r in range(R):
        qr = qs[r]

        def flash(s, v, carry):
            m, l, acc = carry
            m_new = jnp.maximum(m, jnp.max(s, axis=1, keepdims=True))
            alpha = jnp.exp(m - m_new)
            p = jnp.exp(s - m_new)
            return (m_new, alpha * l + jnp.sum(p, axis=1, keepdims=True),
                    alpha * acc + _dot(p.astype(BF16), v))

        def sel_body(kt, carry):
            off = pl.multiple_of(kt * T, T)
            k = ks_ref[0, 0, pl.ds(off, T), :]
            v = vs_ref[0, 0, pl.ds(off, T), :]
            s = _dot_nt(qr, k) + selb_ref[r, jnp.minimum(qi - kt, n_far)] + mask_scr[kt]
            return flash(s, v, carry)

        _, l, acc = lax.fori_loop(0, qi + 1, sel_body, init)
        o_sel = acc / l

        def win_body(d, carry):
            off = pl.multiple_of((qi - d) * T, T)
            k = kw_ref[0, 0, pl.ds(off, T), :]
            v = vw_ref[0, 0, pl.ds(off, T), :]
            s = _dot_nt(qr, k) + winb_ref[r, d]
            return flash(s, v, carry)

        _, l, acc = lax.fori_loop(0, jnp.minimum(qi, n_win) + 1, win_body, init)
        o_win = acc / l

        c0 = r * 3
        outs.append(gates[:, c0:c0 + 1] * o_cmp[r] + gates[:, c0 + 1:c0 + 2] * o_sel
                    + gates[:, c0 + 2:c0 + 3] * o_win)
    o_ref[0] = jnp.concatenate(outs, axis=1)


def _t5_bucket(dist):
    dist = jnp.maximum(dist, 0)
    max_exact = REL_BUCKETS // 2
    log_ratio = jnp.log(jnp.maximum(dist, 1).astype(F32) / max_exact) / math.log(REL_MAX_DIST / max_exact)
    large = jnp.minimum(max_exact + (log_ratio * (REL_BUCKETS - max_exact)).astype(jnp.int32), REL_BUCKETS - 1)
    return jnp.where(dist < max_exact, dist, large)


def _nsa_tables(rel_bias, S):
    T = ATT_TILE
    table = rel_bias.astype(F32)
    ii = jnp.arange(T)
    delta = ii[:, None] - ii[None, :]

    def tile(off):
        return table[_t5_bucket(off * T + delta)].transpose(2, 0, 1)

    n_far = -(-REL_MAX_DIST // T) + 1
    selb = jnp.stack([tile(o) for o in range(n_far + 1)], axis=1)
    n_win = WINDOW // T
    winb = []
    for o in range(n_win + 1):
        dist = o * T + delta
        ok = (dist >= 0) & (dist < WINDOW)
        winb.append(tile(o) + jnp.where(ok, 0.0, NEG)[None])
    winb = jnp.stack(winb, axis=1)
    n_pad = S // CMP_STRIDE
    i16 = jnp.arange(CMP_STRIDE)
    dd = jnp.arange(n_pad)
    gdist = CMP_STRIDE * dd[None, :] + i16[:, None] - (CMP_BLOCK - 1)
    grev = table[_t5_bucket(gdist)].transpose(2, 0, 1)[:, :, ::-1]
    n_sel = S // SEL_BLOCK
    cmp_start = jnp.arange(n_pad) * CMP_STRIDE
    sel_start = jnp.arange(n_sel) * SEL_BLOCK
    overlap = jnp.clip(jnp.minimum(cmp_start[:, None] + CMP_BLOCK, sel_start[None] + SEL_BLOCK)
                       - jnp.maximum(cmp_start[:, None], sel_start[None]), 0).astype(F32) / CMP_BLOCK
    n_cmp = (S - CMP_BLOCK) // CMP_STRIDE + 1
    overlap = jnp.where((jnp.arange(n_pad) < n_cmp)[:, None], overlap, 0.0)
    kpos_blk = jnp.arange(S) // SEL_BLOCK
    expand = (jnp.arange(n_sel)[:, None] == kpos_blk[None, :]).astype(BF16)
    expand = expand.reshape(n_sel, S // T, T).transpose(1, 0, 2)
    return grev, selb, winb, overlap, expand


def _nsa_attention(q, gp, bg, kcmp, vcmp, ks, vs, kw, vw, tables):
    B, S, D = q.shape
    T = ATT_TILE
    grev, selb, winb, overlap, expand = tables
    gw = NSA_R * NSA_DH
    kv_full = pl.BlockSpec((1, 1, S, NSA_DH), lambda b, g, i: (b, g, 0, 0))
    cmp_full = pl.BlockSpec((1, 1, kcmp.shape[2], NSA_DH), lambda b, g, i: (b, g, 0, 0))
    per_head = lambda a: pl.BlockSpec((NSA_R,) + a.shape[1:], lambda b, g, i: (g,) + (0,) * (a.ndim - 1))
    full = lambda a: pl.BlockSpec(a.shape, lambda b, g, i: (0,) * a.ndim)
    return pl.pallas_call(
        _nsa_kernel,
        out_shape=jax.ShapeDtypeStruct((B, S, D), F32),
        grid=(B, NSA_KV, S // T),
        in_specs=[pl.BlockSpec((1, T, gw), lambda b, g, i: (b, i, g)),
                  pl.BlockSpec((1, T, LANES), lambda b, g, i: (b, i, g)),
                  pl.BlockSpec((1, LANES), lambda b, g, i: (0, g)),
                  cmp_full, cmp_full, kv_full, kv_full, kv_full, kv_full,
                  per_head(grev), per_head(selb), per_head(winb), full(overlap), full(expand)],
        out_specs=pl.BlockSpec((1, T, gw), lambda b, g, i: (b, i, g)),
        scratch_shapes=[pltpu.VMEM((S // T, T, T), F32)],
        compiler_params=_cparams(("parallel", "parallel", "arbitrary")),
        name="nsa_attention",
    )(q, gp, bg, kcmp, vcmp, ks, vs, kw, vw, grev, selb, winb, overlap, expand)


def _moe_kernel(x_ref, g_ref, sh_ref, sc_ref, gate_ref, wr_ref, br_ref, wg_ref, wu_ref, wd_ref, fg_ref,
                o_ref, hb_scr, rt_scr, acc_scr, *, final):
    NG, PG, FH = MOE_GROUPS, MOE_PER_GROUP, MOE_HIDDEN
    c = pl.program_id(2)

    @pl.when(c == 0)
    def _():
        h = _modulated_norm(x_ref[0], g_ref[...], sh_ref[0], sc_ref[0])
        hb_scr[...] = h.astype(BF16)
        logits = _dot(h, wr_ref[...], precision=HIGHEST) + br_ref[...]
        gl = [logits[:, NG * PG + g:NG * PG + g + 1] for g in range(NG)]
        gmax = functools.reduce(jnp.maximum, gl)
        gtop = jnp.full_like(gmax, float(NG - 1))
        for g in reversed(range(NG - 1)):
            gtop = jnp.where(gl[g] == gmax, float(g), gtop)
        p_g = 1.0 / functools.reduce(lambda a, b: a + b, [jnp.exp(v - gmax) for v in gl])
        a = []
        for j in range(PG):
            v = logits[:, (NG - 1) * PG + j:(NG - 1) * PG + j + 1]
            for g in reversed(range(NG - 1)):
                v = jnp.where(gtop == float(g), logits[:, g * PG + j:g * PG + j + 1], v)
            a.append(v)

        def first_max(vals):
            vmax = functools.reduce(jnp.maximum, vals)
            taken = jnp.zeros_like(vmax) > 1.0
            hits = []
            for v in vals:
                hit = (v == vmax) & jnp.logical_not(taken)
                taken = taken | hit
                hits.append(hit)
            return vmax, hits

        v1, hit1 = first_max(a)
        rest = [jnp.where(hh, -jnp.inf, v) for hh, v in zip(hit1, a)]
        v2, hit2 = first_max(rest)
        e2 = jnp.exp(v2 - v1)
        w1 = p_g / (1.0 + e2)
        w2 = p_g * e2 / (1.0 + e2)
        lane = lax.broadcasted_iota(jnp.int32, rt_scr.shape, 1)
        rt = jnp.where(lane == PG, gtop, 0.0)
        for j in range(PG):
            wj = jnp.where(hit1[j], w1, jnp.where(hit2[j], w2, 0.0))
            rt = jnp.where(lane == j, wj, rt)
        rt_scr[...] = rt

    hb = hb_scr[...]
    hid = _silu(_dot(hb, wg_ref[0])) * _dot(hb, wu_ref[0])
    rt = rt_scr[...]
    in_group = rt[:, PG:PG + 1] == c.astype(F32)
    parts = [hid[:, j * FH:(j + 1) * FH] * jnp.where(in_group, rt[:, j:j + 1], 0.0) for j in range(PG)]
    contrib = _dot(jnp.concatenate(parts, axis=1).astype(BF16), wd_ref[0])

    @pl.when(c == 0)
    def _():
        acc_scr[...] = contrib

    @pl.when(c > 0)
    def _():
        acc_scr[...] += contrib

    @pl.when(c == NG - 1)
    def _():
        y = x_ref[0] + gate_ref[0] * acc_scr[...]
        if final:
            y = y * lax.rsqrt(jnp.mean(y * y, axis=-1, keepdims=True) + EPS) * fg_ref[...]
        o_ref[0] = y


def _moe(x, g, shift, scale, gate, wg, bg, we, be, w_gate, w_up, w_down, final_g, final, tm=512):
    B, S, D = x.shape
    NG, PG, FH = MOE_GROUPS, MOE_PER_GROUP, MOE_HIDDEN
    wr = jnp.zeros((D, LANES), F32)
    wr = wr.at[:, :NG * PG].set(we.reshape(D, NG * PG).astype(F32)).at[:, NG * PG:NG * PG + NG].set(wg.astype(F32))
    br = jnp.zeros((1, LANES), F32)
    br = br.at[0, :NG * PG].set(be.reshape(NG * PG).astype(F32)).at[0, NG * PG:NG * PG + NG].set(bg.astype(F32))
    grp = lambda w: w.reshape(NG, PG, D, FH).transpose(0, 2, 1, 3).reshape(NG, D, PG * FH).astype(BF16)
    wd = w_down.reshape(NG, PG * FH, D).astype(BF16)
    vec = pl.BlockSpec((1, 1, D), lambda b, i, c: (b, 0, 0))
    row = pl.BlockSpec((1, D), lambda b, i, c: (0, 0))
    wspec = lambda k, n: pl.BlockSpec((1, k, n), lambda b, i, c: (c, 0, 0))
    return pl.pallas_call(
        functools.partial(_moe_kernel, final=final),
        out_shape=jax.ShapeDtypeStruct((B, S, D), F32),
        grid=(B, S // tm, NG),
        in_specs=[pl.BlockSpec((1, tm, D), lambda b, i, c: (b, i, 0)), row, vec, vec, vec,
                  pl.BlockSpec((D, LANES), lambda b, i, c: (0, 0)),
                  pl.BlockSpec((1, LANES), lambda b, i, c: (0, 0)),
                  wspec(D, PG * FH), wspec(D, PG * FH), wspec(PG * FH, D), row],
        out_specs=pl.BlockSpec((1, tm, D), lambda b, i, c: (b, i, 0)),
        scratch_shapes=[pltpu.VMEM((tm, D), BF16), pltpu.VMEM((tm, LANES), F32), pltpu.VMEM((tm, D), F32)],
        compiler_params=_cparams(("parallel", "parallel", "arbitrary")),
        name="moe",
    )(x, g.reshape(1, D), shift, scale, gate, wr, br, grp(w_gate), grp(w_up), wd, final_g.reshape(1, D))


def _mlstm_s5_layer(x, g, shift, scale, gate, w_in, conv_w, b_i, b_f, head_g, s5_params, w_out):
    H = MLSTM_HEADS
    A = MIX_A
    w_if = jnp.zeros((D_MODEL, LANES), F32).at[:, :2 * H].set(w_in[:, 4 * A:4 * A + 2 * H])
    weights = [w_in[:, :2 * A], w_in[:, 2 * A:4 * A], w_if, w_in[:, 4 * A + 2 * H:]]
    qk, vo, ifg, u = _norm_matmul(x, g, shift, scale, [w.astype(BF16) for w in weights], [F32] * 4)
    gate_bias = jnp.zeros((1, LANES), F32).at[0, :H].set(b_i.astype(F32)).at[0, H:2 * H].set(b_f.astype(F32))
    hm = _mlstm(qk, vo, ifg, conv_w.astype(F32), gate_bias, head_g.reshape(1, A).astype(F32))
    ys = _s5(u, _s5_tables(*s5_params))
    w_out = w_out.astype(BF16)
    return _out_residual(x, gate, [hm, ys], [w_out[:A], w_out[A:]])


def _nsa_layer(x, g, shift, scale, gate, w_in, b_gate, cmp_pos, cmp_w1, cmp_b1, cmp_w2, cmp_b2, rel_bias, w_out):
    B, S, D = x.shape
    KV, R, DH = NSA_KV, NSA_R, NSA_DH
    w_g = jnp.zeros((D, KV, LANES), F32).at[:, :, :3 * R].set(w_in[:, D + 6 * KV_W:].reshape(D, KV, 3 * R))
    b_g = jnp.zeros((KV, LANES), F32).at[:, :3 * R].set(b_gate.reshape(KV, 3 * R).astype(F32))
    weights = [w_in[:, :D], w_in[:, D:D + 6 * KV_W], w_g.reshape(D, KV * LANES)]
    q, kv, gp = _norm_matmul(x, g, shift, scale, [w.astype(BF16) for w in weights], [F32] * 3)
    kv_t = kv.reshape(B, S, 6, KV, DH).transpose(2, 0, 3, 1, 4)
    grp = CMP_STRIDE
    xg = kv_t[0:2].reshape(2, B, KV * S // grp, grp * DH)
    cmp = _compress(xg, cmp_pos, cmp_w1, cmp_b1, cmp_w2, cmp_b2).reshape(2, B, KV, S // grp, DH)
    kvb = kv_t[2:].astype(BF16)
    out = _nsa_attention(q, gp, b_g.reshape(1, KV * LANES), cmp[0], cmp[1], kvb[0], kvb[1], kvb[2], kvb[3],
                         _nsa_tables(rel_bias, S))
    return _out_residual(x, gate, [out], [w_out.astype(BF16)])


def kernel(x, c, rel_bias, ada_w, ada_b, norm_g, final_g,
           a_w_in, a_conv, a_b_i, a_b_f, a_head_g,
           s5_lam_re, s5_lam_im, s5_log_dt, s5_b_re, s5_b_im, s5_c_re, s5_c_im,
           s5_d, s5_glu_w, s5_glu_b, a_w_out,
           n_w_in, n_b_gate, n_cmp_pos, n_cmp_w1, n_cmp_b1, n_cmp_w2, n_cmp_b2, n_w_out,
           r_grp_w, r_grp_b, r_exp_w, r_exp_b, e_w_gate, e_w_up, e_w_down):
    B, S, D = x.shape
    mod = _ada_mod(c, ada_w, ada_b).reshape(DEPTH, 2, B, 1, 3 * D)
    split = lambda m: (m[..., :D], m[..., D:2 * D], m[..., 2 * D:])
    for layer in range(DEPTH):
        shift, scale, gate = split(mod[layer, 0])
        j = layer // 2
        if layer % 2 == 0:
            s5_params = (s5_lam_re[j], s5_lam_im[j], s5_log_dt[j], s5_b_re[j], s5_b_im[j],
                         s5_c_re[j], s5_c_im[j], s5_d[j], s5_glu_w[j], s5_glu_b[j])
            x = _mlstm_s5_layer(x, norm_g[layer, 0], shift, scale, gate, a_w_in[j], a_conv[j], a_b_i[j], a_b_f[j],
                                a_head_g[j], s5_params, a_w_out[j])
        else:
            x = _nsa_layer(x, norm_g[layer, 0], shift, scale, gate, n_w_in[j], n_b_gate[j], n_cmp_pos[j],
                           n_cmp_w1[j], n_cmp_b1[j], n_cmp_w2[j], n_cmp_b2[j], rel_bias, n_w_out[j])
        shift, scale, gate = split(mod[layer, 1])
        x = _moe(x, norm_g[layer, 1], shift, scale, gate, r_grp_w[layer], r_grp_b[layer], r_exp_w[layer],
                 r_exp_b[layer], e_w_gate[layer], e_w_up[layer], e_w_down[layer], final_g,
                 final=(layer == DEPTH - 1))
    return x
```

```python
import functools
import math

import jax
import jax.numpy as jnp
from jax import lax
from jax.experimental import pallas as pl
from jax.experimental.pallas import tpu as pltpu

F32 = jnp.float32
BF16 = jnp.bfloat16
HIGHEST = lax.Precision.HIGHEST

D_MODEL = 1024
DEPTH = 2
MIX_A = 512
MLSTM_HEADS = 4
MLSTM_DH = MIX_A // MLSTM_HEADS
MLSTM_CHUNK = 128
CONV_K = 4
MIX_B = D_MODEL - MIX_A
S5_GROUP = 16
S5_GROUPS = MIX_B // S5_GROUP
S5_STATE = 64
S5_CHUNK = 16
NSA_HEADS = 16
NSA_KV = 4
NSA_R = NSA_HEADS // NSA_KV
NSA_DH = D_MODEL // NSA_HEADS
KV_W = NSA_KV * NSA_DH
CMP_BLOCK = 32
CMP_STRIDE = 16
CMP_HIDDEN = 256
SEL_BLOCK = 64
SEL_TOPK = 16
WINDOW = 512
FORCE = 1e9
REL_BUCKETS = 32
REL_MAX_DIST = 128
MOE_GROUPS = 4
MOE_PER_GROUP = 4
MOE_HIDDEN = 256
EPS = 1e-6
NEG = -1e30
BIG = 1e30
LOG2E = math.log2(math.e)
SEL_CHUNK = 4

LANES = 128
SUBLANES = 8
ATT_TILE = 128
VMEM_LIMIT = 56 * 1024 * 1024


def _cparams(sem):
    return pltpu.CompilerParams(dimension_semantics=sem, vmem_limit_bytes=VMEM_LIMIT)


def _dot(a, b, precision=None):
    return jnp.dot(a, b, preferred_element_type=F32, precision=precision)


def _dot_nt(a, b):
    return lax.dot_general(a, b, (((1,), (1,)), ((), ())), preferred_element_type=F32)


def _sigmoid(x):
    return 1.0 / (1.0 + jnp.exp(-x))


def _silu(x):
    return x * _sigmoid(x)


def _gelu_tanh(x):
    return 0.5 * x * (1.0 + jnp.tanh(math.sqrt(2.0 / math.pi) * (x + 0.044715 * (x * x * x))))


def _modulated_norm(x, g, shift, scale):
    y = x * lax.rsqrt(jnp.mean(x * x, axis=-1, keepdims=True) + EPS) * g
    return y * (1.0 + scale) + shift


def _ada_kernel(c_ref, w_ref, b_ref, o_ref):
    c = c_ref[...]
    o_ref[0] = _dot(_silu(c), w_ref[0]) + b_ref[0]


def _ada_mod(c, ada_w, ada_b):
    B, D = c.shape
    n_mod = ada_w.shape[0] * ada_w.shape[1]
    w = ada_w.reshape(n_mod, D, 3 * D)
    b = ada_b.reshape(n_mod, 1, 3 * D)
    tn = 1024
    return pl.pallas_call(
        _ada_kernel,
        out_shape=jax.ShapeDtypeStruct((n_mod, B, 3 * D), F32),
        grid=(n_mod, 3 * D // tn),
        in_specs=[pl.BlockSpec((B, D), lambda i, j: (0, 0)),
                  pl.BlockSpec((1, D, tn), lambda i, j: (i, 0, j)),
                  pl.BlockSpec((1, 1, tn), lambda i, j: (i, 0, j))],
        out_specs=pl.BlockSpec((1, B, tn), lambda i, j: (i, 0, j)),
        compiler_params=_cparams(("parallel", "parallel")),
        name="ada_mod",
    )(c, w, b)


def _norm_mm_kernel(*refs, n_w):
    x_ref, g_ref, sh_ref, sc_ref = refs[:4]
    w_refs = refs[4:4 + n_w]
    o_refs = refs[4 + n_w:]
    h = _modulated_norm(x_ref[0], g_ref[...], sh_ref[0], sc_ref[0]).astype(BF16)
    for w_ref, o_ref in zip(w_refs, o_refs):
        o_ref[0] = _dot(h, w_ref[...]).astype(o_ref.dtype)


def _norm_matmul(x, g, shift, scale, weights, out_dtypes, tm=512):
    B, S, D = x.shape
    n_w = len(weights)
    vec = pl.BlockSpec((1, 1, D), lambda b, i: (b, 0, 0))
    in_specs = [pl.BlockSpec((1, tm, D), lambda b, i: (b, i, 0)),
                pl.BlockSpec((1, D), lambda b, i: (0, 0)), vec, vec]
    in_specs += [pl.BlockSpec(w.shape, lambda b, i: (0, 0)) for w in weights]
    return pl.pallas_call(
        functools.partial(_norm_mm_kernel, n_w=n_w),
        out_shape=[jax.ShapeDtypeStruct((B, S, w.shape[1]), dt) for w, dt in zip(weights, out_dtypes)],
        grid=(B, S // tm),
        in_specs=in_specs,
        out_specs=[pl.BlockSpec((1, tm, w.shape[1]), lambda b, i: (b, i, 0)) for w in weights],
        compiler_params=_cparams(("parallel", "parallel")),
        name="norm_matmul",
    )(x, g.reshape(1, D), shift, scale, *weights)


def _out_res_kernel(*refs, n_in):
    x_ref, gate_ref = refs[:2]
    a_refs = refs[2:2 + n_in]
    w_refs = refs[2 + n_in:2 + 2 * n_in]
    o_ref = refs[2 + 2 * n_in]
    acc = None
    for a_ref, w_ref in zip(a_refs, w_refs):
        t = _dot(a_ref[0].astype(BF16), w_ref[...])
        acc = t if acc is None else acc + t
    o_ref[0] = x_ref[0] + gate_ref[0] * acc


def _out_residual(x, gate, acts, weights, tm=512):
    B, S, D = x.shape
    n_in = len(acts)
    in_specs = [pl.BlockSpec((1, tm, D), lambda b, i: (b, i, 0)),
                pl.BlockSpec((1, 1, D), lambda b, i: (b, 0, 0))]
    in_specs += [pl.BlockSpec((1, tm, a.shape[2]), lambda b, i: (b, i, 0)) for a in acts]
    in_specs += [pl.BlockSpec(w.shape, lambda b, i: (0, 0)) for w in weights]
    return pl.pallas_call(
        functools.partial(_out_res_kernel, n_in=n_in),
        out_shape=jax.ShapeDtypeStruct((B, S, D), F32),
        grid=(B, S // tm),
        in_specs=in_specs,
        out_specs=pl.BlockSpec((1, tm, D), lambda b, i: (b, i, 0)),
        compiler_params=_cparams(("parallel", "parallel")),
        name="out_residual",
    )(x, gate, *acts, *weights)


def _mlstm_kernel(qk_ref, vo_ref, if_ref, cw_ref, gb_ref, hg_ref, tril_ref, o_ref,
                  xbuf, c_scr, n_scr, m_scr):
    L, H, DH = MLSTM_CHUNK, MLSTM_HEADS, MLSTM_DH
    pad = SUBLANES

    @pl.when(pl.program_id(1) == 0)
    def _():
        xbuf[0:pad, :] = jnp.zeros((pad, 2 * MIX_A), F32)
        c_scr[...] = jnp.zeros_like(c_scr)
        n_scr[...] = jnp.zeros_like(n_scr)
        m_scr[...] = jnp.zeros_like(m_scr)

    xbuf[pad:pad + L, :] = qk_ref[0]
    cw = cw_ref[...]
    conv = None
    for j in range(CONV_K):
        lo = pad - (CONV_K - 1) + j
        t = xbuf[lo:lo + L, :] * cw[j:j + 1, :]
        conv = t if conv is None else conv + t
    xbuf[0:pad, :] = xbuf[L:L + pad, :]
    qk = _silu(conv)
    q = qk[:, :MIX_A]
    k = qk[:, MIX_A:] * (DH ** -0.5)
    vo = vo_ref[0]
    v = vo[:, :MIX_A]
    o_pre = vo[:, MIX_A:]

    ifb = if_ref[0] + gb_ref[...]
    lf = jnp.minimum(ifb, 0.0) - jnp.log1p(jnp.exp(-jnp.abs(ifb)))
    bcs = _dot(tril_ref[...], lf, precision=HIGHEST)
    ifb_t = ifb.T
    bcs_t = bcs.T
    row = lax.broadcasted_iota(jnp.int32, (L, L), 0)
    col = lax.broadcasted_iota(jnp.int32, (L, L), 1)
    causal = col <= row

    outs = []
    for h in range(H):
        sl = slice(h * DH, (h + 1) * DH)
        qh, kh, vh = q[:, sl], k[:, sl], v[:, sl]
        qb, kb = qh.astype(BF16), kh.astype(BF16)
        b_col = bcs[:, H + h:H + h + 1]
        b_row = bcs_t[H + h:H + h + 1, :]
        li_col = ifb[:, h:h + 1]
        li_row = ifb_t[h:h + 1, :]
        b_last = b_col[L - 1:L, :]
        m0 = m_scr[h][:, 0:1]
        c0 = c_scr[h]
        n0 = n_scr[h]

        log_d = jnp.where(causal, b_col - b_row + li_row, NEG)
        log_inter = b_col + m0
        m_t = jnp.maximum(log_inter, jnp.max(log_d, axis=1, keepdims=True))
        dmat = jnp.exp(log_d - m_t)
        a_inter = jnp.exp(log_inter - m_t)
        s = _dot_nt(qb, kb) * dmat
        num = _dot(s.astype(BF16), vh.astype(BF16)) + a_inter * _dot_nt(qb, c0.astype(BF16))
        den = jnp.sum(s, axis=1, keepdims=True) + a_inter * jnp.sum(qh * n0, axis=1, keepdims=True)
        hh = num / jnp.maximum(jnp.abs(den), jnp.exp(-m_t))

        w_col = b_last - b_col + li_col
        m_loc = jnp.max(w_col, axis=0, keepdims=True)
        e = jnp.exp(w_col - m_loc)
        c_loc = _dot((vh * e).T.astype(BF16), kb)
        n_loc = jnp.sum(kh * e, axis=0, keepdims=True)
        m_new = jnp.maximum(b_last + m0, m_loc)
        a = jnp.exp(b_last + m0 - m_new)
        sc = jnp.exp(m_loc - m_new)
        c_scr[h] = a * c0 + sc * c_loc
        n_scr[h] = a * n0 + sc * n_loc
        m_scr[h] = jnp.broadcast_to(m_new, (1, LANES))

        outs.append(hh * lax.rsqrt(jnp.mean(hh * hh, axis=1, keepdims=True) + EPS))
    hm = jnp.concatenate(outs, axis=1)
    o_ref[0] = _sigmoid(o_pre) * (hm * hg_ref[...])


def _mlstm(qk, vo, ifg, conv_w, gate_bias, head_g):
    B, S, _ = qk.shape
    L, H, DH = MLSTM_CHUNK, MLSTM_HEADS, MLSTM_DH
    tril = jnp.tril(jnp.ones((L, L), F32))
    return pl.pallas_call(
        _mlstm_kernel,
        out_shape=jax.ShapeDtypeStruct((B, S, MIX_A), F32),
        grid=(B, S // L),
        in_specs=[pl.BlockSpec((1, L, 2 * MIX_A), lambda b, c: (b, c, 0)),
                  pl.BlockSpec((1, L, 2 * MIX_A), lambda b, c: (b, c, 0)),
                  pl.BlockSpec((1, L, LANES), lambda b, c: (b, c, 0)),
                  pl.BlockSpec((CONV_K, 2 * MIX_A), lambda b, c: (0, 0)),
                  pl.BlockSpec((1, LANES), lambda b, c: (0, 0)),
                  pl.BlockSpec((1, MIX_A), lambda b, c: (0, 0)),
                  pl.BlockSpec((L, L), lambda b, c: (0, 0))],
        out_specs=pl.BlockSpec((1, L, MIX_A), lambda b, c: (b, c, 0)),
        scratch_shapes=[pltpu.VMEM((L + SUBLANES, 2 * MIX_A), F32),
                        pltpu.VMEM((H, DH, DH), F32),
                        pltpu.VMEM((H, 1, DH), F32),
                        pltpu.VMEM((H, 1, LANES), F32)],
        compiler_params=_cparams(("parallel", "arbitrary")),
        name="mlstm",
    )(qk, vo, ifg, conv_w, gate_bias, head_g, tril)


def _s5_kernel(u_ref, m_ref, h_ref, e_ref, a1_ref, a2_ref, d_ref, gw_ref, gb_ref, o_ref,
               xl_scr, x0_scr, *, n_chunks, batch):
    u = u_ref[0]
    xl_scr[...] = _dot(u, h_ref[0], precision=HIGHEST)
    a1 = a1_ref[0]
    a2 = a2_ref[0]
    half = S5_STATE

    def body(i, x):
        r = pl.multiple_of(i * batch, batch)
        x0_scr[pl.ds(r, batch), :] = x
        return a1 * x + a2 * pltpu.roll(x, half, 1) + xl_scr[pl.ds(r, batch), :]

    lax.fori_loop(0, n_chunks, body, jnp.zeros((batch, 2 * S5_STATE), F32), unroll=8)
    y = (_dot(u, m_ref[0], precision=HIGHEST) + _dot(x0_scr[...], e_ref[0], precision=HIGHEST)
         + u * d_ref[0])
    ys = _gelu_tanh(y)
    z = _dot(ys.astype(BF16), gw_ref[0]) + gb_ref[0]
    o_ref[0] = ys * _sigmoid(z)


def _s5_tables(lam_re, lam_im, log_dt, b_re, b_im, c_re, c_im, d_skip, glu_w, glu_b):
    T, C, P = S5_CHUNK, S5_GROUP, S5_STATE
    G = lam_re.shape[0]
    lam = lax.complex(lam_re.astype(F32), lam_im.astype(F32))
    dt = jnp.exp(log_dt.astype(F32))[:, None]
    lam_bar = jnp.exp(lam * dt)
    b_bar = ((lam_bar - 1.0) / lam)[..., None] * lax.complex(b_re.astype(F32), b_im.astype(F32))
    c_mat = lax.complex(c_re.astype(F32), c_im.astype(F32))
    taus = jnp.arange(T + 1, dtype=F32)
    pw = jnp.exp((lam * dt)[:, None, :] * taus[None, :, None])
    kern = jnp.einsum('gcp,gtp,gpd->gtcd', c_mat, pw[:, :T], b_bar,
                      precision=HIGHEST).real
    tt = jnp.arange(T)
    diff = tt[:, None] - tt[None, :]
    toe = jnp.where((diff >= 0)[None, :, :, None, None], kern[:, jnp.clip(diff, 0)], 0.0)
    m_t = toe.transpose(0, 2, 4, 1, 3).reshape(G, T * C, T * C)
    hmat = pw[:, :T][:, ::-1, :, None] * b_bar[:, None]
    h_t = jnp.concatenate([hmat.real, hmat.imag], axis=2).transpose(0, 1, 3, 2).reshape(G, T * C, 2 * P)
    emat = c_mat[:, None] * pw[:, 1:][:, :, None, :]
    e_t = jnp.concatenate([emat.real, -emat.imag], axis=3).reshape(G, T * C, 2 * P).transpose(0, 2, 1)
    a_re, a_im = pw[:, T].real, pw[:, T].imag
    a1 = jnp.broadcast_to(jnp.concatenate([a_re, a_re], axis=1)[:, None], (G, SUBLANES, 2 * P))
    a2 = jnp.broadcast_to(jnp.concatenate([-a_im, a_im], axis=1)[:, None], (G, SUBLANES, 2 * P))
    d_t = jnp.tile(d_skip.astype(F32), (1, T))[:, None]
    eye = jnp.eye(T, dtype=F32)
    gw = jnp.einsum('ts,gce->gtcse', eye, glu_w.astype(F32)).reshape(G, T * C, T * C).astype(BF16)
    gb = jnp.tile(glu_b.astype(F32), (1, T))[:, None]
    return m_t, h_t, e_t, a1, a2, d_t, gw, gb


def _s5(u, tables):
    B, S, _ = u.shape
    T, C, P, G = S5_CHUNK, S5_GROUP, S5_STATE, S5_GROUPS
    assert B == SUBLANES
    n_chunks = S // T
    rows = n_chunks * B
    ug = u.reshape(B, n_chunks, T, G, C).transpose(3, 1, 0, 2, 4).reshape(G, rows, T * C)
    m_t, h_t, e_t, a1, a2, d_t, gw, gb = tables
    per_g = lambda shape: pl.BlockSpec((1,) + shape, lambda g: (g, 0, 0))
    out = pl.pallas_call(
        functools.partial(_s5_kernel, n_chunks=n_chunks, batch=B),
        out_shape=jax.ShapeDtypeStruct((G, rows, T * C), F32),
        grid=(G,),
        in_specs=[per_g((rows, T * C)), per_g((T * C, T * C)), per_g((T * C, 2 * P)),
                  per_g((2 * P, T * C)), per_g((SUBLANES, 2 * P)), per_g((SUBLANES, 2 * P)),
                  per_g((1, T * C)), per_g((T * C, T * C)), per_g((1, T * C))],
        out_specs=per_g((rows, T * C)),
        scratch_shapes=[pltpu.VMEM((rows, 2 * P), F32), pltpu.VMEM((rows, 2 * P), F32)],
        compiler_params=_cparams(("parallel",)),
        name="s5",
    )(ug, m_t, h_t, e_t, a1, a2, d_t, gw, gb)
    return out.reshape(G, n_chunks, B, T, C).transpose(2, 1, 3, 0, 4).reshape(B, S, G * C)


def _compress_kernel(x_ref, plo_ref, phi_ref, w1_ref, b1_ref, w2_ref, b2_ref, o_ref):
    x = x_ref[0, 0]
    half = x.shape[1]
    w1 = w1_ref[0]
    lo = _dot((x + plo_ref[0]).astype(BF16), w1[:half])
    hi = _dot((x + phi_ref[0]).astype(BF16), w1[half:])
    rows = x.shape[0]
    hid = _gelu_tanh(lo + pltpu.roll(hi, rows - 1, 0) + b1_ref[0])
    o_ref[0, 0] = _dot(hid.astype(BF16), w2_ref[0]) + b2_ref[0]


def _compress(xg, pos, w1, b1, w2, b2):
    _, B, rows, width = xg.shape
    pos_flat = pos.reshape(2, 2, 1, width).astype(F32)
    sel = lambda shape: pl.BlockSpec((1,) + shape, lambda j, b: (j, 0, 0))
    return pl.pallas_call(
        _compress_kernel,
        out_shape=jax.ShapeDtypeStruct((2, B, rows, NSA_DH), F32),
        grid=(2, B),
        in_specs=[pl.BlockSpec((1, 1, rows, width), lambda j, b: (j, b, 0, 0)),
                  sel((1, width)), sel((1, width)),
                  sel((2 * width, CMP_HIDDEN)), sel((1, CMP_HIDDEN)),
                  sel((CMP_HIDDEN, NSA_DH)), sel((1, NSA_DH))],
        out_specs=pl.BlockSpec((1, 1, rows, NSA_DH), lambda j, b: (j, b, 0, 0)),
        compiler_params=_cparams(("parallel", "parallel")),
        name="nsa_compress",
    )(xg, pos_flat[:, 0], pos_flat[:, 1], w1.astype(BF16), b1[:, None].astype(F32),
      w2.astype(BF16), b2[:, None].astype(F32))


def _nsa_kernel(q_ref, gp_ref, bg_ref, kc_ref, vc_ref, ks_ref, vs_ref, kw_ref, vw_ref,
                grev_ref, selb_ref, winb_ref, ovt_ref, ex_ref, o_ref):
    T = ATT_TILE
    R, DH = NSA_R, NSA_DH
    qi = pl.program_id(2)
    q0 = qi * T
    qall = q_ref[0] * (DH ** -0.5 * LOG2E)
    t_col = q0 + lax.broadcasted_iota(jnp.int32, (T, 1), 0)
    n_cmp_pad = kc_ref.shape[2]
    n_sel = ovt_ref.shape[0]
    grp_rows = CMP_STRIDE

    q4 = jnp.concatenate([qall[:, r * DH:(r + 1) * DH] for r in range(R)], axis=0).astype(BF16)

    n_row = lax.broadcasted_iota(jnp.int32, (1, n_cmp_pad), 1)
    cmask = (t_col >= n_row * CMP_STRIDE + (CMP_BLOCK - 1))[None]
    bias = jnp.stack([jnp.concatenate(
        [pltpu.roll(grev_ref[r], (qi * (T // grp_rows) + al + 1) % n_cmp_pad, 1) for al in range(T // grp_rows)],
        axis=0) for r in range(R)], axis=0)
    s = jnp.where(cmask, _dot_nt(q4, kc_ref[0, 0]).reshape(R, T, n_cmp_pad) + bias, NEG)
    p = jnp.exp2(s - jnp.max(s, axis=2, keepdims=True))
    p = p / jnp.sum(p, axis=2, keepdims=True)
    p = jnp.where(cmask, p, 0.0)
    o_cmp = _dot(p.reshape(R * T, n_cmp_pad).astype(BF16), vc_ref[0, 0])
    psum = functools.reduce(lambda a, b: a + b, [p[r] for r in range(R)])

    imp_t = lax.dot_general(ovt_ref[...], psum, (((1,), (1,)), ((), ())), precision=HIGHEST,
                            preferred_element_type=F32)
    jj = lax.broadcasted_iota(jnp.int32, (n_sel, T), 0)
    blk_t = (q0 + lax.broadcasted_iota(jnp.int32, (1, T), 1)) // SEL_BLOCK
    forced = (jj == 0) | (jj == blk_t) | (jj == blk_t - 1)
    score = jnp.where(forced, FORCE, jnp.where(jj <= blk_t, imp_t, -1.0))
    n_blk = n_sel // SUBLANES
    rows = [score[v * SUBLANES:(v + 1) * SUBLANES] for v in range(n_blk)]
    cnts = [jnp.zeros((SUBLANES, T), F32) for _ in range(n_blk)]
    sub = lax.broadcasted_iota(jnp.int32, (SUBLANES, T), 0)
    for j2 in range(n_sel):
        c2 = score[j2:j2 + 1, :]
        for v in range(n_blk):
            lo = v * SUBLANES
            if lo > j2:
                beats = c2 >= rows[v]
            elif lo + SUBLANES - 1 <= j2:
                beats = c2 > rows[v]
            else:
                beats = (c2 > rows[v]) | ((c2 >= rows[v]) & (sub > j2 - lo))
            cnts[v] = cnts[v] + jnp.where(beats, 1.0, 0.0)
    cnt = jnp.concatenate(cnts, axis=0)
    sel_t = jnp.where((cnt < float(min(SEL_TOPK, n_sel))) & (jj <= blk_t), 1.0, 0.0)
    sel_q = jnp.concatenate([sel_t, jnp.zeros((LANES - n_sel, T), F32)], axis=0).T
    lane = lax.broadcasted_iota(jnp.int32, (T, LANES), 1)
    sel_aug = jnp.where(lane == n_sel, 1.0, sel_q).astype(BF16)

    n_far = selb_ref.shape[0] - 1
    n_win = winb_ref.shape[0] - 2
    CH = ex_ref.shape[2] // T

    def sel_body(kc, carry):
        m, acc = carry
        off = pl.multiple_of(kc * (CH * T), CH * T)
        k = ks_ref[0, 0, pl.ds(off, CH * T), :]
        v = vs_ref[0, 0, pl.ds(off, CH * T), :]
        s = _dot_nt(q4, k)
        mask = _dot(sel_aug, ex_ref[kc])
        subs = []
        for j in range(CH):
            d = jnp.clip(qi - (kc * CH + j), 0, n_far)
            sj = s[:, j * T:(j + 1) * T].reshape(R, T, T) + selb_ref[d] + mask[:, j * T:(j + 1) * T][None]
            subs.append(sj.reshape(R * T, T))
        m_new = jnp.maximum(m, jnp.max(functools.reduce(jnp.maximum, subs), axis=1, keepdims=True))
        pb = jnp.concatenate([jnp.exp2(sj - m_new).astype(BF16) for sj in subs], axis=1)
        return m_new, jnp.exp2(m - m_new) * acc + _dot(pb, v)

    _, acc = lax.fori_loop(0, qi // CH + 1, sel_body,
                           (jnp.full((R * T, 1), NEG, F32), jnp.zeros((R * T, LANES), F32)))
    o_sel = acc[:, :DH] / acc[:, DH:DH + 1]

    subs, vals = [], []
    for d in range(n_win + 1):
        off = pl.multiple_of(jnp.maximum(qi - d, 0) * T, T)
        tile = jnp.where(qi >= d, d, n_win + 1)
        subs.append((_dot_nt(q4, kw_ref[0, 0, pl.ds(off, T), :]).reshape(R, T, T) + winb_ref[tile]).reshape(R * T, T))
        vals.append(vw_ref[0, 0, pl.ds(off, T), :])
    m_w = jnp.max(functools.reduce(jnp.maximum, subs), axis=1, keepdims=True)
    acc = functools.reduce(lambda a, b: a + b,
                           [_dot(jnp.exp2(sj - m_w).astype(BF16), vj) for sj, vj in zip(subs, vals)])
    o_win = acc[:, :DH] / acc[:, DH:DH + 1]

    gates = _sigmoid(gp_ref[0] + bg_ref[...])
    gcol = lambda j: jnp.concatenate([gates[:, 3 * r + j:3 * r + j + 1] for r in range(R)], axis=0)
    out4 = gcol(0) * o_cmp + gcol(1) * o_sel + gcol(2) * o_win
    o_ref[0] = jnp.concatenate([out4[r * T:(r + 1) * T] for r in range(R)], axis=1)


def _t5_bucket(dist):
    dist = jnp.maximum(dist, 0)
    max_exact = REL_BUCKETS // 2
    log_ratio = jnp.log(jnp.maximum(dist, 1).astype(F32) / max_exact) / math.log(REL_MAX_DIST / max_exact)
    large = jnp.minimum(max_exact + (log_ratio * (REL_BUCKETS - max_exact)).astype(jnp.int32), REL_BUCKETS - 1)
    return jnp.where(dist < max_exact, dist, large)


def _nsa_tables(rel_bias, S):
    T = ATT_TILE
    table = rel_bias.astype(F32) * LOG2E
    ii = jnp.arange(T)
    delta = ii[:, None] - ii[None, :]

    def tile(off):
        return table[_t5_bucket(off * T + delta)].transpose(2, 0, 1)

    n_far = -(-REL_MAX_DIST // T) + 1
    selb = [tile(o) for o in range(n_far + 1)]
    selb[0] = selb[0] + jnp.where(delta >= 0, 0.0, NEG)[None]
    selb = jnp.stack(selb, axis=0)
    n_win = WINDOW // T
    winb = []
    for o in range(n_win + 1):
        dist = o * T + delta
        ok = (dist >= 0) & (dist < WINDOW)
        winb.append(tile(o) + jnp.where(ok, 0.0, NEG)[None])
    winb.append(jnp.full_like(winb[0], NEG))
    winb = jnp.stack(winb, axis=0)
    n_pad = S // CMP_STRIDE
    i16 = jnp.arange(CMP_STRIDE)
    dd = jnp.arange(n_pad)
    gdist = CMP_STRIDE * dd[None, :] + i16[:, None] - (CMP_BLOCK - 1)
    grev = table[_t5_bucket(gdist)].transpose(2, 0, 1)[:, :, ::-1]
    n_sel = S // SEL_BLOCK
    cmp_start = jnp.arange(n_pad) * CMP_STRIDE
    sel_start = jnp.arange(n_sel) * SEL_BLOCK
    overlap = jnp.clip(jnp.minimum(cmp_start[:, None] + CMP_BLOCK, sel_start[None] + SEL_BLOCK)
                       - jnp.maximum(cmp_start[:, None], sel_start[None]), 0).astype(F32) / CMP_BLOCK
    n_cmp = (S - CMP_BLOCK) // CMP_STRIDE + 1
    overlap_t = jnp.where((jnp.arange(n_pad) < n_cmp)[:, None], overlap, 0.0).T
    tk = min(SEL_CHUNK * T, S)
    kpos_blk = jnp.arange(S) // SEL_BLOCK
    rows = jnp.arange(LANES)[:, None]
    expand = jnp.where(rows == kpos_blk[None, :], BIG, jnp.where(rows == n_sel, -BIG, 0.0)).astype(BF16)
    expand = expand.reshape(LANES, S // tk, tk).transpose(1, 0, 2)
    return grev, selb, winb, overlap_t, expand


def _nsa_attention(q, gp, bg, kcmp, vcmp, ks, vs, kw, vw, tables):
    B, S, D = q.shape
    T = ATT_TILE
    grev, selb, winb, overlap_t, expand = tables
    gw = NSA_R * NSA_DH
    seq = lambda a: pl.BlockSpec((1, 1) + a.shape[2:], lambda b, g, i: (b, g, 0, 0))
    per_head = lambda a: pl.BlockSpec((NSA_R,) + a.shape[1:], lambda b, g, i: (g,) + (0,) * (a.ndim - 1))
    tiles = lambda a: pl.BlockSpec((a.shape[0], NSA_R) + a.shape[2:], lambda b, g, i: (0, g, 0, 0))
    full = lambda a: pl.BlockSpec(a.shape, lambda b, g, i: (0,) * a.ndim)
    return pl.pallas_call(
        _nsa_kernel,
        out_shape=jax.ShapeDtypeStruct((B, S, D), F32),
        grid=(B, NSA_KV, S // T),
        in_specs=[pl.BlockSpec((1, T, gw), lambda b, g, i: (b, i, g)),
                  pl.BlockSpec((1, T, LANES), lambda b, g, i: (b, i, g)),
                  pl.BlockSpec((1, LANES), lambda b, g, i: (0, g)),
                  seq(kcmp), seq(vcmp), seq(ks), seq(vs), seq(kw), seq(vw),
                  per_head(grev), tiles(selb), tiles(winb), full(overlap_t), full(expand)],
        out_specs=pl.BlockSpec((1, T, gw), lambda b, g, i: (b, i, g)),
        compiler_params=_cparams(("parallel", "parallel", "arbitrary")),
        name="nsa_attention",
    )(q, gp, bg, kcmp, vcmp, ks, vs, kw, vw, grev, selb, winb, overlap_t, expand)


def _moe_kernel(x_ref, g_ref, sh_ref, sc_ref, gate_ref, wr_ref, br_ref, wg_ref, wu_ref, wd_ref, fg_ref,
                o_ref, hb_scr, rt_scr, acc_scr, *, final):
    NG, PG, FH = MOE_GROUPS, MOE_PER_GROUP, MOE_HIDDEN
    c = pl.program_id(2)

    @pl.when(c == 0)
    def _():
        h = _modulated_norm(x_ref[0], g_ref[...], sh_ref[0], sc_ref[0])
        hb_scr[...] = h.astype(BF16)
        logits = _dot(h, wr_ref[...], precision=HIGHEST) + br_ref[...]
        gl = [logits[:, NG * PG + g:NG * PG + g + 1] for g in range(NG)]
        gmax = functools.reduce(jnp.maximum, gl)
        gtop = jnp.full_like(gmax, float(NG - 1))
        for g in reversed(range(NG - 1)):
            gtop = jnp.where(gl[g] == gmax, float(g), gtop)
        p_g = 1.0 / functools.reduce(lambda a, b: a + b, [jnp.exp(v - gmax) for v in gl])
        a = []
        for j in range(PG):
            v = logits[:, (NG - 1) * PG + j:(NG - 1) * PG + j + 1]
            for g in reversed(range(NG - 1)):
                v = jnp.where(gtop == float(g), logits[:, g * PG + j:g * PG + j + 1], v)
            a.append(v)

        def first_max(vals):
            vmax = functools.reduce(jnp.maximum, vals)
            taken = jnp.zeros_like(vmax) > 1.0
            hits = []
            for v in vals:
                hit = (v == vmax) & jnp.logical_not(taken)
                taken = taken | hit
                hits.append(hit)
            return vmax, hits

        v1, hit1 = first_max(a)
        rest = [jnp.where(hh, -jnp.inf, v) for hh, v in zip(hit1, a)]
        v2, hit2 = first_max(rest)
        e2 = jnp.exp(v2 - v1)
        w1 = p_g / (1.0 + e2)
        w2 = p_g * e2 / (1.0 + e2)
        lane = lax.broadcasted_iota(jnp.int32, rt_scr.shape, 1)
        rt = jnp.where(lane == PG, gtop, 0.0)
        for j in range(PG):
            wj = jnp.where(hit1[j], w1, jnp.where(hit2[j], w2, 0.0))
            rt = jnp.where(lane == j, wj, rt)
        rt_scr[...] = rt

    hb = hb_scr[...]
    hid = _silu(_dot(hb, wg_ref[0])) * _dot(hb, wu_ref[0])
    rt = rt_scr[...]
    in_group = rt[:, PG:PG + 1] == c.astype(F32)
    parts = [hid[:, j * FH:(j + 1) * FH] * jnp.where(in_group, rt[:, j:j + 1], 0.0) for j in range(PG)]
    contrib = _dot(jnp.concatenate(parts, axis=1).astype(BF16), wd_ref[0])

    @pl.when(c == 0)
    def _():
        acc_scr[...] = contrib

    @pl.when(c > 0)
    def _():
        acc_scr[...] += contrib

    @pl.when(c == NG - 1)
    def _():
        y = x_ref[0] + gate_ref[0] * acc_scr[...]
        if final:
            y = y * lax.rsqrt(jnp.mean(y * y, axis=-1, keepdims=True) + EPS) * fg_ref[...]
        o_ref[0] = y


def _moe(x, g, shift, scale, gate, wg, bg, we, be, w_gate, w_up, w_down, final_g, final, tm=512):
    B, S, D = x.shape
    NG, PG, FH = MOE_GROUPS, MOE_PER_GROUP, MOE_HIDDEN
    wr = jnp.zeros((D, LANES), F32)
    wr = wr.at[:, :NG * PG].set(we.reshape(D, NG * PG).astype(F32)).at[:, NG * PG:NG * PG + NG].set(wg.astype(F32))
    br = jnp.zeros((1, LANES), F32)
    br = br.at[0, :NG * PG].set(be.reshape(NG * PG).astype(F32)).at[0, NG * PG:NG * PG + NG].set(bg.astype(F32))
    grp = lambda w: w.reshape(NG, PG, D, FH).transpose(0, 2, 1, 3).reshape(NG, D, PG * FH).astype(BF16)
    wd = w_down.reshape(NG, PG * FH, D).astype(BF16)
    vec = pl.BlockSpec((1, 1, D), lambda b, i, c: (b, 0, 0))
    row = pl.BlockSpec((1, D), lambda b, i, c: (0, 0))
    wspec = lambda k, n: pl.BlockSpec((1, k, n), lambda b, i, c: (c, 0, 0))
    return pl.pallas_call(
        functools.partial(_moe_kernel, final=final),
        out_shape=jax.ShapeDtypeStruct((B, S, D), F32),
        grid=(B, S // tm, NG),
        in_specs=[pl.BlockSpec((1, tm, D), lambda b, i, c: (b, i, 0)), row, vec, vec, vec,
                  pl.BlockSpec((D, LANES), lambda b, i, c: (0, 0)),
                  pl.BlockSpec((1, LANES), lambda b, i, c: (0, 0)),
                  wspec(D, PG * FH), wspec(D, PG * FH), wspec(PG * FH, D), row],
        out_specs=pl.BlockSpec((1, tm, D), lambda b, i, c: (b, i, 0)),
        scratch_shapes=[pltpu.VMEM((tm, D), BF16), pltpu.VMEM((tm, LANES), F32), pltpu.VMEM((tm, D), F32)],
        compiler_params=_cparams(("parallel", "parallel", "arbitrary")),
        name="moe",
    )(x, g.reshape(1, D), shift, scale, gate, wr, br, grp(w_gate), grp(w_up), wd, final_g.reshape(1, D))


def _mlstm_s5_layer(x, g, shift, scale, gate, w_in, conv_w, b_i, b_f, head_g, s5_params, w_out):
    H = MLSTM_HEADS
    A = MIX_A
    w_if = jnp.zeros((D_MODEL, LANES), F32).at[:, :2 * H].set(w_in[:, 4 * A:4 * A + 2 * H])
    weights = [w_in[:, :2 * A], w_in[:, 2 * A:4 * A], w_if, w_in[:, 4 * A + 2 * H:]]
    qk, vo, ifg, u = _norm_matmul(x, g, shift, scale, [w.astype(BF16) for w in weights], [F32] * 4)
    gate_bias = jnp.zeros((1, LANES), F32).at[0, :H].set(b_i.astype(F32)).at[0, H:2 * H].set(b_f.astype(F32))
    hm = _mlstm(qk, vo, ifg, conv_w.astype(F32), gate_bias, head_g.reshape(1, A).astype(F32))
    ys = _s5(u, _s5_tables(*s5_params))
    w_out = w_out.astype(BF16)
    return _out_residual(x, gate, [hm, ys], [w_out[:A], w_out[A:]])


def _nsa_layer(x, g, shift, scale, gate, w_in, b_gate, cmp_pos, cmp_w1, cmp_b1, cmp_w2, cmp_b2, rel_bias, w_out):
    B, S, D = x.shape
    KV, R, DH = NSA_KV, NSA_R, NSA_DH
    w_g = jnp.zeros((D, KV, LANES), F32).at[:, :, :3 * R].set(w_in[:, D + 6 * KV_W:].reshape(D, KV, 3 * R))
    b_g = jnp.zeros((KV, LANES), F32).at[:, :3 * R].set(b_gate.reshape(KV, 3 * R).astype(F32))
    weights = [w_in[:, :D], w_in[:, D:D + 6 * KV_W], w_g.reshape(D, KV * LANES)]
    q, kv, gp = _norm_matmul(x, g, shift, scale, [w.astype(BF16) for w in weights], [F32] * 3)
    kv_t = kv.reshape(B, S, 6, KV, DH).transpose(2, 0, 3, 1, 4)
    grp = CMP_STRIDE
    xg = kv_t[0:2].reshape(2, B, KV * S // grp, grp * DH)
    cmp = _compress(xg, cmp_pos, cmp_w1, cmp_b1, cmp_w2, cmp_b2).reshape(2, B, KV, S // grp, DH)
    kvb = kv_t[2:].astype(BF16)
    cmpb = cmp.astype(BF16)
    ones = jnp.ones((B, KV, S, LANES - DH), BF16)
    vs_ext = jnp.concatenate([kvb[1], ones], axis=-1)
    vw_ext = jnp.concatenate([kvb[3], ones], axis=-1)
    out = _nsa_attention(q, gp, b_g.reshape(1, KV * LANES), cmpb[0], cmpb[1], kvb[0], vs_ext, kvb[2], vw_ext,
                         _nsa_tables(rel_bias, S))
    return _out_residual(x, gate, [out], [w_out.astype(BF16)])


def kernel(x, c, rel_bias, ada_w, ada_b, norm_g, final_g,
           a_w_in, a_conv, a_b_i, a_b_f, a_head_g,
           s5_lam_re, s5_lam_im, s5_log_dt, s5_b_re, s5_b_im, s5_c_re, s5_c_im,
           s5_d, s5_glu_w, s5_glu_b, a_w_out,
           n_w_in, n_b_gate, n_cmp_pos, n_cmp_w1, n_cmp_b1, n_cmp_w2, n_cmp_b2, n_w_out,
           r_grp_w, r_grp_b, r_exp_w, r_exp_b, e_w_gate, e_w_up, e_w_down):
    B, S, D = x.shape
    mod = _ada_mod(c, ada_w, ada_b).reshape(DEPTH, 2, B, 1, 3 * D)
    split = lambda m: (m[..., :D], m[..., D:2 * D], m[..., 2 * D:])
    for layer in range(DEPTH):
        shift, scale, gate = split(mod[layer, 0])
        j = layer // 2
        if layer % 2 == 0:
            s5_params = (s5_lam_re[j], s5_lam_im[j], s5_log_dt[j], s5_b_re[j], s5_b_im[j],
                         s5_c_re[j], s5_c_im[j], s5_d[j], s5_glu_w[j], s5_glu_b[j])
            x = _mlstm_s5_layer(x, norm_g[layer, 0], shift, scale, gate, a_w_in[j], a_conv[j], a_b_i[j], a_b_f[j],
                                a_head_g[j], s5_params, a_w_out[j])
        else:
            x = _nsa_layer(x, norm_g[layer, 0], shift, scale, gate, n_w_in[j], n_b_gate[j], n_cmp_pos[j],
                           n_cmp_w1[j], n_cmp_b1[j], n_cmp_w2[j], n_cmp_b2[j], rel_bias, n_w_out[j])
        shift, scale, gate = split(mod[layer, 1])
        x = _moe(x, norm_g[layer, 1], shift, scale, gate, r_grp_w[layer], r_grp_b[layer], r_exp_w[layer],
                 r_exp_b[layer], e_w_gate[layer], e_w_up[layer], e_w_down[layer], final_g,
                 final=(layer == DEPTH - 1))
    return x
```

```python
import functools
import math

import jax
import jax.numpy as jnp
from jax import lax
from jax.experimental import pallas as pl
from jax.experimental.pallas import tpu as pltpu

F32 = jnp.float32
BF16 = jnp.bfloat16
HIGHEST = lax.Precision.HIGHEST

D_MODEL = 1024
DEPTH = 2
MIX_A = 512
MLSTM_HEADS = 4
MLSTM_DH = MIX_A // MLSTM_HEADS
MLSTM_CHUNK = 128
CONV_K = 4
MIX_B = D_MODEL - MIX_A
S5_GROUP = 16
S5_GROUPS = MIX_B // S5_GROUP
S5_STATE = 64
S5_CHUNK = 16
NSA_HEADS = 16
NSA_KV = 4
NSA_R = NSA_HEADS // NSA_KV
NSA_DH = D_MODEL // NSA_HEADS
KV_W = NSA_KV * NSA_DH
CMP_BLOCK = 32
CMP_STRIDE = 16
CMP_HIDDEN = 256
SEL_BLOCK = 64
SEL_TOPK = 16
WINDOW = 512
FORCE = 1e9
REL_BUCKETS = 32
REL_MAX_DIST = 128
MOE_GROUPS = 4
MOE_PER_GROUP = 4
MOE_HIDDEN = 256
EPS = 1e-6
NEG = -1e30
BIG = 1e30
LOG2E = math.log2(math.e)
SEL_CHUNK = 2
SEL_UNROLL = 2

LANES = 128
SUBLANES = 8
ATT_TILE = 128
VMEM_LIMIT = 56 * 1024 * 1024


def _cparams(sem):
    return pltpu.CompilerParams(dimension_semantics=sem, vmem_limit_bytes=VMEM_LIMIT)


def _dot(a, b, precision=None):
    return jnp.dot(a, b, preferred_element_type=F32, precision=precision)


def _dot_nt(a, b):
    return lax.dot_general(a, b, (((1,), (1,)), ((), ())), preferred_element_type=F32)


def _sigmoid(x):
    return 1.0 / (1.0 + jnp.exp(-x))


def _silu(x):
    return x * _sigmoid(x)


def _gelu_tanh(x):
    return 0.5 * x * (1.0 + jnp.tanh(math.sqrt(2.0 / math.pi) * (x + 0.044715 * (x * x * x))))


def _modulated_norm(x, g, shift, scale):
    y = x * lax.rsqrt(jnp.mean(x * x, axis=-1, keepdims=True) + EPS) * g
    return y * (1.0 + scale) + shift


def _ada_kernel(c_ref, w_ref, b_ref, o_ref):
    c = c_ref[...]
    o_ref[0] = _dot(_silu(c), w_ref[0]) + b_ref[0]


def _ada_mod(c, ada_w, ada_b):
    B, D = c.shape
    n_mod = ada_w.shape[0] * ada_w.shape[1]
    w = ada_w.reshape(n_mod, D, 3 * D)
    b = ada_b.reshape(n_mod, 1, 3 * D)
    tn = 1024
    return pl.pallas_call(
        _ada_kernel,
        out_shape=jax.ShapeDtypeStruct((n_mod, B, 3 * D), F32),
        grid=(n_mod, 3 * D // tn),
        in_specs=[pl.BlockSpec((B, D), lambda i, j: (0, 0)),
                  pl.BlockSpec((1, D, tn), lambda i, j: (i, 0, j)),
                  pl.BlockSpec((1, 1, tn), lambda i, j: (i, 0, j))],
        out_specs=pl.BlockSpec((1, B, tn), lambda i, j: (i, 0, j)),
        compiler_params=_cparams(("parallel", "parallel")),
        name="ada_mod",
    )(c, w, b)


def _norm_mm_kernel(*refs, n_w):
    x_ref, g_ref, sh_ref, sc_ref = refs[:4]
    w_refs = refs[4:4 + n_w]
    o_refs = refs[4 + n_w:]
    h = _modulated_norm(x_ref[0], g_ref[...], sh_ref[0], sc_ref[0]).astype(BF16)
    for w_ref, o_ref in zip(w_refs, o_refs):
        o_ref[0] = _dot(h, w_ref[...]).astype(o_ref.dtype)


def _norm_matmul(x, g, shift, scale, weights, out_dtypes, tm=512):
    B, S, D = x.shape
    n_w = len(weights)
    vec = pl.BlockSpec((1, 1, D), lambda b, i: (b, 0, 0))
    in_specs = [pl.BlockSpec((1, tm, D), lambda b, i: (b, i, 0)),
                pl.BlockSpec((1, D), lambda b, i: (0, 0)), vec, vec]
    in_specs += [pl.BlockSpec(w.shape, lambda b, i: (0, 0)) for w in weights]
    return pl.pallas_call(
        functools.partial(_norm_mm_kernel, n_w=n_w),
        out_shape=[jax.ShapeDtypeStruct((B, S, w.shape[1]), dt) for w, dt in zip(weights, out_dtypes)],
        grid=(B, S // tm),
        in_specs=in_specs,
        out_specs=[pl.BlockSpec((1, tm, w.shape[1]), lambda b, i: (b, i, 0)) for w in weights],
        compiler_params=_cparams(("parallel", "parallel")),
        name="norm_matmul",
    )(x, g.reshape(1, D), shift, scale, *weights)


def _out_res_kernel(*refs, n_in):
    x_ref, gate_ref = refs[:2]
    a_refs = refs[2:2 + n_in]
    w_refs = refs[2 + n_in:2 + 2 * n_in]
    o_ref = refs[2 + 2 * n_in]
    acc = None
    for a_ref, w_ref in zip(a_refs, w_refs):
        t = _dot(a_ref[0].astype(BF16), w_ref[...])
        acc = t if acc is None else acc + t
    o_ref[0] = x_ref[0] + gate_ref[0] * acc


def _out_residual(x, gate, acts, weights, tm=512):
    B, S, D = x.shape
    n_in = len(acts)
    in_specs = [pl.BlockSpec((1, tm, D), lambda b, i: (b, i, 0)),
                pl.BlockSpec((1, 1, D), lambda b, i: (b, 0, 0))]
    in_specs += [pl.BlockSpec((1, tm, a.shape[2]), lambda b, i: (b, i, 0)) for a in acts]
    in_specs += [pl.BlockSpec(w.shape, lambda b, i: (0, 0)) for w in weights]
    return pl.pallas_call(
        functools.partial(_out_res_kernel, n_in=n_in),
        out_shape=jax.ShapeDtypeStruct((B, S, D), F32),
        grid=(B, S // tm),
        in_specs=in_specs,
        out_specs=pl.BlockSpec((1, tm, D), lambda b, i: (b, i, 0)),
        compiler_params=_cparams(("parallel", "parallel")),
        name="out_residual",
    )(x, gate, *acts, *weights)


def _mlstm_kernel(qk_ref, vo_ref, if_ref, cw_ref, gb_ref, hg_ref, tril_ref, o_ref,
                  xbuf, c_scr, n_scr, m_scr):
    L, H, DH = MLSTM_CHUNK, MLSTM_HEADS, MLSTM_DH
    pad = SUBLANES

    @pl.when(pl.program_id(1) == 0)
    def _():
        xbuf[0:pad, :] = jnp.zeros((pad, 2 * MIX_A), F32)
        c_scr[...] = jnp.zeros_like(c_scr)
        n_scr[...] = jnp.zeros_like(n_scr)
        m_scr[...] = jnp.zeros_like(m_scr)

    xbuf[pad:pad + L, :] = qk_ref[0]
    cw = cw_ref[...]
    conv = None
    for j in range(CONV_K):
        lo = pad - (CONV_K - 1) + j
        t = xbuf[lo:lo + L, :] * cw[j:j + 1, :]
        conv = t if conv is None else conv + t
    xbuf[0:pad, :] = xbuf[L:L + pad, :]
    qk = _silu(conv)
    q = qk[:, :MIX_A]
    k = qk[:, MIX_A:] * (DH ** -0.5)
    vo = vo_ref[0]
    v = vo[:, :MIX_A]
    o_pre = vo[:, MIX_A:]

    ifb = if_ref[0] + gb_ref[...]
    lf = jnp.minimum(ifb, 0.0) - jnp.log1p(jnp.exp(-jnp.abs(ifb)))
    bcs = _dot(tril_ref[...], lf, precision=HIGHEST)
    ifb_t = ifb.T
    bcs_t = bcs.T
    row = lax.broadcasted_iota(jnp.int32, (L, L), 0)
    col = lax.broadcasted_iota(jnp.int32, (L, L), 1)
    causal = col <= row

    outs = []
    for h in range(H):
        sl = slice(h * DH, (h + 1) * DH)
        qh, kh, vh = q[:, sl], k[:, sl], v[:, sl]
        qb, kb = qh.astype(BF16), kh.astype(BF16)
        b_col = bcs[:, H + h:H + h + 1]
        b_row = bcs_t[H + h:H + h + 1, :]
        li_col = ifb[:, h:h + 1]
        li_row = ifb_t[h:h + 1, :]
        b_last = b_col[L - 1:L, :]
        m0 = m_scr[h][:, 0:1]
        c0 = c_scr[h]
        n0 = n_scr[h]

        log_d = jnp.where(causal, b_col - b_row + li_row, NEG)
        log_inter = b_col + m0
        m_t = jnp.maximum(log_inter, jnp.max(log_d, axis=1, keepdims=True))
        dmat = jnp.exp(log_d - m_t)
        a_inter = jnp.exp(log_inter - m_t)
        s = _dot_nt(qb, kb) * dmat
        num = _dot(s.astype(BF16), vh.astype(BF16)) + a_inter * _dot_nt(qb, c0.astype(BF16))
        den = jnp.sum(s, axis=1, keepdims=True) + a_inter * jnp.sum(qh * n0, axis=1, keepdims=True)
        hh = num / jnp.maximum(jnp.abs(den), jnp.exp(-m_t))

        w_col = b_last - b_col + li_col
        m_loc = jnp.max(w_col, axis=0, keepdims=True)
        e = jnp.exp(w_col - m_loc)
        c_loc = _dot((vh * e).T.astype(BF16), kb)
        n_loc = jnp.sum(kh * e, axis=0, keepdims=True)
        m_new = jnp.maximum(b_last + m0, m_loc)
        a = jnp.exp(b_last + m0 - m_new)
        sc = jnp.exp(m_loc - m_new)
        c_scr[h] = a * c0 + sc * c_loc
        n_scr[h] = a * n0 + sc * n_loc
        m_scr[h] = jnp.broadcast_to(m_new, (1, LANES))

        outs.append(hh * lax.rsqrt(jnp.mean(hh * hh, axis=1, keepdims=True) + EPS))
    hm = jnp.concatenate(outs, axis=1)
    o_ref[0] = _sigmoid(o_pre) * (hm * hg_ref[...])


def _mlstm(qk, vo, ifg, conv_w, gate_bias, head_g):
    B, S, _ = qk.shape
    L, H, DH = MLSTM_CHUNK, MLSTM_HEADS, MLSTM_DH
    tril = jnp.tril(jnp.ones((L, L), F32))
    return pl.pallas_call(
        _mlstm_kernel,
        out_shape=jax.ShapeDtypeStruct((B, S, MIX_A), F32),
        grid=(B, S // L),
        in_specs=[pl.BlockSpec((1, L, 2 * MIX_A), lambda b, c: (b, c, 0)),
                  pl.BlockSpec((1, L, 2 * MIX_A), lambda b, c: (b, c, 0)),
                  pl.BlockSpec((1, L, LANES), lambda b, c: (b, c, 0)),
                  pl.BlockSpec((CONV_K, 2 * MIX_A), lambda b, c: (0, 0)),
                  pl.BlockSpec((1, LANES), lambda b, c: (0, 0)),
                  pl.BlockSpec((1, MIX_A), lambda b, c: (0, 0)),
                  pl.BlockSpec((L, L), lambda b, c: (0, 0))],
        out_specs=pl.BlockSpec((1, L, MIX_A), lambda b, c: (b, c, 0)),
        scratch_shapes=[pltpu.VMEM((L + SUBLANES, 2 * MIX_A), F32),
                        pltpu.VMEM((H, DH, DH), F32),
                        pltpu.VMEM((H, 1, DH), F32),
                        pltpu.VMEM((H, 1, LANES), F32)],
        compiler_params=_cparams(("parallel", "arbitrary")),
        name="mlstm",
    )(qk, vo, ifg, conv_w, gate_bias, head_g, tril)


def _s5_kernel(u_ref, m_ref, h_ref, e_ref, a1_ref, a2_ref, d_ref, gw_ref, gb_ref, o_ref,
               xl_scr, x0_scr, *, n_chunks, batch):
    u = u_ref[0]
    xl_scr[...] = _dot(u, h_ref[0], precision=HIGHEST)
    a1 = a1_ref[0]
    a2 = a2_ref[0]
    half = S5_STATE

    def body(i, x):
        r = pl.multiple_of(i * batch, batch)
        x0_scr[pl.ds(r, batch), :] = x
        return a1 * x + a2 * pltpu.roll(x, half, 1) + xl_scr[pl.ds(r, batch), :]

    lax.fori_loop(0, n_chunks, body, jnp.zeros((batch, 2 * S5_STATE), F32), unroll=8)
    y = (_dot(u, m_ref[0], precision=HIGHEST) + _dot(x0_scr[...], e_ref[0], precision=HIGHEST)
         + u * d_ref[0])
    ys = _gelu_tanh(y)
    z = _dot(ys.astype(BF16), gw_ref[0]) + gb_ref[0]
    o_ref[0] = ys * _sigmoid(z)


def _s5_tables(lam_re, lam_im, log_dt, b_re, b_im, c_re, c_im, d_skip, glu_w, glu_b):
    T, C, P = S5_CHUNK, S5_GROUP, S5_STATE
    G = lam_re.shape[0]
    lam = lax.complex(lam_re.astype(F32), lam_im.astype(F32))
    dt = jnp.exp(log_dt.astype(F32))[:, None]
    lam_bar = jnp.exp(lam * dt)
    b_bar = ((lam_bar - 1.0) / lam)[..., None] * lax.complex(b_re.astype(F32), b_im.astype(F32))
    c_mat = lax.complex(c_re.astype(F32), c_im.astype(F32))
    taus = jnp.arange(T + 1, dtype=F32)
    pw = jnp.exp((lam * dt)[:, None, :] * taus[None, :, None])
    kern = jnp.einsum('gcp,gtp,gpd->gtcd', c_mat, pw[:, :T], b_bar,
                      precision=HIGHEST).real
    tt = jnp.arange(T)
    diff = tt[:, None] - tt[None, :]
    toe = jnp.where((diff >= 0)[None, :, :, None, None], kern[:, jnp.clip(diff, 0)], 0.0)
    m_t = toe.transpose(0, 2, 4, 1, 3).reshape(G, T * C, T * C)
    hmat = pw[:, :T][:, ::-1, :, None] * b_bar[:, None]
    h_t = jnp.concatenate([hmat.real, hmat.imag], axis=2).transpose(0, 1, 3, 2).reshape(G, T * C, 2 * P)
    emat = c_mat[:, None] * pw[:, 1:][:, :, None, :]
    e_t = jnp.concatenate([emat.real, -emat.imag], axis=3).reshape(G, T * C, 2 * P).transpose(0, 2, 1)
    a_re, a_im = pw[:, T].real, pw[:, T].imag
    a1 = jnp.broadcast_to(jnp.concatenate([a_re, a_re], axis=1)[:, None], (G, SUBLANES, 2 * P))
    a2 = jnp.broadcast_to(jnp.concatenate([-a_im, a_im], axis=1)[:, None], (G, SUBLANES, 2 * P))
    d_t = jnp.tile(d_skip.astype(F32), (1, T))[:, None]
    eye = jnp.eye(T, dtype=F32)
    gw = jnp.einsum('ts,gce->gtcse', eye, glu_w.astype(F32)).reshape(G, T * C, T * C).astype(BF16)
    gb = jnp.tile(glu_b.astype(F32), (1, T))[:, None]
    return m_t, h_t, e_t, a1, a2, d_t, gw, gb


def _s5(u, tables):
    B, S, _ = u.shape
    T, C, P, G = S5_CHUNK, S5_GROUP, S5_STATE, S5_GROUPS
    assert B == SUBLANES
    n_chunks = S // T
    rows = n_chunks * B
    ug = u.reshape(B, n_chunks, T, G, C).transpose(3, 1, 0, 2, 4).reshape(G, rows, T * C)
    m_t, h_t, e_t, a1, a2, d_t, gw, gb = tables
    per_g = lambda shape: pl.BlockSpec((1,) + shape, lambda g: (g, 0, 0))
    out = pl.pallas_call(
        functools.partial(_s5_kernel, n_chunks=n_chunks, batch=B),
        out_shape=jax.ShapeDtypeStruct((G, rows, T * C), F32),
        grid=(G,),
        in_specs=[per_g((rows, T * C)), per_g((T * C, T * C)), per_g((T * C, 2 * P)),
                  per_g((2 * P, T * C)), per_g((SUBLANES, 2 * P)), per_g((SUBLANES, 2 * P)),
                  per_g((1, T * C)), per_g((T * C, T * C)), per_g((1, T * C))],
        out_specs=per_g((rows, T * C)),
        scratch_shapes=[pltpu.VMEM((rows, 2 * P), F32), pltpu.VMEM((rows, 2 * P), F32)],
        compiler_params=_cparams(("parallel",)),
        name="s5",
    )(ug, m_t, h_t, e_t, a1, a2, d_t, gw, gb)
    return out.reshape(G, n_chunks, B, T, C).transpose(2, 1, 3, 0, 4).reshape(B, S, G * C)


def _compress_kernel(x_ref, plo_ref, phi_ref, w1_ref, b1_ref, w2_ref, b2_ref, o_ref):
    x = x_ref[0, 0]
    half = x.shape[1]
    w1 = w1_ref[0]
    lo = _dot((x + plo_ref[0]).astype(BF16), w1[:half])
    hi = _dot((x + phi_ref[0]).astype(BF16), w1[half:])
    rows = x.shape[0]
    hid = _gelu_tanh(lo + pltpu.roll(hi, rows - 1, 0) + b1_ref[0])
    o_ref[0, 0] = _dot(hid.astype(BF16), w2_ref[0]) + b2_ref[0]


def _compress(xg, pos, w1, b1, w2, b2):
    _, B, rows, width = xg.shape
    pos_flat = pos.reshape(2, 2, 1, width).astype(F32)
    sel = lambda shape: pl.BlockSpec((1,) + shape, lambda j, b: (j, 0, 0))
    return pl.pallas_call(
        _compress_kernel,
        out_shape=jax.ShapeDtypeStruct((2, B, rows, NSA_DH), F32),
        grid=(2, B),
        in_specs=[pl.BlockSpec((1, 1, rows, width), lambda j, b: (j, b, 0, 0)),
                  sel((1, width)), sel((1, width)),
                  sel((2 * width, CMP_HIDDEN)), sel((1, CMP_HIDDEN)),
                  sel((CMP_HIDDEN, NSA_DH)), sel((1, NSA_DH))],
        out_specs=pl.BlockSpec((1, 1, rows, NSA_DH), lambda j, b: (j, b, 0, 0)),
        compiler_params=_cparams(("parallel", "parallel")),
        name="nsa_compress",
    )(xg, pos_flat[:, 0], pos_flat[:, 1], w1.astype(BF16), b1[:, None].astype(F32),
      w2.astype(BF16), b2[:, None].astype(F32))


def _nsa_kernel(q_ref, gp_ref, bg_ref, kc_ref, vc_ref, ks_ref, vs_ref, kw_ref, vw_ref,
                grev_ref, selb_ref, winb_ref, ovt_ref, ex_ref, o_ref):
    T = ATT_TILE
    R, DH = NSA_R, NSA_DH
    qi = pl.program_id(2)
    q0 = qi * T
    qall = q_ref[0] * (DH ** -0.5 * LOG2E)
    t_col = q0 + lax.broadcasted_iota(jnp.int32, (T, 1), 0)
    n_cmp_pad = kc_ref.shape[2]
    n_sel = ovt_ref.shape[0]
    grp_rows = CMP_STRIDE

    q4 = jnp.concatenate([qall[:, r * DH:(r + 1) * DH] for r in range(R)], axis=0).astype(BF16)

    n_row = lax.broadcasted_iota(jnp.int32, (1, n_cmp_pad), 1)
    cmask = (t_col >= n_row * CMP_STRIDE + (CMP_BLOCK - 1))[None]
    bias = jnp.stack([jnp.concatenate(
        [pltpu.roll(grev_ref[r], (qi * (T // grp_rows) + al + 1) % n_cmp_pad, 1) for al in range(T // grp_rows)],
        axis=0) for r in range(R)], axis=0)
    s = jnp.where(cmask, _dot_nt(q4, kc_ref[0, 0]).reshape(R, T, n_cmp_pad) + bias, NEG)
    p = jnp.exp2(s - jnp.max(s, axis=2, keepdims=True))
    p = p / jnp.sum(p, axis=2, keepdims=True)
    p = jnp.where(cmask, p, 0.0)
    o_cmp = _dot(p.reshape(R * T, n_cmp_pad).astype(BF16), vc_ref[0, 0])
    psum = functools.reduce(lambda a, b: a + b, [p[r] for r in range(R)])

    imp_t = lax.dot_general(ovt_ref[...], psum, (((1,), (1,)), ((), ())), precision=HIGHEST,
                            preferred_element_type=F32)
    jj = lax.broadcasted_iota(jnp.int32, (n_sel, T), 0)
    blk_t = (q0 + lax.broadcasted_iota(jnp.int32, (1, T), 1)) // SEL_BLOCK
    forced = (jj == 0) | (jj == blk_t) | (jj == blk_t - 1)
    score = jnp.where(forced, FORCE, jnp.where(jj <= blk_t, imp_t, -1.0))
    n_blk = n_sel // SUBLANES
    rows = [score[v * SUBLANES:(v + 1) * SUBLANES] for v in range(n_blk)]
    cnts = [jnp.zeros((SUBLANES, T), F32) for _ in range(n_blk)]
    sub = lax.broadcasted_iota(jnp.int32, (SUBLANES, T), 0)
    for j2 in range(n_sel):
        c2 = score[j2:j2 + 1, :]
        for v in range(n_blk):
            lo = v * SUBLANES
            if lo > j2:
                beats = c2 >= rows[v]
            elif lo + SUBLANES - 1 <= j2:
                beats = c2 > rows[v]
            else:
                beats = (c2 > rows[v]) | ((c2 >= rows[v]) & (sub > j2 - lo))
            cnts[v] = cnts[v] + jnp.where(beats, 1.0, 0.0)
    cnt = jnp.concatenate(cnts, axis=0)
    sel_t = jnp.where((cnt < float(min(SEL_TOPK, n_sel))) & (jj <= blk_t), 1.0, 0.0)
    sel_q = jnp.concatenate([sel_t, jnp.zeros((LANES - n_sel, T), F32)], axis=0).T
    lane = lax.broadcasted_iota(jnp.int32, (T, LANES), 1)
    sel_aug = jnp.where(lane == n_sel, 1.0, sel_q).astype(BF16)

    n_far = selb_ref.shape[0] - 1
    n_win = winb_ref.shape[0] - 2
    CH = ex_ref.shape[2] // T

    def sel_body(kc, carry):
        m, acc = carry
        off = pl.multiple_of(kc * (CH * T), CH * T)
        k = ks_ref[0, 0, pl.ds(off, CH * T), :]
        v = vs_ref[0, 0, pl.ds(off, CH * T), :]
        s = _dot_nt(q4, k)
        mask = _dot(sel_aug, ex_ref[kc])
        subs = []
        for j in range(CH):
            d = jnp.clip(qi - (kc * CH + j), 0, n_far)
            sj = s[:, j * T:(j + 1) * T].reshape(R, T, T) + selb_ref[d] + mask[:, j * T:(j + 1) * T][None]
            subs.append(sj.reshape(R * T, T))
        m_new = jnp.maximum(m, jnp.max(functools.reduce(jnp.maximum, subs), axis=1, keepdims=True))
        pb = jnp.concatenate([jnp.exp2(sj - m_new).astype(BF16) for sj in subs], axis=1)
        return m_new, jnp.exp2(m - m_new) * acc + _dot(pb, v)

    _, acc = lax.fori_loop(0, qi // CH + 1, sel_body,
                           (jnp.full((R * T, 1), NEG, F32), jnp.zeros((R * T, LANES), F32)))
    o_sel = acc[:, :DH] / acc[:, DH:DH + 1]

    subs, vals = [], []
    for d in range(n_win + 1):
        off = pl.multiple_of(jnp.maximum(qi - d, 0) * T, T)
        tile = jnp.where(qi >= d, d, n_win + 1)
        subs.append((_dot_nt(q4, kw_ref[0, 0, pl.ds(off, T), :]).reshape(R, T, T) + winb_ref[tile]).reshape(R * T, T))
        vals.append(vw_ref[0, 0, pl.ds(off, T), :])
    m_w = jnp.max(functools.reduce(jnp.maximum, subs), axis=1, keepdims=True)
    acc = functools.reduce(lambda a, b: a + b,
                           [_dot(jnp.exp2(sj - m_w).astype(BF16), vj) for sj, vj in zip(subs, vals)])
    o_win = acc[:, :DH] / acc[:, DH:DH + 1]

    gates = _sigmoid(gp_ref[0] + bg_ref[...])
    gcol = lambda j: jnp.concatenate([gates[:, 3 * r + j:3 * r + j + 1] for r in range(R)], axis=0)
    out4 = gcol(0) * o_cmp + gcol(1) * o_sel + gcol(2) * o_win
    o_ref[0] = jnp.concatenate([out4[r * T:(r + 1) * T] for r in range(R)], axis=1)


def _t5_bucket(dist):
    dist = jnp.maximum(dist, 0)
    max_exact = REL_BUCKETS // 2
    log_ratio = jnp.log(jnp.maximum(dist, 1).astype(F32) / max_exact) / math.log(REL_MAX_DIST / max_exact)
    large = jnp.minimum(max_exact + (log_ratio * (REL_BUCKETS - max_exact)).astype(jnp.int32), REL_BUCKETS - 1)
    return jnp.where(dist < max_exact, dist, large)


def _nsa_tables(rel_bias, S):
    T = ATT_TILE
    table = rel_bias.astype(F32) * LOG2E
    ii = jnp.arange(T)
    delta = ii[:, None] - ii[None, :]

    def tile(off):
        return table[_t5_bucket(off * T + delta)].transpose(2, 0, 1)

    n_far = -(-REL_MAX_DIST // T) + 1
    selb = [tile(o) for o in range(n_far + 1)]
    selb[0] = selb[0] + jnp.where(delta >= 0, 0.0, NEG)[None]
    selb = jnp.stack(selb, axis=0)
    n_win = WINDOW // T
    winb = []
    for o in range(n_win + 1):
        dist = o * T + delta
        ok = (dist >= 0) & (dist < WINDOW)
        winb.append(tile(o) + jnp.where(ok, 0.0, NEG)[None])
    winb.append(jnp.full_like(winb[0], NEG))
    winb = jnp.stack(winb, axis=0)
    n_pad = S // CMP_STRIDE
    i16 = jnp.arange(CMP_STRIDE)
    dd = jnp.arange(n_pad)
    gdist = CMP_STRIDE * dd[None, :] + i16[:, None] - (CMP_BLOCK - 1)
    grev = table[_t5_bucket(gdist)].transpose(2, 0, 1)[:, :, ::-1]
    n_sel = S // SEL_BLOCK
    cmp_start = jnp.arange(n_pad) * CMP_STRIDE
    sel_start = jnp.arange(n_sel) * SEL_BLOCK
    overlap = jnp.clip(jnp.minimum(cmp_start[:, None] + CMP_BLOCK, sel_start[None] + SEL_BLOCK)
                       - jnp.maximum(cmp_start[:, None], sel_start[None]), 0).astype(F32) / CMP_BLOCK
    n_cmp = (S - CMP_BLOCK) // CMP_STRIDE + 1
    overlap_t = jnp.where((jnp.arange(n_pad) < n_cmp)[:, None], overlap, 0.0).T
    tk = min(SEL_CHUNK * T, S)
    kpos_blk = jnp.arange(S) // SEL_BLOCK
    rows = jnp.arange(LANES)[:, None]
    expand = jnp.where(rows == kpos_blk[None, :], BIG, jnp.where(rows == n_sel, -BIG, 0.0)).astype(BF16)
    expand = expand.reshape(LANES, S // tk, tk).transpose(1, 0, 2)
    return grev, selb, winb, overlap_t, expand


def _nsa_attention(q, gp, bg, kcmp, vcmp, ks, vs, kw, vw, tables):
    B, S, D = q.shape
    T = ATT_TILE
    grev, selb, winb, overlap_t, expand = tables
    gw = NSA_R * NSA_DH
    seq = lambda a: pl.BlockSpec((1, 1) + a.shape[2:], lambda b, g, i: (b, g, 0, 0))
    per_head = lambda a: pl.BlockSpec((NSA_R,) + a.shape[1:], lambda b, g, i: (g,) + (0,) * (a.ndim - 1))
    tiles = lambda a: pl.BlockSpec((a.shape[0], NSA_R) + a.shape[2:], lambda b, g, i: (0, g, 0, 0))
    full = lambda a: pl.BlockSpec(a.shape, lambda b, g, i: (0,) * a.ndim)
    return pl.pallas_call(
        _nsa_kernel,
        out_shape=jax.ShapeDtypeStruct((B, S, D), F32),
        grid=(B, NSA_KV, S // T),
        in_specs=[pl.BlockSpec((1, T, gw), lambda b, g, i: (b, i, g)),
                  pl.BlockSpec((1, T, LANES), lambda b, g, i: (b, i, g)),
                  pl.BlockSpec((1, LANES), lambda b, g, i: (0, g)),
                  seq(kcmp), seq(vcmp), seq(ks), seq(vs), seq(kw), seq(vw),
                  per_head(grev), tiles(selb), tiles(winb), full(overlap_t), full(expand)],
        out_specs=pl.BlockSpec((1, T, gw), lambda b, g, i: (b, i, g)),
        compiler_params=_cparams(("parallel", "parallel", "arbitrary")),
        name="nsa_attention",
    )(q, gp, bg, kcmp, vcmp, ks, vs, kw, vw, grev, selb, winb, overlap_t, expand)


def _nsa_t_kernel(q_ref, gp_ref, bg_ref, kc_ref, vct_ref, ks_ref, vst_ref, kw_ref, vwt_ref,
                  cfar_ref, band_ref, selb_ref, winb_ref, ovt_ref, o_ref, s_scr, sel_scr, sbuf):
    T = ATT_TILE
    R, DH = NSA_R, NSA_DH
    qi = pl.program_id(2)
    q0 = qi * T
    n_pad = kc_ref.shape[2]
    n_sel = ovt_ref.shape[0]
    CH = vst_ref.shape[4] // T
    n_far = selb_ref.shape[0] - 1
    n_win = winb_ref.shape[0] - 2
    band_rows = band_ref.shape[2] - T // CMP_STRIDE * 2

    q_t = (q_ref[0] * (DH ** -0.5 * LOG2E)).T
    q4 = jnp.concatenate([q_t[r * DH:(r + 1) * DH] for r in range(R)], axis=1).astype(BF16)
    t_lane = q0 + lax.broadcasted_iota(jnp.int32, (1, R * T), 1) % T

    gates_t = _sigmoid(gp_ref[0] + bg_ref[...]).T
    gvec = lambda j: jnp.concatenate([gates_t[3 * r + j:3 * r + j + 1, :] for r in range(R)], axis=1)

    grp = T // CMP_STRIDE
    s_scr[0:n_pad, :] = _dot(kc_ref[0, 0], q4) + cfar_ref[0]
    s_scr[n_pad:n_pad + 2 * grp, :] = jnp.zeros((2 * grp, R * T), F32)
    r0 = jnp.maximum(qi * grp - 2 * grp, 0)
    x0 = r0 - (qi * grp - 2 * grp)
    r0 = pl.multiple_of(r0, SUBLANES)
    x0 = pl.multiple_of(x0, SUBLANES)
    s_scr[pl.ds(r0, band_rows), :] += band_ref[0, 0, pl.ds(x0, band_rows), :]
    lim = pl.multiple_of(qi * grp + 2 * grp, SUBLANES)
    s_scr[pl.ds(lim, n_pad), :] = jnp.full((n_pad, R * T), NEG, F32)

    w_subs, w_vals = [], []
    for d in range(n_win + 1):
        kt = jnp.maximum(qi - d, 0)
        off = pl.multiple_of(kt * T, T)
        tile = jnp.where(qi >= d, d, n_win + 1)
        w_subs.append(_dot(kw_ref[0, 0, pl.ds(off, T), :], q4) + winb_ref[tile, 0])
        w_vals.append(vwt_ref[0, 0, kt])

    s = s_scr[0:n_pad, :]
    e = jnp.exp2(s - jnp.max(s, axis=0, keepdims=True))
    inv = jnp.where(t_lane >= CMP_BLOCK - 1, 1.0 / jnp.sum(e, axis=0, keepdims=True), 0.0)
    p = e * inv
    o_cmp = _dot(vct_ref[0, 0], p.astype(BF16))
    psum = functools.reduce(lambda a, b: a + b, [p[:, r * T:(r + 1) * T] for r in range(R)])

    m_w = jnp.max(functools.reduce(jnp.maximum, w_subs), axis=0, keepdims=True)
    acc = functools.reduce(lambda a, b: a + b,
                           [_dot(vj, jnp.exp2(sj - m_w).astype(BF16)) for sj, vj in zip(w_subs, w_vals)])
    o_win = acc[:DH] * (1.0 / acc[DH:DH + 1])
    out_t = gvec(0) * o_cmp + gvec(2) * o_win

    imp_t = _dot(ovt_ref[...], psum, precision=HIGHEST)
    jj = lax.broadcasted_iota(jnp.int32, (n_sel, T), 0)
    blk_t = (q0 + lax.broadcasted_iota(jnp.int32, (1, T), 1)) // SEL_BLOCK
    forced = (jj == 0) | (jj == blk_t) | (jj == blk_t - 1)
    score = jnp.where(forced, FORCE, jnp.where(jj <= blk_t, imp_t, -1.0))
    n_blk = n_sel // SUBLANES
    rows = [score[v * SUBLANES:(v + 1) * SUBLANES] for v in range(n_blk)]
    cnts = [jnp.zeros((SUBLANES, T), F32) for _ in range(n_blk)]
    sub = lax.broadcasted_iota(jnp.int32, (SUBLANES, T), 0)
    for j2 in range(n_sel):
        c2 = score[j2:j2 + 1, :]
        for v in range(n_blk):
            lo = v * SUBLANES
            if lo > j2:
                beats = c2 >= rows[v]
            elif lo + SUBLANES - 1 <= j2:
                beats = c2 > rows[v]
            else:
                beats = (c2 > rows[v]) | ((c2 >= rows[v]) & (sub > j2 - lo))
            cnts[v] = cnts[v] + jnp.where(beats, 1.0, 0.0)
    cnt = jnp.concatenate(cnts, axis=0)
    chosen = (cnt < float(min(SEL_TOPK, n_sel))) & (jj <= blk_t)
    sel_scr[...] = jnp.where(chosen, 0.0, -BIG)

    def block_mask(kt):
        per_tile = T // SEL_BLOCK
        parts = [jnp.broadcast_to(sel_scr[pl.ds(kt * per_tile + i, 1), :], (SEL_BLOCK, T)) for i in range(per_tile)]
        m1 = jnp.concatenate(parts, axis=0)
        return jnp.concatenate([m1] * R, axis=1)

    def sel_scores(slot, kc):
        off = pl.multiple_of(kc * (CH * T), CH * T)
        s = _dot(ks_ref[0, 0, pl.ds(off, CH * T), :], q4)
        subs = []
        for j in range(CH):
            kt = kc * CH + j
            d = jnp.clip(qi - kt, 0, n_far)
            subs.append(s[j * T:(j + 1) * T] + selb_ref[d, 0] + block_mask(kt))
        s = jnp.concatenate(subs, axis=0)
        sbuf[slot] = s
        return jnp.max(s, axis=0, keepdims=True)

    def sel_consume(slot, kc, m, acc, m_cur):
        m_new = jnp.maximum(m, m_cur)
        pb = jnp.exp2(sbuf[slot] - m_new).astype(BF16)
        return m_new, jnp.exp2(m - m_new) * acc + _dot(vst_ref[0, 0, kc], pb)

    last_chunk = vst_ref.shape[2] - 1

    def sel_body(i, carry):
        m, acc, m_even = carry
        m_odd = sel_scores(1, 2 * i + 1)
        m, acc = sel_consume(0, 2 * i, m, acc, m_even)
        m_even = sel_scores(0, jnp.minimum(2 * i + 2, last_chunk))
        m, acc = sel_consume(1, 2 * i + 1, m, acc, m_odd)
        return m, acc, m_even

    n_chunks = qi // CH + 1
    _, acc, _ = lax.fori_loop(0, (n_chunks + 1) // 2, sel_body,
                              (jnp.full((1, R * T), NEG, F32), jnp.zeros((2 * DH, R * T), F32), sel_scores(0, 0)))
    out_t = out_t + gvec(1) * (acc[:DH] * (1.0 / acc[DH:DH + 1]))
    for pr in range(R // 2):
        pair = jnp.concatenate([out_t[:, (2 * pr) * T:(2 * pr + 1) * T],
                                out_t[:, (2 * pr + 1) * T:(2 * pr + 2) * T]], axis=0)
        o_ref[0, :, pr * 2 * DH:(pr + 1) * 2 * DH] = pair.T


def _nsa_t_tables(rel_bias, S):
    T, R, KV = ATT_TILE, NSA_R, NSA_KV
    table = rel_bias.astype(F32) * LOG2E
    ii = jnp.arange(T)
    delta = ii[None, :] - ii[:, None]

    def lanes(a):
        a = jnp.moveaxis(a, -1, 0)
        a = a.reshape((KV, R) + a.shape[1:])
        return jnp.moveaxis(a, 1, 2).reshape(KV, a.shape[2], R * a.shape[3])

    def tile(off):
        return lanes(table[_t5_bucket(off * T + delta)])

    mask4 = lambda ok: jnp.tile(jnp.where(ok, 0.0, NEG), (1, R))[None]
    n_far = -(-REL_MAX_DIST // T) + 1
    selb = [tile(o) for o in range(n_far + 1)]
    selb[0] = selb[0] + mask4(delta >= 0)
    selb = jnp.stack(selb, axis=0)
    n_win = WINDOW // T
    winb = [tile(o) + mask4((o * T + delta >= 0) & (o * T + delta < WINDOW)) for o in range(n_win + 1)]
    winb.append(jnp.full_like(winb[0], NEG))
    winb = jnp.stack(winb, axis=0)

    grp = T // CMP_STRIDE
    far = table[_t5_bucket(jnp.asarray(2 * REL_MAX_DIST))]
    xx = jnp.arange(4 * grp)
    bdist = ii[None, :] - CMP_STRIDE * (xx[:, None] - 2 * grp) - (CMP_BLOCK - 1)
    band = jnp.where((bdist >= 0)[..., None], table[_t5_bucket(bdist)] - far, NEG)
    band = jnp.concatenate([lanes(band), jnp.zeros((KV, 2 * grp, R * T), F32)], axis=1)[:, None]
    cfar = jnp.repeat(far.reshape(KV, R), T, axis=1)[:, None]

    n_pad = S // CMP_STRIDE
    n_sel = S // SEL_BLOCK
    cmp_start = jnp.arange(n_pad) * CMP_STRIDE
    sel_start = jnp.arange(n_sel) * SEL_BLOCK
    overlap = jnp.clip(jnp.minimum(cmp_start[:, None] + CMP_BLOCK, sel_start[None] + SEL_BLOCK)
                       - jnp.maximum(cmp_start[:, None], sel_start[None]), 0).astype(F32) / CMP_BLOCK
    n_cmp = (S - CMP_BLOCK) // CMP_STRIDE + 1
    overlap_t = jnp.where((jnp.arange(n_pad) < n_cmp)[:, None], overlap, 0.0).T
    return cfar, band, selb, winb, overlap_t


def _nsa_t_attention(q, gp, bg, kcmp, vcmp_t, ks, vs_t, kw, vw_t, tables):
    B, S, D = q.shape
    T = ATT_TILE
    cfar, band, selb, winb, overlap_t = tables
    gw = NSA_R * NSA_DH
    n_pad = kcmp.shape[2]
    seq = lambda a: pl.BlockSpec((1, 1) + a.shape[2:], lambda b, g, i: (b, g) + (0,) * (a.ndim - 2))
    grp = lambda a: pl.BlockSpec((1,) + a.shape[1:], lambda b, g, i: (g,) + (0,) * (a.ndim - 1))
    tiles = lambda a: pl.BlockSpec((a.shape[0], 1) + a.shape[2:], lambda b, g, i: (0, g, 0, 0))
    full = lambda a: pl.BlockSpec(a.shape, lambda b, g, i: (0,) * a.ndim)
    return pl.pallas_call(
        _nsa_t_kernel,
        out_shape=jax.ShapeDtypeStruct((B, S, D), F32),
        grid=(B, NSA_KV, S // T),
        in_specs=[pl.BlockSpec((1, T, gw), lambda b, g, i: (b, i, g)),
                  pl.BlockSpec((1, T, LANES), lambda b, g, i: (b, i, g)),
                  pl.BlockSpec((1, LANES), lambda b, g, i: (0, g)),
                  seq(kcmp), seq(vcmp_t), seq(ks), seq(vs_t), seq(kw), seq(vw_t),
                  grp(cfar), grp(band), tiles(selb), tiles(winb), full(overlap_t)],
        out_specs=pl.BlockSpec((1, T, gw), lambda b, g, i: (b, i, g)),
        scratch_shapes=[pltpu.VMEM((2 * n_pad + 2 * (T // CMP_STRIDE), NSA_R * T), F32),
                        pltpu.VMEM((S // SEL_BLOCK, T), F32),
                        pltpu.VMEM((2, vs_t.shape[4], NSA_R * T), F32)],
        compiler_params=_cparams(("parallel", "parallel", "arbitrary")),
        name="nsa_attention",
    )(q, gp, bg, kcmp, vcmp_t, ks, vs_t, kw, vw_t, cfar, band, selb, winb, overlap_t)


def _moe_kernel(x_ref, g_ref, sh_ref, sc_ref, gate_ref, wr_ref, br_ref, wg_ref, wu_ref, wd_ref, fg_ref,
                o_ref, hb_scr, rt_scr, acc_scr, *, final):
    NG, PG, FH = MOE_GROUPS, MOE_PER_GROUP, MOE_HIDDEN
    c = pl.program_id(2)

    @pl.when(c == 0)
    def _():
        h = _modulated_norm(x_ref[0], g_ref[...], sh_ref[0], sc_ref[0])
        hb_scr[...] = h.astype(BF16)
        logits = _dot(h, wr_ref[...], precision=HIGHEST) + br_ref[...]
        gl = [logits[:, NG * PG + g:NG * PG + g + 1] for g in range(NG)]
        gmax = functools.reduce(jnp.maximum, gl)
        gtop = jnp.full_like(gmax, float(NG - 1))
        for g in reversed(range(NG - 1)):
            gtop = jnp.where(gl[g] == gmax, float(g), gtop)
        p_g = 1.0 / functools.reduce(lambda a, b: a + b, [jnp.exp(v - gmax) for v in gl])
        a = []
        for j in range(PG):
            v = logits[:, (NG - 1) * PG + j:(NG - 1) * PG + j + 1]
            for g in reversed(range(NG - 1)):
                v = jnp.where(gtop == float(g), logits[:, g * PG + j:g * PG + j + 1], v)
            a.append(v)

        def first_max(vals):
            vmax = functools.reduce(jnp.maximum, vals)
            taken = jnp.zeros_like(vmax) > 1.0
            hits = []
            for v in vals:
                hit = (v == vmax) & jnp.logical_not(taken)
                taken = taken | hit
                hits.append(hit)
            return vmax, hits

        v1, hit1 = first_max(a)
        rest = [jnp.where(hh, -jnp.inf, v) for hh, v in zip(hit1, a)]
        v2, hit2 = first_max(rest)
        e2 = jnp.exp(v2 - v1)
        w1 = p_g / (1.0 + e2)
        w2 = p_g * e2 / (1.0 + e2)
        lane = lax.broadcasted_iota(jnp.int32, rt_scr.shape, 1)
        rt = jnp.where(lane == PG, gtop, 0.0)
        for j in range(PG):
            wj = jnp.where(hit1[j], w1, jnp.where(hit2[j], w2, 0.0))
            rt = jnp.where(lane == j, wj, rt)
        rt_scr[...] = rt

    hb = hb_scr[...]
    hid = _silu(_dot(hb, wg_ref[0])) * _dot(hb, wu_ref[0])
    rt = rt_scr[...]
    in_group = rt[:, PG:PG + 1] == c.astype(F32)
    parts = [hid[:, j * FH:(j + 1) * FH] * jnp.where(in_group, rt[:, j:j + 1], 0.0) for j in range(PG)]
    contrib = _dot(jnp.concatenate(parts, axis=1).astype(BF16), wd_ref[0])

    @pl.when(c == 0)
    def _():
        acc_scr[...] = contrib

    @pl.when(c > 0)
    def _():
        acc_scr[...] += contrib

    @pl.when(c == NG - 1)
    def _():
        y = x_ref[0] + gate_ref[0] * acc_scr[...]
        if final:
            y = y * lax.rsqrt(jnp.mean(y * y, axis=-1, keepdims=True) + EPS) * fg_ref[...]
        o_ref[0] = y


def _moe(x, g, shift, scale, gate, wg, bg, we, be, w_gate, w_up, w_down, final_g, final, tm=512):
    B, S, D = x.shape
    NG, PG, FH = MOE_GROUPS, MOE_PER_GROUP, MOE_HIDDEN
    wr = jnp.zeros((D, LANES), F32)
    wr = wr.at[:, :NG * PG].set(we.reshape(D, NG * PG).astype(F32)).at[:, NG * PG:NG * PG + NG].set(wg.astype(F32))
    br = jnp.zeros((1, LANES), F32)
    br = br.at[0, :NG * PG].set(be.reshape(NG * PG).astype(F32)).at[0, NG * PG:NG * PG + NG].set(bg.astype(F32))
    grp = lambda w: w.reshape(NG, PG, D, FH).transpose(0, 2, 1, 3).reshape(NG, D, PG * FH).astype(BF16)
    wd = w_down.reshape(NG, PG * FH, D).astype(BF16)
    vec = pl.BlockSpec((1, 1, D), lambda b, i, c: (b, 0, 0))
    row = pl.BlockSpec((1, D), lambda b, i, c: (0, 0))
    wspec = lambda k, n: pl.BlockSpec((1, k, n), lambda b, i, c: (c, 0, 0))
    return pl.pallas_call(
        functools.partial(_moe_kernel, final=final),
        out_shape=jax.ShapeDtypeStruct((B, S, D), F32),
        grid=(B, S // tm, NG),
        in_specs=[pl.BlockSpec((1, tm, D), lambda b, i, c: (b, i, 0)), row, vec, vec, vec,
                  pl.BlockSpec((D, LANES), lambda b, i, c: (0, 0)),
                  pl.BlockSpec((1, LANES), lambda b, i, c: (0, 0)),
                  wspec(D, PG * FH), wspec(D, PG * FH), wspec(PG * FH, D), row],
        out_specs=pl.BlockSpec((1, tm, D), lambda b, i, c: (b, i, 0)),
        scratch_shapes=[pltpu.VMEM((tm, D), BF16), pltpu.VMEM((tm, LANES), F32), pltpu.VMEM((tm, D), F32)],
        compiler_params=_cparams(("parallel", "parallel", "arbitrary")),
        name="moe",
    )(x, g.reshape(1, D), shift, scale, gate, wr, br, grp(w_gate), grp(w_up), wd, final_g.reshape(1, D))


def _mlstm_s5_layer(x, g, shift, scale, gate, w_in, conv_w, b_i, b_f, head_g, s5_params, w_out):
    H = MLSTM_HEADS
    A = MIX_A
    w_if = jnp.zeros((D_MODEL, LANES), F32).at[:, :2 * H].set(w_in[:, 4 * A:4 * A + 2 * H])
    weights = [w_in[:, :2 * A], w_in[:, 2 * A:4 * A], w_if, w_in[:, 4 * A + 2 * H:]]
    qk, vo, ifg, u = _norm_matmul(x, g, shift, scale, [w.astype(BF16) for w in weights], [F32] * 4)
    gate_bias = jnp.zeros((1, LANES), F32).at[0, :H].set(b_i.astype(F32)).at[0, H:2 * H].set(b_f.astype(F32))
    hm = _mlstm(qk, vo, ifg, conv_w.astype(F32), gate_bias, head_g.reshape(1, A).astype(F32))
    ys = _s5(u, _s5_tables(*s5_params))
    w_out = w_out.astype(BF16)
    return _out_residual(x, gate, [hm, ys], [w_out[:A], w_out[A:]])


def _nsa_layer(x, g, shift, scale, gate, w_in, b_gate, cmp_pos, cmp_w1, cmp_b1, cmp_w2, cmp_b2, rel_bias, w_out):
    B, S, D = x.shape
    KV, R, DH = NSA_KV, NSA_R, NSA_DH
    w_g = jnp.zeros((D, KV, LANES), F32).at[:, :, :3 * R].set(w_in[:, D + 6 * KV_W:].reshape(D, KV, 3 * R))
    b_g = jnp.zeros((KV, LANES), F32).at[:, :3 * R].set(b_gate.reshape(KV, 3 * R).astype(F32))
    weights = [w_in[:, :D], w_in[:, D:D + 6 * KV_W], w_g.reshape(D, KV * LANES)]
    q, kv, gp = _norm_matmul(x, g, shift, scale, [w.astype(BF16) for w in weights], [F32] * 3)
    kv_t = kv.reshape(B, S, 6, KV, DH).transpose(2, 0, 3, 1, 4)
    grp = CMP_STRIDE
    xg = kv_t[0:2].reshape(2, B, KV * S // grp, grp * DH)
    cmp = _compress(xg, cmp_pos, cmp_w1, cmp_b1, cmp_w2, cmp_b2).reshape(2, B, KV, S // grp, DH)
    kvb = kv_t[2:].astype(BF16)
    cmpb = cmp.astype(BF16)
    T = ATT_TILE
    tk = min(SEL_CHUNK * T, S)

    def value_table(v, width):
        vt = jnp.concatenate([v.transpose(0, 1, 3, 2), jnp.ones((B, KV, LANES - DH, S), BF16)], axis=2)
        return vt.reshape(B, KV, LANES, S // width, width).transpose(0, 1, 3, 2, 4)

    out = _nsa_t_attention(q, gp, b_g.reshape(1, KV * LANES), cmpb[0], cmpb[1].transpose(0, 1, 3, 2),
                           kvb[0], value_table(kvb[1], tk), kvb[2], value_table(kvb[3], T),
                           _nsa_t_tables(rel_bias, S))
    return _out_residual(x, gate, [out], [w_out.astype(BF16)])


def kernel(x, c, rel_bias, ada_w, ada_b, norm_g, final_g,
           a_w_in, a_conv, a_b_i, a_b_f, a_head_g,
           s5_lam_re, s5_lam_im, s5_log_dt, s5_b_re, s5_b_im, s5_c_re, s5_c_im,
           s5_d, s5_glu_w, s5_glu_b, a_w_out,
           n_w_in, n_b_gate, n_cmp_pos, n_cmp_w1, n_cmp_b1, n_cmp_w2, n_cmp_b2, n_w_out,
           r_grp_w, r_grp_b, r_exp_w, r_exp_b, e_w_gate, e_w_up, e_w_down):
    B, S, D = x.shape
    mod = _ada_mod(c, ada_w, ada_b).reshape(DEPTH, 2, B, 1, 3 * D)
    split = lambda m: (m[..., :D], m[..., D:2 * D], m[..., 2 * D:])
    for layer in range(DEPTH):
        shift, scale, gate = split(mod[layer, 0])
        j = layer // 2
        if layer % 2 == 0:
            s5_params = (s5_lam_re[j], s5_lam_im[j], s5_log_dt[j], s5_b_re[j], s5_b_im[j],
                         s5_c_re[j], s5_c_im[j], s5_d[j], s5_glu_w[j], s5_glu_b[j])
            x = _mlstm_s5_layer(x, norm_g[layer, 0], shift, scale, gate, a_w_in[j], a_conv[j], a_b_i[j], a_b_f[j],
                                a_head_g[j], s5_params, a_w_out[j])
        else:
            x = _nsa_layer(x, norm_g[layer, 0], shift, scale, gate, n_w_in[j], n_b_gate[j], n_cmp_pos[j],
                           n_cmp_w1[j], n_cmp_b1[j], n_cmp_w2[j], n_cmp_b2[j], rel_bias, n_w_out[j])
        shift, scale, gate = split(mod[layer, 1])
        x = _moe(x, norm_g[layer, 1], shift, scale, gate, r_grp_w[layer], r_grp_b[layer], r_exp_w[layer],
                 r_exp_b[layer], e_w_gate[layer], e_w_up[layer], e_w_down[layer], final_g,
                 final=(layer == DEPTH - 1))
    return x
```

```python
import functools
import math

import jax
import jax.numpy as jnp
from jax import lax
from jax.experimental import pallas as pl
from jax.experimental.pallas import tpu as pltpu

F32 = jnp.float32
BF16 = jnp.bfloat16
HIGHEST = lax.Precision.HIGHEST

D_MODEL = 1024
DEPTH = 2
MIX_A = 512
MLSTM_HEADS = 4
MLSTM_DH = MIX_A // MLSTM_HEADS
MLSTM_CHUNK = 128
CONV_K = 4
MIX_B = D_MODEL - MIX_A
S5_GROUP = 16
S5_GROUPS = MIX_B // S5_GROUP
S5_STATE = 64
S5_CHUNK = 16
NSA_HEADS = 16
NSA_KV = 4
NSA_R = NSA_HEADS // NSA_KV
NSA_DH = D_MODEL // NSA_HEADS
KV_W = NSA_KV * NSA_DH
CMP_BLOCK = 32
CMP_STRIDE = 16
CMP_HIDDEN = 256
SEL_BLOCK = 64
SEL_TOPK = 16
WINDOW = 512
FORCE = 1e9
REL_BUCKETS = 32
REL_MAX_DIST = 128
MOE_GROUPS = 4
MOE_PER_GROUP = 4
MOE_HIDDEN = 256
EPS = 1e-6
NEG = -1e30
BIG = 1e30
LOG2E = math.log2(math.e)
SEL_CHUNK = 2
SEL_UNROLL = 2

LANES = 128
SUBLANES = 8
ATT_TILE = 128
VMEM_LIMIT = 56 * 1024 * 1024


def _cparams(sem):
    return pltpu.CompilerParams(dimension_semantics=sem, vmem_limit_bytes=VMEM_LIMIT)


def _dot(a, b, precision=None):
    return jnp.dot(a, b, preferred_element_type=F32, precision=precision)


def _dot_nt(a, b):
    return lax.dot_general(a, b, (((1,), (1,)), ((), ())), preferred_element_type=F32)


def _sigmoid(x):
    return 1.0 / (1.0 + jnp.exp(-x))


def _silu(x):
    return x * _sigmoid(x)


def _gelu_tanh(x):
    return 0.5 * x * (1.0 + jnp.tanh(math.sqrt(2.0 / math.pi) * (x + 0.044715 * (x * x * x))))


def _modulated_norm(x, g, shift, scale):
    y = x * lax.rsqrt(jnp.mean(x * x, axis=-1, keepdims=True) + EPS) * g
    return y * (1.0 + scale) + shift


def _ada_kernel(c_ref, w_ref, b_ref, o_ref):
    c = c_ref[...]
    o_ref[0] = _dot(_silu(c), w_ref[0]) + b_ref[0]


def _ada_mod(c, ada_w, ada_b):
    B, D = c.shape
    n_mod = ada_w.shape[0] * ada_w.shape[1]
    w = ada_w.reshape(n_mod, D, 3 * D)
    b = ada_b.reshape(n_mod, 1, 3 * D)
    tn = 1024
    return pl.pallas_call(
        _ada_kernel,
        out_shape=jax.ShapeDtypeStruct((n_mod, B, 3 * D), F32),
        grid=(n_mod, 3 * D // tn),
        in_specs=[pl.BlockSpec((B, D), lambda i, j: (0, 0)),
                  pl.BlockSpec((1, D, tn), lambda i, j: (i, 0, j)),
                  pl.BlockSpec((1, 1, tn), lambda i, j: (i, 0, j))],
        out_specs=pl.BlockSpec((1, B, tn), lambda i, j: (i, 0, j)),
        compiler_params=_cparams(("parallel", "parallel")),
        name="ada_mod",
    )(c, w, b)


def _norm_mm_kernel(*refs, n_w):
    x_ref, g_ref, sh_ref, sc_ref = refs[:4]
    w_refs = refs[4:4 + n_w]
    o_refs = refs[4 + n_w:]
    h = _modulated_norm(x_ref[0], g_ref[...], sh_ref[0], sc_ref[0]).astype(BF16)
    for w_ref, o_ref in zip(w_refs, o_refs):
        o_ref[0] = _dot(h, w_ref[...]).astype(o_ref.dtype)


def _norm_matmul(x, g, shift, scale, weights, out_dtypes, tm=512):
    B, S, D = x.shape
    n_w = len(weights)
    vec = pl.BlockSpec((1, 1, D), lambda b, i: (b, 0, 0))
    in_specs = [pl.BlockSpec((1, tm, D), lambda b, i: (b, i, 0)),
                pl.BlockSpec((1, D), lambda b, i: (0, 0)), vec, vec]
    in_specs += [pl.BlockSpec(w.shape, lambda b, i: (0, 0)) for w in weights]
    return pl.pallas_call(
        functools.partial(_norm_mm_kernel, n_w=n_w),
        out_shape=[jax.ShapeDtypeStruct((B, S, w.shape[1]), dt) for w, dt in zip(weights, out_dtypes)],
        grid=(B, S // tm),
        in_specs=in_specs,
        out_specs=[pl.BlockSpec((1, tm, w.shape[1]), lambda b, i: (b, i, 0)) for w in weights],
        compiler_params=_cparams(("parallel", "parallel")),
        name="norm_matmul",
    )(x, g.reshape(1, D), shift, scale, *weights)


def _out_res_kernel(*refs, n_in):
    x_ref, gate_ref = refs[:2]
    a_refs = refs[2:2 + n_in]
    w_refs = refs[2 + n_in:2 + 2 * n_in]
    o_ref = refs[2 + 2 * n_in]
    acc = None
    for a_ref, w_ref in zip(a_refs, w_refs):
        t = _dot(a_ref[0].astype(BF16), w_ref[...])
        acc = t if acc is None else acc + t
    o_ref[0] = x_ref[0] + gate_ref[0] * acc


def _out_residual(x, gate, acts, weights, tm=512):
    B, S, D = x.shape
    n_in = len(acts)
    in_specs = [pl.BlockSpec((1, tm, D), lambda b, i: (b, i, 0)),
                pl.BlockSpec((1, 1, D), lambda b, i: (b, 0, 0))]
    in_specs += [pl.BlockSpec((1, tm, a.shape[2]), lambda b, i: (b, i, 0)) for a in acts]
    in_specs += [pl.BlockSpec(w.shape, lambda b, i: (0, 0)) for w in weights]
    return pl.pallas_call(
        functools.partial(_out_res_kernel, n_in=n_in),
        out_shape=jax.ShapeDtypeStruct((B, S, D), F32),
        grid=(B, S // tm),
        in_specs=in_specs,
        out_specs=pl.BlockSpec((1, tm, D), lambda b, i: (b, i, 0)),
        compiler_params=_cparams(("parallel", "parallel")),
        name="out_residual",
    )(x, gate, *acts, *weights)


def _mlstm_kernel(qk_ref, vo_ref, if_ref, cw_ref, gb_ref, hg_ref, tril_ref, o_ref,
                  xbuf, c_scr, n_scr, m_scr):
    L, H, DH = MLSTM_CHUNK, MLSTM_HEADS, MLSTM_DH
    pad = SUBLANES

    @pl.when(pl.program_id(1) == 0)
    def _():
        xbuf[0:pad, :] = jnp.zeros((pad, 2 * MIX_A), F32)
        c_scr[...] = jnp.zeros_like(c_scr)
        n_scr[...] = jnp.zeros_like(n_scr)
        m_scr[...] = jnp.zeros_like(m_scr)

    xbuf[pad:pad + L, :] = qk_ref[0]
    cw = cw_ref[...]
    conv = None
    for j in range(CONV_K):
        lo = pad - (CONV_K - 1) + j
        t = xbuf[lo:lo + L, :] * cw[j:j + 1, :]
        conv = t if conv is None else conv + t
    xbuf[0:pad, :] = xbuf[L:L + pad, :]
    qk = _silu(conv)
    q = qk[:, :MIX_A]
    k = qk[:, MIX_A:] * (DH ** -0.5)
    vo = vo_ref[0]
    v = vo[:, :MIX_A]
    o_pre = vo[:, MIX_A:]

    ifb = if_ref[0] + gb_ref[...]
    lf = jnp.minimum(ifb, 0.0) - jnp.log1p(jnp.exp(-jnp.abs(ifb)))
    bcs = _dot(tril_ref[...], lf, precision=HIGHEST)
    ifb_t = ifb.T
    bcs_t = bcs.T
    row = lax.broadcasted_iota(jnp.int32, (L, L), 0)
    col = lax.broadcasted_iota(jnp.int32, (L, L), 1)
    causal = col <= row

    outs = []
    for h in range(H):
        sl = slice(h * DH, (h + 1) * DH)
        qh, kh, vh = q[:, sl], k[:, sl], v[:, sl]
        qb, kb = qh.astype(BF16), kh.astype(BF16)
        b_col = bcs[:, H + h:H + h + 1]
        b_row = bcs_t[H + h:H + h + 1, :]
        li_col = ifb[:, h:h + 1]
        li_row = ifb_t[h:h + 1, :]
        b_last = b_col[L - 1:L, :]
        m0 = m_scr[h][:, 0:1]
        c0 = c_scr[h]
        n0 = n_scr[h]

        log_d = jnp.where(causal, b_col - b_row + li_row, NEG)
        log_inter = b_col + m0
        m_t = jnp.maximum(log_inter, jnp.max(log_d, axis=1, keepdims=True))
        dmat = jnp.exp(log_d - m_t)
        a_inter = jnp.exp(log_inter - m_t)
        s = _dot_nt(qb, kb) * dmat
        num = _dot(s.astype(BF16), vh.astype(BF16)) + a_inter * _dot_nt(qb, c0.astype(BF16))
        den = jnp.sum(s, axis=1, keepdims=True) + a_inter * jnp.sum(qh * n0, axis=1, keepdims=True)
        hh = num / jnp.maximum(jnp.abs(den), jnp.exp(-m_t))

        w_col = b_last - b_col + li_col
        m_loc = jnp.max(w_col, axis=0, keepdims=True)
        e = jnp.exp(w_col - m_loc)
        c_loc = _dot((vh * e).T.astype(BF16), kb)
        n_loc = jnp.sum(kh * e, axis=0, keepdims=True)
        m_new = jnp.maximum(b_last + m0, m_loc)
        a = jnp.exp(b_last + m0 - m_new)
        sc = jnp.exp(m_loc - m_new)
        c_scr[h] = a * c0 + sc * c_loc
        n_scr[h] = a * n0 + sc * n_loc
        m_scr[h] = jnp.broadcast_to(m_new, (1, LANES))

        outs.append(hh * lax.rsqrt(jnp.mean(hh * hh, axis=1, keepdims=True) + EPS))
    hm = jnp.concatenate(outs, axis=1)
    o_ref[0] = _sigmoid(o_pre) * (hm * hg_ref[...])


def _mlstm(qk, vo, ifg, conv_w, gate_bias, head_g):
    B, S, _ = qk.shape
    L, H, DH = MLSTM_CHUNK, MLSTM_HEADS, MLSTM_DH
    tril = jnp.tril(jnp.ones((L, L), F32))
    return pl.pallas_call(
        _mlstm_kernel,
        out_shape=jax.ShapeDtypeStruct((B, S, MIX_A), F32),
        grid=(B, S // L),
        in_specs=[pl.BlockSpec((1, L, 2 * MIX_A), lambda b, c: (b, c, 0)),
                  pl.BlockSpec((1, L, 2 * MIX_A), lambda b, c: (b, c, 0)),
                  pl.BlockSpec((1, L, LANES), lambda b, c: (b, c, 0)),
                  pl.BlockSpec((CONV_K, 2 * MIX_A), lambda b, c: (0, 0)),
                  pl.BlockSpec((1, LANES), lambda b, c: (0, 0)),
                  pl.BlockSpec((1, MIX_A), lambda b, c: (0, 0)),
                  pl.BlockSpec((L, L), lambda b, c: (0, 0))],
        out_specs=pl.BlockSpec((1, L, MIX_A), lambda b, c: (b, c, 0)),
        scratch_shapes=[pltpu.VMEM((L + SUBLANES, 2 * MIX_A), F32),
                        pltpu.VMEM((H, DH, DH), F32),
                        pltpu.VMEM((H, 1, DH), F32),
                        pltpu.VMEM((H, 1, LANES), F32)],
        compiler_params=_cparams(("parallel", "arbitrary")),
        name="mlstm",
    )(qk, vo, ifg, conv_w, gate_bias, head_g, tril)


def _s5_kernel(u_ref, m_ref, hre_ref, him_ref, ere_ref, eim_ref, are_ref, aim_ref, d_ref, gw_ref, gb_ref, o_ref,
               xl_re, xl_im, x0_re, x0_im, *, n_chunks, batch):
    u = u_ref[0]
    xl_re[...] = _dot(u, hre_ref[0])
    xl_im[...] = _dot(u, him_ref[0])
    a_re = are_ref[0]
    a_im = aim_ref[0]

    def body(i, carry):
        re, im = carry
        r = pl.multiple_of(i * batch, batch)
        x0_re[pl.ds(r, batch), :] = re
        x0_im[pl.ds(r, batch), :] = im
        return (a_re * re - a_im * im + xl_re[pl.ds(r, batch), :],
                a_re * im + a_im * re + xl_im[pl.ds(r, batch), :])

    zero = jnp.zeros((batch, S5_STATE), F32)
    lax.fori_loop(0, n_chunks, body, (zero, zero), unroll=8)
    y = (_dot(u, m_ref[0]) + _dot(x0_re[...].astype(BF16), ere_ref[0]) + _dot(x0_im[...].astype(BF16), eim_ref[0])
         + u.astype(F32) * d_ref[0])
    ys = _gelu_tanh(y)
    z = _dot(ys.astype(BF16), gw_ref[0]) + gb_ref[0]
    o_ref[0] = (ys * _sigmoid(z)).astype(o_ref.dtype)


def _s5_tables(lam_re, lam_im, log_dt, b_re, b_im, c_re, c_im, d_skip, glu_w, glu_b):
    T, C, P = S5_CHUNK, S5_GROUP, S5_STATE
    G = lam_re.shape[0]
    lam = lax.complex(lam_re.astype(F32), lam_im.astype(F32))
    dt = jnp.exp(log_dt.astype(F32))[:, None]
    lam_bar = jnp.exp(lam * dt)
    b_bar = ((lam_bar - 1.0) / lam)[..., None] * lax.complex(b_re.astype(F32), b_im.astype(F32))
    c_mat = lax.complex(c_re.astype(F32), c_im.astype(F32))
    taus = jnp.arange(T + 1, dtype=F32)
    pw = jnp.exp((lam * dt)[:, None, :] * taus[None, :, None])
    kern = jnp.einsum('gcp,gtp,gpd->gtcd', c_mat, pw[:, :T], b_bar,
                      precision=HIGHEST).real
    tt = jnp.arange(T)
    shift = (tt[:, None, None] - tt[None, :, None] == tt[None, None, :]).astype(F32)
    toe = jnp.einsum('tsu,gucd->gtscd', shift, kern, precision=HIGHEST)
    m_t = toe.transpose(0, 2, 4, 1, 3).reshape(G, T * C, T * C).astype(BF16)
    hmat = (pw[:, :T][:, ::-1, :, None] * b_bar[:, None]).transpose(0, 1, 3, 2).reshape(G, T * C, P)
    emat = (c_mat[:, None] * pw[:, 1:][:, :, None, :]).reshape(G, T * C, P).transpose(0, 2, 1)
    a_re, a_im = pw[:, T].real, pw[:, T].imag
    rows8 = lambda a: jnp.broadcast_to(a[:, None], (G, SUBLANES, P))
    d_t = jnp.tile(d_skip.astype(F32), (1, T))[:, None]
    eye = jnp.eye(T, dtype=F32)
    gw = jnp.einsum('ts,gce->gtcse', eye, glu_w.astype(F32)).reshape(G, T * C, T * C).astype(BF16)
    gb = jnp.tile(glu_b.astype(F32), (1, T))[:, None]
    return (m_t, hmat.real.astype(BF16), hmat.imag.astype(BF16), emat.real.astype(BF16), (-emat.imag).astype(BF16),
            rows8(a_re), rows8(a_im), d_t, gw, gb)


def _s5(u, tables):
    B, S, _ = u.shape
    T, C, P, G = S5_CHUNK, S5_GROUP, S5_STATE, S5_GROUPS
    assert B == SUBLANES
    n_chunks = S // T
    rows = n_chunks * B
    ug = u.reshape(B, n_chunks, T, G, C).transpose(3, 1, 0, 2, 4).reshape(G, rows, T * C)
    per_g = lambda a: pl.BlockSpec((1,) + a.shape[1:], lambda g: (g, 0, 0))
    out = pl.pallas_call(
        functools.partial(_s5_kernel, n_chunks=n_chunks, batch=B),
        out_shape=jax.ShapeDtypeStruct((G, rows, T * C), BF16),
        grid=(G,),
        in_specs=[per_g(ug)] + [per_g(t) for t in tables],
        out_specs=per_g(ug),
        scratch_shapes=[pltpu.VMEM((rows, P), F32) for _ in range(4)],
        compiler_params=_cparams(("parallel",)),
        name="s5",
    )(ug, *tables)
    return out.reshape(G, n_chunks, B, T, C).transpose(2, 1, 3, 0, 4).reshape(B, S, G * C)


def _compress_kernel(x_ref, plo_ref, phi_ref, w1_ref, b1_ref, w2_ref, b2_ref, o_ref):
    x = x_ref[0, 0]
    half = x.shape[1]
    w1 = w1_ref[0]
    lo = _dot((x + plo_ref[0]).astype(BF16), w1[:half])
    hi = _dot((x + phi_ref[0]).astype(BF16), w1[half:])
    rows = x.shape[0]
    hid = _gelu_tanh(lo + pltpu.roll(hi, rows - 1, 0) + b1_ref[0])
    o_ref[0, 0] = _dot(hid.astype(BF16), w2_ref[0]) + b2_ref[0]


def _compress(xg, pos, w1, b1, w2, b2):
    _, B, rows, width = xg.shape
    pos_flat = pos.reshape(2, 2, 1, width).astype(F32)
    sel = lambda shape: pl.BlockSpec((1,) + shape, lambda j, b: (j, 0, 0))
    return pl.pallas_call(
        _compress_kernel,
        out_shape=jax.ShapeDtypeStruct((2, B, rows, NSA_DH), F32),
        grid=(2, B),
        in_specs=[pl.BlockSpec((1, 1, rows, width), lambda j, b: (j, b, 0, 0)),
                  sel((1, width)), sel((1, width)),
                  sel((2 * width, CMP_HIDDEN)), sel((1, CMP_HIDDEN)),
                  sel((CMP_HIDDEN, NSA_DH)), sel((1, NSA_DH))],
        out_specs=pl.BlockSpec((1, 1, rows, NSA_DH), lambda j, b: (j, b, 0, 0)),
        compiler_params=_cparams(("parallel", "parallel")),
        name="nsa_compress",
    )(xg, pos_flat[:, 0], pos_flat[:, 1], w1.astype(BF16), b1[:, None].astype(F32),
      w2.astype(BF16), b2[:, None].astype(F32))


def _nsa_kernel(q_ref, gp_ref, bg_ref, kc_ref, vc_ref, ks_ref, vs_ref, kw_ref, vw_ref,
                grev_ref, selb_ref, winb_ref, ovt_ref, ex_ref, o_ref):
    T = ATT_TILE
    R, DH = NSA_R, NSA_DH
    qi = pl.program_id(2)
    q0 = qi * T
    qall = q_ref[0] * (DH ** -0.5 * LOG2E)
    t_col = q0 + lax.broadcasted_iota(jnp.int32, (T, 1), 0)
    n_cmp_pad = kc_ref.shape[2]
    n_sel = ovt_ref.shape[0]
    grp_rows = CMP_STRIDE

    q4 = jnp.concatenate([qall[:, r * DH:(r + 1) * DH] for r in range(R)], axis=0).astype(BF16)

    n_row = lax.broadcasted_iota(jnp.int32, (1, n_cmp_pad), 1)
    cmask = (t_col >= n_row * CMP_STRIDE + (CMP_BLOCK - 1))[None]
    bias = jnp.stack([jnp.concatenate(
        [pltpu.roll(grev_ref[r], (qi * (T // grp_rows) + al + 1) % n_cmp_pad, 1) for al in range(T // grp_rows)],
        axis=0) for r in range(R)], axis=0)
    s = jnp.where(cmask, _dot_nt(q4, kc_ref[0, 0]).reshape(R, T, n_cmp_pad) + bias, NEG)
    p = jnp.exp2(s - jnp.max(s, axis=2, keepdims=True))
    p = p / jnp.sum(p, axis=2, keepdims=True)
    p = jnp.where(cmask, p, 0.0)
    o_cmp = _dot(p.reshape(R * T, n_cmp_pad).astype(BF16), vc_ref[0, 0])
    psum = functools.reduce(lambda a, b: a + b, [p[r] for r in range(R)])

    imp_t = lax.dot_general(ovt_ref[...], psum, (((1,), (1,)), ((), ())), precision=HIGHEST,
                            preferred_element_type=F32)
    jj = lax.broadcasted_iota(jnp.int32, (n_sel, T), 0)
    blk_t = (q0 + lax.broadcasted_iota(jnp.int32, (1, T), 1)) // SEL_BLOCK
    forced = (jj == 0) | (jj == blk_t) | (jj == blk_t - 1)
    score = jnp.where(forced, FORCE, jnp.where(jj <= blk_t, imp_t, -1.0))
    n_blk = n_sel // SUBLANES
    rows = [score[v * SUBLANES:(v + 1) * SUBLANES] for v in range(n_blk)]
    cnts = [jnp.zeros((SUBLANES, T), F32) for _ in range(n_blk)]
    sub = lax.broadcasted_iota(jnp.int32, (SUBLANES, T), 0)
    for j2 in range(n_sel):
        c2 = score[j2:j2 + 1, :]
        for v in range(n_blk):
            lo = v * SUBLANES
            if lo > j2:
                beats = c2 >= rows[v]
            elif lo + SUBLANES - 1 <= j2:
                beats = c2 > rows[v]
            else:
                beats = (c2 > rows[v]) | ((c2 >= rows[v]) & (sub > j2 - lo))
            cnts[v] = cnts[v] + jnp.where(beats, 1.0, 0.0)
    cnt = jnp.concatenate(cnts, axis=0)
    sel_t = jnp.where((cnt < float(min(SEL_TOPK, n_sel))) & (jj <= blk_t), 1.0, 0.0)
    sel_q = jnp.concatenate([sel_t, jnp.zeros((LANES - n_sel, T), F32)], axis=0).T
    lane = lax.broadcasted_iota(jnp.int32, (T, LANES), 1)
    sel_aug = jnp.where(lane == n_sel, 1.0, sel_q).astype(BF16)

    n_far = selb_ref.shape[0] - 1
    n_win = winb_ref.shape[0] - 2
    CH = ex_ref.shape[2] // T

    def sel_body(kc, carry):
        m, acc = carry
        off = pl.multiple_of(kc * (CH * T), CH * T)
        k = ks_ref[0, 0, pl.ds(off, CH * T), :]
        v = vs_ref[0, 0, pl.ds(off, CH * T), :]
        s = _dot_nt(q4, k)
        mask = _dot(sel_aug, ex_ref[kc])
        subs = []
        for j in range(CH):
            d = jnp.clip(qi - (kc * CH + j), 0, n_far)
            sj = s[:, j * T:(j + 1) * T].reshape(R, T, T) + selb_ref[d] + mask[:, j * T:(j + 1) * T][None]
            subs.append(sj.reshape(R * T, T))
        m_new = jnp.maximum(m, jnp.max(functools.reduce(jnp.maximum, subs), axis=1, keepdims=True))
        pb = jnp.concatenate([jnp.exp2(sj - m_new).astype(BF16) for sj in subs], axis=1)
        return m_new, jnp.exp2(m - m_new) * acc + _dot(pb, v)

    _, acc = lax.fori_loop(0, qi // CH + 1, sel_body,
                           (jnp.full((R * T, 1), NEG, F32), jnp.zeros((R * T, LANES), F32)))
    o_sel = acc[:, :DH] / acc[:, DH:DH + 1]

    subs, vals = [], []
    for d in range(n_win + 1):
        off = pl.multiple_of(jnp.maximum(qi - d, 0) * T, T)
        tile = jnp.where(qi >= d, d, n_win + 1)
        subs.append((_dot_nt(q4, kw_ref[0, 0, pl.ds(off, T), :]).reshape(R, T, T) + winb_ref[tile]).reshape(R * T, T))
        vals.append(vw_ref[0, 0, pl.ds(off, T), :])
    m_w = jnp.max(functools.reduce(jnp.maximum, subs), axis=1, keepdims=True)
    acc = functools.reduce(lambda a, b: a + b,
                           [_dot(jnp.exp2(sj - m_w).astype(BF16), vj) for sj, vj in zip(subs, vals)])
    o_win = acc[:, :DH] / acc[:, DH:DH + 1]

    gates = _sigmoid(gp_ref[0] + bg_ref[...])
    gcol = lambda j: jnp.concatenate([gates[:, 3 * r + j:3 * r + j + 1] for r in range(R)], axis=0)
    out4 = gcol(0) * o_cmp + gcol(1) * o_sel + gcol(2) * o_win
    o_ref[0] = jnp.concatenate([out4[r * T:(r + 1) * T] for r in range(R)], axis=1)


def _t5_bucket(dist):
    dist = jnp.maximum(dist, 0)
    max_exact = REL_BUCKETS // 2
    log_ratio = jnp.log(jnp.maximum(dist, 1).astype(F32) / max_exact) / math.log(REL_MAX_DIST / max_exact)
    large = jnp.minimum(max_exact + (log_ratio * (REL_BUCKETS - max_exact)).astype(jnp.int32), REL_BUCKETS - 1)
    return jnp.where(dist < max_exact, dist, large)


def _nsa_tables(rel_bias, S):
    T = ATT_TILE
    table = rel_bias.astype(F32) * LOG2E
    ii = jnp.arange(T)
    delta = ii[:, None] - ii[None, :]

    def tile(off):
        return table[_t5_bucket(off * T + delta)].transpose(2, 0, 1)

    n_far = -(-REL_MAX_DIST // T) + 1
    selb = [tile(o) for o in range(n_far + 1)]
    selb[0] = selb[0] + jnp.where(delta >= 0, 0.0, NEG)[None]
    selb = jnp.stack(selb, axis=0)
    n_win = WINDOW // T
    winb = []
    for o in range(n_win + 1):
        dist = o * T + delta
        ok = (dist >= 0) & (dist < WINDOW)
        winb.append(tile(o) + jnp.where(ok, 0.0, NEG)[None])
    winb.append(jnp.full_like(winb[0], NEG))
    winb = jnp.stack(winb, axis=0)
    n_pad = S // CMP_STRIDE
    i16 = jnp.arange(CMP_STRIDE)
    dd = jnp.arange(n_pad)
    gdist = CMP_STRIDE * dd[None, :] + i16[:, None] - (CMP_BLOCK - 1)
    grev = table[_t5_bucket(gdist)].transpose(2, 0, 1)[:, :, ::-1]
    n_sel = S // SEL_BLOCK
    cmp_start = jnp.arange(n_pad) * CMP_STRIDE
    sel_start = jnp.arange(n_sel) * SEL_BLOCK
    overlap = jnp.clip(jnp.minimum(cmp_start[:, None] + CMP_BLOCK, sel_start[None] + SEL_BLOCK)
                       - jnp.maximum(cmp_start[:, None], sel_start[None]), 0).astype(F32) / CMP_BLOCK
    n_cmp = (S - CMP_BLOCK) // CMP_STRIDE + 1
    overlap_t = jnp.where((jnp.arange(n_pad) < n_cmp)[:, None], overlap, 0.0).T
    tk = min(SEL_CHUNK * T, S)
    kpos_blk = jnp.arange(S) // SEL_BLOCK
    rows = jnp.arange(LANES)[:, None]
    expand = jnp.where(rows == kpos_blk[None, :], BIG, jnp.where(rows == n_sel, -BIG, 0.0)).astype(BF16)
    expand = expand.reshape(LANES, S // tk, tk).transpose(1, 0, 2)
    return grev, selb, winb, overlap_t, expand


def _nsa_attention(q, gp, bg, kcmp, vcmp, ks, vs, kw, vw, tables):
    B, S, D = q.shape
    T = ATT_TILE
    grev, selb, winb, overlap_t, expand = tables
    gw = NSA_R * NSA_DH
    seq = lambda a: pl.BlockSpec((1, 1) + a.shape[2:], lambda b, g, i: (b, g, 0, 0))
    per_head = lambda a: pl.BlockSpec((NSA_R,) + a.shape[1:], lambda b, g, i: (g,) + (0,) * (a.ndim - 1))
    tiles = lambda a: pl.BlockSpec((a.shape[0], NSA_R) + a.shape[2:], lambda b, g, i: (0, g, 0, 0))
    full = lambda a: pl.BlockSpec(a.shape, lambda b, g, i: (0,) * a.ndim)
    return pl.pallas_call(
        _nsa_kernel,
        out_shape=jax.ShapeDtypeStruct((B, S, D), F32),
        grid=(B, NSA_KV, S // T),
        in_specs=[pl.BlockSpec((1, T, gw), lambda b, g, i: (b, i, g)),
                  pl.BlockSpec((1, T, LANES), lambda b, g, i: (b, i, g)),
                  pl.BlockSpec((1, LANES), lambda b, g, i: (0, g)),
                  seq(kcmp), seq(vcmp), seq(ks), seq(vs), seq(kw), seq(vw),
                  per_head(grev), tiles(selb), tiles(winb), full(overlap_t), full(expand)],
        out_specs=pl.BlockSpec((1, T, gw), lambda b, g, i: (b, i, g)),
        compiler_params=_cparams(("parallel", "parallel", "arbitrary")),
        name="nsa_attention",
    )(q, gp, bg, kcmp, vcmp, ks, vs, kw, vw, grev, selb, winb, overlap_t, expand)


def _nsa_proj_kernel(x_ref, g_ref, sh_ref, sc_ref, wq_ref, wk_ref, wv_ref, wg_ref,
                     q_ref, gp_ref, kc_ref, vc_ref, ks_ref, kw_ref, vst_ref, vwt_ref):
    KV, DH, T = NSA_KV, NSA_DH, ATT_TILE
    h = _modulated_norm(x_ref[0], g_ref[...], sh_ref[0], sc_ref[0]).astype(BF16)
    q_ref[0] = _dot(h, wq_ref[...])
    gp_ref[0] = _dot(h, wg_ref[...])
    k3 = _dot(h, wk_ref[...])
    v3 = _dot(h, wv_ref[...])
    vs_t = v3[:, KV_W:2 * KV_W].T.astype(BF16)
    vw_t = v3[:, 2 * KV_W:].T.astype(BF16)
    for g in range(KV):
        cols = slice(g * DH, (g + 1) * DH)
        kc_ref[0, g] = k3[:, cols].astype(BF16)
        vc_ref[0, g] = v3[:, cols].astype(BF16)
        ks_ref[0, g] = k3[:, KV_W + g * DH:KV_W + (g + 1) * DH].astype(BF16)
        kw_ref[0, g] = k3[:, 2 * KV_W + g * DH:2 * KV_W + (g + 1) * DH].astype(BF16)
        vst_ref[0, g, 0] = vs_t[cols]
        for j in range(vwt_ref.shape[2]):
            vwt_ref[0, g, j] = vw_t[cols, j * T:(j + 1) * T]


def _nsa_proj(x, g, shift, scale, weights):
    B, S, D = x.shape
    KV, DH, T = NSA_KV, NSA_DH, ATT_TILE
    tm = min(SEL_CHUNK * T, S)
    vec = pl.BlockSpec((1, 1, D), lambda b, i: (b, 0, 0))
    rows = lambda n: pl.BlockSpec((1, tm, n), lambda b, i: (b, i, 0))
    keys = pl.BlockSpec((1, KV, tm, DH), lambda b, i: (b, 0, i, 0))
    key_shape = jax.ShapeDtypeStruct((B, KV, S, DH), BF16)
    return pl.pallas_call(
        _nsa_proj_kernel,
        out_shape=[jax.ShapeDtypeStruct((B, S, D), F32), jax.ShapeDtypeStruct((B, S, KV * LANES), F32),
                   key_shape, key_shape, key_shape, key_shape,
                   jax.ShapeDtypeStruct((B, KV, S // tm, DH, tm), BF16),
                   jax.ShapeDtypeStruct((B, KV, S // T, DH, T), BF16)],
        grid=(B, S // tm),
        in_specs=[pl.BlockSpec((1, tm, D), lambda b, i: (b, i, 0)),
                  pl.BlockSpec((1, D), lambda b, i: (0, 0)), vec, vec]
                 + [pl.BlockSpec(w.shape, lambda b, i: (0, 0)) for w in weights],
        out_specs=[rows(D), rows(KV * LANES), keys, keys, keys, keys,
                   pl.BlockSpec((1, KV, 1, DH, tm), lambda b, i: (b, 0, i, 0, 0)),
                   pl.BlockSpec((1, KV, tm // T, DH, T), lambda b, i: (b, 0, i, 0, 0))],
        compiler_params=_cparams(("parallel", "parallel")),
        name="nsa_proj",
    )(x, g.reshape(1, D), shift, scale, *weights)


def _nsa_t_kernel(q_ref, gp_ref, bg_ref, kc_ref, vct_ref, ks_ref, vst_ref, kw_ref, vwt_ref,
                  cfar_ref, band_ref, selb_ref, winb_ref, ovt_ref, o_ref, s_scr, sel_scr, sbuf):
    T = ATT_TILE
    R, DH = NSA_R, NSA_DH
    qi = pl.program_id(2)
    q0 = qi * T
    n_pad = kc_ref.shape[2]
    n_sel = ovt_ref.shape[0]
    CH = vst_ref.shape[4] // T
    n_far = selb_ref.shape[0] - 1
    n_win = winb_ref.shape[0] - 2
    band_rows = band_ref.shape[2] - T // CMP_STRIDE * 2

    q_t = (q_ref[0] * (DH ** -0.5 * LOG2E)).T
    q4 = jnp.concatenate([q_t[r * DH:(r + 1) * DH] for r in range(R)], axis=1).astype(BF16)
    t_lane = q0 + lax.broadcasted_iota(jnp.int32, (1, R * T), 1) % T

    with_ones = lambda v_t: jnp.concatenate([v_t, jnp.ones_like(v_t)], axis=0)
    gates_t = _sigmoid(gp_ref[0] + bg_ref[...]).T
    gvec = lambda j: jnp.concatenate([gates_t[3 * r + j:3 * r + j + 1, :] for r in range(R)], axis=1)

    grp = T // CMP_STRIDE
    s_scr[0:n_pad, :] = _dot(kc_ref[0, 0], q4) + cfar_ref[0]
    s_scr[n_pad:n_pad + 2 * grp, :] = jnp.zeros((2 * grp, R * T), F32)
    r0 = jnp.maximum(qi * grp - 2 * grp, 0)
    x0 = r0 - (qi * grp - 2 * grp)
    r0 = pl.multiple_of(r0, SUBLANES)
    x0 = pl.multiple_of(x0, SUBLANES)
    s_scr[pl.ds(r0, band_rows), :] += band_ref[0, 0, pl.ds(x0, band_rows), :]
    lim = pl.multiple_of(qi * grp + 2 * grp, SUBLANES)
    s_scr[pl.ds(lim, n_pad), :] = jnp.full((n_pad, R * T), NEG, F32)

    w_subs, w_vals = [], []
    for d in range(n_win + 1):
        kt = jnp.maximum(qi - d, 0)
        off = pl.multiple_of(kt * T, T)
        tile = jnp.where(qi >= d, d, n_win + 1)
        w_subs.append(_dot(kw_ref[0, 0, pl.ds(off, T), :], q4) + winb_ref[tile, 0])
        w_vals.append(with_ones(vwt_ref[0, 0, kt]))

    s = s_scr[0:n_pad, :]
    e = jnp.exp2(s - jnp.max(s, axis=0, keepdims=True))
    inv = jnp.where(t_lane >= CMP_BLOCK - 1, 1.0 / jnp.sum(e, axis=0, keepdims=True), 0.0)
    p = e * inv
    o_cmp = _dot(vct_ref[0, 0], p.astype(BF16))
    psum = functools.reduce(lambda a, b: a + b, [p[:, r * T:(r + 1) * T] for r in range(R)])

    m_w = jnp.max(functools.reduce(jnp.maximum, w_subs), axis=0, keepdims=True)
    acc = functools.reduce(lambda a, b: a + b,
                           [_dot(vj, jnp.exp2(sj - m_w).astype(BF16)) for sj, vj in zip(w_subs, w_vals)])
    o_win = acc[:DH] * (1.0 / acc[DH:DH + 1])
    out_t = gvec(0) * o_cmp + gvec(2) * o_win

    imp_t = _dot(ovt_ref[...], psum, precision=HIGHEST)
    jj = lax.broadcasted_iota(jnp.int32, (n_sel, T), 0)
    blk_t = (q0 + lax.broadcasted_iota(jnp.int32, (1, T), 1)) // SEL_BLOCK
    forced = (jj == 0) | (jj == blk_t) | (jj == blk_t - 1)
    score = jnp.where(forced, FORCE, jnp.where(jj <= blk_t, imp_t, -1.0))
    n_blk = n_sel // SUBLANES
    rows = [score[v * SUBLANES:(v + 1) * SUBLANES] for v in range(n_blk)]
    cnts = [jnp.zeros((SUBLANES, T), F32) for _ in range(n_blk)]
    sub = lax.broadcasted_iota(jnp.int32, (SUBLANES, T), 0)
    for j2 in range(n_sel):
        c2 = score[j2:j2 + 1, :]
        for v in range(n_blk):
            lo = v * SUBLANES
            if lo > j2:
                beats = c2 >= rows[v]
            elif lo + SUBLANES - 1 <= j2:
                beats = c2 > rows[v]
            else:
                beats = (c2 > rows[v]) | ((c2 >= rows[v]) & (sub > j2 - lo))
            cnts[v] = cnts[v] + jnp.where(beats, 1.0, 0.0)
    cnt = jnp.concatenate(cnts, axis=0)
    chosen = (cnt < float(min(SEL_TOPK, n_sel))) & (jj <= blk_t)
    sel_scr[...] = jnp.where(chosen, 0.0, -BIG)

    def block_mask(kt):
        per_tile = T // SEL_BLOCK
        parts = [jnp.broadcast_to(sel_scr[pl.ds(kt * per_tile + i, 1), :], (SEL_BLOCK, T)) for i in range(per_tile)]
        m1 = jnp.concatenate(parts, axis=0)
        return jnp.concatenate([m1] * R, axis=1)

    def sel_scores(slot, kc):
        off = pl.multiple_of(kc * (CH * T), CH * T)
        s = _dot(ks_ref[0, 0, pl.ds(off, CH * T), :], q4)
        subs = []
        for j in range(CH):
            kt = kc * CH + j
            d = jnp.clip(qi - kt, 0, n_far)
            subs.append(s[j * T:(j + 1) * T] + selb_ref[d, 0] + block_mask(kt))
        s = jnp.concatenate(subs, axis=0)
        sbuf[slot] = s
        return jnp.max(s, axis=0, keepdims=True)

    def sel_consume(slot, kc, m, acc, m_cur):
        m_new = jnp.maximum(m, m_cur)
        pb = jnp.exp2(sbuf[slot] - m_new).astype(BF16)
        return m_new, jnp.exp2(m - m_new) * acc + _dot(with_ones(vst_ref[0, 0, kc]), pb)

    last_chunk = vst_ref.shape[2] - 1

    def sel_body(i, carry):
        m, acc, m_even = carry
        m_odd = sel_scores(1, 2 * i + 1)
        m, acc = sel_consume(0, 2 * i, m, acc, m_even)
        m_even = sel_scores(0, jnp.minimum(2 * i + 2, last_chunk))
        m, acc = sel_consume(1, 2 * i + 1, m, acc, m_odd)
        return m, acc, m_even

    n_chunks = qi // CH + 1
    _, acc, _ = lax.fori_loop(0, (n_chunks + 1) // 2, sel_body,
                              (jnp.full((1, R * T), NEG, F32), jnp.zeros((2 * DH, R * T), F32), sel_scores(0, 0)))
    out_t = out_t + gvec(1) * (acc[:DH] * (1.0 / acc[DH:DH + 1]))
    for pr in range(R // 2):
        pair = jnp.concatenate([out_t[:, (2 * pr) * T:(2 * pr + 1) * T],
                                out_t[:, (2 * pr + 1) * T:(2 * pr + 2) * T]], axis=0)
        o_ref[0, :, pr * 2 * DH:(pr + 1) * 2 * DH] = pair.T


def _bias_lookup(table, dist):
    idx = _t5_bucket(dist)
    out = jnp.zeros(idx.shape + (table.shape[1],), F32)
    for k in range(table.shape[0]):
        out = out + jnp.where((idx == k)[..., None], table[k], 0.0)
    return out


def _nsa_t_tables(rel_bias, S):
    T, R, KV = ATT_TILE, NSA_R, NSA_KV
    table = rel_bias.astype(F32) * LOG2E
    ii = jnp.arange(T)
    delta = ii[None, :] - ii[:, None]

    def lanes(a):
        a = jnp.moveaxis(a, -1, 0)
        a = a.reshape((KV, R) + a.shape[1:])
        return jnp.moveaxis(a, 1, 2).reshape(KV, a.shape[2], R * a.shape[3])

    def tile(off):
        return lanes(_bias_lookup(table, off * T + delta))

    mask4 = lambda ok: jnp.tile(jnp.where(ok, 0.0, NEG), (1, R))[None]
    n_far = -(-REL_MAX_DIST // T) + 1
    selb = [tile(o) for o in range(n_far + 1)]
    selb[0] = selb[0] + mask4(delta >= 0)
    selb = jnp.stack(selb, axis=0)
    n_win = WINDOW // T
    winb = [tile(o) + mask4((o * T + delta >= 0) & (o * T + delta < WINDOW)) for o in range(n_win + 1)]
    winb.append(jnp.full_like(winb[0], NEG))
    winb = jnp.stack(winb, axis=0)

    grp = T // CMP_STRIDE
    far = _bias_lookup(table, jnp.asarray(2 * REL_MAX_DIST))
    xx = jnp.arange(4 * grp)
    bdist = ii[None, :] - CMP_STRIDE * (xx[:, None] - 2 * grp) - (CMP_BLOCK - 1)
    band = jnp.where((bdist >= 0)[..., None], _bias_lookup(table, bdist) - far, NEG)
    band = jnp.concatenate([lanes(band), jnp.zeros((KV, 2 * grp, R * T), F32)], axis=1)[:, None]
    cfar = jnp.repeat(far.reshape(KV, R), T, axis=1)[:, None]

    n_pad = S // CMP_STRIDE
    n_sel = S // SEL_BLOCK
    cmp_start = jnp.arange(n_pad) * CMP_STRIDE
    sel_start = jnp.arange(n_sel) * SEL_BLOCK
    overlap = jnp.clip(jnp.minimum(cmp_start[:, None] + CMP_BLOCK, sel_start[None] + SEL_BLOCK)
                       - jnp.maximum(cmp_start[:, None], sel_start[None]), 0).astype(F32) / CMP_BLOCK
    n_cmp = (S - CMP_BLOCK) // CMP_STRIDE + 1
    overlap_t = jnp.where((jnp.arange(n_pad) < n_cmp)[:, None], overlap, 0.0).T
    return cfar, band, selb, winb, overlap_t


def _nsa_t_attention(q, gp, bg, kcmp, vcmp_t, ks, vs_t, kw, vw_t, tables):
    B, S, D = q.shape
    T = ATT_TILE
    cfar, band, selb, winb, overlap_t = tables
    gw = NSA_R * NSA_DH
    n_pad = kcmp.shape[2]
    seq = lambda a: pl.BlockSpec((1, 1) + a.shape[2:], lambda b, g, i: (b, g) + (0,) * (a.ndim - 2))
    grp = lambda a: pl.BlockSpec((1,) + a.shape[1:], lambda b, g, i: (g,) + (0,) * (a.ndim - 1))
    tiles = lambda a: pl.BlockSpec((a.shape[0], 1) + a.shape[2:], lambda b, g, i: (0, g, 0, 0))
    full = lambda a: pl.BlockSpec(a.shape, lambda b, g, i: (0,) * a.ndim)
    return pl.pallas_call(
        _nsa_t_kernel,
        out_shape=jax.ShapeDtypeStruct((B, S, D), F32),
        grid=(B, NSA_KV, S // T),
        in_specs=[pl.BlockSpec((1, T, gw), lambda b, g, i: (b, i, g)),
                  pl.BlockSpec((1, T, LANES), lambda b, g, i: (b, i, g)),
                  pl.BlockSpec((1, LANES), lambda b, g, i: (0, g)),
                  seq(kcmp), seq(vcmp_t), seq(ks), seq(vs_t), seq(kw), seq(vw_t),
                  grp(cfar), grp(band), tiles(selb), tiles(winb), full(overlap_t)],
        out_specs=pl.BlockSpec((1, T, gw), lambda b, g, i: (b, i, g)),
        scratch_shapes=[pltpu.VMEM((2 * n_pad + 2 * (T // CMP_STRIDE), NSA_R * T), F32),
                        pltpu.VMEM((S // SEL_BLOCK, T), F32),
                        pltpu.VMEM((2, vs_t.shape[4], NSA_R * T), F32)],
        compiler_params=_cparams(("parallel", "parallel", "arbitrary")),
        name="nsa_attention",
    )(q, gp, bg, kcmp, vcmp_t, ks, vs_t, kw, vw_t, cfar, band, selb, winb, overlap_t)


def _moe_kernel(x_ref, g_ref, sh_ref, sc_ref, gate_ref, wr_ref, br_ref, wg_ref, wu_ref, wd_ref, fg_ref,
                o_ref, hb_scr, rt_scr, acc_scr, *, final):
    NG, PG, FH = MOE_GROUPS, MOE_PER_GROUP, MOE_HIDDEN
    c = pl.program_id(2)

    @pl.when(c == 0)
    def _():
        h = _modulated_norm(x_ref[0], g_ref[...], sh_ref[0], sc_ref[0])
        hb_scr[...] = h.astype(BF16)
        logits = _dot(h, wr_ref[...], precision=HIGHEST) + br_ref[...]
        gl = [logits[:, NG * PG + g:NG * PG + g + 1] for g in range(NG)]
        gmax = functools.reduce(jnp.maximum, gl)
        gtop = jnp.full_like(gmax, float(NG - 1))
        for g in reversed(range(NG - 1)):
            gtop = jnp.where(gl[g] == gmax, float(g), gtop)
        p_g = 1.0 / functools.reduce(lambda a, b: a + b, [jnp.exp(v - gmax) for v in gl])
        a = []
        for j in range(PG):
            v = logits[:, (NG - 1) * PG + j:(NG - 1) * PG + j + 1]
            for g in reversed(range(NG - 1)):
                v = jnp.where(gtop == float(g), logits[:, g * PG + j:g * PG + j + 1], v)
            a.append(v)

        def first_max(vals):
            vmax = functools.reduce(jnp.maximum, vals)
            taken = jnp.zeros_like(vmax) > 1.0
            hits = []
            for v in vals:
                hit = (v == vmax) & jnp.logical_not(taken)
                taken = taken | hit
                hits.append(hit)
            return vmax, hits

        v1, hit1 = first_max(a)
        rest = [jnp.where(hh, -jnp.inf, v) for hh, v in zip(hit1, a)]
        v2, hit2 = first_max(rest)
        e2 = jnp.exp(v2 - v1)
        w1 = p_g / (1.0 + e2)
        w2 = p_g * e2 / (1.0 + e2)
        lane = lax.broadcasted_iota(jnp.int32, rt_scr.shape, 1)
        rt = jnp.where(lane == PG, gtop, 0.0)
        for j in range(PG):
            wj = jnp.where(hit1[j], w1, jnp.where(hit2[j], w2, 0.0))
            rt = jnp.where(lane == j, wj, rt)
        rt_scr[...] = rt

    hb = hb_scr[...]
    hid = _silu(_dot(hb, wg_ref[0])) * _dot(hb, wu_ref[0])
    rt = rt_scr[...]
    in_group = rt[:, PG:PG + 1] == c.astype(F32)
    parts = [hid[:, j * FH:(j + 1) * FH] * jnp.where(in_group, rt[:, j:j + 1], 0.0) for j in range(PG)]
    contrib = _dot(jnp.concatenate(parts, axis=1).astype(BF16), wd_ref[0])

    @pl.when(c == 0)
    def _():
        acc_scr[...] = contrib

    @pl.when(c > 0)
    def _():
        acc_scr[...] += contrib

    @pl.when(c == NG - 1)
    def _():
        y = x_ref[0] + gate_ref[0] * acc_scr[...]
        if final:
            y = y * lax.rsqrt(jnp.mean(y * y, axis=-1, keepdims=True) + EPS) * fg_ref[...]
        o_ref[0] = y


def _moe(x, g, shift, scale, gate, wg, bg, we, be, w_gate, w_up, w_down, final_g, final, tm=512):
    B, S, D = x.shape
    NG, PG, FH = MOE_GROUPS, MOE_PER_GROUP, MOE_HIDDEN
    wr = jnp.zeros((D, LANES), F32)
    wr = wr.at[:, :NG * PG].set(we.reshape(D, NG * PG).astype(F32)).at[:, NG * PG:NG * PG + NG].set(wg.astype(F32))
    br = jnp.zeros((1, LANES), F32)
    br = br.at[0, :NG * PG].set(be.reshape(NG * PG).astype(F32)).at[0, NG * PG:NG * PG + NG].set(bg.astype(F32))
    grp = lambda w: w.reshape(NG, PG, D, FH).transpose(0, 2, 1, 3).reshape(NG, D, PG * FH).astype(BF16)
    wd = w_down.reshape(NG, PG * FH, D).astype(BF16)
    vec = pl.BlockSpec((1, 1, D), lambda b, i, c: (b, 0, 0))
    row = pl.BlockSpec((1, D), lambda b, i, c: (0, 0))
    wspec = lambda k, n: pl.BlockSpec((1, k, n), lambda b, i, c: (c, 0, 0))
    return pl.pallas_call(
        functools.partial(_moe_kernel, final=final),
        out_shape=jax.ShapeDtypeStruct((B, S, D), F32),
        grid=(B, S // tm, NG),
        in_specs=[pl.BlockSpec((1, tm, D), lambda b, i, c: (b, i, 0)), row, vec, vec, vec,
                  pl.BlockSpec((D, LANES), lambda b, i, c: (0, 0)),
                  pl.BlockSpec((1, LANES), lambda b, i, c: (0, 0)),
                  wspec(D, PG * FH), wspec(D, PG * FH), wspec(PG * FH, D), row],
        out_specs=pl.BlockSpec((1, tm, D), lambda b, i, c: (b, i, 0)),
        scratch_shapes=[pltpu.VMEM((tm, D), BF16), pltpu.VMEM((tm, LANES), F32), pltpu.VMEM((tm, D), F32)],
        compiler_params=_cparams(("parallel", "parallel", "arbitrary")),
        name="moe",
    )(x, g.reshape(1, D), shift, scale, gate, wr, br, grp(w_gate), grp(w_up), wd, final_g.reshape(1, D))


def _mlstm_s5_layer(x, g, shift, scale, gate, w_in, conv_w, b_i, b_f, head_g, s5_params, w_out):
    H = MLSTM_HEADS
    A = MIX_A
    w_if = jnp.zeros((D_MODEL, LANES), F32).at[:, :2 * H].set(w_in[:, 4 * A:4 * A + 2 * H])
    weights = [w_in[:, :2 * A], w_in[:, 2 * A:4 * A], w_if, w_in[:, 4 * A + 2 * H:]]
    qk, vo, ifg, u = _norm_matmul(x, g, shift, scale, [w.astype(BF16) for w in weights], [F32, F32, F32, BF16])
    gate_bias = jnp.zeros((1, LANES), F32).at[0, :H].set(b_i.astype(F32)).at[0, H:2 * H].set(b_f.astype(F32))
    hm = _mlstm(qk, vo, ifg, conv_w.astype(F32), gate_bias, head_g.reshape(1, A).astype(F32))
    ys = _s5(u, _s5_tables(*s5_params))
    w_out = w_out.astype(BF16)
    return _out_residual(x, gate, [hm, ys], [w_out[:A], w_out[A:]])


def _nsa_layer(x, g, shift, scale, gate, w_in, b_gate, cmp_pos, cmp_w1, cmp_b1, cmp_w2, cmp_b2, rel_bias, w_out):
    B, S, D = x.shape
    KV, R, DH = NSA_KV, NSA_R, NSA_DH
    w_g = jnp.zeros((D, KV, LANES), F32).at[:, :, :3 * R].set(w_in[:, D + 6 * KV_W:].reshape(D, KV, 3 * R))
    b_g = jnp.zeros((KV, LANES), F32).at[:, :3 * R].set(b_gate.reshape(KV, 3 * R).astype(F32))
    kv_cols = lambda i: w_in[:, D + i * KV_W:D + (i + 1) * KV_W]
    w_k = jnp.concatenate([kv_cols(0), kv_cols(2), kv_cols(4)], axis=1)
    w_v = jnp.concatenate([kv_cols(1), kv_cols(3), kv_cols(5)], axis=1)
    weights = [w_in[:, :D], w_k, w_v, w_g.reshape(D, KV * LANES)]
    q, gp, kc, vc, ks, kw, vs_t, vw_t = _nsa_proj(x, g, shift, scale, [w.astype(BF16) for w in weights])
    grp = CMP_STRIDE
    xg = jnp.stack([kc, vc]).reshape(2, B, KV * S // grp, grp * DH)
    cmp = _compress(xg, cmp_pos, cmp_w1, cmp_b1, cmp_w2, cmp_b2).reshape(2, B, KV, S // grp, DH).astype(BF16)
    out = _nsa_t_attention(q, gp, b_g.reshape(1, KV * LANES), cmp[0], cmp[1].transpose(0, 1, 3, 2),
                           ks, vs_t, kw, vw_t, _nsa_t_tables(rel_bias, S))
    return _out_residual(x, gate, [out], [w_out.astype(BF16)])


def kernel(x, c, rel_bias, ada_w, ada_b, norm_g, final_g,
           a_w_in, a_conv, a_b_i, a_b_f, a_head_g,
           s5_lam_re, s5_lam_im, s5_log_dt, s5_b_re, s5_b_im, s5_c_re, s5_c_im,
           s5_d, s5_glu_w, s5_glu_b, a_w_out,
           n_w_in, n_b_gate, n_cmp_pos, n_cmp_w1, n_cmp_b1, n_cmp_w2, n_cmp_b2, n_w_out,
           r_grp_w, r_grp_b, r_exp_w, r_exp_b, e_w_gate, e_w_up, e_w_down):
    B, S, D = x.shape
    mod = _ada_mod(c, ada_w, ada_b).reshape(DEPTH, 2, B, 1, 3 * D)
    split = lambda m: (m[..., :D], m[..., D:2 * D], m[..., 2 * D:])
    for layer in range(DEPTH):
        shift, scale, gate = split(mod[layer, 0])
        j = layer // 2
        if layer % 2 == 0:
            s5_params = (s5_lam_re[j], s5_lam_im[j], s5_log_dt[j], s5_b_re[j], s5_b_im[j],
                         s5_c_re[j], s5_c_im[j], s5_d[j], s5_glu_w[j], s5_glu_b[j])
            x = _mlstm_s5_layer(x, norm_g[layer, 0], shift, scale, gate, a_w_in[j], a_conv[j], a_b_i[j], a_b_f[j],
                                a_head_g[j], s5_params, a_w_out[j])
        else:
            x = _nsa_layer(x, norm_g[layer, 0], shift, scale, gate, n_w_in[j], n_b_gate[j], n_cmp_pos[j],
                           n_cmp_w1[j], n_cmp_b1[j], n_cmp_w2[j], n_cmp_b2[j], rel_bias, n_w_out[j])
        shift, scale, gate = split(mod[layer, 1])
        x = _moe(x, norm_g[layer, 1], shift, scale, gate, r_grp_w[layer], r_grp_b[layer], r_exp_w[layer],
                 r_exp_b[layer], e_w_gate[layer], e_w_up[layer], e_w_down[layer], final_g,
                 final=(layer == DEPTH - 1))
    return x
```

```python
import functools
import math

import jax
import jax.numpy as jnp
from jax import lax
from jax.experimental import pallas as pl
from jax.experimental.pallas import tpu as pltpu

F32 = jnp.float32
BF16 = jnp.bfloat16
HIGHEST = lax.Precision.HIGHEST

D_MODEL = 1024
DEPTH = 2
MIX_A = 512
MLSTM_HEADS = 4
MLSTM_DH = MIX_A // MLSTM_HEADS
MLSTM_CHUNK = 128
CONV_K = 4
MIX_B = D_MODEL - MIX_A
S5_GROUP = 16
S5_GROUPS = MIX_B // S5_GROUP
S5_STATE = 64
S5_CHUNK = 16
NSA_HEADS = 16
NSA_KV = 4
NSA_R = NSA_HEADS // NSA_KV
NSA_DH = D_MODEL // NSA_HEADS
KV_W = NSA_KV * NSA_DH
CMP_BLOCK = 32
CMP_STRIDE = 16
CMP_HIDDEN = 256
SEL_BLOCK = 64
SEL_TOPK = 16
WINDOW = 512
FORCE = 1e9
REL_BUCKETS = 32
REL_MAX_DIST = 128
MOE_GROUPS = 4
MOE_PER_GROUP = 4
MOE_HIDDEN = 256
EPS = 1e-6
NEG = -1e30
BIG = 1e30
LOG2E = math.log2(math.e)
SEL_CHUNK = 2
SEL_UNROLL = 2

LANES = 128
SUBLANES = 8
ATT_TILE = 128
VMEM_LIMIT = 56 * 1024 * 1024


def _cparams(sem):
    return pltpu.CompilerParams(dimension_semantics=sem, vmem_limit_bytes=VMEM_LIMIT)


def _dot(a, b, precision=None):
    return jnp.dot(a, b, preferred_element_type=F32, precision=precision)


def _dot_nt(a, b):
    return lax.dot_general(a, b, (((1,), (1,)), ((), ())), preferred_element_type=F32)


def _sigmoid(x):
    return 1.0 / (1.0 + jnp.exp(-x))


def _silu(x):
    return x * _sigmoid(x)


def _gelu_tanh(x):
    return 0.5 * x * (1.0 + jnp.tanh(math.sqrt(2.0 / math.pi) * (x + 0.044715 * (x * x * x))))


def _modulated_norm(x, g, shift, scale):
    y = x * lax.rsqrt(jnp.mean(x * x, axis=-1, keepdims=True) + EPS) * g
    return y * (1.0 + scale) + shift


def _ada_kernel(c_ref, w_ref, b_ref, o_ref):
    c = c_ref[...]
    o_ref[0] = _dot(_silu(c), w_ref[0]) + b_ref[0]


def _ada_mod(c, ada_w, ada_b):
    B, D = c.shape
    n_mod = ada_w.shape[0] * ada_w.shape[1]
    w = ada_w.reshape(n_mod, D, 3 * D)
    b = ada_b.reshape(n_mod, 1, 3 * D)
    tn = 1024
    return pl.pallas_call(
        _ada_kernel,
        out_shape=jax.ShapeDtypeStruct((n_mod, B, 3 * D), F32),
        grid=(n_mod, 3 * D // tn),
        in_specs=[pl.BlockSpec((B, D), lambda i, j: (0, 0)),
                  pl.BlockSpec((1, D, tn), lambda i, j: (i, 0, j)),
                  pl.BlockSpec((1, 1, tn), lambda i, j: (i, 0, j))],
        out_specs=pl.BlockSpec((1, B, tn), lambda i, j: (i, 0, j)),
        compiler_params=_cparams(("parallel", "parallel")),
        name="ada_mod",
    )(c, w, b)


def _norm_mm_kernel(*refs, n_w):
    x_ref, g_ref, sh_ref, sc_ref = refs[:4]
    w_refs = refs[4:4 + n_w]
    o_refs = refs[4 + n_w:]
    h = _modulated_norm(x_ref[0], g_ref[...], sh_ref[0], sc_ref[0]).astype(BF16)
    for w_ref, o_ref in zip(w_refs, o_refs):
        o_ref[0] = _dot(h, w_ref[...]).astype(o_ref.dtype)


def _norm_matmul(x, g, shift, scale, weights, out_dtypes, tm=512):
    B, S, D = x.shape
    n_w = len(weights)
    vec = pl.BlockSpec((1, 1, D), lambda b, i: (b, 0, 0))
    in_specs = [pl.BlockSpec((1, tm, D), lambda b, i: (b, i, 0)),
                pl.BlockSpec((1, D), lambda b, i: (0, 0)), vec, vec]
    in_specs += [pl.BlockSpec(w.shape, lambda b, i: (0, 0)) for w in weights]
    return pl.pallas_call(
        functools.partial(_norm_mm_kernel, n_w=n_w),
        out_shape=[jax.ShapeDtypeStruct((B, S, w.shape[1]), dt) for w, dt in zip(weights, out_dtypes)],
        grid=(B, S // tm),
        in_specs=in_specs,
        out_specs=[pl.BlockSpec((1, tm, w.shape[1]), lambda b, i: (b, i, 0)) for w in weights],
        compiler_params=_cparams(("parallel", "parallel")),
        name="norm_matmul",
    )(x, g.reshape(1, D), shift, scale, *weights)


def _out_res_kernel(*refs, n_in):
    x_ref, gate_ref = refs[:2]
    a_refs = refs[2:2 + n_in]
    w_refs = refs[2 + n_in:2 + 2 * n_in]
    o_ref = refs[2 + 2 * n_in]
    acc = None
    for a_ref, w_ref in zip(a_refs, w_refs):
        t = _dot(a_ref[0].astype(BF16), w_ref[...])
        acc = t if acc is None else acc + t
    o_ref[0] = x_ref[0] + gate_ref[0] * acc


def _out_residual(x, gate, acts, weights, tm=512):
    B, S, D = x.shape
    n_in = len(acts)
    in_specs = [pl.BlockSpec((1, tm, D), lambda b, i: (b, i, 0)),
                pl.BlockSpec((1, 1, D), lambda b, i: (b, 0, 0))]
    in_specs += [pl.BlockSpec((1, tm, a.shape[2]), lambda b, i: (b, i, 0)) for a in acts]
    in_specs += [pl.BlockSpec(w.shape, lambda b, i: (0, 0)) for w in weights]
    return pl.pallas_call(
        functools.partial(_out_res_kernel, n_in=n_in),
        out_shape=jax.ShapeDtypeStruct((B, S, D), F32),
        grid=(B, S // tm),
        in_specs=in_specs,
        out_specs=pl.BlockSpec((1, tm, D), lambda b, i: (b, i, 0)),
        compiler_params=_cparams(("parallel", "parallel")),
        name="out_residual",
    )(x, gate, *acts, *weights)


def _mlstm_kernel(qk_ref, vo_ref, if_ref, cw_ref, gb_ref, hg_ref, tril_ref, o_ref,
                  xbuf, c_scr, n_scr, m_scr):
    L, H, DH = MLSTM_CHUNK, MLSTM_HEADS, MLSTM_DH
    pad = SUBLANES

    @pl.when(pl.program_id(1) == 0)
    def _():
        xbuf[0:pad, :] = jnp.zeros((pad, 2 * MIX_A), F32)
        c_scr[...] = jnp.zeros_like(c_scr)
        n_scr[...] = jnp.zeros_like(n_scr)
        m_scr[...] = jnp.zeros_like(m_scr)

    xbuf[pad:pad + L, :] = qk_ref[0]
    cw = cw_ref[...]
    conv = None
    for j in range(CONV_K):
        lo = pad - (CONV_K - 1) + j
        t = xbuf[lo:lo + L, :] * cw[j:j + 1, :]
        conv = t if conv is None else conv + t
    xbuf[0:pad, :] = xbuf[L:L + pad, :]
    qk = _silu(conv)
    q = qk[:, :MIX_A]
    k = qk[:, MIX_A:] * (DH ** -0.5)
    vo = vo_ref[0]
    v = vo[:, :MIX_A]
    o_pre = vo[:, MIX_A:]

    ifb = if_ref[0] + gb_ref[...]
    lf = jnp.minimum(ifb, 0.0) - jnp.log1p(jnp.exp(-jnp.abs(ifb)))
    bcs = _dot(tril_ref[...], lf, precision=HIGHEST)
    ifb_t = ifb.T
    bcs_t = bcs.T
    row = lax.broadcasted_iota(jnp.int32, (L, L), 0)
    col = lax.broadcasted_iota(jnp.int32, (L, L), 1)
    causal = col <= row

    outs = []
    for h in range(H):
        sl = slice(h * DH, (h + 1) * DH)
        qh, kh, vh = q[:, sl], k[:, sl], v[:, sl]
        qb, kb = qh.astype(BF16), kh.astype(BF16)
        b_col = bcs[:, H + h:H + h + 1]
        b_row = bcs_t[H + h:H + h + 1, :]
        li_col = ifb[:, h:h + 1]
        li_row = ifb_t[h:h + 1, :]
        b_last = b_col[L - 1:L, :]
        m0 = m_scr[h][:, 0:1]
        c0 = c_scr[h]
        n0 = n_scr[h]

        log_d = jnp.where(causal, b_col - b_row + li_row, NEG)
        log_inter = b_col + m0
        m_t = jnp.maximum(log_inter, jnp.max(log_d, axis=1, keepdims=True))
        dmat = jnp.exp(log_d - m_t)
        a_inter = jnp.exp(log_inter - m_t)
        s = _dot_nt(qb, kb) * dmat
        num = _dot(s.astype(BF16), vh.astype(BF16)) + a_inter * _dot_nt(qb, c0.astype(BF16))
        den = jnp.sum(s, axis=1, keepdims=True) + a_inter * jnp.sum(qh * n0, axis=1, keepdims=True)
        hh = num / jnp.maximum(jnp.abs(den), jnp.exp(-m_t))

        w_col = b_last - b_col + li_col
        m_loc = jnp.max(w_col, axis=0, keepdims=True)
        e = jnp.exp(w_col - m_loc)
        c_loc = _dot((vh * e).T.astype(BF16), kb)
        n_loc = jnp.sum(kh * e, axis=0, keepdims=True)
        m_new = jnp.maximum(b_last + m0, m_loc)
        a = jnp.exp(b_last + m0 - m_new)
        sc = jnp.exp(m_loc - m_new)
        c_scr[h] = a * c0 + sc * c_loc
        n_scr[h] = a * n0 + sc * n_loc
        m_scr[h] = jnp.broadcast_to(m_new, (1, LANES))

        outs.append(hh * lax.rsqrt(jnp.mean(hh * hh, axis=1, keepdims=True) + EPS))
    hm = jnp.concatenate(outs, axis=1)
    o_ref[0] = _sigmoid(o_pre) * (hm * hg_ref[...])


def _mlstm(qk, vo, ifg, conv_w, gate_bias, head_g):
    B, S, _ = qk.shape
    L, H, DH = MLSTM_CHUNK, MLSTM_HEADS, MLSTM_DH
    tril = jnp.tril(jnp.ones((L, L), F32))
    return pl.pallas_call(
        _mlstm_kernel,
        out_shape=jax.ShapeDtypeStruct((B, S, MIX_A), F32),
        grid=(B, S // L),
        in_specs=[pl.BlockSpec((1, L, 2 * MIX_A), lambda b, c: (b, c, 0)),
                  pl.BlockSpec((1, L, 2 * MIX_A), lambda b, c: (b, c, 0)),
                  pl.BlockSpec((1, L, LANES), lambda b, c: (b, c, 0)),
                  pl.BlockSpec((CONV_K, 2 * MIX_A), lambda b, c: (0, 0)),
                  pl.BlockSpec((1, LANES), lambda b, c: (0, 0)),
                  pl.BlockSpec((1, MIX_A), lambda b, c: (0, 0)),
                  pl.BlockSpec((L, L), lambda b, c: (0, 0))],
        out_specs=pl.BlockSpec((1, L, MIX_A), lambda b, c: (b, c, 0)),
        scratch_shapes=[pltpu.VMEM((L + SUBLANES, 2 * MIX_A), F32),
                        pltpu.VMEM((H, DH, DH), F32),
                        pltpu.VMEM((H, 1, DH), F32),
                        pltpu.VMEM((H, 1, LANES), F32)],
        compiler_params=_cparams(("parallel", "arbitrary")),
        name="mlstm",
    )(qk, vo, ifg, conv_w, gate_bias, head_g, tril)


def _s5_kernel(u_ref, m_ref, hre_ref, him_ref, ere_ref, eim_ref, are_ref, aim_ref, d_ref, gw_ref, gb_ref, o_ref,
               xl_re, xl_im, x0_re, x0_im, *, n_chunks, batch):
    u = u_ref[0]
    xl_re[...] = _dot(u, hre_ref[0])
    xl_im[...] = _dot(u, him_ref[0])
    a_re = are_ref[0]
    a_im = aim_ref[0]

    def body(i, carry):
        re, im = carry
        r = pl.multiple_of(i * batch, batch)
        x0_re[pl.ds(r, batch), :] = re
        x0_im[pl.ds(r, batch), :] = im
        return (a_re * re - a_im * im + xl_re[pl.ds(r, batch), :],
                a_re * im + a_im * re + xl_im[pl.ds(r, batch), :])

    zero = jnp.zeros((batch, S5_STATE), F32)
    lax.fori_loop(0, n_chunks, body, (zero, zero), unroll=8)
    y = (_dot(u, m_ref[0]) + _dot(x0_re[...].astype(BF16), ere_ref[0]) + _dot(x0_im[...].astype(BF16), eim_ref[0])
         + u.astype(F32) * d_ref[0])
    ys = _gelu_tanh(y)
    z = _dot(ys.astype(BF16), gw_ref[0]) + gb_ref[0]
    o_ref[0] = (ys * _sigmoid(z)).astype(o_ref.dtype)


def _s5_tables(lam_re, lam_im, log_dt, b_re, b_im, c_re, c_im, d_skip, glu_w, glu_b):
    T, C, P = S5_CHUNK, S5_GROUP, S5_STATE
    G = lam_re.shape[0]
    lam = lax.complex(lam_re.astype(F32), lam_im.astype(F32))
    dt = jnp.exp(log_dt.astype(F32))[:, None]
    lam_bar = jnp.exp(lam * dt)
    b_bar = ((lam_bar - 1.0) / lam)[..., None] * lax.complex(b_re.astype(F32), b_im.astype(F32))
    c_mat = lax.complex(c_re.astype(F32), c_im.astype(F32))
    taus = jnp.arange(T + 1, dtype=F32)
    pw = jnp.exp((lam * dt)[:, None, :] * taus[None, :, None])
    kern = jnp.einsum('gcp,gtp,gpd->gtcd', c_mat, pw[:, :T], b_bar,
                      precision=HIGHEST).real
    tt = jnp.arange(T)
    shift = (tt[:, None, None] - tt[None, :, None] == tt[None, None, :]).astype(F32)
    toe = jnp.einsum('tsu,gucd->gtscd', shift, kern, precision=HIGHEST)
    m_t = toe.transpose(0, 2, 4, 1, 3).reshape(G, T * C, T * C).astype(BF16)
    hmat = (pw[:, :T][:, ::-1, :, None] * b_bar[:, None]).transpose(0, 1, 3, 2).reshape(G, T * C, P)
    emat = (c_mat[:, None] * pw[:, 1:][:, :, None, :]).reshape(G, T * C, P).transpose(0, 2, 1)
    a_re, a_im = pw[:, T].real, pw[:, T].imag
    rows8 = lambda a: jnp.broadcast_to(a[:, None], (G, SUBLANES, P))
    d_t = jnp.tile(d_skip.astype(F32), (1, T))[:, None]
    eye = jnp.eye(T, dtype=F32)
    gw = jnp.einsum('ts,gce->gtcse', eye, glu_w.astype(F32)).reshape(G, T * C, T * C).astype(BF16)
    gb = jnp.tile(glu_b.astype(F32), (1, T))[:, None]
    return (m_t, hmat.real.astype(BF16), hmat.imag.astype(BF16), emat.real.astype(BF16), (-emat.imag).astype(BF16),
            rows8(a_re), rows8(a_im), d_t, gw, gb)


def _s5(u, tables):
    B, S, _ = u.shape
    T, C, P, G = S5_CHUNK, S5_GROUP, S5_STATE, S5_GROUPS
    assert B == SUBLANES
    n_chunks = S // T
    rows = n_chunks * B
    ug = u.reshape(B, n_chunks, T, G, C).transpose(3, 1, 0, 2, 4).reshape(G, rows, T * C)
    per_g = lambda a: pl.BlockSpec((1,) + a.shape[1:], lambda g: (g, 0, 0))
    out = pl.pallas_call(
        functools.partial(_s5_kernel, n_chunks=n_chunks, batch=B),
        out_shape=jax.ShapeDtypeStruct((G, rows, T * C), BF16),
        grid=(G,),
        in_specs=[per_g(ug)] + [per_g(t) for t in tables],
        out_specs=per_g(ug),
        scratch_shapes=[pltpu.VMEM((rows, P), F32) for _ in range(4)],
        compiler_params=_cparams(("parallel",)),
        name="s5",
    )(ug, *tables)
    return out.reshape(G, n_chunks, B, T, C).transpose(2, 1, 3, 0, 4).reshape(B, S, G * C)


S5_LT = LANES // S5_GROUP
S5_PAIRS = S5_CHUNK // 2


def _s5s_kernel(u_ref, h_ref, e_ref, kk_ref, are_ref, aim_ref, d_ref, gw_ref, gb_ref, o_ref, xl_scr, x0_scr):
    n_chunks = u_ref.shape[1] // S5_CHUNK
    half = S5_LT * S5_STATE
    tok = lambda s: u_ref[0, pl.ds(s, n_chunks, stride=S5_CHUNK), :]
    u2 = [jnp.concatenate([tok(2 * q), tok(2 * q + 1)], axis=1) for q in range(S5_PAIRS)]
    u2b = [v.astype(BF16) for v in u2]
    xl_scr[...] = functools.reduce(lambda a, b: a + b, [_dot(u2b[q], h_ref[0, q]) for q in range(S5_PAIRS)])
    a_re = are_ref[0]
    a_im = aim_ref[0]

    def body(a, carry):
        re, im = carry
        x0_scr[pl.ds(a, 1), 0:half] = re
        x0_scr[pl.ds(a, 1), half:2 * half] = im
        return (a_re * re - a_im * im + xl_scr[pl.ds(a, 1), 0:half],
                a_re * im + a_im * re + xl_scr[pl.ds(a, 1), half:2 * half])

    zero = jnp.zeros((1, half), F32)
    lax.fori_loop(0, n_chunks, body, (zero, zero), unroll=8)
    x0 = x0_scr[...].astype(BF16)
    for p in range(S5_PAIRS):
        y = _dot(x0, e_ref[0, p]) + u2[p] * d_ref[0]
        for q in range(p + 1):
            y = y + _dot(u2b[q], kk_ref[0, p - q])
        ys = _gelu_tanh(y)
        out = ys * _sigmoid(_dot(ys.astype(BF16), gw_ref[0]) + gb_ref[0])
        o_ref[0, pl.ds(2 * p, n_chunks, stride=S5_CHUNK), :] = out[:, :LANES].astype(o_ref.dtype)
        o_ref[0, pl.ds(2 * p + 1, n_chunks, stride=S5_CHUNK), :] = out[:, LANES:].astype(o_ref.dtype)


def _s5s_tables(lam_re, lam_im, log_dt, b_re, b_im, c_re, c_im, d_skip, glu_w, glu_b):
    T, C, P, LT = S5_CHUNK, S5_GROUP, S5_STATE, S5_LT
    G = lam_re.shape[0]
    NT = G // LT
    lam = lax.complex(lam_re.astype(F32), lam_im.astype(F32))
    dt = jnp.exp(log_dt.astype(F32))[:, None]
    lam_bar = jnp.exp(lam * dt)
    b_bar = ((lam_bar - 1.0) / lam)[..., None] * lax.complex(b_re.astype(F32), b_im.astype(F32))
    c_mat = lax.complex(c_re.astype(F32), c_im.astype(F32))
    taus = jnp.arange(T + 1, dtype=F32)
    pw = jnp.exp((lam * dt)[:, None, :] * taus[None, :, None])
    eye = jnp.eye(LT, dtype=F32)
    tiles = lambda a: a.reshape((NT, LT) + a.shape[1:])

    kern = jnp.einsum('gcp,gtp,gpd->gtdc', c_mat, pw[:, :T], b_bar, precision=HIGHEST).real
    kblk = jnp.einsum('nitdc,ij->ntidjc', tiles(kern), eye).reshape(NT, T, LANES, LANES)
    kblk = jnp.concatenate([jnp.zeros_like(kblk[:, :1]), kblk], axis=1)
    kk = jnp.stack([jnp.concatenate([jnp.concatenate([kblk[:, 2 * d + 1], kblk[:, 2 * d + 2]], axis=2),
                                     jnp.concatenate([kblk[:, 2 * d], kblk[:, 2 * d + 1]], axis=2)], axis=1)
                    for d in range(T // 2)], axis=1)

    hmat = pw[:, :T][:, ::-1, :, None] * b_bar[:, None]

    def state_cols(m):
        return jnp.einsum('nispc,ij->nsicjp', tiles(m), eye).reshape(NT, T, LANES, LT * P)

    h = jnp.concatenate([state_cols(hmat.real), state_cols(hmat.imag)], axis=3)
    h2 = h.reshape(NT, T // 2, 2 * LANES, 2 * LT * P)

    emat = c_mat[:, None] * pw[:, 1:][:, :, None, :]

    def state_rows(m):
        return jnp.einsum('nitcp,ij->ntjpic', tiles(m), eye).reshape(NT, T, LT * P, LANES)

    e = jnp.concatenate([state_rows(emat.real), state_rows(-emat.imag)], axis=2)
    e2 = e.reshape(NT, T // 2, 2, 2 * LT * P, LANES).transpose(0, 1, 3, 2, 4).reshape(NT, T // 2, 2 * LT * P, 2 * LANES)

    a_re = pw[:, T].real.reshape(NT, 1, LT * P)
    a_im = pw[:, T].imag.reshape(NT, 1, LT * P)
    pair = lambda v: jnp.tile(v.astype(F32).reshape(NT, 1, LANES), (1, 1, 2))
    gwb = jnp.einsum('nice,ij->nicje', tiles(glu_w.astype(F32)), eye).reshape(NT, LANES, LANES)
    zeros = jnp.zeros_like(gwb)
    gw2 = jnp.concatenate([jnp.concatenate([gwb, zeros], axis=2), jnp.concatenate([zeros, gwb], axis=2)], axis=1)
    return (h2.astype(BF16), e2.astype(BF16), kk.astype(BF16), a_re, a_im, pair(d_skip), gw2.astype(BF16), pair(glu_b))


def _s5s(u, tables):
    B, S, W = u.shape
    NT = W // LANES
    n_chunks = S // S5_CHUNK
    per_tile = lambda a: pl.BlockSpec((1,) + a.shape[1:], lambda j, b: (j,) + (0,) * (a.ndim - 1))
    return pl.pallas_call(
        _s5s_kernel,
        out_shape=jax.ShapeDtypeStruct((B, S, W), F32),
        grid=(NT, B),
        in_specs=[pl.BlockSpec((1, S, LANES), lambda j, b: (b, 0, j))] + [per_tile(t) for t in tables],
        out_specs=pl.BlockSpec((1, S, LANES), lambda j, b: (b, 0, j)),
        scratch_shapes=[pltpu.VMEM((n_chunks, 2 * S5_LT * S5_STATE), F32) for _ in range(2)],
        compiler_params=_cparams(("parallel", "parallel")),
        name="s5",
    )(u, *tables)


def _compress_kernel(x_ref, plo_ref, phi_ref, w1_ref, b1_ref, w2_ref, b2_ref, o_ref):
    x = x_ref[0, 0]
    half = x.shape[1]
    w1 = w1_ref[0]
    lo = _dot((x + plo_ref[0]).astype(BF16), w1[:half])
    hi = _dot((x + phi_ref[0]).astype(BF16), w1[half:])
    rows = x.shape[0]
    hid = _gelu_tanh(lo + pltpu.roll(hi, rows - 1, 0) + b1_ref[0])
    o_ref[0, 0] = _dot(hid.astype(BF16), w2_ref[0]) + b2_ref[0]


def _compress(xg, pos, w1, b1, w2, b2):
    _, B, rows, width = xg.shape
    pos_flat = pos.reshape(2, 2, 1, width).astype(F32)
    sel = lambda shape: pl.BlockSpec((1,) + shape, lambda j, b: (j, 0, 0))
    return pl.pallas_call(
        _compress_kernel,
        out_shape=jax.ShapeDtypeStruct((2, B, rows, NSA_DH), F32),
        grid=(2, B),
        in_specs=[pl.BlockSpec((1, 1, rows, width), lambda j, b: (j, b, 0, 0)),
                  sel((1, width)), sel((1, width)),
                  sel((2 * width, CMP_HIDDEN)), sel((1, CMP_HIDDEN)),
                  sel((CMP_HIDDEN, NSA_DH)), sel((1, NSA_DH))],
        out_specs=pl.BlockSpec((1, 1, rows, NSA_DH), lambda j, b: (j, b, 0, 0)),
        compiler_params=_cparams(("parallel", "parallel")),
        name="nsa_compress",
    )(xg, pos_flat[:, 0], pos_flat[:, 1], w1.astype(BF16), b1[:, None].astype(F32),
      w2.astype(BF16), b2[:, None].astype(F32))


def _nsa_kernel(q_ref, gp_ref, bg_ref, kc_ref, vc_ref, ks_ref, vs_ref, kw_ref, vw_ref,
                grev_ref, selb_ref, winb_ref, ovt_ref, ex_ref, o_ref):
    T = ATT_TILE
    R, DH = NSA_R, NSA_DH
    qi = pl.program_id(2)
    q0 = qi * T
    qall = q_ref[0] * (DH ** -0.5 * LOG2E)
    t_col = q0 + lax.broadcasted_iota(jnp.int32, (T, 1), 0)
    n_cmp_pad = kc_ref.shape[2]
    n_sel = ovt_ref.shape[0]
    grp_rows = CMP_STRIDE

    q4 = jnp.concatenate([qall[:, r * DH:(r + 1) * DH] for r in range(R)], axis=0).astype(BF16)

    n_row = lax.broadcasted_iota(jnp.int32, (1, n_cmp_pad), 1)
    cmask = (t_col >= n_row * CMP_STRIDE + (CMP_BLOCK - 1))[None]
    bias = jnp.stack([jnp.concatenate(
        [pltpu.roll(grev_ref[r], (qi * (T // grp_rows) + al + 1) % n_cmp_pad, 1) for al in range(T // grp_rows)],
        axis=0) for r in range(R)], axis=0)
    s = jnp.where(cmask, _dot_nt(q4, kc_ref[0, 0]).reshape(R, T, n_cmp_pad) + bias, NEG)
    p = jnp.exp2(s - jnp.max(s, axis=2, keepdims=True))
    p = p / jnp.sum(p, axis=2, keepdims=True)
    p = jnp.where(cmask, p, 0.0)
    o_cmp = _dot(p.reshape(R * T, n_cmp_pad).astype(BF16), vc_ref[0, 0])
    psum = functools.reduce(lambda a, b: a + b, [p[r] for r in range(R)])

    imp_t = lax.dot_general(ovt_ref[...], psum, (((1,), (1,)), ((), ())), precision=HIGHEST,
                            preferred_element_type=F32)
    jj = lax.broadcasted_iota(jnp.int32, (n_sel, T), 0)
    blk_t = (q0 + lax.broadcasted_iota(jnp.int32, (1, T), 1)) // SEL_BLOCK
    forced = (jj == 0) | (jj == blk_t) | (jj == blk_t - 1)
    score = jnp.where(forced, FORCE, jnp.where(jj <= blk_t, imp_t, -1.0))
    n_blk = n_sel // SUBLANES
    rows = [score[v * SUBLANES:(v + 1) * SUBLANES] for v in range(n_blk)]
    cnts = [jnp.zeros((SUBLANES, T), F32) for _ in range(n_blk)]
    sub = lax.broadcasted_iota(jnp.int32, (SUBLANES, T), 0)
    for j2 in range(n_sel):
        c2 = score[j2:j2 + 1, :]
        for v in range(n_blk):
            lo = v * SUBLANES
            if lo > j2:
                beats = c2 >= rows[v]
            elif lo + SUBLANES - 1 <= j2:
                beats = c2 > rows[v]
            else:
                beats = (c2 > rows[v]) | ((c2 >= rows[v]) & (sub > j2 - lo))
            cnts[v] = cnts[v] + jnp.where(beats, 1.0, 0.0)
    cnt = jnp.concatenate(cnts, axis=0)
    sel_t = jnp.where((cnt < float(min(SEL_TOPK, n_sel))) & (jj <= blk_t), 1.0, 0.0)
    sel_q = jnp.concatenate([sel_t, jnp.zeros((LANES - n_sel, T), F32)], axis=0).T
    lane = lax.broadcasted_iota(jnp.int32, (T, LANES), 1)
    sel_aug = jnp.where(lane == n_sel, 1.0, sel_q).astype(BF16)

    n_far = selb_ref.shape[0] - 1
    n_win = winb_ref.shape[0] - 2
    CH = ex_ref.shape[2] // T

    def sel_body(kc, carry):
        m, acc = carry
        off = pl.multiple_of(kc * (CH * T), CH * T)
        k = ks_ref[0, 0, pl.ds(off, CH * T), :]
        v = vs_ref[0, 0, pl.ds(off, CH * T), :]
        s = _dot_nt(q4, k)
        mask = _dot(sel_aug, ex_ref[kc])
        subs = []
        for j in range(CH):
            d = jnp.clip(qi - (kc * CH + j), 0, n_far)
            sj = s[:, j * T:(j + 1) * T].reshape(R, T, T) + selb_ref[d] + mask[:, j * T:(j + 1) * T][None]
            subs.append(sj.reshape(R * T, T))
        m_new = jnp.maximum(m, jnp.max(functools.reduce(jnp.maximum, subs), axis=1, keepdims=True))
        pb = jnp.concatenate([jnp.exp2(sj - m_new).astype(BF16) for sj in subs], axis=1)
        return m_new, jnp.exp2(m - m_new) * acc + _dot(pb, v)

    _, acc = lax.fori_loop(0, qi // CH + 1, sel_body,
                           (jnp.full((R * T, 1), NEG, F32), jnp.zeros((R * T, LANES), F32)))
    o_sel = acc[:, :DH] / acc[:, DH:DH + 1]

    subs, vals = [], []
    for d in range(n_win + 1):
        off = pl.multiple_of(jnp.maximum(qi - d, 0) * T, T)
        tile = jnp.where(qi >= d, d, n_win + 1)
        subs.append((_dot_nt(q4, kw_ref[0, 0, pl.ds(off, T), :]).reshape(R, T, T) + winb_ref[tile]).reshape(R * T, T))
        vals.append(vw_ref[0, 0, pl.ds(off, T), :])
    m_w = jnp.max(functools.reduce(jnp.maximum, subs), axis=1, keepdims=True)
    acc = functools.reduce(lambda a, b: a + b,
                           [_dot(jnp.exp2(sj - m_w).astype(BF16), vj) for sj, vj in zip(subs, vals)])
    o_win = acc[:, :DH] / acc[:, DH:DH + 1]

    gates = _sigmoid(gp_ref[0] + bg_ref[...])
    gcol = lambda j: jnp.concatenate([gates[:, 3 * r + j:3 * r + j + 1] for r in range(R)], axis=0)
    out4 = gcol(0) * o_cmp + gcol(1) * o_sel + gcol(2) * o_win
    o_ref[0] = jnp.concatenate([out4[r * T:(r + 1) * T] for r in range(R)], axis=1)


def _t5_bucket(dist):
    dist = jnp.maximum(dist, 0)
    max_exact = REL_BUCKETS // 2
    log_ratio = jnp.log(jnp.maximum(dist, 1).astype(F32) / max_exact) / math.log(REL_MAX_DIST / max_exact)
    large = jnp.minimum(max_exact + (log_ratio * (REL_BUCKETS - max_exact)).astype(jnp.int32), REL_BUCKETS - 1)
    return jnp.where(dist < max_exact, dist, large)


def _nsa_tables(rel_bias, S):
    T = ATT_TILE
    table = rel_bias.astype(F32) * LOG2E
    ii = jnp.arange(T)
    delta = ii[:, None] - ii[None, :]

    def tile(off):
        return table[_t5_bucket(off * T + delta)].transpose(2, 0, 1)

    n_far = -(-REL_MAX_DIST // T) + 1
    selb = [tile(o) for o in range(n_far + 1)]
    selb[0] = selb[0] + jnp.where(delta >= 0, 0.0, NEG)[None]
    selb = jnp.stack(selb, axis=0)
    n_win = WINDOW // T
    winb = []
    for o in range(n_win + 1):
        dist = o * T + delta
        ok = (dist >= 0) & (dist < WINDOW)
        winb.append(tile(o) + jnp.where(ok, 0.0, NEG)[None])
    winb.append(jnp.full_like(winb[0], NEG))
    winb = jnp.stack(winb, axis=0)
    n_pad = S // CMP_STRIDE
    i16 = jnp.arange(CMP_STRIDE)
    dd = jnp.arange(n_pad)
    gdist = CMP_STRIDE * dd[None, :] + i16[:, None] - (CMP_BLOCK - 1)
    grev = table[_t5_bucket(gdist)].transpose(2, 0, 1)[:, :, ::-1]
    n_sel = S // SEL_BLOCK
    cmp_start = jnp.arange(n_pad) * CMP_STRIDE
    sel_start = jnp.arange(n_sel) * SEL_BLOCK
    overlap = jnp.clip(jnp.minimum(cmp_start[:, None] + CMP_BLOCK, sel_start[None] + SEL_BLOCK)
                       - jnp.maximum(cmp_start[:, None], sel_start[None]), 0).astype(F32) / CMP_BLOCK
    n_cmp = (S - CMP_BLOCK) // CMP_STRIDE + 1
    overlap_t = jnp.where((jnp.arange(n_pad) < n_cmp)[:, None], overlap, 0.0).T
    tk = min(SEL_CHUNK * T, S)
    kpos_blk = jnp.arange(S) // SEL_BLOCK
    rows = jnp.arange(LANES)[:, None]
    expand = jnp.where(rows == kpos_blk[None, :], BIG, jnp.where(rows == n_sel, -BIG, 0.0)).astype(BF16)
    expand = expand.reshape(LANES, S // tk, tk).transpose(1, 0, 2)
    return grev, selb, winb, overlap_t, expand


def _nsa_attention(q, gp, bg, kcmp, vcmp, ks, vs, kw, vw, tables):
    B, S, D = q.shape
    T = ATT_TILE
    grev, selb, winb, overlap_t, expand = tables
    gw = NSA_R * NSA_DH
    seq = lambda a: pl.BlockSpec((1, 1) + a.shape[2:], lambda b, g, i: (b, g, 0, 0))
    per_head = lambda a: pl.BlockSpec((NSA_R,) + a.shape[1:], lambda b, g, i: (g,) + (0,) * (a.ndim - 1))
    tiles = lambda a: pl.BlockSpec((a.shape[0], NSA_R) + a.shape[2:], lambda b, g, i: (0, g, 0, 0))
    full = lambda a: pl.BlockSpec(a.shape, lambda b, g, i: (0,) * a.ndim)
    return pl.pallas_call(
        _nsa_kernel,
        out_shape=jax.ShapeDtypeStruct((B, S, D), F32),
        grid=(B, NSA_KV, S // T),
        in_specs=[pl.BlockSpec((1, T, gw), lambda b, g, i: (b, i, g)),
                  pl.BlockSpec((1, T, LANES), lambda b, g, i: (b, i, g)),
                  pl.BlockSpec((1, LANES), lambda b, g, i: (0, g)),
                  seq(kcmp), seq(vcmp), seq(ks), seq(vs), seq(kw), seq(vw),
                  per_head(grev), tiles(selb), tiles(winb), full(overlap_t), full(expand)],
        out_specs=pl.BlockSpec((1, T, gw), lambda b, g, i: (b, i, g)),
        compiler_params=_cparams(("parallel", "parallel", "arbitrary")),
        name="nsa_attention",
    )(q, gp, bg, kcmp, vcmp, ks, vs, kw, vw, grev, selb, winb, overlap_t, expand)


def _nsa_proj_kernel(x_ref, g_ref, sh_ref, sc_ref, wq_ref, wk_ref, wv_ref, wg_ref,
                     q_ref, gp_ref, kc_ref, vc_ref, ks_ref, kw_ref, vst_ref, vwt_ref):
    KV, DH, T = NSA_KV, NSA_DH, ATT_TILE
    h = _modulated_norm(x_ref[0], g_ref[...], sh_ref[0], sc_ref[0]).astype(BF16)
    q_ref[0] = _dot(h, wq_ref[...])
    gp_ref[0] = _dot(h, wg_ref[...])
    k3 = _dot(h, wk_ref[...])
    v3 = _dot(h, wv_ref[...])
    vs_t = v3[:, KV_W:2 * KV_W].T.astype(BF16)
    vw_t = v3[:, 2 * KV_W:].T.astype(BF16)
    for g in range(KV):
        cols = slice(g * DH, (g + 1) * DH)
        kc_ref[0, g] = k3[:, cols].astype(BF16)
        vc_ref[0, g] = v3[:, cols].astype(BF16)
        ks_ref[0, g] = k3[:, KV_W + g * DH:KV_W + (g + 1) * DH].astype(BF16)
        kw_ref[0, g] = k3[:, 2 * KV_W + g * DH:2 * KV_W + (g + 1) * DH].astype(BF16)
        vst_ref[0, g, 0] = vs_t[cols]
        for j in range(vwt_ref.shape[2]):
            vwt_ref[0, g, j] = vw_t[cols, j * T:(j + 1) * T]


def _nsa_proj(x, g, shift, scale, weights):
    B, S, D = x.shape
    KV, DH, T = NSA_KV, NSA_DH, ATT_TILE
    tm = min(SEL_CHUNK * T, S)
    vec = pl.BlockSpec((1, 1, D), lambda b, i: (b, 0, 0))
    rows = lambda n: pl.BlockSpec((1, tm, n), lambda b, i: (b, i, 0))
    keys = pl.BlockSpec((1, KV, tm, DH), lambda b, i: (b, 0, i, 0))
    key_shape = jax.ShapeDtypeStruct((B, KV, S, DH), BF16)
    return pl.pallas_call(
        _nsa_proj_kernel,
        out_shape=[jax.ShapeDtypeStruct((B, S, D), F32), jax.ShapeDtypeStruct((B, S, KV * LANES), F32),
                   key_shape, key_shape, key_shape, key_shape,
                   jax.ShapeDtypeStruct((B, KV, S // tm, DH, tm), BF16),
                   jax.ShapeDtypeStruct((B, KV, S // T, DH, T), BF16)],
        grid=(B, S // tm),
        in_specs=[pl.BlockSpec((1, tm, D), lambda b, i: (b, i, 0)),
                  pl.BlockSpec((1, D), lambda b, i: (0, 0)), vec, vec]
                 + [pl.BlockSpec(w.shape, lambda b, i: (0, 0)) for w in weights],
        out_specs=[rows(D), rows(KV * LANES), keys, keys, keys, keys,
                   pl.BlockSpec((1, KV, 1, DH, tm), lambda b, i: (b, 0, i, 0, 0)),
                   pl.BlockSpec((1, KV, tm // T, DH, T), lambda b, i: (b, 0, i, 0, 0))],
        compiler_params=_cparams(("parallel", "parallel")),
        name="nsa_proj",
    )(x, g.reshape(1, D), shift, scale, *weights)


def _nsa_t_kernel(q_ref, gp_ref, bg_ref, kc_ref, vct_ref, ks_ref, vst_ref, kw_ref, vwt_ref,
                  cfar_ref, band_ref, selb_ref, winb_ref, ovt_ref, o_ref, s_scr, sel_scr, sbuf):
    T = ATT_TILE
    R, DH = NSA_R, NSA_DH
    qi = pl.program_id(2)
    q0 = qi * T
    n_pad = kc_ref.shape[2]
    n_sel = ovt_ref.shape[0]
    CH = vst_ref.shape[4] // T
    n_far = selb_ref.shape[0] - 1
    n_win = winb_ref.shape[0] - 2
    band_rows = band_ref.shape[2] - T // CMP_STRIDE * 2

    q_t = (q_ref[0] * (DH ** -0.5 * LOG2E)).T
    q4 = jnp.concatenate([q_t[r * DH:(r + 1) * DH] for r in range(R)], axis=1).astype(BF16)
    t_lane = q0 + lax.broadcasted_iota(jnp.int32, (1, R * T), 1) % T

    with_ones = lambda v_t: jnp.concatenate([v_t, jnp.ones_like(v_t)], axis=0)
    gates_t = _sigmoid(gp_ref[0] + bg_ref[...]).T
    gvec = lambda j: jnp.concatenate([gates_t[3 * r + j:3 * r + j + 1, :] for r in range(R)], axis=1)

    grp = T // CMP_STRIDE
    s_scr[0:n_pad, :] = _dot(kc_ref[0, 0], q4) + cfar_ref[0]
    s_scr[n_pad:n_pad + 2 * grp, :] = jnp.zeros((2 * grp, R * T), F32)
    r0 = jnp.maximum(qi * grp - 2 * grp, 0)
    x0 = r0 - (qi * grp - 2 * grp)
    r0 = pl.multiple_of(r0, SUBLANES)
    x0 = pl.multiple_of(x0, SUBLANES)
    s_scr[pl.ds(r0, band_rows), :] += band_ref[0, 0, pl.ds(x0, band_rows), :]
    lim = pl.multiple_of(qi * grp + 2 * grp, SUBLANES)
    s_scr[pl.ds(lim, n_pad), :] = jnp.full((n_pad, R * T), NEG, F32)

    w_subs, w_vals = [], []
    for d in range(n_win + 1):
        kt = jnp.maximum(qi - d, 0)
        off = pl.multiple_of(kt * T, T)
        tile = jnp.where(qi >= d, d, n_win + 1)
        w_subs.append(_dot(kw_ref[0, 0, pl.ds(off, T), :], q4) + winb_ref[tile, 0])
        w_vals.append(with_ones(vwt_ref[0, 0, kt]))

    s = s_scr[0:n_pad, :]
    e = jnp.exp2(s - jnp.max(s, axis=0, keepdims=True))
    inv = jnp.where(t_lane >= CMP_BLOCK - 1, 1.0 / jnp.sum(e, axis=0, keepdims=True), 0.0)
    p = e * inv
    o_cmp = _dot(vct_ref[0, 0], p.astype(BF16))
    psum = functools.reduce(lambda a, b: a + b, [p[:, r * T:(r + 1) * T] for r in range(R)])

    m_w = jnp.max(functools.reduce(jnp.maximum, w_subs), axis=0, keepdims=True)
    acc = functools.reduce(lambda a, b: a + b,
                           [_dot(vj, jnp.exp2(sj - m_w).astype(BF16)) for sj, vj in zip(w_subs, w_vals)])
    o_win = acc[:DH] * (1.0 / acc[DH:DH + 1])
    out_t = gvec(0) * o_cmp + gvec(2) * o_win

    imp_t = _dot(ovt_ref[...], psum, precision=HIGHEST)
    jj = lax.broadcasted_iota(jnp.int32, (n_sel, T), 0)
    blk_t = (q0 + lax.broadcasted_iota(jnp.int32, (1, T), 1)) // SEL_BLOCK
    forced = (jj == 0) | (jj == blk_t) | (jj == blk_t - 1)
    score = jnp.where(forced, FORCE, jnp.where(jj <= blk_t, imp_t, -1.0))
    n_blk = n_sel // SUBLANES
    rows = [score[v * SUBLANES:(v + 1) * SUBLANES] for v in range(n_blk)]
    cnts = [jnp.zeros((SUBLANES, T), F32) for _ in range(n_blk)]
    sub = lax.broadcasted_iota(jnp.int32, (SUBLANES, T), 0)
    for j2 in range(n_sel):
        c2 = score[j2:j2 + 1, :]
        for v in range(n_blk):
            lo = v * SUBLANES
            if lo > j2:
                beats = c2 >= rows[v]
            elif lo + SUBLANES - 1 <= j2:
                beats = c2 > rows[v]
            else:
                beats = (c2 > rows[v]) | ((c2 >= rows[v]) & (sub > j2 - lo))
            cnts[v] = cnts[v] + jnp.where(beats, 1.0, 0.0)
    cnt = jnp.concatenate(cnts, axis=0)
    chosen = (cnt < float(min(SEL_TOPK, n_sel))) & (jj <= blk_t)
    sel_scr[...] = jnp.where(chosen, 0.0, -BIG)

    def block_mask(kt):
        per_tile = T // SEL_BLOCK
        parts = [jnp.broadcast_to(sel_scr[pl.ds(kt * per_tile + i, 1), :], (SEL_BLOCK, T)) for i in range(per_tile)]
        m1 = jnp.concatenate(parts, axis=0)
        return jnp.concatenate([m1] * R, axis=1)

    def sel_scores(slot, kc):
        off = pl.multiple_of(kc * (CH * T), CH * T)
        s = _dot(ks_ref[0, 0, pl.ds(off, CH * T), :], q4)
        subs = []
        for j in range(CH):
            kt = kc * CH + j
            d = jnp.clip(qi - kt, 0, n_far)
            subs.append(s[j * T:(j + 1) * T] + selb_ref[d, 0] + block_mask(kt))
        s = jnp.concatenate(subs, axis=0)
        sbuf[slot] = s
        return jnp.max(s, axis=0, keepdims=True)

    def sel_consume(slot, kc, m, acc, m_cur):
        m_new = jnp.maximum(m, m_cur)
        pb = jnp.exp2(sbuf[slot] - m_new).astype(BF16)
        return m_new, jnp.exp2(m - m_new) * acc + _dot(with_ones(vst_ref[0, 0, kc]), pb)

    last_chunk = vst_ref.shape[2] - 1

    def sel_body(i, carry):
        m, acc, m_even = carry
        m_odd = sel_scores(1, 2 * i + 1)
        m, acc = sel_consume(0, 2 * i, m, acc, m_even)
        m_even = sel_scores(0, jnp.minimum(2 * i + 2, last_chunk))
        m, acc = sel_consume(1, 2 * i + 1, m, acc, m_odd)
        return m, acc, m_even

    n_chunks = qi // CH + 1
    _, acc, _ = lax.fori_loop(0, (n_chunks + 1) // 2, sel_body,
                              (jnp.full((1, R * T), NEG, F32), jnp.zeros((2 * DH, R * T), F32), sel_scores(0, 0)))
    out_t = out_t + gvec(1) * (acc[:DH] * (1.0 / acc[DH:DH + 1]))
    for pr in range(R // 2):
        pair = jnp.concatenate([out_t[:, (2 * pr) * T:(2 * pr + 1) * T],
                                out_t[:, (2 * pr + 1) * T:(2 * pr + 2) * T]], axis=0)
        o_ref[0, :, pr * 2 * DH:(pr + 1) * 2 * DH] = pair.T


def _bias_lookup(table, dist):
    idx = _t5_bucket(dist)
    out = jnp.zeros(idx.shape + (table.shape[1],), F32)
    for k in range(table.shape[0]):
        out = out + jnp.where((idx == k)[..., None], table[k], 0.0)
    return out


def _nsa_t_tables(rel_bias, S):
    T, R, KV = ATT_TILE, NSA_R, NSA_KV
    table = rel_bias.astype(F32) * LOG2E
    ii = jnp.arange(T)
    delta = ii[None, :] - ii[:, None]

    def lanes(a):
        a = jnp.moveaxis(a, -1, 0)
        a = a.reshape((KV, R) + a.shape[1:])
        return jnp.moveaxis(a, 1, 2).reshape(KV, a.shape[2], R * a.shape[3])

    def tile(off):
        return lanes(_bias_lookup(table, off * T + delta))

    mask4 = lambda ok: jnp.tile(jnp.where(ok, 0.0, NEG), (1, R))[None]
    n_far = -(-REL_MAX_DIST // T) + 1
    selb = [tile(o) for o in range(n_far + 1)]
    selb[0] = selb[0] + mask4(delta >= 0)
    selb = jnp.stack(selb, axis=0)
    n_win = WINDOW // T
    winb = [tile(o) + mask4((o * T + delta >= 0) & (o * T + delta < WINDOW)) for o in range(n_win + 1)]
    winb.append(jnp.full_like(winb[0], NEG))
    winb = jnp.stack(winb, axis=0)

    grp = T // CMP_STRIDE
    far = _bias_lookup(table, jnp.asarray(2 * REL_MAX_DIST))
    xx = jnp.arange(4 * grp)
    bdist = ii[None, :] - CMP_STRIDE * (xx[:, None] - 2 * grp) - (CMP_BLOCK - 1)
    band = jnp.where((bdist >= 0)[..., None], _bias_lookup(table, bdist) - far, NEG)
    band = jnp.concatenate([lanes(band), jnp.zeros((KV, 2 * grp, R * T), F32)], axis=1)[:, None]
    cfar = jnp.repeat(far.reshape(KV, R), T, axis=1)[:, None]

    n_pad = S // CMP_STRIDE
    n_sel = S // SEL_BLOCK
    cmp_start = jnp.arange(n_pad) * CMP_STRIDE
    sel_start = jnp.arange(n_sel) * SEL_BLOCK
    overlap = jnp.clip(jnp.minimum(cmp_start[:, None] + CMP_BLOCK, sel_start[None] + SEL_BLOCK)
                       - jnp.maximum(cmp_start[:, None], sel_start[None]), 0).astype(F32) / CMP_BLOCK
    n_cmp = (S - CMP_BLOCK) // CMP_STRIDE + 1
    overlap_t = jnp.where((jnp.arange(n_pad) < n_cmp)[:, None], overlap, 0.0).T
    return cfar, band, selb, winb, overlap_t


def _nsa_t_attention(q, gp, bg, kcmp, vcmp_t, ks, vs_t, kw, vw_t, tables):
    B, S, D = q.shape
    T = ATT_TILE
    cfar, band, selb, winb, overlap_t = tables
    gw = NSA_R * NSA_DH
    n_pad = kcmp.shape[2]
    seq = lambda a: pl.BlockSpec((1, 1) + a.shape[2:], lambda b, g, i: (b, g) + (0,) * (a.ndim - 2))
    grp = lambda a: pl.BlockSpec((1,) + a.shape[1:], lambda b, g, i: (g,) + (0,) * (a.ndim - 1))
    tiles = lambda a: pl.BlockSpec((a.shape[0], 1) + a.shape[2:], lambda b, g, i: (0, g, 0, 0))
    full = lambda a: pl.BlockSpec(a.shape, lambda b, g, i: (0,) * a.ndim)
    return pl.pallas_call(
        _nsa_t_kernel,
        out_shape=jax.ShapeDtypeStruct((B, S, D), F32),
        grid=(B, NSA_KV, S // T),
        in_specs=[pl.BlockSpec((1, T, gw), lambda b, g, i: (b, i, g)),
                  pl.BlockSpec((1, T, LANES), lambda b, g, i: (b, i, g)),
                  pl.BlockSpec((1, LANES), lambda b, g, i: (0, g)),
                  seq(kcmp), seq(vcmp_t), seq(ks), seq(vs_t), seq(kw), seq(vw_t),
                  grp(cfar), grp(band), tiles(selb), tiles(winb), full(overlap_t)],
        out_specs=pl.BlockSpec((1, T, gw), lambda b, g, i: (b, i, g)),
        scratch_shapes=[pltpu.VMEM((2 * n_pad + 2 * (T // CMP_STRIDE), NSA_R * T), F32),
                        pltpu.VMEM((S // SEL_BLOCK, T), F32),
                        pltpu.VMEM((2, vs_t.shape[4], NSA_R * T), F32)],
        compiler_params=_cparams(("parallel", "parallel", "arbitrary")),
        name="nsa_attention",
    )(q, gp, bg, kcmp, vcmp_t, ks, vs_t, kw, vw_t, cfar, band, selb, winb, overlap_t)


def _moe_kernel(x_ref, g_ref, sh_ref, sc_ref, gate_ref, wr_ref, br_ref, wg_ref, wu_ref, wd_ref, fg_ref,
                o_ref, hb_scr, rt_scr, acc_scr, *, final):
    NG, PG, FH = MOE_GROUPS, MOE_PER_GROUP, MOE_HIDDEN
    c = pl.program_id(2)

    @pl.when(c == 0)
    def _():
        h = _modulated_norm(x_ref[0], g_ref[...], sh_ref[0], sc_ref[0])
        h_hi = h.astype(BF16)
        hb_scr[...] = h_hi
        h_lo = (h - h_hi.astype(F32)).astype(BF16)
        logits = (_dot(h_hi, wr_ref[0]) + _dot(h_lo, wr_ref[0]) + _dot(h_hi, wr_ref[1]) + br_ref[...]).T
        gl = [logits[NG * PG + g:NG * PG + g + 1, :] for g in range(NG)]
        gmax = functools.reduce(jnp.maximum, gl)
        gtop = jnp.full_like(gmax, float(NG - 1))
        for g in reversed(range(NG - 1)):
            gtop = jnp.where(gl[g] == gmax, float(g), gtop)
        p_g = 1.0 / functools.reduce(lambda a, b: a + b, [jnp.exp(v - gmax) for v in gl])
        a = []
        for j in range(PG):
            v = logits[(NG - 1) * PG + j:(NG - 1) * PG + j + 1, :]
            for g in reversed(range(NG - 1)):
                v = jnp.where(gtop == float(g), logits[g * PG + j:g * PG + j + 1, :], v)
            a.append(v)

        def first_max(vals):
            vmax = functools.reduce(jnp.maximum, vals)
            taken = jnp.zeros_like(vmax) > 1.0
            hits = []
            for v in vals:
                hit = (v == vmax) & jnp.logical_not(taken)
                taken = taken | hit
                hits.append(hit)
            return vmax, hits

        v1, hit1 = first_max(a)
        rest = [jnp.where(hh, -jnp.inf, v) for hh, v in zip(hit1, a)]
        v2, hit2 = first_max(rest)
        e2 = jnp.exp(v2 - v1)
        w1 = p_g / (1.0 + e2)
        w2 = p_g * e2 / (1.0 + e2)
        tm = gtop.shape[1]
        row = lax.broadcasted_iota(jnp.int32, (SUBLANES, tm), 0)
        rt = jnp.where(row == PG, gtop, 0.0)
        for j in range(PG):
            wj = jnp.where(hit1[j], w1, jnp.where(hit2[j], w2, 0.0))
            rt = jnp.where(row == j, wj, rt)
        rt_scr[...] = jnp.concatenate([rt, jnp.zeros((LANES - SUBLANES, tm), F32)], axis=0).T

    hb = hb_scr[...]
    rt = rt_scr[...]
    in_group = rt[:, PG:PG + 1] == c.astype(F32)
    hid = _silu(_dot(hb, wg_ref[0])) * _dot(hb, wu_ref[0])
    parts = [hid[:, j * FH:(j + 1) * FH] * jnp.where(in_group, rt[:, j:j + 1], 0.0) for j in range(PG)]
    contrib = _dot(jnp.concatenate(parts, axis=1).astype(BF16), wd_ref[0])

    @pl.when(c == 0)
    def _():
        acc_scr[...] = contrib

    @pl.when(c > 0)
    def _():
        acc_scr[...] += contrib

    @pl.when(c == NG - 1)
    def _():
        y = x_ref[0] + gate_ref[0] * acc_scr[...]
        if final:
            y = y * lax.rsqrt(jnp.mean(y * y, axis=-1, keepdims=True) + EPS) * fg_ref[...]
        o_ref[0] = y


def _moe(x, g, shift, scale, gate, wg, bg, we, be, w_gate, w_up, w_down, final_g, final, tm=512):
    B, S, D = x.shape
    NG, PG, FH = MOE_GROUPS, MOE_PER_GROUP, MOE_HIDDEN
    wr = jnp.zeros((D, LANES), F32)
    wr = wr.at[:, :NG * PG].set(we.reshape(D, NG * PG).astype(F32)).at[:, NG * PG:NG * PG + NG].set(wg.astype(F32))
    br = jnp.zeros((1, LANES), F32)
    br = br.at[0, :NG * PG].set(be.reshape(NG * PG).astype(F32)).at[0, NG * PG:NG * PG + NG].set(bg.astype(F32))
    wr_hi = wr.astype(BF16)
    wr = jnp.stack([wr_hi, (wr - wr_hi.astype(F32)).astype(BF16)])
    grp = lambda w: w.reshape(NG, PG, D, FH).transpose(0, 2, 1, 3).reshape(NG, D, PG * FH).astype(BF16)
    wd = w_down.reshape(NG, PG * FH, D).astype(BF16)
    vec = pl.BlockSpec((1, 1, D), lambda b, i, c: (b, 0, 0))
    row = pl.BlockSpec((1, D), lambda b, i, c: (0, 0))
    wspec = lambda k, n: pl.BlockSpec((1, k, n), lambda b, i, c: (c, 0, 0))
    return pl.pallas_call(
        functools.partial(_moe_kernel, final=final),
        out_shape=jax.ShapeDtypeStruct((B, S, D), F32),
        grid=(B, S // tm, NG),
        in_specs=[pl.BlockSpec((1, tm, D), lambda b, i, c: (b, i, 0)), row, vec, vec, vec,
                  pl.BlockSpec((2, D, LANES), lambda b, i, c: (0, 0, 0)),
                  pl.BlockSpec((1, LANES), lambda b, i, c: (0, 0)),
                  wspec(D, PG * FH), wspec(D, PG * FH), wspec(PG * FH, D), row],
        out_specs=pl.BlockSpec((1, tm, D), lambda b, i, c: (b, i, 0)),
        scratch_shapes=[pltpu.VMEM((tm, D), BF16), pltpu.VMEM((tm, LANES), F32), pltpu.VMEM((tm, D), F32)],
        compiler_params=_cparams(("parallel", "parallel", "arbitrary")),
        name="moe",
    )(x, g.reshape(1, D), shift, scale, gate, wr, br, grp(w_gate), grp(w_up), wd, final_g.reshape(1, D))


def _mlstm_s5_layer(x, g, shift, scale, gate, w_in, conv_w, b_i, b_f, head_g, s5_params, w_out):
    H = MLSTM_HEADS
    A = MIX_A
    w_if = jnp.zeros((D_MODEL, LANES), F32).at[:, :2 * H].set(w_in[:, 4 * A:4 * A + 2 * H])
    weights = [w_in[:, :2 * A], w_in[:, 2 * A:4 * A], w_if, w_in[:, 4 * A + 2 * H:]]
    qk, vo, ifg, u = _norm_matmul(x, g, shift, scale, [w.astype(BF16) for w in weights], [F32] * 4)
    gate_bias = jnp.zeros((1, LANES), F32).at[0, :H].set(b_i.astype(F32)).at[0, H:2 * H].set(b_f.astype(F32))
    hm = _mlstm(qk, vo, ifg, conv_w.astype(F32), gate_bias, head_g.reshape(1, A).astype(F32))
    ys = _s5s(u, _s5s_tables(*s5_params))
    w_out = w_out.astype(BF16)
    return _out_residual(x, gate, [hm, ys], [w_out[:A], w_out[A:]])


def _nsa_layer(x, g, shift, scale, gate, w_in, b_gate, cmp_pos, cmp_w1, cmp_b1, cmp_w2, cmp_b2, rel_bias, w_out):
    B, S, D = x.shape
    KV, R, DH = NSA_KV, NSA_R, NSA_DH
    w_g = jnp.zeros((D, KV, LANES), F32).at[:, :, :3 * R].set(w_in[:, D + 6 * KV_W:].reshape(D, KV, 3 * R))
    b_g = jnp.zeros((KV, LANES), F32).at[:, :3 * R].set(b_gate.reshape(KV, 3 * R).astype(F32))
    kv_cols = lambda i: w_in[:, D + i * KV_W:D + (i + 1) * KV_W]
    w_k = jnp.concatenate([kv_cols(0), kv_cols(2), kv_cols(4)], axis=1)
    w_v = jnp.concatenate([kv_cols(1), kv_cols(3), kv_cols(5)], axis=1)
    weights = [w_in[:, :D], w_k, w_v, w_g.reshape(D, KV * LANES)]
    q, gp, kc, vc, ks, kw, vs_t, vw_t = _nsa_proj(x, g, shift, scale, [w.astype(BF16) for w in weights])
    grp = CMP_STRIDE
    xg = jnp.stack([kc, vc]).reshape(2, B, KV * S // grp, grp * DH)
    cmp = _compress(xg, cmp_pos, cmp_w1, cmp_b1, cmp_w2, cmp_b2).reshape(2, B, KV, S // grp, DH).astype(BF16)
    out = _nsa_t_attention(q, gp, b_g.reshape(1, KV * LANES), cmp[0], cmp[1].transpose(0, 1, 3, 2),
                           ks, vs_t, kw, vw_t, _nsa_t_tables(rel_bias, S))
    return _out_residual(x, gate, [out], [w_out.astype(BF16)])


def kernel(x, c, rel_bias, ada_w, ada_b, norm_g, final_g,
           a_w_in, a_conv, a_b_i, a_b_f, a_head_g,
           s5_lam_re, s5_lam_im, s5_log_dt, s5_b_re, s5_b_im, s5_c_re, s5_c_im,
           s5_d, s5_glu_w, s5_glu_b, a_w_out,
           n_w_in, n_b_gate, n_cmp_pos, n_cmp_w1, n_cmp_b1, n_cmp_w2, n_cmp_b2, n_w_out,
           r_grp_w, r_grp_b, r_exp_w, r_exp_b, e_w_gate, e_w_up, e_w_down):
    B, S, D = x.shape
    mod = _ada_mod(c, ada_w, ada_b).reshape(DEPTH, 2, B, 1, 3 * D)
    split = lambda m: (m[..., :D], m[..., D:2 * D], m[..., 2 * D:])
    for layer in range(DEPTH):
        shift, scale, gate = split(mod[layer, 0])
        j = layer // 2
        if layer % 2 == 0:
            s5_params = (s5_lam_re[j], s5_lam_im[j], s5_log_dt[j], s5_b_re[j], s5_b_im[j],
                         s5_c_re[j], s5_c_im[j], s5_d[j], s5_glu_w[j], s5_glu_b[j])
            x = _mlstm_s5_layer(x, norm_g[layer, 0], shift, scale, gate, a_w_in[j], a_conv[j], a_b_i[j], a_b_f[j],
                                a_head_g[j], s5_params, a_w_out[j])
        else:
            x = _nsa_layer(x, norm_g[layer, 0], shift, scale, gate, n_w_in[j], n_b_gate[j], n_cmp_pos[j],
                           n_cmp_w1[j], n_cmp_b1[j], n_cmp_w2[j], n_cmp_b2[j], rel_bias, n_w_out[j])
        shift, scale, gate = split(mod[layer, 1])
        x = _moe(x, norm_g[layer, 1], shift, scale, gate, r_grp_w[layer], r_grp_b[layer], r_exp_w[layer],
                 r_exp_b[layer], e_w_gate[layer], e_w_up[layer], e_w_down[layer], final_g,
                 final=(layer == DEPTH - 1))
    return x
```

```python
import functools
import math

import jax
import jax.numpy as jnp
from jax import lax
from jax.experimental import pallas as pl
from jax.experimental.pallas import tpu as pltpu

F32 = jnp.float32
BF16 = jnp.bfloat16
HIGHEST = lax.Precision.HIGHEST

D_MODEL = 1024
DEPTH = 2
MIX_A = 512
MLSTM_HEADS = 4
MLSTM_DH = MIX_A // MLSTM_HEADS
MLSTM_CHUNK = 128
CONV_K = 4
MIX_B = D_MODEL - MIX_A
S5_GROUP = 16
S5_GROUPS = MIX_B // S5_GROUP
S5_STATE = 64
S5_CHUNK = 16
NSA_HEADS = 16
NSA_KV = 4
NSA_R = NSA_HEADS // NSA_KV
NSA_DH = D_MODEL // NSA_HEADS
KV_W = NSA_KV * NSA_DH
CMP_BLOCK = 32
CMP_STRIDE = 16
CMP_HIDDEN = 256
SEL_BLOCK = 64
SEL_TOPK = 16
WINDOW = 512
FORCE = 1e9
REL_BUCKETS = 32
REL_MAX_DIST = 128
MOE_GROUPS = 4
MOE_PER_GROUP = 4
MOE_HIDDEN = 256
EPS = 1e-6
NEG = -1e30
BIG = 1e30
LOG2E = math.log2(math.e)
SEL_CHUNK = 1

LANES = 128
SUBLANES = 8
ATT_TILE = 256
VMEM_LIMIT = 56 * 1024 * 1024


def _cparams(sem):
    return pltpu.CompilerParams(dimension_semantics=sem, vmem_limit_bytes=VMEM_LIMIT)


def _dot(a, b, precision=None):
    return jnp.dot(a, b, preferred_element_type=F32, precision=precision)


def _dot_nt(a, b):
    return lax.dot_general(a, b, (((1,), (1,)), ((), ())), preferred_element_type=F32)


def _sigmoid(x):
    return 1.0 / (1.0 + jnp.exp(-x))


def _silu(x):
    return x * _sigmoid(x)


def _gelu_tanh(x):
    return 0.5 * x * (1.0 + jnp.tanh(math.sqrt(2.0 / math.pi) * (x + 0.044715 * (x * x * x))))


def _modulated_norm(x, g, shift, scale):
    y = x * lax.rsqrt(jnp.mean(x * x, axis=-1, keepdims=True) + EPS) * g
    return y * (1.0 + scale) + shift


def _ada_kernel(c_ref, w_ref, b_ref, o_ref):
    c = c_ref[...]
    o_ref[0] = _dot(_silu(c), w_ref[0]) + b_ref[0]


def _ada_mod(c, ada_w, ada_b):
    B, D = c.shape
    n_mod = ada_w.shape[0] * ada_w.shape[1]
    w = ada_w.reshape(n_mod, D, 3 * D)
    b = ada_b.reshape(n_mod, 1, 3 * D)
    tn = 1024
    return pl.pallas_call(
        _ada_kernel,
        out_shape=jax.ShapeDtypeStruct((n_mod, B, 3 * D), F32),
        grid=(n_mod, 3 * D // tn),
        in_specs=[pl.BlockSpec((B, D), lambda i, j: (0, 0)),
                  pl.BlockSpec((1, D, tn), lambda i, j: (i, 0, j)),
                  pl.BlockSpec((1, 1, tn), lambda i, j: (i, 0, j))],
        out_specs=pl.BlockSpec((1, B, tn), lambda i, j: (i, 0, j)),
        compiler_params=_cparams(("parallel", "parallel")),
        name="ada_mod",
    )(c, w, b)


def _norm_mm_kernel(*refs, n_w):
    x_ref, g_ref, sh_ref, sc_ref = refs[:4]
    w_refs = refs[4:4 + n_w]
    o_refs = refs[4 + n_w:]
    h = _modulated_norm(x_ref[0], g_ref[...], sh_ref[0], sc_ref[0]).astype(BF16)
    for w_ref, o_ref in zip(w_refs, o_refs):
        o_ref[0] = _dot(h, w_ref[...]).astype(o_ref.dtype)


def _norm_matmul(x, g, shift, scale, weights, out_dtypes, tm=512):
    B, S, D = x.shape
    n_w = len(weights)
    vec = pl.BlockSpec((1, 1, D), lambda b, i: (b, 0, 0))
    in_specs = [pl.BlockSpec((1, tm, D), lambda b, i: (b, i, 0)),
                pl.BlockSpec((1, D), lambda b, i: (0, 0)), vec, vec]
    in_specs += [pl.BlockSpec(w.shape, lambda b, i: (0, 0)) for w in weights]
    return pl.pallas_call(
        functools.partial(_norm_mm_kernel, n_w=n_w),
        out_shape=[jax.ShapeDtypeStruct((B, S, w.shape[1]), dt) for w, dt in zip(weights, out_dtypes)],
        grid=(B, S // tm),
        in_specs=in_specs,
        out_specs=[pl.BlockSpec((1, tm, w.shape[1]), lambda b, i: (b, i, 0)) for w in weights],
        compiler_params=_cparams(("parallel", "parallel")),
        name="norm_matmul",
    )(x, g.reshape(1, D), shift, scale, *weights)


def _out_res_kernel(*refs, n_in):
    x_ref, gate_ref = refs[:2]
    a_refs = refs[2:2 + n_in]
    w_refs = refs[2 + n_in:2 + 2 * n_in]
    o_ref = refs[2 + 2 * n_in]
    acc = None
    for a_ref, w_ref in zip(a_refs, w_refs):
        t = _dot(a_ref[0].astype(BF16), w_ref[...])
        acc = t if acc is None else acc + t
    o_ref[0] = x_ref[0] + gate_ref[0] * acc


def _out_residual(x, gate, acts, weights, tm=512):
    B, S, D = x.shape
    n_in = len(acts)
    in_specs = [pl.BlockSpec((1, tm, D), lambda b, i: (b, i, 0)),
                pl.BlockSpec((1, 1, D), lambda b, i: (b, 0, 0))]
    in_specs += [pl.BlockSpec((1, tm, a.shape[2]), lambda b, i: (b, i, 0)) for a in acts]
    in_specs += [pl.BlockSpec(w.shape, lambda b, i: (0, 0)) for w in weights]
    return pl.pallas_call(
        functools.partial(_out_res_kernel, n_in=n_in),
        out_shape=jax.ShapeDtypeStruct((B, S, D), F32),
        grid=(B, S // tm),
        in_specs=in_specs,
        out_specs=pl.BlockSpec((1, tm, D), lambda b, i: (b, i, 0)),
        compiler_params=_cparams(("parallel", "parallel")),
        name="out_residual",
    )(x, gate, *acts, *weights)


def _mlstm_kernel(qk_ref, vo_ref, if_ref, cw_ref, gb_ref, hg_ref, tril_ref, o_ref,
                  xbuf, c_scr, n_scr, m_scr):
    L, H, DH = MLSTM_CHUNK, MLSTM_HEADS, MLSTM_DH
    pad = SUBLANES

    @pl.when(pl.program_id(1) == 0)
    def _():
        xbuf[0:pad, :] = jnp.zeros((pad, 2 * MIX_A), F32)
        c_scr[...] = jnp.zeros_like(c_scr)
        n_scr[...] = jnp.zeros_like(n_scr)
        m_scr[...] = jnp.zeros_like(m_scr)

    xbuf[pad:pad + L, :] = qk_ref[0]
    cw = cw_ref[...]
    conv = None
    for j in range(CONV_K):
        lo = pad - (CONV_K - 1) + j
        t = xbuf[lo:lo + L, :] * cw[j:j + 1, :]
        conv = t if conv is None else conv + t
    xbuf[0:pad, :] = xbuf[L:L + pad, :]
    qk = _silu(conv)
    q = qk[:, :MIX_A]
    k = qk[:, MIX_A:] * (DH ** -0.5)
    vo = vo_ref[0]
    v = vo[:, :MIX_A]
    o_pre = vo[:, MIX_A:]

    ifb = if_ref[0] + gb_ref[...]
    lf = jnp.minimum(ifb, 0.0) - jnp.log1p(jnp.exp(-jnp.abs(ifb)))
    bcs = _dot(tril_ref[...], lf, precision=HIGHEST)
    ifb_t = ifb.T
    bcs_t = bcs.T
    row = lax.broadcasted_iota(jnp.int32, (L, L), 0)
    col = lax.broadcasted_iota(jnp.int32, (L, L), 1)
    causal = col <= row

    outs = []
    for h in range(H):
        sl = slice(h * DH, (h + 1) * DH)
        qh, kh, vh = q[:, sl], k[:, sl], v[:, sl]
        qb, kb = qh.astype(BF16), kh.astype(BF16)
        b_col = bcs[:, H + h:H + h + 1]
        b_row = bcs_t[H + h:H + h + 1, :]
        li_col = ifb[:, h:h + 1]
        li_row = ifb_t[h:h + 1, :]
        b_last = b_col[L - 1:L, :]
        m0 = m_scr[h][:, 0:1]
        c0 = c_scr[h]
        n0 = n_scr[h]

        log_d = jnp.where(causal, b_col - b_row + li_row, NEG)
        log_inter = b_col + m0
        m_t = jnp.maximum(log_inter, jnp.max(log_d, axis=1, keepdims=True))
        dmat = jnp.exp(log_d - m_t)
        a_inter = jnp.exp(log_inter - m_t)
        s = _dot_nt(qb, kb) * dmat
        num = _dot(s.astype(BF16), vh.astype(BF16)) + a_inter * _dot_nt(qb, c0.astype(BF16))
        den = jnp.sum(s, axis=1, keepdims=True) + a_inter * jnp.sum(qh * n0, axis=1, keepdims=True)
        hh = num / jnp.maximum(jnp.abs(den), jnp.exp(-m_t))

        w_col = b_last - b_col + li_col
        m_loc = jnp.max(w_col, axis=0, keepdims=True)
        e = jnp.exp(w_col - m_loc)
        c_loc = _dot((vh * e).T.astype(BF16), kb)
        n_loc = jnp.sum(kh * e, axis=0, keepdims=True)
        m_new = jnp.maximum(b_last + m0, m_loc)
        a = jnp.exp(b_last + m0 - m_new)
        sc = jnp.exp(m_loc - m_new)
        c_scr[h] = a * c0 + sc * c_loc
        n_scr[h] = a * n0 + sc * n_loc
        m_scr[h] = jnp.broadcast_to(m_new, (1, LANES))

        outs.append(hh * lax.rsqrt(jnp.mean(hh * hh, axis=1, keepdims=True) + EPS))
    hm = jnp.concatenate(outs, axis=1)
    o_ref[0] = _sigmoid(o_pre) * (hm * hg_ref[...])


def _mlstm(qk, vo, ifg, conv_w, gate_bias, head_g):
    B, S, _ = qk.shape
    L, H, DH = MLSTM_CHUNK, MLSTM_HEADS, MLSTM_DH
    tril = jnp.tril(jnp.ones((L, L), F32))
    return pl.pallas_call(
        _mlstm_kernel,
        out_shape=jax.ShapeDtypeStruct((B, S, MIX_A), F32),
        grid=(B, S // L),
        in_specs=[pl.BlockSpec((1, L, 2 * MIX_A), lambda b, c: (b, c, 0)),
                  pl.BlockSpec((1, L, 2 * MIX_A), lambda b, c: (b, c, 0)),
                  pl.BlockSpec((1, L, LANES), lambda b, c: (b, c, 0)),
                  pl.BlockSpec((CONV_K, 2 * MIX_A), lambda b, c: (0, 0)),
                  pl.BlockSpec((1, LANES), lambda b, c: (0, 0)),
                  pl.BlockSpec((1, MIX_A), lambda b, c: (0, 0)),
                  pl.BlockSpec((L, L), lambda b, c: (0, 0))],
        out_specs=pl.BlockSpec((1, L, MIX_A), lambda b, c: (b, c, 0)),
        scratch_shapes=[pltpu.VMEM((L + SUBLANES, 2 * MIX_A), F32),
                        pltpu.VMEM((H, DH, DH), F32),
                        pltpu.VMEM((H, 1, DH), F32),
                        pltpu.VMEM((H, 1, LANES), F32)],
        compiler_params=_cparams(("parallel", "arbitrary")),
        name="mlstm",
    )(qk, vo, ifg, conv_w, gate_bias, head_g, tril)


def _s5_kernel(u_ref, m_ref, hre_ref, him_ref, ere_ref, eim_ref, are_ref, aim_ref, d_ref, gw_ref, gb_ref, o_ref,
               xl_re, xl_im, x0_re, x0_im, *, n_chunks, batch):
    u = u_ref[0]
    xl_re[...] = _dot(u, hre_ref[0])
    xl_im[...] = _dot(u, him_ref[0])
    a_re = are_ref[0]
    a_im = aim_ref[0]

    def body(i, carry):
        re, im = carry
        r = pl.multiple_of(i * batch, batch)
        x0_re[pl.ds(r, batch), :] = re
        x0_im[pl.ds(r, batch), :] = im
        return (a_re * re - a_im * im + xl_re[pl.ds(r, batch), :],
                a_re * im + a_im * re + xl_im[pl.ds(r, batch), :])

    zero = jnp.zeros((batch, S5_STATE), F32)
    lax.fori_loop(0, n_chunks, body, (zero, zero), unroll=8)
    y = (_dot(u, m_ref[0]) + _dot(x0_re[...].astype(BF16), ere_ref[0]) + _dot(x0_im[...].astype(BF16), eim_ref[0])
         + u.astype(F32) * d_ref[0])
    ys = _gelu_tanh(y)
    z = _dot(ys.astype(BF16), gw_ref[0]) + gb_ref[0]
    o_ref[0] = (ys * _sigmoid(z)).astype(o_ref.dtype)


def _s5_tables(lam_re, lam_im, log_dt, b_re, b_im, c_re, c_im, d_skip, glu_w, glu_b):
    T, C, P = S5_CHUNK, S5_GROUP, S5_STATE
    G = lam_re.shape[0]
    lam = lax.complex(lam_re.astype(F32), lam_im.astype(F32))
    dt = jnp.exp(log_dt.astype(F32))[:, None]
    lam_bar = jnp.exp(lam * dt)
    b_bar = ((lam_bar - 1.0) / lam)[..., None] * lax.complex(b_re.astype(F32), b_im.astype(F32))
    c_mat = lax.complex(c_re.astype(F32), c_im.astype(F32))
    taus = jnp.arange(T + 1, dtype=F32)
    pw = jnp.exp((lam * dt)[:, None, :] * taus[None, :, None])
    kern = jnp.einsum('gcp,gtp,gpd->gtcd', c_mat, pw[:, :T], b_bar,
                      precision=HIGHEST).real
    tt = jnp.arange(T)
    shift = (tt[:, None, None] - tt[None, :, None] == tt[None, None, :]).astype(F32)
    toe = jnp.einsum('tsu,gucd->gtscd', shift, kern, precision=HIGHEST)
    m_t = toe.transpose(0, 2, 4, 1, 3).reshape(G, T * C, T * C).astype(BF16)
    hmat = (pw[:, :T][:, ::-1, :, None] * b_bar[:, None]).transpose(0, 1, 3, 2).reshape(G, T * C, P)
    emat = (c_mat[:, None] * pw[:, 1:][:, :, None, :]).reshape(G, T * C, P).transpose(0, 2, 1)
    a_re, a_im = pw[:, T].real, pw[:, T].imag
    rows8 = lambda a: jnp.broadcast_to(a[:, None], (G, SUBLANES, P))
    d_t = jnp.tile(d_skip.astype(F32), (1, T))[:, None]
    eye = jnp.eye(T, dtype=F32)
    gw = jnp.einsum('ts,gce->gtcse', eye, glu_w.astype(F32)).reshape(G, T * C, T * C).astype(BF16)
    gb = jnp.tile(glu_b.astype(F32), (1, T))[:, None]
    return (m_t, hmat.real.astype(BF16), hmat.imag.astype(BF16), emat.real.astype(BF16), (-emat.imag).astype(BF16),
            rows8(a_re), rows8(a_im), d_t, gw, gb)


def _s5(u, tables):
    B, S, _ = u.shape
    T, C, P, G = S5_CHUNK, S5_GROUP, S5_STATE, S5_GROUPS
    assert B == SUBLANES
    n_chunks = S // T
    rows = n_chunks * B
    ug = u.reshape(B, n_chunks, T, G, C).transpose(3, 1, 0, 2, 4).reshape(G, rows, T * C)
    per_g = lambda a: pl.BlockSpec((1,) + a.shape[1:], lambda g: (g, 0, 0))
    out = pl.pallas_call(
        functools.partial(_s5_kernel, n_chunks=n_chunks, batch=B),
        out_shape=jax.ShapeDtypeStruct((G, rows, T * C), BF16),
        grid=(G,),
        in_specs=[per_g(ug)] + [per_g(t) for t in tables],
        out_specs=per_g(ug),
        scratch_shapes=[pltpu.VMEM((rows, P), F32) for _ in range(4)],
        compiler_params=_cparams(("parallel",)),
        name="s5",
    )(ug, *tables)
    return out.reshape(G, n_chunks, B, T, C).transpose(2, 1, 3, 0, 4).reshape(B, S, G * C)


S5_LT = LANES // S5_GROUP
S5_PAIRS = S5_CHUNK // 2


def _s5s_kernel(u_ref, h_ref, e_ref, kk_ref, are_ref, aim_ref, d_ref, gw_ref, gb_ref, o_ref, xl_scr, x0_scr):
    n_chunks = u_ref.shape[1] // S5_CHUNK
    half = S5_LT * S5_STATE
    tok = lambda s: u_ref[0, pl.ds(s, n_chunks, stride=S5_CHUNK), :]
    u2 = [jnp.concatenate([tok(2 * q), tok(2 * q + 1)], axis=1) for q in range(S5_PAIRS)]
    u2b = [v.astype(BF16) for v in u2]
    xl_scr[...] = functools.reduce(lambda a, b: a + b, [_dot(u2b[q], h_ref[0, q]) for q in range(S5_PAIRS)])
    a_re = are_ref[0]
    a_im = aim_ref[0]

    def body(a, carry):
        re, im = carry
        x0_scr[pl.ds(a, 1), 0:half] = re
        x0_scr[pl.ds(a, 1), half:2 * half] = im
        return (a_re * re - a_im * im + xl_scr[pl.ds(a, 1), 0:half],
                a_re * im + a_im * re + xl_scr[pl.ds(a, 1), half:2 * half])

    zero = jnp.zeros((1, half), F32)
    lax.fori_loop(0, n_chunks, body, (zero, zero), unroll=8)
    x0 = x0_scr[...].astype(BF16)
    for p in range(S5_PAIRS):
        y = _dot(x0, e_ref[0, p]) + u2[p] * d_ref[0]
        for q in range(p + 1):
            y = y + _dot(u2b[q], kk_ref[0, p - q])
        ys = _gelu_tanh(y)
        out = ys * _sigmoid(_dot(ys.astype(BF16), gw_ref[0]) + gb_ref[0])
        o_ref[0, pl.ds(2 * p, n_chunks, stride=S5_CHUNK), :] = out[:, :LANES].astype(o_ref.dtype)
        o_ref[0, pl.ds(2 * p + 1, n_chunks, stride=S5_CHUNK), :] = out[:, LANES:].astype(o_ref.dtype)


def _s5s_tables(lam_re, lam_im, log_dt, b_re, b_im, c_re, c_im, d_skip, glu_w, glu_b):
    T, C, P, LT = S5_CHUNK, S5_GROUP, S5_STATE, S5_LT
    G = lam_re.shape[0]
    NT = G // LT
    lam = lax.complex(lam_re.astype(F32), lam_im.astype(F32))
    dt = jnp.exp(log_dt.astype(F32))[:, None]
    lam_bar = jnp.exp(lam * dt)
    b_bar = ((lam_bar - 1.0) / lam)[..., None] * lax.complex(b_re.astype(F32), b_im.astype(F32))
    c_mat = lax.complex(c_re.astype(F32), c_im.astype(F32))
    taus = jnp.arange(T + 1, dtype=F32)
    pw = jnp.exp((lam * dt)[:, None, :] * taus[None, :, None])
    eye = jnp.eye(LT, dtype=F32)
    tiles = lambda a: a.reshape((NT, LT) + a.shape[1:])

    kern = jnp.einsum('gcp,gtp,gpd->gtdc', c_mat, pw[:, :T], b_bar, precision=HIGHEST).real
    kblk = jnp.einsum('nitdc,ij->ntidjc', tiles(kern), eye).reshape(NT, T, LANES, LANES)
    kblk = jnp.concatenate([jnp.zeros_like(kblk[:, :1]), kblk], axis=1)
    kk = jnp.stack([jnp.concatenate([jnp.concatenate([kblk[:, 2 * d + 1], kblk[:, 2 * d + 2]], axis=2),
                                     jnp.concatenate([kblk[:, 2 * d], kblk[:, 2 * d + 1]], axis=2)], axis=1)
                    for d in range(T // 2)], axis=1)

    hmat = pw[:, :T][:, ::-1, :, None] * b_bar[:, None]

    def state_cols(m):
        return jnp.einsum('nispc,ij->nsicjp', tiles(m), eye).reshape(NT, T, LANES, LT * P)

    h = jnp.concatenate([state_cols(hmat.real), state_cols(hmat.imag)], axis=3)
    h2 = h.reshape(NT, T // 2, 2 * LANES, 2 * LT * P)

    emat = c_mat[:, None] * pw[:, 1:][:, :, None, :]

    def state_rows(m):
        return jnp.einsum('nitcp,ij->ntjpic', tiles(m), eye).reshape(NT, T, LT * P, LANES)

    e = jnp.concatenate([state_rows(emat.real), state_rows(-emat.imag)], axis=2)
    e2 = e.reshape(NT, T // 2, 2, 2 * LT * P, LANES).transpose(0, 1, 3, 2, 4).reshape(NT, T // 2, 2 * LT * P, 2 * LANES)

    a_re = pw[:, T].real.reshape(NT, 1, LT * P)
    a_im = pw[:, T].imag.reshape(NT, 1, LT * P)
    pair = lambda v: jnp.tile(v.astype(F32).reshape(NT, 1, LANES), (1, 1, 2))
    gwb = jnp.einsum('nice,ij->nicje', tiles(glu_w.astype(F32)), eye).reshape(NT, LANES, LANES)
    zeros = jnp.zeros_like(gwb)
    gw2 = jnp.concatenate([jnp.concatenate([gwb, zeros], axis=2), jnp.concatenate([zeros, gwb], axis=2)], axis=1)
    return (h2.astype(BF16), e2.astype(BF16), kk.astype(BF16), a_re, a_im, pair(d_skip), gw2.astype(BF16), pair(glu_b))


def _s5s(u, tables):
    B, S, W = u.shape
    NT = W // LANES
    n_chunks = S // S5_CHUNK
    per_tile = lambda a: pl.BlockSpec((1,) + a.shape[1:], lambda j, b: (j,) + (0,) * (a.ndim - 1))
    return pl.pallas_call(
        _s5s_kernel,
        out_shape=jax.ShapeDtypeStruct((B, S, W), F32),
        grid=(NT, B),
        in_specs=[pl.BlockSpec((1, S, LANES), lambda j, b: (b, 0, j))] + [per_tile(t) for t in tables],
        out_specs=pl.BlockSpec((1, S, LANES), lambda j, b: (b, 0, j)),
        scratch_shapes=[pltpu.VMEM((n_chunks, 2 * S5_LT * S5_STATE), F32) for _ in range(2)],
        compiler_params=_cparams(("parallel", "parallel")),
        name="s5",
    )(u, *tables)


def _compress_kernel(x_ref, plo_ref, phi_ref, w1_ref, b1_ref, w2_ref, b2_ref, o_ref):
    x = x_ref[0, 0]
    half = x.shape[1]
    w1 = w1_ref[0]
    lo = _dot((x + plo_ref[0]).astype(BF16), w1[:half])
    hi = _dot((x + phi_ref[0]).astype(BF16), w1[half:])
    rows = x.shape[0]
    hid = _gelu_tanh(lo + pltpu.roll(hi, rows - 1, 0) + b1_ref[0])
    o_ref[0, 0] = _dot(hid.astype(BF16), w2_ref[0]) + b2_ref[0]


def _compress(xg, pos, w1, b1, w2, b2):
    _, B, rows, width = xg.shape
    pos_flat = pos.reshape(2, 2, 1, width).astype(F32)
    sel = lambda shape: pl.BlockSpec((1,) + shape, lambda j, b: (j, 0, 0))
    return pl.pallas_call(
        _compress_kernel,
        out_shape=jax.ShapeDtypeStruct((2, B, rows, NSA_DH), F32),
        grid=(2, B),
        in_specs=[pl.BlockSpec((1, 1, rows, width), lambda j, b: (j, b, 0, 0)),
                  sel((1, width)), sel((1, width)),
                  sel((2 * width, CMP_HIDDEN)), sel((1, CMP_HIDDEN)),
                  sel((CMP_HIDDEN, NSA_DH)), sel((1, NSA_DH))],
        out_specs=pl.BlockSpec((1, 1, rows, NSA_DH), lambda j, b: (j, b, 0, 0)),
        compiler_params=_cparams(("parallel", "parallel")),
        name="nsa_compress",
    )(xg, pos_flat[:, 0], pos_flat[:, 1], w1.astype(BF16), b1[:, None].astype(F32),
      w2.astype(BF16), b2[:, None].astype(F32))


def _nsa_kernel(q_ref, gp_ref, bg_ref, kc_ref, vc_ref, ks_ref, vs_ref, kw_ref, vw_ref,
                grev_ref, selb_ref, winb_ref, ovt_ref, ex_ref, o_ref):
    T = ATT_TILE
    R, DH = NSA_R, NSA_DH
    qi = pl.program_id(2)
    q0 = qi * T
    qall = q_ref[0] * (DH ** -0.5 * LOG2E)
    t_col = q0 + lax.broadcasted_iota(jnp.int32, (T, 1), 0)
    n_cmp_pad = kc_ref.shape[2]
    n_sel = ovt_ref.shape[0]
    grp_rows = CMP_STRIDE

    q4 = jnp.concatenate([qall[:, r * DH:(r + 1) * DH] for r in range(R)], axis=0).astype(BF16)

    n_row = lax.broadcasted_iota(jnp.int32, (1, n_cmp_pad), 1)
    cmask = (t_col >= n_row * CMP_STRIDE + (CMP_BLOCK - 1))[None]
    bias = jnp.stack([jnp.concatenate(
        [pltpu.roll(grev_ref[r], (qi * (T // grp_rows) + al + 1) % n_cmp_pad, 1) for al in range(T // grp_rows)],
        axis=0) for r in range(R)], axis=0)
    s = jnp.where(cmask, _dot_nt(q4, kc_ref[0, 0]).reshape(R, T, n_cmp_pad) + bias, NEG)
    p = jnp.exp2(s - jnp.max(s, axis=2, keepdims=True))
    p = p / jnp.sum(p, axis=2, keepdims=True)
    p = jnp.where(cmask, p, 0.0)
    o_cmp = _dot(p.reshape(R * T, n_cmp_pad).astype(BF16), vc_ref[0, 0])
    psum = functools.reduce(lambda a, b: a + b, [p[r] for r in range(R)])

    imp_t = lax.dot_general(ovt_ref[...], psum, (((1,), (1,)), ((), ())), precision=HIGHEST,
                            preferred_element_type=F32)
    jj = lax.broadcasted_iota(jnp.int32, (n_sel, T), 0)
    blk_t = (q0 + lax.broadcasted_iota(jnp.int32, (1, T), 1)) // SEL_BLOCK
    forced = (jj == 0) | (jj == blk_t) | (jj == blk_t - 1)
    score = jnp.where(forced, FORCE, jnp.where(jj <= blk_t, imp_t, -1.0))
    n_blk = n_sel // SUBLANES
    rows = [score[v * SUBLANES:(v + 1) * SUBLANES] for v in range(n_blk)]
    cnts = [jnp.zeros((SUBLANES, T), F32) for _ in range(n_blk)]
    sub = lax.broadcasted_iota(jnp.int32, (SUBLANES, T), 0)
    for j2 in range(n_sel):
        c2 = score[j2:j2 + 1, :]
        for v in range(n_blk):
            lo = v * SUBLANES
            if lo > j2:
                beats = c2 >= rows[v]
            elif lo + SUBLANES - 1 <= j2:
                beats = c2 > rows[v]
            else:
                beats = (c2 > rows[v]) | ((c2 >= rows[v]) & (sub > j2 - lo))
            cnts[v] = cnts[v] + jnp.where(beats, 1.0, 0.0)
    cnt = jnp.concatenate(cnts, axis=0)
    sel_t = jnp.where((cnt < float(min(SEL_TOPK, n_sel))) & (jj <= blk_t), 1.0, 0.0)
    sel_q = jnp.concatenate([sel_t, jnp.zeros((LANES - n_sel, T), F32)], axis=0).T
    lane = lax.broadcasted_iota(jnp.int32, (T, LANES), 1)
    sel_aug = jnp.where(lane == n_sel, 1.0, sel_q).astype(BF16)

    n_far = selb_ref.shape[0] - 1
    n_win = winb_ref.shape[0] - 2
    CH = ex_ref.shape[2] // T

    def sel_body(kc, carry):
        m, acc = carry
        off = pl.multiple_of(kc * (CH * T), CH * T)
        k = ks_ref[0, 0, pl.ds(off, CH * T), :]
        v = vs_ref[0, 0, pl.ds(off, CH * T), :]
        s = _dot_nt(q4, k)
        mask = _dot(sel_aug, ex_ref[kc])
        subs = []
        for j in range(CH):
            d = jnp.clip(qi - (kc * CH + j), 0, n_far)
            sj = s[:, j * T:(j + 1) * T].reshape(R, T, T) + selb_ref[d] + mask[:, j * T:(j + 1) * T][None]
            subs.append(sj.reshape(R * T, T))
        m_new = jnp.maximum(m, jnp.max(functools.reduce(jnp.maximum, subs), axis=1, keepdims=True))
        pb = jnp.concatenate([jnp.exp2(sj - m_new).astype(BF16) for sj in subs], axis=1)
        return m_new, jnp.exp2(m - m_new) * acc + _dot(pb, v)

    _, acc = lax.fori_loop(0, qi // CH + 1, sel_body,
                           (jnp.full((R * T, 1), NEG, F32), jnp.zeros((R * T, LANES), F32)))
    o_sel = acc[:, :DH] / acc[:, DH:DH + 1]

    subs, vals = [], []
    for d in range(n_win + 1):
        off = pl.multiple_of(jnp.maximum(qi - d, 0) * T, T)
        tile = jnp.where(qi >= d, d, n_win + 1)
        subs.append((_dot_nt(q4, kw_ref[0, 0, pl.ds(off, T), :]).reshape(R, T, T) + winb_ref[tile]).reshape(R * T, T))
        vals.append(vw_ref[0, 0, pl.ds(off, T), :])
    m_w = jnp.max(functools.reduce(jnp.maximum, subs), axis=1, keepdims=True)
    acc = functools.reduce(lambda a, b: a + b,
                           [_dot(jnp.exp2(sj - m_w).astype(BF16), vj) for sj, vj in zip(subs, vals)])
    o_win = acc[:, :DH] / acc[:, DH:DH + 1]

    gates = _sigmoid(gp_ref[0] + bg_ref[...])
    gcol = lambda j: jnp.concatenate([gates[:, 3 * r + j:3 * r + j + 1] for r in range(R)], axis=0)
    out4 = gcol(0) * o_cmp + gcol(1) * o_sel + gcol(2) * o_win
    o_ref[0] = jnp.concatenate([out4[r * T:(r + 1) * T] for r in range(R)], axis=1)


def _t5_bucket(dist):
    dist = jnp.maximum(dist, 0)
    max_exact = REL_BUCKETS // 2
    log_ratio = jnp.log(jnp.maximum(dist, 1).astype(F32) / max_exact) / math.log(REL_MAX_DIST / max_exact)
    large = jnp.minimum(max_exact + (log_ratio * (REL_BUCKETS - max_exact)).astype(jnp.int32), REL_BUCKETS - 1)
    return jnp.where(dist < max_exact, dist, large)


def _nsa_tables(rel_bias, S):
    T = ATT_TILE
    table = rel_bias.astype(F32) * LOG2E
    ii = jnp.arange(T)
    delta = ii[:, None] - ii[None, :]

    def tile(off):
        return table[_t5_bucket(off * T + delta)].transpose(2, 0, 1)

    n_far = -(-REL_MAX_DIST // T) + 1
    selb = [tile(o) for o in range(n_far + 1)]
    selb[0] = selb[0] + jnp.where(delta >= 0, 0.0, NEG)[None]
    selb = jnp.stack(selb, axis=0)
    n_win = WINDOW // T
    winb = []
    for o in range(n_win + 1):
        dist = o * T + delta
        ok = (dist >= 0) & (dist < WINDOW)
        winb.append(tile(o) + jnp.where(ok, 0.0, NEG)[None])
    winb.append(jnp.full_like(winb[0], NEG))
    winb = jnp.stack(winb, axis=0)
    n_pad = S // CMP_STRIDE
    i16 = jnp.arange(CMP_STRIDE)
    dd = jnp.arange(n_pad)
    gdist = CMP_STRIDE * dd[None, :] + i16[:, None] - (CMP_BLOCK - 1)
    grev = table[_t5_bucket(gdist)].transpose(2, 0, 1)[:, :, ::-1]
    n_sel = S // SEL_BLOCK
    cmp_start = jnp.arange(n_pad) * CMP_STRIDE
    sel_start = jnp.arange(n_sel) * SEL_BLOCK
    overlap = jnp.clip(jnp.minimum(cmp_start[:, None] + CMP_BLOCK, sel_start[None] + SEL_BLOCK)
                       - jnp.maximum(cmp_start[:, None], sel_start[None]), 0).astype(F32) / CMP_BLOCK
    n_cmp = (S - CMP_BLOCK) // CMP_STRIDE + 1
    overlap_t = jnp.where((jnp.arange(n_pad) < n_cmp)[:, None], overlap, 0.0).T
    tk = min(SEL_CHUNK * T, S)
    kpos_blk = jnp.arange(S) // SEL_BLOCK
    rows = jnp.arange(LANES)[:, None]
    expand = jnp.where(rows == kpos_blk[None, :], BIG, jnp.where(rows == n_sel, -BIG, 0.0)).astype(BF16)
    expand = expand.reshape(LANES, S // tk, tk).transpose(1, 0, 2)
    return grev, selb, winb, overlap_t, expand


def _nsa_attention(q, gp, bg, kcmp, vcmp, ks, vs, kw, vw, tables):
    B, S, D = q.shape
    T = ATT_TILE
    grev, selb, winb, overlap_t, expand = tables
    gw = NSA_R * NSA_DH
    seq = lambda a: pl.BlockSpec((1, 1) + a.shape[2:], lambda b, g, i: (b, g, 0, 0))
    per_head = lambda a: pl.BlockSpec((NSA_R,) + a.shape[1:], lambda b, g, i: (g,) + (0,) * (a.ndim - 1))
    tiles = lambda a: pl.BlockSpec((a.shape[0], NSA_R) + a.shape[2:], lambda b, g, i: (0, g, 0, 0))
    full = lambda a: pl.BlockSpec(a.shape, lambda b, g, i: (0,) * a.ndim)
    return pl.pallas_call(
        _nsa_kernel,
        out_shape=jax.ShapeDtypeStruct((B, S, D), F32),
        grid=(B, NSA_KV, S // T),
        in_specs=[pl.BlockSpec((1, T, gw), lambda b, g, i: (b, i, g)),
                  pl.BlockSpec((1, T, LANES), lambda b, g, i: (b, i, g)),
                  pl.BlockSpec((1, LANES), lambda b, g, i: (0, g)),
                  seq(kcmp), seq(vcmp), seq(ks), seq(vs), seq(kw), seq(vw),
                  per_head(grev), tiles(selb), tiles(winb), full(overlap_t), full(expand)],
        out_specs=pl.BlockSpec((1, T, gw), lambda b, g, i: (b, i, g)),
        compiler_params=_cparams(("parallel", "parallel", "arbitrary")),
        name="nsa_attention",
    )(q, gp, bg, kcmp, vcmp, ks, vs, kw, vw, grev, selb, winb, overlap_t, expand)


def _nsa_proj_kernel(x_ref, g_ref, sh_ref, sc_ref, wq_ref, wk_ref, wv_ref, wg_ref,
                     q_ref, gp_ref, kc_ref, vc_ref, ks_ref, kw_ref, vst_ref, vwt_ref):
    KV, DH, T = NSA_KV, NSA_DH, ATT_TILE
    h = _modulated_norm(x_ref[0], g_ref[...], sh_ref[0], sc_ref[0]).astype(BF16)
    q_ref[0] = _dot(h, wq_ref[...])
    gp_ref[0] = _dot(h, wg_ref[...])
    k3 = _dot(h, wk_ref[...])
    v3 = _dot(h, wv_ref[...])
    vs_t = v3[:, KV_W:2 * KV_W].T.astype(BF16)
    vw_t = v3[:, 2 * KV_W:].T.astype(BF16)
    for g in range(KV):
        cols = slice(g * DH, (g + 1) * DH)
        kc_ref[0, g] = k3[:, cols].astype(BF16)
        vc_ref[0, g] = v3[:, cols].astype(BF16)
        ks_ref[0, g] = k3[:, KV_W + g * DH:KV_W + (g + 1) * DH].astype(BF16)
        kw_ref[0, g] = k3[:, 2 * KV_W + g * DH:2 * KV_W + (g + 1) * DH].astype(BF16)
        vst_ref[0, g, 0] = vs_t[cols]
        for j in range(vwt_ref.shape[2]):
            vwt_ref[0, g, j] = vw_t[cols, j * T:(j + 1) * T]


def _nsa_proj(x, g, shift, scale, weights):
    B, S, D = x.shape
    KV, DH, T = NSA_KV, NSA_DH, ATT_TILE
    tm = min(SEL_CHUNK * T, S)
    vec = pl.BlockSpec((1, 1, D), lambda b, i: (b, 0, 0))
    rows = lambda n: pl.BlockSpec((1, tm, n), lambda b, i: (b, i, 0))
    keys = pl.BlockSpec((1, KV, tm, DH), lambda b, i: (b, 0, i, 0))
    key_shape = jax.ShapeDtypeStruct((B, KV, S, DH), BF16)
    return pl.pallas_call(
        _nsa_proj_kernel,
        out_shape=[jax.ShapeDtypeStruct((B, S, D), F32), jax.ShapeDtypeStruct((B, S, KV * LANES), F32),
                   key_shape, key_shape, key_shape, key_shape,
                   jax.ShapeDtypeStruct((B, KV, S // tm, DH, tm), BF16),
                   jax.ShapeDtypeStruct((B, KV, S // T, DH, T), BF16)],
        grid=(B, S // tm),
        in_specs=[pl.BlockSpec((1, tm, D), lambda b, i: (b, i, 0)),
                  pl.BlockSpec((1, D), lambda b, i: (0, 0)), vec, vec]
                 + [pl.BlockSpec(w.shape, lambda b, i: (0, 0)) for w in weights],
        out_specs=[rows(D), rows(KV * LANES), keys, keys, keys, keys,
                   pl.BlockSpec((1, KV, 1, DH, tm), lambda b, i: (b, 0, i, 0, 0)),
                   pl.BlockSpec((1, KV, tm // T, DH, T), lambda b, i: (b, 0, i, 0, 0))],
        compiler_params=_cparams(("parallel", "parallel")),
        name="nsa_proj",
    )(x, g.reshape(1, D), shift, scale, *weights)


def _nsa_t_kernel(q_ref, gp_ref, bg_ref, kc_ref, vct_ref, ks_ref, vst_ref, kw_ref, vwt_ref,
                  cfar_ref, band_ref, selb_ref, winb_ref, ovt_ref, o_ref, s_scr, sel_scr, sbuf):
    T = ATT_TILE
    R, DH = NSA_R, NSA_DH
    qi = pl.program_id(2)
    q0 = qi * T
    n_pad = kc_ref.shape[2]
    n_sel = ovt_ref.shape[0]
    CH = vst_ref.shape[4] // T
    n_far = selb_ref.shape[0] - 1
    n_win = winb_ref.shape[0] - 2
    band_rows = band_ref.shape[2] - T // CMP_STRIDE * 2

    q_t = (q_ref[0] * (DH ** -0.5 * LOG2E)).T
    q4 = jnp.concatenate([q_t[r * DH:(r + 1) * DH] for r in range(R)], axis=1).astype(BF16)
    t_lane = q0 + lax.broadcasted_iota(jnp.int32, (1, R * T), 1) % T

    with_ones = lambda v_t: jnp.concatenate([v_t, jnp.ones_like(v_t)], axis=0)
    gates_t = _sigmoid(gp_ref[0] + bg_ref[...]).T
    gvec = lambda j: jnp.concatenate([gates_t[3 * r + j:3 * r + j + 1, :] for r in range(R)], axis=1)

    grp = T // CMP_STRIDE
    s_scr[0:n_pad, :] = _dot(kc_ref[0, 0], q4) + cfar_ref[0]
    s_scr[n_pad:n_pad + 2 * grp, :] = jnp.zeros((2 * grp, R * T), F32)
    r0 = jnp.maximum(qi * grp - 2 * grp, 0)
    x0 = r0 - (qi * grp - 2 * grp)
    r0 = pl.multiple_of(r0, SUBLANES)
    x0 = pl.multiple_of(x0, SUBLANES)
    s_scr[pl.ds(r0, band_rows), :] += band_ref[0, 0, pl.ds(x0, band_rows), :]
    lim = pl.multiple_of(qi * grp + 2 * grp, SUBLANES)
    s_scr[pl.ds(lim, n_pad), :] = jnp.full((n_pad, R * T), NEG, F32)

    w_subs, w_vals = [], []
    for d in range(n_win + 1):
        kt = jnp.maximum(qi - d, 0)
        off = pl.multiple_of(kt * T, T)
        tile = jnp.where(qi >= d, d, n_win + 1)
        w_subs.append(_dot(kw_ref[0, 0, pl.ds(off, T), :], q4) + winb_ref[tile, 0])
        w_vals.append(with_ones(vwt_ref[0, 0, kt]))

    s = s_scr[0:n_pad, :]
    e = jnp.exp2(s - jnp.max(s, axis=0, keepdims=True))
    inv = jnp.where(t_lane >= CMP_BLOCK - 1, 1.0 / jnp.sum(e, axis=0, keepdims=True), 0.0)
    p = e * inv
    o_cmp = _dot(vct_ref[0, 0], p.astype(BF16))
    psum = functools.reduce(lambda a, b: a + b, [p[:, r * T:(r + 1) * T] for r in range(R)])

    m_w = jnp.max(functools.reduce(jnp.maximum, w_subs), axis=0, keepdims=True)
    acc = functools.reduce(lambda a, b: a + b,
                           [_dot(vj, jnp.exp2(sj - m_w).astype(BF16)) for sj, vj in zip(w_subs, w_vals)])
    o_win = acc[:DH] * (1.0 / acc[DH:DH + 1])
    out_t = gvec(0) * o_cmp + gvec(2) * o_win

    imp_t = _dot(ovt_ref[...], psum, precision=HIGHEST)
    jj = lax.broadcasted_iota(jnp.int32, (n_sel, T), 0)
    blk_t = (q0 + lax.broadcasted_iota(jnp.int32, (1, T), 1)) // SEL_BLOCK
    forced = (jj == 0) | (jj == blk_t) | (jj == blk_t - 1)
    score = jnp.where(forced, FORCE, jnp.where(jj <= blk_t, imp_t, -1.0))
    n_blk = n_sel // SUBLANES
    rows = [score[v * SUBLANES:(v + 1) * SUBLANES] for v in range(n_blk)]
    cnts = [jnp.zeros((SUBLANES, T), F32) for _ in range(n_blk)]
    sub = lax.broadcasted_iota(jnp.int32, (SUBLANES, T), 0)
    for j2 in range(n_sel):
        c2 = score[j2:j2 + 1, :]
        for v in range(n_blk):
            lo = v * SUBLANES
            if lo > j2:
                beats = c2 >= rows[v]
            elif lo + SUBLANES - 1 <= j2:
                beats = c2 > rows[v]
            else:
                beats = (c2 > rows[v]) | ((c2 >= rows[v]) & (sub > j2 - lo))
            cnts[v] = cnts[v] + jnp.where(beats, 1.0, 0.0)
    cnt = jnp.concatenate(cnts, axis=0)
    chosen = (cnt < float(min(SEL_TOPK, n_sel))) & (jj <= blk_t)
    sel_scr[...] = jnp.where(chosen, 0.0, -BIG)

    def block_mask(kt):
        per_tile = T // SEL_BLOCK
        parts = [jnp.broadcast_to(sel_scr[pl.ds(kt * per_tile + i, 1), :], (SEL_BLOCK, T)) for i in range(per_tile)]
        m1 = jnp.concatenate(parts, axis=0)
        return jnp.concatenate([m1] * R, axis=1)

    def sel_scores(slot, kc):
        off = pl.multiple_of(kc * (CH * T), CH * T)
        s = _dot(ks_ref[0, 0, pl.ds(off, CH * T), :], q4)
        subs = []
        for j in range(CH):
            kt = kc * CH + j
            d = jnp.clip(qi - kt, 0, n_far)
            subs.append(s[j * T:(j + 1) * T] + selb_ref[d, 0] + block_mask(kt))
        s = jnp.concatenate(subs, axis=0)
        sbuf[slot] = s
        return jnp.max(s, axis=0, keepdims=True)

    def sel_consume(slot, kc, m, acc, m_cur):
        m_new = jnp.maximum(m, m_cur)
        pb = jnp.exp2(sbuf[slot] - m_new).astype(BF16)
        return m_new, jnp.exp2(m - m_new) * acc + _dot(with_ones(vst_ref[0, 0, kc]), pb)

    last_chunk = vst_ref.shape[2] - 1

    def sel_body(i, carry):
        m, acc, m_even = carry
        m_odd = sel_scores(1, 2 * i + 1)
        m, acc = sel_consume(0, 2 * i, m, acc, m_even)
        m_even = sel_scores(0, jnp.minimum(2 * i + 2, last_chunk))
        m, acc = sel_consume(1, 2 * i + 1, m, acc, m_odd)
        return m, acc, m_even

    n_chunks = qi // CH + 1
    _, acc, _ = lax.fori_loop(0, (n_chunks + 1) // 2, sel_body,
                              (jnp.full((1, R * T), NEG, F32), jnp.zeros((2 * DH, R * T), F32), sel_scores(0, 0)))
    out_t = out_t + gvec(1) * (acc[:DH] * (1.0 / acc[DH:DH + 1]))
    for pr in range(R // 2):
        pair = jnp.concatenate([out_t[:, (2 * pr) * T:(2 * pr + 1) * T],
                                out_t[:, (2 * pr + 1) * T:(2 * pr + 2) * T]], axis=0)
        o_ref[0, :, pr * 2 * DH:(pr + 1) * 2 * DH] = pair.T


def _bias_lookup(table, dist):
    idx = _t5_bucket(dist)
    out = jnp.zeros(idx.shape + (table.shape[1],), F32)
    for k in range(table.shape[0]):
        out = out + jnp.where((idx == k)[..., None], table[k], 0.0)
    return out


def _nsa_t_tables(rel_bias, S):
    T, R, KV = ATT_TILE, NSA_R, NSA_KV
    table = rel_bias.astype(F32) * LOG2E
    ii = jnp.arange(T)
    delta = ii[None, :] - ii[:, None]

    def lanes(a):
        a = jnp.moveaxis(a, -1, 0)
        a = a.reshape((KV, R) + a.shape[1:])
        return jnp.moveaxis(a, 1, 2).reshape(KV, a.shape[2], R * a.shape[3])

    def tile(off):
        return lanes(_bias_lookup(table, off * T + delta))

    mask4 = lambda ok: jnp.tile(jnp.where(ok, 0.0, NEG), (1, R))[None]
    n_far = -(-REL_MAX_DIST // T) + 1
    selb = [tile(o) for o in range(n_far + 1)]
    selb[0] = selb[0] + mask4(delta >= 0)
    selb = jnp.stack(selb, axis=0)
    n_win = WINDOW // T
    winb = [tile(o) + mask4((o * T + delta >= 0) & (o * T + delta < WINDOW)) for o in range(n_win + 1)]
    winb.append(jnp.full_like(winb[0], NEG))
    winb = jnp.stack(winb, axis=0)

    grp = T // CMP_STRIDE
    far = _bias_lookup(table, jnp.asarray(2 * REL_MAX_DIST))
    xx = jnp.arange(4 * grp)
    bdist = ii[None, :] - CMP_STRIDE * (xx[:, None] - 2 * grp) - (CMP_BLOCK - 1)
    band = jnp.where((bdist >= 0)[..., None], _bias_lookup(table, bdist) - far, NEG)
    band = jnp.concatenate([lanes(band), jnp.zeros((KV, 2 * grp, R * T), F32)], axis=1)[:, None]
    cfar = jnp.repeat(far.reshape(KV, R), T, axis=1)[:, None]

    n_pad = S // CMP_STRIDE
    n_sel = S // SEL_BLOCK
    cmp_start = jnp.arange(n_pad) * CMP_STRIDE
    sel_start = jnp.arange(n_sel) * SEL_BLOCK
    overlap = jnp.clip(jnp.minimum(cmp_start[:, None] + CMP_BLOCK, sel_start[None] + SEL_BLOCK)
                       - jnp.maximum(cmp_start[:, None], sel_start[None]), 0).astype(F32) / CMP_BLOCK
    n_cmp = (S - CMP_BLOCK) // CMP_STRIDE + 1
    overlap_t = jnp.where((jnp.arange(n_pad) < n_cmp)[:, None], overlap, 0.0).T
    return cfar, band, selb, winb, overlap_t


def _nsa_t_attention(q, gp, bg, kcmp, vcmp_t, ks, vs_t, kw, vw_t, tables):
    B, S, D = q.shape
    T = ATT_TILE
    cfar, band, selb, winb, overlap_t = tables
    gw = NSA_R * NSA_DH
    n_pad = kcmp.shape[2]
    seq = lambda a: pl.BlockSpec((1, 1) + a.shape[2:], lambda b, g, i: (b, g) + (0,) * (a.ndim - 2))
    grp = lambda a: pl.BlockSpec((1,) + a.shape[1:], lambda b, g, i: (g,) + (0,) * (a.ndim - 1))
    tiles = lambda a: pl.BlockSpec((a.shape[0], 1) + a.shape[2:], lambda b, g, i: (0, g, 0, 0))
    full = lambda a: pl.BlockSpec(a.shape, lambda b, g, i: (0,) * a.ndim)
    return pl.pallas_call(
        _nsa_t_kernel,
        out_shape=jax.ShapeDtypeStruct((B, S, D), F32),
        grid=(B, NSA_KV, S // T),
        in_specs=[pl.BlockSpec((1, T, gw), lambda b, g, i: (b, i, g)),
                  pl.BlockSpec((1, T, LANES), lambda b, g, i: (b, i, g)),
                  pl.BlockSpec((1, LANES), lambda b, g, i: (0, g)),
                  seq(kcmp), seq(vcmp_t), seq(ks), seq(vs_t), seq(kw), seq(vw_t),
                  grp(cfar), grp(band), tiles(selb), tiles(winb), full(overlap_t)],
        out_specs=pl.BlockSpec((1, T, gw), lambda b, g, i: (b, i, g)),
        scratch_shapes=[pltpu.VMEM((2 * n_pad + 2 * (T // CMP_STRIDE), NSA_R * T), F32),
                        pltpu.VMEM((S // SEL_BLOCK, T), F32),
                        pltpu.VMEM((2, vs_t.shape[4], NSA_R * T), F32)],
        compiler_params=_cparams(("parallel", "parallel", "arbitrary")),
        name="nsa_attention",
    )(q, gp, bg, kcmp, vcmp_t, ks, vs_t, kw, vw_t, cfar, band, selb, winb, overlap_t)


def _moe_kernel(x_ref, g_ref, sh_ref, sc_ref, gate_ref, wr_ref, br_ref, wg_ref, wu_ref, wd_ref, fg_ref,
                o_ref, hb_scr, rt_scr, acc_scr, *, final):
    NG, PG, FH = MOE_GROUPS, MOE_PER_GROUP, MOE_HIDDEN
    c = pl.program_id(2)

    @pl.when(c == 0)
    def _():
        h = _modulated_norm(x_ref[0], g_ref[...], sh_ref[0], sc_ref[0])
        h_hi = h.astype(BF16)
        hb_scr[...] = h_hi
        h_lo = (h - h_hi.astype(F32)).astype(BF16)
        logits = (_dot(h_hi, wr_ref[0]) + _dot(h_lo, wr_ref[0]) + _dot(h_hi, wr_ref[1]) + br_ref[...]).T
        gl = [logits[NG * PG + g:NG * PG + g + 1, :] for g in range(NG)]
        gmax = functools.reduce(jnp.maximum, gl)
        gtop = jnp.full_like(gmax, float(NG - 1))
        for g in reversed(range(NG - 1)):
            gtop = jnp.where(gl[g] == gmax, float(g), gtop)
        p_g = 1.0 / functools.reduce(lambda a, b: a + b, [jnp.exp(v - gmax) for v in gl])
        a = []
        for j in range(PG):
            v = logits[(NG - 1) * PG + j:(NG - 1) * PG + j + 1, :]
            for g in reversed(range(NG - 1)):
                v = jnp.where(gtop == float(g), logits[g * PG + j:g * PG + j + 1, :], v)
            a.append(v)

        def first_max(vals):
            vmax = functools.reduce(jnp.maximum, vals)
            taken = jnp.zeros_like(vmax) > 1.0
            hits = []
            for v in vals:
                hit = (v == vmax) & jnp.logical_not(taken)
                taken = taken | hit
                hits.append(hit)
            return vmax, hits

        v1, hit1 = first_max(a)
        rest = [jnp.where(hh, -jnp.inf, v) for hh, v in zip(hit1, a)]
        v2, hit2 = first_max(rest)
        e2 = jnp.exp(v2 - v1)
        w1 = p_g / (1.0 + e2)
        w2 = p_g * e2 / (1.0 + e2)
        tm = gtop.shape[1]
        row = lax.broadcasted_iota(jnp.int32, (SUBLANES, tm), 0)
        rt = jnp.where(row == PG, gtop, 0.0)
        for j in range(PG):
            wj = jnp.where(hit1[j], w1, jnp.where(hit2[j], w2, 0.0))
            rt = jnp.where(row == j, wj, rt)
        rt_scr[...] = jnp.concatenate([rt, jnp.zeros((LANES - SUBLANES, tm), F32)], axis=0).T

    hb = hb_scr[...]
    rt = rt_scr[...]
    in_group = rt[:, PG:PG + 1] == c.astype(F32)
    hid = _silu(_dot(hb, wg_ref[0])) * _dot(hb, wu_ref[0])
    parts = [hid[:, j * FH:(j + 1) * FH] * jnp.where(in_group, rt[:, j:j + 1], 0.0) for j in range(PG)]
    contrib = _dot(jnp.concatenate(parts, axis=1).astype(BF16), wd_ref[0])

    @pl.when(c == 0)
    def _():
        acc_scr[...] = contrib

    @pl.when(c > 0)
    def _():
        acc_scr[...] += contrib

    @pl.when(c == NG - 1)
    def _():
        y = x_ref[0] + gate_ref[0] * acc_scr[...]
        if final:
            y = y * lax.rsqrt(jnp.mean(y * y, axis=-1, keepdims=True) + EPS) * fg_ref[...]
        o_ref[0] = y


def _moe(x, g, shift, scale, gate, wg, bg, we, be, w_gate, w_up, w_down, final_g, final, tm=512):
    B, S, D = x.shape
    NG, PG, FH = MOE_GROUPS, MOE_PER_GROUP, MOE_HIDDEN
    wr = jnp.zeros((D, LANES), F32)
    wr = wr.at[:, :NG * PG].set(we.reshape(D, NG * PG).astype(F32)).at[:, NG * PG:NG * PG + NG].set(wg.astype(F32))
    br = jnp.zeros((1, LANES), F32)
    br = br.at[0, :NG * PG].set(be.reshape(NG * PG).astype(F32)).at[0, NG * PG:NG * PG + NG].set(bg.astype(F32))
    wr_hi = wr.astype(BF16)
    wr = jnp.stack([wr_hi, (wr - wr_hi.astype(F32)).astype(BF16)])
    grp = lambda w: w.reshape(NG, PG, D, FH).transpose(0, 2, 1, 3).reshape(NG, D, PG * FH).astype(BF16)
    wd = w_down.reshape(NG, PG * FH, D).astype(BF16)
    vec = pl.BlockSpec((1, 1, D), lambda b, i, c: (b, 0, 0))
    row = pl.BlockSpec((1, D), lambda b, i, c: (0, 0))
    wspec = lambda k, n: pl.BlockSpec((1, k, n), lambda b, i, c: (c, 0, 0))
    return pl.pallas_call(
        functools.partial(_moe_kernel, final=final),
        out_shape=jax.ShapeDtypeStruct((B, S, D), F32),
        grid=(B, S // tm, NG),
        in_specs=[pl.BlockSpec((1, tm, D), lambda b, i, c: (b, i, 0)), row, vec, vec, vec,
                  pl.BlockSpec((2, D, LANES), lambda b, i, c: (0, 0, 0)),
                  pl.BlockSpec((1, LANES), lambda b, i, c: (0, 0)),
                  wspec(D, PG * FH), wspec(D, PG * FH), wspec(PG * FH, D), row],
        out_specs=pl.BlockSpec((1, tm, D), lambda b, i, c: (b, i, 0)),
        scratch_shapes=[pltpu.VMEM((tm, D), BF16), pltpu.VMEM((tm, LANES), F32), pltpu.VMEM((tm, D), F32)],
        compiler_params=_cparams(("parallel", "parallel", "arbitrary")),
        name="moe",
    )(x, g.reshape(1, D), shift, scale, gate, wr, br, grp(w_gate), grp(w_up), wd, final_g.reshape(1, D))


def _mlstm_s5_layer(x, g, shift, scale, gate, w_in, conv_w, b_i, b_f, head_g, s5_params, w_out):
    H = MLSTM_HEADS
    A = MIX_A
    w_if = jnp.zeros((D_MODEL, LANES), F32).at[:, :2 * H].set(w_in[:, 4 * A:4 * A + 2 * H])
    weights = [w_in[:, :2 * A], w_in[:, 2 * A:4 * A], w_if, w_in[:, 4 * A + 2 * H:]]
    qk, vo, ifg, u = _norm_matmul(x, g, shift, scale, [w.astype(BF16) for w in weights], [F32] * 4)
    gate_bias = jnp.zeros((1, LANES), F32).at[0, :H].set(b_i.astype(F32)).at[0, H:2 * H].set(b_f.astype(F32))
    hm = _mlstm(qk, vo, ifg, conv_w.astype(F32), gate_bias, head_g.reshape(1, A).astype(F32))
    ys = _s5s(u, _s5s_tables(*s5_params))
    w_out = w_out.astype(BF16)
    return _out_residual(x, gate, [hm, ys], [w_out[:A], w_out[A:]])


def _nsa_layer(x, g, shift, scale, gate, w_in, b_gate, cmp_pos, cmp_w1, cmp_b1, cmp_w2, cmp_b2, rel_bias, w_out):
    B, S, D = x.shape
    KV, R, DH = NSA_KV, NSA_R, NSA_DH
    w_g = jnp.zeros((D, KV, LANES), F32).at[:, :, :3 * R].set(w_in[:, D + 6 * KV_W:].reshape(D, KV, 3 * R))
    b_g = jnp.zeros((KV, LANES), F32).at[:, :3 * R].set(b_gate.reshape(KV, 3 * R).astype(F32))
    kv_cols = lambda i: w_in[:, D + i * KV_W:D + (i + 1) * KV_W]
    w_k = jnp.concatenate([kv_cols(0), kv_cols(2), kv_cols(4)], axis=1)
    w_v = jnp.concatenate([kv_cols(1), kv_cols(3), kv_cols(5)], axis=1)
    weights = [w_in[:, :D], w_k, w_v, w_g.reshape(D, KV * LANES)]
    q, gp, kc, vc, ks, kw, vs_t, vw_t = _nsa_proj(x, g, shift, scale, [w.astype(BF16) for w in weights])
    grp = CMP_STRIDE
    xg = jnp.stack([kc, vc]).reshape(2, B, KV * S // grp, grp * DH)
    cmp = _compress(xg, cmp_pos, cmp_w1, cmp_b1, cmp_w2, cmp_b2).reshape(2, B, KV, S // grp, DH).astype(BF16)
    out = _nsa_t_attention(q, gp, b_g.reshape(1, KV * LANES), cmp[0], cmp[1].transpose(0, 1, 3, 2),
                           ks, vs_t, kw, vw_t, _nsa_t_tables(rel_bias, S))
    return _out_residual(x, gate, [out], [w_out.astype(BF16)])


def kernel(x, c, rel_bias, ada_w, ada_b, norm_g, final_g,
           a_w_in, a_conv, a_b_i, a_b_f, a_head_g,
           s5_lam_re, s5_lam_im, s5_log_dt, s5_b_re, s5_b_im, s5_c_re, s5_c_im,
           s5_d, s5_glu_w, s5_glu_b, a_w_out,
           n_w_in, n_b_gate, n_cmp_pos, n_cmp_w1, n_cmp_b1, n_cmp_w2, n_cmp_b2, n_w_out,
           r_grp_w, r_grp_b, r_exp_w, r_exp_b, e_w_gate, e_w_up, e_w_down):
    B, S, D = x.shape
    mod = _ada_mod(c, ada_w, ada_b).reshape(DEPTH, 2, B, 1, 3 * D)
    split = lambda m: (m[..., :D], m[..., D:2 * D], m[..., 2 * D:])
    for layer in range(DEPTH):
        shift, scale, gate = split(mod[layer, 0])
        j = layer // 2
        if layer % 2 == 0:
            s5_params = (s5_lam_re[j], s5_lam_im[j], s5_log_dt[j], s5_b_re[j], s5_b_im[j],
                         s5_c_re[j], s5_c_im[j], s5_d[j], s5_glu_w[j], s5_glu_b[j])
            x = _mlstm_s5_layer(x, norm_g[layer, 0], shift, scale, gate, a_w_in[j], a_conv[j], a_b_i[j], a_b_f[j],
                                a_head_g[j], s5_params, a_w_out[j])
        else:
            x = _nsa_layer(x, norm_g[layer, 0], shift, scale, gate, n_w_in[j], n_b_gate[j], n_cmp_pos[j],
                           n_cmp_w1[j], n_cmp_b1[j], n_cmp_w2[j], n_cmp_b2[j], rel_bias, n_w_out[j])
        shift, scale, gate = split(mod[layer, 1])
        x = _moe(x, norm_g[layer, 1], shift, scale, gate, r_grp_w[layer], r_grp_b[layer], r_exp_w[layer],
                 r_exp_b[layer], e_w_gate[layer], e_w_up[layer], e_w_down[layer], final_g,
                 final=(layer == DEPTH - 1))
    return x
```

```python
import functools
import math

import jax
import jax.numpy as jnp
from jax import lax
from jax.experimental import pallas as pl
from jax.experimental.pallas import tpu as pltpu

F32 = jnp.float32
BF16 = jnp.bfloat16
HIGHEST = lax.Precision.HIGHEST

D_MODEL = 1024
DEPTH = 2
MIX_A = 512
MLSTM_HEADS = 4
MLSTM_DH = MIX_A // MLSTM_HEADS
MLSTM_CHUNK = 128
CONV_K = 4
MIX_B = D_MODEL - MIX_A
S5_GROUP = 16
S5_GROUPS = MIX_B // S5_GROUP
S5_STATE = 64
S5_CHUNK = 16
NSA_HEADS = 16
NSA_KV = 4
NSA_R = NSA_HEADS // NSA_KV
NSA_DH = D_MODEL // NSA_HEADS
KV_W = NSA_KV * NSA_DH
CMP_BLOCK = 32
CMP_STRIDE = 16
CMP_HIDDEN = 256
SEL_BLOCK = 64
SEL_TOPK = 16
WINDOW = 512
FORCE = 1e9
REL_BUCKETS = 32
REL_MAX_DIST = 128
MOE_GROUPS = 4
MOE_PER_GROUP = 4
MOE_HIDDEN = 256
EPS = 1e-6
NEG = -1e30
BIG = 1e30
LOG2E = math.log2(math.e)
SEL_CHUNK = 1

LANES = 128
SUBLANES = 8
ATT_TILE = 256
VMEM_LIMIT = 56 * 1024 * 1024


def _cparams(sem):
    return pltpu.CompilerParams(dimension_semantics=sem, vmem_limit_bytes=VMEM_LIMIT)


def _dot(a, b, precision=None):
    return jnp.dot(a, b, preferred_element_type=F32, precision=precision)


def _dot_nt(a, b):
    return lax.dot_general(a, b, (((1,), (1,)), ((), ())), preferred_element_type=F32)


def _sigmoid(x):
    return 1.0 / (1.0 + jnp.exp(-x))


def _silu(x):
    return x * _sigmoid(x)


def _gelu_tanh(x):
    return 0.5 * x * (1.0 + jnp.tanh(math.sqrt(2.0 / math.pi) * (x + 0.044715 * (x * x * x))))


def _modulated_norm(x, g, shift, scale):
    y = x * lax.rsqrt(jnp.mean(x * x, axis=-1, keepdims=True) + EPS) * g
    return y * (1.0 + scale) + shift


def _ada_kernel(c_ref, w_ref, b_ref, o_ref):
    c = c_ref[...]
    o_ref[0] = _dot(_silu(c), w_ref[0]) + b_ref[0]


def _ada_mod(c, ada_w, ada_b):
    B, D = c.shape
    n_mod = ada_w.shape[0] * ada_w.shape[1]
    w = ada_w.reshape(n_mod, D, 3 * D)
    b = ada_b.reshape(n_mod, 1, 3 * D)
    tn = 1024
    return pl.pallas_call(
        _ada_kernel,
        out_shape=jax.ShapeDtypeStruct((n_mod, B, 3 * D), F32),
        grid=(n_mod, 3 * D // tn),
        in_specs=[pl.BlockSpec((B, D), lambda i, j: (0, 0)),
                  pl.BlockSpec((1, D, tn), lambda i, j: (i, 0, j)),
                  pl.BlockSpec((1, 1, tn), lambda i, j: (i, 0, j))],
        out_specs=pl.BlockSpec((1, B, tn), lambda i, j: (i, 0, j)),
        compiler_params=_cparams(("parallel", "parallel")),
        name="ada_mod",
    )(c, w, b)


def _norm_mm_kernel(*refs, n_w):
    x_ref, g_ref, sh_ref, sc_ref = refs[:4]
    w_refs = refs[4:4 + n_w]
    o_refs = refs[4 + n_w:]
    h = _modulated_norm(x_ref[0], g_ref[...], sh_ref[0], sc_ref[0]).astype(BF16)
    for w_ref, o_ref in zip(w_refs, o_refs):
        o_ref[0] = _dot(h, w_ref[...]).astype(o_ref.dtype)


def _norm_matmul(x, g, shift, scale, weights, out_dtypes, tm=512):
    B, S, D = x.shape
    n_w = len(weights)
    vec = pl.BlockSpec((1, 1, D), lambda b, i: (b, 0, 0))
    in_specs = [pl.BlockSpec((1, tm, D), lambda b, i: (b, i, 0)),
                pl.BlockSpec((1, D), lambda b, i: (0, 0)), vec, vec]
    in_specs += [pl.BlockSpec(w.shape, lambda b, i: (0, 0)) for w in weights]
    return pl.pallas_call(
        functools.partial(_norm_mm_kernel, n_w=n_w),
        out_shape=[jax.ShapeDtypeStruct((B, S, w.shape[1]), dt) for w, dt in zip(weights, out_dtypes)],
        grid=(B, S // tm),
        in_specs=in_specs,
        out_specs=[pl.BlockSpec((1, tm, w.shape[1]), lambda b, i: (b, i, 0)) for w in weights],
        compiler_params=_cparams(("parallel", "parallel")),
        name="norm_matmul",
    )(x, g.reshape(1, D), shift, scale, *weights)


def _out_res_kernel(*refs, n_in):
    x_ref, gate_ref = refs[:2]
    a_refs = refs[2:2 + n_in]
    w_refs = refs[2 + n_in:2 + 2 * n_in]
    o_ref = refs[2 + 2 * n_in]
    acc = None
    for a_ref, w_ref in zip(a_refs, w_refs):
        t = _dot(a_ref[0].astype(BF16), w_ref[...])
        acc = t if acc is None else acc + t
    o_ref[0] = x_ref[0] + gate_ref[0] * acc


def _out_residual(x, gate, acts, weights, tm=512):
    B, S, D = x.shape
    n_in = len(acts)
    in_specs = [pl.BlockSpec((1, tm, D), lambda b, i: (b, i, 0)),
                pl.BlockSpec((1, 1, D), lambda b, i: (b, 0, 0))]
    in_specs += [pl.BlockSpec((1, tm, a.shape[2]), lambda b, i: (b, i, 0)) for a in acts]
    in_specs += [pl.BlockSpec(w.shape, lambda b, i: (0, 0)) for w in weights]
    return pl.pallas_call(
        functools.partial(_out_res_kernel, n_in=n_in),
        out_shape=jax.ShapeDtypeStruct((B, S, D), F32),
        grid=(B, S // tm),
        in_specs=in_specs,
        out_specs=pl.BlockSpec((1, tm, D), lambda b, i: (b, i, 0)),
        compiler_params=_cparams(("parallel", "parallel")),
        name="out_residual",
    )(x, gate, *acts, *weights)


def _mlstm_kernel(qk_ref, vo_ref, if_ref, cw_ref, gb_ref, hg_ref, tril_ref, o_ref,
                  xbuf, c_scr, n_scr, m_scr):
    L, H, DH = MLSTM_CHUNK, MLSTM_HEADS, MLSTM_DH
    pad = SUBLANES

    @pl.when(pl.program_id(1) == 0)
    def _():
        xbuf[0:pad, :] = jnp.zeros((pad, 2 * MIX_A), F32)
        c_scr[...] = jnp.zeros_like(c_scr)
        n_scr[...] = jnp.zeros_like(n_scr)
        m_scr[...] = jnp.zeros_like(m_scr)

    xbuf[pad:pad + L, :] = qk_ref[0]
    cw = cw_ref[...]
    conv = None
    for j in range(CONV_K):
        lo = pad - (CONV_K - 1) + j
        t = xbuf[lo:lo + L, :] * cw[j:j + 1, :]
        conv = t if conv is None else conv + t
    xbuf[0:pad, :] = xbuf[L:L + pad, :]
    qk = _silu(conv)
    q = qk[:, :MIX_A]
    k = qk[:, MIX_A:] * (DH ** -0.5)
    vo = vo_ref[0]
    v = vo[:, :MIX_A]
    o_pre = vo[:, MIX_A:]

    ifb = if_ref[0] + gb_ref[...]
    lf = jnp.minimum(ifb, 0.0) - jnp.log1p(jnp.exp(-jnp.abs(ifb)))
    bcs = _dot(tril_ref[...], lf, precision=HIGHEST)
    ifb_t = ifb.T
    bcs_t = bcs.T
    row = lax.broadcasted_iota(jnp.int32, (L, L), 0)
    col = lax.broadcasted_iota(jnp.int32, (L, L), 1)
    causal = col <= row

    outs = []
    for h in range(H):
        sl = slice(h * DH, (h + 1) * DH)
        qh, kh, vh = q[:, sl], k[:, sl], v[:, sl]
        qb, kb = qh.astype(BF16), kh.astype(BF16)
        b_col = bcs[:, H + h:H + h + 1]
        b_row = bcs_t[H + h:H + h + 1, :]
        li_col = ifb[:, h:h + 1]
        li_row = ifb_t[h:h + 1, :]
        b_last = b_col[L - 1:L, :]
        m0 = m_scr[h][:, 0:1]
        c0 = c_scr[h]
        n0 = n_scr[h]

        log_d = jnp.where(causal, b_col - b_row + li_row, NEG)
        log_inter = b_col + m0
        m_t = jnp.maximum(log_inter, jnp.max(log_d, axis=1, keepdims=True))
        dmat = jnp.exp(log_d - m_t)
        a_inter = jnp.exp(log_inter - m_t)
        s = _dot_nt(qb, kb) * dmat
        num = _dot(s.astype(BF16), vh.astype(BF16)) + a_inter * _dot_nt(qb, c0.astype(BF16))
        den = jnp.sum(s, axis=1, keepdims=True) + a_inter * jnp.sum(qh * n0, axis=1, keepdims=True)
        hh = num / jnp.maximum(jnp.abs(den), jnp.exp(-m_t))

        w_col = b_last - b_col + li_col
        m_loc = jnp.max(w_col, axis=0, keepdims=True)
        e = jnp.exp(w_col - m_loc)
        c_loc = _dot((vh * e).T.astype(BF16), kb)
        n_loc = jnp.sum(kh * e, axis=0, keepdims=True)
        m_new = jnp.maximum(b_last + m0, m_loc)
        a = jnp.exp(b_last + m0 - m_new)
        sc = jnp.exp(m_loc - m_new)
        c_scr[h] = a * c0 + sc * c_loc
        n_scr[h] = a * n0 + sc * n_loc
        m_scr[h] = jnp.broadcast_to(m_new, (1, LANES))

        outs.append(hh * lax.rsqrt(jnp.mean(hh * hh, axis=1, keepdims=True) + EPS))
    hm = jnp.concatenate(outs, axis=1)
    o_ref[0] = _sigmoid(o_pre) * (hm * hg_ref[...])


def _mlstm(qk, vo, ifg, conv_w, gate_bias, head_g):
    B, S, _ = qk.shape
    L, H, DH = MLSTM_CHUNK, MLSTM_HEADS, MLSTM_DH
    tril = jnp.tril(jnp.ones((L, L), F32))
    return pl.pallas_call(
        _mlstm_kernel,
        out_shape=jax.ShapeDtypeStruct((B, S, MIX_A), F32),
        grid=(B, S // L),
        in_specs=[pl.BlockSpec((1, L, 2 * MIX_A), lambda b, c: (b, c, 0)),
                  pl.BlockSpec((1, L, 2 * MIX_A), lambda b, c: (b, c, 0)),
                  pl.BlockSpec((1, L, LANES), lambda b, c: (b, c, 0)),
                  pl.BlockSpec((CONV_K, 2 * MIX_A), lambda b, c: (0, 0)),
                  pl.BlockSpec((1, LANES), lambda b, c: (0, 0)),
                  pl.BlockSpec((1, MIX_A), lambda b, c: (0, 0)),
                  pl.BlockSpec((L, L), lambda b, c: (0, 0))],
        out_specs=pl.BlockSpec((1, L, MIX_A), lambda b, c: (b, c, 0)),
        scratch_shapes=[pltpu.VMEM((L + SUBLANES, 2 * MIX_A), F32),
                        pltpu.VMEM((H, DH, DH), F32),
                        pltpu.VMEM((H, 1, DH), F32),
                        pltpu.VMEM((H, 1, LANES), F32)],
        compiler_params=_cparams(("parallel", "arbitrary")),
        name="mlstm",
    )(qk, vo, ifg, conv_w, gate_bias, head_g, tril)


def _s5_kernel(u_ref, m_ref, hre_ref, him_ref, ere_ref, eim_ref, are_ref, aim_ref, d_ref, gw_ref, gb_ref, o_ref,
               xl_re, xl_im, x0_re, x0_im, *, n_chunks, batch):
    u = u_ref[0]
    xl_re[...] = _dot(u, hre_ref[0])
    xl_im[...] = _dot(u, him_ref[0])
    a_re = are_ref[0]
    a_im = aim_ref[0]

    def body(i, carry):
        re, im = carry
        r = pl.multiple_of(i * batch, batch)
        x0_re[pl.ds(r, batch), :] = re
        x0_im[pl.ds(r, batch), :] = im
        return (a_re * re - a_im * im + xl_re[pl.ds(r, batch), :],
                a_re * im + a_im * re + xl_im[pl.ds(r, batch), :])

    zero = jnp.zeros((batch, S5_STATE), F32)
    lax.fori_loop(0, n_chunks, body, (zero, zero), unroll=8)
    y = (_dot(u, m_ref[0]) + _dot(x0_re[...].astype(BF16), ere_ref[0]) + _dot(x0_im[...].astype(BF16), eim_ref[0])
         + u.astype(F32) * d_ref[0])
    ys = _gelu_tanh(y)
    z = _dot(ys.astype(BF16), gw_ref[0]) + gb_ref[0]
    o_ref[0] = (ys * _sigmoid(z)).astype(o_ref.dtype)


def _s5_tables(lam_re, lam_im, log_dt, b_re, b_im, c_re, c_im, d_skip, glu_w, glu_b):
    T, C, P = S5_CHUNK, S5_GROUP, S5_STATE
    G = lam_re.shape[0]
    lam = lax.complex(lam_re.astype(F32), lam_im.astype(F32))
    dt = jnp.exp(log_dt.astype(F32))[:, None]
    lam_bar = jnp.exp(lam * dt)
    b_bar = ((lam_bar - 1.0) / lam)[..., None] * lax.complex(b_re.astype(F32), b_im.astype(F32))
    c_mat = lax.complex(c_re.astype(F32), c_im.astype(F32))
    taus = jnp.arange(T + 1, dtype=F32)
    pw = jnp.exp((lam * dt)[:, None, :] * taus[None, :, None])
    kern = jnp.einsum('gcp,gtp,gpd->gtcd', c_mat, pw[:, :T], b_bar,
                      precision=HIGHEST).real
    tt = jnp.arange(T)
    shift = (tt[:, None, None] - tt[None, :, None] == tt[None, None, :]).astype(F32)
    toe = jnp.einsum('tsu,gucd->gtscd', shift, kern, precision=HIGHEST)
    m_t = toe.transpose(0, 2, 4, 1, 3).reshape(G, T * C, T * C).astype(BF16)
    hmat = (pw[:, :T][:, ::-1, :, None] * b_bar[:, None]).transpose(0, 1, 3, 2).reshape(G, T * C, P)
    emat = (c_mat[:, None] * pw[:, 1:][:, :, None, :]).reshape(G, T * C, P).transpose(0, 2, 1)
    a_re, a_im = pw[:, T].real, pw[:, T].imag
    rows8 = lambda a: jnp.broadcast_to(a[:, None], (G, SUBLANES, P))
    d_t = jnp.tile(d_skip.astype(F32), (1, T))[:, None]
    eye = jnp.eye(T, dtype=F32)
    gw = jnp.einsum('ts,gce->gtcse', eye, glu_w.astype(F32)).reshape(G, T * C, T * C).astype(BF16)
    gb = jnp.tile(glu_b.astype(F32), (1, T))[:, None]
    return (m_t, hmat.real.astype(BF16), hmat.imag.astype(BF16), emat.real.astype(BF16), (-emat.imag).astype(BF16),
            rows8(a_re), rows8(a_im), d_t, gw, gb)


def _s5(u, tables):
    B, S, _ = u.shape
    T, C, P, G = S5_CHUNK, S5_GROUP, S5_STATE, S5_GROUPS
    assert B == SUBLANES
    n_chunks = S // T
    rows = n_chunks * B
    ug = u.reshape(B, n_chunks, T, G, C).transpose(3, 1, 0, 2, 4).reshape(G, rows, T * C)
    per_g = lambda a: pl.BlockSpec((1,) + a.shape[1:], lambda g: (g, 0, 0))
    out = pl.pallas_call(
        functools.partial(_s5_kernel, n_chunks=n_chunks, batch=B),
        out_shape=jax.ShapeDtypeStruct((G, rows, T * C), BF16),
        grid=(G,),
        in_specs=[per_g(ug)] + [per_g(t) for t in tables],
        out_specs=per_g(ug),
        scratch_shapes=[pltpu.VMEM((rows, P), F32) for _ in range(4)],
        compiler_params=_cparams(("parallel",)),
        name="s5",
    )(ug, *tables)
    return out.reshape(G, n_chunks, B, T, C).transpose(2, 1, 3, 0, 4).reshape(B, S, G * C)


S5_LT = LANES // S5_GROUP
S5_PAIRS = S5_CHUNK // 2


def _s5s_kernel(u_ref, h_ref, e_ref, kk_ref, are_ref, aim_ref, d_ref, gw_ref, gb_ref, o_ref, xl_scr, x0_scr):
    n_chunks = u_ref.shape[1] // S5_CHUNK
    half = S5_LT * S5_STATE
    tok = lambda s: u_ref[0, pl.ds(s, n_chunks, stride=S5_CHUNK), :]
    u2 = [jnp.concatenate([tok(2 * q), tok(2 * q + 1)], axis=1) for q in range(S5_PAIRS)]
    u2b = [v.astype(BF16) for v in u2]
    xl_scr[...] = functools.reduce(lambda a, b: a + b, [_dot(u2b[q], h_ref[0, q]) for q in range(S5_PAIRS)])
    a_re = are_ref[0]
    a_im = aim_ref[0]

    def body(a, carry):
        re, im = carry
        x0_scr[pl.ds(a, 1), 0:half] = re
        x0_scr[pl.ds(a, 1), half:2 * half] = im
        return (a_re * re - a_im * im + xl_scr[pl.ds(a, 1), 0:half],
                a_re * im + a_im * re + xl_scr[pl.ds(a, 1), half:2 * half])

    zero = jnp.zeros((1, half), F32)
    lax.fori_loop(0, n_chunks, body, (zero, zero), unroll=8)
    x0 = x0_scr[...].astype(BF16)
    for p in range(S5_PAIRS):
        y = _dot(x0, e_ref[0, p]) + u2[p] * d_ref[0]
        for q in range(p + 1):
            y = y + _dot(u2b[q], kk_ref[0, p - q])
        ys = _gelu_tanh(y)
        out = ys * _sigmoid(_dot(ys.astype(BF16), gw_ref[0]) + gb_ref[0])
        o_ref[0, pl.ds(2 * p, n_chunks, stride=S5_CHUNK), :] = out[:, :LANES].astype(o_ref.dtype)
        o_ref[0, pl.ds(2 * p + 1, n_chunks, stride=S5_CHUNK), :] = out[:, LANES:].astype(o_ref.dtype)


def _s5s_tables(lam_re, lam_im, log_dt, b_re, b_im, c_re, c_im, d_skip, glu_w, glu_b):
    T, C, P, LT = S5_CHUNK, S5_GROUP, S5_STATE, S5_LT
    G = lam_re.shape[0]
    NT = G // LT
    lam = lax.complex(lam_re.astype(F32), lam_im.astype(F32))
    dt = jnp.exp(log_dt.astype(F32))[:, None]
    lam_bar = jnp.exp(lam * dt)
    b_bar = ((lam_bar - 1.0) / lam)[..., None] * lax.complex(b_re.astype(F32), b_im.astype(F32))
    c_mat = lax.complex(c_re.astype(F32), c_im.astype(F32))
    taus = jnp.arange(T + 1, dtype=F32)
    pw = jnp.exp((lam * dt)[:, None, :] * taus[None, :, None])
    eye = jnp.eye(LT, dtype=F32)
    tiles = lambda a: a.reshape((NT, LT) + a.shape[1:])

    kern = jnp.einsum('gcp,gtp,gpd->gtdc', c_mat, pw[:, :T], b_bar, precision=HIGHEST).real
    kblk = jnp.einsum('nitdc,ij->ntidjc', tiles(kern), eye).reshape(NT, T, LANES, LANES)
    kblk = jnp.concatenate([jnp.zeros_like(kblk[:, :1]), kblk], axis=1)
    kk = jnp.stack([jnp.concatenate([jnp.concatenate([kblk[:, 2 * d + 1], kblk[:, 2 * d + 2]], axis=2),
                                     jnp.concatenate([kblk[:, 2 * d], kblk[:, 2 * d + 1]], axis=2)], axis=1)
                    for d in range(T // 2)], axis=1)

    hmat = pw[:, :T][:, ::-1, :, None] * b_bar[:, None]

    def state_cols(m):
        return jnp.einsum('nispc,ij->nsicjp', tiles(m), eye).reshape(NT, T, LANES, LT * P)

    h = jnp.concatenate([state_cols(hmat.real), state_cols(hmat.imag)], axis=3)
    h2 = h.reshape(NT, T // 2, 2 * LANES, 2 * LT * P)

    emat = c_mat[:, None] * pw[:, 1:][:, :, None, :]

    def state_rows(m):
        return jnp.einsum('nitcp,ij->ntjpic', tiles(m), eye).reshape(NT, T, LT * P, LANES)

    e = jnp.concatenate([state_rows(emat.real), state_rows(-emat.imag)], axis=2)
    e2 = e.reshape(NT, T // 2, 2, 2 * LT * P, LANES).transpose(0, 1, 3, 2, 4).reshape(NT, T // 2, 2 * LT * P, 2 * LANES)

    a_re = pw[:, T].real.reshape(NT, 1, LT * P)
    a_im = pw[:, T].imag.reshape(NT, 1, LT * P)
    pair = lambda v: jnp.tile(v.astype(F32).reshape(NT, 1, LANES), (1, 1, 2))
    gwb = jnp.einsum('nice,ij->nicje', tiles(glu_w.astype(F32)), eye).reshape(NT, LANES, LANES)
    zeros = jnp.zeros_like(gwb)
    gw2 = jnp.concatenate([jnp.concatenate([gwb, zeros], axis=2), jnp.concatenate([zeros, gwb], axis=2)], axis=1)
    return (h2.astype(BF16), e2.astype(BF16), kk.astype(BF16), a_re, a_im, pair(d_skip), gw2.astype(BF16), pair(glu_b))


def _s5s(u, tables):
    B, S, W = u.shape
    NT = W // LANES
    n_chunks = S // S5_CHUNK
    per_tile = lambda a: pl.BlockSpec((1,) + a.shape[1:], lambda j, b: (j,) + (0,) * (a.ndim - 1))
    return pl.pallas_call(
        _s5s_kernel,
        out_shape=jax.ShapeDtypeStruct((B, S, W), F32),
        grid=(NT, B),
        in_specs=[pl.BlockSpec((1, S, LANES), lambda j, b: (b, 0, j))] + [per_tile(t) for t in tables],
        out_specs=pl.BlockSpec((1, S, LANES), lambda j, b: (b, 0, j)),
        scratch_shapes=[pltpu.VMEM((n_chunks, 2 * S5_LT * S5_STATE), F32) for _ in range(2)],
        compiler_params=_cparams(("parallel", "parallel")),
        name="s5",
    )(u, *tables)


def _compress_kernel(x_ref, plo_ref, phi_ref, w1_ref, b1_ref, w2_ref, b2_ref, o_ref):
    x = x_ref[0, 0]
    half = x.shape[1]
    w1 = w1_ref[0]
    lo = _dot((x + plo_ref[0]).astype(BF16), w1[:half])
    hi = _dot((x + phi_ref[0]).astype(BF16), w1[half:])
    rows = x.shape[0]
    hid = _gelu_tanh(lo + pltpu.roll(hi, rows - 1, 0) + b1_ref[0])
    o_ref[0, 0] = _dot(hid.astype(BF16), w2_ref[0]) + b2_ref[0]


def _compress(xg, pos, w1, b1, w2, b2):
    _, B, rows, width = xg.shape
    pos_flat = pos.reshape(2, 2, 1, width).astype(F32)
    sel = lambda shape: pl.BlockSpec((1,) + shape, lambda j, b: (j, 0, 0))
    return pl.pallas_call(
        _compress_kernel,
        out_shape=jax.ShapeDtypeStruct((2, B, rows, NSA_DH), F32),
        grid=(2, B),
        in_specs=[pl.BlockSpec((1, 1, rows, width), lambda j, b: (j, b, 0, 0)),
                  sel((1, width)), sel((1, width)),
                  sel((2 * width, CMP_HIDDEN)), sel((1, CMP_HIDDEN)),
                  sel((CMP_HIDDEN, NSA_DH)), sel((1, NSA_DH))],
        out_specs=pl.BlockSpec((1, 1, rows, NSA_DH), lambda j, b: (j, b, 0, 0)),
        compiler_params=_cparams(("parallel", "parallel")),
        name="nsa_compress",
    )(xg, pos_flat[:, 0], pos_flat[:, 1], w1.astype(BF16), b1[:, None].astype(F32),
      w2.astype(BF16), b2[:, None].astype(F32))


def _nsa_kernel(q_ref, gp_ref, bg_ref, kc_ref, vc_ref, ks_ref, vs_ref, kw_ref, vw_ref,
                grev_ref, selb_ref, winb_ref, ovt_ref, ex_ref, o_ref):
    T = ATT_TILE
    R, DH = NSA_R, NSA_DH
    qi = pl.program_id(2)
    q0 = qi * T
    qall = q_ref[0] * (DH ** -0.5 * LOG2E)
    t_col = q0 + lax.broadcasted_iota(jnp.int32, (T, 1), 0)
    n_cmp_pad = kc_ref.shape[2]
    n_sel = ovt_ref.shape[0]
    grp_rows = CMP_STRIDE

    q4 = jnp.concatenate([qall[:, r * DH:(r + 1) * DH] for r in range(R)], axis=0).astype(BF16)

    n_row = lax.broadcasted_iota(jnp.int32, (1, n_cmp_pad), 1)
    cmask = (t_col >= n_row * CMP_STRIDE + (CMP_BLOCK - 1))[None]
    bias = jnp.stack([jnp.concatenate(
        [pltpu.roll(grev_ref[r], (qi * (T // grp_rows) + al + 1) % n_cmp_pad, 1) for al in range(T // grp_rows)],
        axis=0) for r in range(R)], axis=0)
    s = jnp.where(cmask, _dot_nt(q4, kc_ref[0, 0]).reshape(R, T, n_cmp_pad) + bias, NEG)
    p = jnp.exp2(s - jnp.max(s, axis=2, keepdims=True))
    p = p / jnp.sum(p, axis=2, keepdims=True)
    p = jnp.where(cmask, p, 0.0)
    o_cmp = _dot(p.reshape(R * T, n_cmp_pad).astype(BF16), vc_ref[0, 0])
    psum = functools.reduce(lambda a, b: a + b, [p[r] for r in range(R)])

    imp_t = lax.dot_general(ovt_ref[...], psum, (((1,), (1,)), ((), ())), precision=HIGHEST,
                            preferred_element_type=F32)
    jj = lax.broadcasted_iota(jnp.int32, (n_sel, T), 0)
    blk_t = (q0 + lax.broadcasted_iota(jnp.int32, (1, T), 1)) // SEL_BLOCK
    forced = (jj == 0) | (jj == blk_t) | (jj == blk_t - 1)
    score = jnp.where(forced, FORCE, jnp.where(jj <= blk_t, imp_t, -1.0))
    n_blk = n_sel // SUBLANES
    rows = [score[v * SUBLANES:(v + 1) * SUBLANES] for v in range(n_blk)]
    cnts = [jnp.zeros((SUBLANES, T), F32) for _ in range(n_blk)]
    sub = lax.broadcasted_iota(jnp.int32, (SUBLANES, T), 0)
    for j2 in range(n_sel):
        c2 = score[j2:j2 + 1, :]
        for v in range(n_blk):
            lo = v * SUBLANES
            if lo > j2:
                beats = c2 >= rows[v]
            elif lo + SUBLANES - 1 <= j2:
                beats = c2 > rows[v]
            else:
                beats = (c2 > rows[v]) | ((c2 >= rows[v]) & (sub > j2 - lo))
            cnts[v] = cnts[v] + jnp.where(beats, 1.0, 0.0)
    cnt = jnp.concatenate(cnts, axis=0)
    sel_t = jnp.where((cnt < float(min(SEL_TOPK, n_sel))) & (jj <= blk_t), 1.0, 0.0)
    sel_q = jnp.concatenate([sel_t, jnp.zeros((LANES - n_sel, T), F32)], axis=0).T
    lane = lax.broadcasted_iota(jnp.int32, (T, LANES), 1)
    sel_aug = jnp.where(lane == n_sel, 1.0, sel_q).astype(BF16)

    n_far = selb_ref.shape[0] - 1
    n_win = winb_ref.shape[0] - 2
    CH = ex_ref.shape[2] // T

    def sel_body(kc, carry):
        m, acc = carry
        off = pl.multiple_of(kc * (CH * T), CH * T)
        k = ks_ref[0, 0, pl.ds(off, CH * T), :]
        v = vs_ref[0, 0, pl.ds(off, CH * T), :]
        s = _dot_nt(q4, k)
        mask = _dot(sel_aug, ex_ref[kc])
        subs = []
        for j in range(CH):
            d = jnp.clip(qi - (kc * CH + j), 0, n_far)
            sj = s[:, j * T:(j + 1) * T].reshape(R, T, T) + selb_ref[d] + mask[:, j * T:(j + 1) * T][None]
            subs.append(sj.reshape(R * T, T))
        m_new = jnp.maximum(m, jnp.max(functools.reduce(jnp.maximum, subs), axis=1, keepdims=True))
        pb = jnp.concatenate([jnp.exp2(sj - m_new).astype(BF16) for sj in subs], axis=1)
        return m_new, jnp.exp2(m - m_new) * acc + _dot(pb, v)

    _, acc = lax.fori_loop(0, qi // CH + 1, sel_body,
                           (jnp.full((R * T, 1), NEG, F32), jnp.zeros((R * T, LANES), F32)))
    o_sel = acc[:, :DH] / acc[:, DH:DH + 1]

    subs, vals = [], []
    for d in range(n_win + 1):
        off = pl.multiple_of(jnp.maximum(qi - d, 0) * T, T)
        tile = jnp.where(qi >= d, d, n_win + 1)
        subs.append((_dot_nt(q4, kw_ref[0, 0, pl.ds(off, T), :]).reshape(R, T, T) + winb_ref[tile]).reshape(R * T, T))
        vals.append(vw_ref[0, 0, pl.ds(off, T), :])
    m_w = jnp.max(functools.reduce(jnp.maximum, subs), axis=1, keepdims=True)
    acc = functools.reduce(lambda a, b: a + b,
                           [_dot(jnp.exp2(sj - m_w).astype(BF16), vj) for sj, vj in zip(subs, vals)])
    o_win = acc[:, :DH] / acc[:, DH:DH + 1]

    gates = _sigmoid(gp_ref[0] + bg_ref[...])
    gcol = lambda j: jnp.concatenate([gates[:, 3 * r + j:3 * r + j + 1] for r in range(R)], axis=0)
    out4 = gcol(0) * o_cmp + gcol(1) * o_sel + gcol(2) * o_win
    o_ref[0] = jnp.concatenate([out4[r * T:(r + 1) * T] for r in range(R)], axis=1)


def _t5_bucket(dist):
    dist = jnp.maximum(dist, 0)
    max_exact = REL_BUCKETS // 2
    log_ratio = jnp.log(jnp.maximum(dist, 1).astype(F32) / max_exact) / math.log(REL_MAX_DIST / max_exact)
    large = jnp.minimum(max_exact + (log_ratio * (REL_BUCKETS - max_exact)).astype(jnp.int32), REL_BUCKETS - 1)
    return jnp.where(dist < max_exact, dist, large)


def _nsa_tables(rel_bias, S):
    T = ATT_TILE
    table = rel_bias.astype(F32) * LOG2E
    ii = jnp.arange(T)
    delta = ii[:, None] - ii[None, :]

    def tile(off):
        return table[_t5_bucket(off * T + delta)].transpose(2, 0, 1)

    n_far = -(-REL_MAX_DIST // T) + 1
    selb = [tile(o) for o in range(n_far + 1)]
    selb[0] = selb[0] + jnp.where(delta >= 0, 0.0, NEG)[None]
    selb = jnp.stack(selb, axis=0)
    n_win = WINDOW // T
    winb = []
    for o in range(n_win + 1):
        dist = o * T + delta
        ok = (dist >= 0) & (dist < WINDOW)
        winb.append(tile(o) + jnp.where(ok, 0.0, NEG)[None])
    winb.append(jnp.full_like(winb[0], NEG))
    winb = jnp.stack(winb, axis=0)
    n_pad = S // CMP_STRIDE
    i16 = jnp.arange(CMP_STRIDE)
    dd = jnp.arange(n_pad)
    gdist = CMP_STRIDE * dd[None, :] + i16[:, None] - (CMP_BLOCK - 1)
    grev = table[_t5_bucket(gdist)].transpose(2, 0, 1)[:, :, ::-1]
    n_sel = S // SEL_BLOCK
    cmp_start = jnp.arange(n_pad) * CMP_STRIDE
    sel_start = jnp.arange(n_sel) * SEL_BLOCK
    overlap = jnp.clip(jnp.minimum(cmp_start[:, None] + CMP_BLOCK, sel_start[None] + SEL_BLOCK)
                       - jnp.maximum(cmp_start[:, None], sel_start[None]), 0).astype(F32) / CMP_BLOCK
    n_cmp = (S - CMP_BLOCK) // CMP_STRIDE + 1
    overlap_t = jnp.where((jnp.arange(n_pad) < n_cmp)[:, None], overlap, 0.0).T
    tk = min(SEL_CHUNK * T, S)
    kpos_blk = jnp.arange(S) // SEL_BLOCK
    rows = jnp.arange(LANES)[:, None]
    expand = jnp.where(rows == kpos_blk[None, :], BIG, jnp.where(rows == n_sel, -BIG, 0.0)).astype(BF16)
    expand = expand.reshape(LANES, S // tk, tk).transpose(1, 0, 2)
    return grev, selb, winb, overlap_t, expand


def _nsa_attention(q, gp, bg, kcmp, vcmp, ks, vs, kw, vw, tables):
    B, S, D = q.shape
    T = ATT_TILE
    grev, selb, winb, overlap_t, expand = tables
    gw = NSA_R * NSA_DH
    seq = lambda a: pl.BlockSpec((1, 1) + a.shape[2:], lambda b, g, i: (b, g, 0, 0))
    per_head = lambda a: pl.BlockSpec((NSA_R,) + a.shape[1:], lambda b, g, i: (g,) + (0,) * (a.ndim - 1))
    tiles = lambda a: pl.BlockSpec((a.shape[0], NSA_R) + a.shape[2:], lambda b, g, i: (0, g, 0, 0))
    full = lambda a: pl.BlockSpec(a.shape, lambda b, g, i: (0,) * a.ndim)
    return pl.pallas_call(
        _nsa_kernel,
        out_shape=jax.ShapeDtypeStruct((B, S, D), F32),
        grid=(B, NSA_KV, S // T),
        in_specs=[pl.BlockSpec((1, T, gw), lambda b, g, i: (b, i, g)),
                  pl.BlockSpec((1, T, LANES), lambda b, g, i: (b, i, g)),
                  pl.BlockSpec((1, LANES), lambda b, g, i: (0, g)),
                  seq(kcmp), seq(vcmp), seq(ks), seq(vs), seq(kw), seq(vw),
                  per_head(grev), tiles(selb), tiles(winb), full(overlap_t), full(expand)],
        out_specs=pl.BlockSpec((1, T, gw), lambda b, g, i: (b, i, g)),
        compiler_params=_cparams(("parallel", "parallel", "arbitrary")),
        name="nsa_attention",
    )(q, gp, bg, kcmp, vcmp, ks, vs, kw, vw, grev, selb, winb, overlap_t, expand)


def _nsa_proj_kernel(x_ref, g_ref, sh_ref, sc_ref, wq_ref, wk_ref, wv_ref, wg_ref, bg_ref,
                     q4_ref, gv_ref, kc_ref, vc_ref, ks_ref, kw_ref, vst_ref, vwt_ref):
    KV, R, DH, T = NSA_KV, NSA_R, NSA_DH, ATT_TILE
    h = _modulated_norm(x_ref[0], g_ref[...], sh_ref[0], sc_ref[0]).astype(BF16)
    q_t = (_dot(h, wq_ref[...]) * (DH ** -0.5 * LOG2E)).T.astype(BF16)
    gates_t = _sigmoid(_dot(h, wg_ref[...]) + bg_ref[...]).T
    row = lax.broadcasted_iota(jnp.int32, (SUBLANES, R * T), 0)
    for g in range(KV):
        q4_ref[0, g, 0] = jnp.concatenate([q_t[(g * R + r) * DH:(g * R + r + 1) * DH] for r in range(R)], axis=1)
        gv = jnp.zeros((SUBLANES, R * T), F32)
        for j in range(3):
            gj = jnp.concatenate([gates_t[g * LANES + 3 * r + j:g * LANES + 3 * r + j + 1] for r in range(R)], axis=1)
            gv = jnp.where(row == j, gj, gv)
        gv_ref[0, g, 0] = gv
    k3 = _dot(h, wk_ref[...])
    v3 = _dot(h, wv_ref[...])
    vs_t = v3[:, KV_W:2 * KV_W].T.astype(BF16)
    vw_t = v3[:, 2 * KV_W:].T.astype(BF16)
    for g in range(KV):
        cols = slice(g * DH, (g + 1) * DH)
        kc_ref[0, g] = k3[:, cols].astype(BF16)
        vc_ref[0, g] = v3[:, cols].astype(BF16)
        ks_ref[0, g] = k3[:, KV_W + g * DH:KV_W + (g + 1) * DH].astype(BF16)
        kw_ref[0, g] = k3[:, 2 * KV_W + g * DH:2 * KV_W + (g + 1) * DH].astype(BF16)
        vst_ref[0, g, 0] = vs_t[cols]
        vwt_ref[0, g, 0] = vw_t[cols]


def _nsa_proj(x, g, shift, scale, weights, b_gate):
    B, S, D = x.shape
    KV, R, DH, T = NSA_KV, NSA_R, NSA_DH, ATT_TILE
    vec = pl.BlockSpec((1, 1, D), lambda b, i: (b, 0, 0))
    keys = pl.BlockSpec((1, KV, T, DH), lambda b, i: (b, 0, i, 0))
    key_shape = jax.ShapeDtypeStruct((B, KV, S, DH), BF16)
    tile = lambda rows, width: pl.BlockSpec((1, KV, 1, rows, width), lambda b, i: (b, 0, i, 0, 0))
    tile_shape = lambda rows, width, dt: jax.ShapeDtypeStruct((B, KV, S // T, rows, width), dt)
    return pl.pallas_call(
        _nsa_proj_kernel,
        out_shape=[tile_shape(DH, R * T, BF16), tile_shape(SUBLANES, R * T, F32),
                   key_shape, key_shape, key_shape, key_shape,
                   tile_shape(DH, T, BF16), tile_shape(DH, T, BF16)],
        grid=(B, S // T),
        in_specs=[pl.BlockSpec((1, T, D), lambda b, i: (b, i, 0)),
                  pl.BlockSpec((1, D), lambda b, i: (0, 0)), vec, vec]
                 + [pl.BlockSpec(w.shape, lambda b, i: (0, 0)) for w in weights]
                 + [pl.BlockSpec(b_gate.shape, lambda b, i: (0, 0))],
        out_specs=[tile(DH, R * T), tile(SUBLANES, R * T), keys, keys, keys, keys, tile(DH, T), tile(DH, T)],
        compiler_params=_cparams(("parallel", "parallel")),
        name="nsa_proj",
    )(x, g.reshape(1, D), shift, scale, *weights, b_gate)


def _nsa_t_kernel(q4_ref, gv_ref, kc_ref, vct_ref, ks_ref, vst_ref, kw_ref, vwt_ref,
                  cfar_ref, band_ref, selb_ref, winb_ref, ovt_ref, o_ref, s_scr, sel_scr, sbuf):
    T = ATT_TILE
    R, DH = NSA_R, NSA_DH
    qi = pl.program_id(2)
    q0 = qi * T
    n_pad = kc_ref.shape[2]
    n_sel = ovt_ref.shape[0]
    CH = SEL_CHUNK
    n_far = selb_ref.shape[0] - 1
    n_win = winb_ref.shape[0] - 2
    band_rows = band_ref.shape[2] - T // CMP_STRIDE * 2

    q4 = q4_ref[0, 0, 0]
    t_lane = q0 + lax.broadcasted_iota(jnp.int32, (1, R * T), 1) % T

    with_ones = lambda v_t: jnp.concatenate([v_t, jnp.ones_like(v_t)], axis=0)
    gvec = lambda j: gv_ref[0, 0, 0, j:j + 1, :]

    grp = T // CMP_STRIDE
    s_scr[0:n_pad, :] = _dot(kc_ref[0, 0], q4) + cfar_ref[0]
    s_scr[n_pad:n_pad + 2 * grp, :] = jnp.zeros((2 * grp, R * T), F32)
    r0 = jnp.maximum(qi * grp - 2 * grp, 0)
    x0 = r0 - (qi * grp - 2 * grp)
    r0 = pl.multiple_of(r0, SUBLANES)
    x0 = pl.multiple_of(x0, SUBLANES)
    s_scr[pl.ds(r0, band_rows), :] += band_ref[0, 0, pl.ds(x0, band_rows), :]
    lim = pl.multiple_of(qi * grp + 2 * grp, SUBLANES)
    s_scr[pl.ds(lim, n_pad), :] = jnp.full((n_pad, R * T), NEG, F32)

    w_subs, w_vals = [], []
    for d in range(n_win + 1):
        kt = jnp.maximum(qi - d, 0)
        off = pl.multiple_of(kt * T, T)
        tile = jnp.where(qi >= d, d, n_win + 1)
        w_subs.append((_dot(kw_ref[0, 0, pl.ds(off, T), :], q4) + winb_ref[tile, 0]).astype(BF16))
        w_vals.append(with_ones(vwt_ref[0, 0, kt]))

    s = s_scr[0:n_pad, :]
    e = jnp.exp2(s - jnp.max(s, axis=0, keepdims=True))
    inv = jnp.where(t_lane >= CMP_BLOCK - 1, 1.0 / jnp.sum(e, axis=0, keepdims=True), 0.0)
    p = e * inv
    o_cmp = _dot(vct_ref[0, 0], p.astype(BF16))
    psum = functools.reduce(lambda a, b: a + b, [p[:, r * T:(r + 1) * T] for r in range(R)])

    m_w = jnp.max(functools.reduce(jnp.maximum, w_subs), axis=0, keepdims=True)
    acc = functools.reduce(lambda a, b: a + b,
                           [_dot(vj, jnp.exp2(sj - m_w)) for sj, vj in zip(w_subs, w_vals)])
    o_win = acc[:DH] * (1.0 / acc[DH:DH + 1])
    out_t = gvec(0) * o_cmp + gvec(2) * o_win

    imp_t = _dot(ovt_ref[...], psum, precision=HIGHEST)
    jj = lax.broadcasted_iota(jnp.int32, (n_sel, T), 0)
    blk_t = (q0 + lax.broadcasted_iota(jnp.int32, (1, T), 1)) // SEL_BLOCK
    forced = (jj == 0) | (jj == blk_t) | (jj == blk_t - 1)
    score = jnp.where(forced, FORCE, jnp.where(jj <= blk_t, imp_t, -1.0))
    n_blk = n_sel // SUBLANES
    rows = [score[v * SUBLANES:(v + 1) * SUBLANES] for v in range(n_blk)]
    cnts = [jnp.zeros((SUBLANES, T), F32) for _ in range(n_blk)]
    sub = lax.broadcasted_iota(jnp.int32, (SUBLANES, T), 0)
    for j2 in range(n_sel):
        c2 = score[j2:j2 + 1, :]
        for v in range(n_blk):
            lo = v * SUBLANES
            if lo > j2:
                beats = c2 >= rows[v]
            elif lo + SUBLANES - 1 <= j2:
                beats = c2 > rows[v]
            else:
                beats = (c2 > rows[v]) | ((c2 >= rows[v]) & (sub > j2 - lo))
            cnts[v] = cnts[v] + jnp.where(beats, 1.0, 0.0)
    cnt = jnp.concatenate(cnts, axis=0)
    chosen = (cnt < float(min(SEL_TOPK, n_sel))) & (jj <= blk_t)
    sel_scr[...] = jnp.where(chosen, 0.0, -BIG)

    def block_mask(kt):
        per_tile = T // SEL_BLOCK
        parts = [jnp.broadcast_to(sel_scr[pl.ds(kt * per_tile + i, 1), :], (SEL_BLOCK, T)) for i in range(per_tile)]
        m1 = jnp.concatenate(parts, axis=0)
        return jnp.concatenate([m1] * R, axis=1)

    def sel_scores(slot, kc):
        off = pl.multiple_of(kc * (CH * T), CH * T)
        s = _dot(ks_ref[0, 0, pl.ds(off, CH * T), :], q4)
        subs = []
        for j in range(CH):
            kt = kc * CH + j
            d = jnp.clip(qi - kt, 0, n_far)
            subs.append(s[j * T:(j + 1) * T] + selb_ref[d, 0] + block_mask(kt))
        s = jnp.concatenate(subs, axis=0).astype(BF16)
        sbuf[slot] = s
        return jnp.max(s, axis=0, keepdims=True).astype(F32)

    def sel_weighted(slot, kc, m_new):
        v_t = jnp.concatenate([vst_ref[0, 0, kc * CH + j] for j in range(CH)], axis=1)
        return _dot(with_ones(v_t), jnp.exp2(sbuf[slot] - m_new.astype(BF16)))

    last_chunk = vst_ref.shape[2] // CH - 1

    def sel_body(i, carry):
        m, acc, m_even = carry
        m_odd = sel_scores(1, 2 * i + 1)
        m_new = jnp.maximum(m, m_even)
        acc = jnp.exp2(m - m_new) * acc + sel_weighted(0, 2 * i, m_new)
        m_even = sel_scores(0, jnp.minimum(2 * i + 2, last_chunk))
        m_fin = jnp.maximum(m_new, m_odd)
        acc = jnp.exp2(m_new - m_fin) * acc + sel_weighted(1, 2 * i + 1, m_fin)
        return m_fin, acc, m_even

    n_chunks = qi // CH + 1
    _, acc, _ = lax.fori_loop(0, (n_chunks + 1) // 2, sel_body,
                              (jnp.full((1, R * T), NEG, F32), jnp.zeros((2 * DH, R * T), F32), sel_scores(0, 0)))
    out_t = out_t + gvec(1) * (acc[:DH] * (1.0 / acc[DH:DH + 1]))
    for pr in range(R // 2):
        pair = jnp.concatenate([out_t[:, (2 * pr) * T:(2 * pr + 1) * T],
                                out_t[:, (2 * pr + 1) * T:(2 * pr + 2) * T]], axis=0)
        o_ref[0, :, pr * 2 * DH:(pr + 1) * 2 * DH] = pair.T


def _bias_lookup(table, dist):
    idx = _t5_bucket(dist)
    out = jnp.zeros(idx.shape + (table.shape[1],), F32)
    for k in range(table.shape[0]):
        out = out + jnp.where((idx == k)[..., None], table[k], 0.0)
    return out


def _nsa_t_tables(rel_bias, S):
    T, R, KV = ATT_TILE, NSA_R, NSA_KV
    table = rel_bias.astype(F32) * LOG2E
    ii = jnp.arange(T)
    delta = ii[None, :] - ii[:, None]

    def lanes(a):
        a = jnp.moveaxis(a, -1, 0)
        a = a.reshape((KV, R) + a.shape[1:])
        return jnp.moveaxis(a, 1, 2).reshape(KV, a.shape[2], R * a.shape[3])

    def tile(off):
        return lanes(_bias_lookup(table, off * T + delta))

    mask4 = lambda ok: jnp.tile(jnp.where(ok, 0.0, NEG), (1, R))[None]
    n_far = -(-REL_MAX_DIST // T) + 1
    selb = [tile(o) for o in range(n_far + 1)]
    selb[0] = selb[0] + mask4(delta >= 0)
    selb = jnp.stack(selb, axis=0)
    n_win = WINDOW // T
    winb = [tile(o) + mask4((o * T + delta >= 0) & (o * T + delta < WINDOW)) for o in range(n_win + 1)]
    winb.append(jnp.full_like(winb[0], NEG))
    winb = jnp.stack(winb, axis=0)

    grp = T // CMP_STRIDE
    far = _bias_lookup(table, jnp.asarray(2 * REL_MAX_DIST))
    xx = jnp.arange(4 * grp)
    bdist = ii[None, :] - CMP_STRIDE * (xx[:, None] - 2 * grp) - (CMP_BLOCK - 1)
    band = jnp.where((bdist >= 0)[..., None], _bias_lookup(table, bdist) - far, NEG)
    band = jnp.concatenate([lanes(band), jnp.zeros((KV, 2 * grp, R * T), F32)], axis=1)[:, None]
    cfar = jnp.repeat(far.reshape(KV, R), T, axis=1)[:, None]

    n_pad = S // CMP_STRIDE
    n_sel = S // SEL_BLOCK
    cmp_start = jnp.arange(n_pad) * CMP_STRIDE
    sel_start = jnp.arange(n_sel) * SEL_BLOCK
    overlap = jnp.clip(jnp.minimum(cmp_start[:, None] + CMP_BLOCK, sel_start[None] + SEL_BLOCK)
                       - jnp.maximum(cmp_start[:, None], sel_start[None]), 0).astype(F32) / CMP_BLOCK
    n_cmp = (S - CMP_BLOCK) // CMP_STRIDE + 1
    overlap_t = jnp.where((jnp.arange(n_pad) < n_cmp)[:, None], overlap, 0.0).T
    return cfar, band, selb, winb, overlap_t


def _nsa_t_attention(q4, gv, kcmp, vcmp_t, ks, vs_t, kw, vw_t, tables):
    B, KV, S, DH = ks.shape
    T = ATT_TILE
    cfar, band, selb, winb, overlap_t = tables
    gw = NSA_R * DH
    n_pad = kcmp.shape[2]
    seq = lambda a: pl.BlockSpec((1, 1) + a.shape[2:], lambda b, g, i: (b, g) + (0,) * (a.ndim - 2))
    qtile = lambda a: pl.BlockSpec((1, 1, 1) + a.shape[3:], lambda b, g, i: (b, g, i, 0, 0))
    grp = lambda a: pl.BlockSpec((1,) + a.shape[1:], lambda b, g, i: (g,) + (0,) * (a.ndim - 1))
    tiles = lambda a: pl.BlockSpec((a.shape[0], 1) + a.shape[2:], lambda b, g, i: (0, g, 0, 0))
    full = lambda a: pl.BlockSpec(a.shape, lambda b, g, i: (0,) * a.ndim)
    return pl.pallas_call(
        _nsa_t_kernel,
        out_shape=jax.ShapeDtypeStruct((B, S, KV * gw), F32),
        grid=(B, KV, S // T),
        in_specs=[qtile(q4), qtile(gv),
                  seq(kcmp), seq(vcmp_t), seq(ks), seq(vs_t), seq(kw), seq(vw_t),
                  grp(cfar), grp(band), tiles(selb), tiles(winb), full(overlap_t)],
        out_specs=pl.BlockSpec((1, T, gw), lambda b, g, i: (b, i, g)),
        scratch_shapes=[pltpu.VMEM((2 * n_pad + 2 * (T // CMP_STRIDE), NSA_R * T), F32),
                        pltpu.VMEM((S // SEL_BLOCK, T), F32),
                        pltpu.VMEM((2, SEL_CHUNK * T, NSA_R * T), BF16)],
        compiler_params=_cparams(("parallel", "parallel", "arbitrary")),
        name="nsa_attention",
    )(q4, gv, kcmp, vcmp_t, ks, vs_t, kw, vw_t, cfar, band, selb, winb, overlap_t)


def _moe_kernel(x_ref, g_ref, sh_ref, sc_ref, gate_ref, wr_ref, br_ref, wg_ref, wu_ref, wd_ref, fg_ref,
                o_ref, hb_scr, rt_scr, acc_scr, *, final):
    NG, PG, FH = MOE_GROUPS, MOE_PER_GROUP, MOE_HIDDEN
    c = pl.program_id(2)

    @pl.when(c == 0)
    def _():
        h = _modulated_norm(x_ref[0], g_ref[...], sh_ref[0], sc_ref[0])
        h_hi = h.astype(BF16)
        hb_scr[...] = h_hi
        h_lo = (h - h_hi.astype(F32)).astype(BF16)
        logits = (_dot(h_hi, wr_ref[0]) + _dot(h_lo, wr_ref[0]) + _dot(h_hi, wr_ref[1]) + br_ref[...]).T
        gl = [logits[NG * PG + g:NG * PG + g + 1, :] for g in range(NG)]
        gmax = functools.reduce(jnp.maximum, gl)
        gtop = jnp.full_like(gmax, float(NG - 1))
        for g in reversed(range(NG - 1)):
            gtop = jnp.where(gl[g] == gmax, float(g), gtop)
        p_g = 1.0 / functools.reduce(lambda a, b: a + b, [jnp.exp(v - gmax) for v in gl])
        a = []
        for j in range(PG):
            v = logits[(NG - 1) * PG + j:(NG - 1) * PG + j + 1, :]
            for g in reversed(range(NG - 1)):
                v = jnp.where(gtop == float(g), logits[g * PG + j:g * PG + j + 1, :], v)
            a.append(v)

        def first_max(vals):
            vmax = functools.reduce(jnp.maximum, vals)
            taken = jnp.zeros_like(vmax) > 1.0
            hits = []
            for v in vals:
                hit = (v == vmax) & jnp.logical_not(taken)
                taken = taken | hit
                hits.append(hit)
            return vmax, hits

        v1, hit1 = first_max(a)
        rest = [jnp.where(hh, -jnp.inf, v) for hh, v in zip(hit1, a)]
        v2, hit2 = first_max(rest)
        e2 = jnp.exp(v2 - v1)
        w1 = p_g / (1.0 + e2)
        w2 = p_g * e2 / (1.0 + e2)
        tm = gtop.shape[1]
        row = lax.broadcasted_iota(jnp.int32, (SUBLANES, tm), 0)
        rt = jnp.where(row == PG, gtop, 0.0)
        for j in range(PG):
            wj = jnp.where(hit1[j], w1, jnp.where(hit2[j], w2, 0.0))
            rt = jnp.where(row == j, wj, rt)
        rt_scr[...] = jnp.concatenate([rt, jnp.zeros((LANES - SUBLANES, tm), F32)], axis=0).T

    hb = hb_scr[...]
    rt = rt_scr[...]
    in_group = rt[:, PG:PG + 1] == c.astype(F32)
    hid = _silu(_dot(hb, wg_ref[0])) * _dot(hb, wu_ref[0])
    parts = [hid[:, j * FH:(j + 1) * FH] * jnp.where(in_group, rt[:, j:j + 1], 0.0) for j in range(PG)]
    contrib = _dot(jnp.concatenate(parts, axis=1).astype(BF16), wd_ref[0])

    @pl.when(c == 0)
    def _():
        acc_scr[...] = contrib

    @pl.when(c > 0)
    def _():
        acc_scr[...] += contrib

    @pl.when(c == NG - 1)
    def _():
        y = x_ref[0] + gate_ref[0] * acc_scr[...]
        if final:
            y = y * lax.rsqrt(jnp.mean(y * y, axis=-1, keepdims=True) + EPS) * fg_ref[...]
        o_ref[0] = y


def _moe(x, g, shift, scale, gate, wg, bg, we, be, w_gate, w_up, w_down, final_g, final, tm=512):
    B, S, D = x.shape
    NG, PG, FH = MOE_GROUPS, MOE_PER_GROUP, MOE_HIDDEN
    wr = jnp.zeros((D, LANES), F32)
    wr = wr.at[:, :NG * PG].set(we.reshape(D, NG * PG).astype(F32)).at[:, NG * PG:NG * PG + NG].set(wg.astype(F32))
    br = jnp.zeros((1, LANES), F32)
    br = br.at[0, :NG * PG].set(be.reshape(NG * PG).astype(F32)).at[0, NG * PG:NG * PG + NG].set(bg.astype(F32))
    wr_hi = wr.astype(BF16)
    wr = jnp.stack([wr_hi, (wr - wr_hi.astype(F32)).astype(BF16)])
    grp = lambda w: w.reshape(NG, PG, D, FH).transpose(0, 2, 1, 3).reshape(NG, D, PG * FH).astype(BF16)
    wd = w_down.reshape(NG, PG * FH, D).astype(BF16)
    vec = pl.BlockSpec((1, 1, D), lambda b, i, c: (b, 0, 0))
    row = pl.BlockSpec((1, D), lambda b, i, c: (0, 0))
    wspec = lambda k, n: pl.BlockSpec((1, k, n), lambda b, i, c: (c, 0, 0))
    return pl.pallas_call(
        functools.partial(_moe_kernel, final=final),
        out_shape=jax.ShapeDtypeStruct((B, S, D), F32),
        grid=(B, S // tm, NG),
        in_specs=[pl.BlockSpec((1, tm, D), lambda b, i, c: (b, i, 0)), row, vec, vec, vec,
                  pl.BlockSpec((2, D, LANES), lambda b, i, c: (0, 0, 0)),
                  pl.BlockSpec((1, LANES), lambda b, i, c: (0, 0)),
                  wspec(D, PG * FH), wspec(D, PG * FH), wspec(PG * FH, D), row],
        out_specs=pl.BlockSpec((1, tm, D), lambda b, i, c: (b, i, 0)),
        scratch_shapes=[pltpu.VMEM((tm, D), BF16), pltpu.VMEM((tm, LANES), F32), pltpu.VMEM((tm, D), F32)],
        compiler_params=_cparams(("parallel", "parallel", "arbitrary")),
        name="moe",
    )(x, g.reshape(1, D), shift, scale, gate, wr, br, grp(w_gate), grp(w_up), wd, final_g.reshape(1, D))


def _mlstm_s5_layer(x, g, shift, scale, gate, w_in, conv_w, b_i, b_f, head_g, s5_params, w_out):
    H = MLSTM_HEADS
    A = MIX_A
    w_if = jnp.zeros((D_MODEL, LANES), F32).at[:, :2 * H].set(w_in[:, 4 * A:4 * A + 2 * H])
    weights = [w_in[:, :2 * A], w_in[:, 2 * A:4 * A], w_if, w_in[:, 4 * A + 2 * H:]]
    qk, vo, ifg, u = _norm_matmul(x, g, shift, scale, [w.astype(BF16) for w in weights], [F32] * 4)
    gate_bias = jnp.zeros((1, LANES), F32).at[0, :H].set(b_i.astype(F32)).at[0, H:2 * H].set(b_f.astype(F32))
    hm = _mlstm(qk, vo, ifg, conv_w.astype(F32), gate_bias, head_g.reshape(1, A).astype(F32))
    ys = _s5s(u, _s5s_tables(*s5_params))
    w_out = w_out.astype(BF16)
    return _out_residual(x, gate, [hm, ys], [w_out[:A], w_out[A:]])


def _nsa_layer(x, g, shift, scale, gate, w_in, b_gate, cmp_pos, cmp_w1, cmp_b1, cmp_w2, cmp_b2, rel_bias, w_out):
    B, S, D = x.shape
    KV, R, DH = NSA_KV, NSA_R, NSA_DH
    w_g = jnp.zeros((D, KV, LANES), F32).at[:, :, :3 * R].set(w_in[:, D + 6 * KV_W:].reshape(D, KV, 3 * R))
    b_g = jnp.zeros((KV, LANES), F32).at[:, :3 * R].set(b_gate.reshape(KV, 3 * R).astype(F32))
    kv_cols = lambda i: w_in[:, D + i * KV_W:D + (i + 1) * KV_W]
    w_k = jnp.concatenate([kv_cols(0), kv_cols(2), kv_cols(4)], axis=1)
    w_v = jnp.concatenate([kv_cols(1), kv_cols(3), kv_cols(5)], axis=1)
    weights = [w_in[:, :D], w_k, w_v, w_g.reshape(D, KV * LANES)]
    q4, gv, kc, vc, ks, kw, vs_t, vw_t = _nsa_proj(x, g, shift, scale, [w.astype(BF16) for w in weights],
                                                   b_g.reshape(1, KV * LANES))
    grp = CMP_STRIDE
    xg = jnp.stack([kc, vc]).reshape(2, B, KV * S // grp, grp * DH)
    cmp = _compress(xg, cmp_pos, cmp_w1, cmp_b1, cmp_w2, cmp_b2).reshape(2, B, KV, S // grp, DH).astype(BF16)
    out = _nsa_t_attention(q4, gv, cmp[0], cmp[1].transpose(0, 1, 3, 2), ks, vs_t, kw, vw_t,
                           _nsa_t_tables(rel_bias, S))
    return _out_residual(x, gate, [out], [w_out.astype(BF16)])


def kernel(x, c, rel_bias, ada_w, ada_b, norm_g, final_g,
           a_w_in, a_conv, a_b_i, a_b_f, a_head_g,
           s5_lam_re, s5_lam_im, s5_log_dt, s5_b_re, s5_b_im, s5_c_re, s5_c_im,
           s5_d, s5_glu_w, s5_glu_b, a_w_out,
           n_w_in, n_b_gate, n_cmp_pos, n_cmp_w1, n_cmp_b1, n_cmp_w2, n_cmp_b2, n_w_out,
           r_grp_w, r_grp_b, r_exp_w, r_exp_b, e_w_gate, e_w_up, e_w_down):
    B, S, D = x.shape
    mod = _ada_mod(c, ada_w, ada_b).reshape(DEPTH, 2, B, 1, 3 * D)
    split = lambda m: (m[..., :D], m[..., D:2 * D], m[..., 2 * D:])
    for layer in range(DEPTH):
        shift, scale, gate = split(mod[layer, 0])
        j = layer // 2
        if layer % 2 == 0:
            s5_params = (s5_lam_re[j], s5_lam_im[j], s5_log_dt[j], s5_b_re[j], s5_b_im[j],
                         s5_c_re[j], s5_c_im[j], s5_d[j], s5_glu_w[j], s5_glu_b[j])
            x = _mlstm_s5_layer(x, norm_g[layer, 0], shift, scale, gate, a_w_in[j], a_conv[j], a_b_i[j], a_b_f[j],
                                a_head_g[j], s5_params, a_w_out[j])
        else:
            x = _nsa_layer(x, norm_g[layer, 0], shift, scale, gate, n_w_in[j], n_b_gate[j], n_cmp_pos[j],
                           n_cmp_w1[j], n_cmp_b1[j], n_cmp_w2[j], n_cmp_b2[j], rel_bias, n_w_out[j])
        shift, scale, gate = split(mod[layer, 1])
        x = _moe(x, norm_g[layer, 1], shift, scale, gate, r_grp_w[layer], r_grp_b[layer], r_exp_w[layer],
                 r_exp_b[layer], e_w_gate[layer], e_w_up[layer], e_w_down[layer], final_g,
                 final=(layer == DEPTH - 1))
    return x
```

```python
import functools
import math

import jax
import jax.numpy as jnp
from jax import lax
from jax.experimental import pallas as pl
from jax.experimental.pallas import tpu as pltpu

F32 = jnp.float32
BF16 = jnp.bfloat16
HIGHEST = lax.Precision.HIGHEST

D_MODEL = 1024
DEPTH = 2
MIX_A = 512
MLSTM_HEADS = 4
MLSTM_DH = MIX_A // MLSTM_HEADS
MLSTM_CHUNK = 128
CONV_K = 4
MIX_B = D_MODEL - MIX_A
S5_GROUP = 16
S5_GROUPS = MIX_B // S5_GROUP
S5_STATE = 64
S5_CHUNK = 16
NSA_HEADS = 16
NSA_KV = 4
NSA_R = NSA_HEADS // NSA_KV
NSA_DH = D_MODEL // NSA_HEADS
KV_W = NSA_KV * NSA_DH
CMP_BLOCK = 32
CMP_STRIDE = 16
CMP_HIDDEN = 256
SEL_BLOCK = 64
SEL_TOPK = 16
WINDOW = 512
FORCE = 1e9
REL_BUCKETS = 32
REL_MAX_DIST = 128
MOE_GROUPS = 4
MOE_PER_GROUP = 4
MOE_HIDDEN = 256
EPS = 1e-6
NEG = -1e30
BIG = 1e30
LOG2E = math.log2(math.e)
SEL_CHUNK = 1

LANES = 128
SUBLANES = 8
ATT_TILE = 256
VMEM_LIMIT = 56 * 1024 * 1024


def _cparams(sem):
    return pltpu.CompilerParams(dimension_semantics=sem, vmem_limit_bytes=VMEM_LIMIT)


def _dot(a, b, precision=None):
    return jnp.dot(a, b, preferred_element_type=F32, precision=precision)


def _dot_nt(a, b):
    return lax.dot_general(a, b, (((1,), (1,)), ((), ())), preferred_element_type=F32)


def _sigmoid(x):
    return 1.0 / (1.0 + jnp.exp(-x))


def _silu(x):
    return x * _sigmoid(x)


def _gelu_tanh(x):
    return 0.5 * x * (1.0 + jnp.tanh(math.sqrt(2.0 / math.pi) * (x + 0.044715 * (x * x * x))))


def _modulated_norm(x, g, shift, scale):
    y = x * lax.rsqrt(jnp.mean(x * x, axis=-1, keepdims=True) + EPS) * g
    return y * (1.0 + scale) + shift


def _ada_kernel(c_ref, w_ref, b_ref, o_ref):
    c = c_ref[...]
    o_ref[0] = _dot(_silu(c), w_ref[0]) + b_ref[0]


def _ada_mod(c, ada_w, ada_b):
    B, D = c.shape
    n_mod = ada_w.shape[0] * ada_w.shape[1]
    w = ada_w.reshape(n_mod, D, 3 * D)
    b = ada_b.reshape(n_mod, 1, 3 * D)
    tn = 1024
    return pl.pallas_call(
        _ada_kernel,
        out_shape=jax.ShapeDtypeStruct((n_mod, B, 3 * D), F32),
        grid=(n_mod, 3 * D // tn),
        in_specs=[pl.BlockSpec((B, D), lambda i, j: (0, 0)),
                  pl.BlockSpec((1, D, tn), lambda i, j: (i, 0, j)),
                  pl.BlockSpec((1, 1, tn), lambda i, j: (i, 0, j))],
        out_specs=pl.BlockSpec((1, B, tn), lambda i, j: (i, 0, j)),
        compiler_params=_cparams(("parallel", "parallel")),
        name="ada_mod",
    )(c, w, b)


def _norm_mm_kernel(*refs, n_w):
    x_ref, g_ref, sh_ref, sc_ref = refs[:4]
    w_refs = refs[4:4 + n_w]
    o_refs = refs[4 + n_w:]
    h = _modulated_norm(x_ref[0], g_ref[...], sh_ref[0], sc_ref[0]).astype(BF16)
    for w_ref, o_ref in zip(w_refs, o_refs):
        o_ref[0] = _dot(h, w_ref[...]).astype(o_ref.dtype)


def _norm_matmul(x, g, shift, scale, weights, out_dtypes, tm=512):
    B, S, D = x.shape
    n_w = len(weights)
    vec = pl.BlockSpec((1, 1, D), lambda b, i: (b, 0, 0))
    in_specs = [pl.BlockSpec((1, tm, D), lambda b, i: (b, i, 0)),
                pl.BlockSpec((1, D), lambda b, i: (0, 0)), vec, vec]
    in_specs += [pl.BlockSpec(w.shape, lambda b, i: (0, 0)) for w in weights]
    return pl.pallas_call(
        functools.partial(_norm_mm_kernel, n_w=n_w),
        out_shape=[jax.ShapeDtypeStruct((B, S, w.shape[1]), dt) for w, dt in zip(weights, out_dtypes)],
        grid=(B, S // tm),
        in_specs=in_specs,
        out_specs=[pl.BlockSpec((1, tm, w.shape[1]), lambda b, i: (b, i, 0)) for w in weights],
        compiler_params=_cparams(("parallel", "parallel")),
        name="norm_matmul",
    )(x, g.reshape(1, D), shift, scale, *weights)


def _out_res_kernel(*refs, n_in):
    x_ref, gate_ref = refs[:2]
    a_refs = refs[2:2 + n_in]
    w_refs = refs[2 + n_in:2 + 2 * n_in]
    o_ref = refs[2 + 2 * n_in]
    acc = None
    for a_ref, w_ref in zip(a_refs, w_refs):
        t = _dot(a_ref[0].astype(BF16), w_ref[...])
        acc = t if acc is None else acc + t
    o_ref[0] = x_ref[0] + gate_ref[0] * acc


def _out_residual(x, gate, acts, weights, tm=512):
    B, S, D = x.shape
    n_in = len(acts)
    in_specs = [pl.BlockSpec((1, tm, D), lambda b, i: (b, i, 0)),
                pl.BlockSpec((1, 1, D), lambda b, i: (b, 0, 0))]
    in_specs += [pl.BlockSpec((1, tm, a.shape[2]), lambda b, i: (b, i, 0)) for a in acts]
    in_specs += [pl.BlockSpec(w.shape, lambda b, i: (0, 0)) for w in weights]
    return pl.pallas_call(
        functools.partial(_out_res_kernel, n_in=n_in),
        out_shape=jax.ShapeDtypeStruct((B, S, D), F32),
        grid=(B, S // tm),
        in_specs=in_specs,
        out_specs=pl.BlockSpec((1, tm, D), lambda b, i: (b, i, 0)),
        compiler_params=_cparams(("parallel", "parallel")),
        name="out_residual",
    )(x, gate, *acts, *weights)


def _mlstm_kernel(qk_ref, vo_ref, if_ref, cw_ref, gb_ref, hg_ref, tril_ref, o_ref,
                  xbuf, c_scr, n_scr, m_scr):
    L, H, DH = MLSTM_CHUNK, MLSTM_HEADS, MLSTM_DH
    pad = SUBLANES

    @pl.when(pl.program_id(1) == 0)
    def _():
        xbuf[0:pad, :] = jnp.zeros((pad, 2 * MIX_A), F32)
        c_scr[...] = jnp.zeros_like(c_scr)
        n_scr[...] = jnp.zeros_like(n_scr)
        m_scr[...] = jnp.zeros_like(m_scr)

    xbuf[pad:pad + L, :] = qk_ref[0].astype(F32)
    cw = cw_ref[...]
    conv = None
    for j in range(CONV_K):
        lo = pad - (CONV_K - 1) + j
        t = xbuf[lo:lo + L, :] * cw[j:j + 1, :]
        conv = t if conv is None else conv + t
    xbuf[0:pad, :] = xbuf[L:L + pad, :]
    qk = _silu(conv)
    q = qk[:, :MIX_A]
    k = qk[:, MIX_A:] * (DH ** -0.5)
    vo = vo_ref[0].astype(F32)
    v = vo[:, :MIX_A]
    o_pre = vo[:, MIX_A:]

    ifb = if_ref[0] + gb_ref[...]
    lf = jnp.minimum(ifb, 0.0) - jnp.log1p(jnp.exp(-jnp.abs(ifb)))
    bcs = _dot(tril_ref[...], lf, precision=HIGHEST)
    ifb_t = ifb.T
    bcs_t = bcs.T
    row = lax.broadcasted_iota(jnp.int32, (L, L), 0)
    col = lax.broadcasted_iota(jnp.int32, (L, L), 1)
    causal = col <= row

    outs = []
    for h in range(H):
        sl = slice(h * DH, (h + 1) * DH)
        qh, kh, vh = q[:, sl], k[:, sl], v[:, sl]
        qb, kb = qh.astype(BF16), kh.astype(BF16)
        b_col = bcs[:, H + h:H + h + 1]
        b_row = bcs_t[H + h:H + h + 1, :]
        li_col = ifb[:, h:h + 1]
        li_row = ifb_t[h:h + 1, :]
        b_last = b_col[L - 1:L, :]
        m0 = m_scr[h][:, 0:1]
        c0 = c_scr[h]
        n0 = n_scr[h]

        log_d = jnp.where(causal, b_col - b_row + li_row, NEG)
        log_inter = b_col + m0
        m_t = jnp.maximum(log_inter, jnp.max(log_d, axis=1, keepdims=True))
        dmat = jnp.exp(log_d - m_t)
        a_inter = jnp.exp(log_inter - m_t)
        s = _dot_nt(qb, kb) * dmat
        num = _dot(s.astype(BF16), vh.astype(BF16)) + a_inter * _dot_nt(qb, c0.astype(BF16))
        den = jnp.sum(s, axis=1, keepdims=True) + a_inter * jnp.sum(qh * n0, axis=1, keepdims=True)
        hh = num / jnp.maximum(jnp.abs(den), jnp.exp(-m_t))

        w_col = b_last - b_col + li_col
        m_loc = jnp.max(w_col, axis=0, keepdims=True)
        e = jnp.exp(w_col - m_loc)
        c_loc = _dot((vh * e).T.astype(BF16), kb)
        n_loc = jnp.sum(kh * e, axis=0, keepdims=True)
        m_new = jnp.maximum(b_last + m0, m_loc)
        a = jnp.exp(b_last + m0 - m_new)
        sc = jnp.exp(m_loc - m_new)
        c_scr[h] = a * c0 + sc * c_loc
        n_scr[h] = a * n0 + sc * n_loc
        m_scr[h] = jnp.broadcast_to(m_new, (1, LANES))

        outs.append(hh * lax.rsqrt(jnp.mean(hh * hh, axis=1, keepdims=True) + EPS))
    hm = jnp.concatenate(outs, axis=1)
    o_ref[0] = (_sigmoid(o_pre) * (hm * hg_ref[...])).astype(o_ref.dtype)


def _mlstm(qk, vo, ifg, conv_w, gate_bias, head_g):
    B, S, _ = qk.shape
    L, H, DH = MLSTM_CHUNK, MLSTM_HEADS, MLSTM_DH
    tril = jnp.tril(jnp.ones((L, L), F32))
    return pl.pallas_call(
        _mlstm_kernel,
        out_shape=jax.ShapeDtypeStruct((B, S, MIX_A), BF16),
        grid=(B, S // L),
        in_specs=[pl.BlockSpec((1, L, 2 * MIX_A), lambda b, c: (b, c, 0)),
                  pl.BlockSpec((1, L, 2 * MIX_A), lambda b, c: (b, c, 0)),
                  pl.BlockSpec((1, L, LANES), lambda b, c: (b, c, 0)),
                  pl.BlockSpec((CONV_K, 2 * MIX_A), lambda b, c: (0, 0)),
                  pl.BlockSpec((1, LANES), lambda b, c: (0, 0)),
                  pl.BlockSpec((1, MIX_A), lambda b, c: (0, 0)),
                  pl.BlockSpec((L, L), lambda b, c: (0, 0))],
        out_specs=pl.BlockSpec((1, L, MIX_A), lambda b, c: (b, c, 0)),
        scratch_shapes=[pltpu.VMEM((L + SUBLANES, 2 * MIX_A), F32),
                        pltpu.VMEM((H, DH, DH), F32),
                        pltpu.VMEM((H, 1, DH), F32),
                        pltpu.VMEM((H, 1, LANES), F32)],
        compiler_params=_cparams(("parallel", "arbitrary")),
        name="mlstm",
    )(qk, vo, ifg, conv_w, gate_bias, head_g, tril)


def _s5_kernel(u_ref, m_ref, hre_ref, him_ref, ere_ref, eim_ref, are_ref, aim_ref, d_ref, gw_ref, gb_ref, o_ref,
               xl_re, xl_im, x0_re, x0_im, *, n_chunks, batch):
    u = u_ref[0]
    xl_re[...] = _dot(u, hre_ref[0])
    xl_im[...] = _dot(u, him_ref[0])
    a_re = are_ref[0]
    a_im = aim_ref[0]

    def body(i, carry):
        re, im = carry
        r = pl.multiple_of(i * batch, batch)
        x0_re[pl.ds(r, batch), :] = re
        x0_im[pl.ds(r, batch), :] = im
        return (a_re * re - a_im * im + xl_re[pl.ds(r, batch), :],
                a_re * im + a_im * re + xl_im[pl.ds(r, batch), :])

    zero = jnp.zeros((batch, S5_STATE), F32)
    lax.fori_loop(0, n_chunks, body, (zero, zero), unroll=8)
    y = (_dot(u, m_ref[0]) + _dot(x0_re[...].astype(BF16), ere_ref[0]) + _dot(x0_im[...].astype(BF16), eim_ref[0])
         + u.astype(F32) * d_ref[0])
    ys = _gelu_tanh(y)
    z = _dot(ys.astype(BF16), gw_ref[0]) + gb_ref[0]
    o_ref[0] = (ys * _sigmoid(z)).astype(o_ref.dtype)


def _s5_tables(lam_re, lam_im, log_dt, b_re, b_im, c_re, c_im, d_skip, glu_w, glu_b):
    T, C, P = S5_CHUNK, S5_GROUP, S5_STATE
    G = lam_re.shape[0]
    lam = lax.complex(lam_re.astype(F32), lam_im.astype(F32))
    dt = jnp.exp(log_dt.astype(F32))[:, None]
    lam_bar = jnp.exp(lam * dt)
    b_bar = ((lam_bar - 1.0) / lam)[..., None] * lax.complex(b_re.astype(F32), b_im.astype(F32))
    c_mat = lax.complex(c_re.astype(F32), c_im.astype(F32))
    taus = jnp.arange(T + 1, dtype=F32)
    pw = jnp.exp((lam * dt)[:, None, :] * taus[None, :, None])
    kern = jnp.einsum('gcp,gtp,gpd->gtcd', c_mat, pw[:, :T], b_bar,
                      precision=HIGHEST).real
    tt = jnp.arange(T)
    shift = (tt[:, None, None] - tt[None, :, None] == tt[None, None, :]).astype(F32)
    toe = jnp.einsum('tsu,gucd->gtscd', shift, kern, precision=HIGHEST)
    m_t = toe.transpose(0, 2, 4, 1, 3).reshape(G, T * C, T * C).astype(BF16)
    hmat = (pw[:, :T][:, ::-1, :, None] * b_bar[:, None]).transpose(0, 1, 3, 2).reshape(G, T * C, P)
    emat = (c_mat[:, None] * pw[:, 1:][:, :, None, :]).reshape(G, T * C, P).transpose(0, 2, 1)
    a_re, a_im = pw[:, T].real, pw[:, T].imag
    rows8 = lambda a: jnp.broadcast_to(a[:, None], (G, SUBLANES, P))
    d_t = jnp.tile(d_skip.astype(F32), (1, T))[:, None]
    eye = jnp.eye(T, dtype=F32)
    gw = jnp.einsum('ts,gce->gtcse', eye, glu_w.astype(F32)).reshape(G, T * C, T * C).astype(BF16)
    gb = jnp.tile(glu_b.astype(F32), (1, T))[:, None]
    return (m_t, hmat.real.astype(BF16), hmat.imag.astype(BF16), emat.real.astype(BF16), (-emat.imag).astype(BF16),
            rows8(a_re), rows8(a_im), d_t, gw, gb)


def _s5(u, tables):
    B, S, _ = u.shape
    T, C, P, G = S5_CHUNK, S5_GROUP, S5_STATE, S5_GROUPS
    assert B == SUBLANES
    n_chunks = S // T
    rows = n_chunks * B
    ug = u.reshape(B, n_chunks, T, G, C).transpose(3, 1, 0, 2, 4).reshape(G, rows, T * C)
    per_g = lambda a: pl.BlockSpec((1,) + a.shape[1:], lambda g: (g, 0, 0))
    out = pl.pallas_call(
        functools.partial(_s5_kernel, n_chunks=n_chunks, batch=B),
        out_shape=jax.ShapeDtypeStruct((G, rows, T * C), BF16),
        grid=(G,),
        in_specs=[per_g(ug)] + [per_g(t) for t in tables],
        out_specs=per_g(ug),
        scratch_shapes=[pltpu.VMEM((rows, P), F32) for _ in range(4)],
        compiler_params=_cparams(("parallel",)),
        name="s5",
    )(ug, *tables)
    return out.reshape(G, n_chunks, B, T, C).transpose(2, 1, 3, 0, 4).reshape(B, S, G * C)


S5_LT = LANES // S5_GROUP
S5_PAIRS = S5_CHUNK // 2


def _s5s_kernel(u_ref, h_ref, e_ref, kk_ref, are_ref, aim_ref, d_ref, gw_ref, gb_ref, o_ref, xl_scr, x0_scr):
    n_chunks = u_ref.shape[1] // S5_CHUNK
    half = S5_LT * S5_STATE
    tok = lambda s: u_ref[0, pl.ds(s, n_chunks, stride=S5_CHUNK), :]
    u2 = [jnp.concatenate([tok(2 * q), tok(2 * q + 1)], axis=1) for q in range(S5_PAIRS)]
    u2b = [v.astype(BF16) for v in u2]
    xl_scr[...] = functools.reduce(lambda a, b: a + b, [_dot(u2b[q], h_ref[0, q]) for q in range(S5_PAIRS)])
    a_re = are_ref[0]
    a_im = aim_ref[0]

    def body(a, carry):
        re, im = carry
        x0_scr[pl.ds(a, 1), 0:half] = re
        x0_scr[pl.ds(a, 1), half:2 * half] = im
        return (a_re * re - a_im * im + xl_scr[pl.ds(a, 1), 0:half],
                a_re * im + a_im * re + xl_scr[pl.ds(a, 1), half:2 * half])

    zero = jnp.zeros((1, half), F32)
    lax.fori_loop(0, n_chunks, body, (zero, zero), unroll=8)
    x0 = x0_scr[...].astype(BF16)
    for p in range(S5_PAIRS):
        y = _dot(x0, e_ref[0, p]) + u2[p] * d_ref[0]
        for q in range(p + 1):
            y = y + _dot(u2b[q], kk_ref[0, p - q])
        ys = _gelu_tanh(y)
        out = ys * _sigmoid(_dot(ys.astype(BF16), gw_ref[0]) + gb_ref[0])
        o_ref[0, pl.ds(2 * p, n_chunks, stride=S5_CHUNK), :] = out[:, :LANES].astype(o_ref.dtype)
        o_ref[0, pl.ds(2 * p + 1, n_chunks, stride=S5_CHUNK), :] = out[:, LANES:].astype(o_ref.dtype)


def _s5s_tables(lam_re, lam_im, log_dt, b_re, b_im, c_re, c_im, d_skip, glu_w, glu_b):
    T, C, P, LT = S5_CHUNK, S5_GROUP, S5_STATE, S5_LT
    G = lam_re.shape[0]
    NT = G // LT
    lam = lax.complex(lam_re.astype(F32), lam_im.astype(F32))
    dt = jnp.exp(log_dt.astype(F32))[:, None]
    lam_bar = jnp.exp(lam * dt)
    b_bar = ((lam_bar - 1.0) / lam)[..., None] * lax.complex(b_re.astype(F32), b_im.astype(F32))
    c_mat = lax.complex(c_re.astype(F32), c_im.astype(F32))
    taus = jnp.arange(T + 1, dtype=F32)
    pw = jnp.exp((lam * dt)[:, None, :] * taus[None, :, None])
    eye = jnp.eye(LT, dtype=F32)
    tiles = lambda a: a.reshape((NT, LT) + a.shape[1:])

    kern = jnp.einsum('gcp,gtp,gpd->gtdc', c_mat, pw[:, :T], b_bar, precision=HIGHEST).real
    kblk = jnp.einsum('nitdc,ij->ntidjc', tiles(kern), eye).reshape(NT, T, LANES, LANES)
    kblk = jnp.concatenate([jnp.zeros_like(kblk[:, :1]), kblk], axis=1)
    kk = jnp.stack([jnp.concatenate([jnp.concatenate([kblk[:, 2 * d + 1], kblk[:, 2 * d + 2]], axis=2),
                                     jnp.concatenate([kblk[:, 2 * d], kblk[:, 2 * d + 1]], axis=2)], axis=1)
                    for d in range(T // 2)], axis=1)

    hmat = pw[:, :T][:, ::-1, :, None] * b_bar[:, None]

    def state_cols(m):
        return jnp.einsum('nispc,ij->nsicjp', tiles(m), eye).reshape(NT, T, LANES, LT * P)

    h = jnp.concatenate([state_cols(hmat.real), state_cols(hmat.imag)], axis=3)
    h2 = h.reshape(NT, T // 2, 2 * LANES, 2 * LT * P)

    emat = c_mat[:, None] * pw[:, 1:][:, :, None, :]

    def state_rows(m):
        return jnp.einsum('nitcp,ij->ntjpic', tiles(m), eye).reshape(NT, T, LT * P, LANES)

    e = jnp.concatenate([state_rows(emat.real), state_rows(-emat.imag)], axis=2)
    e2 = e.reshape(NT, T // 2, 2, 2 * LT * P, LANES).transpose(0, 1, 3, 2, 4).reshape(NT, T // 2, 2 * LT * P, 2 * LANES)

    a_re = pw[:, T].real.reshape(NT, 1, LT * P)
    a_im = pw[:, T].imag.reshape(NT, 1, LT * P)
    pair = lambda v: jnp.tile(v.astype(F32).reshape(NT, 1, LANES), (1, 1, 2))
    gwb = jnp.einsum('nice,ij->nicje', tiles(glu_w.astype(F32)), eye).reshape(NT, LANES, LANES)
    zeros = jnp.zeros_like(gwb)
    gw2 = jnp.concatenate([jnp.concatenate([gwb, zeros], axis=2), jnp.concatenate([zeros, gwb], axis=2)], axis=1)
    return (h2.astype(BF16), e2.astype(BF16), kk.astype(BF16), a_re, a_im, pair(d_skip), gw2.astype(BF16), pair(glu_b))


def _s5s(u, tables):
    B, S, W = u.shape
    NT = W // LANES
    n_chunks = S // S5_CHUNK
    per_tile = lambda a: pl.BlockSpec((1,) + a.shape[1:], lambda j, b: (j,) + (0,) * (a.ndim - 1))
    return pl.pallas_call(
        _s5s_kernel,
        out_shape=jax.ShapeDtypeStruct((B, S, W), F32),
        grid=(NT, B),
        in_specs=[pl.BlockSpec((1, S, LANES), lambda j, b: (b, 0, j))] + [per_tile(t) for t in tables],
        out_specs=pl.BlockSpec((1, S, LANES), lambda j, b: (b, 0, j)),
        scratch_shapes=[pltpu.VMEM((n_chunks, 2 * S5_LT * S5_STATE), F32) for _ in range(2)],
        compiler_params=_cparams(("parallel", "parallel")),
        name="s5",
    )(u, *tables)


def _compress_kernel(x_ref, plo_ref, phi_ref, w1_ref, b1_ref, w2_ref, b2_ref, o_ref):
    x = x_ref[0, 0]
    half = x.shape[1]
    w1 = w1_ref[0]
    lo = _dot((x + plo_ref[0]).astype(BF16), w1[:half])
    hi = _dot((x + phi_ref[0]).astype(BF16), w1[half:])
    rows = x.shape[0]
    hid = _gelu_tanh(lo + pltpu.roll(hi, rows - 1, 0) + b1_ref[0])
    o_ref[0, 0] = _dot(hid.astype(BF16), w2_ref[0]) + b2_ref[0]


def _compress(xg, pos, w1, b1, w2, b2):
    _, B, rows, width = xg.shape
    pos_flat = pos.reshape(2, 2, 1, width).astype(F32)
    sel = lambda shape: pl.BlockSpec((1,) + shape, lambda j, b: (j, 0, 0))
    return pl.pallas_call(
        _compress_kernel,
        out_shape=jax.ShapeDtypeStruct((2, B, rows, NSA_DH), F32),
        grid=(2, B),
        in_specs=[pl.BlockSpec((1, 1, rows, width), lambda j, b: (j, b, 0, 0)),
                  sel((1, width)), sel((1, width)),
                  sel((2 * width, CMP_HIDDEN)), sel((1, CMP_HIDDEN)),
                  sel((CMP_HIDDEN, NSA_DH)), sel((1, NSA_DH))],
        out_specs=pl.BlockSpec((1, 1, rows, NSA_DH), lambda j, b: (j, b, 0, 0)),
        compiler_params=_cparams(("parallel", "parallel")),
        name="nsa_compress",
    )(xg, pos_flat[:, 0], pos_flat[:, 1], w1.astype(BF16), b1[:, None].astype(F32),
      w2.astype(BF16), b2[:, None].astype(F32))


def _nsa_kernel(q_ref, gp_ref, bg_ref, kc_ref, vc_ref, ks_ref, vs_ref, kw_ref, vw_ref,
                grev_ref, selb_ref, winb_ref, ovt_ref, ex_ref, o_ref):
    T = ATT_TILE
    R, DH = NSA_R, NSA_DH
    qi = pl.program_id(2)
    q0 = qi * T
    qall = q_ref[0] * (DH ** -0.5 * LOG2E)
    t_col = q0 + lax.broadcasted_iota(jnp.int32, (T, 1), 0)
    n_cmp_pad = kc_ref.shape[2]
    n_sel = ovt_ref.shape[0]
    grp_rows = CMP_STRIDE

    q4 = jnp.concatenate([qall[:, r * DH:(r + 1) * DH] for r in range(R)], axis=0).astype(BF16)

    n_row = lax.broadcasted_iota(jnp.int32, (1, n_cmp_pad), 1)
    cmask = (t_col >= n_row * CMP_STRIDE + (CMP_BLOCK - 1))[None]
    bias = jnp.stack([jnp.concatenate(
        [pltpu.roll(grev_ref[r], (qi * (T // grp_rows) + al + 1) % n_cmp_pad, 1) for al in range(T // grp_rows)],
        axis=0) for r in range(R)], axis=0)
    s = jnp.where(cmask, _dot_nt(q4, kc_ref[0, 0]).reshape(R, T, n_cmp_pad) + bias, NEG)
    p = jnp.exp2(s - jnp.max(s, axis=2, keepdims=True))
    p = p / jnp.sum(p, axis=2, keepdims=True)
    p = jnp.where(cmask, p, 0.0)
    o_cmp = _dot(p.reshape(R * T, n_cmp_pad).astype(BF16), vc_ref[0, 0])
    psum = functools.reduce(lambda a, b: a + b, [p[r] for r in range(R)])

    imp_t = lax.dot_general(ovt_ref[...], psum, (((1,), (1,)), ((), ())), precision=HIGHEST,
                            preferred_element_type=F32)
    jj = lax.broadcasted_iota(jnp.int32, (n_sel, T), 0)
    blk_t = (q0 + lax.broadcasted_iota(jnp.int32, (1, T), 1)) // SEL_BLOCK
    forced = (jj == 0) | (jj == blk_t) | (jj == blk_t - 1)
    score = jnp.where(forced, FORCE, jnp.where(jj <= blk_t, imp_t, -1.0))
    n_blk = n_sel // SUBLANES
    rows = [score[v * SUBLANES:(v + 1) * SUBLANES] for v in range(n_blk)]
    cnts = [jnp.zeros((SUBLANES, T), F32) for _ in range(n_blk)]
    sub = lax.broadcasted_iota(jnp.int32, (SUBLANES, T), 0)
    for j2 in range(n_sel):
        c2 = score[j2:j2 + 1, :]
        for v in range(n_blk):
            lo = v * SUBLANES
            if lo > j2:
                beats = c2 >= rows[v]
            elif lo + SUBLANES - 1 <= j2:
                beats = c2 > rows[v]
            else:
                beats = (c2 > rows[v]) | ((c2 >= rows[v]) & (sub > j2 - lo))
            cnts[v] = cnts[v] + jnp.where(beats, 1.0, 0.0)
    cnt = jnp.concatenate(cnts, axis=0)
    sel_t = jnp.where((cnt < float(min(SEL_TOPK, n_sel))) & (jj <= blk_t), 1.0, 0.0)
    sel_q = jnp.concatenate([sel_t, jnp.zeros((LANES - n_sel, T), F32)], axis=0).T
    lane = lax.broadcasted_iota(jnp.int32, (T, LANES), 1)
    sel_aug = jnp.where(lane == n_sel, 1.0, sel_q).astype(BF16)

    n_far = selb_ref.shape[0] - 1
    n_win = winb_ref.shape[0] - 2
    CH = ex_ref.shape[2] // T

    def sel_body(kc, carry):
        m, acc = carry
        off = pl.multiple_of(kc * (CH * T), CH * T)
        k = ks_ref[0, 0, pl.ds(off, CH * T), :]
        v = vs_ref[0, 0, pl.ds(off, CH * T), :]
        s = _dot_nt(q4, k)
        mask = _dot(sel_aug, ex_ref[kc])
        subs = []
        for j in range(CH):
            d = jnp.clip(qi - (kc * CH + j), 0, n_far)
            sj = s[:, j * T:(j + 1) * T].reshape(R, T, T) + selb_ref[d] + mask[:, j * T:(j + 1) * T][None]
            subs.append(sj.reshape(R * T, T))
        m_new = jnp.maximum(m, jnp.max(functools.reduce(jnp.maximum, subs), axis=1, keepdims=True))
        pb = jnp.concatenate([jnp.exp2(sj - m_new).astype(BF16) for sj in subs], axis=1)
        return m_new, jnp.exp2(m - m_new) * acc + _dot(pb, v)

    _, acc = lax.fori_loop(0, qi // CH + 1, sel_body,
                           (jnp.full((R * T, 1), NEG, F32), jnp.zeros((R * T, LANES), F32)))
    o_sel = acc[:, :DH] / acc[:, DH:DH + 1]

    subs, vals = [], []
    for d in range(n_win + 1):
        off = pl.multiple_of(jnp.maximum(qi - d, 0) * T, T)
        tile = jnp.where(qi >= d, d, n_win + 1)
        subs.append((_dot_nt(q4, kw_ref[0, 0, pl.ds(off, T), :]).reshape(R, T, T) + winb_ref[tile]).reshape(R * T, T))
        vals.append(vw_ref[0, 0, pl.ds(off, T), :])
    m_w = jnp.max(functools.reduce(jnp.maximum, subs), axis=1, keepdims=True)
    acc = functools.reduce(lambda a, b: a + b,
                           [_dot(jnp.exp2(sj - m_w).astype(BF16), vj) for sj, vj in zip(subs, vals)])
    o_win = acc[:, :DH] / acc[:, DH:DH + 1]

    gates = _sigmoid(gp_ref[0] + bg_ref[...])
    gcol = lambda j: jnp.concatenate([gates[:, 3 * r + j:3 * r + j + 1] for r in range(R)], axis=0)
    out4 = gcol(0) * o_cmp + gcol(1) * o_sel + gcol(2) * o_win
    o_ref[0] = jnp.concatenate([out4[r * T:(r + 1) * T] for r in range(R)], axis=1)


def _t5_bucket(dist):
    dist = jnp.maximum(dist, 0)
    max_exact = REL_BUCKETS // 2
    log_ratio = jnp.log(jnp.maximum(dist, 1).astype(F32) / max_exact) / math.log(REL_MAX_DIST / max_exact)
    large = jnp.minimum(max_exact + (log_ratio * (REL_BUCKETS - max_exact)).astype(jnp.int32), REL_BUCKETS - 1)
    return jnp.where(dist < max_exact, dist, large)


def _nsa_tables(rel_bias, S):
    T = ATT_TILE
    table = rel_bias.astype(F32) * LOG2E
    ii = jnp.arange(T)
    delta = ii[:, None] - ii[None, :]

    def tile(off):
        return table[_t5_bucket(off * T + delta)].transpose(2, 0, 1)

    n_far = -(-REL_MAX_DIST // T) + 1
    selb = [tile(o) for o in range(n_far + 1)]
    selb[0] = selb[0] + jnp.where(delta >= 0, 0.0, NEG)[None]
    selb = jnp.stack(selb, axis=0)
    n_win = WINDOW // T
    winb = []
    for o in range(n_win + 1):
        dist = o * T + delta
        ok = (dist >= 0) & (dist < WINDOW)
        winb.append(tile(o) + jnp.where(ok, 0.0, NEG)[None])
    winb.append(jnp.full_like(winb[0], NEG))
    winb = jnp.stack(winb, axis=0)
    n_pad = S // CMP_STRIDE
    i16 = jnp.arange(CMP_STRIDE)
    dd = jnp.arange(n_pad)
    gdist = CMP_STRIDE * dd[None, :] + i16[:, None] - (CMP_BLOCK - 1)
    grev = table[_t5_bucket(gdist)].transpose(2, 0, 1)[:, :, ::-1]
    n_sel = S // SEL_BLOCK
    cmp_start = jnp.arange(n_pad) * CMP_STRIDE
    sel_start = jnp.arange(n_sel) * SEL_BLOCK
    overlap = jnp.clip(jnp.minimum(cmp_start[:, None] + CMP_BLOCK, sel_start[None] + SEL_BLOCK)
                       - jnp.maximum(cmp_start[:, None], sel_start[None]), 0).astype(F32) / CMP_BLOCK
    n_cmp = (S - CMP_BLOCK) // CMP_STRIDE + 1
    overlap_t = jnp.where((jnp.arange(n_pad) < n_cmp)[:, None], overlap, 0.0).T
    tk = min(SEL_CHUNK * T, S)
    kpos_blk = jnp.arange(S) // SEL_BLOCK
    rows = jnp.arange(LANES)[:, None]
    expand = jnp.where(rows == kpos_blk[None, :], BIG, jnp.where(rows == n_sel, -BIG, 0.0)).astype(BF16)
    expand = expand.reshape(LANES, S // tk, tk).transpose(1, 0, 2)
    return grev, selb, winb, overlap_t, expand


def _nsa_attention(q, gp, bg, kcmp, vcmp, ks, vs, kw, vw, tables):
    B, S, D = q.shape
    T = ATT_TILE
    grev, selb, winb, overlap_t, expand = tables
    gw = NSA_R * NSA_DH
    seq = lambda a: pl.BlockSpec((1, 1) + a.shape[2:], lambda b, g, i: (b, g, 0, 0))
    per_head = lambda a: pl.BlockSpec((NSA_R,) + a.shape[1:], lambda b, g, i: (g,) + (0,) * (a.ndim - 1))
    tiles = lambda a: pl.BlockSpec((a.shape[0], NSA_R) + a.shape[2:], lambda b, g, i: (0, g, 0, 0))
    full = lambda a: pl.BlockSpec(a.shape, lambda b, g, i: (0,) * a.ndim)
    return pl.pallas_call(
        _nsa_kernel,
        out_shape=jax.ShapeDtypeStruct((B, S, D), F32),
        grid=(B, NSA_KV, S // T),
        in_specs=[pl.BlockSpec((1, T, gw), lambda b, g, i: (b, i, g)),
                  pl.BlockSpec((1, T, LANES), lambda b, g, i: (b, i, g)),
                  pl.BlockSpec((1, LANES), lambda b, g, i: (0, g)),
                  seq(kcmp), seq(vcmp), seq(ks), seq(vs), seq(kw), seq(vw),
                  per_head(grev), tiles(selb), tiles(winb), full(overlap_t), full(expand)],
        out_specs=pl.BlockSpec((1, T, gw), lambda b, g, i: (b, i, g)),
        compiler_params=_cparams(("parallel", "parallel", "arbitrary")),
        name="nsa_attention",
    )(q, gp, bg, kcmp, vcmp, ks, vs, kw, vw, grev, selb, winb, overlap_t, expand)


def _nsa_proj_kernel(x_ref, g_ref, sh_ref, sc_ref, wq_ref, wk_ref, wv_ref, wg_ref, bg_ref,
                     q4_ref, gv_ref, kc_ref, vc_ref, ks_ref, kw_ref, vst_ref, vwt_ref):
    KV, R, DH, T = NSA_KV, NSA_R, NSA_DH, ATT_TILE
    h = _modulated_norm(x_ref[0], g_ref[...], sh_ref[0], sc_ref[0]).astype(BF16)
    q_t = (_dot(h, wq_ref[...]) * (DH ** -0.5 * LOG2E)).T.astype(BF16)
    gates_t = _sigmoid(_dot(h, wg_ref[...]) + bg_ref[...]).T
    row = lax.broadcasted_iota(jnp.int32, (SUBLANES, R * T), 0)
    for g in range(KV):
        q4_ref[0, g, 0] = jnp.concatenate([q_t[(g * R + r) * DH:(g * R + r + 1) * DH] for r in range(R)], axis=1)
        gv = jnp.zeros((SUBLANES, R * T), F32)
        for j in range(3):
            gj = jnp.concatenate([gates_t[g * LANES + 3 * r + j:g * LANES + 3 * r + j + 1] for r in range(R)], axis=1)
            gv = jnp.where(row == j, gj, gv)
        gv_ref[0, g, 0] = gv
    k3 = _dot(h, wk_ref[...])
    v3 = _dot(h, wv_ref[...])
    vs_t = v3[:, KV_W:2 * KV_W].T.astype(BF16)
    vw_t = v3[:, 2 * KV_W:].T.astype(BF16)
    for g in range(KV):
        cols = slice(g * DH, (g + 1) * DH)
        kc_ref[0, g] = k3[:, cols].astype(BF16)
        vc_ref[0, g] = v3[:, cols].astype(BF16)
        ks_ref[0, g] = k3[:, KV_W + g * DH:KV_W + (g + 1) * DH].astype(BF16)
        kw_ref[0, g] = k3[:, 2 * KV_W + g * DH:2 * KV_W + (g + 1) * DH].astype(BF16)
        vst_ref[0, g, 0] = vs_t[cols]
        vwt_ref[0, g, 0] = vw_t[cols]


def _nsa_proj(x, g, shift, scale, weights, b_gate):
    B, S, D = x.shape
    KV, R, DH, T = NSA_KV, NSA_R, NSA_DH, ATT_TILE
    vec = pl.BlockSpec((1, 1, D), lambda b, i: (b, 0, 0))
    keys = pl.BlockSpec((1, KV, T, DH), lambda b, i: (b, 0, i, 0))
    key_shape = jax.ShapeDtypeStruct((B, KV, S, DH), BF16)
    tile = lambda rows, width: pl.BlockSpec((1, KV, 1, rows, width), lambda b, i: (b, 0, i, 0, 0))
    tile_shape = lambda rows, width, dt: jax.ShapeDtypeStruct((B, KV, S // T, rows, width), dt)
    return pl.pallas_call(
        _nsa_proj_kernel,
        out_shape=[tile_shape(DH, R * T, BF16), tile_shape(SUBLANES, R * T, F32),
                   key_shape, key_shape, key_shape, key_shape,
                   tile_shape(DH, T, BF16), tile_shape(DH, T, BF16)],
        grid=(B, S // T),
        in_specs=[pl.BlockSpec((1, T, D), lambda b, i: (b, i, 0)),
                  pl.BlockSpec((1, D), lambda b, i: (0, 0)), vec, vec]
                 + [pl.BlockSpec(w.shape, lambda b, i: (0, 0)) for w in weights]
                 + [pl.BlockSpec(b_gate.shape, lambda b, i: (0, 0))],
        out_specs=[tile(DH, R * T), tile(SUBLANES, R * T), keys, keys, keys, keys, tile(DH, T), tile(DH, T)],
        compiler_params=_cparams(("parallel", "parallel")),
        name="nsa_proj",
    )(x, g.reshape(1, D), shift, scale, *weights, b_gate)


def _nsa_t_kernel(q4_ref, gv_ref, kc_ref, vct_ref, ks_ref, vst_ref, kw_ref, vwt_ref,
                  cfar_ref, band_ref, selb_ref, winb_ref, ovt_ref, o_ref, s_scr, sel_scr, sbuf):
    T = ATT_TILE
    R, DH = NSA_R, NSA_DH
    qi = pl.program_id(2)
    q0 = qi * T
    n_pad = kc_ref.shape[2]
    n_sel = ovt_ref.shape[0]
    CH = SEL_CHUNK
    n_far = selb_ref.shape[0] - 1
    n_win = winb_ref.shape[0] - 2
    band_rows = band_ref.shape[2] - T // CMP_STRIDE * 2

    q4 = q4_ref[0, 0, 0]
    t_lane = q0 + lax.broadcasted_iota(jnp.int32, (1, R * T), 1) % T

    with_ones = lambda v_t: jnp.concatenate([v_t, jnp.ones_like(v_t)], axis=0)
    gvec = lambda j: gv_ref[0, 0, 0, j:j + 1, :]

    grp = T // CMP_STRIDE
    s_scr[0:n_pad, :] = _dot(kc_ref[0, 0], q4) + cfar_ref[0]
    s_scr[n_pad:n_pad + 2 * grp, :] = jnp.zeros((2 * grp, R * T), F32)
    r0 = jnp.maximum(qi * grp - 2 * grp, 0)
    x0 = r0 - (qi * grp - 2 * grp)
    r0 = pl.multiple_of(r0, SUBLANES)
    x0 = pl.multiple_of(x0, SUBLANES)
    s_scr[pl.ds(r0, band_rows), :] += band_ref[0, 0, pl.ds(x0, band_rows), :]
    lim = pl.multiple_of(qi * grp + 2 * grp, SUBLANES)
    s_scr[pl.ds(lim, n_pad), :] = jnp.full((n_pad, R * T), NEG, F32)

    w_subs, w_vals = [], []
    for d in range(n_win + 1):
        kt = jnp.maximum(qi - d, 0)
        off = pl.multiple_of(kt * T, T)
        tile = jnp.where(qi >= d, d, n_win + 1)
        w_subs.append((_dot(kw_ref[0, 0, pl.ds(off, T), :], q4) + winb_ref[tile, 0]).astype(BF16))
        w_vals.append(with_ones(vwt_ref[0, 0, kt]))

    s = s_scr[0:n_pad, :]
    e = jnp.exp2(s - jnp.max(s, axis=0, keepdims=True))
    inv = jnp.where(t_lane >= CMP_BLOCK - 1, 1.0 / jnp.sum(e, axis=0, keepdims=True), 0.0)
    p = e * inv
    o_cmp = _dot(vct_ref[0, 0], p.astype(BF16))
    psum = functools.reduce(lambda a, b: a + b, [p[:, r * T:(r + 1) * T] for r in range(R)])

    m_w = jnp.max(functools.reduce(jnp.maximum, w_subs), axis=0, keepdims=True)
    acc = functools.reduce(lambda a, b: a + b,
                           [_dot(vj, jnp.exp2(sj - m_w)) for sj, vj in zip(w_subs, w_vals)])
    o_win = acc[:DH] * (1.0 / acc[DH:DH + 1])
    out_t = gvec(0) * o_cmp + gvec(2) * o_win

    imp_t = _dot(ovt_ref[...], psum, precision=HIGHEST)
    jj = lax.broadcasted_iota(jnp.int32, (n_sel, T), 0)
    blk_t = (q0 + lax.broadcasted_iota(jnp.int32, (1, T), 1)) // SEL_BLOCK
    forced = (jj == 0) | (jj == blk_t) | (jj == blk_t - 1)
    score = jnp.where(forced, FORCE, jnp.where(jj <= blk_t, imp_t, -1.0))
    n_blk = n_sel // SUBLANES
    rows = [score[v * SUBLANES:(v + 1) * SUBLANES] for v in range(n_blk)]
    cnts = [jnp.zeros((SUBLANES, T), F32) for _ in range(n_blk)]
    sub = lax.broadcasted_iota(jnp.int32, (SUBLANES, T), 0)
    for j2 in range(n_sel):
        c2 = score[j2:j2 + 1, :]
        for v in range(n_blk):
            lo = v * SUBLANES
            if lo > j2:
                beats = c2 >= rows[v]
            elif lo + SUBLANES - 1 <= j2:
                beats = c2 > rows[v]
            else:
                beats = (c2 > rows[v]) | ((c2 >= rows[v]) & (sub > j2 - lo))
            cnts[v] = cnts[v] + jnp.where(beats, 1.0, 0.0)
    cnt = jnp.concatenate(cnts, axis=0)
    chosen = (cnt < float(min(SEL_TOPK, n_sel))) & (jj <= blk_t)
    sel_scr[...] = jnp.where(chosen, 0.0, -BIG)

    def block_mask(kt):
        per_tile = T // SEL_BLOCK
        parts = [jnp.broadcast_to(sel_scr[pl.ds(kt * per_tile + i, 1), :], (SEL_BLOCK, T)) for i in range(per_tile)]
        m1 = jnp.concatenate(parts, axis=0)
        return jnp.concatenate([m1] * R, axis=1)

    def sel_scores(slot, kc):
        off = pl.multiple_of(kc * (CH * T), CH * T)
        s = _dot(ks_ref[0, 0, pl.ds(off, CH * T), :], q4)
        subs = []
        for j in range(CH):
            kt = kc * CH + j
            d = jnp.clip(qi - kt, 0, n_far)
            subs.append(s[j * T:(j + 1) * T] + selb_ref[d, 0] + block_mask(kt))
        s = jnp.concatenate(subs, axis=0).astype(BF16)
        sbuf[slot] = s
        return jnp.max(s, axis=0, keepdims=True).astype(F32)

    def sel_weighted(slot, kc, m_new):
        v_t = jnp.concatenate([vst_ref[0, 0, kc * CH + j] for j in range(CH)], axis=1)
        return _dot(with_ones(v_t), jnp.exp2(sbuf[slot] - m_new.astype(BF16)))

    last_chunk = vst_ref.shape[2] // CH - 1

    def sel_body(i, carry):
        m, acc, m_even = carry
        m_odd = sel_scores(1, 2 * i + 1)
        m_new = jnp.maximum(m, m_even)
        acc = jnp.exp2(m - m_new) * acc + sel_weighted(0, 2 * i, m_new)
        m_even = sel_scores(0, jnp.minimum(2 * i + 2, last_chunk))
        m_fin = jnp.maximum(m_new, m_odd)
        acc = jnp.exp2(m_new - m_fin) * acc + sel_weighted(1, 2 * i + 1, m_fin)
        return m_fin, acc, m_even

    n_chunks = qi // CH + 1
    _, acc, _ = lax.fori_loop(0, (n_chunks + 1) // 2, sel_body,
                              (jnp.full((1, R * T), NEG, F32), jnp.zeros((2 * DH, R * T), F32), sel_scores(0, 0)))
    out_t = out_t + gvec(1) * (acc[:DH] * (1.0 / acc[DH:DH + 1]))
    for pr in range(R // 2):
        pair = jnp.concatenate([out_t[:, (2 * pr) * T:(2 * pr + 1) * T],
                                out_t[:, (2 * pr + 1) * T:(2 * pr + 2) * T]], axis=0)
        o_ref[0, :, pr * 2 * DH:(pr + 1) * 2 * DH] = pair.T.astype(o_ref.dtype)


def _bias_lookup(table, dist):
    idx = _t5_bucket(dist)
    out = jnp.zeros(idx.shape + (table.shape[1],), F32)
    for k in range(table.shape[0]):
        out = out + jnp.where((idx == k)[..., None], table[k], 0.0)
    return out


def _nsa_t_tables(rel_bias, S):
    T, R, KV = ATT_TILE, NSA_R, NSA_KV
    table = rel_bias.astype(F32) * LOG2E
    ii = jnp.arange(T)
    delta = ii[None, :] - ii[:, None]

    def lanes(a):
        a = jnp.moveaxis(a, -1, 0)
        a = a.reshape((KV, R) + a.shape[1:])
        return jnp.moveaxis(a, 1, 2).reshape(KV, a.shape[2], R * a.shape[3])

    def tile(off):
        return lanes(_bias_lookup(table, off * T + delta))

    mask4 = lambda ok: jnp.tile(jnp.where(ok, 0.0, NEG), (1, R))[None]
    n_far = -(-REL_MAX_DIST // T) + 1
    selb = [tile(o) for o in range(n_far + 1)]
    selb[0] = selb[0] + mask4(delta >= 0)
    selb = jnp.stack(selb, axis=0)
    n_win = WINDOW // T
    winb = [tile(o) + mask4((o * T + delta >= 0) & (o * T + delta < WINDOW)) for o in range(n_win + 1)]
    winb.append(jnp.full_like(winb[0], NEG))
    winb = jnp.stack(winb, axis=0)

    grp = T // CMP_STRIDE
    far = _bias_lookup(table, jnp.asarray(2 * REL_MAX_DIST))
    xx = jnp.arange(4 * grp)
    bdist = ii[None, :] - CMP_STRIDE * (xx[:, None] - 2 * grp) - (CMP_BLOCK - 1)
    band = jnp.where((bdist >= 0)[..., None], _bias_lookup(table, bdist) - far, NEG)
    band = jnp.concatenate([lanes(band), jnp.zeros((KV, 2 * grp, R * T), F32)], axis=1)[:, None]
    cfar = jnp.repeat(far.reshape(KV, R), T, axis=1)[:, None]

    n_pad = S // CMP_STRIDE
    n_sel = S // SEL_BLOCK
    cmp_start = jnp.arange(n_pad) * CMP_STRIDE
    sel_start = jnp.arange(n_sel) * SEL_BLOCK
    overlap = jnp.clip(jnp.minimum(cmp_start[:, None] + CMP_BLOCK, sel_start[None] + SEL_BLOCK)
                       - jnp.maximum(cmp_start[:, None], sel_start[None]), 0).astype(F32) / CMP_BLOCK
    n_cmp = (S - CMP_BLOCK) // CMP_STRIDE + 1
    overlap_t = jnp.where((jnp.arange(n_pad) < n_cmp)[:, None], overlap, 0.0).T
    return cfar, band, selb, winb, overlap_t


def _nsa_t_attention(q4, gv, kcmp, vcmp_t, ks, vs_t, kw, vw_t, tables):
    B, KV, S, DH = ks.shape
    T = ATT_TILE
    cfar, band, selb, winb, overlap_t = tables
    gw = NSA_R * DH
    n_pad = kcmp.shape[2]
    seq = lambda a: pl.BlockSpec((1, 1) + a.shape[2:], lambda b, g, i: (b, g) + (0,) * (a.ndim - 2))
    qtile = lambda a: pl.BlockSpec((1, 1, 1) + a.shape[3:], lambda b, g, i: (b, g, i, 0, 0))
    grp = lambda a: pl.BlockSpec((1,) + a.shape[1:], lambda b, g, i: (g,) + (0,) * (a.ndim - 1))
    tiles = lambda a: pl.BlockSpec((a.shape[0], 1) + a.shape[2:], lambda b, g, i: (0, g, 0, 0))
    full = lambda a: pl.BlockSpec(a.shape, lambda b, g, i: (0,) * a.ndim)
    return pl.pallas_call(
        _nsa_t_kernel,
        out_shape=jax.ShapeDtypeStruct((B, S, KV * gw), BF16),
        grid=(B, KV, S // T),
        in_specs=[qtile(q4), qtile(gv),
                  seq(kcmp), seq(vcmp_t), seq(ks), seq(vs_t), seq(kw), seq(vw_t),
                  grp(cfar), grp(band), tiles(selb), tiles(winb), full(overlap_t)],
        out_specs=pl.BlockSpec((1, T, gw), lambda b, g, i: (b, i, g)),
        scratch_shapes=[pltpu.VMEM((2 * n_pad + 2 * (T // CMP_STRIDE), NSA_R * T), F32),
                        pltpu.VMEM((S // SEL_BLOCK, T), F32),
                        pltpu.VMEM((2, SEL_CHUNK * T, NSA_R * T), BF16)],
        compiler_params=_cparams(("parallel", "parallel", "arbitrary")),
        name="nsa_attention",
    )(q4, gv, kcmp, vcmp_t, ks, vs_t, kw, vw_t, cfar, band, selb, winb, overlap_t)


def _moe_kernel(x_ref, g_ref, sh_ref, sc_ref, gate_ref, wr_ref, br_ref, wg_ref, wu_ref, wd_ref, fg_ref,
                o_ref, hb_scr, rt_scr, acc_scr, *, final):
    NG, PG, FH = MOE_GROUPS, MOE_PER_GROUP, MOE_HIDDEN
    c = pl.program_id(2)

    @pl.when(c == 0)
    def _():
        h = _modulated_norm(x_ref[0], g_ref[...], sh_ref[0], sc_ref[0])
        h_hi = h.astype(BF16)
        hb_scr[...] = h_hi
        h_lo = (h - h_hi.astype(F32)).astype(BF16)
        logits = (_dot(h_hi, wr_ref[0]) + _dot(h_lo, wr_ref[0]) + _dot(h_hi, wr_ref[1]) + br_ref[...]).T
        gl = [logits[NG * PG + g:NG * PG + g + 1, :] for g in range(NG)]
        gmax = functools.reduce(jnp.maximum, gl)
        gtop = jnp.full_like(gmax, float(NG - 1))
        for g in reversed(range(NG - 1)):
            gtop = jnp.where(gl[g] == gmax, float(g), gtop)
        p_g = 1.0 / functools.reduce(lambda a, b: a + b, [jnp.exp(v - gmax) for v in gl])
        a = []
        for j in range(PG):
            v = logits[(NG - 1) * PG + j:(NG - 1) * PG + j + 1, :]
            for g in reversed(range(NG - 1)):
                v = jnp.where(gtop == float(g), logits[g * PG + j:g * PG + j + 1, :], v)
            a.append(v)

        def first_max(vals):
            vmax = functools.reduce(jnp.maximum, vals)
            taken = jnp.zeros_like(vmax) > 1.0
            hits = []
            for v in vals:
                hit = (v == vmax) & jnp.logical_not(taken)
                taken = taken | hit
                hits.append(hit)
            return vmax, hits

        v1, hit1 = first_max(a)
        rest = [jnp.where(hh, -jnp.inf, v) for hh, v in zip(hit1, a)]
        v2, hit2 = first_max(rest)
        e2 = jnp.exp(v2 - v1)
        w1 = p_g / (1.0 + e2)
        w2 = p_g * e2 / (1.0 + e2)
        tm = gtop.shape[1]
        row = lax.broadcasted_iota(jnp.int32, (SUBLANES, tm), 0)
        rt = jnp.where(row == PG, gtop, 0.0)
        for j in range(PG):
            wj = jnp.where(hit1[j], w1, jnp.where(hit2[j], w2, 0.0))
            rt = jnp.where(row == j, wj, rt)
        rt_scr[...] = jnp.concatenate([rt, jnp.zeros((LANES - SUBLANES, tm), F32)], axis=0).T

    hb = hb_scr[...]
    rt = rt_scr[...]
    in_group = rt[:, PG:PG + 1] == c.astype(F32)
    hid = _silu(_dot(hb, wg_ref[0])) * _dot(hb, wu_ref[0])
    parts = [hid[:, j * FH:(j + 1) * FH] * jnp.where(in_group, rt[:, j:j + 1], 0.0) for j in range(PG)]
    contrib = _dot(jnp.concatenate(parts, axis=1).astype(BF16), wd_ref[0])

    @pl.when(c == 0)
    def _():
        acc_scr[...] = contrib

    @pl.when(c > 0)
    def _():
        acc_scr[...] += contrib

    @pl.when(c == NG - 1)
    def _():
        y = x_ref[0] + gate_ref[0] * acc_scr[...]
        if final:
            y = y * lax.rsqrt(jnp.mean(y * y, axis=-1, keepdims=True) + EPS) * fg_ref[...]
        o_ref[0] = y


def _moe(x, g, shift, scale, gate, wg, bg, we, be, w_gate, w_up, w_down, final_g, final, tm=512):
    B, S, D = x.shape
    NG, PG, FH = MOE_GROUPS, MOE_PER_GROUP, MOE_HIDDEN
    wr = jnp.zeros((D, LANES), F32)
    wr = wr.at[:, :NG * PG].set(we.reshape(D, NG * PG).astype(F32)).at[:, NG * PG:NG * PG + NG].set(wg.astype(F32))
    br = jnp.zeros((1, LANES), F32)
    br = br.at[0, :NG * PG].set(be.reshape(NG * PG).astype(F32)).at[0, NG * PG:NG * PG + NG].set(bg.astype(F32))
    wr_hi = wr.astype(BF16)
    wr = jnp.stack([wr_hi, (wr - wr_hi.astype(F32)).astype(BF16)])
    grp = lambda w: w.reshape(NG, PG, D, FH).transpose(0, 2, 1, 3).reshape(NG, D, PG * FH).astype(BF16)
    wd = w_down.reshape(NG, PG * FH, D).astype(BF16)
    vec = pl.BlockSpec((1, 1, D), lambda b, i, c: (b, 0, 0))
    row = pl.BlockSpec((1, D), lambda b, i, c: (0, 0))
    wspec = lambda k, n: pl.BlockSpec((1, k, n), lambda b, i, c: (c, 0, 0))
    return pl.pallas_call(
        functools.partial(_moe_kernel, final=final),
        out_shape=jax.ShapeDtypeStruct((B, S, D), F32),
        grid=(B, S // tm, NG),
        in_specs=[pl.BlockSpec((1, tm, D), lambda b, i, c: (b, i, 0)), row, vec, vec, vec,
                  pl.BlockSpec((2, D, LANES), lambda b, i, c: (0, 0, 0)),
                  pl.BlockSpec((1, LANES), lambda b, i, c: (0, 0)),
                  wspec(D, PG * FH), wspec(D, PG * FH), wspec(PG * FH, D), row],
        out_specs=pl.BlockSpec((1, tm, D), lambda b, i, c: (b, i, 0)),
        scratch_shapes=[pltpu.VMEM((tm, D), BF16), pltpu.VMEM((tm, LANES), F32), pltpu.VMEM((tm, D), F32)],
        compiler_params=_cparams(("parallel", "parallel", "arbitrary")),
        name="moe",
    )(x, g.reshape(1, D), shift, scale, gate, wr, br, grp(w_gate), grp(w_up), wd, final_g.reshape(1, D))


def _mlstm_s5_layer(x, g, shift, scale, gate, w_in, conv_w, b_i, b_f, head_g, s5_params, w_out):
    H = MLSTM_HEADS
    A = MIX_A
    w_if = jnp.zeros((D_MODEL, LANES), F32).at[:, :2 * H].set(w_in[:, 4 * A:4 * A + 2 * H])
    weights = [w_in[:, :2 * A], w_in[:, 2 * A:4 * A], w_if, w_in[:, 4 * A + 2 * H:]]
    qk, vo, ifg, u = _norm_matmul(x, g, shift, scale, [w.astype(BF16) for w in weights], [BF16, BF16, F32, F32])
    gate_bias = jnp.zeros((1, LANES), F32).at[0, :H].set(b_i.astype(F32)).at[0, H:2 * H].set(b_f.astype(F32))
    hm = _mlstm(qk, vo, ifg, conv_w.astype(F32), gate_bias, head_g.reshape(1, A).astype(F32))
    ys = _s5s(u, _s5s_tables(*s5_params))
    w_out = w_out.astype(BF16)
    return _out_residual(x, gate, [hm, ys], [w_out[:A], w_out[A:]])


def _nsa_layer(x, g, shift, scale, gate, w_in, b_gate, cmp_pos, cmp_w1, cmp_b1, cmp_w2, cmp_b2, rel_bias, w_out):
    B, S, D = x.shape
    KV, R, DH = NSA_KV, NSA_R, NSA_DH
    w_g = jnp.zeros((D, KV, LANES), F32).at[:, :, :3 * R].set(w_in[:, D + 6 * KV_W:].reshape(D, KV, 3 * R))
    b_g = jnp.zeros((KV, LANES), F32).at[:, :3 * R].set(b_gate.reshape(KV, 3 * R).astype(F32))
    kv_cols = lambda i: w_in[:, D + i * KV_W:D + (i + 1) * KV_W]
    w_k = jnp.concatenate([kv_cols(0), kv_cols(2), kv_cols(4)], axis=1)
    w_v = jnp.concatenate([kv_cols(1), kv_cols(3), kv_cols(5)], axis=1)
    weights = [w_in[:, :D], w_k, w_v, w_g.reshape(D, KV * LANES)]
    q4, gv, kc, vc, ks, kw, vs_t, vw_t = _nsa_proj(x, g, shift, scale, [w.astype(BF16) for w in weights],
                                                   b_g.reshape(1, KV * LANES))
    grp = CMP_STRIDE
    xg = jnp.stack([kc, vc]).reshape(2, B, KV * S // grp, grp * DH)
    cmp = _compress(xg, cmp_pos, cmp_w1, cmp_b1, cmp_w2, cmp_b2).reshape(2, B, KV, S // grp, DH).astype(BF16)
    out = _nsa_t_attention(q4, gv, cmp[0], cmp[1].transpose(0, 1, 3, 2), ks, vs_t, kw, vw_t,
                           _nsa_t_tables(rel_bias, S))
    return _out_residual(x, gate, [out], [w_out.astype(BF16)])


def kernel(x, c, rel_bias, ada_w, ada_b, norm_g, final_g,
           a_w_in, a_conv, a_b_i, a_b_f, a_head_g,
           s5_lam_re, s5_lam_im, s5_log_dt, s5_b_re, s5_b_im, s5_c_re, s5_c_im,
           s5_d, s5_glu_w, s5_glu_b, a_w_out,
           n_w_in, n_b_gate, n_cmp_pos, n_cmp_w1, n_cmp_b1, n_cmp_w2, n_cmp_b2, n_w_out,
           r_grp_w, r_grp_b, r_exp_w, r_exp_b, e_w_gate, e_w_up, e_w_down):
    B, S, D = x.shape
    mod = _ada_mod(c, ada_w, ada_b).reshape(DEPTH, 2, B, 1, 3 * D)
    split = lambda m: (m[..., :D], m[..., D:2 * D], m[..., 2 * D:])
    for layer in range(DEPTH):
        shift, scale, gate = split(mod[layer, 0])
        j = layer // 2
        if layer % 2 == 0:
            s5_params = (s5_lam_re[j], s5_lam_im[j], s5_log_dt[j], s5_b_re[j], s5_b_im[j],
                         s5_c_re[j], s5_c_im[j], s5_d[j], s5_glu_w[j], s5_glu_b[j])
            x = _mlstm_s5_layer(x, norm_g[layer, 0], shift, scale, gate, a_w_in[j], a_conv[j], a_b_i[j], a_b_f[j],
                                a_head_g[j], s5_params, a_w_out[j])
        else:
            x = _nsa_layer(x, norm_g[layer, 0], shift, scale, gate, n_w_in[j], n_b_gate[j], n_cmp_pos[j],
                           n_cmp_w1[j], n_cmp_b1[j], n_cmp_w2[j], n_cmp_b2[j], rel_bias, n_w_out[j])
        shift, scale, gate = split(mod[layer, 1])
        x = _moe(x, norm_g[layer, 1], shift, scale, gate, r_grp_w[layer], r_grp_b[layer], r_exp_w[layer],
                 r_exp_b[layer], e_w_gate[layer], e_w_up[layer], e_w_down[layer], final_g,
                 final=(layer == DEPTH - 1))
    return x
```

```python
import functools
import math

import jax
import jax.numpy as jnp
from jax import lax
from jax.experimental import pallas as pl
from jax.experimental.pallas import tpu as pltpu

F32 = jnp.float32
BF16 = jnp.bfloat16
HIGHEST = lax.Precision.HIGHEST

D_MODEL = 1024
DEPTH = 2
MIX_A = 512
MLSTM_HEADS = 4
MLSTM_DH = MIX_A // MLSTM_HEADS
MLSTM_CHUNK = 128
CONV_K = 4
S5_GROUP = 16
S5_STATE = 64
S5_CHUNK = 16
NSA_HEADS = 16
NSA_KV = 4
NSA_R = NSA_HEADS // NSA_KV
NSA_DH = D_MODEL // NSA_HEADS
KV_W = NSA_KV * NSA_DH
CMP_BLOCK = 32
CMP_STRIDE = 16
CMP_HIDDEN = 256
SEL_BLOCK = 64
SEL_TOPK = 16
WINDOW = 512
FORCE = 1e9
REL_BUCKETS = 32
REL_MAX_DIST = 128
MOE_GROUPS = 4
MOE_PER_GROUP = 4
MOE_HIDDEN = 256
EPS = 1e-6
NEG = -1e30
BIG = 1e30
LOG2E = math.log2(math.e)
SEL_CHUNK = 1

LANES = 128
SUBLANES = 8
ATT_TILE = 256
VMEM_LIMIT = 56 * 1024 * 1024


def _cparams(sem):
    return pltpu.CompilerParams(dimension_semantics=sem, vmem_limit_bytes=VMEM_LIMIT)


def _dot(a, b, precision=None):
    return jnp.dot(a, b, preferred_element_type=F32, precision=precision)


def _dot_nt(a, b):
    return lax.dot_general(a, b, (((1,), (1,)), ((), ())), preferred_element_type=F32)


def _sigmoid(x):
    return 1.0 / (1.0 + jnp.exp(-x))


def _silu(x):
    return x * _sigmoid(x)


def _gelu_tanh(x):
    return 0.5 * x * (1.0 + jnp.tanh(math.sqrt(2.0 / math.pi) * (x + 0.044715 * (x * x * x))))


def _modulated_norm(x, g, shift, scale):
    y = x * lax.rsqrt(jnp.mean(x * x, axis=-1, keepdims=True) + EPS) * g
    return y * (1.0 + scale) + shift


def _ada_kernel(c_ref, w_ref, b_ref, o_ref):
    c = c_ref[...]
    o_ref[0] = _dot(_silu(c), w_ref[0]) + b_ref[0]


def _ada_mod(c, ada_w, ada_b):
    B, D = c.shape
    n_mod = ada_w.shape[0] * ada_w.shape[1]
    w = ada_w.reshape(n_mod, D, 3 * D)
    b = ada_b.reshape(n_mod, 1, 3 * D)
    tn = 1024
    return pl.pallas_call(
        _ada_kernel,
        out_shape=jax.ShapeDtypeStruct((n_mod, B, 3 * D), F32),
        grid=(n_mod, 3 * D // tn),
        in_specs=[pl.BlockSpec((B, D), lambda i, j: (0, 0)),
                  pl.BlockSpec((1, D, tn), lambda i, j: (i, 0, j)),
                  pl.BlockSpec((1, 1, tn), lambda i, j: (i, 0, j))],
        out_specs=pl.BlockSpec((1, B, tn), lambda i, j: (i, 0, j)),
        compiler_params=_cparams(("parallel", "parallel")),
        name="ada_mod",
    )(c, w, b)


def _norm_mm_kernel(*refs, n_w):
    x_ref, g_ref, sh_ref, sc_ref = refs[:4]
    w_refs = refs[4:4 + n_w]
    o_refs = refs[4 + n_w:]
    h = _modulated_norm(x_ref[0], g_ref[...], sh_ref[0], sc_ref[0]).astype(BF16)
    for w_ref, o_ref in zip(w_refs, o_refs):
        o_ref[0] = _dot(h, w_ref[...]).astype(o_ref.dtype)


def _norm_matmul(x, g, shift, scale, weights, out_dtypes, tm=512):
    B, S, D = x.shape
    n_w = len(weights)
    vec = pl.BlockSpec((1, 1, D), lambda b, i: (b, 0, 0))
    in_specs = [pl.BlockSpec((1, tm, D), lambda b, i: (b, i, 0)),
                pl.BlockSpec((1, D), lambda b, i: (0, 0)), vec, vec]
    in_specs += [pl.BlockSpec(w.shape, lambda b, i: (0, 0)) for w in weights]
    return pl.pallas_call(
        functools.partial(_norm_mm_kernel, n_w=n_w),
        out_shape=[jax.ShapeDtypeStruct((B, S, w.shape[1]), dt) for w, dt in zip(weights, out_dtypes)],
        grid=(B, S // tm),
        in_specs=in_specs,
        out_specs=[pl.BlockSpec((1, tm, w.shape[1]), lambda b, i: (b, i, 0)) for w in weights],
        compiler_params=_cparams(("parallel", "parallel")),
        name="norm_matmul",
    )(x, g.reshape(1, D), shift, scale, *weights)


def _out_res_kernel(*refs, n_in):
    x_ref, gate_ref = refs[:2]
    a_refs = refs[2:2 + n_in]
    w_refs = refs[2 + n_in:2 + 2 * n_in]
    o_ref = refs[2 + 2 * n_in]
    acc = None
    for a_ref, w_ref in zip(a_refs, w_refs):
        t = _dot(a_ref[0].astype(BF16), w_ref[...])
        acc = t if acc is None else acc + t
    o_ref[0] = x_ref[0] + gate_ref[0] * acc


def _out_residual(x, gate, acts, weights, tm=512):
    B, S, D = x.shape
    n_in = len(acts)
    in_specs = [pl.BlockSpec((1, tm, D), lambda b, i: (b, i, 0)),
                pl.BlockSpec((1, 1, D), lambda b, i: (b, 0, 0))]
    in_specs += [pl.BlockSpec((1, tm, a.shape[2]), lambda b, i: (b, i, 0)) for a in acts]
    in_specs += [pl.BlockSpec(w.shape, lambda b, i: (0, 0)) for w in weights]
    return pl.pallas_call(
        functools.partial(_out_res_kernel, n_in=n_in),
        out_shape=jax.ShapeDtypeStruct((B, S, D), F32),
        grid=(B, S // tm),
        in_specs=in_specs,
        out_specs=pl.BlockSpec((1, tm, D), lambda b, i: (b, i, 0)),
        compiler_params=_cparams(("parallel", "parallel")),
        name="out_residual",
    )(x, gate, *acts, *weights)


def _mlstm_kernel(qk_ref, vo_ref, if_ref, cw_ref, gb_ref, hg_ref, tril_ref, o_ref,
                  xbuf, c_scr, n_scr, m_scr):
    L, H, DH = MLSTM_CHUNK, MLSTM_HEADS, MLSTM_DH
    pad = SUBLANES

    @pl.when(pl.program_id(1) == 0)
    def _():
        xbuf[0:pad, :] = jnp.zeros((pad, 2 * MIX_A), F32)
        c_scr[...] = jnp.zeros_like(c_scr)
        n_scr[...] = jnp.zeros_like(n_scr)
        m_scr[...] = jnp.zeros_like(m_scr)

    xbuf[pad:pad + L, :] = qk_ref[0].astype(F32)
    cw = cw_ref[...]
    conv = None
    for j in range(CONV_K):
        lo = pad - (CONV_K - 1) + j
        t = xbuf[lo:lo + L, :] * cw[j:j + 1, :]
        conv = t if conv is None else conv + t
    xbuf[0:pad, :] = xbuf[L:L + pad, :]
    qk = _silu(conv)
    q = qk[:, :MIX_A]
    k = qk[:, MIX_A:] * (DH ** -0.5)
    vo = vo_ref[0].astype(F32)
    v = vo[:, :MIX_A]
    o_pre = vo[:, MIX_A:]

    ifb = if_ref[0] + gb_ref[...]
    lf = jnp.minimum(ifb, 0.0) - jnp.log1p(jnp.exp(-jnp.abs(ifb)))
    bcs = _dot(tril_ref[...], lf, precision=HIGHEST)
    ifb_t = ifb.T
    bcs_t = bcs.T
    row = lax.broadcasted_iota(jnp.int32, (L, L), 0)
    col = lax.broadcasted_iota(jnp.int32, (L, L), 1)
    causal = col <= row

    outs = []
    for h in range(H):
        sl = slice(h * DH, (h + 1) * DH)
        qh, kh, vh = q[:, sl], k[:, sl], v[:, sl]
        qb, kb = qh.astype(BF16), kh.astype(BF16)
        b_col = bcs[:, H + h:H + h + 1]
        b_row = bcs_t[H + h:H + h + 1, :]
        li_col = ifb[:, h:h + 1]
        li_row = ifb_t[h:h + 1, :]
        b_last = b_col[L - 1:L, :]
        m0 = m_scr[h][:, 0:1]
        c0 = c_scr[h]
        n0 = n_scr[h]

        log_d = jnp.where(causal, b_col - b_row + li_row, NEG)
        log_inter = b_col + m0
        m_t = jnp.maximum(log_inter, jnp.max(log_d, axis=1, keepdims=True))
        dmat = jnp.exp(log_d - m_t)
        a_inter = jnp.exp(log_inter - m_t)
        s = _dot_nt(qb, kb) * dmat
        num = _dot(s.astype(BF16), vh.astype(BF16)) + a_inter * _dot_nt(qb, c0.astype(BF16))
        den = jnp.sum(s, axis=1, keepdims=True) + a_inter * jnp.sum(qh * n0, axis=1, keepdims=True)
        hh = num / jnp.maximum(jnp.abs(den), jnp.exp(-m_t))

        w_col = b_last - b_col + li_col
        m_loc = jnp.max(w_col, axis=0, keepdims=True)
        e = jnp.exp(w_col - m_loc)
        c_loc = _dot((vh * e).T.astype(BF16), kb)
        n_loc = jnp.sum(kh * e, axis=0, keepdims=True)
        m_new = jnp.maximum(b_last + m0, m_loc)
        a = jnp.exp(b_last + m0 - m_new)
        sc = jnp.exp(m_loc - m_new)
        c_scr[h] = a * c0 + sc * c_loc
        n_scr[h] = a * n0 + sc * n_loc
        m_scr[h] = jnp.broadcast_to(m_new, (1, LANES))

        outs.append(hh * lax.rsqrt(jnp.mean(hh * hh, axis=1, keepdims=True) + EPS))
    hm = jnp.concatenate(outs, axis=1)
    o_ref[0] = (_sigmoid(o_pre) * (hm * hg_ref[...])).astype(o_ref.dtype)


def _mlstm(qk, vo, ifg, conv_w, gate_bias, head_g):
    B, S, _ = qk.shape
    L, H, DH = MLSTM_CHUNK, MLSTM_HEADS, MLSTM_DH
    tril = jnp.tril(jnp.ones((L, L), F32))
    return pl.pallas_call(
        _mlstm_kernel,
        out_shape=jax.ShapeDtypeStruct((B, S, MIX_A), BF16),
        grid=(B, S // L),
        in_specs=[pl.BlockSpec((1, L, 2 * MIX_A), lambda b, c: (b, c, 0)),
                  pl.BlockSpec((1, L, 2 * MIX_A), lambda b, c: (b, c, 0)),
                  pl.BlockSpec((1, L, LANES), lambda b, c: (b, c, 0)),
                  pl.BlockSpec((CONV_K, 2 * MIX_A), lambda b, c: (0, 0)),
                  pl.BlockSpec((1, LANES), lambda b, c: (0, 0)),
                  pl.BlockSpec((1, MIX_A), lambda b, c: (0, 0)),
                  pl.BlockSpec((L, L), lambda b, c: (0, 0))],
        out_specs=pl.BlockSpec((1, L, MIX_A), lambda b, c: (b, c, 0)),
        scratch_shapes=[pltpu.VMEM((L + SUBLANES, 2 * MIX_A), F32),
                        pltpu.VMEM((H, DH, DH), F32),
                        pltpu.VMEM((H, 1, DH), F32),
                        pltpu.VMEM((H, 1, LANES), F32)],
        compiler_params=_cparams(("parallel", "arbitrary")),
        name="mlstm",
    )(qk, vo, ifg, conv_w, gate_bias, head_g, tril)


S5_LT = LANES // S5_GROUP
S5_PAIRS = S5_CHUNK // 2


def _s5s_kernel(u_ref, h_ref, e_ref, kk_ref, are_ref, aim_ref, d_ref, gw_ref, gb_ref, o_ref, xl_scr, x0_scr):
    n_chunks = u_ref.shape[1] // S5_CHUNK
    half = S5_LT * S5_STATE
    tok = lambda s: u_ref[0, pl.ds(s, n_chunks, stride=S5_CHUNK), :]
    u2 = [jnp.concatenate([tok(2 * q), tok(2 * q + 1)], axis=1) for q in range(S5_PAIRS)]
    u2b = [v.astype(BF16) for v in u2]
    xl_scr[...] = functools.reduce(lambda a, b: a + b, [_dot(u2b[q], h_ref[0, q]) for q in range(S5_PAIRS)])
    a_re = are_ref[0]
    a_im = aim_ref[0]

    def body(a, carry):
        re, im = carry
        x0_scr[pl.ds(a, 1), 0:half] = re
        x0_scr[pl.ds(a, 1), half:2 * half] = im
        return (a_re * re - a_im * im + xl_scr[pl.ds(a, 1), 0:half],
                a_re * im + a_im * re + xl_scr[pl.ds(a, 1), half:2 * half])

    zero = jnp.zeros((1, half), F32)
    lax.fori_loop(0, n_chunks, body, (zero, zero), unroll=8)
    x0 = x0_scr[...].astype(BF16)
    for p in range(S5_PAIRS):
        y = _dot(x0, e_ref[0, p]) + u2[p] * d_ref[0]
        for q in range(p + 1):
            y = y + _dot(u2b[q], kk_ref[0, p - q])
        ys = _gelu_tanh(y)
        out = ys * _sigmoid(_dot(ys.astype(BF16), gw_ref[0]) + gb_ref[0])
        o_ref[0, pl.ds(2 * p, n_chunks, stride=S5_CHUNK), :] = out[:, :LANES].astype(o_ref.dtype)
        o_ref[0, pl.ds(2 * p + 1, n_chunks, stride=S5_CHUNK), :] = out[:, LANES:].astype(o_ref.dtype)


def _s5s_tables(lam_re, lam_im, log_dt, b_re, b_im, c_re, c_im, d_skip, glu_w, glu_b):
    T, C, P, LT = S5_CHUNK, S5_GROUP, S5_STATE, S5_LT
    G = lam_re.shape[0]
    NT = G // LT
    lam = lax.complex(lam_re.astype(F32), lam_im.astype(F32))
    dt = jnp.exp(log_dt.astype(F32))[:, None]
    lam_bar = jnp.exp(lam * dt)
    b_bar = ((lam_bar - 1.0) / lam)[..., None] * lax.complex(b_re.astype(F32), b_im.astype(F32))
    c_mat = lax.complex(c_re.astype(F32), c_im.astype(F32))
    taus = jnp.arange(T + 1, dtype=F32)
    pw = jnp.exp((lam * dt)[:, None, :] * taus[None, :, None])
    eye = jnp.eye(LT, dtype=F32)
    tiles = lambda a: a.reshape((NT, LT) + a.shape[1:])

    kern = jnp.einsum('gcp,gtp,gpd->gtdc', c_mat, pw[:, :T], b_bar, precision=HIGHEST).real
    kblk = jnp.einsum('nitdc,ij->ntidjc', tiles(kern), eye).reshape(NT, T, LANES, LANES)
    kblk = jnp.concatenate([jnp.zeros_like(kblk[:, :1]), kblk], axis=1)
    kk = jnp.stack([jnp.concatenate([jnp.concatenate([kblk[:, 2 * d + 1], kblk[:, 2 * d + 2]], axis=2),
                                     jnp.concatenate([kblk[:, 2 * d], kblk[:, 2 * d + 1]], axis=2)], axis=1)
                    for d in range(T // 2)], axis=1)

    hmat = pw[:, :T][:, ::-1, :, None] * b_bar[:, None]

    def state_cols(m):
        return jnp.einsum('nispc,ij->nsicjp', tiles(m), eye).reshape(NT, T, LANES, LT * P)

    h = jnp.concatenate([state_cols(hmat.real), state_cols(hmat.imag)], axis=3)
    h2 = h.reshape(NT, T // 2, 2 * LANES, 2 * LT * P)

    emat = c_mat[:, None] * pw[:, 1:][:, :, None, :]

    def state_rows(m):
        return jnp.einsum('nitcp,ij->ntjpic', tiles(m), eye).reshape(NT, T, LT * P, LANES)

    e = jnp.concatenate([state_rows(emat.real), state_rows(-emat.imag)], axis=2)
    e2 = e.reshape(NT, T // 2, 2, 2 * LT * P, LANES).transpose(0, 1, 3, 2, 4).reshape(NT, T // 2, 2 * LT * P, 2 * LANES)

    a_re = pw[:, T].real.reshape(NT, 1, LT * P)
    a_im = pw[:, T].imag.reshape(NT, 1, LT * P)
    pair = lambda v: jnp.tile(v.astype(F32).reshape(NT, 1, LANES), (1, 1, 2))
    gwb = jnp.einsum('nice,ij->nicje', tiles(glu_w.astype(F32)), eye).reshape(NT, LANES, LANES)
    zeros = jnp.zeros_like(gwb)
    gw2 = jnp.concatenate([jnp.concatenate([gwb, zeros], axis=2), jnp.concatenate([zeros, gwb], axis=2)], axis=1)
    return (h2.astype(BF16), e2.astype(BF16), kk.astype(BF16), a_re, a_im, pair(d_skip), gw2.astype(BF16), pair(glu_b))


def _s5s(u, tables):
    B, S, W = u.shape
    NT = W // LANES
    n_chunks = S // S5_CHUNK
    per_tile = lambda a: pl.BlockSpec((1,) + a.shape[1:], lambda j, b: (j,) + (0,) * (a.ndim - 1))
    return pl.pallas_call(
        _s5s_kernel,
        out_shape=jax.ShapeDtypeStruct((B, S, W), F32),
        grid=(NT, B),
        in_specs=[pl.BlockSpec((1, S, LANES), lambda j, b: (b, 0, j))] + [per_tile(t) for t in tables],
        out_specs=pl.BlockSpec((1, S, LANES), lambda j, b: (b, 0, j)),
        scratch_shapes=[pltpu.VMEM((n_chunks, 2 * S5_LT * S5_STATE), F32) for _ in range(2)],
        compiler_params=_cparams(("parallel", "parallel")),
        name="s5",
    )(u, *tables)


def _compress_kernel(x_ref, plo_ref, phi_ref, w1_ref, b1_ref, w2_ref, b2_ref, o_ref):
    x = x_ref[0, 0]
    half = x.shape[1]
    w1 = w1_ref[0]
    lo = _dot((x + plo_ref[0]).astype(BF16), w1[:half])
    hi = _dot((x + phi_ref[0]).astype(BF16), w1[half:])
    rows = x.shape[0]
    hid = _gelu_tanh(lo + pltpu.roll(hi, rows - 1, 0) + b1_ref[0])
    o_ref[0, 0] = _dot(hid.astype(BF16), w2_ref[0]) + b2_ref[0]


def _compress(xg, pos, w1, b1, w2, b2):
    _, B, rows, width = xg.shape
    pos_flat = pos.reshape(2, 2, 1, width).astype(F32)
    sel = lambda shape: pl.BlockSpec((1,) + shape, lambda j, b: (j, 0, 0))
    return pl.pallas_call(
        _compress_kernel,
        out_shape=jax.ShapeDtypeStruct((2, B, rows, NSA_DH), F32),
        grid=(2, B),
        in_specs=[pl.BlockSpec((1, 1, rows, width), lambda j, b: (j, b, 0, 0)),
                  sel((1, width)), sel((1, width)),
                  sel((2 * width, CMP_HIDDEN)), sel((1, CMP_HIDDEN)),
                  sel((CMP_HIDDEN, NSA_DH)), sel((1, NSA_DH))],
        out_specs=pl.BlockSpec((1, 1, rows, NSA_DH), lambda j, b: (j, b, 0, 0)),
        compiler_params=_cparams(("parallel", "parallel")),
        name="nsa_compress",
    )(xg, pos_flat[:, 0], pos_flat[:, 1], w1.astype(BF16), b1[:, None].astype(F32),
      w2.astype(BF16), b2[:, None].astype(F32))


def _t5_bucket(dist):
    dist = jnp.maximum(dist, 0)
    max_exact = REL_BUCKETS // 2
    log_ratio = jnp.log(jnp.maximum(dist, 1).astype(F32) / max_exact) / math.log(REL_MAX_DIST / max_exact)
    large = jnp.minimum(max_exact + (log_ratio * (REL_BUCKETS - max_exact)).astype(jnp.int32), REL_BUCKETS - 1)
    return jnp.where(dist < max_exact, dist, large)


def _nsa_proj_kernel(x_ref, g_ref, sh_ref, sc_ref, wq_ref, wk_ref, wv_ref, wg_ref, bg_ref,
                     q4_ref, gv_ref, kc_ref, vc_ref, ks_ref, kw_ref, vst_ref, vwt_ref):
    KV, R, DH, T = NSA_KV, NSA_R, NSA_DH, ATT_TILE
    h = _modulated_norm(x_ref[0], g_ref[...], sh_ref[0], sc_ref[0]).astype(BF16)
    q_t = (_dot(h, wq_ref[...]) * (DH ** -0.5 * LOG2E)).T.astype(BF16)
    gates_t = _sigmoid(_dot(h, wg_ref[...]) + bg_ref[...]).T
    row = lax.broadcasted_iota(jnp.int32, (SUBLANES, R * T), 0)
    for g in range(KV):
        q4_ref[0, g, 0] = jnp.concatenate([q_t[(g * R + r) * DH:(g * R + r + 1) * DH] for r in range(R)], axis=1)
        gv = jnp.zeros((SUBLANES, R * T), F32)
        for j in range(3):
            gj = jnp.concatenate([gates_t[g * LANES + 3 * r + j:g * LANES + 3 * r + j + 1] for r in range(R)], axis=1)
            gv = jnp.where(row == j, gj, gv)
        gv_ref[0, g, 0] = gv
    k3 = _dot(h, wk_ref[...])
    v3 = _dot(h, wv_ref[...])
    vs_t = v3[:, KV_W:2 * KV_W].T.astype(BF16)
    vw_t = v3[:, 2 * KV_W:].T.astype(BF16)
    for g in range(KV):
        cols = slice(g * DH, (g + 1) * DH)
        kc_ref[0, g] = k3[:, cols].astype(BF16)
        vc_ref[0, g] = v3[:, cols].astype(BF16)
        ks_ref[0, g] = k3[:, KV_W + g * DH:KV_W + (g + 1) * DH].astype(BF16)
        kw_ref[0, g] = k3[:, 2 * KV_W + g * DH:2 * KV_W + (g + 1) * DH].astype(BF16)
        vst_ref[0, g, 0] = vs_t[cols]
        vwt_ref[0, g, 0] = vw_t[cols]


def _nsa_proj(x, g, shift, scale, weights, b_gate):
    B, S, D = x.shape
    KV, R, DH, T = NSA_KV, NSA_R, NSA_DH, ATT_TILE
    vec = pl.BlockSpec((1, 1, D), lambda b, i: (b, 0, 0))
    keys = pl.BlockSpec((1, KV, T, DH), lambda b, i: (b, 0, i, 0))
    key_shape = jax.ShapeDtypeStruct((B, KV, S, DH), BF16)
    tile = lambda rows, width: pl.BlockSpec((1, KV, 1, rows, width), lambda b, i: (b, 0, i, 0, 0))
    tile_shape = lambda rows, width, dt: jax.ShapeDtypeStruct((B, KV, S // T, rows, width), dt)
    return pl.pallas_call(
        _nsa_proj_kernel,
        out_shape=[tile_shape(DH, R * T, BF16), tile_shape(SUBLANES, R * T, F32),
                   key_shape, key_shape, key_shape, key_shape,
                   tile_shape(DH, T, BF16), tile_shape(DH, T, BF16)],
        grid=(B, S // T),
        in_specs=[pl.BlockSpec((1, T, D), lambda b, i: (b, i, 0)),
                  pl.BlockSpec((1, D), lambda b, i: (0, 0)), vec, vec]
                 + [pl.BlockSpec(w.shape, lambda b, i: (0, 0)) for w in weights]
                 + [pl.BlockSpec(b_gate.shape, lambda b, i: (0, 0))],
        out_specs=[tile(DH, R * T), tile(SUBLANES, R * T), keys, keys, keys, keys, tile(DH, T), tile(DH, T)],
        compiler_params=_cparams(("parallel", "parallel")),
        name="nsa_proj",
    )(x, g.reshape(1, D), shift, scale, *weights, b_gate)


def _nsa_t_kernel(q4_ref, gv_ref, kc_ref, vct_ref, ks_ref, vst_ref, kw_ref, vwt_ref,
                  cfar_ref, band_ref, selb_ref, winb_ref, ovt_ref, o_ref, s_scr, sel_scr, sbuf):
    T = ATT_TILE
    R, DH = NSA_R, NSA_DH
    qi = pl.program_id(2)
    q0 = qi * T
    n_pad = kc_ref.shape[2]
    n_sel = ovt_ref.shape[0]
    CH = SEL_CHUNK
    n_far = selb_ref.shape[0] - 1
    n_win = winb_ref.shape[0] - 2
    band_rows = band_ref.shape[2] - T // CMP_STRIDE * 2

    q4 = q4_ref[0, 0, 0]
    t_lane = q0 + lax.broadcasted_iota(jnp.int32, (1, R * T), 1) % T

    ones_rows = 2 * SUBLANES
    with_ones = lambda v_t: jnp.concatenate([v_t, jnp.ones((ones_rows, v_t.shape[1]), v_t.dtype)], axis=0)
    gvec = lambda j: gv_ref[0, 0, 0, j:j + 1, :]

    grp = T // CMP_STRIDE
    s_scr[0:n_pad, :] = _dot(kc_ref[0, 0], q4) + cfar_ref[0]
    s_scr[n_pad:n_pad + 2 * grp, :] = jnp.zeros((2 * grp, R * T), F32)
    r0 = jnp.maximum(qi * grp - 2 * grp, 0)
    x0 = r0 - (qi * grp - 2 * grp)
    r0 = pl.multiple_of(r0, SUBLANES)
    x0 = pl.multiple_of(x0, SUBLANES)
    s_scr[pl.ds(r0, band_rows), :] += band_ref[0, 0, pl.ds(x0, band_rows), :]
    lim = pl.multiple_of(qi * grp + 2 * grp, SUBLANES)
    s_scr[pl.ds(lim, n_pad), :] = jnp.full((n_pad, R * T), NEG, F32)

    w_subs, w_vals = [], []
    for d in range(n_win + 1):
        kt = jnp.maximum(qi - d, 0)
        off = pl.multiple_of(kt * T, T)
        tile = jnp.where(qi >= d, d, n_win + 1)
        w_subs.append((_dot(kw_ref[0, 0, pl.ds(off, T), :], q4) + winb_ref[tile, 0]).astype(BF16))
        w_vals.append(with_ones(vwt_ref[0, 0, kt]))

    s = s_scr[0:n_pad, :]
    e = jnp.exp2(s - jnp.max(s, axis=0, keepdims=True))
    inv = jnp.where(t_lane >= CMP_BLOCK - 1, 1.0 / jnp.sum(e, axis=0, keepdims=True), 0.0)
    p = e * inv
    o_cmp = _dot(vct_ref[0, 0], p.astype(BF16))
    psum = functools.reduce(lambda a, b: a + b, [p[:, r * T:(r + 1) * T] for r in range(R)])

    m_w = jnp.max(functools.reduce(jnp.maximum, w_subs), axis=0, keepdims=True)
    acc = functools.reduce(lambda a, b: a + b,
                           [_dot(vj, jnp.exp2(sj - m_w)) for sj, vj in zip(w_subs, w_vals)])
    o_win = acc[:DH] * (1.0 / acc[DH:DH + 1])
    out_t = gvec(0) * o_cmp + gvec(2) * o_win

    imp_t = _dot(ovt_ref[...], psum, precision=HIGHEST)
    jj = lax.broadcasted_iota(jnp.int32, (n_sel, T), 0)
    blk_t = (q0 + lax.broadcasted_iota(jnp.int32, (1, T), 1)) // SEL_BLOCK
    forced = (jj == 0) | (jj == blk_t) | (jj == blk_t - 1)
    score = jnp.where(forced, FORCE, jnp.where(jj <= blk_t, imp_t, -1.0))
    n_blk = n_sel // SUBLANES
    rows = [score[v * SUBLANES:(v + 1) * SUBLANES] for v in range(n_blk)]
    cnts = [jnp.zeros((SUBLANES, T), F32) for _ in range(n_blk)]
    sub = lax.broadcasted_iota(jnp.int32, (SUBLANES, T), 0)
    for j2 in range(n_sel):
        c2 = score[j2:j2 + 1, :]
        for v in range(n_blk):
            lo = v * SUBLANES
            if lo > j2:
                beats = c2 >= rows[v]
            elif lo + SUBLANES - 1 <= j2:
                beats = c2 > rows[v]
            else:
                beats = (c2 > rows[v]) | ((c2 >= rows[v]) & (sub > j2 - lo))
            cnts[v] = cnts[v] + jnp.where(beats, 1.0, 0.0)
    cnt = jnp.concatenate(cnts, axis=0)
    chosen = (cnt < float(min(SEL_TOPK, n_sel))) & (jj <= blk_t)
    sel_scr[...] = jnp.where(chosen, 0.0, -BIG)

    def block_mask(kt):
        per_tile = T // SEL_BLOCK
        parts = [jnp.broadcast_to(sel_scr[pl.ds(kt * per_tile + i, 1), :], (SEL_BLOCK, T)) for i in range(per_tile)]
        m1 = jnp.concatenate(parts, axis=0)
        return jnp.concatenate([m1] * R, axis=1)

    def sel_scores(slot, kc):
        off = pl.multiple_of(kc * (CH * T), CH * T)
        s = _dot(ks_ref[0, 0, pl.ds(off, CH * T), :], q4)
        subs = []
        for j in range(CH):
            kt = kc * CH + j
            d = jnp.clip(qi - kt, 0, n_far)
            subs.append(s[j * T:(j + 1) * T] + selb_ref[d, 0] + block_mask(kt))
        s = jnp.concatenate(subs, axis=0).astype(BF16)
        sbuf[slot] = s
        return jnp.max(s, axis=0, keepdims=True).astype(F32)

    def sel_weighted(slot, kc, m_new):
        v_t = jnp.concatenate([vst_ref[0, 0, kc * CH + j] for j in range(CH)], axis=1)
        return _dot(with_ones(v_t), jnp.exp2(sbuf[slot] - m_new.astype(BF16)))

    last_chunk = vst_ref.shape[2] // CH - 1

    def sel_body(i, carry):
        m, acc, m_even = carry
        m_odd = sel_scores(1, 2 * i + 1)
        m_new = jnp.maximum(m, m_even)
        acc = jnp.exp2(m - m_new) * acc + sel_weighted(0, 2 * i, m_new)
        m_even = sel_scores(0, jnp.minimum(2 * i + 2, last_chunk))
        m_fin = jnp.maximum(m_new, m_odd)
        acc = jnp.exp2(m_new - m_fin) * acc + sel_weighted(1, 2 * i + 1, m_fin)
        return m_fin, acc, m_even

    n_chunks = qi // CH + 1
    _, acc, _ = lax.fori_loop(0, (n_chunks + 1) // 2, sel_body,
                              (jnp.full((1, R * T), NEG, F32), jnp.zeros((DH + ones_rows, R * T), F32),
                               sel_scores(0, 0)))
    out_t = out_t + gvec(1) * (acc[:DH] * (1.0 / acc[DH:DH + 1]))
    for pr in range(R // 2):
        pair = jnp.concatenate([out_t[:, (2 * pr) * T:(2 * pr + 1) * T],
                                out_t[:, (2 * pr + 1) * T:(2 * pr + 2) * T]], axis=0)
        o_ref[0, :, pr * 2 * DH:(pr + 1) * 2 * DH] = pair.T.astype(o_ref.dtype)


def _bias_lookup(table, dist):
    idx = _t5_bucket(dist)
    out = jnp.zeros(idx.shape + (table.shape[1],), F32)
    for k in range(table.shape[0]):
        out = out + jnp.where((idx == k)[..., None], table[k], 0.0)
    return out


def _nsa_t_tables(rel_bias, S):
    T, R, KV = ATT_TILE, NSA_R, NSA_KV
    table = rel_bias.astype(F32) * LOG2E
    ii = jnp.arange(T)
    delta = ii[None, :] - ii[:, None]

    def lanes(a):
        a = jnp.moveaxis(a, -1, 0)
        a = a.reshape((KV, R) + a.shape[1:])
        return jnp.moveaxis(a, 1, 2).reshape(KV, a.shape[2], R * a.shape[3])

    def tile(off):
        return lanes(_bias_lookup(table, off * T + delta))

    mask4 = lambda ok: jnp.tile(jnp.where(ok, 0.0, NEG), (1, R))[None]
    n_far = -(-REL_MAX_DIST // T) + 1
    selb = [tile(o) for o in range(n_far + 1)]
    selb[0] = selb[0] + mask4(delta >= 0)
    selb = jnp.stack(selb, axis=0)
    n_win = WINDOW // T
    winb = [tile(o) + mask4((o * T + delta >= 0) & (o * T + delta < WINDOW)) for o in range(n_win + 1)]
    winb.append(jnp.full_like(winb[0], NEG))
    winb = jnp.stack(winb, axis=0)

    grp = T // CMP_STRIDE
    far = _bias_lookup(table, jnp.asarray(2 * REL_MAX_DIST))
    xx = jnp.arange(4 * grp)
    bdist = ii[None, :] - CMP_STRIDE * (xx[:, None] - 2 * grp) - (CMP_BLOCK - 1)
    band = jnp.where((bdist >= 0)[..., None], _bias_lookup(table, bdist) - far, NEG)
    band = jnp.concatenate([lanes(band), jnp.zeros((KV, 2 * grp, R * T), F32)], axis=1)[:, None]
    cfar = jnp.repeat(far.reshape(KV, R), T, axis=1)[:, None]

    n_pad = S // CMP_STRIDE
    n_sel = S // SEL_BLOCK
    cmp_start = jnp.arange(n_pad) * CMP_STRIDE
    sel_start = jnp.arange(n_sel) * SEL_BLOCK
    overlap = jnp.clip(jnp.minimum(cmp_start[:, None] + CMP_BLOCK, sel_start[None] + SEL_BLOCK)
                       - jnp.maximum(cmp_start[:, None], sel_start[None]), 0).astype(F32) / CMP_BLOCK
    n_cmp = (S - CMP_BLOCK) // CMP_STRIDE + 1
    overlap_t = jnp.where((jnp.arange(n_pad) < n_cmp)[:, None], overlap, 0.0).T
    return cfar, band, selb, winb, overlap_t


def _nsa_t_attention(q4, gv, kcmp, vcmp_t, ks, vs_t, kw, vw_t, tables):
    B, KV, S, DH = ks.shape
    T = ATT_TILE
    cfar, band, selb, winb, overlap_t = tables
    gw = NSA_R * DH
    n_pad = kcmp.shape[2]
    seq = lambda a: pl.BlockSpec((1, 1) + a.shape[2:], lambda b, g, i: (b, g) + (0,) * (a.ndim - 2))
    qtile = lambda a: pl.BlockSpec((1, 1, 1) + a.shape[3:], lambda b, g, i: (b, g, i, 0, 0))
    grp = lambda a: pl.BlockSpec((1,) + a.shape[1:], lambda b, g, i: (g,) + (0,) * (a.ndim - 1))
    tiles = lambda a: pl.BlockSpec((a.shape[0], 1) + a.shape[2:], lambda b, g, i: (0, g, 0, 0))
    full = lambda a: pl.BlockSpec(a.shape, lambda b, g, i: (0,) * a.ndim)
    return pl.pallas_call(
        _nsa_t_kernel,
        out_shape=jax.ShapeDtypeStruct((B, S, KV * gw), BF16),
        grid=(B, KV, S // T),
        in_specs=[qtile(q4), qtile(gv),
                  seq(kcmp), seq(vcmp_t), seq(ks), seq(vs_t), seq(kw), seq(vw_t),
                  grp(cfar), grp(band), tiles(selb), tiles(winb), full(overlap_t)],
        out_specs=pl.BlockSpec((1, T, gw), lambda b, g, i: (b, i, g)),
        scratch_shapes=[pltpu.VMEM((2 * n_pad + 2 * (T // CMP_STRIDE), NSA_R * T), F32),
                        pltpu.VMEM((S // SEL_BLOCK, T), F32),
                        pltpu.VMEM((2, SEL_CHUNK * T, NSA_R * T), BF16)],
        compiler_params=_cparams(("parallel", "parallel", "arbitrary")),
        name="nsa_attention",
    )(q4, gv, kcmp, vcmp_t, ks, vs_t, kw, vw_t, cfar, band, selb, winb, overlap_t)


def _moe_kernel(x_ref, g_ref, sh_ref, sc_ref, gate_ref, wr_ref, br_ref, wg_ref, wu_ref, wd_ref, fg_ref,
                o_ref, hb_scr, rt_scr, acc_scr, *, final):
    NG, PG, FH = MOE_GROUPS, MOE_PER_GROUP, MOE_HIDDEN
    c = pl.program_id(2)

    @pl.when(c == 0)
    def _():
        h = _modulated_norm(x_ref[0], g_ref[...], sh_ref[0], sc_ref[0])
        h_hi = h.astype(BF16)
        hb_scr[...] = h_hi
        h_lo = (h - h_hi.astype(F32)).astype(BF16)
        logits = (_dot(h_hi, wr_ref[0]) + _dot(h_lo, wr_ref[0]) + _dot(h_hi, wr_ref[1]) + br_ref[...]).T
        gl = [logits[NG * PG + g:NG * PG + g + 1, :] for g in range(NG)]
        gmax = functools.reduce(jnp.maximum, gl)
        gtop = jnp.full_like(gmax, float(NG - 1))
        for g in reversed(range(NG - 1)):
            gtop = jnp.where(gl[g] == gmax, float(g), gtop)
        p_g = 1.0 / functools.reduce(lambda a, b: a + b, [jnp.exp(v - gmax) for v in gl])
        a = []
        for j in range(PG):
            v = logits[(NG - 1) * PG + j:(NG - 1) * PG + j + 1, :]
            for g in reversed(range(NG - 1)):
                v = jnp.where(gtop == float(g), logits[g * PG + j:g * PG + j + 1, :], v)
            a.append(v)

        def first_max(vals):
            vmax = functools.reduce(jnp.maximum, vals)
            taken = jnp.zeros_like(vmax) > 1.0
            hits = []
            for v in vals:
                hit = (v == vmax) & jnp.logical_not(taken)
                taken = taken | hit
                hits.append(hit)
            return vmax, hits

        v1, hit1 = first_max(a)
        rest = [jnp.where(hh, -jnp.inf, v) for hh, v in zip(hit1, a)]
        v2, hit2 = first_max(rest)
        e2 = jnp.exp(v2 - v1)
        w1 = p_g / (1.0 + e2)
        w2 = p_g * e2 / (1.0 + e2)
        tm = gtop.shape[1]
        row = lax.broadcasted_iota(jnp.int32, (SUBLANES, tm), 0)
        rt = jnp.where(row == PG, gtop, 0.0)
        for j in range(PG):
            wj = jnp.where(hit1[j], w1, jnp.where(hit2[j], w2, 0.0))
            rt = jnp.where(row == j, wj, rt)
        rt_scr[...] = jnp.concatenate([rt, jnp.zeros((LANES - SUBLANES, tm), F32)], axis=0).T

    hb = hb_scr[...]
    rt = rt_scr[...]
    in_group = rt[:, PG:PG + 1] == c.astype(F32)
    hid = _silu(_dot(hb, wg_ref[0])) * _dot(hb, wu_ref[0])
    parts = [hid[:, j * FH:(j + 1) * FH] * jnp.where(in_group, rt[:, j:j + 1], 0.0) for j in range(PG)]
    contrib = _dot(jnp.concatenate(parts, axis=1).astype(BF16), wd_ref[0])

    @pl.when(c == 0)
    def _():
        acc_scr[...] = contrib

    @pl.when(c > 0)
    def _():
        acc_scr[...] += contrib

    @pl.when(c == NG - 1)
    def _():
        y = x_ref[0] + gate_ref[0] * acc_scr[...]
        if final:
            y = y * lax.rsqrt(jnp.mean(y * y, axis=-1, keepdims=True) + EPS) * fg_ref[...]
        o_ref[0] = y


def _moe(x, g, shift, scale, gate, wg, bg, we, be, w_gate, w_up, w_down, final_g, final, tm=512):
    B, S, D = x.shape
    NG, PG, FH = MOE_GROUPS, MOE_PER_GROUP, MOE_HIDDEN
    wr = jnp.zeros((D, LANES), F32)
    wr = wr.at[:, :NG * PG].set(we.reshape(D, NG * PG).astype(F32)).at[:, NG * PG:NG * PG + NG].set(wg.astype(F32))
    br = jnp.zeros((1, LANES), F32)
    br = br.at[0, :NG * PG].set(be.reshape(NG * PG).astype(F32)).at[0, NG * PG:NG * PG + NG].set(bg.astype(F32))
    wr_hi = wr.astype(BF16)
    wr = jnp.stack([wr_hi, (wr - wr_hi.astype(F32)).astype(BF16)])
    grp = lambda w: w.reshape(NG, PG, D, FH).transpose(0, 2, 1, 3).reshape(NG, D, PG * FH).astype(BF16)
    wd = w_down.reshape(NG, PG * FH, D).astype(BF16)
    vec = pl.BlockSpec((1, 1, D), lambda b, i, c: (b, 0, 0))
    row = pl.BlockSpec((1, D), lambda b, i, c: (0, 0))
    wspec = lambda k, n: pl.BlockSpec((1, k, n), lambda b, i, c: (c, 0, 0))
    return pl.pallas_call(
        functools.partial(_moe_kernel, final=final),
        out_shape=jax.ShapeDtypeStruct((B, S, D), F32),
        grid=(B, S // tm, NG),
        in_specs=[pl.BlockSpec((1, tm, D), lambda b, i, c: (b, i, 0)), row, vec, vec, vec,
                  pl.BlockSpec((2, D, LANES), lambda b, i, c: (0, 0, 0)),
                  pl.BlockSpec((1, LANES), lambda b, i, c: (0, 0)),
                  wspec(D, PG * FH), wspec(D, PG * FH), wspec(PG * FH, D), row],
        out_specs=pl.BlockSpec((1, tm, D), lambda b, i, c: (b, i, 0)),
        scratch_shapes=[pltpu.VMEM((tm, D), BF16), pltpu.VMEM((tm, LANES), F32), pltpu.VMEM((tm, D), F32)],
        compiler_params=_cparams(("parallel", "parallel", "arbitrary")),
        name="moe",
    )(x, g.reshape(1, D), shift, scale, gate, wr, br, grp(w_gate), grp(w_up), wd, final_g.reshape(1, D))


def _mlstm_s5_layer(x, g, shift, scale, gate, w_in, conv_w, b_i, b_f, head_g, s5_params, w_out):
    H = MLSTM_HEADS
    A = MIX_A
    w_if = jnp.zeros((D_MODEL, LANES), F32).at[:, :2 * H].set(w_in[:, 4 * A:4 * A + 2 * H])
    weights = [w_in[:, :2 * A], w_in[:, 2 * A:4 * A], w_if, w_in[:, 4 * A + 2 * H:]]
    qk, vo, ifg, u = _norm_matmul(x, g, shift, scale, [w.astype(BF16) for w in weights], [BF16, BF16, F32, F32])
    gate_bias = jnp.zeros((1, LANES), F32).at[0, :H].set(b_i.astype(F32)).at[0, H:2 * H].set(b_f.astype(F32))
    hm = _mlstm(qk, vo, ifg, conv_w.astype(F32), gate_bias, head_g.reshape(1, A).astype(F32))
    ys = _s5s(u, _s5s_tables(*s5_params))
    w_out = w_out.astype(BF16)
    return _out_residual(x, gate, [hm, ys], [w_out[:A], w_out[A:]])


def _nsa_layer(x, g, shift, scale, gate, w_in, b_gate, cmp_pos, cmp_w1, cmp_b1, cmp_w2, cmp_b2, rel_bias, w_out):
    B, S, D = x.shape
    KV, R, DH = NSA_KV, NSA_R, NSA_DH
    w_g = jnp.zeros((D, KV, LANES), F32).at[:, :, :3 * R].set(w_in[:, D + 6 * KV_W:].reshape(D, KV, 3 * R))
    b_g = jnp.zeros((KV, LANES), F32).at[:, :3 * R].set(b_gate.reshape(KV, 3 * R).astype(F32))
    kv_cols = lambda i: w_in[:, D + i * KV_W:D + (i + 1) * KV_W]
    w_k = jnp.concatenate([kv_cols(0), kv_cols(2), kv_cols(4)], axis=1)
    w_v = jnp.concatenate([kv_cols(1), kv_cols(3), kv_cols(5)], axis=1)
    weights = [w_in[:, :D], w_k, w_v, w_g.reshape(D, KV * LANES)]
    q4, gv, kc, vc, ks, kw, vs_t, vw_t = _nsa_proj(x, g, shift, scale, [w.astype(BF16) for w in weights],
                                                   b_g.reshape(1, KV * LANES))
    grp = CMP_STRIDE
    xg = jnp.stack([kc, vc]).reshape(2, B, KV * S // grp, grp * DH)
    cmp = _compress(xg, cmp_pos, cmp_w1, cmp_b1, cmp_w2, cmp_b2).reshape(2, B, KV, S // grp, DH).astype(BF16)
    out = _nsa_t_attention(q4, gv, cmp[0], cmp[1].transpose(0, 1, 3, 2), ks, vs_t, kw, vw_t,
                           _nsa_t_tables(rel_bias, S))
    return _out_residual(x, gate, [out], [w_out.astype(BF16)])


def kernel(x, c, rel_bias, ada_w, ada_b, norm_g, final_g,
           a_w_in, a_conv, a_b_i, a_b_f, a_head_g,
           s5_lam_re, s5_lam_im, s5_log_dt, s5_b_re, s5_b_im, s5_c_re, s5_c_im,
           s5_d, s5_glu_w, s5_glu_b, a_w_out,
           n_w_in, n_b_gate, n_cmp_pos, n_cmp_w1, n_cmp_b1, n_cmp_w2, n_cmp_b2, n_w_out,
           r_grp_w, r_grp_b, r_exp_w, r_exp_b, e_w_gate, e_w_up, e_w_down):
    B, S, D = x.shape
    mod = _ada_mod(c, ada_w, ada_b).reshape(DEPTH, 2, B, 1, 3 * D)
    split = lambda m: (m[..., :D], m[..., D:2 * D], m[..., 2 * D:])
    for layer in range(DEPTH):
        shift, scale, gate = split(mod[layer, 0])
        j = layer // 2
        if layer % 2 == 0:
            s5_params = (s5_lam_re[j], s5_lam_im[j], s5_log_dt[j], s5_b_re[j], s5_b_im[j],
                         s5_c_re[j], s5_c_im[j], s5_d[j], s5_glu_w[j], s5_glu_b[j])
            x = _mlstm_s5_layer(x, norm_g[layer, 0], shift, scale, gate, a_w_in[j], a_conv[j], a_b_i[j], a_b_f[j],
                                a_head_g[j], s5_params, a_w_out[j])
        else:
            x = _nsa_layer(x, norm_g[layer, 0], shift, scale, gate, n_w_in[j], n_b_gate[j], n_cmp_pos[j],
                           n_cmp_w1[j], n_cmp_b1[j], n_cmp_w2[j], n_cmp_b2[j], rel_bias, n_w_out[j])
        shift, scale, gate = split(mod[layer, 1])
        x = _moe(x, norm_g[layer, 1], shift, scale, gate, r_grp_w[layer], r_grp_b[layer], r_exp_w[layer],
                 r_exp_b[layer], e_w_gate[layer], e_w_up[layer], e_w_down[layer], final_g,
                 final=(layer == DEPTH - 1))
    return x
```

```python
import functools
import math

import jax
import jax.numpy as jnp
from jax import lax
from jax.experimental import pallas as pl
from jax.experimental.pallas import tpu as pltpu

F32 = jnp.float32
BF16 = jnp.bfloat16
HIGHEST = lax.Precision.HIGHEST

D_MODEL = 1024
DEPTH = 2
MIX_A = 512
MLSTM_HEADS = 4
MLSTM_DH = MIX_A // MLSTM_HEADS
MLSTM_CHUNK = 128
CONV_K = 4
S5_GROUP = 16
S5_STATE = 64
S5_CHUNK = 16
NSA_HEADS = 16
NSA_KV = 4
NSA_R = NSA_HEADS // NSA_KV
NSA_DH = D_MODEL // NSA_HEADS
KV_W = NSA_KV * NSA_DH
CMP_BLOCK = 32
CMP_STRIDE = 16
CMP_HIDDEN = 256
SEL_BLOCK = 64
SEL_TOPK = 16
WINDOW = 512
FORCE = 1e9
REL_BUCKETS = 32
REL_MAX_DIST = 128
MOE_GROUPS = 4
MOE_PER_GROUP = 4
MOE_HIDDEN = 256
EPS = 1e-6
NEG = -1e30
BIG = 1e30
LOG2E = math.log2(math.e)
SEL_CHUNK = 1

LANES = 128
SUBLANES = 8
ATT_TILE = 256
VMEM_LIMIT = 56 * 1024 * 1024


def _cparams(sem):
    return pltpu.CompilerParams(dimension_semantics=sem, vmem_limit_bytes=VMEM_LIMIT)


def _dot(a, b, precision=None):
    return jnp.dot(a, b, preferred_element_type=F32, precision=precision)


def _dot_nt(a, b):
    return lax.dot_general(a, b, (((1,), (1,)), ((), ())), preferred_element_type=F32)


def _sigmoid(x):
    return 1.0 / (1.0 + jnp.exp(-x))


def _silu(x):
    return x * _sigmoid(x)


def _gelu_tanh(x):
    return 0.5 * x * (1.0 + jnp.tanh(math.sqrt(2.0 / math.pi) * (x + 0.044715 * (x * x * x))))


def _modulated_norm(x, g, shift, scale):
    y = x * lax.rsqrt(jnp.mean(x * x, axis=-1, keepdims=True) + EPS) * g
    return y * (1.0 + scale) + shift


def _ada_kernel(c_ref, w_ref, b_ref, o_ref):
    c = c_ref[...]
    o_ref[0] = _dot(_silu(c), w_ref[0]) + b_ref[0]


def _ada_mod(c, ada_w, ada_b):
    B, D = c.shape
    n_mod = ada_w.shape[0] * ada_w.shape[1]
    w = ada_w.reshape(n_mod, D, 3 * D)
    b = ada_b.reshape(n_mod, 1, 3 * D)
    tn = 1024
    return pl.pallas_call(
        _ada_kernel,
        out_shape=jax.ShapeDtypeStruct((n_mod, B, 3 * D), F32),
        grid=(n_mod, 3 * D // tn),
        in_specs=[pl.BlockSpec((B, D), lambda i, j: (0, 0)),
                  pl.BlockSpec((1, D, tn), lambda i, j: (i, 0, j)),
                  pl.BlockSpec((1, 1, tn), lambda i, j: (i, 0, j))],
        out_specs=pl.BlockSpec((1, B, tn), lambda i, j: (i, 0, j)),
        compiler_params=_cparams(("parallel", "parallel")),
        name="ada_mod",
    )(c, w, b)


def _norm_mm_kernel(*refs, n_w):
    x_ref, g_ref, sh_ref, sc_ref = refs[:4]
    w_refs = refs[4:4 + n_w]
    o_refs = refs[4 + n_w:]
    h = _modulated_norm(x_ref[0], g_ref[...], sh_ref[0], sc_ref[0]).astype(BF16)
    for w_ref, o_ref in zip(w_refs, o_refs):
        o_ref[0] = _dot(h, w_ref[...]).astype(o_ref.dtype)


def _norm_matmul(x, g, shift, scale, weights, out_dtypes, tm=512):
    B, S, D = x.shape
    n_w = len(weights)
    vec = pl.BlockSpec((1, 1, D), lambda b, i: (b, 0, 0))
    in_specs = [pl.BlockSpec((1, tm, D), lambda b, i: (b, i, 0)),
                pl.BlockSpec((1, D), lambda b, i: (0, 0)), vec, vec]
    in_specs += [pl.BlockSpec(w.shape, lambda b, i: (0, 0)) for w in weights]
    return pl.pallas_call(
        functools.partial(_norm_mm_kernel, n_w=n_w),
        out_shape=[jax.ShapeDtypeStruct((B, S, w.shape[1]), dt) for w, dt in zip(weights, out_dtypes)],
        grid=(B, S // tm),
        in_specs=in_specs,
        out_specs=[pl.BlockSpec((1, tm, w.shape[1]), lambda b, i: (b, i, 0)) for w in weights],
        compiler_params=_cparams(("parallel", "parallel")),
        name="norm_matmul",
    )(x, g.reshape(1, D), shift, scale, *weights)


def _mlstm_kernel(qk_ref, vo_ref, if_ref, cw_ref, gb_ref, hg_ref, tril_ref, o_ref,
                  xbuf, c_scr, n_scr, m_scr):
    pad = SUBLANES

    @pl.when(pl.program_id(1) == 0)
    def _():
        xbuf[:, 0:pad, :] = jnp.zeros((xbuf.shape[0], pad, 2 * MIX_A), F32)
        c_scr[...] = jnp.zeros_like(c_scr)
        n_scr[...] = jnp.zeros_like(n_scr)
        m_scr[...] = jnp.zeros_like(m_scr)

    for bb in range(qk_ref.shape[0]):
        _mlstm_chunk(qk_ref.at[bb], vo_ref.at[bb], if_ref.at[bb], cw_ref, gb_ref, hg_ref, tril_ref, o_ref.at[bb],
                     xbuf.at[bb], c_scr.at[bb], n_scr.at[bb], m_scr.at[bb])


def _mlstm_chunk(qk_ref, vo_ref, if_ref, cw_ref, gb_ref, hg_ref, tril_ref, o_ref, xbuf, c_scr, n_scr, m_scr):
    L, H, DH = MLSTM_CHUNK, MLSTM_HEADS, MLSTM_DH
    pad = SUBLANES
    xbuf[pad:pad + L, :] = qk_ref[...].astype(F32)
    cw = cw_ref[...]
    conv = None
    for j in range(CONV_K):
        lo = pad - (CONV_K - 1) + j
        t = xbuf[lo:lo + L, :] * cw[j:j + 1, :]
        conv = t if conv is None else conv + t
    xbuf[0:pad, :] = xbuf[L:L + pad, :]
    qk = _silu(conv)
    q = qk[:, :MIX_A]
    k = qk[:, MIX_A:] * (DH ** -0.5)
    vo = vo_ref[...].astype(F32)
    v = vo[:, :MIX_A]
    o_pre = vo[:, MIX_A:]

    ifb = if_ref[...] + gb_ref[...]
    lf = jnp.minimum(ifb, 0.0) - jnp.log1p(jnp.exp(-jnp.abs(ifb)))
    bcs = _dot(tril_ref[...], lf, precision=HIGHEST)
    ifb_t = ifb.T
    bcs_t = bcs.T
    row = lax.broadcasted_iota(jnp.int32, (L, L), 0)
    col = lax.broadcasted_iota(jnp.int32, (L, L), 1)
    causal = col <= row

    outs = []
    for h in range(H):
        sl = slice(h * DH, (h + 1) * DH)
        qh, kh, vh = q[:, sl], k[:, sl], v[:, sl]
        qb, kb = qh.astype(BF16), kh.astype(BF16)
        b_col = bcs[:, H + h:H + h + 1]
        b_row = bcs_t[H + h:H + h + 1, :]
        li_col = ifb[:, h:h + 1]
        li_row = ifb_t[h:h + 1, :]
        b_last = b_col[L - 1:L, :]
        m0 = m_scr[h][:, 0:1]
        c0 = c_scr[h]
        n0 = n_scr[h]

        log_d = jnp.where(causal, b_col - b_row + li_row, NEG)
        log_inter = b_col + m0
        m_t = jnp.maximum(log_inter, jnp.max(log_d, axis=1, keepdims=True))
        dmat = jnp.exp(log_d - m_t)
        a_inter = jnp.exp(log_inter - m_t)
        s = _dot_nt(qb, kb) * dmat
        num = _dot(s.astype(BF16), vh.astype(BF16)) + a_inter * _dot_nt(qb, c0.astype(BF16))
        den = jnp.sum(s, axis=1, keepdims=True) + a_inter * jnp.sum(qh * n0, axis=1, keepdims=True)
        hh = num / jnp.maximum(jnp.abs(den), jnp.exp(-m_t))

        w_col = b_last - b_col + li_col
        m_loc = jnp.max(w_col, axis=0, keepdims=True)
        e = jnp.exp(w_col - m_loc)
        c_loc = _dot((vh * e).T.astype(BF16), kb)
        n_loc = jnp.sum(kh * e, axis=0, keepdims=True)
        m_new = jnp.maximum(b_last + m0, m_loc)
        a = jnp.exp(b_last + m0 - m_new)
        sc = jnp.exp(m_loc - m_new)
        c_scr[h] = a * c0 + sc * c_loc
        n_scr[h] = a * n0 + sc * n_loc
        m_scr[h] = jnp.broadcast_to(m_new, (1, LANES))

        outs.append(hh * lax.rsqrt(jnp.mean(hh * hh, axis=1, keepdims=True) + EPS))
    hm = jnp.concatenate(outs, axis=1)
    o_ref[...] = (_sigmoid(o_pre) * (hm * hg_ref[...])).astype(o_ref.dtype)


def _mlstm(qk, vo, ifg, conv_w, gate_bias, head_g, rows=1):
    B, S, _ = qk.shape
    L, H, DH = MLSTM_CHUNK, MLSTM_HEADS, MLSTM_DH
    tril = jnp.tril(jnp.ones((L, L), F32))
    return pl.pallas_call(
        _mlstm_kernel,
        out_shape=jax.ShapeDtypeStruct((B, S, MIX_A), BF16),
        grid=(B // rows, S // L),
        in_specs=[pl.BlockSpec((rows, L, 2 * MIX_A), lambda b, c: (b, c, 0)),
                  pl.BlockSpec((rows, L, 2 * MIX_A), lambda b, c: (b, c, 0)),
                  pl.BlockSpec((rows, L, LANES), lambda b, c: (b, c, 0)),
                  pl.BlockSpec((CONV_K, 2 * MIX_A), lambda b, c: (0, 0)),
                  pl.BlockSpec((1, LANES), lambda b, c: (0, 0)),
                  pl.BlockSpec((1, MIX_A), lambda b, c: (0, 0)),
                  pl.BlockSpec((L, L), lambda b, c: (0, 0))],
        out_specs=pl.BlockSpec((rows, L, MIX_A), lambda b, c: (b, c, 0)),
        scratch_shapes=[pltpu.VMEM((rows, L + SUBLANES, 2 * MIX_A), F32),
                        pltpu.VMEM((rows, H, DH, DH), F32),
                        pltpu.VMEM((rows, H, 1, DH), F32),
                        pltpu.VMEM((rows, H, 1, LANES), F32)],
        compiler_params=_cparams(("parallel", "arbitrary")),
        name="mlstm",
    )(qk, vo, ifg, conv_w, gate_bias, head_g, tril)


S5_LT = LANES // S5_GROUP
S5_PAIRS = S5_CHUNK // 2


def _s5s_kernel(u_ref, h_ref, e_ref, kk_ref, are_ref, aim_ref, d_ref, gw_ref, gb_ref, o_ref, xl_scr, x0_scr):
    n_chunks = u_ref.shape[1] // S5_CHUNK
    half = S5_LT * S5_STATE
    tok = lambda s: u_ref[0, pl.ds(s, n_chunks, stride=S5_CHUNK), :]
    u2 = [jnp.concatenate([tok(2 * q), tok(2 * q + 1)], axis=1) for q in range(S5_PAIRS)]
    u2b = [v.astype(BF16) for v in u2]
    xl_scr[...] = functools.reduce(lambda a, b: a + b, [_dot(u2b[q], h_ref[0, q]) for q in range(S5_PAIRS)])
    a_re = are_ref[0]
    a_im = aim_ref[0]

    def body(a, carry):
        re, im = carry
        x0_scr[pl.ds(a, 1), 0:half] = re
        x0_scr[pl.ds(a, 1), half:2 * half] = im
        return (a_re * re - a_im * im + xl_scr[pl.ds(a, 1), 0:half],
                a_re * im + a_im * re + xl_scr[pl.ds(a, 1), half:2 * half])

    zero = jnp.zeros((1, half), F32)
    lax.fori_loop(0, n_chunks, body, (zero, zero), unroll=8)
    x0 = x0_scr[...].astype(BF16)
    for p in range(S5_PAIRS):
        y = _dot(x0, e_ref[0, p]) + u2[p] * d_ref[0]
        for q in range(p + 1):
            y = y + _dot(u2b[q], kk_ref[0, p - q])
        ys = _gelu_tanh(y)
        out = ys * _sigmoid(_dot(ys.astype(BF16), gw_ref[0]) + gb_ref[0])
        o_ref[0, pl.ds(2 * p, n_chunks, stride=S5_CHUNK), :] = out[:, :LANES].astype(o_ref.dtype)
        o_ref[0, pl.ds(2 * p + 1, n_chunks, stride=S5_CHUNK), :] = out[:, LANES:].astype(o_ref.dtype)


def _s5s_tables(lam_re, lam_im, log_dt, b_re, b_im, c_re, c_im, d_skip, glu_w, glu_b):
    T, C, P, LT = S5_CHUNK, S5_GROUP, S5_STATE, S5_LT
    G = lam_re.shape[0]
    NT = G // LT
    lam = lax.complex(lam_re.astype(F32), lam_im.astype(F32))
    dt = jnp.exp(log_dt.astype(F32))[:, None]
    lam_bar = jnp.exp(lam * dt)
    b_bar = ((lam_bar - 1.0) / lam)[..., None] * lax.complex(b_re.astype(F32), b_im.astype(F32))
    c_mat = lax.complex(c_re.astype(F32), c_im.astype(F32))
    taus = jnp.arange(T + 1, dtype=F32)
    pw = jnp.exp((lam * dt)[:, None, :] * taus[None, :, None])
    eye = jnp.eye(LT, dtype=F32)
    tiles = lambda a: a.reshape((NT, LT) + a.shape[1:])

    kern = jnp.einsum('gcp,gtp,gpd->gtdc', c_mat, pw[:, :T], b_bar, precision=HIGHEST).real
    kblk = jnp.einsum('nitdc,ij->ntidjc', tiles(kern), eye).reshape(NT, T, LANES, LANES)
    kblk = jnp.concatenate([jnp.zeros_like(kblk[:, :1]), kblk], axis=1)
    kk = jnp.stack([jnp.concatenate([jnp.concatenate([kblk[:, 2 * d + 1], kblk[:, 2 * d + 2]], axis=2),
                                     jnp.concatenate([kblk[:, 2 * d], kblk[:, 2 * d + 1]], axis=2)], axis=1)
                    for d in range(T // 2)], axis=1)

    hmat = pw[:, :T][:, ::-1, :, None] * b_bar[:, None]

    def state_cols(m):
        return jnp.einsum('nispc,ij->nsicjp', tiles(m), eye).reshape(NT, T, LANES, LT * P)

    h = jnp.concatenate([state_cols(hmat.real), state_cols(hmat.imag)], axis=3)
    h2 = h.reshape(NT, T // 2, 2 * LANES, 2 * LT * P)

    emat = c_mat[:, None] * pw[:, 1:][:, :, None, :]

    def state_rows(m):
        return jnp.einsum('nitcp,ij->ntjpic', tiles(m), eye).reshape(NT, T, LT * P, LANES)

    e = jnp.concatenate([state_rows(emat.real), state_rows(-emat.imag)], axis=2)
    e2 = e.reshape(NT, T // 2, 2, 2 * LT * P, LANES).transpose(0, 1, 3, 2, 4).reshape(NT, T // 2, 2 * LT * P, 2 * LANES)

    a_re = pw[:, T].real.reshape(NT, 1, LT * P)
    a_im = pw[:, T].imag.reshape(NT, 1, LT * P)
    pair = lambda v: jnp.tile(v.astype(F32).reshape(NT, 1, LANES), (1, 1, 2))
    gwb = jnp.einsum('nice,ij->nicje', tiles(glu_w.astype(F32)), eye).reshape(NT, LANES, LANES)
    zeros = jnp.zeros_like(gwb)
    gw2 = jnp.concatenate([jnp.concatenate([gwb, zeros], axis=2), jnp.concatenate([zeros, gwb], axis=2)], axis=1)
    return (h2.astype(BF16), e2.astype(BF16), kk.astype(BF16), a_re, a_im, pair(d_skip), gw2.astype(BF16), pair(glu_b))


def _s5s(u, tables):
    B, S, W = u.shape
    NT = W // LANES
    n_chunks = S // S5_CHUNK
    per_tile = lambda a: pl.BlockSpec((1,) + a.shape[1:], lambda j, b: (j,) + (0,) * (a.ndim - 1))
    return pl.pallas_call(
        _s5s_kernel,
        out_shape=jax.ShapeDtypeStruct((B, S, W), F32),
        grid=(NT, B),
        in_specs=[pl.BlockSpec((1, S, LANES), lambda j, b: (b, 0, j))] + [per_tile(t) for t in tables],
        out_specs=pl.BlockSpec((1, S, LANES), lambda j, b: (b, 0, j)),
        scratch_shapes=[pltpu.VMEM((n_chunks, 2 * S5_LT * S5_STATE), F32) for _ in range(2)],
        compiler_params=_cparams(("parallel", "parallel")),
        name="s5",
    )(u, *tables)


def _compress_kernel(x_ref, plo_ref, phi_ref, w1_ref, b1_ref, w2_ref, b2_ref, o_ref):
    x = x_ref[0, 0]
    half = x.shape[1]
    w1 = w1_ref[0]
    lo = _dot((x + plo_ref[0]).astype(BF16), w1[:half])
    hi = _dot((x + phi_ref[0]).astype(BF16), w1[half:])
    rows = x.shape[0]
    hid = _gelu_tanh(lo + pltpu.roll(hi, rows - 1, 0) + b1_ref[0])
    o_ref[0, 0] = _dot(hid.astype(BF16), w2_ref[0]) + b2_ref[0]


def _compress(xg, pos, w1, b1, w2, b2):
    _, B, rows, width = xg.shape
    pos_flat = pos.reshape(2, 2, 1, width).astype(F32)
    sel = lambda shape: pl.BlockSpec((1,) + shape, lambda j, b: (j, 0, 0))
    return pl.pallas_call(
        _compress_kernel,
        out_shape=jax.ShapeDtypeStruct((2, B, rows, NSA_DH), F32),
        grid=(2, B),
        in_specs=[pl.BlockSpec((1, 1, rows, width), lambda j, b: (j, b, 0, 0)),
                  sel((1, width)), sel((1, width)),
                  sel((2 * width, CMP_HIDDEN)), sel((1, CMP_HIDDEN)),
                  sel((CMP_HIDDEN, NSA_DH)), sel((1, NSA_DH))],
        out_specs=pl.BlockSpec((1, 1, rows, NSA_DH), lambda j, b: (j, b, 0, 0)),
        compiler_params=_cparams(("parallel", "parallel")),
        name="nsa_compress",
    )(xg, pos_flat[:, 0], pos_flat[:, 1], w1.astype(BF16), b1[:, None].astype(F32),
      w2.astype(BF16), b2[:, None].astype(F32))


def _t5_bucket(dist):
    dist = jnp.maximum(dist, 0)
    max_exact = REL_BUCKETS // 2
    log_ratio = jnp.log(jnp.maximum(dist, 1).astype(F32) / max_exact) / math.log(REL_MAX_DIST / max_exact)
    large = jnp.minimum(max_exact + (log_ratio * (REL_BUCKETS - max_exact)).astype(jnp.int32), REL_BUCKETS - 1)
    return jnp.where(dist < max_exact, dist, large)


def _nsa_proj_kernel(x_ref, g_ref, sh_ref, sc_ref, wq_ref, wk_ref, wv_ref, wg_ref, bg_ref,
                     q4_ref, gv_ref, kc_ref, vc_ref, ks_ref, kw_ref, vst_ref, vwt_ref):
    KV, R, DH, T = NSA_KV, NSA_R, NSA_DH, ATT_TILE
    h = _modulated_norm(x_ref[0], g_ref[...], sh_ref[0], sc_ref[0]).astype(BF16)
    q_t = (_dot(h, wq_ref[...]) * (DH ** -0.5 * LOG2E)).T.astype(BF16)
    gates_t = _sigmoid(_dot(h, wg_ref[...]) + bg_ref[...]).T
    row = lax.broadcasted_iota(jnp.int32, (SUBLANES, R * T), 0)
    for g in range(KV):
        q4_ref[0, g, 0] = jnp.concatenate([q_t[(g * R + r) * DH:(g * R + r + 1) * DH] for r in range(R)], axis=1)
        gv = jnp.zeros((SUBLANES, R * T), F32)
        for j in range(3):
            gj = jnp.concatenate([gates_t[g * LANES + 3 * r + j:g * LANES + 3 * r + j + 1] for r in range(R)], axis=1)
            gv = jnp.where(row == j, gj, gv)
        gv_ref[0, g, 0] = gv
    k3 = _dot(h, wk_ref[...])
    v3 = _dot(h, wv_ref[...])
    vs_t = v3[:, KV_W:2 * KV_W].T.astype(BF16)
    vw_t = v3[:, 2 * KV_W:].T.astype(BF16)
    for g in range(KV):
        cols = slice(g * DH, (g + 1) * DH)
        kc_ref[0, g] = k3[:, cols].astype(BF16)
        vc_ref[0, g] = v3[:, cols].astype(BF16)
        ks_ref[0, g] = k3[:, KV_W + g * DH:KV_W + (g + 1) * DH].astype(BF16)
        kw_ref[0, g] = k3[:, 2 * KV_W + g * DH:2 * KV_W + (g + 1) * DH].astype(BF16)
        vst_ref[0, g, 0] = vs_t[cols]
        vwt_ref[0, g, 0] = vw_t[cols]


def _nsa_proj(x, g, shift, scale, weights, b_gate):
    B, S, D = x.shape
    KV, R, DH, T = NSA_KV, NSA_R, NSA_DH, ATT_TILE
    vec = pl.BlockSpec((1, 1, D), lambda b, i: (b, 0, 0))
    keys = pl.BlockSpec((1, KV, T, DH), lambda b, i: (b, 0, i, 0))
    key_shape = jax.ShapeDtypeStruct((B, KV, S, DH), BF16)
    tile = lambda rows, width: pl.BlockSpec((1, KV, 1, rows, width), lambda b, i: (b, 0, i, 0, 0))
    tile_shape = lambda rows, width, dt: jax.ShapeDtypeStruct((B, KV, S // T, rows, width), dt)
    return pl.pallas_call(
        _nsa_proj_kernel,
        out_shape=[tile_shape(DH, R * T, BF16), tile_shape(SUBLANES, R * T, F32),
                   key_shape, key_shape, key_shape, key_shape,
                   tile_shape(DH, T, BF16), tile_shape(DH, T, BF16)],
        grid=(B, S // T),
        in_specs=[pl.BlockSpec((1, T, D), lambda b, i: (b, i, 0)),
                  pl.BlockSpec((1, D), lambda b, i: (0, 0)), vec, vec]
                 + [pl.BlockSpec(w.shape, lambda b, i: (0, 0)) for w in weights]
                 + [pl.BlockSpec(b_gate.shape, lambda b, i: (0, 0))],
        out_specs=[tile(DH, R * T), tile(SUBLANES, R * T), keys, keys, keys, keys, tile(DH, T), tile(DH, T)],
        compiler_params=_cparams(("parallel", "parallel")),
        name="nsa_proj",
    )(x, g.reshape(1, D), shift, scale, *weights, b_gate)


def _nsa_t_kernel(q4_ref, gv_ref, kc_ref, vct_ref, ks_ref, vst_ref, kw_ref, vwt_ref,
                  cfar_ref, band_ref, selb_ref, winb_ref, ovt_ref, o_ref, s_scr, sel_scr, sbuf):
    T = ATT_TILE
    R, DH = NSA_R, NSA_DH
    qi = pl.program_id(2)
    q0 = qi * T
    n_pad = kc_ref.shape[2]
    n_sel = ovt_ref.shape[0]
    CH = SEL_CHUNK
    n_far = selb_ref.shape[0] - 1
    n_win = winb_ref.shape[0] - 2
    band_rows = band_ref.shape[2] - T // CMP_STRIDE * 2

    q4 = q4_ref[0, 0, 0]
    t_lane = q0 + lax.broadcasted_iota(jnp.int32, (1, R * T), 1) % T

    ones_rows = DH
    with_ones = lambda v_t: jnp.concatenate([v_t, jnp.ones((ones_rows, v_t.shape[1]), v_t.dtype)], axis=0)
    gvec = lambda j: gv_ref[0, 0, 0, j:j + 1, :]

    grp = T // CMP_STRIDE
    s_scr[0:n_pad, :] = _dot(kc_ref[0, 0], q4) + cfar_ref[0]
    s_scr[n_pad:n_pad + 2 * grp, :] = jnp.zeros((2 * grp, R * T), F32)
    r0 = jnp.maximum(qi * grp - 2 * grp, 0)
    x0 = r0 - (qi * grp - 2 * grp)
    r0 = pl.multiple_of(r0, SUBLANES)
    x0 = pl.multiple_of(x0, SUBLANES)
    s_scr[pl.ds(r0, band_rows), :] += band_ref[0, 0, pl.ds(x0, band_rows), :]
    lim = pl.multiple_of(qi * grp + 2 * grp, SUBLANES)
    s_scr[pl.ds(lim, n_pad), :] = jnp.full((n_pad, R * T), NEG, F32)

    w_subs, w_vals = [], []
    for d in range(n_win + 1):
        kt = jnp.maximum(qi - d, 0)
        off = pl.multiple_of(kt * T, T)
        tile = jnp.where(qi >= d, d, n_win + 1)
        w_subs.append((_dot(kw_ref[0, 0, pl.ds(off, T), :], q4) + winb_ref[tile, 0]).astype(BF16))
        w_vals.append(with_ones(vwt_ref[0, 0, kt]))

    s = s_scr[0:n_pad, :]
    e = jnp.exp2(s - jnp.max(s, axis=0, keepdims=True))
    inv = jnp.where(t_lane >= CMP_BLOCK - 1, 1.0 / jnp.sum(e, axis=0, keepdims=True), 0.0)
    p = e * inv
    o_cmp = _dot(vct_ref[0, 0], p.astype(BF16))
    psum = functools.reduce(lambda a, b: a + b, [p[:, r * T:(r + 1) * T] for r in range(R)])

    m_w = jnp.max(functools.reduce(jnp.maximum, w_subs), axis=0, keepdims=True)
    acc = functools.reduce(lambda a, b: a + b,
                           [_dot(vj, jnp.exp2(sj - m_w)) for sj, vj in zip(w_subs, w_vals)])
    o_win = acc[:DH] * (1.0 / acc[DH:DH + 1])
    out_t = gvec(0) * o_cmp + gvec(2) * o_win

    imp_t = _dot(ovt_ref[...], psum, precision=HIGHEST)
    jj = lax.broadcasted_iota(jnp.int32, (n_sel, T), 0)
    blk_t = (q0 + lax.broadcasted_iota(jnp.int32, (1, T), 1)) // SEL_BLOCK
    forced = (jj == 0) | (jj == blk_t) | (jj == blk_t - 1)
    score = jnp.where(forced, FORCE, jnp.where(jj <= blk_t, imp_t, -1.0))
    n_blk = n_sel // SUBLANES
    rows = [score[v * SUBLANES:(v + 1) * SUBLANES] for v in range(n_blk)]
    cnts = [jnp.zeros((SUBLANES, T), F32) for _ in range(n_blk)]
    sub = lax.broadcasted_iota(jnp.int32, (SUBLANES, T), 0)
    for j2 in range(n_sel):
        c2 = score[j2:j2 + 1, :]
        for v in range(n_blk):
            lo = v * SUBLANES
            if lo > j2:
                beats = c2 >= rows[v]
            elif lo + SUBLANES - 1 <= j2:
                beats = c2 > rows[v]
            else:
                beats = (c2 > rows[v]) | ((c2 >= rows[v]) & (sub > j2 - lo))
            cnts[v] = cnts[v] + jnp.where(beats, 1.0, 0.0)
    cnt = jnp.concatenate(cnts, axis=0)
    chosen = (cnt < float(min(SEL_TOPK, n_sel))) & (jj <= blk_t)
    sel_scr[...] = jnp.where(chosen, 0.0, -BIG)

    def block_mask(kt):
        per_tile = T // SEL_BLOCK
        parts = [jnp.broadcast_to(sel_scr[pl.ds(kt * per_tile + i, 1), :], (SEL_BLOCK, T)) for i in range(per_tile)]
        m1 = jnp.concatenate(parts, axis=0)
        return jnp.concatenate([m1] * R, axis=1)

    def sel_scores(slot, kc):
        off = pl.multiple_of(kc * (CH * T), CH * T)
        s = _dot(ks_ref[0, 0, pl.ds(off, CH * T), :], q4)
        subs = []
        for j in range(CH):
            kt = kc * CH + j
            d = jnp.clip(qi - kt, 0, n_far)
            subs.append(s[j * T:(j + 1) * T] + selb_ref[d, 0] + block_mask(kt))
        s = jnp.concatenate(subs, axis=0).astype(BF16)
        sbuf[slot] = s
        return jnp.max(s, axis=0, keepdims=True).astype(F32)

    def sel_weighted(slot, kc, m_new):
        v_t = jnp.concatenate([vst_ref[0, 0, kc * CH + j] for j in range(CH)], axis=1)
        return _dot(with_ones(v_t), jnp.exp2(sbuf[slot] - m_new.astype(BF16)))

    last_chunk = vst_ref.shape[2] // CH - 1

    def sel_body(i, carry):
        m, acc, m_even = carry
        m_odd = sel_scores(1, 2 * i + 1)
        m_new = jnp.maximum(m, m_even)
        acc = jnp.exp2(m - m_new) * acc + sel_weighted(0, 2 * i, m_new)
        m_even = sel_scores(0, jnp.minimum(2 * i + 2, last_chunk))
        m_fin = jnp.maximum(m_new, m_odd)
        acc = jnp.exp2(m_new - m_fin) * acc + sel_weighted(1, 2 * i + 1, m_fin)
        return m_fin, acc, m_even

    n_chunks = qi // CH + 1
    _, acc, _ = lax.fori_loop(0, (n_chunks + 1) // 2, sel_body,
                              (jnp.full((1, R * T), NEG, F32), jnp.zeros((DH + ones_rows, R * T), F32),
                               sel_scores(0, 0)))
    out_t = out_t + gvec(1) * (acc[:DH] * (1.0 / acc[DH:DH + 1]))
    for pr in range(R // 2):
        pair = jnp.concatenate([out_t[:, (2 * pr) * T:(2 * pr + 1) * T],
                                out_t[:, (2 * pr + 1) * T:(2 * pr + 2) * T]], axis=0)
        o_ref[0, :, pr * 2 * DH:(pr + 1) * 2 * DH] = pair.T.astype(o_ref.dtype)


def _bias_lookup(table, dist):
    idx = _t5_bucket(dist)
    out = jnp.zeros(idx.shape + (table.shape[1],), F32)
    for k in range(table.shape[0]):
        out = out + jnp.where((idx == k)[..., None], table[k], 0.0)
    return out


def _nsa_t_tables(rel_bias, S):
    T, R, KV = ATT_TILE, NSA_R, NSA_KV
    table = rel_bias.astype(F32) * LOG2E
    ii = jnp.arange(T)
    delta = ii[None, :] - ii[:, None]

    def lanes(a):
        a = jnp.moveaxis(a, -1, 0)
        a = a.reshape((KV, R) + a.shape[1:])
        return jnp.moveaxis(a, 1, 2).reshape(KV, a.shape[2], R * a.shape[3])

    def tile(off):
        return lanes(_bias_lookup(table, off * T + delta))

    mask4 = lambda ok: jnp.tile(jnp.where(ok, 0.0, NEG), (1, R))[None]
    n_far = -(-REL_MAX_DIST // T) + 1
    selb = [tile(o) for o in range(n_far + 1)]
    selb[0] = selb[0] + mask4(delta >= 0)
    selb = jnp.stack(selb, axis=0)
    n_win = WINDOW // T
    winb = [tile(o) + mask4((o * T + delta >= 0) & (o * T + delta < WINDOW)) for o in range(n_win + 1)]
    winb.append(jnp.full_like(winb[0], NEG))
    winb = jnp.stack(winb, axis=0)

    grp = T // CMP_STRIDE
    far = _bias_lookup(table, jnp.asarray(2 * REL_MAX_DIST))
    xx = jnp.arange(4 * grp)
    bdist = ii[None, :] - CMP_STRIDE * (xx[:, None] - 2 * grp) - (CMP_BLOCK - 1)
    band = jnp.where((bdist >= 0)[..., None], _bias_lookup(table, bdist) - far, NEG)
    band = jnp.concatenate([lanes(band), jnp.zeros((KV, 2 * grp, R * T), F32)], axis=1)[:, None]
    cfar = jnp.repeat(far.reshape(KV, R), T, axis=1)[:, None]

    n_pad = S // CMP_STRIDE
    n_sel = S // SEL_BLOCK
    cmp_start = jnp.arange(n_pad) * CMP_STRIDE
    sel_start = jnp.arange(n_sel) * SEL_BLOCK
    overlap = jnp.clip(jnp.minimum(cmp_start[:, None] + CMP_BLOCK, sel_start[None] + SEL_BLOCK)
                       - jnp.maximum(cmp_start[:, None], sel_start[None]), 0).astype(F32) / CMP_BLOCK
    n_cmp = (S - CMP_BLOCK) // CMP_STRIDE + 1
    overlap_t = jnp.where((jnp.arange(n_pad) < n_cmp)[:, None], overlap, 0.0).T
    return cfar, band, selb, winb, overlap_t


def _nsa_t_attention(q4, gv, kcmp, vcmp_t, ks, vs_t, kw, vw_t, tables):
    B, KV, S, DH = ks.shape
    T = ATT_TILE
    cfar, band, selb, winb, overlap_t = tables
    gw = NSA_R * DH
    n_pad = kcmp.shape[2]
    seq = lambda a: pl.BlockSpec((1, 1) + a.shape[2:], lambda b, g, i: (b, g) + (0,) * (a.ndim - 2))
    qtile = lambda a: pl.BlockSpec((1, 1, 1) + a.shape[3:], lambda b, g, i: (b, g, i, 0, 0))
    grp = lambda a: pl.BlockSpec((1,) + a.shape[1:], lambda b, g, i: (g,) + (0,) * (a.ndim - 1))
    tiles = lambda a: pl.BlockSpec((a.shape[0], 1) + a.shape[2:], lambda b, g, i: (0, g, 0, 0))
    full = lambda a: pl.BlockSpec(a.shape, lambda b, g, i: (0,) * a.ndim)
    return pl.pallas_call(
        _nsa_t_kernel,
        out_shape=jax.ShapeDtypeStruct((B, S, KV * gw), BF16),
        grid=(B, KV, S // T),
        in_specs=[qtile(q4), qtile(gv),
                  seq(kcmp), seq(vcmp_t), seq(ks), seq(vs_t), seq(kw), seq(vw_t),
                  grp(cfar), grp(band), tiles(selb), tiles(winb), full(overlap_t)],
        out_specs=pl.BlockSpec((1, T, gw), lambda b, g, i: (b, i, g)),
        scratch_shapes=[pltpu.VMEM((2 * n_pad + 2 * (T // CMP_STRIDE), NSA_R * T), F32),
                        pltpu.VMEM((S // SEL_BLOCK, T), F32),
                        pltpu.VMEM((2, SEL_CHUNK * T, NSA_R * T), BF16)],
        compiler_params=_cparams(("parallel", "parallel", "arbitrary")),
        name="nsa_attention",
    )(q4, gv, kcmp, vcmp_t, ks, vs_t, kw, vw_t, cfar, band, selb, winb, overlap_t)


def _moe_kernel(*refs, n_in, final):
    x_ref, mgate_ref = refs[:2]
    a_refs = refs[2:2 + n_in]
    wo_refs = refs[2 + n_in:2 + 2 * n_in]
    (g_ref, sh_ref, sc_ref, gate_ref, wr_ref, br_ref, wg_ref, wu_ref, wd_ref, fg_ref,
     o_ref, x_scr, hb_scr, rt_scr, acc_scr) = refs[2 + 2 * n_in:]
    NG, PG, FH = MOE_GROUPS, MOE_PER_GROUP, MOE_HIDDEN
    c = pl.program_id(2)

    @pl.when(c == 0)
    def _():
        mix = functools.reduce(lambda a, b: a + b,
                               [_dot(a_ref[0].astype(BF16), wo_ref[...]) for a_ref, wo_ref in zip(a_refs, wo_refs)])
        x = x_ref[0] + mgate_ref[0] * mix
        x_scr[...] = x
        h = _modulated_norm(x, g_ref[...], sh_ref[0], sc_ref[0])
        h_hi = h.astype(BF16)
        hb_scr[...] = h_hi
        h_lo = (h - h_hi.astype(F32)).astype(BF16)
        logits = (_dot(h_hi, wr_ref[0]) + _dot(h_lo, wr_ref[0]) + _dot(h_hi, wr_ref[1]) + br_ref[...]).T
        gl = [logits[NG * PG + g:NG * PG + g + 1, :] for g in range(NG)]
        gmax = functools.reduce(jnp.maximum, gl)
        gtop = jnp.full_like(gmax, float(NG - 1))
        for g in reversed(range(NG - 1)):
            gtop = jnp.where(gl[g] == gmax, float(g), gtop)
        p_g = 1.0 / functools.reduce(lambda a, b: a + b, [jnp.exp(v - gmax) for v in gl])
        a = []
        for j in range(PG):
            v = logits[(NG - 1) * PG + j:(NG - 1) * PG + j + 1, :]
            for g in reversed(range(NG - 1)):
                v = jnp.where(gtop == float(g), logits[g * PG + j:g * PG + j + 1, :], v)
            a.append(v)

        def first_max(vals):
            vmax = functools.reduce(jnp.maximum, vals)
            taken = jnp.zeros_like(vmax) > 1.0
            hits = []
            for v in vals:
                hit = (v == vmax) & jnp.logical_not(taken)
                taken = taken | hit
                hits.append(hit)
            return vmax, hits

        v1, hit1 = first_max(a)
        rest = [jnp.where(hh, -jnp.inf, v) for hh, v in zip(hit1, a)]
        v2, hit2 = first_max(rest)
        e2 = jnp.exp(v2 - v1)
        w1 = p_g / (1.0 + e2)
        w2 = p_g * e2 / (1.0 + e2)
        tm = gtop.shape[1]
        row = lax.broadcasted_iota(jnp.int32, (SUBLANES, tm), 0)
        rt = jnp.where(row == PG, gtop, 0.0)
        for j in range(PG):
            wj = jnp.where(hit1[j], w1, jnp.where(hit2[j], w2, 0.0))
            rt = jnp.where(row == j, wj, rt)
        rt_scr[...] = jnp.concatenate([rt, jnp.zeros((LANES - SUBLANES, tm), F32)], axis=0).T

    hb = hb_scr[...]
    rt = rt_scr[...]
    in_group = rt[:, PG:PG + 1] == c.astype(F32)
    hid = _silu(_dot(hb, wg_ref[0])) * _dot(hb, wu_ref[0])
    parts = [hid[:, j * FH:(j + 1) * FH] * jnp.where(in_group, rt[:, j:j + 1], 0.0) for j in range(PG)]
    contrib = _dot(jnp.concatenate(parts, axis=1).astype(BF16), wd_ref[0])

    @pl.when(c == 0)
    def _():
        acc_scr[...] = contrib

    @pl.when(c > 0)
    def _():
        acc_scr[...] += contrib

    @pl.when(c == NG - 1)
    def _():
        y = x_scr[...] + gate_ref[0] * acc_scr[...]
        if final:
            y = y * lax.rsqrt(jnp.mean(y * y, axis=-1, keepdims=True) + EPS) * fg_ref[...]
        o_ref[0] = y


def _moe(x, mix_gate, acts, w_outs, g, shift, scale, gate, wg, bg, we, be, w_gate, w_up, w_down, final_g, final,
         tm=512):
    B, S, D = x.shape
    n_in = len(acts)
    NG, PG, FH = MOE_GROUPS, MOE_PER_GROUP, MOE_HIDDEN
    wr = jnp.zeros((D, LANES), F32)
    wr = wr.at[:, :NG * PG].set(we.reshape(D, NG * PG).astype(F32)).at[:, NG * PG:NG * PG + NG].set(wg.astype(F32))
    br = jnp.zeros((1, LANES), F32)
    br = br.at[0, :NG * PG].set(be.reshape(NG * PG).astype(F32)).at[0, NG * PG:NG * PG + NG].set(bg.astype(F32))
    wr_hi = wr.astype(BF16)
    wr = jnp.stack([wr_hi, (wr - wr_hi.astype(F32)).astype(BF16)])
    grp = lambda w: w.reshape(NG, PG, D, FH).transpose(0, 2, 1, 3).reshape(NG, D, PG * FH).astype(BF16)
    wd = w_down.reshape(NG, PG * FH, D).astype(BF16)
    vec = pl.BlockSpec((1, 1, D), lambda b, i, c: (b, 0, 0))
    row = pl.BlockSpec((1, D), lambda b, i, c: (0, 0))
    wspec = lambda k, n: pl.BlockSpec((1, k, n), lambda b, i, c: (c, 0, 0))
    tokens = lambda n: pl.BlockSpec((1, tm, n), lambda b, i, c: (b, i, 0))
    return pl.pallas_call(
        functools.partial(_moe_kernel, n_in=n_in, final=final),
        out_shape=jax.ShapeDtypeStruct((B, S, D), F32),
        grid=(B, S // tm, NG),
        in_specs=[tokens(D), vec] + [tokens(a.shape[2]) for a in acts]
                 + [pl.BlockSpec(w.shape, lambda b, i, c: (0, 0)) for w in w_outs]
                 + [row, vec, vec, vec,
                    pl.BlockSpec((2, D, LANES), lambda b, i, c: (0, 0, 0)),
                    pl.BlockSpec((1, LANES), lambda b, i, c: (0, 0)),
                    wspec(D, PG * FH), wspec(D, PG * FH), wspec(PG * FH, D), row],
        out_specs=tokens(D),
        scratch_shapes=[pltpu.VMEM((tm, D), F32), pltpu.VMEM((tm, D), BF16), pltpu.VMEM((tm, LANES), F32),
                        pltpu.VMEM((tm, D), F32)],
        compiler_params=_cparams(("parallel", "parallel", "arbitrary")),
        name="moe",
    )(x, mix_gate, *acts, *w_outs, g.reshape(1, D), shift, scale, gate, wr, br, grp(w_gate), grp(w_up), wd,
      final_g.reshape(1, D))


def _mlstm_s5_layer(x, g, shift, scale, w_in, conv_w, b_i, b_f, head_g, s5_params, w_out):
    H = MLSTM_HEADS
    A = MIX_A
    w_if = jnp.zeros((D_MODEL, LANES), F32).at[:, :2 * H].set(w_in[:, 4 * A:4 * A + 2 * H])
    weights = [w_in[:, :2 * A], w_in[:, 2 * A:4 * A], w_if, w_in[:, 4 * A + 2 * H:]]
    qk, vo, ifg, u = _norm_matmul(x, g, shift, scale, [w.astype(BF16) for w in weights], [BF16, BF16, F32, F32])
    gate_bias = jnp.zeros((1, LANES), F32).at[0, :H].set(b_i.astype(F32)).at[0, H:2 * H].set(b_f.astype(F32))
    hm = _mlstm(qk, vo, ifg, conv_w.astype(F32), gate_bias, head_g.reshape(1, A).astype(F32))
    ys = _s5s(u, _s5s_tables(*s5_params))
    w_out = w_out.astype(BF16)
    return [hm, ys], [w_out[:A], w_out[A:]]


def _nsa_layer(x, g, shift, scale, w_in, b_gate, cmp_pos, cmp_w1, cmp_b1, cmp_w2, cmp_b2, rel_bias, w_out):
    B, S, D = x.shape
    KV, R, DH = NSA_KV, NSA_R, NSA_DH
    w_g = jnp.zeros((D, KV, LANES), F32).at[:, :, :3 * R].set(w_in[:, D + 6 * KV_W:].reshape(D, KV, 3 * R))
    b_g = jnp.zeros((KV, LANES), F32).at[:, :3 * R].set(b_gate.reshape(KV, 3 * R).astype(F32))
    kv_cols = lambda i: w_in[:, D + i * KV_W:D + (i + 1) * KV_W]
    w_k = jnp.concatenate([kv_cols(0), kv_cols(2), kv_cols(4)], axis=1)
    w_v = jnp.concatenate([kv_cols(1), kv_cols(3), kv_cols(5)], axis=1)
    weights = [w_in[:, :D], w_k, w_v, w_g.reshape(D, KV * LANES)]
    q4, gv, kc, vc, ks, kw, vs_t, vw_t = _nsa_proj(x, g, shift, scale, [w.astype(BF16) for w in weights],
                                                   b_g.reshape(1, KV * LANES))
    grp = CMP_STRIDE
    xg = jnp.stack([kc, vc]).reshape(2, B, KV * S // grp, grp * DH)
    cmp = _compress(xg, cmp_pos, cmp_w1, cmp_b1, cmp_w2, cmp_b2).reshape(2, B, KV, S // grp, DH).astype(BF16)
    out = _nsa_t_attention(q4, gv, cmp[0], cmp[1].transpose(0, 1, 3, 2), ks, vs_t, kw, vw_t,
                           _nsa_t_tables(rel_bias, S))
    return [out], [w_out.astype(BF16)]


def kernel(x, c, rel_bias, ada_w, ada_b, norm_g, final_g,
           a_w_in, a_conv, a_b_i, a_b_f, a_head_g,
           s5_lam_re, s5_lam_im, s5_log_dt, s5_b_re, s5_b_im, s5_c_re, s5_c_im,
           s5_d, s5_glu_w, s5_glu_b, a_w_out,
           n_w_in, n_b_gate, n_cmp_pos, n_cmp_w1, n_cmp_b1, n_cmp_w2, n_cmp_b2, n_w_out,
           r_grp_w, r_grp_b, r_exp_w, r_exp_b, e_w_gate, e_w_up, e_w_down):
    B, S, D = x.shape
    mod = _ada_mod(c, ada_w, ada_b).reshape(DEPTH, 2, B, 1, 3 * D)
    split = lambda m: (m[..., :D], m[..., D:2 * D], m[..., 2 * D:])
    for layer in range(DEPTH):
        shift, scale, mix_gate = split(mod[layer, 0])
        j = layer // 2
        if layer % 2 == 0:
            s5_params = (s5_lam_re[j], s5_lam_im[j], s5_log_dt[j], s5_b_re[j], s5_b_im[j],
                         s5_c_re[j], s5_c_im[j], s5_d[j], s5_glu_w[j], s5_glu_b[j])
            acts, w_outs = _mlstm_s5_layer(x, norm_g[layer, 0], shift, scale, a_w_in[j], a_conv[j], a_b_i[j],
                                           a_b_f[j], a_head_g[j], s5_params, a_w_out[j])
        else:
            acts, w_outs = _nsa_layer(x, norm_g[layer, 0], shift, scale, n_w_in[j], n_b_gate[j], n_cmp_pos[j],
                                      n_cmp_w1[j], n_cmp_b1[j], n_cmp_w2[j], n_cmp_b2[j], rel_bias, n_w_out[j])
        shift, scale, gate = split(mod[layer, 1])
        x = _moe(x, mix_gate, acts, w_outs, norm_g[layer, 1], shift, scale, gate, r_grp_w[layer], r_grp_b[layer],
                 r_exp_w[layer], r_exp_b[layer], e_w_gate[layer], e_w_up[layer], e_w_down[layer], final_g,
                 final=(layer == DEPTH - 1))
    return x
```

```python
import functools
import math

import jax
import jax.numpy as jnp
from jax import lax
from jax.experimental import pallas as pl
from jax.experimental.pallas import tpu as pltpu

F32 = jnp.float32
BF16 = jnp.bfloat16
HIGHEST = lax.Precision.HIGHEST

D_MODEL = 1024
DEPTH = 2
MIX_A = 512
MLSTM_HEADS = 4
MLSTM_DH = MIX_A // MLSTM_HEADS
MLSTM_CHUNK = 128
CONV_K = 4
S5_GROUP = 16
S5_STATE = 64
S5_CHUNK = 16
NSA_HEADS = 16
NSA_KV = 4
NSA_R = NSA_HEADS // NSA_KV
NSA_DH = D_MODEL // NSA_HEADS
KV_W = NSA_KV * NSA_DH
CMP_BLOCK = 32
CMP_STRIDE = 16
CMP_HIDDEN = 256
SEL_BLOCK = 64
SEL_TOPK = 16
WINDOW = 512
FORCE = 1e9
REL_BUCKETS = 32
REL_MAX_DIST = 128
MOE_GROUPS = 4
MOE_PER_GROUP = 4
MOE_HIDDEN = 256
EPS = 1e-6
NEG = -1e30
BIG = 1e30
LOG2E = math.log2(math.e)

LANES = 128
SUBLANES = 8
ATT_TILE = 256
MOE_WIN = 192
MOE_ALIGN = 16
VMEM_LIMIT = 56 * 1024 * 1024


def _cparams(sem):
    return pltpu.CompilerParams(dimension_semantics=sem, vmem_limit_bytes=VMEM_LIMIT)


def _dot(a, b, precision=None):
    return jnp.dot(a, b, preferred_element_type=F32, precision=precision)


def _dot_nt(a, b):
    return lax.dot_general(a, b, (((1,), (1,)), ((), ())), preferred_element_type=F32)


def _sigmoid(x):
    return 1.0 / (1.0 + jnp.exp(-x))


def _silu(x):
    return x * _sigmoid(x)


def _gelu_tanh(x):
    return 0.5 * x * (1.0 + jnp.tanh(math.sqrt(2.0 / math.pi) * (x + 0.044715 * (x * x * x))))


def _modulated_norm(x, g, shift, scale):
    y = x * lax.rsqrt(jnp.mean(x * x, axis=-1, keepdims=True) + EPS) * g
    return y * (1.0 + scale) + shift


def _ada_kernel(c_ref, w_ref, b_ref, o_ref):
    c = c_ref[...]
    o_ref[0] = _dot(_silu(c), w_ref[0]) + b_ref[0]


def _ada_mod(c, ada_w, ada_b):
    B, D = c.shape
    n_mod = ada_w.shape[0] * ada_w.shape[1]
    w = ada_w.reshape(n_mod, D, 3 * D)
    b = ada_b.reshape(n_mod, 1, 3 * D)
    tn = 1024
    return pl.pallas_call(
        _ada_kernel,
        out_shape=jax.ShapeDtypeStruct((n_mod, B, 3 * D), F32),
        grid=(n_mod, 3 * D // tn),
        in_specs=[pl.BlockSpec((B, D), lambda i, j: (0, 0)),
                  pl.BlockSpec((1, D, tn), lambda i, j: (i, 0, j)),
                  pl.BlockSpec((1, 1, tn), lambda i, j: (i, 0, j))],
        out_specs=pl.BlockSpec((1, B, tn), lambda i, j: (i, 0, j)),
        compiler_params=_cparams(("parallel", "parallel")),
        name="ada_mod",
    )(c, w, b)


def _norm_mm_kernel(*refs, n_w):
    x_ref, g_ref, sh_ref, sc_ref = refs[:4]
    w_refs = refs[4:4 + n_w]
    o_refs = refs[4 + n_w:]
    h = _modulated_norm(x_ref[0], g_ref[...], sh_ref[0], sc_ref[0]).astype(BF16)
    for w_ref, o_ref in zip(w_refs, o_refs):
        o_ref[0] = _dot(h, w_ref[...]).astype(o_ref.dtype)


def _norm_matmul(x, g, shift, scale, weights, out_dtypes, tm=512):
    B, S, D = x.shape
    n_w = len(weights)
    vec = pl.BlockSpec((1, 1, D), lambda b, i: (b, 0, 0))
    in_specs = [pl.BlockSpec((1, tm, D), lambda b, i: (b, i, 0)),
                pl.BlockSpec((1, D), lambda b, i: (0, 0)), vec, vec]
    in_specs += [pl.BlockSpec(w.shape, lambda b, i: (0, 0)) for w in weights]
    return pl.pallas_call(
        functools.partial(_norm_mm_kernel, n_w=n_w),
        out_shape=[jax.ShapeDtypeStruct((B, S, w.shape[1]), dt) for w, dt in zip(weights, out_dtypes)],
        grid=(B, S // tm),
        in_specs=in_specs,
        out_specs=[pl.BlockSpec((1, tm, w.shape[1]), lambda b, i: (b, i, 0)) for w in weights],
        compiler_params=_cparams(("parallel", "parallel")),
        name="norm_matmul",
    )(x, g.reshape(1, D), shift, scale, *weights)


def _mlstm_kernel(qk_ref, vo_ref, if_ref, cw_ref, gb_ref, hg_ref, tril_ref, o_ref,
                  xbuf, c_scr, n_scr, m_scr):
    pad = SUBLANES

    @pl.when(pl.program_id(1) == 0)
    def _():
        xbuf[:, 0:pad, :] = jnp.zeros((xbuf.shape[0], pad, 2 * MIX_A), F32)
        c_scr[...] = jnp.zeros_like(c_scr)
        n_scr[...] = jnp.zeros_like(n_scr)
        m_scr[...] = jnp.zeros_like(m_scr)

    for bb in range(qk_ref.shape[0]):
        _mlstm_chunk(qk_ref.at[bb], vo_ref.at[bb], if_ref.at[bb], cw_ref, gb_ref, hg_ref, tril_ref, o_ref.at[bb],
                     xbuf.at[bb], c_scr.at[bb], n_scr.at[bb], m_scr.at[bb])


def _mlstm_chunk(qk_ref, vo_ref, if_ref, cw_ref, gb_ref, hg_ref, tril_ref, o_ref, xbuf, c_scr, n_scr, m_scr):
    L, H, DH = MLSTM_CHUNK, MLSTM_HEADS, MLSTM_DH
    pad = SUBLANES
    xbuf[pad:pad + L, :] = qk_ref[...].astype(F32)
    cw = cw_ref[...]
    conv = None
    for j in range(CONV_K):
        lo = pad - (CONV_K - 1) + j
        t = xbuf[lo:lo + L, :] * cw[j:j + 1, :]
        conv = t if conv is None else conv + t
    xbuf[0:pad, :] = xbuf[L:L + pad, :]
    qk = _silu(conv)
    q = qk[:, :MIX_A]
    k = qk[:, MIX_A:] * (DH ** -0.5)
    vo = vo_ref[...].astype(F32)
    v = vo[:, :MIX_A]
    o_pre = vo[:, MIX_A:]

    ifb = if_ref[...] + gb_ref[...]
    lf = jnp.minimum(ifb, 0.0) - jnp.log1p(jnp.exp(-jnp.abs(ifb)))
    bcs = _dot(tril_ref[...], lf, precision=HIGHEST)
    ifb_t = ifb.T
    bcs_t = bcs.T
    row = lax.broadcasted_iota(jnp.int32, (L, L), 0)
    col = lax.broadcasted_iota(jnp.int32, (L, L), 1)
    causal = col <= row

    outs = []
    for h in range(H):
        sl = slice(h * DH, (h + 1) * DH)
        qh, kh, vh = q[:, sl], k[:, sl], v[:, sl]
        qb, kb = qh.astype(BF16), kh.astype(BF16)
        b_col = bcs[:, H + h:H + h + 1]
        b_row = bcs_t[H + h:H + h + 1, :]
        li_col = ifb[:, h:h + 1]
        li_row = ifb_t[h:h + 1, :]
        b_last = b_col[L - 1:L, :]
        m0 = m_scr[h][:, 0:1]
        c0 = c_scr[h]
        n0 = n_scr[h]

        log_d = jnp.where(causal, b_col - b_row + li_row, NEG)
        log_inter = b_col + m0
        m_t = jnp.maximum(log_inter, jnp.max(log_d, axis=1, keepdims=True))
        dmat = jnp.exp(log_d - m_t)
        a_inter = jnp.exp(log_inter - m_t)
        s = _dot_nt(qb, kb) * dmat
        num = _dot(s.astype(BF16), vh.astype(BF16)) + a_inter * _dot_nt(qb, c0.astype(BF16))
        den = jnp.sum(s, axis=1, keepdims=True) + a_inter * jnp.sum(qh * n0, axis=1, keepdims=True)
        hh = num / jnp.maximum(jnp.abs(den), jnp.exp(-m_t))

        w_col = b_last - b_col + li_col
        m_loc = jnp.max(w_col, axis=0, keepdims=True)
        e = jnp.exp(w_col - m_loc)
        c_loc = _dot((vh * e).T.astype(BF16), kb)
        n_loc = jnp.sum(kh * e, axis=0, keepdims=True)
        m_new = jnp.maximum(b_last + m0, m_loc)
        a = jnp.exp(b_last + m0 - m_new)
        sc = jnp.exp(m_loc - m_new)
        c_scr[h] = a * c0 + sc * c_loc
        n_scr[h] = a * n0 + sc * n_loc
        m_scr[h] = jnp.broadcast_to(m_new, (1, LANES))

        outs.append(hh * lax.rsqrt(jnp.mean(hh * hh, axis=1, keepdims=True) + EPS))
    hm = jnp.concatenate(outs, axis=1)
    o_ref[...] = (_sigmoid(o_pre) * (hm * hg_ref[...])).astype(o_ref.dtype)


def _mlstm(qk, vo, ifg, conv_w, gate_bias, head_g):
    B, S, _ = qk.shape
    rows = 1
    L, H, DH = MLSTM_CHUNK, MLSTM_HEADS, MLSTM_DH
    tril = jnp.tril(jnp.ones((L, L), F32))
    return pl.pallas_call(
        _mlstm_kernel,
        out_shape=jax.ShapeDtypeStruct((B, S, MIX_A), BF16),
        grid=(B // rows, S // L),
        in_specs=[pl.BlockSpec((rows, L, 2 * MIX_A), lambda b, c: (b, c, 0)),
                  pl.BlockSpec((rows, L, 2 * MIX_A), lambda b, c: (b, c, 0)),
                  pl.BlockSpec((rows, L, LANES), lambda b, c: (b, c, 0)),
                  pl.BlockSpec((CONV_K, 2 * MIX_A), lambda b, c: (0, 0)),
                  pl.BlockSpec((1, LANES), lambda b, c: (0, 0)),
                  pl.BlockSpec((1, MIX_A), lambda b, c: (0, 0)),
                  pl.BlockSpec((L, L), lambda b, c: (0, 0))],
        out_specs=pl.BlockSpec((rows, L, MIX_A), lambda b, c: (b, c, 0)),
        scratch_shapes=[pltpu.VMEM((rows, L + SUBLANES, 2 * MIX_A), F32),
                        pltpu.VMEM((rows, H, DH, DH), F32),
                        pltpu.VMEM((rows, H, 1, DH), F32),
                        pltpu.VMEM((rows, H, 1, LANES), F32)],
        compiler_params=_cparams(("parallel", "arbitrary")),
        name="mlstm",
    )(qk, vo, ifg, conv_w, gate_bias, head_g, tril)


S5_LT = LANES // S5_GROUP
S5_PAIRS = S5_CHUNK // 2


def _s5s_kernel(u_ref, h_ref, e_ref, kk_ref, are_ref, aim_ref, d_ref, gw_ref, gb_ref, o_ref, xl_scr, x0_scr):
    n_chunks = u_ref.shape[1] // S5_CHUNK
    half = S5_LT * S5_STATE
    tok = lambda s: u_ref[0, pl.ds(s, n_chunks, stride=S5_CHUNK), :]
    u2 = [jnp.concatenate([tok(2 * q), tok(2 * q + 1)], axis=1) for q in range(S5_PAIRS)]
    u2b = [v.astype(BF16) for v in u2]
    xl_scr[...] = functools.reduce(lambda a, b: a + b, [_dot(u2b[q], h_ref[0, q]) for q in range(S5_PAIRS)])
    a_re = are_ref[0]
    a_im = aim_ref[0]

    def body(a, carry):
        re, im = carry
        x0_scr[pl.ds(a, 1), 0:half] = re
        x0_scr[pl.ds(a, 1), half:2 * half] = im
        return (a_re * re - a_im * im + xl_scr[pl.ds(a, 1), 0:half],
                a_re * im + a_im * re + xl_scr[pl.ds(a, 1), half:2 * half])

    zero = jnp.zeros((1, half), F32)
    lax.fori_loop(0, n_chunks, body, (zero, zero), unroll=8)
    x0 = x0_scr[...].astype(BF16)
    for p in range(S5_PAIRS):
        y = _dot(x0, e_ref[0, p]) + u2[p] * d_ref[0]
        for q in range(p + 1):
            y = y + _dot(u2b[q], kk_ref[0, p - q])
        ys = _gelu_tanh(y)
        out = ys * _sigmoid(_dot(ys.astype(BF16), gw_ref[0]) + gb_ref[0])
        o_ref[0, pl.ds(2 * p, n_chunks, stride=S5_CHUNK), :] = out[:, :LANES].astype(o_ref.dtype)
        o_ref[0, pl.ds(2 * p + 1, n_chunks, stride=S5_CHUNK), :] = out[:, LANES:].astype(o_ref.dtype)


def _s5s_tables(lam_re, lam_im, log_dt, b_re, b_im, c_re, c_im, d_skip, glu_w, glu_b):
    T, C, P, LT = S5_CHUNK, S5_GROUP, S5_STATE, S5_LT
    G = lam_re.shape[0]
    NT = G // LT
    lam = lax.complex(lam_re.astype(F32), lam_im.astype(F32))
    dt = jnp.exp(log_dt.astype(F32))[:, None]
    lam_bar = jnp.exp(lam * dt)
    b_bar = ((lam_bar - 1.0) / lam)[..., None] * lax.complex(b_re.astype(F32), b_im.astype(F32))
    c_mat = lax.complex(c_re.astype(F32), c_im.astype(F32))
    taus = jnp.arange(T + 1, dtype=F32)
    pw = jnp.exp((lam * dt)[:, None, :] * taus[None, :, None])
    eye = jnp.eye(LT, dtype=F32)
    tiles = lambda a: a.reshape((NT, LT) + a.shape[1:])

    kern = jnp.einsum('gcp,gtp,gpd->gtdc', c_mat, pw[:, :T], b_bar, precision=HIGHEST).real
    kblk = jnp.einsum('nitdc,ij->ntidjc', tiles(kern), eye).reshape(NT, T, LANES, LANES)
    kblk = jnp.concatenate([jnp.zeros_like(kblk[:, :1]), kblk], axis=1)
    kk = jnp.stack([jnp.concatenate([jnp.concatenate([kblk[:, 2 * d + 1], kblk[:, 2 * d + 2]], axis=2),
                                     jnp.concatenate([kblk[:, 2 * d], kblk[:, 2 * d + 1]], axis=2)], axis=1)
                    for d in range(T // 2)], axis=1)

    hmat = pw[:, :T][:, ::-1, :, None] * b_bar[:, None]

    def state_cols(m):
        return jnp.einsum('nispc,ij->nsicjp', tiles(m), eye).reshape(NT, T, LANES, LT * P)

    h = jnp.concatenate([state_cols(hmat.real), state_cols(hmat.imag)], axis=3)
    h2 = h.reshape(NT, T // 2, 2 * LANES, 2 * LT * P)

    emat = c_mat[:, None] * pw[:, 1:][:, :, None, :]

    def state_rows(m):
        return jnp.einsum('nitcp,ij->ntjpic', tiles(m), eye).reshape(NT, T, LT * P, LANES)

    e = jnp.concatenate([state_rows(emat.real), state_rows(-emat.imag)], axis=2)
    e2 = e.reshape(NT, T // 2, 2, 2 * LT * P, LANES).transpose(0, 1, 3, 2, 4).reshape(NT, T // 2, 2 * LT * P, 2 * LANES)

    a_re = pw[:, T].real.reshape(NT, 1, LT * P)
    a_im = pw[:, T].imag.reshape(NT, 1, LT * P)
    pair = lambda v: jnp.tile(v.astype(F32).reshape(NT, 1, LANES), (1, 1, 2))
    gwb = jnp.einsum('nice,ij->nicje', tiles(glu_w.astype(F32)), eye).reshape(NT, LANES, LANES)
    zeros = jnp.zeros_like(gwb)
    gw2 = jnp.concatenate([jnp.concatenate([gwb, zeros], axis=2), jnp.concatenate([zeros, gwb], axis=2)], axis=1)
    return (h2.astype(BF16), e2.astype(BF16), kk.astype(BF16), a_re, a_im, pair(d_skip), gw2.astype(BF16), pair(glu_b))


def _s5s(u, tables):
    B, S, W = u.shape
    NT = W // LANES
    n_chunks = S // S5_CHUNK
    per_tile = lambda a: pl.BlockSpec((1,) + a.shape[1:], lambda j, b: (j,) + (0,) * (a.ndim - 1))
    return pl.pallas_call(
        _s5s_kernel,
        out_shape=jax.ShapeDtypeStruct((B, S, W), F32),
        grid=(NT, B),
        in_specs=[pl.BlockSpec((1, S, LANES), lambda j, b: (b, 0, j))] + [per_tile(t) for t in tables],
        out_specs=pl.BlockSpec((1, S, LANES), lambda j, b: (b, 0, j)),
        scratch_shapes=[pltpu.VMEM((n_chunks, 2 * S5_LT * S5_STATE), F32) for _ in range(2)],
        compiler_params=_cparams(("parallel", "parallel")),
        name="s5",
    )(u, *tables)


def _compress_kernel(x_ref, plo_ref, phi_ref, w1_ref, b1_ref, w2_ref, b2_ref, o_ref):
    x = x_ref[0, 0]
    half = x.shape[1]
    w1 = w1_ref[0]
    lo = _dot((x + plo_ref[0]).astype(BF16), w1[:half])
    hi = _dot((x + phi_ref[0]).astype(BF16), w1[half:])
    rows = x.shape[0]
    hid = _gelu_tanh(lo + pltpu.roll(hi, rows - 1, 0) + b1_ref[0])
    o_ref[0, 0] = _dot(hid.astype(BF16), w2_ref[0]) + b2_ref[0]


def _compress(xg, pos, w1, b1, w2, b2):
    _, B, rows, width = xg.shape
    pos_flat = pos.reshape(2, 2, 1, width).astype(F32)
    sel = lambda shape: pl.BlockSpec((1,) + shape, lambda j, b: (j, 0, 0))
    return pl.pallas_call(
        _compress_kernel,
        out_shape=jax.ShapeDtypeStruct((2, B, rows, NSA_DH), F32),
        grid=(2, B),
        in_specs=[pl.BlockSpec((1, 1, rows, width), lambda j, b: (j, b, 0, 0)),
                  sel((1, width)), sel((1, width)),
                  sel((2 * width, CMP_HIDDEN)), sel((1, CMP_HIDDEN)),
                  sel((CMP_HIDDEN, NSA_DH)), sel((1, NSA_DH))],
        out_specs=pl.BlockSpec((1, 1, rows, NSA_DH), lambda j, b: (j, b, 0, 0)),
        compiler_params=_cparams(("parallel", "parallel")),
        name="nsa_compress",
    )(xg, pos_flat[:, 0], pos_flat[:, 1], w1.astype(BF16), b1[:, None].astype(F32),
      w2.astype(BF16), b2[:, None].astype(F32))


def _t5_bucket(dist):
    dist = jnp.maximum(dist, 0)
    max_exact = REL_BUCKETS // 2
    log_ratio = jnp.log(jnp.maximum(dist, 1).astype(F32) / max_exact) / math.log(REL_MAX_DIST / max_exact)
    large = jnp.minimum(max_exact + (log_ratio * (REL_BUCKETS - max_exact)).astype(jnp.int32), REL_BUCKETS - 1)
    return jnp.where(dist < max_exact, dist, large)


def _nsa_proj_kernel(x_ref, g_ref, sh_ref, sc_ref, wq_ref, wk_ref, wv_ref, wg_ref, bg_ref,
                     q4_ref, gv_ref, kc_ref, vc_ref, ks_ref, kw_ref, vst_ref, vwt_ref):
    KV, R, DH, T = NSA_KV, NSA_R, NSA_DH, ATT_TILE
    h = _modulated_norm(x_ref[0], g_ref[...], sh_ref[0], sc_ref[0]).astype(BF16)
    q_t = (_dot(h, wq_ref[...]) * (DH ** -0.5 * LOG2E)).T.astype(BF16)
    gates_t = _sigmoid(_dot(h, wg_ref[...]) + bg_ref[...]).T
    row = lax.broadcasted_iota(jnp.int32, (SUBLANES, R * T), 0)
    for g in range(KV):
        q4_ref[0, g, 0] = jnp.concatenate([q_t[(g * R + r) * DH:(g * R + r + 1) * DH] for r in range(R)], axis=1)
        gv = jnp.zeros((SUBLANES, R * T), F32)
        for j in range(3):
            gj = jnp.concatenate([gates_t[g * LANES + 3 * r + j:g * LANES + 3 * r + j + 1] for r in range(R)], axis=1)
            gv = jnp.where(row == j, gj, gv)
        gv_ref[0, g, 0] = gv
    k3 = _dot(h, wk_ref[...])
    v3 = _dot(h, wv_ref[...])
    vs_t = v3[:, KV_W:2 * KV_W].T.astype(BF16)
    vw_t = v3[:, 2 * KV_W:].T.astype(BF16)
    for g in range(KV):
        cols = slice(g * DH, (g + 1) * DH)
        kc_ref[0, g] = k3[:, cols].astype(BF16)
        vc_ref[0, g] = v3[:, cols].astype(BF16)
        ks_ref[0, g] = k3[:, KV_W + g * DH:KV_W + (g + 1) * DH].astype(BF16)
        kw_ref[0, g] = k3[:, 2 * KV_W + g * DH:2 * KV_W + (g + 1) * DH].astype(BF16)
        vst_ref[0, g, 0] = vs_t[cols]
        vwt_ref[0, g, 0] = vw_t[cols]


def _nsa_proj(x, g, shift, scale, weights, b_gate):
    B, S, D = x.shape
    KV, R, DH, T = NSA_KV, NSA_R, NSA_DH, ATT_TILE
    vec = pl.BlockSpec((1, 1, D), lambda b, i: (b, 0, 0))
    keys = pl.BlockSpec((1, KV, T, DH), lambda b, i: (b, 0, i, 0))
    key_shape = jax.ShapeDtypeStruct((B, KV, S, DH), BF16)
    tile = lambda rows, width: pl.BlockSpec((1, KV, 1, rows, width), lambda b, i: (b, 0, i, 0, 0))
    tile_shape = lambda rows, width, dt: jax.ShapeDtypeStruct((B, KV, S // T, rows, width), dt)
    return pl.pallas_call(
        _nsa_proj_kernel,
        out_shape=[tile_shape(DH, R * T, BF16), tile_shape(SUBLANES, R * T, F32),
                   key_shape, key_shape, key_shape, key_shape,
                   tile_shape(DH, T, BF16), tile_shape(DH, T, BF16)],
        grid=(B, S // T),
        in_specs=[pl.BlockSpec((1, T, D), lambda b, i: (b, i, 0)),
                  pl.BlockSpec((1, D), lambda b, i: (0, 0)), vec, vec]
                 + [pl.BlockSpec(w.shape, lambda b, i: (0, 0)) for w in weights]
                 + [pl.BlockSpec(b_gate.shape, lambda b, i: (0, 0))],
        out_specs=[tile(DH, R * T), tile(SUBLANES, R * T), keys, keys, keys, keys, tile(DH, T), tile(DH, T)],
        compiler_params=_cparams(("parallel", "parallel")),
        name="nsa_proj",
    )(x, g.reshape(1, D), shift, scale, *weights, b_gate)


def _nsa_t_kernel(q4_ref, gv_ref, kc_ref, vct_ref, ks_ref, vst_ref, kw_ref, vwt_ref,
                  cfar_ref, band_ref, selb_ref, winb_ref, ovt_ref, o_ref, s_scr, sel_scr, sbuf):
    T = ATT_TILE
    R, DH = NSA_R, NSA_DH
    qi = pl.program_id(2)
    q0 = qi * T
    n_pad = kc_ref.shape[2]
    n_sel = ovt_ref.shape[0]
    n_far = selb_ref.shape[0] - 1
    n_win = winb_ref.shape[0] - 2
    band_rows = band_ref.shape[2] - T // CMP_STRIDE * 2

    q4 = q4_ref[0, 0, 0]
    t_lane = q0 + lax.broadcasted_iota(jnp.int32, (1, R * T), 1) % T

    ones_rows = DH
    with_ones = lambda v_t: jnp.concatenate([v_t, jnp.ones((ones_rows, v_t.shape[1]), v_t.dtype)], axis=0)
    gvec = lambda j: gv_ref[0, 0, 0, j:j + 1, :]

    grp = T // CMP_STRIDE
    s_scr[0:n_pad, :] = _dot(kc_ref[0, 0], q4) + cfar_ref[0]
    s_scr[n_pad:n_pad + 2 * grp, :] = jnp.zeros((2 * grp, R * T), F32)
    r0 = jnp.maximum(qi * grp - 2 * grp, 0)
    x0 = r0 - (qi * grp - 2 * grp)
    r0 = pl.multiple_of(r0, SUBLANES)
    x0 = pl.multiple_of(x0, SUBLANES)
    s_scr[pl.ds(r0, band_rows), :] += band_ref[0, 0, pl.ds(x0, band_rows), :]
    lim = pl.multiple_of(qi * grp + 2 * grp, SUBLANES)
    s_scr[pl.ds(lim, n_pad), :] = jnp.full((n_pad, R * T), NEG, F32)

    w_subs, w_vals = [], []
    for d in range(n_win + 1):
        kt = jnp.maximum(qi - d, 0)
        off = pl.multiple_of(kt * T, T)
        tile = jnp.where(qi >= d, d, n_win + 1)
        w_subs.append((_dot(kw_ref[0, 0, pl.ds(off, T), :], q4) + winb_ref[tile, 0]).astype(BF16))
        w_vals.append(with_ones(vwt_ref[0, 0, kt]))

    s = s_scr[0:n_pad, :]
    e = jnp.exp2(s - jnp.max(s, axis=0, keepdims=True))
    inv = jnp.where(t_lane >= CMP_BLOCK - 1, 1.0 / jnp.sum(e, axis=0, keepdims=True), 0.0)
    p = e * inv
    o_cmp = _dot(vct_ref[0, 0], p.astype(BF16))
    psum = functools.reduce(lambda a, b: a + b, [p[:, r * T:(r + 1) * T] for r in range(R)])

    m_w = jnp.max(functools.reduce(jnp.maximum, w_subs), axis=0, keepdims=True)
    acc = functools.reduce(lambda a, b: a + b,
                           [_dot(vj, jnp.exp2(sj - m_w)) for sj, vj in zip(w_subs, w_vals)])
    o_win = acc[:DH] * (1.0 / acc[DH:DH + 1])
    out_t = gvec(0) * o_cmp + gvec(2) * o_win

    imp_t = _dot(ovt_ref[...], psum, precision=HIGHEST)
    jj = lax.broadcasted_iota(jnp.int32, (n_sel, T), 0)
    blk_t = (q0 + lax.broadcasted_iota(jnp.int32, (1, T), 1)) // SEL_BLOCK
    forced = (jj == 0) | (jj == blk_t) | (jj == blk_t - 1)
    score = jnp.where(forced, FORCE, jnp.where(jj <= blk_t, imp_t, -1.0))
    n_blk = n_sel // SUBLANES
    rows = [score[v * SUBLANES:(v + 1) * SUBLANES] for v in range(n_blk)]
    cnts = [jnp.zeros((SUBLANES, T), F32) for _ in range(n_blk)]
    sub = lax.broadcasted_iota(jnp.int32, (SUBLANES, T), 0)
    for j2 in range(n_sel):
        c2 = score[j2:j2 + 1, :]
        for v in range(n_blk):
            lo = v * SUBLANES
            if lo > j2:
                beats = c2 >= rows[v]
            elif lo + SUBLANES - 1 <= j2:
                beats = c2 > rows[v]
            else:
                beats = (c2 > rows[v]) | ((c2 >= rows[v]) & (sub > j2 - lo))
            cnts[v] = cnts[v] + jnp.where(beats, 1.0, 0.0)
    cnt = jnp.concatenate(cnts, axis=0)
    chosen = (cnt < float(min(SEL_TOPK, n_sel))) & (jj <= blk_t)
    sel_scr[...] = jnp.where(chosen, 0.0, -BIG)

    def block_mask(kt):
        per_tile = T // SEL_BLOCK
        parts = [jnp.broadcast_to(sel_scr[pl.ds(kt * per_tile + i, 1), :], (SEL_BLOCK, T)) for i in range(per_tile)]
        m1 = jnp.concatenate(parts, axis=0)
        return jnp.concatenate([m1] * R, axis=1)

    def sel_scores(slot, kc):
        off = pl.multiple_of(kc * T, T)
        s = _dot(ks_ref[0, 0, pl.ds(off, T), :], q4)
        s = (s + selb_ref[jnp.clip(qi - kc, 0, n_far), 0] + block_mask(kc)).astype(BF16)
        sbuf[slot] = s
        return jnp.max(s, axis=0, keepdims=True).astype(F32)

    def sel_weighted(slot, kc, m_new):
        return _dot(with_ones(vst_ref[0, 0, kc]), jnp.exp2(sbuf[slot] - m_new.astype(BF16)))

    last_tile = vst_ref.shape[2] - 1

    def sel_body(i, carry):
        m, acc, m_even = carry
        m_odd = sel_scores(1, 2 * i + 1)
        m_new = jnp.maximum(m, m_even)
        acc = jnp.exp2(m - m_new) * acc + sel_weighted(0, 2 * i, m_new)
        m_even = sel_scores(0, jnp.minimum(2 * i + 2, last_tile))
        m_fin = jnp.maximum(m_new, m_odd)
        acc = jnp.exp2(m_new - m_fin) * acc + sel_weighted(1, 2 * i + 1, m_fin)
        return m_fin, acc, m_even

    _, acc, _ = lax.fori_loop(0, qi // 2 + 1, sel_body,
                              (jnp.full((1, R * T), NEG, F32), jnp.zeros((DH + ones_rows, R * T), F32),
                               sel_scores(0, 0)))
    out_t = out_t + gvec(1) * (acc[:DH] * (1.0 / acc[DH:DH + 1]))
    for pr in range(R // 2):
        pair = jnp.concatenate([out_t[:, (2 * pr) * T:(2 * pr + 1) * T],
                                out_t[:, (2 * pr + 1) * T:(2 * pr + 2) * T]], axis=0)
        o_ref[0, :, pr * 2 * DH:(pr + 1) * 2 * DH] = pair.T.astype(o_ref.dtype)


def _bias_lookup(table, dist):
    idx = _t5_bucket(dist)
    out = jnp.zeros(idx.shape + (table.shape[1],), F32)
    for k in range(table.shape[0]):
        out = out + jnp.where((idx == k)[..., None], table[k], 0.0)
    return out


def _nsa_t_tables(rel_bias, S):
    T, R, KV = ATT_TILE, NSA_R, NSA_KV
    table = rel_bias.astype(F32) * LOG2E
    ii = jnp.arange(T)
    delta = ii[None, :] - ii[:, None]

    def lanes(a):
        a = jnp.moveaxis(a, -1, 0)
        a = a.reshape((KV, R) + a.shape[1:])
        return jnp.moveaxis(a, 1, 2).reshape(KV, a.shape[2], R * a.shape[3])

    def tile(off):
        return lanes(_bias_lookup(table, off * T + delta))

    mask4 = lambda ok: jnp.tile(jnp.where(ok, 0.0, NEG), (1, R))[None]
    n_far = -(-REL_MAX_DIST // T) + 1
    selb = [tile(o) for o in range(n_far + 1)]
    selb[0] = selb[0] + mask4(delta >= 0)
    selb = jnp.stack(selb, axis=0)
    n_win = WINDOW // T
    winb = [tile(o) + mask4((o * T + delta >= 0) & (o * T + delta < WINDOW)) for o in range(n_win + 1)]
    winb.append(jnp.full_like(winb[0], NEG))
    winb = jnp.stack(winb, axis=0)

    grp = T // CMP_STRIDE
    far = _bias_lookup(table, jnp.asarray(2 * REL_MAX_DIST))
    xx = jnp.arange(4 * grp)
    bdist = ii[None, :] - CMP_STRIDE * (xx[:, None] - 2 * grp) - (CMP_BLOCK - 1)
    band = jnp.where((bdist >= 0)[..., None], _bias_lookup(table, bdist) - far, NEG)
    band = jnp.concatenate([lanes(band), jnp.zeros((KV, 2 * grp, R * T), F32)], axis=1)[:, None]
    cfar = jnp.repeat(far.reshape(KV, R), T, axis=1)[:, None]

    n_pad = S // CMP_STRIDE
    n_sel = S // SEL_BLOCK
    cmp_start = jnp.arange(n_pad) * CMP_STRIDE
    sel_start = jnp.arange(n_sel) * SEL_BLOCK
    overlap = jnp.clip(jnp.minimum(cmp_start[:, None] + CMP_BLOCK, sel_start[None] + SEL_BLOCK)
                       - jnp.maximum(cmp_start[:, None], sel_start[None]), 0).astype(F32) / CMP_BLOCK
    n_cmp = (S - CMP_BLOCK) // CMP_STRIDE + 1
    overlap_t = jnp.where((jnp.arange(n_pad) < n_cmp)[:, None], overlap, 0.0).T
    return cfar, band, selb, winb, overlap_t


def _nsa_t_attention(q4, gv, kcmp, vcmp_t, ks, vs_t, kw, vw_t, tables):
    B, KV, S, _ = kw.shape
    T = ATT_TILE
    cfar, band, selb, winb, overlap_t = tables
    gw = NSA_R * NSA_DH
    n_pad = kcmp.shape[2]
    seq = lambda a: pl.BlockSpec((1, 1) + a.shape[2:], lambda b, g, i: (b, g) + (0,) * (a.ndim - 2))
    qtile = lambda a: pl.BlockSpec((1, 1, 1) + a.shape[3:], lambda b, g, i: (b, g, i, 0, 0))
    grp = lambda a: pl.BlockSpec((1,) + a.shape[1:], lambda b, g, i: (g,) + (0,) * (a.ndim - 1))
    tiles = lambda a: pl.BlockSpec((a.shape[0], 1) + a.shape[2:], lambda b, g, i: (0, g, 0, 0))
    full = lambda a: pl.BlockSpec(a.shape, lambda b, g, i: (0,) * a.ndim)
    return pl.pallas_call(
        _nsa_t_kernel,
        out_shape=jax.ShapeDtypeStruct((B, S, KV * gw), BF16),
        grid=(B, KV, S // T),
        in_specs=[qtile(q4), qtile(gv),
                  seq(kcmp), seq(vcmp_t), seq(ks), seq(vs_t), seq(kw), seq(vw_t),
                  grp(cfar), grp(band), tiles(selb), tiles(winb), full(overlap_t)],
        out_specs=pl.BlockSpec((1, T, gw), lambda b, g, i: (b, i, g)),
        scratch_shapes=[pltpu.VMEM((2 * n_pad + 2 * (T // CMP_STRIDE), NSA_R * T), F32),
                        pltpu.VMEM((S // SEL_BLOCK, T), F32),
                        pltpu.VMEM((2, T, NSA_R * T), BF16)],
        compiler_params=_cparams(("parallel", "parallel", "arbitrary")),
        name="nsa_attention",
    )(q4, gv, kcmp, vcmp_t, ks, vs_t, kw, vw_t, cfar, band, selb, winb, overlap_t)


def _moe_kernel(*refs, n_in, final):
    x_ref, mgate_ref = refs[:2]
    a_refs = refs[2:2 + n_in]
    wo_refs = refs[2 + n_in:2 + 2 * n_in]
    (g_ref, sh_ref, sc_ref, gate_ref, wr_ref, br_ref, before_ref, wg_ref, wu_ref, wd_ref, fg_ref,
     o_ref, x_scr, hs_scr, rts_scr, acc_scr, perm_t_scr, meta) = refs[2 + 2 * n_in:]
    NG, PG, FH = MOE_GROUPS, MOE_PER_GROUP, MOE_HIDDEN
    c = pl.program_id(2)

    @pl.when(c == 0)
    def _():
        mix = functools.reduce(lambda a, b: a + b,
                               [_dot(a_ref[0].astype(BF16), wo_ref[...]) for a_ref, wo_ref in zip(a_refs, wo_refs)])
        x = x_ref[0] + mgate_ref[0] * mix
        x_scr[...] = x
        h = _modulated_norm(x, g_ref[...], sh_ref[0], sc_ref[0])
        h_hi = h.astype(BF16)
        h_lo = (h - h_hi.astype(F32)).astype(BF16)
        logits = (_dot(h_hi, wr_ref[0]) + _dot(h_lo, wr_ref[0]) + _dot(h_hi, wr_ref[1]) + br_ref[...]).T
        gl = [logits[NG * PG + g:NG * PG + g + 1, :] for g in range(NG)]
        gmax = functools.reduce(jnp.maximum, gl)
        gtop = jnp.full_like(gmax, float(NG - 1))
        for g in reversed(range(NG - 1)):
            gtop = jnp.where(gl[g] == gmax, float(g), gtop)
        p_g = 1.0 / functools.reduce(lambda a, b: a + b, [jnp.exp(v - gmax) for v in gl])
        a = []
        for j in range(PG):
            v = logits[(NG - 1) * PG + j:(NG - 1) * PG + j + 1, :]
            for g in reversed(range(NG - 1)):
                v = jnp.where(gtop == float(g), logits[g * PG + j:g * PG + j + 1, :], v)
            a.append(v)

        def first_max(vals):
            vmax = functools.reduce(jnp.maximum, vals)
            taken = jnp.zeros_like(vmax) > 1.0
            hits = []
            for v in vals:
                hit = (v == vmax) & jnp.logical_not(taken)
                taken = taken | hit
                hits.append(hit)
            return vmax, hits

        v1, hit1 = first_max(a)
        rest = [jnp.where(hh, -jnp.inf, v) for hh, v in zip(hit1, a)]
        v2, hit2 = first_max(rest)
        e2 = jnp.exp(v2 - v1)
        w1 = p_g / (1.0 + e2)
        w2 = p_g * e2 / (1.0 + e2)
        tm = gtop.shape[1]
        row = lax.broadcasted_iota(jnp.int32, (SUBLANES, tm), 0)
        onehot = [jnp.where(gtop == float(g), 1.0, 0.0) for g in range(NG)]
        oh8 = jnp.zeros((SUBLANES, tm), F32)
        for g in range(NG):
            oh8 = jnp.where(row == g, onehot[g], oh8)
        before = _dot(oh8.astype(BF16), before_ref[...])
        pos = jnp.zeros_like(gtop)
        off = jnp.int32(0)
        for g in range(NG):
            cnt = jnp.sum(onehot[g]).astype(jnp.int32)
            meta[g] = off
            meta[NG + g] = cnt
            pos = pos + onehot[g] * (before[g:g + 1, :] + off.astype(F32))
            off = off + cnt
        rt = jnp.where(row == PG, gtop, jnp.where(row == PG + 1, pos, 0.0))
        for j in range(PG):
            wj = jnp.where(hit1[j], w1, jnp.where(hit2[j], w2, 0.0))
            rt = jnp.where(row == j, wj, rt)
        rt_tok = jnp.concatenate([rt, jnp.zeros((LANES - SUBLANES, tm), F32)], axis=0).T
        rid = lax.broadcasted_iota(jnp.int32, (tm, tm), 0).astype(F32)
        cid = lax.broadcasted_iota(jnp.int32, (tm, tm), 1).astype(F32)
        perm = jnp.where(rid == pos, 1.0, 0.0).astype(BF16)
        perm_t_scr[...] = jnp.where(rt_tok[:, PG + 1:PG + 2] == cid, 1.0, 0.0).astype(BF16)
        pad = hs_scr.shape[0] - tm
        hs_scr[0:tm, :] = _dot(perm, h_hi).astype(BF16)
        hs_scr[tm:, :] = jnp.zeros((pad, hs_scr.shape[1]), BF16)
        r1 = rt_tok.astype(BF16)
        res = rt_tok - r1.astype(F32)
        r2 = res.astype(BF16)
        r3 = (res - r2.astype(F32)).astype(BF16)
        rts_scr[0:tm, :] = _dot(perm, r1) + _dot(perm, r2) + _dot(perm, r3)
        rts_scr[tm:, :] = jnp.full((pad, LANES), -1.0, F32)
        acc_scr[...] = jnp.zeros_like(acc_scr)

    WIN = hs_scr.shape[0] - x_scr.shape[0]
    off = meta[c]
    cnt = meta[NG + c]
    base = (off // MOE_ALIGN) * MOE_ALIGN
    n_win = jnp.where(cnt > 0, (off + cnt - base + WIN - 1) // WIN, 0)
    cf = c.astype(F32)

    def win_body(w, carry):
        start = pl.multiple_of(base + w * WIN, MOE_ALIGN)
        hs = hs_scr[pl.ds(start, WIN), :]
        rt = rts_scr[pl.ds(start, WIN), :]
        in_group = rt[:, PG:PG + 1] == cf
        hid = _silu(_dot(hs, wg_ref[0])) * _dot(hs, wu_ref[0])
        parts = [hid[:, j * FH:(j + 1) * FH] * jnp.where(in_group, rt[:, j:j + 1], 0.0) for j in range(PG)]
        acc_scr[pl.ds(start, WIN), :] += _dot(jnp.concatenate(parts, axis=1).astype(BF16), wd_ref[0])
        return carry

    lax.fori_loop(0, n_win, win_body, 0)

    @pl.when(c == NG - 1)
    def _():
        tm = x_scr.shape[0]
        ys = acc_scr[0:tm, :]
        ys_hi = ys.astype(BF16)
        ys_lo = (ys - ys_hi.astype(F32)).astype(BF16)
        y = x_scr[...] + gate_ref[0] * (_dot(perm_t_scr[...], ys_hi) + _dot(perm_t_scr[...], ys_lo))
        if final:
            y = y * lax.rsqrt(jnp.mean(y * y, axis=-1, keepdims=True) + EPS) * fg_ref[...]
        o_ref[0] = y


def _moe(x, mix_gate, acts, w_outs, g, shift, scale, gate, wg, bg, we, be, w_gate, w_up, w_down, final_g, final,
         tm=512):
    B, S, D = x.shape
    n_in = len(acts)
    NG, PG, FH = MOE_GROUPS, MOE_PER_GROUP, MOE_HIDDEN
    wr = jnp.zeros((D, LANES), F32)
    wr = wr.at[:, :NG * PG].set(we.reshape(D, NG * PG).astype(F32)).at[:, NG * PG:NG * PG + NG].set(wg.astype(F32))
    br = jnp.zeros((1, LANES), F32)
    br = br.at[0, :NG * PG].set(be.reshape(NG * PG).astype(F32)).at[0, NG * PG:NG * PG + NG].set(bg.astype(F32))
    wr_hi = wr.astype(BF16)
    wr = jnp.stack([wr_hi, (wr - wr_hi.astype(F32)).astype(BF16)])
    grp = lambda w: w.reshape(NG, PG, D, FH).transpose(0, 2, 1, 3).reshape(NG, D, PG * FH).astype(BF16)
    wd = w_down.reshape(NG, PG * FH, D).astype(BF16)
    ids = jnp.arange(tm)
    before = (ids[:, None] < ids[None, :]).astype(BF16)
    vec = pl.BlockSpec((1, 1, D), lambda b, i, c: (b, 0, 0))
    row = pl.BlockSpec((1, D), lambda b, i, c: (0, 0))
    wspec = lambda k, n: pl.BlockSpec((1, k, n), lambda b, i, c: (c, 0, 0))
    tokens = lambda n: pl.BlockSpec((1, tm, n), lambda b, i, c: (b, i, 0))
    return pl.pallas_call(
        functools.partial(_moe_kernel, n_in=n_in, final=final),
        out_shape=jax.ShapeDtypeStruct((B, S, D), F32),
        grid=(B, S // tm, NG),
        in_specs=[tokens(D), vec] + [tokens(a.shape[2]) for a in acts]
                 + [pl.BlockSpec(w.shape, lambda b, i, c: (0, 0)) for w in w_outs]
                 + [row, vec, vec, vec,
                    pl.BlockSpec((2, D, LANES), lambda b, i, c: (0, 0, 0)),
                    pl.BlockSpec((1, LANES), lambda b, i, c: (0, 0)),
                    pl.BlockSpec((tm, tm), lambda b, i, c: (0, 0)),
                    wspec(D, PG * FH), wspec(D, PG * FH), wspec(PG * FH, D), row],
        out_specs=tokens(D),
        scratch_shapes=[pltpu.VMEM((tm, D), F32), pltpu.VMEM((tm + MOE_WIN, D), BF16),
                        pltpu.VMEM((tm + MOE_WIN, LANES), F32), pltpu.VMEM((tm + MOE_WIN, D), F32),
                        pltpu.VMEM((tm, tm), BF16), pltpu.SMEM((2 * NG,), jnp.int32)],
        compiler_params=_cparams(("parallel", "parallel", "arbitrary")),
        name="moe",
    )(x, mix_gate, *acts, *w_outs, g.reshape(1, D), shift, scale, gate, wr, br, before, grp(w_gate), grp(w_up), wd,
      final_g.reshape(1, D))


def _mlstm_s5_layer(x, g, shift, scale, w_in, conv_w, b_i, b_f, head_g, s5_params, w_out):
    H = MLSTM_HEADS
    A = MIX_A
    w_if = jnp.zeros((D_MODEL, LANES), F32).at[:, :2 * H].set(w_in[:, 4 * A:4 * A + 2 * H])
    weights = [w_in[:, :2 * A], w_in[:, 2 * A:4 * A], w_if, w_in[:, 4 * A + 2 * H:]]
    qk, vo, ifg, u = _norm_matmul(x, g, shift, scale, [w.astype(BF16) for w in weights], [BF16, BF16, F32, F32])
    gate_bias = jnp.zeros((1, LANES), F32).at[0, :H].set(b_i.astype(F32)).at[0, H:2 * H].set(b_f.astype(F32))
    hm = _mlstm(qk, vo, ifg, conv_w.astype(F32), gate_bias, head_g.reshape(1, A).astype(F32))
    ys = _s5s(u, _s5s_tables(*s5_params))
    w_out = w_out.astype(BF16)
    return [hm, ys], [w_out[:A], w_out[A:]]


def _nsa_layer(x, g, shift, scale, w_in, b_gate, cmp_pos, cmp_w1, cmp_b1, cmp_w2, cmp_b2, rel_bias, w_out):
    B, S, D = x.shape
    KV, R, DH = NSA_KV, NSA_R, NSA_DH
    w_g = jnp.zeros((D, KV, LANES), F32).at[:, :, :3 * R].set(w_in[:, D + 6 * KV_W:].reshape(D, KV, 3 * R))
    b_g = jnp.zeros((KV, LANES), F32).at[:, :3 * R].set(b_gate.reshape(KV, 3 * R).astype(F32))
    kv_cols = lambda i: w_in[:, D + i * KV_W:D + (i + 1) * KV_W]
    w_k = jnp.concatenate([kv_cols(0), kv_cols(2), kv_cols(4)], axis=1)
    w_v = jnp.concatenate([kv_cols(1), kv_cols(3), kv_cols(5)], axis=1)
    weights = [w_in[:, :D], w_k, w_v, w_g.reshape(D, KV * LANES)]
    q4, gv, kc, vc, ks, kw, vs_t, vw_t = _nsa_proj(x, g, shift, scale, [w.astype(BF16) for w in weights],
                                                   b_g.reshape(1, KV * LANES))
    grp = CMP_STRIDE
    xg = jnp.stack([kc, vc]).reshape(2, B, KV * S // grp, grp * DH)
    cmp = _compress(xg, cmp_pos, cmp_w1, cmp_b1, cmp_w2, cmp_b2).reshape(2, B, KV, S // grp, DH).astype(BF16)
    out = _nsa_t_attention(q4, gv, cmp[0], cmp[1].transpose(0, 1, 3, 2), ks, vs_t, kw, vw_t,
                           _nsa_t_tables(rel_bias, S))
    return [out], [w_out.astype(BF16)]


def kernel(x, c, rel_bias, ada_w, ada_b, norm_g, final_g,
           a_w_in, a_conv, a_b_i, a_b_f, a_head_g,
           s5_lam_re, s5_lam_im, s5_log_dt, s5_b_re, s5_b_im, s5_c_re, s5_c_im,
           s5_d, s5_glu_w, s5_glu_b, a_w_out,
           n_w_in, n_b_gate, n_cmp_pos, n_cmp_w1, n_cmp_b1, n_cmp_w2, n_cmp_b2, n_w_out,
           r_grp_w, r_grp_b, r_exp_w, r_exp_b, e_w_gate, e_w_up, e_w_down):
    B, S, D = x.shape
    mod = _ada_mod(c, ada_w, ada_b).reshape(DEPTH, 2, B, 1, 3 * D)
    split = lambda m: (m[..., :D], m[..., D:2 * D], m[..., 2 * D:])
    for layer in range(DEPTH):
        shift, scale, mix_gate = split(mod[layer, 0])
        j = layer // 2
        if layer % 2 == 0:
            s5_params = (s5_lam_re[j], s5_lam_im[j], s5_log_dt[j], s5_b_re[j], s5_b_im[j],
                         s5_c_re[j], s5_c_im[j], s5_d[j], s5_glu_w[j], s5_glu_b[j])
            acts, w_outs = _mlstm_s5_layer(x, norm_g[layer, 0], shift, scale, a_w_in[j], a_conv[j], a_b_i[j],
                                           a_b_f[j], a_head_g[j], s5_params, a_w_out[j])
        else:
            acts, w_outs = _nsa_layer(x, norm_g[layer, 0], shift, scale, n_w_in[j], n_b_gate[j], n_cmp_pos[j],
                                      n_cmp_w1[j], n_cmp_b1[j], n_cmp_w2[j], n_cmp_b2[j], rel_bias, n_w_out[j])
        shift, scale, gate = split(mod[layer, 1])
        x = _moe(x, mix_gate, acts, w_outs, norm_g[layer, 1], shift, scale, gate, r_grp_w[layer], r_grp_b[layer],
                 r_exp_w[layer], r_exp_b[layer], e_w_gate[layer], e_w_up[layer], e_w_down[layer], final_g,
                 final=(layer == DEPTH - 1))
    return x
```

```python
import functools
import math

import jax
import jax.numpy as jnp
from jax import lax
from jax.experimental import pallas as pl
from jax.experimental.pallas import tpu as pltpu

F32 = jnp.float32
BF16 = jnp.bfloat16
HIGHEST = lax.Precision.HIGHEST

D_MODEL = 1024
DEPTH = 2
MIX_A = 512
MLSTM_HEADS = 4
MLSTM_DH = MIX_A // MLSTM_HEADS
MLSTM_CHUNK = 128
CONV_K = 4
S5_GROUP = 16
S5_STATE = 64
S5_CHUNK = 16
NSA_HEADS = 16
NSA_KV = 4
NSA_R = NSA_HEADS // NSA_KV
NSA_DH = D_MODEL // NSA_HEADS
KV_W = NSA_KV * NSA_DH
CMP_BLOCK = 32
CMP_STRIDE = 16
CMP_HIDDEN = 256
SEL_BLOCK = 64
SEL_TOPK = 16
WINDOW = 512
FORCE = 1e9
REL_BUCKETS = 32
REL_MAX_DIST = 128
MOE_GROUPS = 4
MOE_PER_GROUP = 4
MOE_HIDDEN = 256
EPS = 1e-6
NEG = -1e30
BIG = 1e30
LOG2E = math.log2(math.e)

LANES = 128
SUBLANES = 8
ATT_TILE = 256
MOE_WIN = 160
MOE_ALIGN = 16
VMEM_LIMIT = 56 * 1024 * 1024


def _cparams(sem):
    return pltpu.CompilerParams(dimension_semantics=sem, vmem_limit_bytes=VMEM_LIMIT)


def _dot(a, b, precision=None):
    return jnp.dot(a, b, preferred_element_type=F32, precision=precision)


def _dot_nt(a, b):
    return lax.dot_general(a, b, (((1,), (1,)), ((), ())), preferred_element_type=F32)


def _sigmoid(x):
    return 1.0 / (1.0 + jnp.exp(-x))


def _silu(x):
    return x * _sigmoid(x)


def _gelu_tanh(x):
    return 0.5 * x * (1.0 + jnp.tanh(math.sqrt(2.0 / math.pi) * (x + 0.044715 * (x * x * x))))


def _modulated_norm(x, g, shift, scale):
    y = x * lax.rsqrt(jnp.mean(x * x, axis=-1, keepdims=True) + EPS) * g
    return y * (1.0 + scale) + shift


def _ada_kernel(c_ref, w_ref, b_ref, o_ref):
    c = c_ref[...]
    o_ref[0] = _dot(_silu(c), w_ref[0]) + b_ref[0]


def _ada_mod(c, ada_w, ada_b):
    B, D = c.shape
    n_mod = ada_w.shape[0] * ada_w.shape[1]
    w = ada_w.reshape(n_mod, D, 3 * D)
    b = ada_b.reshape(n_mod, 1, 3 * D)
    tn = 1024
    return pl.pallas_call(
        _ada_kernel,
        out_shape=jax.ShapeDtypeStruct((n_mod, B, 3 * D), F32),
        grid=(n_mod, 3 * D // tn),
        in_specs=[pl.BlockSpec((B, D), lambda i, j: (0, 0)),
                  pl.BlockSpec((1, D, tn), lambda i, j: (i, 0, j)),
                  pl.BlockSpec((1, 1, tn), lambda i, j: (i, 0, j))],
        out_specs=pl.BlockSpec((1, B, tn), lambda i, j: (i, 0, j)),
        compiler_params=_cparams(("parallel", "parallel")),
        name="ada_mod",
    )(c, w, b)


def _norm_mm_kernel(*refs, n_w):
    x_ref, g_ref, sh_ref, sc_ref = refs[:4]
    w_refs = refs[4:4 + n_w]
    o_refs = refs[4 + n_w:]
    h = _modulated_norm(x_ref[0], g_ref[...], sh_ref[0], sc_ref[0]).astype(BF16)
    for w_ref, o_ref in zip(w_refs, o_refs):
        o_ref[0] = _dot(h, w_ref[...]).astype(o_ref.dtype)


def _norm_matmul(x, g, shift, scale, weights, out_dtypes, tm=512):
    B, S, D = x.shape
    n_w = len(weights)
    vec = pl.BlockSpec((1, 1, D), lambda b, i: (b, 0, 0))
    in_specs = [pl.BlockSpec((1, tm, D), lambda b, i: (b, i, 0)),
                pl.BlockSpec((1, D), lambda b, i: (0, 0)), vec, vec]
    in_specs += [pl.BlockSpec(w.shape, lambda b, i: (0, 0)) for w in weights]
    return pl.pallas_call(
        functools.partial(_norm_mm_kernel, n_w=n_w),
        out_shape=[jax.ShapeDtypeStruct((B, S, w.shape[1]), dt) for w, dt in zip(weights, out_dtypes)],
        grid=(B, S // tm),
        in_specs=in_specs,
        out_specs=[pl.BlockSpec((1, tm, w.shape[1]), lambda b, i: (b, i, 0)) for w in weights],
        compiler_params=_cparams(("parallel", "parallel")),
        name="norm_matmul",
    )(x, g.reshape(1, D), shift, scale, *weights)


def _mlstm_kernel(qk_ref, vo_ref, if_ref, cw_ref, gb_ref, hg_ref, tril_ref, o_ref,
                  xbuf, c_scr, n_scr, m_scr):
    pad = SUBLANES

    @pl.when(pl.program_id(1) == 0)
    def _():
        xbuf[:, 0:pad, :] = jnp.zeros((xbuf.shape[0], pad, 2 * MIX_A), F32)
        c_scr[...] = jnp.zeros_like(c_scr)
        n_scr[...] = jnp.zeros_like(n_scr)
        m_scr[...] = jnp.zeros_like(m_scr)

    for bb in range(qk_ref.shape[0]):
        _mlstm_chunk(qk_ref.at[bb], vo_ref.at[bb], if_ref.at[bb], cw_ref, gb_ref, hg_ref, tril_ref, o_ref.at[bb],
                     xbuf.at[bb], c_scr.at[bb], n_scr.at[bb], m_scr.at[bb])


def _mlstm_chunk(qk_ref, vo_ref, if_ref, cw_ref, gb_ref, hg_ref, tril_ref, o_ref, xbuf, c_scr, n_scr, m_scr):
    L, H, DH = MLSTM_CHUNK, MLSTM_HEADS, MLSTM_DH
    pad = SUBLANES
    xbuf[pad:pad + L, :] = qk_ref[...].astype(F32)
    cw = cw_ref[...]
    conv = None
    for j in range(CONV_K):
        lo = pad - (CONV_K - 1) + j
        t = xbuf[lo:lo + L, :] * cw[j:j + 1, :]
        conv = t if conv is None else conv + t
    xbuf[0:pad, :] = xbuf[L:L + pad, :]
    qk = _silu(conv)
    q = qk[:, :MIX_A]
    k = qk[:, MIX_A:] * (DH ** -0.5)
    vo = vo_ref[...].astype(F32)
    v = vo[:, :MIX_A]
    o_pre = vo[:, MIX_A:]

    ifb = if_ref[...] + gb_ref[...]
    lf = jnp.minimum(ifb, 0.0) - jnp.log1p(jnp.exp(-jnp.abs(ifb)))
    bcs = _dot(tril_ref[...], lf, precision=HIGHEST)
    ifb_t = ifb.T
    bcs_t = bcs.T
    row = lax.broadcasted_iota(jnp.int32, (L, L), 0)
    col = lax.broadcasted_iota(jnp.int32, (L, L), 1)
    causal = col <= row

    outs = []
    for h in range(H):
        sl = slice(h * DH, (h + 1) * DH)
        qh, kh, vh = q[:, sl], k[:, sl], v[:, sl]
        qb, kb = qh.astype(BF16), kh.astype(BF16)
        b_col = bcs[:, H + h:H + h + 1]
        b_row = bcs_t[H + h:H + h + 1, :]
        li_col = ifb[:, h:h + 1]
        li_row = ifb_t[h:h + 1, :]
        b_last = b_col[L - 1:L, :]
        m0 = m_scr[h][:, 0:1]
        c0 = c_scr[h]
        n0 = n_scr[h]

        log_d = jnp.where(causal, b_col - b_row + li_row, NEG)
        log_inter = b_col + m0
        m_t = jnp.maximum(log_inter, jnp.max(log_d, axis=1, keepdims=True))
        dmat = jnp.exp(log_d - m_t)
        a_inter = jnp.exp(log_inter - m_t)
        s = _dot_nt(qb, kb) * dmat
        num = _dot(s.astype(BF16), vh.astype(BF16)) + a_inter * _dot_nt(qb, c0.astype(BF16))
        den = jnp.sum(s, axis=1, keepdims=True) + a_inter * jnp.sum(qh * n0, axis=1, keepdims=True)
        hh = num / jnp.maximum(jnp.abs(den), jnp.exp(-m_t))

        w_col = b_last - b_col + li_col
        m_loc = jnp.max(w_col, axis=0, keepdims=True)
        e = jnp.exp(w_col - m_loc)
        c_loc = _dot((vh * e).T.astype(BF16), kb)
        n_loc = jnp.sum(kh * e, axis=0, keepdims=True)
        m_new = jnp.maximum(b_last + m0, m_loc)
        a = jnp.exp(b_last + m0 - m_new)
        sc = jnp.exp(m_loc - m_new)
        c_scr[h] = a * c0 + sc * c_loc
        n_scr[h] = a * n0 + sc * n_loc
        m_scr[h] = jnp.broadcast_to(m_new, (1, LANES))

        outs.append(hh * lax.rsqrt(jnp.mean(hh * hh, axis=1, keepdims=True) + EPS))
    hm = jnp.concatenate(outs, axis=1)
    o_ref[...] = (_sigmoid(o_pre) * (hm * hg_ref[...])).astype(o_ref.dtype)


def _mlstm(qk, vo, ifg, conv_w, gate_bias, head_g):
    B, S, _ = qk.shape
    rows = 1
    L, H, DH = MLSTM_CHUNK, MLSTM_HEADS, MLSTM_DH
    tril = jnp.tril(jnp.ones((L, L), F32))
    return pl.pallas_call(
        _mlstm_kernel,
        out_shape=jax.ShapeDtypeStruct((B, S, MIX_A), BF16),
        grid=(B // rows, S // L),
        in_specs=[pl.BlockSpec((rows, L, 2 * MIX_A), lambda b, c: (b, c, 0)),
                  pl.BlockSpec((rows, L, 2 * MIX_A), lambda b, c: (b, c, 0)),
                  pl.BlockSpec((rows, L, LANES), lambda b, c: (b, c, 0)),
                  pl.BlockSpec((CONV_K, 2 * MIX_A), lambda b, c: (0, 0)),
                  pl.BlockSpec((1, LANES), lambda b, c: (0, 0)),
                  pl.BlockSpec((1, MIX_A), lambda b, c: (0, 0)),
                  pl.BlockSpec((L, L), lambda b, c: (0, 0))],
        out_specs=pl.BlockSpec((rows, L, MIX_A), lambda b, c: (b, c, 0)),
        scratch_shapes=[pltpu.VMEM((rows, L + SUBLANES, 2 * MIX_A), F32),
                        pltpu.VMEM((rows, H, DH, DH), F32),
                        pltpu.VMEM((rows, H, 1, DH), F32),
                        pltpu.VMEM((rows, H, 1, LANES), F32)],
        compiler_params=_cparams(("parallel", "arbitrary")),
        name="mlstm",
    )(qk, vo, ifg, conv_w, gate_bias, head_g, tril)


S5_LT = LANES // S5_GROUP
S5_PAIRS = S5_CHUNK // 2


def _s5s_kernel(u_ref, h_ref, e_ref, kk_ref, are_ref, aim_ref, d_ref, gw_ref, gb_ref, o_ref, xl_scr, x0_scr):
    n_chunks = u_ref.shape[1] // S5_CHUNK
    half = S5_LT * S5_STATE
    tok = lambda s: u_ref[0, pl.ds(s, n_chunks, stride=S5_CHUNK), :]
    u2 = [jnp.concatenate([tok(2 * q), tok(2 * q + 1)], axis=1) for q in range(S5_PAIRS)]
    u2b = [v.astype(BF16) for v in u2]
    xl_scr[...] = functools.reduce(lambda a, b: a + b, [_dot(u2b[q], h_ref[0, q]) for q in range(S5_PAIRS)])
    a_re = are_ref[0]
    a_im = aim_ref[0]

    def body(a, carry):
        re, im = carry
        x0_scr[pl.ds(a, 1), 0:half] = re
        x0_scr[pl.ds(a, 1), half:2 * half] = im
        return (a_re * re - a_im * im + xl_scr[pl.ds(a, 1), 0:half],
                a_re * im + a_im * re + xl_scr[pl.ds(a, 1), half:2 * half])

    zero = jnp.zeros((1, half), F32)
    lax.fori_loop(0, n_chunks, body, (zero, zero), unroll=8)
    x0 = x0_scr[...].astype(BF16)
    for p in range(S5_PAIRS):
        y = _dot(x0, e_ref[0, p]) + u2[p] * d_ref[0]
        for q in range(p + 1):
            y = y + _dot(u2b[q], kk_ref[0, p - q])
        ys = _gelu_tanh(y)
        out = ys * _sigmoid(_dot(ys.astype(BF16), gw_ref[0]) + gb_ref[0])
        o_ref[0, pl.ds(2 * p, n_chunks, stride=S5_CHUNK), :] = out[:, :LANES].astype(o_ref.dtype)
        o_ref[0, pl.ds(2 * p + 1, n_chunks, stride=S5_CHUNK), :] = out[:, LANES:].astype(o_ref.dtype)


def _s5s_tables(lam_re, lam_im, log_dt, b_re, b_im, c_re, c_im, d_skip, glu_w, glu_b):
    T, C, P, LT = S5_CHUNK, S5_GROUP, S5_STATE, S5_LT
    G = lam_re.shape[0]
    NT = G // LT
    lam = lax.complex(lam_re.astype(F32), lam_im.astype(F32))
    dt = jnp.exp(log_dt.astype(F32))[:, None]
    lam_bar = jnp.exp(lam * dt)
    b_bar = ((lam_bar - 1.0) / lam)[..., None] * lax.complex(b_re.astype(F32), b_im.astype(F32))
    c_mat = lax.complex(c_re.astype(F32), c_im.astype(F32))
    taus = jnp.arange(T + 1, dtype=F32)
    pw = jnp.exp((lam * dt)[:, None, :] * taus[None, :, None])
    eye = jnp.eye(LT, dtype=F32)
    tiles = lambda a: a.reshape((NT, LT) + a.shape[1:])

    kern = jnp.einsum('gcp,gtp,gpd->gtdc', c_mat, pw[:, :T], b_bar, precision=HIGHEST).real
    kblk = jnp.einsum('nitdc,ij->ntidjc', tiles(kern), eye).reshape(NT, T, LANES, LANES)
    kblk = jnp.concatenate([jnp.zeros_like(kblk[:, :1]), kblk], axis=1)
    kk = jnp.stack([jnp.concatenate([jnp.concatenate([kblk[:, 2 * d + 1], kblk[:, 2 * d + 2]], axis=2),
                                     jnp.concatenate([kblk[:, 2 * d], kblk[:, 2 * d + 1]], axis=2)], axis=1)
                    for d in range(T // 2)], axis=1)

    hmat = pw[:, :T][:, ::-1, :, None] * b_bar[:, None]

    def state_cols(m):
        return jnp.einsum('nispc,ij->nsicjp', tiles(m), eye).reshape(NT, T, LANES, LT * P)

    h = jnp.concatenate([state_cols(hmat.real), state_cols(hmat.imag)], axis=3)
    h2 = h.reshape(NT, T // 2, 2 * LANES, 2 * LT * P)

    emat = c_mat[:, None] * pw[:, 1:][:, :, None, :]

    def state_rows(m):
        return jnp.einsum('nitcp,ij->ntjpic', tiles(m), eye).reshape(NT, T, LT * P, LANES)

    e = jnp.concatenate([state_rows(emat.real), state_rows(-emat.imag)], axis=2)
    e2 = e.reshape(NT, T // 2, 2, 2 * LT * P, LANES).transpose(0, 1, 3, 2, 4).reshape(NT, T // 2, 2 * LT * P, 2 * LANES)

    a_re = pw[:, T].real.reshape(NT, 1, LT * P)
    a_im = pw[:, T].imag.reshape(NT, 1, LT * P)
    pair = lambda v: jnp.tile(v.astype(F32).reshape(NT, 1, LANES), (1, 1, 2))
    gwb = jnp.einsum('nice,ij->nicje', tiles(glu_w.astype(F32)), eye).reshape(NT, LANES, LANES)
    zeros = jnp.zeros_like(gwb)
    gw2 = jnp.concatenate([jnp.concatenate([gwb, zeros], axis=2), jnp.concatenate([zeros, gwb], axis=2)], axis=1)
    return (h2.astype(BF16), e2.astype(BF16), kk.astype(BF16), a_re, a_im, pair(d_skip), gw2.astype(BF16), pair(glu_b))


def _s5s(u, tables):
    B, S, W = u.shape
    NT = W // LANES
    n_chunks = S // S5_CHUNK
    per_tile = lambda a: pl.BlockSpec((1,) + a.shape[1:], lambda j, b: (j,) + (0,) * (a.ndim - 1))
    return pl.pallas_call(
        _s5s_kernel,
        out_shape=jax.ShapeDtypeStruct((B, S, W), F32),
        grid=(NT, B),
        in_specs=[pl.BlockSpec((1, S, LANES), lambda j, b: (b, 0, j))] + [per_tile(t) for t in tables],
        out_specs=pl.BlockSpec((1, S, LANES), lambda j, b: (b, 0, j)),
        scratch_shapes=[pltpu.VMEM((n_chunks, 2 * S5_LT * S5_STATE), F32) for _ in range(2)],
        compiler_params=_cparams(("parallel", "parallel")),
        name="s5",
    )(u, *tables)


def _compress_kernel(x_ref, plo_ref, phi_ref, w1_ref, b1_ref, w2_ref, b2_ref, o_ref):
    x = x_ref[0, 0]
    half = x.shape[1]
    w1 = w1_ref[0]
    lo = _dot((x + plo_ref[0]).astype(BF16), w1[:half])
    hi = _dot((x + phi_ref[0]).astype(BF16), w1[half:])
    rows = x.shape[0]
    hid = _gelu_tanh(lo + pltpu.roll(hi, rows - 1, 0) + b1_ref[0])
    o_ref[0, 0] = _dot(hid.astype(BF16), w2_ref[0]) + b2_ref[0]


def _compress(xg, pos, w1, b1, w2, b2):
    _, B, rows, width = xg.shape
    pos_flat = pos.reshape(2, 2, 1, width).astype(F32)
    sel = lambda shape: pl.BlockSpec((1,) + shape, lambda j, b: (j, 0, 0))
    return pl.pallas_call(
        _compress_kernel,
        out_shape=jax.ShapeDtypeStruct((2, B, rows, NSA_DH), F32),
        grid=(2, B),
        in_specs=[pl.BlockSpec((1, 1, rows, width), lambda j, b: (j, b, 0, 0)),
                  sel((1, width)), sel((1, width)),
                  sel((2 * width, CMP_HIDDEN)), sel((1, CMP_HIDDEN)),
                  sel((CMP_HIDDEN, NSA_DH)), sel((1, NSA_DH))],
        out_specs=pl.BlockSpec((1, 1, rows, NSA_DH), lambda j, b: (j, b, 0, 0)),
        compiler_params=_cparams(("parallel", "parallel")),
        name="nsa_compress",
    )(xg, pos_flat[:, 0], pos_flat[:, 1], w1.astype(BF16), b1[:, None].astype(F32),
      w2.astype(BF16), b2[:, None].astype(F32))


def _t5_bucket(dist):
    dist = jnp.maximum(dist, 0)
    max_exact = REL_BUCKETS // 2
    log_ratio = jnp.log(jnp.maximum(dist, 1).astype(F32) / max_exact) / math.log(REL_MAX_DIST / max_exact)
    large = jnp.minimum(max_exact + (log_ratio * (REL_BUCKETS - max_exact)).astype(jnp.int32), REL_BUCKETS - 1)
    return jnp.where(dist < max_exact, dist, large)


def _nsa_proj_kernel(x_ref, g_ref, sh_ref, sc_ref, wq_ref, wk_ref, wv_ref, wg_ref, bg_ref,
                     q4_ref, gv_ref, kc_ref, vc_ref, ks_ref, kw_ref, vst_ref, vwt_ref):
    KV, R, DH, T = NSA_KV, NSA_R, NSA_DH, ATT_TILE
    h = _modulated_norm(x_ref[0], g_ref[...], sh_ref[0], sc_ref[0]).astype(BF16)
    q_t = (_dot(h, wq_ref[...]) * (DH ** -0.5 * LOG2E)).T.astype(BF16)
    gates_t = _sigmoid(_dot(h, wg_ref[...]) + bg_ref[...]).T
    row = lax.broadcasted_iota(jnp.int32, (SUBLANES, R * T), 0)
    for g in range(KV):
        q4_ref[0, g, 0] = jnp.concatenate([q_t[(g * R + r) * DH:(g * R + r + 1) * DH] for r in range(R)], axis=1)
        gv = jnp.zeros((SUBLANES, R * T), F32)
        for j in range(3):
            gj = jnp.concatenate([gates_t[g * LANES + 3 * r + j:g * LANES + 3 * r + j + 1] for r in range(R)], axis=1)
            gv = jnp.where(row == j, gj, gv)
        gv_ref[0, g, 0] = gv
    k3 = _dot(h, wk_ref[...])
    v3 = _dot(h, wv_ref[...])
    vs_t = v3[:, KV_W:2 * KV_W].T.astype(BF16)
    vw_t = v3[:, 2 * KV_W:].T.astype(BF16)
    for g in range(KV):
        cols = slice(g * DH, (g + 1) * DH)
        kc_ref[0, g] = k3[:, cols].astype(BF16)
        vc_ref[0, g] = v3[:, cols].astype(BF16)
        ks_ref[0, g] = k3[:, KV_W + g * DH:KV_W + (g + 1) * DH].astype(BF16)
        kw_ref[0, g] = k3[:, 2 * KV_W + g * DH:2 * KV_W + (g + 1) * DH].astype(BF16)
        vst_ref[0, g, 0] = vs_t[cols]
        vwt_ref[0, g, 0] = vw_t[cols]


def _nsa_proj(x, g, shift, scale, weights, b_gate):
    B, S, D = x.shape
    KV, R, DH, T = NSA_KV, NSA_R, NSA_DH, ATT_TILE
    vec = pl.BlockSpec((1, 1, D), lambda b, i: (b, 0, 0))
    keys = pl.BlockSpec((1, KV, T, DH), lambda b, i: (b, 0, i, 0))
    key_shape = jax.ShapeDtypeStruct((B, KV, S, DH), BF16)
    tile = lambda rows, width: pl.BlockSpec((1, KV, 1, rows, width), lambda b, i: (b, 0, i, 0, 0))
    tile_shape = lambda rows, width, dt: jax.ShapeDtypeStruct((B, KV, S // T, rows, width), dt)
    return pl.pallas_call(
        _nsa_proj_kernel,
        out_shape=[tile_shape(DH, R * T, BF16), tile_shape(SUBLANES, R * T, F32),
                   key_shape, key_shape, key_shape, key_shape,
                   tile_shape(DH, T, BF16), tile_shape(DH, T, BF16)],
        grid=(B, S // T),
        in_specs=[pl.BlockSpec((1, T, D), lambda b, i: (b, i, 0)),
                  pl.BlockSpec((1, D), lambda b, i: (0, 0)), vec, vec]
                 + [pl.BlockSpec(w.shape, lambda b, i: (0, 0)) for w in weights]
                 + [pl.BlockSpec(b_gate.shape, lambda b, i: (0, 0))],
        out_specs=[tile(DH, R * T), tile(SUBLANES, R * T), keys, keys, keys, keys, tile(DH, T), tile(DH, T)],
        compiler_params=_cparams(("parallel", "parallel")),
        name="nsa_proj",
    )(x, g.reshape(1, D), shift, scale, *weights, b_gate)


def _nsa_t_kernel(q4_ref, gv_ref, kc_ref, vct_ref, ks_ref, vst_ref, kw_ref, vwt_ref,
                  cfar_ref, band_ref, selb_ref, winb_ref, ovt_ref, o_ref, s_scr, sel_scr, sbuf):
    T = ATT_TILE
    R, DH = NSA_R, NSA_DH
    qi = pl.program_id(2)
    q0 = qi * T
    n_pad = kc_ref.shape[2]
    n_sel = ovt_ref.shape[0]
    n_far = selb_ref.shape[0] - 1
    n_win = winb_ref.shape[0] - 2
    band_rows = band_ref.shape[2] - T // CMP_STRIDE * 2

    q4 = q4_ref[0, 0, 0]
    t_lane = q0 + lax.broadcasted_iota(jnp.int32, (1, R * T), 1) % T

    ones_rows = DH
    with_ones = lambda v_t: jnp.concatenate([v_t, jnp.ones((ones_rows, v_t.shape[1]), v_t.dtype)], axis=0)
    gvec = lambda j: gv_ref[0, 0, 0, j:j + 1, :]

    grp = T // CMP_STRIDE
    s_scr[0:n_pad, :] = _dot(kc_ref[0, 0], q4) + cfar_ref[0]
    s_scr[n_pad:n_pad + 2 * grp, :] = jnp.zeros((2 * grp, R * T), F32)
    r0 = jnp.maximum(qi * grp - 2 * grp, 0)
    x0 = r0 - (qi * grp - 2 * grp)
    r0 = pl.multiple_of(r0, SUBLANES)
    x0 = pl.multiple_of(x0, SUBLANES)
    s_scr[pl.ds(r0, band_rows), :] += band_ref[0, 0, pl.ds(x0, band_rows), :]
    lim = pl.multiple_of(qi * grp + 2 * grp, SUBLANES)
    s_scr[pl.ds(lim, n_pad), :] = jnp.full((n_pad, R * T), NEG, F32)

    w_subs, w_vals = [], []
    for d in range(n_win + 1):
        kt = jnp.maximum(qi - d, 0)
        off = pl.multiple_of(kt * T, T)
        tile = jnp.where(qi >= d, d, n_win + 1)
        w_subs.append((_dot(kw_ref[0, 0, pl.ds(off, T), :], q4) + winb_ref[tile, 0]).astype(BF16))
        w_vals.append(with_ones(vwt_ref[0, 0, kt]))

    s = s_scr[0:n_pad, :]
    e = jnp.exp2(s - jnp.max(s, axis=0, keepdims=True))
    inv = jnp.where(t_lane >= CMP_BLOCK - 1, 1.0 / jnp.sum(e, axis=0, keepdims=True), 0.0)
    p = e * inv
    o_cmp = _dot(vct_ref[0, 0], p.astype(BF16))
    psum = functools.reduce(lambda a, b: a + b, [p[:, r * T:(r + 1) * T] for r in range(R)])

    m_w = jnp.max(functools.reduce(jnp.maximum, w_subs), axis=0, keepdims=True)
    acc = functools.reduce(lambda a, b: a + b,
                           [_dot(vj, jnp.exp2(sj - m_w)) for sj, vj in zip(w_subs, w_vals)])
    o_win = acc[:DH] * (1.0 / acc[DH:DH + 1])
    out_t = gvec(0) * o_cmp + gvec(2) * o_win

    imp_t = _dot(ovt_ref[...], psum, precision=HIGHEST)
    jj = lax.broadcasted_iota(jnp.int32, (n_sel, T), 0)
    blk_t = (q0 + lax.broadcasted_iota(jnp.int32, (1, T), 1)) // SEL_BLOCK
    forced = (jj == 0) | (jj == blk_t) | (jj == blk_t - 1)
    score = jnp.where(forced, FORCE, jnp.where(jj <= blk_t, imp_t, -1.0))
    n_blk = n_sel // SUBLANES
    rows = [score[v * SUBLANES:(v + 1) * SUBLANES] for v in range(n_blk)]
    cnts = [jnp.zeros((SUBLANES, T), F32) for _ in range(n_blk)]
    sub = lax.broadcasted_iota(jnp.int32, (SUBLANES, T), 0)
    for j2 in range(n_sel):
        c2 = score[j2:j2 + 1, :]
        for v in range(n_blk):
            lo = v * SUBLANES
            if lo > j2:
                beats = c2 >= rows[v]
            elif lo + SUBLANES - 1 <= j2:
                beats = c2 > rows[v]
            else:
                beats = (c2 > rows[v]) | ((c2 >= rows[v]) & (sub > j2 - lo))
            cnts[v] = cnts[v] + jnp.where(beats, 1.0, 0.0)
    cnt = jnp.concatenate(cnts, axis=0)
    chosen = (cnt < float(min(SEL_TOPK, n_sel))) & (jj <= blk_t)
    sel_scr[...] = jnp.where(chosen, 0.0, -BIG)

    def block_mask(kt):
        per_tile = T // SEL_BLOCK
        parts = [jnp.broadcast_to(sel_scr[pl.ds(kt * per_tile + i, 1), :], (SEL_BLOCK, T)) for i in range(per_tile)]
        m1 = jnp.concatenate(parts, axis=0)
        return jnp.concatenate([m1] * R, axis=1)

    def sel_scores(slot, kc):
        off = pl.multiple_of(kc * T, T)
        s = _dot(ks_ref[0, 0, pl.ds(off, T), :], q4)
        s = (s + selb_ref[jnp.clip(qi - kc, 0, n_far), 0] + block_mask(kc)).astype(BF16)
        sbuf[slot] = s
        return jnp.max(s, axis=0, keepdims=True).astype(F32)

    def sel_weighted(slot, kc, m_new):
        return _dot(with_ones(vst_ref[0, 0, kc]), jnp.exp2(sbuf[slot] - m_new.astype(BF16)))

    last_tile = vst_ref.shape[2] - 1

    def sel_body(i, carry):
        m, acc, m_even = carry
        m_odd = sel_scores(1, 2 * i + 1)
        m_new = jnp.maximum(m, m_even)
        acc = jnp.exp2(m - m_new) * acc + sel_weighted(0, 2 * i, m_new)
        m_even = sel_scores(0, jnp.minimum(2 * i + 2, last_tile))
        m_fin = jnp.maximum(m_new, m_odd)
        acc = jnp.exp2(m_new - m_fin) * acc + sel_weighted(1, 2 * i + 1, m_fin)
        return m_fin, acc, m_even

    _, acc, _ = lax.fori_loop(0, qi // 2 + 1, sel_body,
                              (jnp.full((1, R * T), NEG, F32), jnp.zeros((DH + ones_rows, R * T), F32),
                               sel_scores(0, 0)))
    out_t = out_t + gvec(1) * (acc[:DH] * (1.0 / acc[DH:DH + 1]))
    for pr in range(R // 2):
        pair = jnp.concatenate([out_t[:, (2 * pr) * T:(2 * pr + 1) * T],
                                out_t[:, (2 * pr + 1) * T:(2 * pr + 2) * T]], axis=0)
        o_ref[0, :, pr * 2 * DH:(pr + 1) * 2 * DH] = pair.T.astype(o_ref.dtype)


def _bias_lookup(table, dist):
    idx = _t5_bucket(dist)
    out = jnp.zeros(idx.shape + (table.shape[1],), F32)
    for k in range(table.shape[0]):
        out = out + jnp.where((idx == k)[..., None], table[k], 0.0)
    return out


def _nsa_t_tables(rel_bias, S):
    T, R, KV = ATT_TILE, NSA_R, NSA_KV
    table = rel_bias.astype(F32) * LOG2E
    ii = jnp.arange(T)
    delta = ii[None, :] - ii[:, None]

    def lanes(a):
        a = jnp.moveaxis(a, -1, 0)
        a = a.reshape((KV, R) + a.shape[1:])
        return jnp.moveaxis(a, 1, 2).reshape(KV, a.shape[2], R * a.shape[3])

    def tile(off):
        return lanes(_bias_lookup(table, off * T + delta))

    mask4 = lambda ok: jnp.tile(jnp.where(ok, 0.0, NEG), (1, R))[None]
    n_far = -(-REL_MAX_DIST // T) + 1
    selb = [tile(o) for o in range(n_far + 1)]
    selb[0] = selb[0] + mask4(delta >= 0)
    selb = jnp.stack(selb, axis=0)
    n_win = WINDOW // T
    winb = [tile(o) + mask4((o * T + delta >= 0) & (o * T + delta < WINDOW)) for o in range(n_win + 1)]
    winb.append(jnp.full_like(winb[0], NEG))
    winb = jnp.stack(winb, axis=0)

    grp = T // CMP_STRIDE
    far = _bias_lookup(table, jnp.asarray(2 * REL_MAX_DIST))
    xx = jnp.arange(4 * grp)
    bdist = ii[None, :] - CMP_STRIDE * (xx[:, None] - 2 * grp) - (CMP_BLOCK - 1)
    band = jnp.where((bdist >= 0)[..., None], _bias_lookup(table, bdist) - far, NEG)
    band = jnp.concatenate([lanes(band), jnp.zeros((KV, 2 * grp, R * T), F32)], axis=1)[:, None]
    cfar = jnp.repeat(far.reshape(KV, R), T, axis=1)[:, None]

    n_pad = S // CMP_STRIDE
    n_sel = S // SEL_BLOCK
    cmp_start = jnp.arange(n_pad) * CMP_STRIDE
    sel_start = jnp.arange(n_sel) * SEL_BLOCK
    overlap = jnp.clip(jnp.minimum(cmp_start[:, None] + CMP_BLOCK, sel_start[None] + SEL_BLOCK)
                       - jnp.maximum(cmp_start[:, None], sel_start[None]), 0).astype(F32) / CMP_BLOCK
    n_cmp = (S - CMP_BLOCK) // CMP_STRIDE + 1
    overlap_t = jnp.where((jnp.arange(n_pad) < n_cmp)[:, None], overlap, 0.0).T
    return cfar, band, selb, winb, overlap_t


def _nsa_t_attention(q4, gv, kcmp, vcmp_t, ks, vs_t, kw, vw_t, tables):
    B, KV, S, _ = kw.shape
    T = ATT_TILE
    cfar, band, selb, winb, overlap_t = tables
    gw = NSA_R * NSA_DH
    n_pad = kcmp.shape[2]
    seq = lambda a: pl.BlockSpec((1, 1) + a.shape[2:], lambda b, g, i: (b, g) + (0,) * (a.ndim - 2))
    qtile = lambda a: pl.BlockSpec((1, 1, 1) + a.shape[3:], lambda b, g, i: (b, g, i, 0, 0))
    grp = lambda a: pl.BlockSpec((1,) + a.shape[1:], lambda b, g, i: (g,) + (0,) * (a.ndim - 1))
    tiles = lambda a: pl.BlockSpec((a.shape[0], 1) + a.shape[2:], lambda b, g, i: (0, g, 0, 0))
    full = lambda a: pl.BlockSpec(a.shape, lambda b, g, i: (0,) * a.ndim)
    return pl.pallas_call(
        _nsa_t_kernel,
        out_shape=jax.ShapeDtypeStruct((B, S, KV * gw), BF16),
        grid=(B, KV, S // T),
        in_specs=[qtile(q4), qtile(gv),
                  seq(kcmp), seq(vcmp_t), seq(ks), seq(vs_t), seq(kw), seq(vw_t),
                  grp(cfar), grp(band), tiles(selb), tiles(winb), full(overlap_t)],
        out_specs=pl.BlockSpec((1, T, gw), lambda b, g, i: (b, i, g)),
        scratch_shapes=[pltpu.VMEM((2 * n_pad + 2 * (T // CMP_STRIDE), NSA_R * T), F32),
                        pltpu.VMEM((S // SEL_BLOCK, T), F32),
                        pltpu.VMEM((2, T, NSA_R * T), BF16)],
        compiler_params=_cparams(("parallel", "parallel", "arbitrary")),
        name="nsa_attention",
    )(q4, gv, kcmp, vcmp_t, ks, vs_t, kw, vw_t, cfar, band, selb, winb, overlap_t)


def _moe_kernel(*refs, n_in, final):
    x_ref, mgate_ref = refs[:2]
    a_refs = refs[2:2 + n_in]
    wo_refs = refs[2 + n_in:2 + 2 * n_in]
    (g_ref, sh_ref, sc_ref, gate_ref, wr_ref, br_ref, before_ref, wg_ref, wu_ref, wd_ref, fg_ref,
     o_ref, x_scr, hs_scr, rts_scr, acc_scr, perm_t_scr, meta) = refs[2 + 2 * n_in:]
    NG, PG, FH = MOE_GROUPS, MOE_PER_GROUP, MOE_HIDDEN
    c = pl.program_id(2)

    @pl.when(c == 0)
    def _():
        mix = functools.reduce(lambda a, b: a + b,
                               [_dot(a_ref[0].astype(BF16), wo_ref[...]) for a_ref, wo_ref in zip(a_refs, wo_refs)])
        x = x_ref[0] + mgate_ref[0] * mix
        x_scr[...] = x
        h = _modulated_norm(x, g_ref[...], sh_ref[0], sc_ref[0])
        h_hi = h.astype(BF16)
        h_lo = (h - h_hi.astype(F32)).astype(BF16)
        logits = (_dot(h_hi, wr_ref[0]) + _dot(h_lo, wr_ref[0]) + _dot(h_hi, wr_ref[1]) + br_ref[...]).T
        gl = [logits[NG * PG + g:NG * PG + g + 1, :] for g in range(NG)]
        gmax = functools.reduce(jnp.maximum, gl)
        gtop = jnp.full_like(gmax, float(NG - 1))
        for g in reversed(range(NG - 1)):
            gtop = jnp.where(gl[g] == gmax, float(g), gtop)
        p_g = 1.0 / functools.reduce(lambda a, b: a + b, [jnp.exp(v - gmax) for v in gl])
        a = []
        for j in range(PG):
            v = logits[(NG - 1) * PG + j:(NG - 1) * PG + j + 1, :]
            for g in reversed(range(NG - 1)):
                v = jnp.where(gtop == float(g), logits[g * PG + j:g * PG + j + 1, :], v)
            a.append(v)

        def first_max(vals):
            vmax = functools.reduce(jnp.maximum, vals)
            taken = jnp.zeros_like(vmax) > 1.0
            hits = []
            for v in vals:
                hit = (v == vmax) & jnp.logical_not(taken)
                taken = taken | hit
                hits.append(hit)
            return vmax, hits

        v1, hit1 = first_max(a)
        rest = [jnp.where(hh, -jnp.inf, v) for hh, v in zip(hit1, a)]
        v2, hit2 = first_max(rest)
        e2 = jnp.exp(v2 - v1)
        w1 = p_g / (1.0 + e2)
        w2 = p_g * e2 / (1.0 + e2)
        tm = gtop.shape[1]
        row = lax.broadcasted_iota(jnp.int32, (SUBLANES, tm), 0)
        onehot = [jnp.where(gtop == float(g), 1.0, 0.0) for g in range(NG)]
        oh8 = jnp.zeros((SUBLANES, tm), F32)
        for g in range(NG):
            oh8 = jnp.where(row == g, onehot[g], oh8)
        before = _dot(oh8.astype(BF16), before_ref[...])
        pos = jnp.zeros_like(gtop)
        off = jnp.int32(0)
        for g in range(NG):
            cnt = jnp.sum(onehot[g]).astype(jnp.int32)
            meta[g] = off
            meta[NG + g] = cnt
            pos = pos + onehot[g] * (before[g:g + 1, :] + off.astype(F32))
            off = off + cnt
        rt = jnp.where(row == PG, gtop, jnp.where(row == PG + 1, pos, 0.0))
        for j in range(PG):
            wj = jnp.where(hit1[j], w1, jnp.where(hit2[j], w2, 0.0))
            rt = jnp.where(row == j, wj, rt)
        rt_tok = jnp.concatenate([rt, jnp.zeros((LANES - SUBLANES, tm), F32)], axis=0).T
        rid = lax.broadcasted_iota(jnp.int32, (tm, tm), 0).astype(F32)
        cid = lax.broadcasted_iota(jnp.int32, (tm, tm), 1).astype(F32)
        perm = jnp.where(rid == pos, 1.0, 0.0).astype(BF16)
        perm_t_scr[...] = jnp.where(rt_tok[:, PG + 1:PG + 2] == cid, 1.0, 0.0).astype(BF16)
        r1 = rt_tok.astype(BF16)
        res = rt_tok - r1.astype(F32)
        r2 = res.astype(BF16)
        r3 = (res - r2.astype(F32)).astype(BF16)
        moved = _dot(perm, jnp.concatenate([h_hi, r1, r2, r3], axis=1))
        d = h_hi.shape[1]
        pad = hs_scr.shape[0] - tm
        hs_scr[0:tm, :] = moved[:, :d].astype(BF16)
        hs_scr[tm:, :] = jnp.zeros((pad, d), BF16)
        rts_scr[0:tm, :] = moved[:, d:d + LANES] + moved[:, d + LANES:d + 2 * LANES] + moved[:, d + 2 * LANES:]
        rts_scr[tm:, :] = jnp.full((pad, LANES), -1.0, F32)
        acc_scr[...] = jnp.zeros_like(acc_scr)

    WIN = hs_scr.shape[0] - x_scr.shape[0]
    off = meta[c]
    cnt = meta[NG + c]
    base = (off // MOE_ALIGN) * MOE_ALIGN
    n_win = jnp.where(cnt > 0, (off + cnt - base + WIN - 1) // WIN, 0)
    cf = c.astype(F32)

    def win_body(w, carry):
        start = pl.multiple_of(base + w * WIN, MOE_ALIGN)
        hs = hs_scr[pl.ds(start, WIN), :]
        rt = rts_scr[pl.ds(start, WIN), :]
        in_group = rt[:, PG:PG + 1] == cf
        hid = _silu(_dot(hs, wg_ref[0])) * _dot(hs, wu_ref[0])
        parts = [hid[:, j * FH:(j + 1) * FH] * jnp.where(in_group, rt[:, j:j + 1], 0.0) for j in range(PG)]
        acc_scr[pl.ds(start, WIN), :] += _dot(jnp.concatenate(parts, axis=1).astype(BF16), wd_ref[0])
        return carry

    lax.fori_loop(0, n_win, win_body, 0)

    @pl.when(c == NG - 1)
    def _():
        tm = x_scr.shape[0]
        ys = acc_scr[0:tm, :]
        ys_hi = ys.astype(BF16)
        ys_lo = (ys - ys_hi.astype(F32)).astype(BF16)
        back = _dot(perm_t_scr[...], jnp.concatenate([ys_hi, ys_lo], axis=1))
        d = ys.shape[1]
        y = x_scr[...] + gate_ref[0] * (back[:, :d] + back[:, d:])
        if final:
            y = y * lax.rsqrt(jnp.mean(y * y, axis=-1, keepdims=True) + EPS) * fg_ref[...]
        o_ref[0] = y


def _moe(x, mix_gate, acts, w_outs, g, shift, scale, gate, wg, bg, we, be, w_gate, w_up, w_down, final_g, final,
         tm=512):
    B, S, D = x.shape
    n_in = len(acts)
    NG, PG, FH = MOE_GROUPS, MOE_PER_GROUP, MOE_HIDDEN
    wr = jnp.zeros((D, LANES), F32)
    wr = wr.at[:, :NG * PG].set(we.reshape(D, NG * PG).astype(F32)).at[:, NG * PG:NG * PG + NG].set(wg.astype(F32))
    br = jnp.zeros((1, LANES), F32)
    br = br.at[0, :NG * PG].set(be.reshape(NG * PG).astype(F32)).at[0, NG * PG:NG * PG + NG].set(bg.astype(F32))
    wr_hi = wr.astype(BF16)
    wr = jnp.stack([wr_hi, (wr - wr_hi.astype(F32)).astype(BF16)])
    grp = lambda w: w.reshape(NG, PG, D, FH).transpose(0, 2, 1, 3).reshape(NG, D, PG * FH).astype(BF16)
    wd = w_down.reshape(NG, PG * FH, D).astype(BF16)
    ids = jnp.arange(tm)
    before = (ids[:, None] < ids[None, :]).astype(BF16)
    vec = pl.BlockSpec((1, 1, D), lambda b, i, c: (b, 0, 0))
    row = pl.BlockSpec((1, D), lambda b, i, c: (0, 0))
    wspec = lambda k, n: pl.BlockSpec((1, k, n), lambda b, i, c: (c, 0, 0))
    tokens = lambda n: pl.BlockSpec((1, tm, n), lambda b, i, c: (b, i, 0))
    return pl.pallas_call(
        functools.partial(_moe_kernel, n_in=n_in, final=final),
        out_shape=jax.ShapeDtypeStruct((B, S, D), F32),
        grid=(B, S // tm, NG),
        in_specs=[tokens(D), vec] + [tokens(a.shape[2]) for a in acts]
                 + [pl.BlockSpec(w.shape, lambda b, i, c: (0, 0)) for w in w_outs]
                 + [row, vec, vec, vec,
                    pl.BlockSpec((2, D, LANES), lambda b, i, c: (0, 0, 0)),
                    pl.BlockSpec((1, LANES), lambda b, i, c: (0, 0)),
                    pl.BlockSpec((tm, tm), lambda b, i, c: (0, 0)),
                    wspec(D, PG * FH), wspec(D, PG * FH), wspec(PG * FH, D), row],
        out_specs=tokens(D),
        scratch_shapes=[pltpu.VMEM((tm, D), F32), pltpu.VMEM((tm + MOE_WIN, D), BF16),
                        pltpu.VMEM((tm + MOE_WIN, LANES), F32), pltpu.VMEM((tm + MOE_WIN, D), F32),
                        pltpu.VMEM((tm, tm), BF16), pltpu.SMEM((2 * NG,), jnp.int32)],
        compiler_params=_cparams(("parallel", "parallel", "arbitrary")),
        name="moe",
    )(x, mix_gate, *acts, *w_outs, g.reshape(1, D), shift, scale, gate, wr, br, before, grp(w_gate), grp(w_up), wd,
      final_g.reshape(1, D))


def _mlstm_s5_layer(x, g, shift, scale, w_in, conv_w, b_i, b_f, head_g, s5_params, w_out):
    H = MLSTM_HEADS
    A = MIX_A
    w_if = jnp.zeros((D_MODEL, LANES), F32).at[:, :2 * H].set(w_in[:, 4 * A:4 * A + 2 * H])
    weights = [w_in[:, :2 * A], w_in[:, 2 * A:4 * A], w_if, w_in[:, 4 * A + 2 * H:]]
    qk, vo, ifg, u = _norm_matmul(x, g, shift, scale, [w.astype(BF16) for w in weights], [BF16, BF16, F32, F32])
    gate_bias = jnp.zeros((1, LANES), F32).at[0, :H].set(b_i.astype(F32)).at[0, H:2 * H].set(b_f.astype(F32))
    hm = _mlstm(qk, vo, ifg, conv_w.astype(F32), gate_bias, head_g.reshape(1, A).astype(F32))
    ys = _s5s(u, _s5s_tables(*s5_params))
    w_out = w_out.astype(BF16)
    return [hm, ys], [w_out[:A], w_out[A:]]


def _nsa_layer(x, g, shift, scale, w_in, b_gate, cmp_pos, cmp_w1, cmp_b1, cmp_w2, cmp_b2, rel_bias, w_out):
    B, S, D = x.shape
    KV, R, DH = NSA_KV, NSA_R, NSA_DH
    w_g = jnp.zeros((D, KV, LANES), F32).at[:, :, :3 * R].set(w_in[:, D + 6 * KV_W:].reshape(D, KV, 3 * R))
    b_g = jnp.zeros((KV, LANES), F32).at[:, :3 * R].set(b_gate.reshape(KV, 3 * R).astype(F32))
    kv_cols = lambda i: w_in[:, D + i * KV_W:D + (i + 1) * KV_W]
    w_k = jnp.concatenate([kv_cols(0), kv_cols(2), kv_cols(4)], axis=1)
    w_v = jnp.concatenate([kv_cols(1), kv_cols(3), kv_cols(5)], axis=1)
    weights = [w_in[:, :D], w_k, w_v, w_g.reshape(D, KV * LANES)]
    q4, gv, kc, vc, ks, kw, vs_t, vw_t = _nsa_proj(x, g, shift, scale, [w.astype(BF16) for w in weights],
                                                   b_g.reshape(1, KV * LANES))
    grp = CMP_STRIDE
    xg = jnp.stack([kc, vc]).reshape(2, B, KV * S // grp, grp * DH)
    cmp = _compress(xg, cmp_pos, cmp_w1, cmp_b1, cmp_w2, cmp_b2).reshape(2, B, KV, S // grp, DH).astype(BF16)
    out = _nsa_t_attention(q4, gv, cmp[0], cmp[1].transpose(0, 1, 3, 2), ks, vs_t, kw, vw_t,
                           _nsa_t_tables(rel_bias, S))
    return [out], [w_out.astype(BF16)]


def kernel(x, c, rel_bias, ada_w, ada_b, norm_g, final_g,
           a_w_in, a_conv, a_b_i, a_b_f, a_head_g,
           s5_lam_re, s5_lam_im, s5_log_dt, s5_b_re, s5_b_im, s5_c_re, s5_c_im,
           s5_d, s5_glu_w, s5_glu_b, a_w_out,
           n_w_in, n_b_gate, n_cmp_pos, n_cmp_w1, n_cmp_b1, n_cmp_w2, n_cmp_b2, n_w_out,
           r_grp_w, r_grp_b, r_exp_w, r_exp_b, e_w_gate, e_w_up, e_w_down):
    B, S, D = x.shape
    mod = _ada_mod(c, ada_w, ada_b).reshape(DEPTH, 2, B, 1, 3 * D)
    split = lambda m: (m[..., :D], m[..., D:2 * D], m[..., 2 * D:])
    for layer in range(DEPTH):
        shift, scale, mix_gate = split(mod[layer, 0])
        j = layer // 2
        if layer % 2 == 0:
            s5_params = (s5_lam_re[j], s5_lam_im[j], s5_log_dt[j], s5_b_re[j], s5_b_im[j],
                         s5_c_re[j], s5_c_im[j], s5_d[j], s5_glu_w[j], s5_glu_b[j])
            acts, w_outs = _mlstm_s5_layer(x, norm_g[layer, 0], shift, scale, a_w_in[j], a_conv[j], a_b_i[j],
                                           a_b_f[j], a_head_g[j], s5_params, a_w_out[j])
        else:
            acts, w_outs = _nsa_layer(x, norm_g[layer, 0], shift, scale, n_w_in[j], n_b_gate[j], n_cmp_pos[j],
                                      n_cmp_w1[j], n_cmp_b1[j], n_cmp_w2[j], n_cmp_b2[j], rel_bias, n_w_out[j])
        shift, scale, gate = split(mod[layer, 1])
        x = _moe(x, mix_gate, acts, w_outs, norm_g[layer, 1], shift, scale, gate, r_grp_w[layer], r_grp_b[layer],
                 r_exp_w[layer], r_exp_b[layer], e_w_gate[layer], e_w_up[layer], e_w_down[layer], final_g,
                 final=(layer == DEPTH - 1))
    return x
```

```python
import functools
import math

import jax
import jax.numpy as jnp
from jax import lax
from jax.experimental import pallas as pl
from jax.experimental.pallas import tpu as pltpu

F32 = jnp.float32
BF16 = jnp.bfloat16
HIGHEST = lax.Precision.HIGHEST

D_MODEL = 1024
DEPTH = 2
MIX_A = 512
MLSTM_HEADS = 4
MLSTM_DH = MIX_A // MLSTM_HEADS
MLSTM_CHUNK = 128
CONV_K = 4
S5_GROUP = 16
S5_STATE = 64
S5_CHUNK = 16
NSA_HEADS = 16
NSA_KV = 4
NSA_R = NSA_HEADS // NSA_KV
NSA_DH = D_MODEL // NSA_HEADS
KV_W = NSA_KV * NSA_DH
CMP_BLOCK = 32
CMP_STRIDE = 16
CMP_HIDDEN = 256
SEL_BLOCK = 64
SEL_TOPK = 16
WINDOW = 512
FORCE = 1e9
REL_BUCKETS = 32
REL_MAX_DIST = 128
MOE_GROUPS = 4
MOE_PER_GROUP = 4
MOE_HIDDEN = 256
EPS = 1e-6
NEG = -1e30
BIG = 1e30
LOG2E = math.log2(math.e)

LANES = 128
SUBLANES = 8
ATT_TILE = 256
MOE_WIN = 160
MOE_ALIGN = 16
MOE_TILES = 2
VMEM_LIMIT = 56 * 1024 * 1024


def _cparams(sem):
    return pltpu.CompilerParams(dimension_semantics=sem, vmem_limit_bytes=VMEM_LIMIT)


def _dot(a, b, precision=None):
    return jnp.dot(a, b, preferred_element_type=F32, precision=precision)


def _dot_nt(a, b):
    return lax.dot_general(a, b, (((1,), (1,)), ((), ())), preferred_element_type=F32)


def _sigmoid(x):
    return 1.0 / (1.0 + jnp.exp(-x))


def _silu(x):
    return x * _sigmoid(x)


def _gelu_tanh(x):
    return 0.5 * x * (1.0 + jnp.tanh(math.sqrt(2.0 / math.pi) * (x + 0.044715 * (x * x * x))))


def _modulated_norm(x, g, shift, scale):
    y = x * lax.rsqrt(jnp.mean(x * x, axis=-1, keepdims=True) + EPS) * g
    return y * (1.0 + scale) + shift


def _ada_kernel(c_ref, w_ref, b_ref, o_ref):
    c = c_ref[...]
    o_ref[0] = _dot(_silu(c), w_ref[0]) + b_ref[0]


def _ada_mod(c, ada_w, ada_b):
    B, D = c.shape
    n_mod = ada_w.shape[0] * ada_w.shape[1]
    w = ada_w.reshape(n_mod, D, 3 * D)
    b = ada_b.reshape(n_mod, 1, 3 * D)
    tn = 1024
    return pl.pallas_call(
        _ada_kernel,
        out_shape=jax.ShapeDtypeStruct((n_mod, B, 3 * D), F32),
        grid=(n_mod, 3 * D // tn),
        in_specs=[pl.BlockSpec((B, D), lambda i, j: (0, 0)),
                  pl.BlockSpec((1, D, tn), lambda i, j: (i, 0, j)),
                  pl.BlockSpec((1, 1, tn), lambda i, j: (i, 0, j))],
        out_specs=pl.BlockSpec((1, B, tn), lambda i, j: (i, 0, j)),
        compiler_params=_cparams(("parallel", "parallel")),
        name="ada_mod",
    )(c, w, b)


def _norm_mm_kernel(*refs, n_w):
    x_ref, g_ref, sh_ref, sc_ref = refs[:4]
    w_refs = refs[4:4 + n_w]
    o_refs = refs[4 + n_w:]
    h = _modulated_norm(x_ref[0], g_ref[...], sh_ref[0], sc_ref[0]).astype(BF16)
    for w_ref, o_ref in zip(w_refs, o_refs):
        o_ref[0] = _dot(h, w_ref[...]).astype(o_ref.dtype)


def _norm_matmul(x, g, shift, scale, weights, out_dtypes, tm=512):
    B, S, D = x.shape
    n_w = len(weights)
    vec = pl.BlockSpec((1, 1, D), lambda b, i: (b, 0, 0))
    in_specs = [pl.BlockSpec((1, tm, D), lambda b, i: (b, i, 0)),
                pl.BlockSpec((1, D), lambda b, i: (0, 0)), vec, vec]
    in_specs += [pl.BlockSpec(w.shape, lambda b, i: (0, 0)) for w in weights]
    return pl.pallas_call(
        functools.partial(_norm_mm_kernel, n_w=n_w),
        out_shape=[jax.ShapeDtypeStruct((B, S, w.shape[1]), dt) for w, dt in zip(weights, out_dtypes)],
        grid=(B, S // tm),
        in_specs=in_specs,
        out_specs=[pl.BlockSpec((1, tm, w.shape[1]), lambda b, i: (b, i, 0)) for w in weights],
        compiler_params=_cparams(("parallel", "parallel")),
        name="norm_matmul",
    )(x, g.reshape(1, D), shift, scale, *weights)


def _mlstm_kernel(qk_ref, vo_ref, if_ref, cw_ref, gb_ref, hg_ref, tril_ref, o_ref,
                  xbuf, c_scr, n_scr, m_scr):
    pad = SUBLANES

    @pl.when(pl.program_id(1) == 0)
    def _():
        xbuf[:, 0:pad, :] = jnp.zeros((xbuf.shape[0], pad, 2 * MIX_A), F32)
        c_scr[...] = jnp.zeros_like(c_scr)
        n_scr[...] = jnp.zeros_like(n_scr)
        m_scr[...] = jnp.zeros_like(m_scr)

    for bb in range(qk_ref.shape[0]):
        _mlstm_chunk(qk_ref.at[bb], vo_ref.at[bb], if_ref.at[bb], cw_ref, gb_ref, hg_ref, tril_ref, o_ref.at[bb],
                     xbuf.at[bb], c_scr.at[bb], n_scr.at[bb], m_scr.at[bb])


def _mlstm_chunk(qk_ref, vo_ref, if_ref, cw_ref, gb_ref, hg_ref, tril_ref, o_ref, xbuf, c_scr, n_scr, m_scr):
    L, H, DH = MLSTM_CHUNK, MLSTM_HEADS, MLSTM_DH
    pad = SUBLANES
    xbuf[pad:pad + L, :] = qk_ref[...].astype(F32)
    cw = cw_ref[...]
    conv = None
    for j in range(CONV_K):
        lo = pad - (CONV_K - 1) + j
        t = xbuf[lo:lo + L, :] * cw[j:j + 1, :]
        conv = t if conv is None else conv + t
    xbuf[0:pad, :] = xbuf[L:L + pad, :]
    qk = _silu(conv)
    q = qk[:, :MIX_A]
    k = qk[:, MIX_A:] * (DH ** -0.5)
    vo = vo_ref[...].astype(F32)
    v = vo[:, :MIX_A]
    o_pre = vo[:, MIX_A:]

    ifb = if_ref[...] + gb_ref[...]
    lf = jnp.minimum(ifb, 0.0) - jnp.log1p(jnp.exp(-jnp.abs(ifb)))
    bcs = _dot(tril_ref[...], lf, precision=HIGHEST)
    ifb_t = ifb.T
    bcs_t = bcs.T
    row = lax.broadcasted_iota(jnp.int32, (L, L), 0)
    col = lax.broadcasted_iota(jnp.int32, (L, L), 1)
    causal = col <= row

    outs = []
    for h in range(H):
        sl = slice(h * DH, (h + 1) * DH)
        qh, kh, vh = q[:, sl], k[:, sl], v[:, sl]
        qb, kb = qh.astype(BF16), kh.astype(BF16)
        b_col = bcs[:, H + h:H + h + 1]
        b_row = bcs_t[H + h:H + h + 1, :]
        li_col = ifb[:, h:h + 1]
        li_row = ifb_t[h:h + 1, :]
        b_last = b_col[L - 1:L, :]
        m0 = m_scr[h][:, 0:1]
        c0 = c_scr[h]
        n0 = n_scr[h]

        log_d = jnp.where(causal, b_col - b_row + li_row, NEG)
        log_inter = b_col + m0
        m_t = jnp.maximum(log_inter, jnp.max(log_d, axis=1, keepdims=True))
        dmat = jnp.exp(log_d - m_t)
        a_inter = jnp.exp(log_inter - m_t)
        s = _dot_nt(qb, kb) * dmat
        num = _dot(s.astype(BF16), vh.astype(BF16)) + a_inter * _dot_nt(qb, c0.astype(BF16))
        den = jnp.sum(s, axis=1, keepdims=True) + a_inter * jnp.sum(qh * n0, axis=1, keepdims=True)
        hh = num / jnp.maximum(jnp.abs(den), jnp.exp(-m_t))

        w_col = b_last - b_col + li_col
        m_loc = jnp.max(w_col, axis=0, keepdims=True)
        e = jnp.exp(w_col - m_loc)
        c_loc = _dot((vh * e).T.astype(BF16), kb)
        n_loc = jnp.sum(kh * e, axis=0, keepdims=True)
        m_new = jnp.maximum(b_last + m0, m_loc)
        a = jnp.exp(b_last + m0 - m_new)
        sc = jnp.exp(m_loc - m_new)
        c_scr[h] = a * c0 + sc * c_loc
        n_scr[h] = a * n0 + sc * n_loc
        m_scr[h] = jnp.broadcast_to(m_new, (1, LANES))

        outs.append(hh * lax.rsqrt(jnp.mean(hh * hh, axis=1, keepdims=True) + EPS))
    hm = jnp.concatenate(outs, axis=1)
    o_ref[...] = (_sigmoid(o_pre) * (hm * hg_ref[...])).astype(o_ref.dtype)


def _mlstm(qk, vo, ifg, conv_w, gate_bias, head_g):
    B, S, _ = qk.shape
    rows = 1
    L, H, DH = MLSTM_CHUNK, MLSTM_HEADS, MLSTM_DH
    tril = jnp.tril(jnp.ones((L, L), F32))
    return pl.pallas_call(
        _mlstm_kernel,
        out_shape=jax.ShapeDtypeStruct((B, S, MIX_A), BF16),
        grid=(B // rows, S // L),
        in_specs=[pl.BlockSpec((rows, L, 2 * MIX_A), lambda b, c: (b, c, 0)),
                  pl.BlockSpec((rows, L, 2 * MIX_A), lambda b, c: (b, c, 0)),
                  pl.BlockSpec((rows, L, LANES), lambda b, c: (b, c, 0)),
                  pl.BlockSpec((CONV_K, 2 * MIX_A), lambda b, c: (0, 0)),
                  pl.BlockSpec((1, LANES), lambda b, c: (0, 0)),
                  pl.BlockSpec((1, MIX_A), lambda b, c: (0, 0)),
                  pl.BlockSpec((L, L), lambda b, c: (0, 0))],
        out_specs=pl.BlockSpec((rows, L, MIX_A), lambda b, c: (b, c, 0)),
        scratch_shapes=[pltpu.VMEM((rows, L + SUBLANES, 2 * MIX_A), F32),
                        pltpu.VMEM((rows, H, DH, DH), F32),
                        pltpu.VMEM((rows, H, 1, DH), F32),
                        pltpu.VMEM((rows, H, 1, LANES), F32)],
        compiler_params=_cparams(("parallel", "arbitrary")),
        name="mlstm",
    )(qk, vo, ifg, conv_w, gate_bias, head_g, tril)


S5_LT = LANES // S5_GROUP
S5_PAIRS = S5_CHUNK // 2


def _s5s_kernel(u_ref, h_ref, e_ref, kk_ref, are_ref, aim_ref, d_ref, gw_ref, gb_ref, o_ref, xl_scr, x0_scr):
    n_chunks = u_ref.shape[1] // S5_CHUNK
    half = S5_LT * S5_STATE
    tok = lambda s: u_ref[0, pl.ds(s, n_chunks, stride=S5_CHUNK), :]
    u2 = [jnp.concatenate([tok(2 * q), tok(2 * q + 1)], axis=1) for q in range(S5_PAIRS)]
    u2b = [v.astype(BF16) for v in u2]
    xl_scr[...] = functools.reduce(lambda a, b: a + b, [_dot(u2b[q], h_ref[0, q]) for q in range(S5_PAIRS)])
    a_re = are_ref[0]
    a_im = aim_ref[0]

    def body(a, carry):
        re, im = carry
        x0_scr[pl.ds(a, 1), 0:half] = re
        x0_scr[pl.ds(a, 1), half:2 * half] = im
        return (a_re * re - a_im * im + xl_scr[pl.ds(a, 1), 0:half],
                a_re * im + a_im * re + xl_scr[pl.ds(a, 1), half:2 * half])

    zero = jnp.zeros((1, half), F32)
    lax.fori_loop(0, n_chunks, body, (zero, zero), unroll=8)
    x0 = x0_scr[...].astype(BF16)
    for p in range(S5_PAIRS):
        y = _dot(x0, e_ref[0, p]) + u2[p] * d_ref[0]
        for q in range(p + 1):
            y = y + _dot(u2b[q], kk_ref[0, p - q])
        ys = _gelu_tanh(y)
        out = ys * _sigmoid(_dot(ys.astype(BF16), gw_ref[0]) + gb_ref[0])
        o_ref[0, pl.ds(2 * p, n_chunks, stride=S5_CHUNK), :] = out[:, :LANES].astype(o_ref.dtype)
        o_ref[0, pl.ds(2 * p + 1, n_chunks, stride=S5_CHUNK), :] = out[:, LANES:].astype(o_ref.dtype)


def _s5s_tables(lam_re, lam_im, log_dt, b_re, b_im, c_re, c_im, d_skip, glu_w, glu_b):
    T, C, P, LT = S5_CHUNK, S5_GROUP, S5_STATE, S5_LT
    G = lam_re.shape[0]
    NT = G // LT
    lam = lax.complex(lam_re.astype(F32), lam_im.astype(F32))
    dt = jnp.exp(log_dt.astype(F32))[:, None]
    lam_bar = jnp.exp(lam * dt)
    b_bar = ((lam_bar - 1.0) / lam)[..., None] * lax.complex(b_re.astype(F32), b_im.astype(F32))
    c_mat = lax.complex(c_re.astype(F32), c_im.astype(F32))
    taus = jnp.arange(T + 1, dtype=F32)
    pw = jnp.exp((lam * dt)[:, None, :] * taus[None, :, None])
    eye = jnp.eye(LT, dtype=F32)
    tiles = lambda a: a.reshape((NT, LT) + a.shape[1:])

    kern = jnp.einsum('gcp,gtp,gpd->gtdc', c_mat, pw[:, :T], b_bar, precision=HIGHEST).real
    kblk = jnp.einsum('nitdc,ij->ntidjc', tiles(kern), eye).reshape(NT, T, LANES, LANES)
    kblk = jnp.concatenate([jnp.zeros_like(kblk[:, :1]), kblk], axis=1)
    kk = jnp.stack([jnp.concatenate([jnp.concatenate([kblk[:, 2 * d + 1], kblk[:, 2 * d + 2]], axis=2),
                                     jnp.concatenate([kblk[:, 2 * d], kblk[:, 2 * d + 1]], axis=2)], axis=1)
                    for d in range(T // 2)], axis=1)

    hmat = pw[:, :T][:, ::-1, :, None] * b_bar[:, None]

    def state_cols(m):
        return jnp.einsum('nispc,ij->nsicjp', tiles(m), eye).reshape(NT, T, LANES, LT * P)

    h = jnp.concatenate([state_cols(hmat.real), state_cols(hmat.imag)], axis=3)
    h2 = h.reshape(NT, T // 2, 2 * LANES, 2 * LT * P)

    emat = c_mat[:, None] * pw[:, 1:][:, :, None, :]

    def state_rows(m):
        return jnp.einsum('nitcp,ij->ntjpic', tiles(m), eye).reshape(NT, T, LT * P, LANES)

    e = jnp.concatenate([state_rows(emat.real), state_rows(-emat.imag)], axis=2)
    e2 = e.reshape(NT, T // 2, 2, 2 * LT * P, LANES).transpose(0, 1, 3, 2, 4).reshape(NT, T // 2, 2 * LT * P, 2 * LANES)

    a_re = pw[:, T].real.reshape(NT, 1, LT * P)
    a_im = pw[:, T].imag.reshape(NT, 1, LT * P)
    pair = lambda v: jnp.tile(v.astype(F32).reshape(NT, 1, LANES), (1, 1, 2))
    gwb = jnp.einsum('nice,ij->nicje', tiles(glu_w.astype(F32)), eye).reshape(NT, LANES, LANES)
    zeros = jnp.zeros_like(gwb)
    gw2 = jnp.concatenate([jnp.concatenate([gwb, zeros], axis=2), jnp.concatenate([zeros, gwb], axis=2)], axis=1)
    return (h2.astype(BF16), e2.astype(BF16), kk.astype(BF16), a_re, a_im, pair(d_skip), gw2.astype(BF16), pair(glu_b))


def _s5s(u, tables):
    B, S, W = u.shape
    NT = W // LANES
    n_chunks = S // S5_CHUNK
    per_tile = lambda a: pl.BlockSpec((1,) + a.shape[1:], lambda j, b: (j,) + (0,) * (a.ndim - 1))
    return pl.pallas_call(
        _s5s_kernel,
        out_shape=jax.ShapeDtypeStruct((B, S, W), F32),
        grid=(NT, B),
        in_specs=[pl.BlockSpec((1, S, LANES), lambda j, b: (b, 0, j))] + [per_tile(t) for t in tables],
        out_specs=pl.BlockSpec((1, S, LANES), lambda j, b: (b, 0, j)),
        scratch_shapes=[pltpu.VMEM((n_chunks, 2 * S5_LT * S5_STATE), F32) for _ in range(2)],
        compiler_params=_cparams(("parallel", "parallel")),
        name="s5",
    )(u, *tables)


def _compress_kernel(x_ref, plo_ref, phi_ref, w1_ref, b1_ref, w2_ref, b2_ref, o_ref):
    x = x_ref[0, 0]
    half = x.shape[1]
    w1 = w1_ref[0]
    lo = _dot((x + plo_ref[0]).astype(BF16), w1[:half])
    hi = _dot((x + phi_ref[0]).astype(BF16), w1[half:])
    rows = x.shape[0]
    hid = _gelu_tanh(lo + pltpu.roll(hi, rows - 1, 0) + b1_ref[0])
    o_ref[0, 0] = _dot(hid.astype(BF16), w2_ref[0]) + b2_ref[0]


def _compress(xg, pos, w1, b1, w2, b2):
    _, B, rows, width = xg.shape
    pos_flat = pos.reshape(2, 2, 1, width).astype(F32)
    sel = lambda shape: pl.BlockSpec((1,) + shape, lambda j, b: (j, 0, 0))
    return pl.pallas_call(
        _compress_kernel,
        out_shape=jax.ShapeDtypeStruct((2, B, rows, NSA_DH), F32),
        grid=(2, B),
        in_specs=[pl.BlockSpec((1, 1, rows, width), lambda j, b: (j, b, 0, 0)),
                  sel((1, width)), sel((1, width)),
                  sel((2 * width, CMP_HIDDEN)), sel((1, CMP_HIDDEN)),
                  sel((CMP_HIDDEN, NSA_DH)), sel((1, NSA_DH))],
        out_specs=pl.BlockSpec((1, 1, rows, NSA_DH), lambda j, b: (j, b, 0, 0)),
        compiler_params=_cparams(("parallel", "parallel")),
        name="nsa_compress",
    )(xg, pos_flat[:, 0], pos_flat[:, 1], w1.astype(BF16), b1[:, None].astype(F32),
      w2.astype(BF16), b2[:, None].astype(F32))


def _t5_bucket(dist):
    dist = jnp.maximum(dist, 0)
    max_exact = REL_BUCKETS // 2
    log_ratio = jnp.log(jnp.maximum(dist, 1).astype(F32) / max_exact) / math.log(REL_MAX_DIST / max_exact)
    large = jnp.minimum(max_exact + (log_ratio * (REL_BUCKETS - max_exact)).astype(jnp.int32), REL_BUCKETS - 1)
    return jnp.where(dist < max_exact, dist, large)


def _nsa_proj_kernel(x_ref, g_ref, sh_ref, sc_ref, wq_ref, wk_ref, wv_ref, wg_ref, bg_ref,
                     q4_ref, gv_ref, kc_ref, vc_ref, ks_ref, kw_ref, vst_ref, vwt_ref):
    KV, R, DH, T = NSA_KV, NSA_R, NSA_DH, ATT_TILE
    h = _modulated_norm(x_ref[0], g_ref[...], sh_ref[0], sc_ref[0]).astype(BF16)
    q_t = (_dot(h, wq_ref[...]) * (DH ** -0.5 * LOG2E)).T.astype(BF16)
    gates_t = _sigmoid(_dot(h, wg_ref[...]) + bg_ref[...]).T
    row = lax.broadcasted_iota(jnp.int32, (SUBLANES, R * T), 0)
    for g in range(KV):
        q4_ref[0, g, 0] = jnp.concatenate([q_t[(g * R + r) * DH:(g * R + r + 1) * DH] for r in range(R)], axis=1)
        gv = jnp.zeros((SUBLANES, R * T), F32)
        for j in range(3):
            gj = jnp.concatenate([gates_t[g * LANES + 3 * r + j:g * LANES + 3 * r + j + 1] for r in range(R)], axis=1)
            gv = jnp.where(row == j, gj, gv)
        gv_ref[0, g, 0] = gv
    k3 = _dot(h, wk_ref[...])
    v3 = _dot(h, wv_ref[...])
    vs_t = v3[:, KV_W:2 * KV_W].T.astype(BF16)
    vw_t = v3[:, 2 * KV_W:].T.astype(BF16)
    for g in range(KV):
        cols = slice(g * DH, (g + 1) * DH)
        kc_ref[0, g] = k3[:, cols].astype(BF16)
        vc_ref[0, g] = v3[:, cols].astype(BF16)
        ks_ref[0, g] = k3[:, KV_W + g * DH:KV_W + (g + 1) * DH].astype(BF16)
        kw_ref[0, g] = k3[:, 2 * KV_W + g * DH:2 * KV_W + (g + 1) * DH].astype(BF16)
        vst_ref[0, g, 0] = vs_t[cols]
        vwt_ref[0, g, 0] = vw_t[cols]


def _nsa_proj(x, g, shift, scale, weights, b_gate):
    B, S, D = x.shape
    KV, R, DH, T = NSA_KV, NSA_R, NSA_DH, ATT_TILE
    vec = pl.BlockSpec((1, 1, D), lambda b, i: (b, 0, 0))
    keys = pl.BlockSpec((1, KV, T, DH), lambda b, i: (b, 0, i, 0))
    key_shape = jax.ShapeDtypeStruct((B, KV, S, DH), BF16)
    tile = lambda rows, width: pl.BlockSpec((1, KV, 1, rows, width), lambda b, i: (b, 0, i, 0, 0))
    tile_shape = lambda rows, width, dt: jax.ShapeDtypeStruct((B, KV, S // T, rows, width), dt)
    return pl.pallas_call(
        _nsa_proj_kernel,
        out_shape=[tile_shape(DH, R * T, BF16), tile_shape(SUBLANES, R * T, F32),
                   key_shape, key_shape, key_shape, key_shape,
                   tile_shape(DH, T, BF16), tile_shape(DH, T, BF16)],
        grid=(B, S // T),
        in_specs=[pl.BlockSpec((1, T, D), lambda b, i: (b, i, 0)),
                  pl.BlockSpec((1, D), lambda b, i: (0, 0)), vec, vec]
                 + [pl.BlockSpec(w.shape, lambda b, i: (0, 0)) for w in weights]
                 + [pl.BlockSpec(b_gate.shape, lambda b, i: (0, 0))],
        out_specs=[tile(DH, R * T), tile(SUBLANES, R * T), keys, keys, keys, keys, tile(DH, T), tile(DH, T)],
        compiler_params=_cparams(("parallel", "parallel")),
        name="nsa_proj",
    )(x, g.reshape(1, D), shift, scale, *weights, b_gate)


def _nsa_t_kernel(q4_ref, gv_ref, kc_ref, vct_ref, ks_ref, vst_ref, kw_ref, vwt_ref,
                  cfar_ref, band_ref, selb_ref, winb_ref, ovt_ref, o_ref, s_scr, sel_scr, sbuf):
    T = ATT_TILE
    R, DH = NSA_R, NSA_DH
    qi = pl.program_id(2)
    q0 = qi * T
    n_pad = kc_ref.shape[2]
    n_sel = ovt_ref.shape[0]
    n_far = selb_ref.shape[0] - 1
    n_win = winb_ref.shape[0] - 2
    band_rows = band_ref.shape[2] - T // CMP_STRIDE * 2

    q4 = q4_ref[0, 0, 0]
    t_lane = q0 + lax.broadcasted_iota(jnp.int32, (1, R * T), 1) % T

    ones_rows = DH
    with_ones = lambda v_t: jnp.concatenate([v_t, jnp.ones((ones_rows, v_t.shape[1]), v_t.dtype)], axis=0)
    gvec = lambda j: gv_ref[0, 0, 0, j:j + 1, :]

    grp = T // CMP_STRIDE
    s_scr[0:n_pad, :] = _dot(kc_ref[0, 0], q4) + cfar_ref[0]
    s_scr[n_pad:n_pad + 2 * grp, :] = jnp.zeros((2 * grp, R * T), F32)
    r0 = jnp.maximum(qi * grp - 2 * grp, 0)
    x0 = r0 - (qi * grp - 2 * grp)
    r0 = pl.multiple_of(r0, SUBLANES)
    x0 = pl.multiple_of(x0, SUBLANES)
    s_scr[pl.ds(r0, band_rows), :] += band_ref[0, 0, pl.ds(x0, band_rows), :]
    lim = pl.multiple_of(qi * grp + 2 * grp, SUBLANES)
    s_scr[pl.ds(lim, n_pad), :] = jnp.full((n_pad, R * T), NEG, F32)

    w_subs, w_vals = [], []
    for d in range(n_win + 1):
        kt = jnp.maximum(qi - d, 0)
        off = pl.multiple_of(kt * T, T)
        tile = jnp.where(qi >= d, d, n_win + 1)
        w_subs.append((_dot(kw_ref[0, 0, pl.ds(off, T), :], q4) + winb_ref[tile, 0]).astype(BF16))
        w_vals.append(with_ones(vwt_ref[0, 0, kt]))

    s = s_scr[0:n_pad, :]
    e = jnp.exp2(s - jnp.max(s, axis=0, keepdims=True))
    inv = jnp.where(t_lane >= CMP_BLOCK - 1, 1.0 / jnp.sum(e, axis=0, keepdims=True), 0.0)
    p = e * inv
    o_cmp = _dot(vct_ref[0, 0], p.astype(BF16))
    psum = functools.reduce(lambda a, b: a + b, [p[:, r * T:(r + 1) * T] for r in range(R)])

    m_w = jnp.max(functools.reduce(jnp.maximum, w_subs), axis=0, keepdims=True)
    acc = functools.reduce(lambda a, b: a + b,
                           [_dot(vj, jnp.exp2(sj - m_w)) for sj, vj in zip(w_subs, w_vals)])
    o_win = acc[:DH] * (1.0 / acc[DH:DH + 1])
    out_t = gvec(0) * o_cmp + gvec(2) * o_win

    imp_t = _dot(ovt_ref[...], psum, precision=HIGHEST)
    jj = lax.broadcasted_iota(jnp.int32, (n_sel, T), 0)
    blk_t = (q0 + lax.broadcasted_iota(jnp.int32, (1, T), 1)) // SEL_BLOCK
    forced = (jj == 0) | (jj == blk_t) | (jj == blk_t - 1)
    score = jnp.where(forced, FORCE, jnp.where(jj <= blk_t, imp_t, -1.0))
    n_blk = n_sel // SUBLANES
    rows = [score[v * SUBLANES:(v + 1) * SUBLANES] for v in range(n_blk)]
    cnts = [jnp.zeros((SUBLANES, T), F32) for _ in range(n_blk)]
    sub = lax.broadcasted_iota(jnp.int32, (SUBLANES, T), 0)
    for j2 in range(n_sel):
        c2 = score[j2:j2 + 1, :]
        for v in range(n_blk):
            lo = v * SUBLANES
            if lo > j2:
                beats = c2 >= rows[v]
            elif lo + SUBLANES - 1 <= j2:
                beats = c2 > rows[v]
            else:
                beats = (c2 > rows[v]) | ((c2 >= rows[v]) & (sub > j2 - lo))
            cnts[v] = cnts[v] + jnp.where(beats, 1.0, 0.0)
    cnt = jnp.concatenate(cnts, axis=0)
    chosen = (cnt < float(min(SEL_TOPK, n_sel))) & (jj <= blk_t)
    sel_scr[...] = jnp.where(chosen, 0.0, -BIG)

    def block_mask(kt):
        per_tile = T // SEL_BLOCK
        parts = [jnp.broadcast_to(sel_scr[pl.ds(kt * per_tile + i, 1), :], (SEL_BLOCK, T)) for i in range(per_tile)]
        m1 = jnp.concatenate(parts, axis=0)
        return jnp.concatenate([m1] * R, axis=1)

    def sel_scores(slot, kc):
        off = pl.multiple_of(kc * T, T)
        s = _dot(ks_ref[0, 0, pl.ds(off, T), :], q4)
        s = (s + selb_ref[jnp.clip(qi - kc, 0, n_far), 0] + block_mask(kc)).astype(BF16)
        sbuf[slot] = s
        return jnp.max(s, axis=0, keepdims=True).astype(F32)

    def sel_weighted(slot, kc, m_new):
        return _dot(with_ones(vst_ref[0, 0, kc]), jnp.exp2(sbuf[slot] - m_new.astype(BF16)))

    last_tile = vst_ref.shape[2] - 1

    def sel_body(i, carry):
        m, acc, m_even = carry
        m_odd = sel_scores(1, 2 * i + 1)
        m_new = jnp.maximum(m, m_even)
        acc = jnp.exp2(m - m_new) * acc + sel_weighted(0, 2 * i, m_new)
        m_even = sel_scores(0, jnp.minimum(2 * i + 2, last_tile))
        m_fin = jnp.maximum(m_new, m_odd)
        acc = jnp.exp2(m_new - m_fin) * acc + sel_weighted(1, 2 * i + 1, m_fin)
        return m_fin, acc, m_even

    _, acc, _ = lax.fori_loop(0, qi // 2 + 1, sel_body,
                              (jnp.full((1, R * T), NEG, F32), jnp.zeros((DH + ones_rows, R * T), F32),
                               sel_scores(0, 0)))
    out_t = out_t + gvec(1) * (acc[:DH] * (1.0 / acc[DH:DH + 1]))
    for pr in range(R // 2):
        pair = jnp.concatenate([out_t[:, (2 * pr) * T:(2 * pr + 1) * T],
                                out_t[:, (2 * pr + 1) * T:(2 * pr + 2) * T]], axis=0)
        o_ref[0, :, pr * 2 * DH:(pr + 1) * 2 * DH] = pair.T.astype(o_ref.dtype)


def _bias_lookup(table, dist):
    idx = _t5_bucket(dist)
    out = jnp.zeros(idx.shape + (table.shape[1],), F32)
    for k in range(table.shape[0]):
        out = out + jnp.where((idx == k)[..., None], table[k], 0.0)
    return out


def _nsa_t_tables(rel_bias, S):
    T, R, KV = ATT_TILE, NSA_R, NSA_KV
    table = rel_bias.astype(F32) * LOG2E
    ii = jnp.arange(T)
    delta = ii[None, :] - ii[:, None]

    def lanes(a):
        a = jnp.moveaxis(a, -1, 0)
        a = a.reshape((KV, R) + a.shape[1:])
        return jnp.moveaxis(a, 1, 2).reshape(KV, a.shape[2], R * a.shape[3])

    def tile(off):
        return lanes(_bias_lookup(table, off * T + delta))

    mask4 = lambda ok: jnp.tile(jnp.where(ok, 0.0, NEG), (1, R))[None]
    n_far = -(-REL_MAX_DIST // T) + 1
    selb = [tile(o) for o in range(n_far + 1)]
    selb[0] = selb[0] + mask4(delta >= 0)
    selb = jnp.stack(selb, axis=0)
    n_win = WINDOW // T
    winb = [tile(o) + mask4((o * T + delta >= 0) & (o * T + delta < WINDOW)) for o in range(n_win + 1)]
    winb.append(jnp.full_like(winb[0], NEG))
    winb = jnp.stack(winb, axis=0)

    grp = T // CMP_STRIDE
    far = _bias_lookup(table, jnp.asarray(2 * REL_MAX_DIST))
    xx = jnp.arange(4 * grp)
    bdist = ii[None, :] - CMP_STRIDE * (xx[:, None] - 2 * grp) - (CMP_BLOCK - 1)
    band = jnp.where((bdist >= 0)[..., None], _bias_lookup(table, bdist) - far, NEG)
    band = jnp.concatenate([lanes(band), jnp.zeros((KV, 2 * grp, R * T), F32)], axis=1)[:, None]
    cfar = jnp.repeat(far.reshape(KV, R), T, axis=1)[:, None]

    n_pad = S // CMP_STRIDE
    n_sel = S // SEL_BLOCK
    cmp_start = jnp.arange(n_pad) * CMP_STRIDE
    sel_start = jnp.arange(n_sel) * SEL_BLOCK
    overlap = jnp.clip(jnp.minimum(cmp_start[:, None] + CMP_BLOCK, sel_start[None] + SEL_BLOCK)
                       - jnp.maximum(cmp_start[:, None], sel_start[None]), 0).astype(F32) / CMP_BLOCK
    n_cmp = (S - CMP_BLOCK) // CMP_STRIDE + 1
    overlap_t = jnp.where((jnp.arange(n_pad) < n_cmp)[:, None], overlap, 0.0).T
    return cfar, band, selb, winb, overlap_t


def _nsa_t_attention(q4, gv, kcmp, vcmp_t, ks, vs_t, kw, vw_t, tables):
    B, KV, S, _ = kw.shape
    T = ATT_TILE
    cfar, band, selb, winb, overlap_t = tables
    gw = NSA_R * NSA_DH
    n_pad = kcmp.shape[2]
    seq = lambda a: pl.BlockSpec((1, 1) + a.shape[2:], lambda b, g, i: (b, g) + (0,) * (a.ndim - 2))
    qtile = lambda a: pl.BlockSpec((1, 1, 1) + a.shape[3:], lambda b, g, i: (b, g, i, 0, 0))
    grp = lambda a: pl.BlockSpec((1,) + a.shape[1:], lambda b, g, i: (g,) + (0,) * (a.ndim - 1))
    tiles = lambda a: pl.BlockSpec((a.shape[0], 1) + a.shape[2:], lambda b, g, i: (0, g, 0, 0))
    full = lambda a: pl.BlockSpec(a.shape, lambda b, g, i: (0,) * a.ndim)
    return pl.pallas_call(
        _nsa_t_kernel,
        out_shape=jax.ShapeDtypeStruct((B, S, KV * gw), BF16),
        grid=(B, KV, S // T),
        in_specs=[qtile(q4), qtile(gv),
                  seq(kcmp), seq(vcmp_t), seq(ks), seq(vs_t), seq(kw), seq(vw_t),
                  grp(cfar), grp(band), tiles(selb), tiles(winb), full(overlap_t)],
        out_specs=pl.BlockSpec((1, T, gw), lambda b, g, i: (b, i, g)),
        scratch_shapes=[pltpu.VMEM((2 * n_pad + 2 * (T // CMP_STRIDE), NSA_R * T), F32),
                        pltpu.VMEM((S // SEL_BLOCK, T), F32),
                        pltpu.VMEM((2, T, NSA_R * T), BF16)],
        compiler_params=_cparams(("parallel", "parallel", "arbitrary")),
        name="nsa_attention",
    )(q4, gv, kcmp, vcmp_t, ks, vs_t, kw, vw_t, cfar, band, selb, winb, overlap_t)


def _moe_kernel(*refs, n_in, final):
    x_ref, mgate_ref = refs[:2]
    a_refs = refs[2:2 + n_in]
    wo_refs = refs[2 + n_in:2 + 2 * n_in]
    (g_ref, sh_ref, sc_ref, gate_ref, wr_ref, br_ref, before_ref, wg_ref, wu_ref, wd_ref, fg_ref,
     o_ref, x_all, hs_all, rts_all, acc_all, perm_t_all, meta_all) = refs[2 + 2 * n_in:]
    NG, PG, FH = MOE_GROUPS, MOE_PER_GROUP, MOE_HIDDEN
    c = pl.program_id(2)
    t = pl.program_id(3)
    x_scr, hs_scr, rts_scr, acc_scr, perm_t_scr = (r.at[t] for r in (x_all, hs_all, rts_all, acc_all, perm_t_all))
    meta = meta_all.at[t]

    @pl.when(c == 0)
    def _():
        mix = functools.reduce(lambda a, b: a + b,
                               [_dot(a_ref[0].astype(BF16), wo_ref[...]) for a_ref, wo_ref in zip(a_refs, wo_refs)])
        x = x_ref[0] + mgate_ref[0] * mix
        x_scr[...] = x
        h = _modulated_norm(x, g_ref[...], sh_ref[0], sc_ref[0])
        h_hi = h.astype(BF16)
        h_lo = (h - h_hi.astype(F32)).astype(BF16)
        logits = (_dot(h_hi, wr_ref[0]) + _dot(h_lo, wr_ref[0]) + _dot(h_hi, wr_ref[1]) + br_ref[...]).T
        gl = [logits[NG * PG + g:NG * PG + g + 1, :] for g in range(NG)]
        gmax = functools.reduce(jnp.maximum, gl)
        gtop = jnp.full_like(gmax, float(NG - 1))
        for g in reversed(range(NG - 1)):
            gtop = jnp.where(gl[g] == gmax, float(g), gtop)
        p_g = 1.0 / functools.reduce(lambda a, b: a + b, [jnp.exp(v - gmax) for v in gl])
        a = []
        for j in range(PG):
            v = logits[(NG - 1) * PG + j:(NG - 1) * PG + j + 1, :]
            for g in reversed(range(NG - 1)):
                v = jnp.where(gtop == float(g), logits[g * PG + j:g * PG + j + 1, :], v)
            a.append(v)

        def first_max(vals):
            vmax = functools.reduce(jnp.maximum, vals)
            taken = jnp.zeros_like(vmax) > 1.0
            hits = []
            for v in vals:
                hit = (v == vmax) & jnp.logical_not(taken)
                taken = taken | hit
                hits.append(hit)
            return vmax, hits

        v1, hit1 = first_max(a)
        rest = [jnp.where(hh, -jnp.inf, v) for hh, v in zip(hit1, a)]
        v2, hit2 = first_max(rest)
        e2 = jnp.exp(v2 - v1)
        w1 = p_g / (1.0 + e2)
        w2 = p_g * e2 / (1.0 + e2)
        tm = gtop.shape[1]
        row = lax.broadcasted_iota(jnp.int32, (SUBLANES, tm), 0)
        onehot = [jnp.where(gtop == float(g), 1.0, 0.0) for g in range(NG)]
        oh8 = jnp.zeros((SUBLANES, tm), F32)
        for g in range(NG):
            oh8 = jnp.where(row == g, onehot[g], oh8)
        before = _dot(oh8.astype(BF16), before_ref[...])
        pos = jnp.zeros_like(gtop)
        off = jnp.int32(0)
        for g in range(NG):
            cnt = jnp.sum(onehot[g]).astype(jnp.int32)
            meta[g] = off
            meta[NG + g] = cnt
            pos = pos + onehot[g] * (before[g:g + 1, :] + off.astype(F32))
            off = off + cnt
        rt = jnp.where(row == PG, gtop, jnp.where(row == PG + 1, pos, 0.0))
        for j in range(PG):
            wj = jnp.where(hit1[j], w1, jnp.where(hit2[j], w2, 0.0))
            rt = jnp.where(row == j, wj, rt)
        rt_tok = jnp.concatenate([rt, jnp.zeros((LANES - SUBLANES, tm), F32)], axis=0).T
        rid = lax.broadcasted_iota(jnp.int32, (tm, tm), 0).astype(F32)
        cid = lax.broadcasted_iota(jnp.int32, (tm, tm), 1).astype(F32)
        perm = jnp.where(rid == pos, 1.0, 0.0).astype(BF16)
        perm_t_scr[...] = jnp.where(rt_tok[:, PG + 1:PG + 2] == cid, 1.0, 0.0).astype(BF16)
        r1 = rt_tok.astype(BF16)
        res = rt_tok - r1.astype(F32)
        r2 = res.astype(BF16)
        r3 = (res - r2.astype(F32)).astype(BF16)
        moved = _dot(perm, jnp.concatenate([h_hi, r1, r2, r3], axis=1))
        d = h_hi.shape[1]
        pad = hs_scr.shape[0] - tm
        hs_scr[0:tm, :] = moved[:, :d].astype(BF16)
        hs_scr[tm:, :] = jnp.zeros((pad, d), BF16)
        rts_scr[0:tm, :] = moved[:, d:d + LANES] + moved[:, d + LANES:d + 2 * LANES] + moved[:, d + 2 * LANES:]
        rts_scr[tm:, :] = jnp.full((pad, LANES), -1.0, F32)
        acc_scr[...] = jnp.zeros(acc_scr.shape, F32)

    WIN = hs_scr.shape[0] - x_scr.shape[0]
    off = meta[c]
    cnt = meta[NG + c]
    base = (off // MOE_ALIGN) * MOE_ALIGN
    n_win = jnp.where(cnt > 0, (off + cnt - base + WIN - 1) // WIN, 0)
    cf = c.astype(F32)

    def win_body(w, carry):
        start = pl.multiple_of(base + w * WIN, MOE_ALIGN)
        hs = hs_scr[pl.ds(start, WIN), :]
        rt = rts_scr[pl.ds(start, WIN), :]
        in_group = rt[:, PG:PG + 1] == cf
        hid = _silu(_dot(hs, wg_ref[0])) * _dot(hs, wu_ref[0])
        parts = [hid[:, j * FH:(j + 1) * FH] * jnp.where(in_group, rt[:, j:j + 1], 0.0) for j in range(PG)]
        acc_scr[pl.ds(start, WIN), :] += _dot(jnp.concatenate(parts, axis=1).astype(BF16), wd_ref[0])
        return carry

    lax.fori_loop(0, n_win, win_body, 0)

    @pl.when(c == NG - 1)
    def _():
        tm = x_scr.shape[0]
        ys = acc_scr[0:tm, :]
        ys_hi = ys.astype(BF16)
        ys_lo = (ys - ys_hi.astype(F32)).astype(BF16)
        back = _dot(perm_t_scr[...], jnp.concatenate([ys_hi, ys_lo], axis=1))
        d = ys.shape[1]
        y = x_scr[...] + gate_ref[0] * (back[:, :d] + back[:, d:])
        if final:
            y = y * lax.rsqrt(jnp.mean(y * y, axis=-1, keepdims=True) + EPS) * fg_ref[...]
        o_ref[0] = y


def _moe(x, mix_gate, acts, w_outs, g, shift, scale, gate, wg, bg, we, be, w_gate, w_up, w_down, final_g, final,
         tm=512):
    B, S, D = x.shape
    n_in = len(acts)
    NG, PG, FH = MOE_GROUPS, MOE_PER_GROUP, MOE_HIDDEN
    wr = jnp.zeros((D, LANES), F32)
    wr = wr.at[:, :NG * PG].set(we.reshape(D, NG * PG).astype(F32)).at[:, NG * PG:NG * PG + NG].set(wg.astype(F32))
    br = jnp.zeros((1, LANES), F32)
    br = br.at[0, :NG * PG].set(be.reshape(NG * PG).astype(F32)).at[0, NG * PG:NG * PG + NG].set(bg.astype(F32))
    wr_hi = wr.astype(BF16)
    wr = jnp.stack([wr_hi, (wr - wr_hi.astype(F32)).astype(BF16)])
    grp = lambda w: w.reshape(NG, PG, D, FH).transpose(0, 2, 1, 3).reshape(NG, D, PG * FH).astype(BF16)
    wd = w_down.reshape(NG, PG * FH, D).astype(BF16)
    ids = jnp.arange(tm)
    before = (ids[:, None] < ids[None, :]).astype(BF16)
    TP = MOE_TILES
    vec = pl.BlockSpec((1, 1, D), lambda b, i, c, t: (b, 0, 0))
    row = pl.BlockSpec((1, D), lambda b, i, c, t: (0, 0))
    wspec = lambda k, n: pl.BlockSpec((1, k, n), lambda b, i, c, t: (c, 0, 0))
    tokens_in = lambda n: pl.BlockSpec((1, tm, n),
                                       lambda b, i, c, t: (b, i * TP + jnp.where(c == 0, t, TP - 1), 0))
    tokens_out = pl.BlockSpec((1, tm, D), lambda b, i, c, t: (b, i * TP + jnp.where(c == NG - 1, t, 0), 0))
    const = lambda a: pl.BlockSpec(a.shape, lambda b, i, c, t: (0,) * a.ndim)
    return pl.pallas_call(
        functools.partial(_moe_kernel, n_in=n_in, final=final),
        out_shape=jax.ShapeDtypeStruct((B, S, D), F32),
        grid=(B, S // (tm * TP), NG, TP),
        in_specs=[tokens_in(D), vec] + [tokens_in(a.shape[2]) for a in acts] + [const(w) for w in w_outs]
                 + [row, vec, vec, vec, const(wr), const(br), const(before),
                    wspec(D, PG * FH), wspec(D, PG * FH), wspec(PG * FH, D), row],
        out_specs=tokens_out,
        scratch_shapes=[pltpu.VMEM((TP, tm, D), F32), pltpu.VMEM((TP, tm + MOE_WIN, D), BF16),
                        pltpu.VMEM((TP, tm + MOE_WIN, LANES), F32), pltpu.VMEM((TP, tm + MOE_WIN, D), F32),
                        pltpu.VMEM((TP, tm, tm), BF16), pltpu.SMEM((TP, 2 * NG), jnp.int32)],
        compiler_params=_cparams(("parallel", "parallel", "arbitrary", "arbitrary")),
        name="moe",
    )(x, mix_gate, *acts, *w_outs, g.reshape(1, D), shift, scale, gate, wr, br, before, grp(w_gate), grp(w_up), wd,
      final_g.reshape(1, D))


def _mlstm_s5_layer(x, g, shift, scale, w_in, conv_w, b_i, b_f, head_g, s5_params, w_out):
    H = MLSTM_HEADS
    A = MIX_A
    w_if = jnp.zeros((D_MODEL, LANES), F32).at[:, :2 * H].set(w_in[:, 4 * A:4 * A + 2 * H])
    weights = [w_in[:, :2 * A], w_in[:, 2 * A:4 * A], w_if, w_in[:, 4 * A + 2 * H:]]
    qk, vo, ifg, u = _norm_matmul(x, g, shift, scale, [w.astype(BF16) for w in weights], [BF16, BF16, F32, F32])
    gate_bias = jnp.zeros((1, LANES), F32).at[0, :H].set(b_i.astype(F32)).at[0, H:2 * H].set(b_f.astype(F32))
    hm = _mlstm(qk, vo, ifg, conv_w.astype(F32), gate_bias, head_g.reshape(1, A).astype(F32))
    ys = _s5s(u, _s5s_tables(*s5_params))
    w_out = w_out.astype(BF16)
    return [hm, ys], [w_out[:A], w_out[A:]]


def _nsa_layer(x, g, shift, scale, w_in, b_gate, cmp_pos, cmp_w1, cmp_b1, cmp_w2, cmp_b2, rel_bias, w_out):
    B, S, D = x.shape
    KV, R, DH = NSA_KV, NSA_R, NSA_DH
    w_g = jnp.zeros((D, KV, LANES), F32).at[:, :, :3 * R].set(w_in[:, D + 6 * KV_W:].reshape(D, KV, 3 * R))
    b_g = jnp.zeros((KV, LANES), F32).at[:, :3 * R].set(b_gate.reshape(KV, 3 * R).astype(F32))
    kv_cols = lambda i: w_in[:, D + i * KV_W:D + (i + 1) * KV_W]
    w_k = jnp.concatenate([kv_cols(0), kv_cols(2), kv_cols(4)], axis=1)
    w_v = jnp.concatenate([kv_cols(1), kv_cols(3), kv_cols(5)], axis=1)
    weights = [w_in[:, :D], w_k, w_v, w_g.reshape(D, KV * LANES)]
    q4, gv, kc, vc, ks, kw, vs_t, vw_t = _nsa_proj(x, g, shift, scale, [w.astype(BF16) for w in weights],
                                                   b_g.reshape(1, KV * LANES))
    grp = CMP_STRIDE
    xg = jnp.stack([kc, vc]).reshape(2, B, KV * S // grp, grp * DH)
    cmp = _compress(xg, cmp_pos, cmp_w1, cmp_b1, cmp_w2, cmp_b2).reshape(2, B, KV, S // grp, DH).astype(BF16)
    out = _nsa_t_attention(q4, gv, cmp[0], cmp[1].transpose(0, 1, 3, 2), ks, vs_t, kw, vw_t,
                           _nsa_t_tables(rel_bias, S))
    return [out], [w_out.astype(BF16)]


def kernel(x, c, rel_bias, ada_w, ada_b, norm_g, final_g,
           a_w_in, a_conv, a_b_i, a_b_f, a_head_g,
           s5_lam_re, s5_lam_im, s5_log_dt, s5_b_re, s5_b_im, s5_c_re, s5_c_im,
           s5_d, s5_glu_w, s5_glu_b, a_w_out,
           n_w_in, n_b_gate, n_cmp_pos, n_cmp_w1, n_cmp_b1, n_cmp_w2, n_cmp_b2, n_w_out,
           r_grp_w, r_grp_b, r_exp_w, r_exp_b, e_w_gate, e_w_up, e_w_down):
    B, S, D = x.shape
    mod = _ada_mod(c, ada_w, ada_b).reshape(DEPTH, 2, B, 1, 3 * D)
    split = lambda m: (m[..., :D], m[..., D:2 * D], m[..., 2 * D:])
    for layer in range(DEPTH):
        shift, scale, mix_gate = split(mod[layer, 0])
        j = layer // 2
        if layer % 2 == 0:
            s5_params = (s5_lam_re[j], s5_lam_im[j], s5_log_dt[j], s5_b_re[j], s5_b_im[j],
                         s5_c_re[j], s5_c_im[j], s5_d[j], s5_glu_w[j], s5_glu_b[j])
            acts, w_outs = _mlstm_s5_layer(x, norm_g[layer, 0], shift, scale, a_w_in[j], a_conv[j], a_b_i[j],
                                           a_b_f[j], a_head_g[j], s5_params, a_w_out[j])
        else:
            acts, w_outs = _nsa_layer(x, norm_g[layer, 0], shift, scale, n_w_in[j], n_b_gate[j], n_cmp_pos[j],
                                      n_cmp_w1[j], n_cmp_b1[j], n_cmp_w2[j], n_cmp_b2[j], rel_bias, n_w_out[j])
        shift, scale, gate = split(mod[layer, 1])
        x = _moe(x, mix_gate, acts, w_outs, norm_g[layer, 1], shift, scale, gate, r_grp_w[layer], r_grp_b[layer],
                 r_exp_w[layer], r_exp_b[layer], e_w_gate[layer], e_w_up[layer], e_w_down[layer], final_g,
                 final=(layer == DEPTH - 1))
    return x
```

```python
import functools
import math

import jax
import jax.numpy as jnp
from jax import lax
from jax.experimental import pallas as pl
from jax.experimental.pallas import tpu as pltpu

F32 = jnp.float32
BF16 = jnp.bfloat16
HIGHEST = lax.Precision.HIGHEST

D_MODEL = 1024
DEPTH = 2
MIX_A = 512
MLSTM_HEADS = 4
MLSTM_DH = MIX_A // MLSTM_HEADS
MLSTM_CHUNK = 128
CONV_K = 4
S5_GROUP = 16
S5_STATE = 64
S5_CHUNK = 16
NSA_HEADS = 16
NSA_KV = 4
NSA_R = NSA_HEADS // NSA_KV
NSA_DH = D_MODEL // NSA_HEADS
KV_W = NSA_KV * NSA_DH
CMP_BLOCK = 32
CMP_STRIDE = 16
CMP_HIDDEN = 256
SEL_BLOCK = 64
SEL_TOPK = 16
WINDOW = 512
FORCE = 1e9
REL_BUCKETS = 32
REL_MAX_DIST = 128
MOE_GROUPS = 4
MOE_PER_GROUP = 4
MOE_HIDDEN = 256
EPS = 1e-6
NEG = -1e30
BIG = 1e30
LOG2E = math.log2(math.e)

LANES = 128
SUBLANES = 8
ATT_TILE = 256
MOE_WIN = 160
MOE_ALIGN = 16
MOE_TILES = 2
VMEM_LIMIT = 56 * 1024 * 1024


def _cparams(sem):
    return pltpu.CompilerParams(dimension_semantics=sem, vmem_limit_bytes=VMEM_LIMIT)


def _dot(a, b, precision=None):
    return jnp.dot(a, b, preferred_element_type=F32, precision=precision)


def _dot_nt(a, b):
    return lax.dot_general(a, b, (((1,), (1,)), ((), ())), preferred_element_type=F32)


def _sigmoid(x):
    return 1.0 / (1.0 + jnp.exp(-x))


def _silu(x):
    return x * _sigmoid(x)


def _gelu_tanh(x):
    return 0.5 * x * (1.0 + jnp.tanh(math.sqrt(2.0 / math.pi) * (x + 0.044715 * (x * x * x))))


def _modulated_norm(x, g, shift, scale):
    y = x * lax.rsqrt(jnp.mean(x * x, axis=-1, keepdims=True) + EPS) * g
    return y * (1.0 + scale) + shift


def _ada_kernel(c_ref, w_ref, b_ref, o_ref):
    c = c_ref[...]
    o_ref[0] = _dot(_silu(c), w_ref[0]) + b_ref[0]


def _ada_mod(c, ada_w, ada_b):
    B, D = c.shape
    n_mod = ada_w.shape[0] * ada_w.shape[1]
    w = ada_w.reshape(n_mod, D, 3 * D)
    b = ada_b.reshape(n_mod, 1, 3 * D)
    tn = 1024
    return pl.pallas_call(
        _ada_kernel,
        out_shape=jax.ShapeDtypeStruct((n_mod, B, 3 * D), F32),
        grid=(n_mod, 3 * D // tn),
        in_specs=[pl.BlockSpec((B, D), lambda i, j: (0, 0)),
                  pl.BlockSpec((1, D, tn), lambda i, j: (i, 0, j)),
                  pl.BlockSpec((1, 1, tn), lambda i, j: (i, 0, j))],
        out_specs=pl.BlockSpec((1, B, tn), lambda i, j: (i, 0, j)),
        compiler_params=_cparams(("parallel", "parallel")),
        name="ada_mod",
    )(c, w, b)


def _norm_mm_kernel(*refs, n_w):
    x_ref, g_ref, sh_ref, sc_ref = refs[:4]
    w_refs = refs[4:4 + n_w]
    o_refs = refs[4 + n_w:]
    h = _modulated_norm(x_ref[0], g_ref[...], sh_ref[0], sc_ref[0]).astype(BF16)
    for w_ref, o_ref in zip(w_refs, o_refs):
        o_ref[0] = _dot(h, w_ref[...]).astype(o_ref.dtype)


def _norm_matmul(x, g, shift, scale, weights, out_dtypes, tm=512):
    B, S, D = x.shape
    n_w = len(weights)
    vec = pl.BlockSpec((1, 1, D), lambda b, i: (b, 0, 0))
    in_specs = [pl.BlockSpec((1, tm, D), lambda b, i: (b, i, 0)),
                pl.BlockSpec((1, D), lambda b, i: (0, 0)), vec, vec]
    in_specs += [pl.BlockSpec(w.shape, lambda b, i: (0, 0)) for w in weights]
    return pl.pallas_call(
        functools.partial(_norm_mm_kernel, n_w=n_w),
        out_shape=[jax.ShapeDtypeStruct((B, S, w.shape[1]), dt) for w, dt in zip(weights, out_dtypes)],
        grid=(B, S // tm),
        in_specs=in_specs,
        out_specs=[pl.BlockSpec((1, tm, w.shape[1]), lambda b, i: (b, i, 0)) for w in weights],
        compiler_params=_cparams(("parallel", "parallel")),
        name="norm_matmul",
    )(x, g.reshape(1, D), shift, scale, *weights)


def _mlstm_kernel(qk_ref, vo_ref, if_ref, cw_ref, gb_ref, hg_ref, tril_ref, o_ref,
                  xbuf, c_scr, n_scr, m_scr):
    pad = SUBLANES

    @pl.when(pl.program_id(1) == 0)
    def _():
        xbuf[:, 0:pad, :] = jnp.zeros((xbuf.shape[0], pad, 2 * MIX_A), F32)
        c_scr[...] = jnp.zeros_like(c_scr)
        n_scr[...] = jnp.zeros_like(n_scr)
        m_scr[...] = jnp.zeros_like(m_scr)

    for bb in range(qk_ref.shape[0]):
        _mlstm_chunk(qk_ref.at[bb], vo_ref.at[bb], if_ref.at[bb], cw_ref, gb_ref, hg_ref, tril_ref, o_ref.at[bb],
                     xbuf.at[bb], c_scr.at[bb], n_scr.at[bb], m_scr.at[bb])


def _mlstm_chunk(qk_ref, vo_ref, if_ref, cw_ref, gb_ref, hg_ref, tril_ref, o_ref, xbuf, c_scr, n_scr, m_scr):
    L, H, DH = MLSTM_CHUNK, MLSTM_HEADS, MLSTM_DH
    pad = SUBLANES
    xbuf[pad:pad + L, :] = qk_ref[...].astype(F32)
    cw = cw_ref[...]
    conv = None
    for j in range(CONV_K):
        lo = pad - (CONV_K - 1) + j
        t = xbuf[lo:lo + L, :] * cw[j:j + 1, :]
        conv = t if conv is None else conv + t
    xbuf[0:pad, :] = xbuf[L:L + pad, :]
    qk = _silu(conv)
    q = qk[:, :MIX_A]
    k = qk[:, MIX_A:] * (DH ** -0.5)
    vo = vo_ref[...].astype(F32)
    v = vo[:, :MIX_A]
    o_pre = vo[:, MIX_A:]

    ifb = if_ref[...] + gb_ref[...]
    lf = jnp.minimum(ifb, 0.0) - jnp.log1p(jnp.exp(-jnp.abs(ifb)))
    bcs = _dot(tril_ref[...], lf, precision=HIGHEST)
    ifb_t = ifb.T
    bcs_t = bcs.T
    row = lax.broadcasted_iota(jnp.int32, (L, L), 0)
    col = lax.broadcasted_iota(jnp.int32, (L, L), 1)
    causal = col <= row

    outs = []
    for h in range(H):
        sl = slice(h * DH, (h + 1) * DH)
        qh, kh, vh = q[:, sl], k[:, sl], v[:, sl]
        qb, kb = qh.astype(BF16), kh.astype(BF16)
        b_col = bcs[:, H + h:H + h + 1]
        b_row = bcs_t[H + h:H + h + 1, :]
        li_col = ifb[:, h:h + 1]
        li_row = ifb_t[h:h + 1, :]
        b_last = b_col[L - 1:L, :]
        m0 = m_scr[h][:, 0:1]
        c0 = c_scr[h]
        n0 = n_scr[h]

        log_d = jnp.where(causal, b_col - b_row + li_row, NEG)
        log_inter = b_col + m0
        m_t = jnp.maximum(log_inter, jnp.max(log_d, axis=1, keepdims=True))
        dmat = jnp.exp(log_d - m_t)
        a_inter = jnp.exp(log_inter - m_t)
        s = _dot_nt(qb, kb) * dmat
        num = _dot(s.astype(BF16), vh.astype(BF16)) + a_inter * _dot_nt(qb, c0.astype(BF16))
        den = jnp.sum(s, axis=1, keepdims=True) + a_inter * jnp.sum(qh * n0, axis=1, keepdims=True)
        hh = num / jnp.maximum(jnp.abs(den), jnp.exp(-m_t))

        w_col = b_last - b_col + li_col
        m_loc = jnp.max(w_col, axis=0, keepdims=True)
        e = jnp.exp(w_col - m_loc)
        c_loc = _dot((vh * e).T.astype(BF16), kb)
        n_loc = jnp.sum(kh * e, axis=0, keepdims=True)
        m_new = jnp.maximum(b_last + m0, m_loc)
        a = jnp.exp(b_last + m0 - m_new)
        sc = jnp.exp(m_loc - m_new)
        c_scr[h] = a * c0 + sc * c_loc
        n_scr[h] = a * n0 + sc * n_loc
        m_scr[h] = jnp.broadcast_to(m_new, (1, LANES))

        outs.append(hh * lax.rsqrt(jnp.mean(hh * hh, axis=1, keepdims=True) + EPS))
    hm = jnp.concatenate(outs, axis=1)
    o_ref[...] = (_sigmoid(o_pre) * (hm * hg_ref[...])).astype(o_ref.dtype)


def _mlstm(qk, vo, ifg, conv_w, gate_bias, head_g):
    B, S, _ = qk.shape
    rows = 1
    L, H, DH = MLSTM_CHUNK, MLSTM_HEADS, MLSTM_DH
    tril = jnp.tril(jnp.ones((L, L), F32))
    return pl.pallas_call(
        _mlstm_kernel,
        out_shape=jax.ShapeDtypeStruct((B, S, MIX_A), BF16),
        grid=(B // rows, S // L),
        in_specs=[pl.BlockSpec((rows, L, 2 * MIX_A), lambda b, c: (b, c, 0)),
                  pl.BlockSpec((rows, L, 2 * MIX_A), lambda b, c: (b, c, 0)),
                  pl.BlockSpec((rows, L, LANES), lambda b, c: (b, c, 0)),
                  pl.BlockSpec((CONV_K, 2 * MIX_A), lambda b, c: (0, 0)),
                  pl.BlockSpec((1, LANES), lambda b, c: (0, 0)),
                  pl.BlockSpec((1, MIX_A), lambda b, c: (0, 0)),
                  pl.BlockSpec((L, L), lambda b, c: (0, 0))],
        out_specs=pl.BlockSpec((rows, L, MIX_A), lambda b, c: (b, c, 0)),
        scratch_shapes=[pltpu.VMEM((rows, L + SUBLANES, 2 * MIX_A), F32),
                        pltpu.VMEM((rows, H, DH, DH), F32),
                        pltpu.VMEM((rows, H, 1, DH), F32),
                        pltpu.VMEM((rows, H, 1, LANES), F32)],
        compiler_params=_cparams(("parallel", "arbitrary")),
        name="mlstm",
    )(qk, vo, ifg, conv_w, gate_bias, head_g, tril)


S5_LT = LANES // S5_GROUP
S5_PAIRS = S5_CHUNK // 2


def _s5s_kernel(u_ref, h_ref, e_ref, kk_ref, are_ref, aim_ref, d_ref, gw_ref, gb_ref, o_ref, xl_scr, x0_scr):
    n_chunks = u_ref.shape[1] // S5_CHUNK
    half = S5_LT * S5_STATE
    tok = lambda s: u_ref[0, pl.ds(s, n_chunks, stride=S5_CHUNK), :]
    u2 = [jnp.concatenate([tok(2 * q), tok(2 * q + 1)], axis=1) for q in range(S5_PAIRS)]
    u2b = [v.astype(BF16) for v in u2]
    xl_scr[...] = functools.reduce(lambda a, b: a + b, [_dot(u2b[q], h_ref[0, q]) for q in range(S5_PAIRS)])
    a_re = are_ref[0]
    a_im = aim_ref[0]

    def body(a, carry):
        re, im = carry
        x0_scr[pl.ds(a, 1), 0:half] = re
        x0_scr[pl.ds(a, 1), half:2 * half] = im
        return (a_re * re - a_im * im + xl_scr[pl.ds(a, 1), 0:half],
                a_re * im + a_im * re + xl_scr[pl.ds(a, 1), half:2 * half])

    zero = jnp.zeros((1, half), F32)
    lax.fori_loop(0, n_chunks, body, (zero, zero), unroll=8)
    x0 = x0_scr[...].astype(BF16)
    for p in range(S5_PAIRS):
        y = _dot(x0, e_ref[0, p]) + u2[p] * d_ref[0]
        for q in range(p + 1):
            y = y + _dot(u2b[q], kk_ref[0, p - q])
        ys = _gelu_tanh(y)
        out = ys * _sigmoid(_dot(ys.astype(BF16), gw_ref[0]) + gb_ref[0])
        o_ref[0, pl.ds(2 * p, n_chunks, stride=S5_CHUNK), :] = out[:, :LANES].astype(o_ref.dtype)
        o_ref[0, pl.ds(2 * p + 1, n_chunks, stride=S5_CHUNK), :] = out[:, LANES:].astype(o_ref.dtype)


def _s5s_tables(lam_re, lam_im, log_dt, b_re, b_im, c_re, c_im, d_skip, glu_w, glu_b):
    T, C, P, LT = S5_CHUNK, S5_GROUP, S5_STATE, S5_LT
    G = lam_re.shape[0]
    NT = G // LT
    lam = lax.complex(lam_re.astype(F32), lam_im.astype(F32))
    dt = jnp.exp(log_dt.astype(F32))[:, None]
    lam_bar = jnp.exp(lam * dt)
    b_bar = ((lam_bar - 1.0) / lam)[..., None] * lax.complex(b_re.astype(F32), b_im.astype(F32))
    c_mat = lax.complex(c_re.astype(F32), c_im.astype(F32))
    taus = jnp.arange(T + 1, dtype=F32)
    pw = jnp.exp((lam * dt)[:, None, :] * taus[None, :, None])
    eye = jnp.eye(LT, dtype=F32)
    tiles = lambda a: a.reshape((NT, LT) + a.shape[1:])

    kern = jnp.einsum('gcp,gtp,gpd->gtdc', c_mat, pw[:, :T], b_bar, precision=HIGHEST).real
    kblk = jnp.einsum('nitdc,ij->ntidjc', tiles(kern), eye).reshape(NT, T, LANES, LANES)
    kblk = jnp.concatenate([jnp.zeros_like(kblk[:, :1]), kblk], axis=1)
    kk = jnp.stack([jnp.concatenate([jnp.concatenate([kblk[:, 2 * d + 1], kblk[:, 2 * d + 2]], axis=2),
                                     jnp.concatenate([kblk[:, 2 * d], kblk[:, 2 * d + 1]], axis=2)], axis=1)
                    for d in range(T // 2)], axis=1)

    hmat = pw[:, :T][:, ::-1, :, None] * b_bar[:, None]

    def state_cols(m):
        return jnp.einsum('nispc,ij->nsicjp', tiles(m), eye).reshape(NT, T, LANES, LT * P)

    h = jnp.concatenate([state_cols(hmat.real), state_cols(hmat.imag)], axis=3)
    h2 = h.reshape(NT, T // 2, 2 * LANES, 2 * LT * P)

    emat = c_mat[:, None] * pw[:, 1:][:, :, None, :]

    def state_rows(m):
        return jnp.einsum('nitcp,ij->ntjpic', tiles(m), eye).reshape(NT, T, LT * P, LANES)

    e = jnp.concatenate([state_rows(emat.real), state_rows(-emat.imag)], axis=2)
    e2 = e.reshape(NT, T // 2, 2, 2 * LT * P, LANES).transpose(0, 1, 3, 2, 4).reshape(NT, T // 2, 2 * LT * P, 2 * LANES)

    a_re = pw[:, T].real.reshape(NT, 1, LT * P)
    a_im = pw[:, T].imag.reshape(NT, 1, LT * P)
    pair = lambda v: jnp.tile(v.astype(F32).reshape(NT, 1, LANES), (1, 1, 2))
    gwb = jnp.einsum('nice,ij->nicje', tiles(glu_w.astype(F32)), eye).reshape(NT, LANES, LANES)
    zeros = jnp.zeros_like(gwb)
    gw2 = jnp.concatenate([jnp.concatenate([gwb, zeros], axis=2), jnp.concatenate([zeros, gwb], axis=2)], axis=1)
    return (h2.astype(BF16), e2.astype(BF16), kk.astype(BF16), a_re, a_im, pair(d_skip), gw2.astype(BF16), pair(glu_b))


def _s5s(u, tables):
    B, S, W = u.shape
    NT = W // LANES
    n_chunks = S // S5_CHUNK
    per_tile = lambda a: pl.BlockSpec((1,) + a.shape[1:], lambda j, b: (j,) + (0,) * (a.ndim - 1))
    return pl.pallas_call(
        _s5s_kernel,
        out_shape=jax.ShapeDtypeStruct((B, S, W), F32),
        grid=(NT, B),
        in_specs=[pl.BlockSpec((1, S, LANES), lambda j, b: (b, 0, j))] + [per_tile(t) for t in tables],
        out_specs=pl.BlockSpec((1, S, LANES), lambda j, b: (b, 0, j)),
        scratch_shapes=[pltpu.VMEM((n_chunks, 2 * S5_LT * S5_STATE), F32) for _ in range(2)],
        compiler_params=_cparams(("parallel", "parallel")),
        name="s5",
    )(u, *tables)


def _compress_kernel(x_ref, plo_ref, phi_ref, w1_ref, b1_ref, w2_ref, b2_ref, o_ref):
    x = x_ref[0, 0]
    half = x.shape[1]
    w1 = w1_ref[0]
    lo = _dot((x + plo_ref[0]).astype(BF16), w1[:half])
    hi = _dot((x + phi_ref[0]).astype(BF16), w1[half:])
    rows = x.shape[0]
    hid = _gelu_tanh(lo + pltpu.roll(hi, rows - 1, 0) + b1_ref[0])
    o_ref[0, 0] = _dot(hid.astype(BF16), w2_ref[0]) + b2_ref[0]


def _compress(xg, pos, w1, b1, w2, b2):
    _, B, rows, width = xg.shape
    pos_flat = pos.reshape(2, 2, 1, width).astype(F32)
    sel = lambda shape: pl.BlockSpec((1,) + shape, lambda j, b: (j, 0, 0))
    return pl.pallas_call(
        _compress_kernel,
        out_shape=jax.ShapeDtypeStruct((2, B, rows, NSA_DH), F32),
        grid=(2, B),
        in_specs=[pl.BlockSpec((1, 1, rows, width), lambda j, b: (j, b, 0, 0)),
                  sel((1, width)), sel((1, width)),
                  sel((2 * width, CMP_HIDDEN)), sel((1, CMP_HIDDEN)),
                  sel((CMP_HIDDEN, NSA_DH)), sel((1, NSA_DH))],
        out_specs=pl.BlockSpec((1, 1, rows, NSA_DH), lambda j, b: (j, b, 0, 0)),
        compiler_params=_cparams(("parallel", "parallel")),
        name="nsa_compress",
    )(xg, pos_flat[:, 0], pos_flat[:, 1], w1.astype(BF16), b1[:, None].astype(F32),
      w2.astype(BF16), b2[:, None].astype(F32))


def _t5_bucket(dist):
    dist = jnp.maximum(dist, 0)
    max_exact = REL_BUCKETS // 2
    log_ratio = jnp.log(jnp.maximum(dist, 1).astype(F32) / max_exact) / math.log(REL_MAX_DIST / max_exact)
    large = jnp.minimum(max_exact + (log_ratio * (REL_BUCKETS - max_exact)).astype(jnp.int32), REL_BUCKETS - 1)
    return jnp.where(dist < max_exact, dist, large)


def _nsa_proj_kernel(x_ref, g_ref, sh_ref, sc_ref, wq_ref, wk_ref, wv_ref, wg_ref, bg_ref,
                     q4_ref, gv_ref, kc_ref, vc_ref, ks_ref, kw_ref, vst_ref, vwt_ref):
    KV, R, DH, T = NSA_KV, NSA_R, NSA_DH, ATT_TILE
    h = _modulated_norm(x_ref[0], g_ref[...], sh_ref[0], sc_ref[0]).astype(BF16)
    q_t = (_dot(h, wq_ref[...]) * (DH ** -0.5 * LOG2E)).T.astype(BF16)
    gates_t = _sigmoid(_dot(h, wg_ref[...]) + bg_ref[...]).T
    row = lax.broadcasted_iota(jnp.int32, (SUBLANES, R * T), 0)
    for g in range(KV):
        q4_ref[0, g, 0] = jnp.concatenate([q_t[(g * R + r) * DH:(g * R + r + 1) * DH] for r in range(R)], axis=1)
        gv = jnp.zeros((SUBLANES, R * T), F32)
        for j in range(3):
            gj = jnp.concatenate([gates_t[g * LANES + 3 * r + j:g * LANES + 3 * r + j + 1] for r in range(R)], axis=1)
            gv = jnp.where(row == j, gj, gv)
        gv_ref[0, g, 0] = gv
    k3 = _dot(h, wk_ref[...])
    v3 = _dot(h, wv_ref[...])
    vs_t = v3[:, KV_W:2 * KV_W].T.astype(BF16)
    vw_t = v3[:, 2 * KV_W:].T.astype(BF16)
    for g in range(KV):
        cols = slice(g * DH, (g + 1) * DH)
        kc_ref[0, g] = k3[:, cols].astype(BF16)
        vc_ref[0, g] = v3[:, cols].astype(BF16)
        ks_ref[0, g] = k3[:, KV_W + g * DH:KV_W + (g + 1) * DH].astype(BF16)
        kw_ref[0, g] = k3[:, 2 * KV_W + g * DH:2 * KV_W + (g + 1) * DH].astype(BF16)
        vst_ref[0, g, 0] = vs_t[cols]
        vwt_ref[0, g, 0] = vw_t[cols]


def _nsa_proj(x, g, shift, scale, weights, b_gate):
    B, S, D = x.shape
    KV, R, DH, T = NSA_KV, NSA_R, NSA_DH, ATT_TILE
    vec = pl.BlockSpec((1, 1, D), lambda b, i: (b, 0, 0))
    keys = pl.BlockSpec((1, KV, T, DH), lambda b, i: (b, 0, i, 0))
    key_shape = jax.ShapeDtypeStruct((B, KV, S, DH), BF16)
    tile = lambda rows, width: pl.BlockSpec((1, KV, 1, rows, width), lambda b, i: (b, 0, i, 0, 0))
    tile_shape = lambda rows, width, dt: jax.ShapeDtypeStruct((B, KV, S // T, rows, width), dt)
    return pl.pallas_call(
        _nsa_proj_kernel,
        out_shape=[tile_shape(DH, R * T, BF16), tile_shape(SUBLANES, R * T, F32),
                   key_shape, key_shape, key_shape, key_shape,
                   tile_shape(DH, T, BF16), tile_shape(DH, T, BF16)],
        grid=(B, S // T),
        in_specs=[pl.BlockSpec((1, T, D), lambda b, i: (b, i, 0)),
                  pl.BlockSpec((1, D), lambda b, i: (0, 0)), vec, vec]
                 + [pl.BlockSpec(w.shape, lambda b, i: (0, 0)) for w in weights]
                 + [pl.BlockSpec(b_gate.shape, lambda b, i: (0, 0))],
        out_specs=[tile(DH, R * T), tile(SUBLANES, R * T), keys, keys, keys, keys, tile(DH, T), tile(DH, T)],
        compiler_params=_cparams(("parallel", "parallel")),
        name="nsa_proj",
    )(x, g.reshape(1, D), shift, scale, *weights, b_gate)


def _nsa_t_kernel(q4_ref, gv_ref, kc_ref, vct_ref, ks_ref, vst_ref, kw_ref, vwt_ref,
                  cfar_ref, band_ref, selb_ref, winb_ref, ovt_ref, o_ref, s_scr, sel_scr, sbuf):
    T = ATT_TILE
    R, DH = NSA_R, NSA_DH
    qi = pl.program_id(2)
    q0 = qi * T
    n_pad = kc_ref.shape[2]
    n_sel = ovt_ref.shape[0]
    n_far = selb_ref.shape[0] - 1
    n_win = winb_ref.shape[0] - 2
    band_rows = band_ref.shape[2] - T // CMP_STRIDE * 2

    q4 = q4_ref[0, 0, 0]
    t_lane = q0 + lax.broadcasted_iota(jnp.int32, (1, R * T), 1) % T

    ones_rows = DH
    with_ones = lambda v_t: jnp.concatenate([v_t, jnp.ones((ones_rows, v_t.shape[1]), v_t.dtype)], axis=0)
    gvec = lambda j: gv_ref[0, 0, 0, j:j + 1, :]

    grp = T // CMP_STRIDE
    s_scr[0:n_pad, :] = _dot(kc_ref[0, 0], q4) + cfar_ref[0]
    s_scr[n_pad:n_pad + 2 * grp, :] = jnp.zeros((2 * grp, R * T), F32)
    r0 = jnp.maximum(qi * grp - 2 * grp, 0)
    x0 = r0 - (qi * grp - 2 * grp)
    r0 = pl.multiple_of(r0, SUBLANES)
    x0 = pl.multiple_of(x0, SUBLANES)
    s_scr[pl.ds(r0, band_rows), :] += band_ref[0, 0, pl.ds(x0, band_rows), :]
    lim = pl.multiple_of(qi * grp + 2 * grp, SUBLANES)
    s_scr[pl.ds(lim, n_pad), :] = jnp.full((n_pad, R * T), NEG, F32)

    w_subs, w_vals = [], []
    for d in range(n_win + 1):
        kt = jnp.maximum(qi - d, 0)
        off = pl.multiple_of(kt * T, T)
        tile = jnp.where(qi >= d, d, n_win + 1)
        w_subs.append((_dot(kw_ref[0, 0, pl.ds(off, T), :], q4) + winb_ref[tile, 0]).astype(BF16))
        w_vals.append(with_ones(vwt_ref[0, 0, kt]))

    s = s_scr[0:n_pad, :]
    e = jnp.exp2(s - jnp.max(s, axis=0, keepdims=True))
    inv = jnp.where(t_lane >= CMP_BLOCK - 1, 1.0 / jnp.sum(e, axis=0, keepdims=True), 0.0)
    p = e * inv
    o_cmp = _dot(vct_ref[0, 0], p.astype(BF16))
    psum = functools.reduce(lambda a, b: a + b, [p[:, r * T:(r + 1) * T] for r in range(R)])

    m_w = jnp.max(functools.reduce(jnp.maximum, w_subs), axis=0, keepdims=True)
    acc = functools.reduce(lambda a, b: a + b,
                           [_dot(vj, jnp.exp2(sj - m_w)) for sj, vj in zip(w_subs, w_vals)])
    o_win = acc[:DH] * (1.0 / acc[DH:DH + 1])
    out_t = gvec(0) * o_cmp + gvec(2) * o_win

    imp_t = _dot(ovt_ref[...], psum, precision=HIGHEST)
    jj = lax.broadcasted_iota(jnp.int32, (n_sel, T), 0)
    blk_t = (q0 + lax.broadcasted_iota(jnp.int32, (1, T), 1)) // SEL_BLOCK
    forced = (jj == 0) | (jj == blk_t) | (jj == blk_t - 1)
    score = jnp.where(forced, FORCE, jnp.where(jj <= blk_t, imp_t, -1.0))
    n_blk = n_sel // SUBLANES
    rows = [score[v * SUBLANES:(v + 1) * SUBLANES] for v in range(n_blk)]
    cnts = [jnp.zeros((SUBLANES, T), F32) for _ in range(n_blk)]
    sub = lax.broadcasted_iota(jnp.int32, (SUBLANES, T), 0)
    for j2 in range(n_sel):
        c2 = score[j2:j2 + 1, :]
        for v in range(n_blk):
            lo = v * SUBLANES
            if lo > j2:
                beats = c2 >= rows[v]
            elif lo + SUBLANES - 1 <= j2:
                beats = c2 > rows[v]
            else:
                beats = (c2 > rows[v]) | ((c2 >= rows[v]) & (sub > j2 - lo))
            cnts[v] = cnts[v] + jnp.where(beats, 1.0, 0.0)
    cnt = jnp.concatenate(cnts, axis=0)
    chosen = (cnt < float(min(SEL_TOPK, n_sel))) & (jj <= blk_t)
    sel_scr[...] = jnp.where(chosen, 0.0, -BIG)

    def block_mask(kt):
        per_tile = T // SEL_BLOCK
        parts = [jnp.broadcast_to(sel_scr[pl.ds(kt * per_tile + i, 1), :], (SEL_BLOCK, T)) for i in range(per_tile)]
        m1 = jnp.concatenate(parts, axis=0)
        return jnp.concatenate([m1] * R, axis=1)

    def sel_scores(slot, kc):
        off = pl.multiple_of(kc * T, T)
        s = _dot(ks_ref[0, 0, pl.ds(off, T), :], q4)
        s = (s + selb_ref[jnp.clip(qi - kc, 0, n_far), 0] + block_mask(kc)).astype(BF16)
        sbuf[slot] = s
        return jnp.max(s, axis=0, keepdims=True).astype(F32)

    def sel_weighted(slot, kc, m_new):
        return _dot(with_ones(vst_ref[0, 0, kc]), jnp.exp2(sbuf[slot] - m_new.astype(BF16)))

    last_tile = vst_ref.shape[2] - 1

    def sel_body(i, carry):
        m, acc, m_even = carry
        m_odd = sel_scores(1, 2 * i + 1)
        m_new = jnp.maximum(m, m_even)
        acc = jnp.exp2(m - m_new) * acc + sel_weighted(0, 2 * i, m_new)
        m_even = sel_scores(0, jnp.minimum(2 * i + 2, last_tile))
        m_fin = jnp.maximum(m_new, m_odd)
        acc = jnp.exp2(m_new - m_fin) * acc + sel_weighted(1, 2 * i + 1, m_fin)
        return m_fin, acc, m_even

    _, acc, _ = lax.fori_loop(0, qi // 2 + 1, sel_body,
                              (jnp.full((1, R * T), NEG, F32), jnp.zeros((DH + ones_rows, R * T), F32),
                               sel_scores(0, 0)))
    out_t = out_t + gvec(1) * (acc[:DH] * (1.0 / acc[DH:DH + 1]))
    for pr in range(R // 2):
        pair = jnp.concatenate([out_t[:, (2 * pr) * T:(2 * pr + 1) * T],
                                out_t[:, (2 * pr + 1) * T:(2 * pr + 2) * T]], axis=0)
        o_ref[0, :, pr * 2 * DH:(pr + 1) * 2 * DH] = pair.T.astype(o_ref.dtype)


def _bias_lookup(table, dist):
    idx = _t5_bucket(dist)
    out = jnp.zeros(idx.shape + (table.shape[1],), F32)
    for k in range(table.shape[0]):
        out = out + jnp.where((idx == k)[..., None], table[k], 0.0)
    return out


def _nsa_t_tables(rel_bias, S):
    T, R, KV = ATT_TILE, NSA_R, NSA_KV
    table = rel_bias.astype(F32) * LOG2E
    ii = jnp.arange(T)
    delta = ii[None, :] - ii[:, None]

    def lanes(a):
        a = jnp.moveaxis(a, -1, 0)
        a = a.reshape((KV, R) + a.shape[1:])
        return jnp.moveaxis(a, 1, 2).reshape(KV, a.shape[2], R * a.shape[3])

    def tile(off):
        return lanes(_bias_lookup(table, off * T + delta))

    mask4 = lambda ok: jnp.tile(jnp.where(ok, 0.0, NEG), (1, R))[None]
    n_far = -(-REL_MAX_DIST // T) + 1
    selb = [tile(o) for o in range(n_far + 1)]
    selb[0] = selb[0] + mask4(delta >= 0)
    selb = jnp.stack(selb, axis=0)
    n_win = WINDOW // T
    winb = [tile(o) + mask4((o * T + delta >= 0) & (o * T + delta < WINDOW)) for o in range(n_win + 1)]
    winb.append(jnp.full_like(winb[0], NEG))
    winb = jnp.stack(winb, axis=0)

    grp = T // CMP_STRIDE
    far = _bias_lookup(table, jnp.asarray(2 * REL_MAX_DIST))
    xx = jnp.arange(4 * grp)
    bdist = ii[None, :] - CMP_STRIDE * (xx[:, None] - 2 * grp) - (CMP_BLOCK - 1)
    band = jnp.where((bdist >= 0)[..., None], _bias_lookup(table, bdist) - far, NEG)
    band = jnp.concatenate([lanes(band), jnp.zeros((KV, 2 * grp, R * T), F32)], axis=1)[:, None]
    cfar = jnp.repeat(far.reshape(KV, R), T, axis=1)[:, None]

    n_pad = S // CMP_STRIDE
    n_sel = S // SEL_BLOCK
    cmp_start = jnp.arange(n_pad) * CMP_STRIDE
    sel_start = jnp.arange(n_sel) * SEL_BLOCK
    overlap = jnp.clip(jnp.minimum(cmp_start[:, None] + CMP_BLOCK, sel_start[None] + SEL_BLOCK)
                       - jnp.maximum(cmp_start[:, None], sel_start[None]), 0).astype(F32) / CMP_BLOCK
    n_cmp = (S - CMP_BLOCK) // CMP_STRIDE + 1
    overlap_t = jnp.where((jnp.arange(n_pad) < n_cmp)[:, None], overlap, 0.0).T
    return cfar, band, selb, winb, overlap_t


def _nsa_t_attention(q4, gv, kcmp, vcmp_t, ks, vs_t, kw, vw_t, tables):
    B, KV, S, _ = kw.shape
    T = ATT_TILE
    cfar, band, selb, winb, overlap_t = tables
    gw = NSA_R * NSA_DH
    n_pad = kcmp.shape[2]
    seq = lambda a: pl.BlockSpec((1, 1) + a.shape[2:], lambda b, g, i: (b, g) + (0,) * (a.ndim - 2))
    qtile = lambda a: pl.BlockSpec((1, 1, 1) + a.shape[3:], lambda b, g, i: (b, g, i, 0, 0))
    grp = lambda a: pl.BlockSpec((1,) + a.shape[1:], lambda b, g, i: (g,) + (0,) * (a.ndim - 1))
    tiles = lambda a: pl.BlockSpec((a.shape[0], 1) + a.shape[2:], lambda b, g, i: (0, g, 0, 0))
    full = lambda a: pl.BlockSpec(a.shape, lambda b, g, i: (0,) * a.ndim)
    return pl.pallas_call(
        _nsa_t_kernel,
        out_shape=jax.ShapeDtypeStruct((B, S, KV * gw), BF16),
        grid=(B, KV, S // T),
        in_specs=[qtile(q4), qtile(gv),
                  seq(kcmp), seq(vcmp_t), seq(ks), seq(vs_t), seq(kw), seq(vw_t),
                  grp(cfar), grp(band), tiles(selb), tiles(winb), full(overlap_t)],
        out_specs=pl.BlockSpec((1, T, gw), lambda b, g, i: (b, i, g)),
        scratch_shapes=[pltpu.VMEM((2 * n_pad + 2 * (T // CMP_STRIDE), NSA_R * T), F32),
                        pltpu.VMEM((S // SEL_BLOCK, T), F32),
                        pltpu.VMEM((2, T, NSA_R * T), BF16)],
        compiler_params=_cparams(("parallel", "parallel", "arbitrary")),
        name="nsa_attention",
    )(q4, gv, kcmp, vcmp_t, ks, vs_t, kw, vw_t, cfar, band, selb, winb, overlap_t)


def _moe_kernel(*refs, n_in, final):
    x_ref, mgate_ref = refs[:2]
    a_refs = refs[2:2 + n_in]
    wo_refs = refs[2 + n_in:2 + 2 * n_in]
    (g_ref, sh_ref, sc_ref, gate_ref, wr_ref, br_ref, before_ref, wg_ref, wu_ref, wd_ref, fg_ref,
     o_ref, x_all, hs_all, rts_all, acc_all, perm_t_all, meta_all) = refs[2 + 2 * n_in:]
    NG, PG, FH = MOE_GROUPS, MOE_PER_GROUP, MOE_HIDDEN
    TP = x_all.shape[0]
    s = pl.program_id(2)

    def prologue(hh):
        x_scr, hs_scr, rts_scr, acc_scr, perm_t_scr = (r.at[hh] for r in (x_all, hs_all, rts_all, acc_all, perm_t_all))
        meta = meta_all.at[hh]
        mix = functools.reduce(lambda a, b: a + b,
                               [_dot(a_ref[0].astype(BF16), wo_ref[...]) for a_ref, wo_ref in zip(a_refs, wo_refs)])
        x = x_ref[0] + mgate_ref[0] * mix
        x_scr[...] = x
        h = _modulated_norm(x, g_ref[...], sh_ref[0], sc_ref[0])
        h_hi = h.astype(BF16)
        h_lo = (h - h_hi.astype(F32)).astype(BF16)
        logits = (_dot(h_hi, wr_ref[0]) + _dot(h_lo, wr_ref[0]) + _dot(h_hi, wr_ref[1]) + br_ref[...]).T
        gl = [logits[NG * PG + g:NG * PG + g + 1, :] for g in range(NG)]
        gmax = functools.reduce(jnp.maximum, gl)
        gtop = jnp.full_like(gmax, float(NG - 1))
        for g in reversed(range(NG - 1)):
            gtop = jnp.where(gl[g] == gmax, float(g), gtop)
        p_g = 1.0 / functools.reduce(lambda a, b: a + b, [jnp.exp(v - gmax) for v in gl])
        a = []
        for j in range(PG):
            v = logits[(NG - 1) * PG + j:(NG - 1) * PG + j + 1, :]
            for g in reversed(range(NG - 1)):
                v = jnp.where(gtop == float(g), logits[g * PG + j:g * PG + j + 1, :], v)
            a.append(v)

        def first_max(vals):
            vmax = functools.reduce(jnp.maximum, vals)
            taken = jnp.zeros_like(vmax) > 1.0
            hits = []
            for v in vals:
                hit = (v == vmax) & jnp.logical_not(taken)
                taken = taken | hit
                hits.append(hit)
            return vmax, hits

        v1, hit1 = first_max(a)
        rest = [jnp.where(hh, -jnp.inf, v) for hh, v in zip(hit1, a)]
        v2, hit2 = first_max(rest)
        e2 = jnp.exp(v2 - v1)
        w1 = p_g / (1.0 + e2)
        w2 = p_g * e2 / (1.0 + e2)
        tm = gtop.shape[1]
        row = lax.broadcasted_iota(jnp.int32, (SUBLANES, tm), 0)
        onehot = [jnp.where(gtop == float(g), 1.0, 0.0) for g in range(NG)]
        oh8 = jnp.zeros((SUBLANES, tm), F32)
        for g in range(NG):
            oh8 = jnp.where(row == g, onehot[g], oh8)
        before = _dot(oh8.astype(BF16), before_ref[...])
        pos = jnp.zeros_like(gtop)
        off = jnp.int32(0)
        for g in range(NG):
            cnt = jnp.sum(onehot[g]).astype(jnp.int32)
            meta[g] = off
            meta[NG + g] = cnt
            pos = pos + onehot[g] * (before[g:g + 1, :] + off.astype(F32))
            off = off + cnt
        rt = jnp.where(row == PG, gtop, jnp.where(row == PG + 1, pos, 0.0))
        for j in range(PG):
            wj = jnp.where(hit1[j], w1, jnp.where(hit2[j], w2, 0.0))
            rt = jnp.where(row == j, wj, rt)
        rt_tok = jnp.concatenate([rt, jnp.zeros((LANES - SUBLANES, tm), F32)], axis=0).T
        rid = lax.broadcasted_iota(jnp.int32, (tm, tm), 0).astype(F32)
        cid = lax.broadcasted_iota(jnp.int32, (tm, tm), 1).astype(F32)
        perm = jnp.where(rid == pos, 1.0, 0.0).astype(BF16)
        perm_t_scr[...] = jnp.where(rt_tok[:, PG + 1:PG + 2] == cid, 1.0, 0.0).astype(BF16)
        r1 = rt_tok.astype(BF16)
        res = rt_tok - r1.astype(F32)
        r2 = res.astype(BF16)
        r3 = (res - r2.astype(F32)).astype(BF16)
        moved = _dot(perm, jnp.concatenate([h_hi, r1, r2, r3], axis=1))
        d = h_hi.shape[1]
        pad = hs_scr.shape[0] - tm
        hs_scr[0:tm, :] = moved[:, :d].astype(BF16)
        hs_scr[tm:, :] = jnp.zeros((pad, d), BF16)
        rts_scr[0:tm, :] = moved[:, d:d + LANES] + moved[:, d + LANES:d + 2 * LANES] + moved[:, d + 2 * LANES:]
        rts_scr[tm:, :] = jnp.full((pad, LANES), -1.0, F32)
        acc_scr[...] = jnp.zeros(acc_scr.shape, F32)

    tm = x_all.shape[1]
    WIN = hs_all.shape[1] - tm

    def experts(c):
        cf = c.astype(F32)
        bases, counts = [], []
        for hh in range(TP):
            off = meta_all[hh, c]
            cnt = meta_all[hh, NG + c]
            base = (off // MOE_ALIGN) * MOE_ALIGN
            bases.append(base)
            counts.append(jnp.where(cnt > 0, (off + cnt - base + WIN - 1) // WIN, 0))

        def win_body(w, carry):
            starts = [pl.multiple_of(jnp.where(w < counts[hh], bases[hh] + w * WIN, tm), MOE_ALIGN) for hh in range(TP)]
            hs = jnp.concatenate([hs_all[hh, pl.ds(starts[hh], WIN), :] for hh in range(TP)], axis=0)
            rt = jnp.concatenate([rts_all[hh, pl.ds(starts[hh], WIN), :] for hh in range(TP)], axis=0)
            in_group = rt[:, PG:PG + 1] == cf
            hid = _silu(_dot(hs, wg_ref[0])) * _dot(hs, wu_ref[0])
            parts = [hid[:, j * FH:(j + 1) * FH] * jnp.where(in_group, rt[:, j:j + 1], 0.0) for j in range(PG)]
            out = _dot(jnp.concatenate(parts, axis=1).astype(BF16), wd_ref[0])
            for hh in range(TP):
                acc_all[hh, pl.ds(starts[hh], WIN), :] += out[hh * WIN:(hh + 1) * WIN]
            return carry

        lax.fori_loop(0, functools.reduce(jnp.maximum, counts), win_body, 0)

    def epilogue(hh):
        ys = acc_all[hh, 0:tm, :]
        ys_hi = ys.astype(BF16)
        ys_lo = (ys - ys_hi.astype(F32)).astype(BF16)
        back = _dot(perm_t_all[hh], jnp.concatenate([ys_hi, ys_lo], axis=1))
        d = ys.shape[1]
        y = x_all[hh] + gate_ref[0] * (back[:, :d] + back[:, d:])
        if final:
            y = y * lax.rsqrt(jnp.mean(y * y, axis=-1, keepdims=True) + EPS) * fg_ref[...]
        o_ref[0] = y

    for hh in range(TP):
        pl.when(s == hh)(functools.partial(prologue, hh))
    pl.when((s >= TP - 1) & (s <= TP + NG - 2))(lambda: experts(s - (TP - 1)))
    for hh in range(TP):
        pl.when(s == NG + TP - 2 + hh)(functools.partial(epilogue, hh))


def _moe(x, mix_gate, acts, w_outs, g, shift, scale, gate, wg, bg, we, be, w_gate, w_up, w_down, final_g, final,
         tm=512):
    B, S, D = x.shape
    n_in = len(acts)
    NG, PG, FH = MOE_GROUPS, MOE_PER_GROUP, MOE_HIDDEN
    wr = jnp.zeros((D, LANES), F32)
    wr = wr.at[:, :NG * PG].set(we.reshape(D, NG * PG).astype(F32)).at[:, NG * PG:NG * PG + NG].set(wg.astype(F32))
    br = jnp.zeros((1, LANES), F32)
    br = br.at[0, :NG * PG].set(be.reshape(NG * PG).astype(F32)).at[0, NG * PG:NG * PG + NG].set(bg.astype(F32))
    wr_hi = wr.astype(BF16)
    wr = jnp.stack([wr_hi, (wr - wr_hi.astype(F32)).astype(BF16)])
    grp = lambda w: w.reshape(NG, PG, D, FH).transpose(0, 2, 1, 3).reshape(NG, D, PG * FH).astype(BF16)
    wd = w_down.reshape(NG, PG * FH, D).astype(BF16)
    ids = jnp.arange(tm)
    before = (ids[:, None] < ids[None, :]).astype(BF16)
    TP = MOE_TILES
    n_steps = NG + 2 * TP - 2
    vec = pl.BlockSpec((1, 1, D), lambda b, i, s: (b, 0, 0))
    row = pl.BlockSpec((1, D), lambda b, i, s: (0, 0))
    wspec = lambda k, n: pl.BlockSpec((1, k, n), lambda b, i, s: (jnp.clip(s - (TP - 1), 0, NG - 1), 0, 0))
    tokens_in = lambda n: pl.BlockSpec((1, tm, n), lambda b, i, s: (b, i * TP + jnp.minimum(s, TP - 1), 0))
    tokens_out = pl.BlockSpec((1, tm, D), lambda b, i, s: (b, i * TP + jnp.clip(s - (NG + TP - 2), 0, TP - 1), 0))
    const = lambda a: pl.BlockSpec(a.shape, lambda b, i, s: (0,) * a.ndim)
    return pl.pallas_call(
        functools.partial(_moe_kernel, n_in=n_in, final=final),
        out_shape=jax.ShapeDtypeStruct((B, S, D), F32),
        grid=(B, S // (tm * TP), n_steps),
        in_specs=[tokens_in(D), vec] + [tokens_in(a.shape[2]) for a in acts] + [const(w) for w in w_outs]
                 + [row, vec, vec, vec, const(wr), const(br), const(before),
                    wspec(D, PG * FH), wspec(D, PG * FH), wspec(PG * FH, D), row],
        out_specs=tokens_out,
        scratch_shapes=[pltpu.VMEM((TP, tm, D), F32), pltpu.VMEM((TP, tm + MOE_WIN, D), BF16),
                        pltpu.VMEM((TP, tm + MOE_WIN, LANES), F32), pltpu.VMEM((TP, tm + MOE_WIN, D), F32),
                        pltpu.VMEM((TP, tm, tm), BF16), pltpu.SMEM((TP, 2 * NG), jnp.int32)],
        compiler_params=_cparams(("parallel", "parallel", "arbitrary")),
        name="moe",
    )(x, mix_gate, *acts, *w_outs, g.reshape(1, D), shift, scale, gate, wr, br, before, grp(w_gate), grp(w_up), wd,
      final_g.reshape(1, D))


def _mlstm_s5_layer(x, g, shift, scale, w_in, conv_w, b_i, b_f, head_g, s5_params, w_out):
    H = MLSTM_HEADS
    A = MIX_A
    w_if = jnp.zeros((D_MODEL, LANES), F32).at[:, :2 * H].set(w_in[:, 4 * A:4 * A + 2 * H])
    weights = [w_in[:, :2 * A], w_in[:, 2 * A:4 * A], w_if, w_in[:, 4 * A + 2 * H:]]
    qk, vo, ifg, u = _norm_matmul(x, g, shift, scale, [w.astype(BF16) for w in weights], [BF16, BF16, F32, F32])
    gate_bias = jnp.zeros((1, LANES), F32).at[0, :H].set(b_i.astype(F32)).at[0, H:2 * H].set(b_f.astype(F32))
    hm = _mlstm(qk, vo, ifg, conv_w.astype(F32), gate_bias, head_g.reshape(1, A).astype(F32))
    ys = _s5s(u, _s5s_tables(*s5_params))
    w_out = w_out.astype(BF16)
    return [hm, ys], [w_out[:A], w_out[A:]]


def _nsa_layer(x, g, shift, scale, w_in, b_gate, cmp_pos, cmp_w1, cmp_b1, cmp_w2, cmp_b2, rel_bias, w_out):
    B, S, D = x.shape
    KV, R, DH = NSA_KV, NSA_R, NSA_DH
    w_g = jnp.zeros((D, KV, LANES), F32).at[:, :, :3 * R].set(w_in[:, D + 6 * KV_W:].reshape(D, KV, 3 * R))
    b_g = jnp.zeros((KV, LANES), F32).at[:, :3 * R].set(b_gate.reshape(KV, 3 * R).astype(F32))
    kv_cols = lambda i: w_in[:, D + i * KV_W:D + (i + 1) * KV_W]
    w_k = jnp.concatenate([kv_cols(0), kv_cols(2), kv_cols(4)], axis=1)
    w_v = jnp.concatenate([kv_cols(1), kv_cols(3), kv_cols(5)], axis=1)
    weights = [w_in[:, :D], w_k, w_v, w_g.reshape(D, KV * LANES)]
    q4, gv, kc, vc, ks, kw, vs_t, vw_t = _nsa_proj(x, g, shift, scale, [w.astype(BF16) for w in weights],
                                                   b_g.reshape(1, KV * LANES))
    grp = CMP_STRIDE
    xg = jnp.stack([kc, vc]).reshape(2, B, KV * S // grp, grp * DH)
    cmp = _compress(xg, cmp_pos, cmp_w1, cmp_b1, cmp_w2, cmp_b2).reshape(2, B, KV, S // grp, DH).astype(BF16)
    out = _nsa_t_attention(q4, gv, cmp[0], cmp[1].transpose(0, 1, 3, 2), ks, vs_t, kw, vw_t,
                           _nsa_t_tables(rel_bias, S))
    return [out], [w_out.astype(BF16)]


def kernel(x, c, rel_bias, ada_w, ada_b, norm_g, final_g,
           a_w_in, a_conv, a_b_i, a_b_f, a_head_g,
           s5_lam_re, s5_lam_im, s5_log_dt, s5_b_re, s5_b_im, s5_c_re, s5_c_im,
           s5_d, s5_glu_w, s5_glu_b, a_w_out,
           n_w_in, n_b_gate, n_cmp_pos, n_cmp_w1, n_cmp_b1, n_cmp_w2, n_cmp_b2, n_w_out,
           r_grp_w, r_grp_b, r_exp_w, r_exp_b, e_w_gate, e_w_up, e_w_down):
    B, S, D = x.shape
    mod = _ada_mod(c, ada_w, ada_b).reshape(DEPTH, 2, B, 1, 3 * D)
    split = lambda m: (m[..., :D], m[..., D:2 * D], m[..., 2 * D:])
    for layer in range(DEPTH):
        shift, scale, mix_gate = split(mod[layer, 0])
        j = layer // 2
        if layer % 2 == 0:
            s5_params = (s5_lam_re[j], s5_lam_im[j], s5_log_dt[j], s5_b_re[j], s5_b_im[j],
                         s5_c_re[j], s5_c_im[j], s5_d[j], s5_glu_w[j], s5_glu_b[j])
            acts, w_outs = _mlstm_s5_layer(x, norm_g[layer, 0], shift, scale, a_w_in[j], a_conv[j], a_b_i[j],
                                           a_b_f[j], a_head_g[j], s5_params, a_w_out[j])
        else:
            acts, w_outs = _nsa_layer(x, norm_g[layer, 0], shift, scale, n_w_in[j], n_b_gate[j], n_cmp_pos[j],
                                      n_cmp_w1[j], n_cmp_b1[j], n_cmp_w2[j], n_cmp_b2[j], rel_bias, n_w_out[j])
        shift, scale, gate = split(mod[layer, 1])
        x = _moe(x, mix_gate, acts, w_outs, norm_g[layer, 1], shift, scale, gate, r_grp_w[layer], r_grp_b[layer],
                 r_exp_w[layer], r_exp_b[layer], e_w_gate[layer], e_w_up[layer], e_w_down[layer], final_g,
                 final=(layer == DEPTH - 1))
    return x
```

```python
import functools
import math

import jax
import jax.numpy as jnp
from jax import lax
from jax.experimental import pallas as pl
from jax.experimental.pallas import tpu as pltpu

F32 = jnp.float32
BF16 = jnp.bfloat16
HIGHEST = lax.Precision.HIGHEST

D_MODEL = 1024
DEPTH = 2
MIX_A = 512
MLSTM_HEADS = 4
MLSTM_DH = MIX_A // MLSTM_HEADS
MLSTM_CHUNK = 128
CONV_K = 4
S5_GROUP = 16
S5_STATE = 64
S5_CHUNK = 16
NSA_HEADS = 16
NSA_KV = 4
NSA_R = NSA_HEADS // NSA_KV
NSA_DH = D_MODEL // NSA_HEADS
KV_W = NSA_KV * NSA_DH
CMP_BLOCK = 32
CMP_STRIDE = 16
CMP_HIDDEN = 256
SEL_BLOCK = 64
SEL_TOPK = 16
WINDOW = 512
FORCE = 1e9
REL_BUCKETS = 32
REL_MAX_DIST = 128
MOE_GROUPS = 4
MOE_PER_GROUP = 4
MOE_HIDDEN = 256
EPS = 1e-6
NEG = -1e30
BIG = 1e30
LOG2E = math.log2(math.e)

LANES = 128
SUBLANES = 8
ATT_TILE = 256
MOE_WIN = 160
MOE_ALIGN = 16
MOE_TILES = 2
VMEM_LIMIT = 56 * 1024 * 1024


def _cparams(sem):
    return pltpu.CompilerParams(dimension_semantics=sem, vmem_limit_bytes=VMEM_LIMIT)


def _dot(a, b, precision=None):
    return jnp.dot(a, b, preferred_element_type=F32, precision=precision)


def _dot_nt(a, b):
    return lax.dot_general(a, b, (((1,), (1,)), ((), ())), preferred_element_type=F32)


def _sigmoid(x):
    return 1.0 / (1.0 + jnp.exp(-x))


def _silu(x):
    return x * _sigmoid(x)


def _gelu_tanh(x):
    return 0.5 * x * (1.0 + jnp.tanh(math.sqrt(2.0 / math.pi) * (x + 0.044715 * (x * x * x))))


def _modulated_norm(x, g, shift, scale):
    y = x * lax.rsqrt(jnp.mean(x * x, axis=-1, keepdims=True) + EPS) * g
    return y * (1.0 + scale) + shift


def _ada_kernel(c_ref, w_ref, b_ref, o_ref):
    c = c_ref[...]
    o_ref[0] = _dot(_silu(c), w_ref[0]) + b_ref[0]


def _ada_mod(c, ada_w, ada_b):
    B, D = c.shape
    n_mod = ada_w.shape[0] * ada_w.shape[1]
    w = ada_w.reshape(n_mod, D, 3 * D)
    b = ada_b.reshape(n_mod, 1, 3 * D)
    tn = 1024
    return pl.pallas_call(
        _ada_kernel,
        out_shape=jax.ShapeDtypeStruct((n_mod, B, 3 * D), F32),
        grid=(n_mod, 3 * D // tn),
        in_specs=[pl.BlockSpec((B, D), lambda i, j: (0, 0)),
                  pl.BlockSpec((1, D, tn), lambda i, j: (i, 0, j)),
                  pl.BlockSpec((1, 1, tn), lambda i, j: (i, 0, j))],
        out_specs=pl.BlockSpec((1, B, tn), lambda i, j: (i, 0, j)),
        compiler_params=_cparams(("parallel", "parallel")),
        name="ada_mod",
    )(c, w, b)


def _norm_mm_kernel(*refs, n_w):
    x_ref, g_ref, sh_ref, sc_ref = refs[:4]
    w_refs = refs[4:4 + n_w]
    o_refs = refs[4 + n_w:]
    h = _modulated_norm(x_ref[0], g_ref[...], sh_ref[0], sc_ref[0]).astype(BF16)
    for w_ref, o_ref in zip(w_refs, o_refs):
        o_ref[0] = _dot(h, w_ref[...]).astype(o_ref.dtype)


def _norm_matmul(x, g, shift, scale, weights, out_dtypes, tm=512):
    B, S, D = x.shape
    n_w = len(weights)
    vec = pl.BlockSpec((1, 1, D), lambda b, i: (b, 0, 0))
    in_specs = [pl.BlockSpec((1, tm, D), lambda b, i: (b, i, 0)),
                pl.BlockSpec((1, D), lambda b, i: (0, 0)), vec, vec]
    in_specs += [pl.BlockSpec(w.shape, lambda b, i: (0, 0)) for w in weights]
    return pl.pallas_call(
        functools.partial(_norm_mm_kernel, n_w=n_w),
        out_shape=[jax.ShapeDtypeStruct((B, S, w.shape[1]), dt) for w, dt in zip(weights, out_dtypes)],
        grid=(B, S // tm),
        in_specs=in_specs,
        out_specs=[pl.BlockSpec((1, tm, w.shape[1]), lambda b, i: (b, i, 0)) for w in weights],
        compiler_params=_cparams(("parallel", "parallel")),
        name="norm_matmul",
    )(x, g.reshape(1, D), shift, scale, *weights)


def _mlstm_kernel(qk_ref, vo_ref, if_ref, cw_ref, gb_ref, hg_ref, tril_ref, o_ref,
                  xbuf, c_scr, n_scr, m_scr):
    pad = SUBLANES

    @pl.when(pl.program_id(1) == 0)
    def _():
        xbuf[:, 0:pad, :] = jnp.zeros((xbuf.shape[0], pad, 2 * MIX_A), F32)
        c_scr[...] = jnp.zeros_like(c_scr)
        n_scr[...] = jnp.zeros_like(n_scr)
        m_scr[...] = jnp.zeros_like(m_scr)

    for bb in range(qk_ref.shape[0]):
        _mlstm_chunk(qk_ref.at[bb], vo_ref.at[bb], if_ref.at[bb], cw_ref, gb_ref, hg_ref, tril_ref, o_ref.at[bb],
                     xbuf.at[bb], c_scr.at[bb], n_scr.at[bb], m_scr.at[bb])


def _mlstm_chunk(qk_ref, vo_ref, if_ref, cw_ref, gb_ref, hg_ref, tril_ref, o_ref, xbuf, c_scr, n_scr, m_scr):
    L, H, DH = MLSTM_CHUNK, MLSTM_HEADS, MLSTM_DH
    pad = SUBLANES
    xbuf[pad:pad + L, :] = qk_ref[...].astype(F32)
    cw = cw_ref[...]
    conv = None
    for j in range(CONV_K):
        lo = pad - (CONV_K - 1) + j
        t = xbuf[lo:lo + L, :] * cw[j:j + 1, :]
        conv = t if conv is None else conv + t
    xbuf[0:pad, :] = xbuf[L:L + pad, :]
    qk = _silu(conv)
    q = qk[:, :MIX_A]
    k = qk[:, MIX_A:] * (DH ** -0.5)
    vo = vo_ref[...].astype(F32)
    v = vo[:, :MIX_A]
    o_pre = vo[:, MIX_A:]

    ifb = if_ref[...] + gb_ref[...]
    lf = jnp.minimum(ifb, 0.0) - jnp.log1p(jnp.exp(-jnp.abs(ifb)))
    bcs = _dot(tril_ref[...], lf, precision=HIGHEST)
    ifb_t = ifb.T
    bcs_t = bcs.T
    row = lax.broadcasted_iota(jnp.int32, (L, L), 0)
    col = lax.broadcasted_iota(jnp.int32, (L, L), 1)
    causal = col <= row

    outs = []
    for h in range(H):
        sl = slice(h * DH, (h + 1) * DH)
        qh, kh, vh = q[:, sl], k[:, sl], v[:, sl]
        qb, kb = qh.astype(BF16), kh.astype(BF16)
        b_col = bcs[:, H + h:H + h + 1]
        b_row = bcs_t[H + h:H + h + 1, :]
        li_col = ifb[:, h:h + 1]
        li_row = ifb_t[h:h + 1, :]
        b_last = b_col[L - 1:L, :]
        m0 = m_scr[h][:, 0:1]
        c0 = c_scr[h]
        n0 = n_scr[h]

        log_d = jnp.where(causal, b_col - b_row + li_row, NEG)
        log_inter = b_col + m0
        m_t = jnp.maximum(log_inter, jnp.max(log_d, axis=1, keepdims=True))
        dmat = jnp.exp(log_d - m_t)
        a_inter = jnp.exp(log_inter - m_t)
        s = _dot_nt(qb, kb) * dmat
        num = _dot(s.astype(BF16), vh.astype(BF16)) + a_inter * _dot_nt(qb, c0.astype(BF16))
        den = jnp.sum(s, axis=1, keepdims=True) + a_inter * jnp.sum(qh * n0, axis=1, keepdims=True)
        hh = num / jnp.maximum(jnp.abs(den), jnp.exp(-m_t))

        w_col = b_last - b_col + li_col
        m_loc = jnp.max(w_col, axis=0, keepdims=True)
        e = jnp.exp(w_col - m_loc)
        c_loc = _dot((vh * e).T.astype(BF16), kb)
        n_loc = jnp.sum(kh * e, axis=0, keepdims=True)
        m_new = jnp.maximum(b_last + m0, m_loc)
        a = jnp.exp(b_last + m0 - m_new)
        sc = jnp.exp(m_loc - m_new)
        c_scr[h] = a * c0 + sc * c_loc
        n_scr[h] = a * n0 + sc * n_loc
        m_scr[h] = jnp.broadcast_to(m_new, (1, LANES))

        outs.append(hh * lax.rsqrt(jnp.mean(hh * hh, axis=1, keepdims=True) + EPS))
    hm = jnp.concatenate(outs, axis=1)
    o_ref[...] = (_sigmoid(o_pre) * (hm * hg_ref[...])).astype(o_ref.dtype)


def _mlstm(qk, vo, ifg, conv_w, gate_bias, head_g):
    B, S, _ = qk.shape
    rows = 1
    L, H, DH = MLSTM_CHUNK, MLSTM_HEADS, MLSTM_DH
    tril = jnp.tril(jnp.ones((L, L), F32))
    return pl.pallas_call(
        _mlstm_kernel,
        out_shape=jax.ShapeDtypeStruct((B, S, MIX_A), BF16),
        grid=(B // rows, S // L),
        in_specs=[pl.BlockSpec((rows, L, 2 * MIX_A), lambda b, c: (b, c, 0)),
                  pl.BlockSpec((rows, L, 2 * MIX_A), lambda b, c: (b, c, 0)),
                  pl.BlockSpec((rows, L, LANES), lambda b, c: (b, c, 0)),
                  pl.BlockSpec((CONV_K, 2 * MIX_A), lambda b, c: (0, 0)),
                  pl.BlockSpec((1, LANES), lambda b, c: (0, 0)),
                  pl.BlockSpec((1, MIX_A), lambda b, c: (0, 0)),
                  pl.BlockSpec((L, L), lambda b, c: (0, 0))],
        out_specs=pl.BlockSpec((rows, L, MIX_A), lambda b, c: (b, c, 0)),
        scratch_shapes=[pltpu.VMEM((rows, L + SUBLANES, 2 * MIX_A), F32),
                        pltpu.VMEM((rows, H, DH, DH), F32),
                        pltpu.VMEM((rows, H, 1, DH), F32),
                        pltpu.VMEM((rows, H, 1, LANES), F32)],
        compiler_params=_cparams(("parallel", "arbitrary")),
        name="mlstm",
    )(qk, vo, ifg, conv_w, gate_bias, head_g, tril)


S5_LT = LANES // S5_GROUP
S5_PAIRS = S5_CHUNK // 2


def _s5s_kernel(u_ref, h_ref, e_ref, kk_ref, are_ref, aim_ref, d_ref, gw_ref, gb_ref, o_ref, xl_scr, x0_scr):
    n_chunks = u_ref.shape[1] // S5_CHUNK
    half = S5_LT * S5_STATE
    tok = lambda s: u_ref[0, pl.ds(s, n_chunks, stride=S5_CHUNK), :]
    u2 = [jnp.concatenate([tok(2 * q), tok(2 * q + 1)], axis=1) for q in range(S5_PAIRS)]
    u2b = [v.astype(BF16) for v in u2]
    xl_scr[...] = functools.reduce(lambda a, b: a + b, [_dot(u2b[q], h_ref[0, q]) for q in range(S5_PAIRS)])
    a_re = are_ref[0]
    a_im = aim_ref[0]

    def body(a, carry):
        re, im = carry
        x0_scr[pl.ds(a, 1), 0:half] = re
        x0_scr[pl.ds(a, 1), half:2 * half] = im
        return (a_re * re - a_im * im + xl_scr[pl.ds(a, 1), 0:half],
                a_re * im + a_im * re + xl_scr[pl.ds(a, 1), half:2 * half])

    zero = jnp.zeros((1, half), F32)
    lax.fori_loop(0, n_chunks, body, (zero, zero), unroll=8)
    x0 = x0_scr[...].astype(BF16)
    for p in range(S5_PAIRS):
        y = _dot(x0, e_ref[0, p]) + u2[p] * d_ref[0]
        for q in range(p + 1):
            y = y + _dot(u2b[q], kk_ref[0, p - q])
        ys = _gelu_tanh(y)
        out = ys * _sigmoid(_dot(ys.astype(BF16), gw_ref[0]) + gb_ref[0])
        o_ref[0, pl.ds(2 * p, n_chunks, stride=S5_CHUNK), :] = out[:, :LANES].astype(o_ref.dtype)
        o_ref[0, pl.ds(2 * p + 1, n_chunks, stride=S5_CHUNK), :] = out[:, LANES:].astype(o_ref.dtype)


def _s5s_tables(lam_re, lam_im, log_dt, b_re, b_im, c_re, c_im, d_skip, glu_w, glu_b):
    T, C, P, LT = S5_CHUNK, S5_GROUP, S5_STATE, S5_LT
    G = lam_re.shape[0]
    NT = G // LT
    lam = lax.complex(lam_re.astype(F32), lam_im.astype(F32))
    dt = jnp.exp(log_dt.astype(F32))[:, None]
    lam_bar = jnp.exp(lam * dt)
    b_bar = ((lam_bar - 1.0) / lam)[..., None] * lax.complex(b_re.astype(F32), b_im.astype(F32))
    c_mat = lax.complex(c_re.astype(F32), c_im.astype(F32))
    taus = jnp.arange(T + 1, dtype=F32)
    pw = jnp.exp((lam * dt)[:, None, :] * taus[None, :, None])
    eye = jnp.eye(LT, dtype=F32)
    tiles = lambda a: a.reshape((NT, LT) + a.shape[1:])

    kern = jnp.einsum('gcp,gtp,gpd->gtdc', c_mat, pw[:, :T], b_bar, precision=HIGHEST).real
    kblk = jnp.einsum('nitdc,ij->ntidjc', tiles(kern), eye).reshape(NT, T, LANES, LANES)
    kblk = jnp.concatenate([jnp.zeros_like(kblk[:, :1]), kblk], axis=1)
    kk = jnp.stack([jnp.concatenate([jnp.concatenate([kblk[:, 2 * d + 1], kblk[:, 2 * d + 2]], axis=2),
                                     jnp.concatenate([kblk[:, 2 * d], kblk[:, 2 * d + 1]], axis=2)], axis=1)
                    for d in range(T // 2)], axis=1)

    hmat = pw[:, :T][:, ::-1, :, None] * b_bar[:, None]

    def state_cols(m):
        return jnp.einsum('nispc,ij->nsicjp', tiles(m), eye).reshape(NT, T, LANES, LT * P)

    h = jnp.concatenate([state_cols(hmat.real), state_cols(hmat.imag)], axis=3)
    h2 = h.reshape(NT, T // 2, 2 * LANES, 2 * LT * P)

    emat = c_mat[:, None] * pw[:, 1:][:, :, None, :]

    def state_rows(m):
        return jnp.einsum('nitcp,ij->ntjpic', tiles(m), eye).reshape(NT, T, LT * P, LANES)

    e = jnp.concatenate([state_rows(emat.real), state_rows(-emat.imag)], axis=2)
    e2 = e.reshape(NT, T // 2, 2, 2 * LT * P, LANES).transpose(0, 1, 3, 2, 4).reshape(NT, T // 2, 2 * LT * P, 2 * LANES)

    a_re = pw[:, T].real.reshape(NT, 1, LT * P)
    a_im = pw[:, T].imag.reshape(NT, 1, LT * P)
    pair = lambda v: jnp.tile(v.astype(F32).reshape(NT, 1, LANES), (1, 1, 2))
    gwb = jnp.einsum('nice,ij->nicje', tiles(glu_w.astype(F32)), eye).reshape(NT, LANES, LANES)
    zeros = jnp.zeros_like(gwb)
    gw2 = jnp.concatenate([jnp.concatenate([gwb, zeros], axis=2), jnp.concatenate([zeros, gwb], axis=2)], axis=1)
    return (h2.astype(BF16), e2.astype(BF16), kk.astype(BF16), a_re, a_im, pair(d_skip), gw2.astype(BF16), pair(glu_b))


def _s5s(u, tables):
    B, S, W = u.shape
    NT = W // LANES
    n_chunks = S // S5_CHUNK
    per_tile = lambda a: pl.BlockSpec((1,) + a.shape[1:], lambda j, b: (j,) + (0,) * (a.ndim - 1))
    return pl.pallas_call(
        _s5s_kernel,
        out_shape=jax.ShapeDtypeStruct((B, S, W), F32),
        grid=(NT, B),
        in_specs=[pl.BlockSpec((1, S, LANES), lambda j, b: (b, 0, j))] + [per_tile(t) for t in tables],
        out_specs=pl.BlockSpec((1, S, LANES), lambda j, b: (b, 0, j)),
        scratch_shapes=[pltpu.VMEM((n_chunks, 2 * S5_LT * S5_STATE), F32) for _ in range(2)],
        compiler_params=_cparams(("parallel", "parallel")),
        name="s5",
    )(u, *tables)


def _compress_kernel(x_ref, plo_ref, phi_ref, w1_ref, b1_ref, w2_ref, b2_ref, o_ref):
    x = x_ref[0, 0]
    half = x.shape[1]
    w1 = w1_ref[0]
    lo = _dot((x + plo_ref[0]).astype(BF16), w1[:half])
    hi = _dot((x + phi_ref[0]).astype(BF16), w1[half:])
    rows = x.shape[0]
    hid = _gelu_tanh(lo + pltpu.roll(hi, rows - 1, 0) + b1_ref[0])
    o_ref[0, 0] = _dot(hid.astype(BF16), w2_ref[0]) + b2_ref[0]


def _compress(xg, pos, w1, b1, w2, b2):
    _, B, rows, width = xg.shape
    pos_flat = pos.reshape(2, 2, 1, width).astype(F32)
    sel = lambda shape: pl.BlockSpec((1,) + shape, lambda j, b: (j, 0, 0))
    return pl.pallas_call(
        _compress_kernel,
        out_shape=jax.ShapeDtypeStruct((2, B, rows, NSA_DH), F32),
        grid=(2, B),
        in_specs=[pl.BlockSpec((1, 1, rows, width), lambda j, b: (j, b, 0, 0)),
                  sel((1, width)), sel((1, width)),
                  sel((2 * width, CMP_HIDDEN)), sel((1, CMP_HIDDEN)),
                  sel((CMP_HIDDEN, NSA_DH)), sel((1, NSA_DH))],
        out_specs=pl.BlockSpec((1, 1, rows, NSA_DH), lambda j, b: (j, b, 0, 0)),
        compiler_params=_cparams(("parallel", "parallel")),
        name="nsa_compress",
    )(xg, pos_flat[:, 0], pos_flat[:, 1], w1.astype(BF16), b1[:, None].astype(F32),
      w2.astype(BF16), b2[:, None].astype(F32))


def _t5_bucket(dist):
    dist = jnp.maximum(dist, 0)
    max_exact = REL_BUCKETS // 2
    log_ratio = jnp.log(jnp.maximum(dist, 1).astype(F32) / max_exact) / math.log(REL_MAX_DIST / max_exact)
    large = jnp.minimum(max_exact + (log_ratio * (REL_BUCKETS - max_exact)).astype(jnp.int32), REL_BUCKETS - 1)
    return jnp.where(dist < max_exact, dist, large)


def _nsa_proj_kernel(x_ref, g_ref, sh_ref, sc_ref, wq_ref, wk_ref, wv_ref, wg_ref, bg_ref,
                     q4_ref, gv_ref, kc_ref, vc_ref, ks_ref, kw_ref, vst_ref, vwt_ref):
    KV, R, DH, T = NSA_KV, NSA_R, NSA_DH, ATT_TILE
    h = _modulated_norm(x_ref[0], g_ref[...], sh_ref[0], sc_ref[0]).astype(BF16)
    q_t = (_dot(h, wq_ref[...]) * (DH ** -0.5 * LOG2E)).T.astype(BF16)
    gates_t = _sigmoid(_dot(h, wg_ref[...]) + bg_ref[...]).T
    row = lax.broadcasted_iota(jnp.int32, (SUBLANES, R * T), 0)
    for g in range(KV):
        q4_ref[0, g, 0] = jnp.concatenate([q_t[(g * R + r) * DH:(g * R + r + 1) * DH] for r in range(R)], axis=1)
        gv = jnp.zeros((SUBLANES, R * T), F32)
        for j in range(3):
            gj = jnp.concatenate([gates_t[g * LANES + 3 * r + j:g * LANES + 3 * r + j + 1] for r in range(R)], axis=1)
            gv = jnp.where(row == j, gj, gv)
        gv_ref[0, g, 0] = gv
    k3 = _dot(h, wk_ref[...])
    v3 = _dot(h, wv_ref[...])
    vs_t = v3[:, KV_W:2 * KV_W].T.astype(BF16)
    vw_t = v3[:, 2 * KV_W:].T.astype(BF16)
    for g in range(KV):
        cols = slice(g * DH, (g + 1) * DH)
        kc_ref[0, g] = k3[:, cols].astype(BF16)
        vc_ref[0, g] = v3[:, cols].astype(BF16)
        ks_ref[0, g] = k3[:, KV_W + g * DH:KV_W + (g + 1) * DH].astype(BF16)
        kw_ref[0, g] = k3[:, 2 * KV_W + g * DH:2 * KV_W + (g + 1) * DH].astype(BF16)
        vst_ref[0, g, 0] = vs_t[cols]
        vwt_ref[0, g, 0] = vw_t[cols]


def _nsa_proj(x, g, shift, scale, weights, b_gate):
    B, S, D = x.shape
    KV, R, DH, T = NSA_KV, NSA_R, NSA_DH, ATT_TILE
    vec = pl.BlockSpec((1, 1, D), lambda b, i: (b, 0, 0))
    keys = pl.BlockSpec((1, KV, T, DH), lambda b, i: (b, 0, i, 0))
    key_shape = jax.ShapeDtypeStruct((B, KV, S, DH), BF16)
    tile = lambda rows, width: pl.BlockSpec((1, KV, 1, rows, width), lambda b, i: (b, 0, i, 0, 0))
    tile_shape = lambda rows, width, dt: jax.ShapeDtypeStruct((B, KV, S // T, rows, width), dt)
    return pl.pallas_call(
        _nsa_proj_kernel,
        out_shape=[tile_shape(DH, R * T, BF16), tile_shape(SUBLANES, R * T, F32),
                   key_shape, key_shape, key_shape, key_shape,
                   tile_shape(DH, T, BF16), tile_shape(DH, T, BF16)],
        grid=(B, S // T),
        in_specs=[pl.BlockSpec((1, T, D), lambda b, i: (b, i, 0)),
                  pl.BlockSpec((1, D), lambda b, i: (0, 0)), vec, vec]
                 + [pl.BlockSpec(w.shape, lambda b, i: (0, 0)) for w in weights]
                 + [pl.BlockSpec(b_gate.shape, lambda b, i: (0, 0))],
        out_specs=[tile(DH, R * T), tile(SUBLANES, R * T), keys, keys, keys, keys, tile(DH, T), tile(DH, T)],
        compiler_params=_cparams(("parallel", "parallel")),
        name="nsa_proj",
    )(x, g.reshape(1, D), shift, scale, *weights, b_gate)


def _nsa_t_kernel(q4_ref, gv_ref, kc_ref, vct_ref, ks_ref, vst_ref, kw_ref, vwt_ref,
                  cfar_ref, band_ref, selb_ref, winb_ref, ovt_ref, o_ref, s_scr, sel_scr, sbuf):
    T = ATT_TILE
    R, DH = NSA_R, NSA_DH
    qi = pl.program_id(2)
    q0 = qi * T
    n_pad = kc_ref.shape[2]
    n_sel = ovt_ref.shape[0]
    n_far = selb_ref.shape[0] - 1
    n_win = winb_ref.shape[0] - 2
    band_rows = band_ref.shape[2] - T // CMP_STRIDE * 2

    q4 = q4_ref[0, 0, 0]
    t_lane = q0 + lax.broadcasted_iota(jnp.int32, (1, R * T), 1) % T

    ones_rows = DH
    with_ones = lambda v_t: jnp.concatenate([v_t, jnp.ones((ones_rows, v_t.shape[1]), v_t.dtype)], axis=0)
    gvec = lambda j: gv_ref[0, 0, 0, j:j + 1, :]

    grp = T // CMP_STRIDE
    s_scr[0:n_pad, :] = _dot(kc_ref[0, 0], q4) + cfar_ref[0]
    s_scr[n_pad:n_pad + 2 * grp, :] = jnp.zeros((2 * grp, R * T), F32)
    r0 = jnp.maximum(qi * grp - 2 * grp, 0)
    x0 = r0 - (qi * grp - 2 * grp)
    r0 = pl.multiple_of(r0, SUBLANES)
    x0 = pl.multiple_of(x0, SUBLANES)
    s_scr[pl.ds(r0, band_rows), :] += band_ref[0, 0, pl.ds(x0, band_rows), :]
    lim = pl.multiple_of(qi * grp + 2 * grp, SUBLANES)
    s_scr[pl.ds(lim, n_pad), :] = jnp.full((n_pad, R * T), NEG, F32)

    w_subs, w_vals = [], []
    for d in range(n_win + 1):
        kt = jnp.maximum(qi - d, 0)
        off = pl.multiple_of(kt * T, T)
        tile = jnp.where(qi >= d, d, n_win + 1)
        w_subs.append((_dot(kw_ref[0, 0, pl.ds(off, T), :], q4) + winb_ref[tile, 0]).astype(BF16))
        w_vals.append(with_ones(vwt_ref[0, 0, kt]))

    s = s_scr[0:n_pad, :]
    e = jnp.exp2(s - jnp.max(s, axis=0, keepdims=True))
    inv = jnp.where(t_lane >= CMP_BLOCK - 1, 1.0 / jnp.sum(e, axis=0, keepdims=True), 0.0)
    p = e * inv
    o_cmp = _dot(vct_ref[0, 0], p.astype(BF16))
    psum = functools.reduce(lambda a, b: a + b, [p[:, r * T:(r + 1) * T] for r in range(R)])

    m_w = jnp.max(functools.reduce(jnp.maximum, w_subs), axis=0, keepdims=True)
    acc = functools.reduce(lambda a, b: a + b,
                           [_dot(vj, jnp.exp2(sj - m_w)) for sj, vj in zip(w_subs, w_vals)])
    o_win = acc[:DH] * (1.0 / acc[DH:DH + 1])
    out_t = gvec(0) * o_cmp + gvec(2) * o_win

    imp_t = _dot(ovt_ref[...], psum, precision=HIGHEST)
    jj = lax.broadcasted_iota(jnp.int32, (n_sel, T), 0)
    blk_t = (q0 + lax.broadcasted_iota(jnp.int32, (1, T), 1)) // SEL_BLOCK
    forced = (jj == 0) | (jj == blk_t) | (jj == blk_t - 1)
    score = jnp.where(forced, FORCE, jnp.where(jj <= blk_t, imp_t, -1.0))
    n_blk = n_sel // SUBLANES
    rows = [score[v * SUBLANES:(v + 1) * SUBLANES] for v in range(n_blk)]
    cnts = [jnp.zeros((SUBLANES, T), F32) for _ in range(n_blk)]
    sub = lax.broadcasted_iota(jnp.int32, (SUBLANES, T), 0)
    for j2 in range(n_sel):
        c2 = score[j2:j2 + 1, :]
        for v in range(n_blk):
            lo = v * SUBLANES
            if lo > j2:
                beats = c2 >= rows[v]
            elif lo + SUBLANES - 1 <= j2:
                beats = c2 > rows[v]
            else:
                beats = (c2 > rows[v]) | ((c2 >= rows[v]) & (sub > j2 - lo))
            cnts[v] = cnts[v] + jnp.where(beats, 1.0, 0.0)
    cnt = jnp.concatenate(cnts, axis=0)
    chosen = (cnt < float(min(SEL_TOPK, n_sel))) & (jj <= blk_t)
    sel_scr[...] = jnp.where(chosen, 0.0, -BIG)

    def block_mask(kt):
        per_tile = T // SEL_BLOCK
        parts = [jnp.broadcast_to(sel_scr[pl.ds(kt * per_tile + i, 1), :], (SEL_BLOCK, T)) for i in range(per_tile)]
        m1 = jnp.concatenate(parts, axis=0)
        return jnp.concatenate([m1] * R, axis=1)

    def sel_scores(slot, kc):
        off = pl.multiple_of(kc * T, T)
        s = _dot(ks_ref[0, 0, pl.ds(off, T), :], q4)
        s = (s + selb_ref[jnp.clip(qi - kc, 0, n_far), 0] + block_mask(kc)).astype(BF16)
        sbuf[slot] = s
        return jnp.max(s, axis=0, keepdims=True).astype(F32)

    def sel_weighted(slot, kc, m_new):
        return _dot(with_ones(vst_ref[0, 0, kc]), jnp.exp2(sbuf[slot] - m_new.astype(BF16)))

    last_tile = vst_ref.shape[2] - 1

    def sel_body(i, carry):
        m, acc, m_even = carry
        m_odd = sel_scores(1, 2 * i + 1)
        m_new = jnp.maximum(m, m_even)
        acc = jnp.exp2(m - m_new) * acc + sel_weighted(0, 2 * i, m_new)
        m_even = sel_scores(0, jnp.minimum(2 * i + 2, last_tile))
        m_fin = jnp.maximum(m_new, m_odd)
        acc = jnp.exp2(m_new - m_fin) * acc + sel_weighted(1, 2 * i + 1, m_fin)
        return m_fin, acc, m_even

    _, acc, _ = lax.fori_loop(0, qi // 2 + 1, sel_body,
                              (jnp.full((1, R * T), NEG, F32), jnp.zeros((DH + ones_rows, R * T), F32),
                               sel_scores(0, 0)))
    out_t = out_t + gvec(1) * (acc[:DH] * (1.0 / acc[DH:DH + 1]))
    for pr in range(R // 2):
        pair = jnp.concatenate([out_t[:, (2 * pr) * T:(2 * pr + 1) * T],
                                out_t[:, (2 * pr + 1) * T:(2 * pr + 2) * T]], axis=0)
        o_ref[0, :, pr * 2 * DH:(pr + 1) * 2 * DH] = pair.T.astype(o_ref.dtype)


def _bias_lookup(table, dist):
    onehot = (_t5_bucket(dist)[..., None] == jnp.arange(table.shape[0])).astype(F32)
    return jnp.einsum('...k,kh->...h', onehot, table, precision=HIGHEST)


def _nsa_t_tables(rel_bias, S):
    T, R, KV = ATT_TILE, NSA_R, NSA_KV
    table = rel_bias.astype(F32) * LOG2E
    ii = jnp.arange(T)
    delta = ii[None, :] - ii[:, None]

    def lanes(a):
        a = jnp.moveaxis(a, -1, 0)
        a = a.reshape((KV, R) + a.shape[1:])
        return jnp.moveaxis(a, 1, 2).reshape(KV, a.shape[2], R * a.shape[3])

    def tile(off):
        return lanes(_bias_lookup(table, off * T + delta))

    mask4 = lambda ok: jnp.tile(jnp.where(ok, 0.0, NEG), (1, R))[None]
    n_far = -(-REL_MAX_DIST // T) + 1
    selb = [tile(o) for o in range(n_far + 1)]
    selb[0] = selb[0] + mask4(delta >= 0)
    selb = jnp.stack(selb, axis=0)
    n_win = WINDOW // T
    winb = [tile(o) + mask4((o * T + delta >= 0) & (o * T + delta < WINDOW)) for o in range(n_win + 1)]
    winb.append(jnp.full_like(winb[0], NEG))
    winb = jnp.stack(winb, axis=0)

    grp = T // CMP_STRIDE
    far = _bias_lookup(table, jnp.asarray(2 * REL_MAX_DIST))
    xx = jnp.arange(4 * grp)
    bdist = ii[None, :] - CMP_STRIDE * (xx[:, None] - 2 * grp) - (CMP_BLOCK - 1)
    band = jnp.where((bdist >= 0)[..., None], _bias_lookup(table, bdist) - far, NEG)
    band = jnp.concatenate([lanes(band), jnp.zeros((KV, 2 * grp, R * T), F32)], axis=1)[:, None]
    cfar = jnp.repeat(far.reshape(KV, R), T, axis=1)[:, None]

    n_pad = S // CMP_STRIDE
    n_sel = S // SEL_BLOCK
    cmp_start = jnp.arange(n_pad) * CMP_STRIDE
    sel_start = jnp.arange(n_sel) * SEL_BLOCK
    overlap = jnp.clip(jnp.minimum(cmp_start[:, None] + CMP_BLOCK, sel_start[None] + SEL_BLOCK)
                       - jnp.maximum(cmp_start[:, None], sel_start[None]), 0).astype(F32) / CMP_BLOCK
    n_cmp = (S - CMP_BLOCK) // CMP_STRIDE + 1
    overlap_t = jnp.where((jnp.arange(n_pad) < n_cmp)[:, None], overlap, 0.0).T
    return cfar, band, selb, winb, overlap_t


def _nsa_t_attention(q4, gv, kcmp, vcmp_t, ks, vs_t, kw, vw_t, tables):
    B, KV, S, _ = kw.shape
    T = ATT_TILE
    cfar, band, selb, winb, overlap_t = tables
    gw = NSA_R * NSA_DH
    n_pad = kcmp.shape[2]
    seq = lambda a: pl.BlockSpec((1, 1) + a.shape[2:], lambda b, g, i: (b, g) + (0,) * (a.ndim - 2))
    qtile = lambda a: pl.BlockSpec((1, 1, 1) + a.shape[3:], lambda b, g, i: (b, g, i, 0, 0))
    grp = lambda a: pl.BlockSpec((1,) + a.shape[1:], lambda b, g, i: (g,) + (0,) * (a.ndim - 1))
    tiles = lambda a: pl.BlockSpec((a.shape[0], 1) + a.shape[2:], lambda b, g, i: (0, g, 0, 0))
    full = lambda a: pl.BlockSpec(a.shape, lambda b, g, i: (0,) * a.ndim)
    return pl.pallas_call(
        _nsa_t_kernel,
        out_shape=jax.ShapeDtypeStruct((B, S, KV * gw), BF16),
        grid=(B, KV, S // T),
        in_specs=[qtile(q4), qtile(gv),
                  seq(kcmp), seq(vcmp_t), seq(ks), seq(vs_t), seq(kw), seq(vw_t),
                  grp(cfar), grp(band), tiles(selb), tiles(winb), full(overlap_t)],
        out_specs=pl.BlockSpec((1, T, gw), lambda b, g, i: (b, i, g)),
        scratch_shapes=[pltpu.VMEM((2 * n_pad + 2 * (T // CMP_STRIDE), NSA_R * T), F32),
                        pltpu.VMEM((S // SEL_BLOCK, T), F32),
                        pltpu.VMEM((2, T, NSA_R * T), BF16)],
        compiler_params=_cparams(("parallel", "parallel", "arbitrary")),
        name="nsa_attention",
    )(q4, gv, kcmp, vcmp_t, ks, vs_t, kw, vw_t, cfar, band, selb, winb, overlap_t)


def _moe_kernel(*refs, n_in, final):
    x_ref, mgate_ref = refs[:2]
    a_refs = refs[2:2 + n_in]
    wo_refs = refs[2 + n_in:2 + 2 * n_in]
    (g_ref, sh_ref, sc_ref, gate_ref, wr_ref, br_ref, before_ref, wg_ref, wu_ref, wd_ref, fg_ref,
     o_ref, x_all, hs_all, rts_all, acc_all, perm_t_all, meta_all) = refs[2 + 2 * n_in:]
    NG, PG, FH = MOE_GROUPS, MOE_PER_GROUP, MOE_HIDDEN
    TP = x_all.shape[0]
    s = pl.program_id(2)

    def prologue(hh):
        x_scr, hs_scr, rts_scr, acc_scr, perm_t_scr = (r.at[hh] for r in (x_all, hs_all, rts_all, acc_all, perm_t_all))
        meta = meta_all.at[hh]
        mix = functools.reduce(lambda a, b: a + b,
                               [_dot(a_ref[0].astype(BF16), wo_ref[...]) for a_ref, wo_ref in zip(a_refs, wo_refs)])
        x = x_ref[0] + mgate_ref[0] * mix
        x_scr[...] = x
        h = _modulated_norm(x, g_ref[...], sh_ref[0], sc_ref[0])
        h_hi = h.astype(BF16)
        h_lo = (h - h_hi.astype(F32)).astype(BF16)
        logits = (_dot(h_hi, wr_ref[0]) + _dot(h_lo, wr_ref[0]) + _dot(h_hi, wr_ref[1]) + br_ref[...]).T
        gl = [logits[NG * PG + g:NG * PG + g + 1, :] for g in range(NG)]
        gmax = functools.reduce(jnp.maximum, gl)
        gtop = jnp.full_like(gmax, float(NG - 1))
        for g in reversed(range(NG - 1)):
            gtop = jnp.where(gl[g] == gmax, float(g), gtop)
        p_g = 1.0 / functools.reduce(lambda a, b: a + b, [jnp.exp(v - gmax) for v in gl])
        a = []
        for j in range(PG):
            v = logits[(NG - 1) * PG + j:(NG - 1) * PG + j + 1, :]
            for g in reversed(range(NG - 1)):
                v = jnp.where(gtop == float(g), logits[g * PG + j:g * PG + j + 1, :], v)
            a.append(v)

        def first_max(vals):
            vmax = functools.reduce(jnp.maximum, vals)
            taken = jnp.zeros_like(vmax) > 1.0
            hits = []
            for v in vals:
                hit = (v == vmax) & jnp.logical_not(taken)
                taken = taken | hit
                hits.append(hit)
            return vmax, hits

        v1, hit1 = first_max(a)
        rest = [jnp.where(hh, -jnp.inf, v) for hh, v in zip(hit1, a)]
        v2, hit2 = first_max(rest)
        e2 = jnp.exp(v2 - v1)
        w1 = p_g / (1.0 + e2)
        w2 = p_g * e2 / (1.0 + e2)
        tm = gtop.shape[1]
        row = lax.broadcasted_iota(jnp.int32, (SUBLANES, tm), 0)
        onehot = [jnp.where(gtop == float(g), 1.0, 0.0) for g in range(NG)]
        oh8 = jnp.zeros((SUBLANES, tm), F32)
        for g in range(NG):
            oh8 = jnp.where(row == g, onehot[g], oh8)
        before = _dot(oh8.astype(BF16), before_ref[...])
        pos = jnp.zeros_like(gtop)
        off = jnp.int32(0)
        for g in range(NG):
            cnt = jnp.sum(onehot[g]).astype(jnp.int32)
            meta[g] = off
            meta[NG + g] = cnt
            pos = pos + onehot[g] * (before[g:g + 1, :] + off.astype(F32))
            off = off + cnt
        rt = jnp.where(row == PG, gtop, jnp.where(row == PG + 1, pos, 0.0))
        for j in range(PG):
            wj = jnp.where(hit1[j], w1, jnp.where(hit2[j], w2, 0.0))
            rt = jnp.where(row == j, wj, rt)
        rt_tok = jnp.concatenate([rt, jnp.zeros((LANES - SUBLANES, tm), F32)], axis=0).T
        rid = lax.broadcasted_iota(jnp.int32, (tm, tm), 0).astype(F32)
        cid = lax.broadcasted_iota(jnp.int32, (tm, tm), 1).astype(F32)
        perm = jnp.where(rid == pos, 1.0, 0.0).astype(BF16)
        perm_t_scr[...] = jnp.where(rt_tok[:, PG + 1:PG + 2] == cid, 1.0, 0.0).astype(BF16)
        r1 = rt_tok.astype(BF16)
        res = rt_tok - r1.astype(F32)
        r2 = res.astype(BF16)
        r3 = (res - r2.astype(F32)).astype(BF16)
        moved = _dot(perm, jnp.concatenate([h_hi, r1, r2, r3], axis=1))
        d = h_hi.shape[1]
        pad = hs_scr.shape[0] - tm
        hs_scr[0:tm, :] = moved[:, :d].astype(BF16)
        hs_scr[tm:, :] = jnp.zeros((pad, d), BF16)
        rts_scr[0:tm, :] = moved[:, d:d + LANES] + moved[:, d + LANES:d + 2 * LANES] + moved[:, d + 2 * LANES:]
        rts_scr[tm:, :] = jnp.full((pad, LANES), -1.0, F32)
        acc_scr[...] = jnp.zeros(acc_scr.shape, F32)

    tm = x_all.shape[1]
    WIN = hs_all.shape[1] - tm

    def experts(c):
        cf = c.astype(F32)
        bases, counts = [], []
        for hh in range(TP):
            off = meta_all[hh, c]
            cnt = meta_all[hh, NG + c]
            base = (off // MOE_ALIGN) * MOE_ALIGN
            bases.append(base)
            counts.append(jnp.where(cnt > 0, (off + cnt - base + WIN - 1) // WIN, 0))

        def win_body(w, carry):
            starts = [pl.multiple_of(jnp.where(w < counts[hh], bases[hh] + w * WIN, tm), MOE_ALIGN) for hh in range(TP)]
            hs = jnp.concatenate([hs_all[hh, pl.ds(starts[hh], WIN), :] for hh in range(TP)], axis=0)
            rt = jnp.concatenate([rts_all[hh, pl.ds(starts[hh], WIN), :] for hh in range(TP)], axis=0)
            in_group = rt[:, PG:PG + 1] == cf
            hid = _silu(_dot(hs, wg_ref[0])) * _dot(hs, wu_ref[0])
            parts = [hid[:, j * FH:(j + 1) * FH] * jnp.where(in_group, rt[:, j:j + 1], 0.0) for j in range(PG)]
            out = _dot(jnp.concatenate(parts, axis=1).astype(BF16), wd_ref[0])
            for hh in range(TP):
                acc_all[hh, pl.ds(starts[hh], WIN), :] += out[hh * WIN:(hh + 1) * WIN]
            return carry

        lax.fori_loop(0, functools.reduce(jnp.maximum, counts), win_body, 0)

    def epilogue(hh):
        ys = acc_all[hh, 0:tm, :]
        ys_hi = ys.astype(BF16)
        ys_lo = (ys - ys_hi.astype(F32)).astype(BF16)
        back = _dot(perm_t_all[hh], jnp.concatenate([ys_hi, ys_lo], axis=1))
        d = ys.shape[1]
        y = x_all[hh] + gate_ref[0] * (back[:, :d] + back[:, d:])
        if final:
            y = y * lax.rsqrt(jnp.mean(y * y, axis=-1, keepdims=True) + EPS) * fg_ref[...]
        o_ref[0] = y

    for hh in range(TP):
        pl.when(s == hh)(functools.partial(prologue, hh))
    pl.when((s >= TP - 1) & (s <= TP + NG - 2))(lambda: experts(s - (TP - 1)))
    for hh in range(TP):
        pl.when(s == NG + TP - 2 + hh)(functools.partial(epilogue, hh))


def _moe(x, mix_gate, acts, w_outs, g, shift, scale, gate, wg, bg, we, be, w_gate, w_up, w_down, final_g, final,
         tm=512):
    B, S, D = x.shape
    n_in = len(acts)
    NG, PG, FH = MOE_GROUPS, MOE_PER_GROUP, MOE_HIDDEN
    wr = jnp.zeros((D, LANES), F32)
    wr = wr.at[:, :NG * PG].set(we.reshape(D, NG * PG).astype(F32)).at[:, NG * PG:NG * PG + NG].set(wg.astype(F32))
    br = jnp.zeros((1, LANES), F32)
    br = br.at[0, :NG * PG].set(be.reshape(NG * PG).astype(F32)).at[0, NG * PG:NG * PG + NG].set(bg.astype(F32))
    wr_hi = wr.astype(BF16)
    wr = jnp.stack([wr_hi, (wr - wr_hi.astype(F32)).astype(BF16)])
    grp = lambda w: w.reshape(NG, PG, D, FH).transpose(0, 2, 1, 3).reshape(NG, D, PG * FH).astype(BF16)
    wd = w_down.reshape(NG, PG * FH, D).astype(BF16)
    ids = jnp.arange(tm)
    before = (ids[:, None] < ids[None, :]).astype(BF16)
    TP = MOE_TILES
    n_steps = NG + 2 * TP - 2
    vec = pl.BlockSpec((1, 1, D), lambda b, i, s: (b, 0, 0))
    row = pl.BlockSpec((1, D), lambda b, i, s: (0, 0))
    wspec = lambda k, n: pl.BlockSpec((1, k, n), lambda b, i, s: (jnp.clip(s - (TP - 1), 0, NG - 1), 0, 0))
    tokens_in = lambda n: pl.BlockSpec((1, tm, n), lambda b, i, s: (b, i * TP + jnp.minimum(s, TP - 1), 0))
    tokens_out = pl.BlockSpec((1, tm, D), lambda b, i, s: (b, i * TP + jnp.clip(s - (NG + TP - 2), 0, TP - 1), 0))
    const = lambda a: pl.BlockSpec(a.shape, lambda b, i, s: (0,) * a.ndim)
    return pl.pallas_call(
        functools.partial(_moe_kernel, n_in=n_in, final=final),
        out_shape=jax.ShapeDtypeStruct((B, S, D), F32),
        grid=(B, S // (tm * TP), n_steps),
        in_specs=[tokens_in(D), vec] + [tokens_in(a.shape[2]) for a in acts] + [const(w) for w in w_outs]
                 + [row, vec, vec, vec, const(wr), const(br), const(before),
                    wspec(D, PG * FH), wspec(D, PG * FH), wspec(PG * FH, D), row],
        out_specs=tokens_out,
        scratch_shapes=[pltpu.VMEM((TP, tm, D), F32), pltpu.VMEM((TP, tm + MOE_WIN, D), BF16),
                        pltpu.VMEM((TP, tm + MOE_WIN, LANES), F32), pltpu.VMEM((TP, tm + MOE_WIN, D), F32),
                        pltpu.VMEM((TP, tm, tm), BF16), pltpu.SMEM((TP, 2 * NG), jnp.int32)],
        compiler_params=_cparams(("parallel", "parallel", "arbitrary")),
        name="moe",
    )(x, mix_gate, *acts, *w_outs, g.reshape(1, D), shift, scale, gate, wr, br, before, grp(w_gate), grp(w_up), wd,
      final_g.reshape(1, D))


def _mlstm_s5_layer(x, g, shift, scale, w_in, conv_w, b_i, b_f, head_g, s5_params, w_out):
    H = MLSTM_HEADS
    A = MIX_A
    w_if = jnp.zeros((D_MODEL, LANES), F32).at[:, :2 * H].set(w_in[:, 4 * A:4 * A + 2 * H])
    weights = [w_in[:, :2 * A], w_in[:, 2 * A:4 * A], w_if, w_in[:, 4 * A + 2 * H:]]
    qk, vo, ifg, u = _norm_matmul(x, g, shift, scale, [w.astype(BF16) for w in weights], [BF16, BF16, F32, F32])
    gate_bias = jnp.zeros((1, LANES), F32).at[0, :H].set(b_i.astype(F32)).at[0, H:2 * H].set(b_f.astype(F32))
    hm = _mlstm(qk, vo, ifg, conv_w.astype(F32), gate_bias, head_g.reshape(1, A).astype(F32))
    ys = _s5s(u, _s5s_tables(*s5_params))
    w_out = w_out.astype(BF16)
    return [hm, ys], [w_out[:A], w_out[A:]]


def _nsa_layer(x, g, shift, scale, w_in, b_gate, cmp_pos, cmp_w1, cmp_b1, cmp_w2, cmp_b2, rel_bias, w_out):
    B, S, D = x.shape
    KV, R, DH = NSA_KV, NSA_R, NSA_DH
    w_g = jnp.zeros((D, KV, LANES), F32).at[:, :, :3 * R].set(w_in[:, D + 6 * KV_W:].reshape(D, KV, 3 * R))
    b_g = jnp.zeros((KV, LANES), F32).at[:, :3 * R].set(b_gate.reshape(KV, 3 * R).astype(F32))
    kv_cols = lambda i: w_in[:, D + i * KV_W:D + (i + 1) * KV_W]
    w_k = jnp.concatenate([kv_cols(0), kv_cols(2), kv_cols(4)], axis=1)
    w_v = jnp.concatenate([kv_cols(1), kv_cols(3), kv_cols(5)], axis=1)
    weights = [w_in[:, :D], w_k, w_v, w_g.reshape(D, KV * LANES)]
    q4, gv, kc, vc, ks, kw, vs_t, vw_t = _nsa_proj(x, g, shift, scale, [w.astype(BF16) for w in weights],
                                                   b_g.reshape(1, KV * LANES))
    grp = CMP_STRIDE
    xg = jnp.stack([kc, vc]).reshape(2, B, KV * S // grp, grp * DH)
    cmp = _compress(xg, cmp_pos, cmp_w1, cmp_b1, cmp_w2, cmp_b2).reshape(2, B, KV, S // grp, DH).astype(BF16)
    out = _nsa_t_attention(q4, gv, cmp[0], cmp[1].transpose(0, 1, 3, 2), ks, vs_t, kw, vw_t,
                           _nsa_t_tables(rel_bias, S))
    return [out], [w_out.astype(BF16)]


def kernel(x, c, rel_bias, ada_w, ada_b, norm_g, final_g,
           a_w_in, a_conv, a_b_i, a_b_f, a_head_g,
           s5_lam_re, s5_lam_im, s5_log_dt, s5_b_re, s5_b_im, s5_c_re, s5_c_im,
           s5_d, s5_glu_w, s5_glu_b, a_w_out,
           n_w_in, n_b_gate, n_cmp_pos, n_cmp_w1, n_cmp_b1, n_cmp_w2, n_cmp_b2, n_w_out,
           r_grp_w, r_grp_b, r_exp_w, r_exp_b, e_w_gate, e_w_up, e_w_down):
    B, S, D = x.shape
    mod = _ada_mod(c, ada_w, ada_b).reshape(DEPTH, 2, B, 1, 3 * D)
    split = lambda m: (m[..., :D], m[..., D:2 * D], m[..., 2 * D:])
    for layer in range(DEPTH):
        shift, scale, mix_gate = split(mod[layer, 0])
        j = layer // 2
        if layer % 2 == 0:
            s5_params = (s5_lam_re[j], s5_lam_im[j], s5_log_dt[j], s5_b_re[j], s5_b_im[j],
                         s5_c_re[j], s5_c_im[j], s5_d[j], s5_glu_w[j], s5_glu_b[j])
            acts, w_outs = _mlstm_s5_layer(x, norm_g[layer, 0], shift, scale, a_w_in[j], a_conv[j], a_b_i[j],
                                           a_b_f[j], a_head_g[j], s5_params, a_w_out[j])
        else:
            acts, w_outs = _nsa_layer(x, norm_g[layer, 0], shift, scale, n_w_in[j], n_b_gate[j], n_cmp_pos[j],
                                      n_cmp_w1[j], n_cmp_b1[j], n_cmp_w2[j], n_cmp_b2[j], rel_bias, n_w_out[j])
        shift, scale, gate = split(mod[layer, 1])
        x = _moe(x, mix_gate, acts, w_outs, norm_g[layer, 1], shift, scale, gate, r_grp_w[layer], r_grp_b[layer],
                 r_exp_w[layer], r_exp_b[layer], e_w_gate[layer], e_w_up[layer], e_w_down[layer], final_g,
                 final=(layer == DEPTH - 1))
    return x
```

```python
import functools
import math

import jax
import jax.numpy as jnp
from jax import lax
from jax.experimental import pallas as pl
from jax.experimental.pallas import tpu as pltpu

F32 = jnp.float32
BF16 = jnp.bfloat16
HIGHEST = lax.Precision.HIGHEST

D_MODEL = 1024
DEPTH = 2
MIX_A = 512
MLSTM_HEADS = 4
MLSTM_DH = MIX_A // MLSTM_HEADS
MLSTM_CHUNK = 128
CONV_K = 4
S5_GROUP = 16
S5_STATE = 64
S5_CHUNK = 16
NSA_HEADS = 16
NSA_KV = 4
NSA_R = NSA_HEADS // NSA_KV
NSA_DH = D_MODEL // NSA_HEADS
KV_W = NSA_KV * NSA_DH
CMP_BLOCK = 32
CMP_STRIDE = 16
CMP_HIDDEN = 256
SEL_BLOCK = 64
SEL_TOPK = 16
WINDOW = 512
FORCE = 1e9
REL_BUCKETS = 32
REL_MAX_DIST = 128
MOE_GROUPS = 4
MOE_PER_GROUP = 4
MOE_HIDDEN = 256
EPS = 1e-6
NEG = -1e30
BIG = 1e30
LOG2E = math.log2(math.e)

LANES = 128
SUBLANES = 8
ATT_TILE = 256
MOE_WIN = 160
MOE_ALIGN = 16
MOE_TILES = 2
VMEM_LIMIT = 56 * 1024 * 1024


def _cparams(sem):
    return pltpu.CompilerParams(dimension_semantics=sem, vmem_limit_bytes=VMEM_LIMIT)


def _dot(a, b, precision=None):
    return jnp.dot(a, b, preferred_element_type=F32, precision=precision)


def _dot_nt(a, b):
    return lax.dot_general(a, b, (((1,), (1,)), ((), ())), preferred_element_type=F32)


def _sigmoid(x):
    return 1.0 / (1.0 + jnp.exp(-x))


def _silu(x):
    return x * _sigmoid(x)


def _gelu_tanh(x):
    return 0.5 * x * (1.0 + jnp.tanh(math.sqrt(2.0 / math.pi) * (x + 0.044715 * (x * x * x))))


def _modulated_norm(x, g, shift, scale):
    y = x * lax.rsqrt(jnp.mean(x * x, axis=-1, keepdims=True) + EPS) * g
    return y * (1.0 + scale) + shift


def _ada_kernel(c_ref, w_ref, b_ref, o_ref):
    c = c_ref[...]
    o_ref[0] = _dot(_silu(c), w_ref[0]) + b_ref[0]


def _ada_mod(c, ada_w, ada_b):
    B, D = c.shape
    n_mod = ada_w.shape[0] * ada_w.shape[1]
    w = ada_w.reshape(n_mod, D, 3 * D)
    b = ada_b.reshape(n_mod, 1, 3 * D)
    tn = 1024
    return pl.pallas_call(
        _ada_kernel,
        out_shape=jax.ShapeDtypeStruct((n_mod, B, 3 * D), F32),
        grid=(n_mod, 3 * D // tn),
        in_specs=[pl.BlockSpec((B, D), lambda i, j: (0, 0)),
                  pl.BlockSpec((1, D, tn), lambda i, j: (i, 0, j)),
                  pl.BlockSpec((1, 1, tn), lambda i, j: (i, 0, j))],
        out_specs=pl.BlockSpec((1, B, tn), lambda i, j: (i, 0, j)),
        compiler_params=_cparams(("parallel", "parallel")),
        name="ada_mod",
    )(c, w, b)


def _norm_mm_kernel(*refs, n_w):
    x_ref, g_ref, sh_ref, sc_ref = refs[:4]
    w_refs = refs[4:4 + n_w]
    o_refs = refs[4 + n_w:]
    h = _modulated_norm(x_ref[0], g_ref[...], sh_ref[0], sc_ref[0]).astype(BF16)
    for w_ref, o_ref in zip(w_refs, o_refs):
        o_ref[0] = _dot(h, w_ref[...]).astype(o_ref.dtype)


def _norm_matmul(x, g, shift, scale, weights, out_dtypes, tm=512):
    B, S, D = x.shape
    n_w = len(weights)
    vec = pl.BlockSpec((1, 1, D), lambda b, i: (b, 0, 0))
    in_specs = [pl.BlockSpec((1, tm, D), lambda b, i: (b, i, 0)),
                pl.BlockSpec((1, D), lambda b, i: (0, 0)), vec, vec]
    in_specs += [pl.BlockSpec(w.shape, lambda b, i: (0, 0)) for w in weights]
    return pl.pallas_call(
        functools.partial(_norm_mm_kernel, n_w=n_w),
        out_shape=[jax.ShapeDtypeStruct((B, S, w.shape[1]), dt) for w, dt in zip(weights, out_dtypes)],
        grid=(B, S // tm),
        in_specs=in_specs,
        out_specs=[pl.BlockSpec((1, tm, w.shape[1]), lambda b, i: (b, i, 0)) for w in weights],
        compiler_params=_cparams(("parallel", "parallel")),
        name="norm_matmul",
    )(x, g.reshape(1, D), shift, scale, *weights)


def _mlstm_kernel(qk_ref, vo_ref, if_ref, cw_ref, gb_ref, hg_ref, tril_ref, o_ref,
                  xbuf, c_scr, n_scr, m_scr):
    pad = SUBLANES

    @pl.when(pl.program_id(1) == 0)
    def _():
        xbuf[:, 0:pad, :] = jnp.zeros((xbuf.shape[0], pad, 2 * MIX_A), F32)
        c_scr[...] = jnp.zeros_like(c_scr)
        n_scr[...] = jnp.zeros_like(n_scr)
        m_scr[...] = jnp.zeros_like(m_scr)

    for bb in range(qk_ref.shape[0]):
        _mlstm_chunk(qk_ref.at[bb], vo_ref.at[bb], if_ref.at[bb], cw_ref, gb_ref, hg_ref, tril_ref, o_ref.at[bb],
                     xbuf.at[bb], c_scr.at[bb], n_scr.at[bb], m_scr.at[bb])


def _mlstm_chunk(qk_ref, vo_ref, if_ref, cw_ref, gb_ref, hg_ref, tril_ref, o_ref, xbuf, c_scr, n_scr, m_scr):
    L, H, DH = MLSTM_CHUNK, MLSTM_HEADS, MLSTM_DH
    pad = SUBLANES
    xbuf[pad:pad + L, :] = qk_ref[...].astype(F32)
    cw = cw_ref[...]
    conv = None
    for j in range(CONV_K):
        lo = pad - (CONV_K - 1) + j
        t = xbuf[lo:lo + L, :] * cw[j:j + 1, :]
        conv = t if conv is None else conv + t
    xbuf[0:pad, :] = xbuf[L:L + pad, :]
    qk = _silu(conv)
    q = qk[:, :MIX_A]
    k = qk[:, MIX_A:] * (DH ** -0.5)
    vo = vo_ref[...].astype(F32)
    v = vo[:, :MIX_A]
    o_pre = vo[:, MIX_A:]

    ifb = if_ref[...] + gb_ref[...]
    lf = jnp.minimum(ifb, 0.0) - jnp.log1p(jnp.exp(-jnp.abs(ifb)))
    bcs = _dot(tril_ref[...], lf, precision=HIGHEST)
    ifb_t = ifb.T
    bcs_t = bcs.T
    row = lax.broadcasted_iota(jnp.int32, (L, L), 0)
    col = lax.broadcasted_iota(jnp.int32, (L, L), 1)
    causal = col <= row

    outs = []
    for h in range(H):
        sl = slice(h * DH, (h + 1) * DH)
        qh, kh, vh = q[:, sl], k[:, sl], v[:, sl]
        qb, kb = qh.astype(BF16), kh.astype(BF16)
        b_col = bcs[:, H + h:H + h + 1]
        b_row = bcs_t[H + h:H + h + 1, :]
        li_col = ifb[:, h:h + 1]
        li_row = ifb_t[h:h + 1, :]
        b_last = b_col[L - 1:L, :]
        m0 = m_scr[h][:, 0:1]
        c0 = c_scr[h]
        n0 = n_scr[h]

        log_d = jnp.where(causal, b_col - b_row + li_row, NEG)
        log_inter = b_col + m0
        m_t = jnp.maximum(log_inter, jnp.max(log_d, axis=1, keepdims=True))
        dmat = jnp.exp(log_d - m_t)
        a_inter = jnp.exp(log_inter - m_t)
        s = _dot_nt(qb, kb) * dmat
        num = _dot(s.astype(BF16), vh.astype(BF16)) + a_inter * _dot_nt(qb, c0.astype(BF16))
        den = jnp.sum(s, axis=1, keepdims=True) + a_inter * jnp.sum(qh * n0, axis=1, keepdims=True)
        hh = num / jnp.maximum(jnp.abs(den), jnp.exp(-m_t))

        w_col = b_last - b_col + li_col
        m_loc = jnp.max(w_col, axis=0, keepdims=True)
        e = jnp.exp(w_col - m_loc)
        c_loc = _dot((vh * e).T.astype(BF16), kb)
        n_loc = jnp.sum(kh * e, axis=0, keepdims=True)
        m_new = jnp.maximum(b_last + m0, m_loc)
        a = jnp.exp(b_last + m0 - m_new)
        sc = jnp.exp(m_loc - m_new)
        c_scr[h] = a * c0 + sc * c_loc
        n_scr[h] = a * n0 + sc * n_loc
        m_scr[h] = jnp.broadcast_to(m_new, (1, LANES))

        outs.append(hh * lax.rsqrt(jnp.mean(hh * hh, axis=1, keepdims=True) + EPS))
    hm = jnp.concatenate(outs, axis=1)
    o_ref[...] = (_sigmoid(o_pre) * (hm * hg_ref[...])).astype(o_ref.dtype)


def _mlstm(qk, vo, ifg, conv_w, gate_bias, head_g):
    B, S, _ = qk.shape
    rows = 1
    L, H, DH = MLSTM_CHUNK, MLSTM_HEADS, MLSTM_DH
    tril = jnp.tril(jnp.ones((L, L), F32))
    return pl.pallas_call(
        _mlstm_kernel,
        out_shape=jax.ShapeDtypeStruct((B, S, MIX_A), BF16),
        grid=(B // rows, S // L),
        in_specs=[pl.BlockSpec((rows, L, 2 * MIX_A), lambda b, c: (b, c, 0)),
                  pl.BlockSpec((rows, L, 2 * MIX_A), lambda b, c: (b, c, 0)),
                  pl.BlockSpec((rows, L, LANES), lambda b, c: (b, c, 0)),
                  pl.BlockSpec((CONV_K, 2 * MIX_A), lambda b, c: (0, 0)),
                  pl.BlockSpec((1, LANES), lambda b, c: (0, 0)),
                  pl.BlockSpec((1, MIX_A), lambda b, c: (0, 0)),
                  pl.BlockSpec((L, L), lambda b, c: (0, 0))],
        out_specs=pl.BlockSpec((rows, L, MIX_A), lambda b, c: (b, c, 0)),
        scratch_shapes=[pltpu.VMEM((rows, L + SUBLANES, 2 * MIX_A), F32),
                        pltpu.VMEM((rows, H, DH, DH), F32),
                        pltpu.VMEM((rows, H, 1, DH), F32),
                        pltpu.VMEM((rows, H, 1, LANES), F32)],
        compiler_params=_cparams(("parallel", "arbitrary")),
        name="mlstm",
    )(qk, vo, ifg, conv_w, gate_bias, head_g, tril)


S5_LT = LANES // S5_GROUP
S5_PAIRS = S5_CHUNK // 2


def _s5s_kernel(u_ref, h_ref, e_ref, kk_ref, are_ref, aim_ref, d_ref, gw_ref, gb_ref, o_ref, xl_scr, x0_scr):
    n_chunks = u_ref.shape[1] // S5_CHUNK
    half = S5_LT * S5_STATE
    tok = lambda s: u_ref[0, pl.ds(s, n_chunks, stride=S5_CHUNK), :]
    u2 = [jnp.concatenate([tok(2 * q), tok(2 * q + 1)], axis=1) for q in range(S5_PAIRS)]
    u2b = [v.astype(BF16) for v in u2]
    xl_scr[...] = functools.reduce(lambda a, b: a + b, [_dot(u2b[q], h_ref[0, q]) for q in range(S5_PAIRS)])
    a_re = are_ref[0]
    a_im = aim_ref[0]

    def body(a, carry):
        re, im = carry
        x0_scr[pl.ds(a, 1), 0:half] = re
        x0_scr[pl.ds(a, 1), half:2 * half] = im
        return (a_re * re - a_im * im + xl_scr[pl.ds(a, 1), 0:half],
                a_re * im + a_im * re + xl_scr[pl.ds(a, 1), half:2 * half])

    zero = jnp.zeros((1, half), F32)
    lax.fori_loop(0, n_chunks, body, (zero, zero), unroll=8)
    x0 = x0_scr[...].astype(BF16)
    for p in range(S5_PAIRS):
        y = _dot(x0, e_ref[0, p]) + u2[p] * d_ref[0]
        for q in range(p + 1):
            y = y + _dot(u2b[q], kk_ref[0, p - q])
        ys = _gelu_tanh(y)
        out = ys * _sigmoid(_dot(ys.astype(BF16), gw_ref[0]) + gb_ref[0])
        o_ref[0, pl.ds(2 * p, n_chunks, stride=S5_CHUNK), :] = out[:, :LANES].astype(o_ref.dtype)
        o_ref[0, pl.ds(2 * p + 1, n_chunks, stride=S5_CHUNK), :] = out[:, LANES:].astype(o_ref.dtype)


def _s5s_tables(lam_re, lam_im, log_dt, b_re, b_im, c_re, c_im, d_skip, glu_w, glu_b):
    T, C, P, LT = S5_CHUNK, S5_GROUP, S5_STATE, S5_LT
    G = lam_re.shape[0]
    NT = G // LT
    lam = lax.complex(lam_re.astype(F32), lam_im.astype(F32))
    dt = jnp.exp(log_dt.astype(F32))[:, None]
    lam_bar = jnp.exp(lam * dt)
    b_bar = ((lam_bar - 1.0) / lam)[..., None] * lax.complex(b_re.astype(F32), b_im.astype(F32))
    c_mat = lax.complex(c_re.astype(F32), c_im.astype(F32))
    taus = jnp.arange(T + 1, dtype=F32)
    pw = jnp.exp((lam * dt)[:, None, :] * taus[None, :, None])
    eye = jnp.eye(LT, dtype=F32)
    tiles = lambda a: a.reshape((NT, LT) + a.shape[1:])

    kern = jnp.einsum('gcp,gtp,gpd->gtdc', c_mat, pw[:, :T], b_bar, precision=HIGHEST).real
    kblk = jnp.einsum('nitdc,ij->ntidjc', tiles(kern), eye).reshape(NT, T, LANES, LANES)
    kblk = jnp.concatenate([jnp.zeros_like(kblk[:, :1]), kblk], axis=1)
    kk = jnp.stack([jnp.concatenate([jnp.concatenate([kblk[:, 2 * d + 1], kblk[:, 2 * d + 2]], axis=2),
                                     jnp.concatenate([kblk[:, 2 * d], kblk[:, 2 * d + 1]], axis=2)], axis=1)
                    for d in range(T // 2)], axis=1)

    hmat = pw[:, :T][:, ::-1, :, None] * b_bar[:, None]

    def state_cols(m):
        return jnp.einsum('nispc,ij->nsicjp', tiles(m), eye).reshape(NT, T, LANES, LT * P)

    h = jnp.concatenate([state_cols(hmat.real), state_cols(hmat.imag)], axis=3)
    h2 = h.reshape(NT, T // 2, 2 * LANES, 2 * LT * P)

    emat = c_mat[:, None] * pw[:, 1:][:, :, None, :]

    def state_rows(m):
        return jnp.einsum('nitcp,ij->ntjpic', tiles(m), eye).reshape(NT, T, LT * P, LANES)

    e = jnp.concatenate([state_rows(emat.real), state_rows(-emat.imag)], axis=2)
    e2 = e.reshape(NT, T // 2, 2, 2 * LT * P, LANES).transpose(0, 1, 3, 2, 4).reshape(NT, T // 2, 2 * LT * P, 2 * LANES)

    a_re = pw[:, T].real.reshape(NT, 1, LT * P)
    a_im = pw[:, T].imag.reshape(NT, 1, LT * P)
    pair = lambda v: jnp.tile(v.astype(F32).reshape(NT, 1, LANES), (1, 1, 2))
    gwb = jnp.einsum('nice,ij->nicje', tiles(glu_w.astype(F32)), eye).reshape(NT, LANES, LANES)
    zeros = jnp.zeros_like(gwb)
    gw2 = jnp.concatenate([jnp.concatenate([gwb, zeros], axis=2), jnp.concatenate([zeros, gwb], axis=2)], axis=1)
    return (h2.astype(BF16), e2.astype(BF16), kk.astype(BF16), a_re, a_im, pair(d_skip), gw2.astype(BF16), pair(glu_b))


def _s5s(u, tables):
    B, S, W = u.shape
    NT = W // LANES
    n_chunks = S // S5_CHUNK
    per_tile = lambda a: pl.BlockSpec((1,) + a.shape[1:], lambda j, b: (j,) + (0,) * (a.ndim - 1))
    return pl.pallas_call(
        _s5s_kernel,
        out_shape=jax.ShapeDtypeStruct((B, S, W), F32),
        grid=(NT, B),
        in_specs=[pl.BlockSpec((1, S, LANES), lambda j, b: (b, 0, j))] + [per_tile(t) for t in tables],
        out_specs=pl.BlockSpec((1, S, LANES), lambda j, b: (b, 0, j)),
        scratch_shapes=[pltpu.VMEM((n_chunks, 2 * S5_LT * S5_STATE), F32) for _ in range(2)],
        compiler_params=_cparams(("parallel", "parallel")),
        name="s5",
    )(u, *tables)


def _compress_kernel(x_ref, plo_ref, phi_ref, w1_ref, b1_ref, w2_ref, b2_ref, o_ref):
    x = x_ref[0, 0]
    half = x.shape[1]
    w1 = w1_ref[0]
    lo = _dot((x + plo_ref[0]).astype(BF16), w1[:half])
    hi = _dot((x + phi_ref[0]).astype(BF16), w1[half:])
    rows = x.shape[0]
    hid = _gelu_tanh(lo + pltpu.roll(hi, rows - 1, 0) + b1_ref[0])
    o_ref[0, 0] = _dot(hid.astype(BF16), w2_ref[0]) + b2_ref[0]


def _compress(xg, pos, w1, b1, w2, b2):
    _, B, rows, width = xg.shape
    pos_flat = pos.reshape(2, 2, 1, width).astype(F32)
    sel = lambda shape: pl.BlockSpec((1,) + shape, lambda j, b: (j, 0, 0))
    return pl.pallas_call(
        _compress_kernel,
        out_shape=jax.ShapeDtypeStruct((2, B, rows, NSA_DH), F32),
        grid=(2, B),
        in_specs=[pl.BlockSpec((1, 1, rows, width), lambda j, b: (j, b, 0, 0)),
                  sel((1, width)), sel((1, width)),
                  sel((2 * width, CMP_HIDDEN)), sel((1, CMP_HIDDEN)),
                  sel((CMP_HIDDEN, NSA_DH)), sel((1, NSA_DH))],
        out_specs=pl.BlockSpec((1, 1, rows, NSA_DH), lambda j, b: (j, b, 0, 0)),
        compiler_params=_cparams(("parallel", "parallel")),
        name="nsa_compress",
    )(xg, pos_flat[:, 0], pos_flat[:, 1], w1.astype(BF16), b1[:, None].astype(F32),
      w2.astype(BF16), b2[:, None].astype(F32))


def _t5_bucket(dist):
    dist = jnp.maximum(dist, 0)
    max_exact = REL_BUCKETS // 2
    log_ratio = jnp.log(jnp.maximum(dist, 1).astype(F32) / max_exact) / math.log(REL_MAX_DIST / max_exact)
    large = jnp.minimum(max_exact + (log_ratio * (REL_BUCKETS - max_exact)).astype(jnp.int32), REL_BUCKETS - 1)
    return jnp.where(dist < max_exact, dist, large)


def _nsa_proj_kernel(x_ref, g_ref, sh_ref, sc_ref, wq_ref, wk_ref, wv_ref, wg_ref, bg_ref,
                     q4_ref, gv_ref, kc_ref, vc_ref, ks_ref, kw_ref, vst_ref, vwt_ref):
    KV, R, DH, T = NSA_KV, NSA_R, NSA_DH, ATT_TILE
    h = _modulated_norm(x_ref[0], g_ref[...], sh_ref[0], sc_ref[0]).astype(BF16)
    q_t = (_dot(h, wq_ref[...]) * (DH ** -0.5 * LOG2E)).T.astype(BF16)
    gates_t = _sigmoid(_dot(h, wg_ref[...]) + bg_ref[...]).T
    row = lax.broadcasted_iota(jnp.int32, (SUBLANES, R * T), 0)
    for g in range(KV):
        q4_ref[0, g, 0] = jnp.concatenate([q_t[(g * R + r) * DH:(g * R + r + 1) * DH] for r in range(R)], axis=1)
        gv = jnp.zeros((SUBLANES, R * T), F32)
        for j in range(3):
            gj = jnp.concatenate([gates_t[g * LANES + 3 * r + j:g * LANES + 3 * r + j + 1] for r in range(R)], axis=1)
            gv = jnp.where(row == j, gj, gv)
        gv_ref[0, g, 0] = gv
    k3 = _dot(h, wk_ref[...])
    v3 = _dot(h, wv_ref[...])
    vs_t = v3[:, KV_W:2 * KV_W].T.astype(BF16)
    vw_t = v3[:, 2 * KV_W:].T.astype(BF16)
    for g in range(KV):
        cols = slice(g * DH, (g + 1) * DH)
        kc_ref[0, g] = k3[:, cols].astype(BF16)
        vc_ref[0, g] = v3[:, cols].astype(BF16)
        ks_ref[0, g] = k3[:, KV_W + g * DH:KV_W + (g + 1) * DH].astype(BF16)
        kw_ref[0, g] = k3[:, 2 * KV_W + g * DH:2 * KV_W + (g + 1) * DH].astype(BF16)
        vst_ref[0, g, 0] = vs_t[cols]
        vwt_ref[0, g, 0] = vw_t[cols]


def _nsa_proj(x, g, shift, scale, weights, b_gate):
    B, S, D = x.shape
    KV, R, DH, T = NSA_KV, NSA_R, NSA_DH, ATT_TILE
    vec = pl.BlockSpec((1, 1, D), lambda b, i: (b, 0, 0))
    keys = pl.BlockSpec((1, KV, T, DH), lambda b, i: (b, 0, i, 0))
    key_shape = jax.ShapeDtypeStruct((B, KV, S, DH), BF16)
    tile = lambda rows, width: pl.BlockSpec((1, KV, 1, rows, width), lambda b, i: (b, 0, i, 0, 0))
    tile_shape = lambda rows, width, dt: jax.ShapeDtypeStruct((B, KV, S // T, rows, width), dt)
    return pl.pallas_call(
        _nsa_proj_kernel,
        out_shape=[tile_shape(DH, R * T, BF16), tile_shape(SUBLANES, R * T, F32),
                   key_shape, key_shape, key_shape, key_shape,
                   tile_shape(DH, T, BF16), tile_shape(DH, T, BF16)],
        grid=(B, S // T),
        in_specs=[pl.BlockSpec((1, T, D), lambda b, i: (b, i, 0)),
                  pl.BlockSpec((1, D), lambda b, i: (0, 0)), vec, vec]
                 + [pl.BlockSpec(w.shape, lambda b, i: (0, 0)) for w in weights]
                 + [pl.BlockSpec(b_gate.shape, lambda b, i: (0, 0))],
        out_specs=[tile(DH, R * T), tile(SUBLANES, R * T), keys, keys, keys, keys, tile(DH, T), tile(DH, T)],
        compiler_params=_cparams(("parallel", "parallel")),
        name="nsa_proj",
    )(x, g.reshape(1, D), shift, scale, *weights, b_gate)


def _nsa_t_kernel(q4_ref, gv_ref, kc_ref, vct_ref, ks_ref, vst_ref, kw_ref, vwt_ref,
                  cfar_ref, band_ref, selb_ref, winb_ref, ovt_ref, o_ref, s_scr, sel_scr, sbuf):
    T = ATT_TILE
    R, DH = NSA_R, NSA_DH
    qi = pl.program_id(2)
    q0 = qi * T
    n_pad = kc_ref.shape[2]
    n_sel = ovt_ref.shape[0]
    n_far = selb_ref.shape[0] - 1
    n_win = winb_ref.shape[0] - 2
    band_rows = band_ref.shape[2] - T // CMP_STRIDE * 2

    q4 = q4_ref[0, 0, 0]
    t_lane = q0 + lax.broadcasted_iota(jnp.int32, (1, R * T), 1) % T

    ones_rows = DH
    with_ones = lambda v_t: jnp.concatenate([v_t, jnp.ones((ones_rows, v_t.shape[1]), v_t.dtype)], axis=0)
    gvec = lambda j: gv_ref[0, 0, 0, j:j + 1, :]

    grp = T // CMP_STRIDE
    s_scr[0:n_pad, :] = _dot(kc_ref[0, 0], q4) + cfar_ref[0]
    s_scr[n_pad:n_pad + 2 * grp, :] = jnp.zeros((2 * grp, R * T), F32)
    r0 = jnp.maximum(qi * grp - 2 * grp, 0)
    x0 = r0 - (qi * grp - 2 * grp)
    r0 = pl.multiple_of(r0, SUBLANES)
    x0 = pl.multiple_of(x0, SUBLANES)
    s_scr[pl.ds(r0, band_rows), :] += band_ref[0, 0, pl.ds(x0, band_rows), :]
    lim = pl.multiple_of(qi * grp + 2 * grp, SUBLANES)
    s_scr[pl.ds(lim, n_pad), :] = jnp.full((n_pad, R * T), NEG, F32)

    w_subs, w_vals = [], []
    for d in range(n_win + 1):
        kt = jnp.maximum(qi - d, 0)
        off = pl.multiple_of(kt * T, T)
        tile = jnp.where(qi >= d, d, n_win + 1)
        w_subs.append((_dot(kw_ref[0, 0, pl.ds(off, T), :], q4) + winb_ref[tile, 0]).astype(BF16))
        w_vals.append(with_ones(vwt_ref[0, 0, kt]))

    s = s_scr[0:n_pad, :]
    e = jnp.exp2(s - jnp.max(s, axis=0, keepdims=True))
    inv = jnp.where(t_lane >= CMP_BLOCK - 1, 1.0 / jnp.sum(e, axis=0, keepdims=True), 0.0)
    p = e * inv
    o_cmp = _dot(vct_ref[0, 0], p.astype(BF16))
    psum = functools.reduce(lambda a, b: a + b, [p[:, r * T:(r + 1) * T] for r in range(R)])

    m_w = jnp.max(functools.reduce(jnp.maximum, w_subs), axis=0, keepdims=True)
    acc = functools.reduce(lambda a, b: a + b,
                           [_dot(vj, jnp.exp2(sj - m_w)) for sj, vj in zip(w_subs, w_vals)])
    o_win = acc[:DH] * (1.0 / acc[DH:DH + 1])
    out_t = gvec(0) * o_cmp + gvec(2) * o_win

    imp_t = _dot(ovt_ref[...], psum, precision=HIGHEST)
    jj = lax.broadcasted_iota(jnp.int32, (n_sel, T), 0)
    blk_t = (q0 + lax.broadcasted_iota(jnp.int32, (1, T), 1)) // SEL_BLOCK
    forced = (jj == 0) | (jj == blk_t) | (jj == blk_t - 1)
    score = jnp.where(forced, FORCE, jnp.where(jj <= blk_t, imp_t, -1.0))
    n_blk = n_sel // SUBLANES
    rows = [score[v * SUBLANES:(v + 1) * SUBLANES] for v in range(n_blk)]
    cnts = [jnp.zeros((SUBLANES, T), F32) for _ in range(n_blk)]
    sub = lax.broadcasted_iota(jnp.int32, (SUBLANES, T), 0)
    for j2 in range(n_sel):
        c2 = score[j2:j2 + 1, :]
        for v in range(n_blk):
            lo = v * SUBLANES
            if lo > j2:
                beats = c2 >= rows[v]
            elif lo + SUBLANES - 1 <= j2:
                beats = c2 > rows[v]
            else:
                beats = (c2 > rows[v]) | ((c2 >= rows[v]) & (sub > j2 - lo))
            cnts[v] = cnts[v] + jnp.where(beats, 1.0, 0.0)
    cnt = jnp.concatenate(cnts, axis=0)
    chosen = (cnt < float(min(SEL_TOPK, n_sel))) & (jj <= blk_t)
    sel_scr[...] = jnp.where(chosen, 0.0, -BIG)

    def block_mask(kt):
        per_tile = T // SEL_BLOCK
        parts = [jnp.broadcast_to(sel_scr[pl.ds(kt * per_tile + i, 1), :], (SEL_BLOCK, T)) for i in range(per_tile)]
        m1 = jnp.concatenate(parts, axis=0)
        return jnp.concatenate([m1] * R, axis=1)

    def sel_scores(slot, kc):
        off = pl.multiple_of(kc * T, T)
        s = _dot(ks_ref[0, 0, pl.ds(off, T), :], q4)
        s = (s + selb_ref[jnp.clip(qi - kc, 0, n_far), 0] + block_mask(kc)).astype(BF16)
        sbuf[slot] = s
        return jnp.max(s, axis=0, keepdims=True).astype(F32)

    def sel_weighted(slot, kc, m_new):
        return _dot(with_ones(vst_ref[0, 0, kc]), jnp.exp2(sbuf[slot] - m_new.astype(BF16)))

    last_tile = vst_ref.shape[2] - 1

    def sel_body(i, carry):
        m, acc, m_even = carry
        m_odd = sel_scores(1, 2 * i + 1)
        m_new = jnp.maximum(m, m_even)
        acc = jnp.exp2(m - m_new) * acc + sel_weighted(0, 2 * i, m_new)
        m_even = sel_scores(0, jnp.minimum(2 * i + 2, last_tile))
        m_fin = jnp.maximum(m_new, m_odd)
        acc = jnp.exp2(m_new - m_fin) * acc + sel_weighted(1, 2 * i + 1, m_fin)
        return m_fin, acc, m_even

    _, acc, _ = lax.fori_loop(0, qi // 2 + 1, sel_body,
                              (jnp.full((1, R * T), NEG, F32), jnp.zeros((DH + ones_rows, R * T), F32),
                               sel_scores(0, 0)))
    out_t = out_t + gvec(1) * (acc[:DH] * (1.0 / acc[DH:DH + 1]))
    for pr in range(R // 2):
        pair = jnp.concatenate([out_t[:, (2 * pr) * T:(2 * pr + 1) * T],
                                out_t[:, (2 * pr + 1) * T:(2 * pr + 2) * T]], axis=0)
        o_ref[0, :, pr * 2 * DH:(pr + 1) * 2 * DH] = pair.T.astype(o_ref.dtype)


def _bias_lookup(table, dist):
    onehot = (_t5_bucket(dist)[..., None] == jnp.arange(table.shape[0])).astype(F32)
    return jnp.einsum('...k,kh->...h', onehot, table, precision=HIGHEST)


def _nsa_t_tables(rel_bias, S):
    T, R, KV = ATT_TILE, NSA_R, NSA_KV
    table = rel_bias.astype(F32) * LOG2E
    ii = jnp.arange(T)
    delta = ii[None, :] - ii[:, None]

    def lanes(a):
        a = jnp.moveaxis(a, -1, 0)
        a = a.reshape((KV, R) + a.shape[1:])
        return jnp.moveaxis(a, 1, 2).reshape(KV, a.shape[2], R * a.shape[3])

    def tile(off):
        return lanes(_bias_lookup(table, off * T + delta))

    mask4 = lambda ok: jnp.tile(jnp.where(ok, 0.0, NEG), (1, R))[None]
    n_far = -(-REL_MAX_DIST // T) + 1
    selb = [tile(o) for o in range(n_far + 1)]
    selb[0] = selb[0] + mask4(delta >= 0)
    selb = jnp.stack(selb, axis=0)
    n_win = WINDOW // T
    winb = [tile(o) + mask4((o * T + delta >= 0) & (o * T + delta < WINDOW)) for o in range(n_win + 1)]
    winb.append(jnp.full_like(winb[0], NEG))
    winb = jnp.stack(winb, axis=0)

    grp = T // CMP_STRIDE
    far = _bias_lookup(table, jnp.asarray(2 * REL_MAX_DIST))
    xx = jnp.arange(4 * grp)
    bdist = ii[None, :] - CMP_STRIDE * (xx[:, None] - 2 * grp) - (CMP_BLOCK - 1)
    band = jnp.where((bdist >= 0)[..., None], _bias_lookup(table, bdist) - far, NEG)
    band = jnp.concatenate([lanes(band), jnp.zeros((KV, 2 * grp, R * T), F32)], axis=1)[:, None]
    cfar = jnp.repeat(far.reshape(KV, R), T, axis=1)[:, None]

    n_pad = S // CMP_STRIDE
    n_sel = S // SEL_BLOCK
    cmp_start = jnp.arange(n_pad) * CMP_STRIDE
    sel_start = jnp.arange(n_sel) * SEL_BLOCK
    overlap = jnp.clip(jnp.minimum(cmp_start[:, None] + CMP_BLOCK, sel_start[None] + SEL_BLOCK)
                       - jnp.maximum(cmp_start[:, None], sel_start[None]), 0).astype(F32) / CMP_BLOCK
    n_cmp = (S - CMP_BLOCK) // CMP_STRIDE + 1
    overlap_t = jnp.where((jnp.arange(n_pad) < n_cmp)[:, None], overlap, 0.0).T
    return cfar, band, selb, winb, overlap_t


def _nsa_t_attention(q4, gv, kcmp, vcmp_t, ks, vs_t, kw, vw_t, tables):
    B, KV, S, _ = kw.shape
    T = ATT_TILE
    cfar, band, selb, winb, overlap_t = tables
    gw = NSA_R * NSA_DH
    n_pad = kcmp.shape[2]
    seq = lambda a: pl.BlockSpec((1, 1) + a.shape[2:], lambda b, g, i: (b, g) + (0,) * (a.ndim - 2))
    qtile = lambda a: pl.BlockSpec((1, 1, 1) + a.shape[3:], lambda b, g, i: (b, g, i, 0, 0))
    grp = lambda a: pl.BlockSpec((1,) + a.shape[1:], lambda b, g, i: (g,) + (0,) * (a.ndim - 1))
    tiles = lambda a: pl.BlockSpec((a.shape[0], 1) + a.shape[2:], lambda b, g, i: (0, g, 0, 0))
    full = lambda a: pl.BlockSpec(a.shape, lambda b, g, i: (0,) * a.ndim)
    return pl.pallas_call(
        _nsa_t_kernel,
        out_shape=jax.ShapeDtypeStruct((B, S, KV * gw), BF16),
        grid=(B, KV, S // T),
        in_specs=[qtile(q4), qtile(gv),
                  seq(kcmp), seq(vcmp_t), seq(ks), seq(vs_t), seq(kw), seq(vw_t),
                  grp(cfar), grp(band), tiles(selb), tiles(winb), full(overlap_t)],
        out_specs=pl.BlockSpec((1, T, gw), lambda b, g, i: (b, i, g)),
        scratch_shapes=[pltpu.VMEM((2 * n_pad + 2 * (T // CMP_STRIDE), NSA_R * T), F32),
                        pltpu.VMEM((S // SEL_BLOCK, T), F32),
                        pltpu.VMEM((2, T, NSA_R * T), BF16)],
        compiler_params=_cparams(("parallel", "parallel", "arbitrary")),
        name="nsa_attention",
    )(q4, gv, kcmp, vcmp_t, ks, vs_t, kw, vw_t, cfar, band, selb, winb, overlap_t)


def _moe_kernel(*refs, n_in, final):
    x_ref, mgate_ref = refs[:2]
    a_refs = refs[2:2 + n_in]
    wo_refs = refs[2 + n_in:2 + 2 * n_in]
    (g_ref, sh_ref, sc_ref, gate_ref, wr_ref, br_ref, before_ref, wg_ref, wu_ref, wd_ref, fg_ref,
     o_ref, x_all, hs_all, rts_all, acc_all, perm_t_all, meta_all) = refs[2 + 2 * n_in:]
    NG, PG, FH = MOE_GROUPS, MOE_PER_GROUP, MOE_HIDDEN
    TP = x_all.shape[0]
    s = pl.program_id(2)

    def prologue(hh):
        x_scr, hs_scr, rts_scr, acc_scr, perm_t_scr = (r.at[hh] for r in (x_all, hs_all, rts_all, acc_all, perm_t_all))
        meta = meta_all.at[hh]
        mix = functools.reduce(lambda a, b: a + b,
                               [_dot(a_ref[0].astype(BF16), wo_ref[...]) for a_ref, wo_ref in zip(a_refs, wo_refs)])
        x = x_ref[0] + mgate_ref[0] * mix
        x_scr[...] = x
        h = _modulated_norm(x, g_ref[...], sh_ref[0], sc_ref[0])
        h_hi = h.astype(BF16)
        h_lo = (h - h_hi.astype(F32)).astype(BF16)
        both = _dot(h_hi, wr_ref[...])
        logits = (both[:, :LANES] + both[:, LANES:] + _dot(h_lo, wr_ref[:, :LANES]) + br_ref[...]).T
        gl = [logits[NG * PG + g:NG * PG + g + 1, :] for g in range(NG)]
        gmax = functools.reduce(jnp.maximum, gl)
        gtop = jnp.full_like(gmax, float(NG - 1))
        for g in reversed(range(NG - 1)):
            gtop = jnp.where(gl[g] == gmax, float(g), gtop)
        p_g = 1.0 / functools.reduce(lambda a, b: a + b, [jnp.exp(v - gmax) for v in gl])
        a = []
        for j in range(PG):
            v = logits[(NG - 1) * PG + j:(NG - 1) * PG + j + 1, :]
            for g in reversed(range(NG - 1)):
                v = jnp.where(gtop == float(g), logits[g * PG + j:g * PG + j + 1, :], v)
            a.append(v)

        def first_max(vals):
            vmax = functools.reduce(jnp.maximum, vals)
            taken = jnp.zeros_like(vmax) > 1.0
            hits = []
            for v in vals:
                hit = (v == vmax) & jnp.logical_not(taken)
                taken = taken | hit
                hits.append(hit)
            return vmax, hits

        v1, hit1 = first_max(a)
        rest = [jnp.where(hh, -jnp.inf, v) for hh, v in zip(hit1, a)]
        v2, hit2 = first_max(rest)
        e2 = jnp.exp(v2 - v1)
        w1 = p_g / (1.0 + e2)
        w2 = p_g * e2 / (1.0 + e2)
        tm = gtop.shape[1]
        row = lax.broadcasted_iota(jnp.int32, (SUBLANES, tm), 0)
        onehot = [jnp.where(gtop == float(g), 1.0, 0.0) for g in range(NG)]
        oh8 = jnp.zeros((SUBLANES, tm), F32)
        for g in range(NG):
            oh8 = jnp.where(row == g, onehot[g], oh8)
        before = _dot(oh8.astype(BF16), before_ref[...])
        pos = jnp.zeros_like(gtop)
        off = jnp.int32(0)
        for g in range(NG):
            cnt = jnp.sum(onehot[g]).astype(jnp.int32)
            meta[g] = off
            meta[NG + g] = cnt
            pos = pos + onehot[g] * (before[g:g + 1, :] + off.astype(F32))
            off = off + cnt
        rt = jnp.where(row == PG, gtop, jnp.where(row == PG + 1, pos, 0.0))
        for j in range(PG):
            wj = jnp.where(hit1[j], w1, jnp.where(hit2[j], w2, 0.0))
            rt = jnp.where(row == j, wj, rt)
        rt_tok = jnp.concatenate([rt, jnp.zeros((LANES - SUBLANES, tm), F32)], axis=0).T
        rid = lax.broadcasted_iota(jnp.int32, (tm, tm), 0).astype(F32)
        cid = lax.broadcasted_iota(jnp.int32, (tm, tm), 1).astype(F32)
        perm = jnp.where(rid == pos, 1.0, 0.0).astype(BF16)
        perm_t = jnp.where(rt_tok[:, PG + 1:PG + 2] == cid, 1.0, 0.0).astype(BF16)
        perm_t_scr[...] = perm_t
        pad = hs_scr.shape[0] - tm
        hs_scr[0:tm, :] = _dot(perm, h_hi).astype(BF16)
        hs_scr[tm:, :] = jnp.zeros((pad, h_hi.shape[1]), BF16)
        r1 = rt.astype(BF16)
        res = rt - r1.astype(F32)
        r2 = res.astype(BF16)
        r3 = (res - r2.astype(F32)).astype(BF16)
        rt_sorted = _dot(r1, perm_t) + _dot(r2, perm_t) + _dot(r3, perm_t)
        rts_scr[0:tm, :] = jnp.concatenate([rt_sorted, jnp.zeros((LANES - SUBLANES, tm), F32)], axis=0).T
        rts_scr[tm:, :] = jnp.full((pad, LANES), -1.0, F32)
        acc_scr[...] = jnp.zeros(acc_scr.shape, F32)

    tm = x_all.shape[1]
    WIN = hs_all.shape[1] - tm

    def experts(c):
        cf = c.astype(F32)
        bases, counts = [], []
        for hh in range(TP):
            off = meta_all[hh, c]
            cnt = meta_all[hh, NG + c]
            base = (off // MOE_ALIGN) * MOE_ALIGN
            bases.append(base)
            counts.append(jnp.where(cnt > 0, (off + cnt - base + WIN - 1) // WIN, 0))

        def win_body(w, carry):
            starts = [pl.multiple_of(jnp.where(w < counts[hh], bases[hh] + w * WIN, tm), MOE_ALIGN) for hh in range(TP)]
            hs = jnp.concatenate([hs_all[hh, pl.ds(starts[hh], WIN), :] for hh in range(TP)], axis=0)
            rt = jnp.concatenate([rts_all[hh, pl.ds(starts[hh], WIN), :] for hh in range(TP)], axis=0)
            in_group = rt[:, PG:PG + 1] == cf
            hid = _silu(_dot(hs, wg_ref[0])) * _dot(hs, wu_ref[0])
            parts = [hid[:, j * FH:(j + 1) * FH] * jnp.where(in_group, rt[:, j:j + 1], 0.0) for j in range(PG)]
            out = _dot(jnp.concatenate(parts, axis=1).astype(BF16), wd_ref[0])
            for hh in range(TP):
                acc_all[hh, pl.ds(starts[hh], WIN), :] += out[hh * WIN:(hh + 1) * WIN]
            return carry

        lax.fori_loop(0, functools.reduce(jnp.maximum, counts), win_body, 0)

    def epilogue(hh):
        ys = acc_all[hh, 0:tm, :]
        ys_hi = ys.astype(BF16)
        ys_lo = (ys - ys_hi.astype(F32)).astype(BF16)
        back = _dot(perm_t_all[hh], jnp.concatenate([ys_hi, ys_lo], axis=1))
        d = ys.shape[1]
        y = x_all[hh] + gate_ref[0] * (back[:, :d] + back[:, d:])
        if final:
            y = y * lax.rsqrt(jnp.mean(y * y, axis=-1, keepdims=True) + EPS) * fg_ref[...]
        o_ref[0] = y

    for hh in range(TP):
        pl.when(s == hh)(functools.partial(prologue, hh))
    pl.when((s >= TP - 1) & (s <= TP + NG - 2))(lambda: experts(s - (TP - 1)))
    for hh in range(TP):
        pl.when(s == NG + TP - 2 + hh)(functools.partial(epilogue, hh))


def _moe(x, mix_gate, acts, w_outs, g, shift, scale, gate, wg, bg, we, be, w_gate, w_up, w_down, final_g, final,
         tm=512):
    B, S, D = x.shape
    n_in = len(acts)
    NG, PG, FH = MOE_GROUPS, MOE_PER_GROUP, MOE_HIDDEN
    wr = jnp.zeros((D, LANES), F32)
    wr = wr.at[:, :NG * PG].set(we.reshape(D, NG * PG).astype(F32)).at[:, NG * PG:NG * PG + NG].set(wg.astype(F32))
    br = jnp.zeros((1, LANES), F32)
    br = br.at[0, :NG * PG].set(be.reshape(NG * PG).astype(F32)).at[0, NG * PG:NG * PG + NG].set(bg.astype(F32))
    wr_hi = wr.astype(BF16)
    wr = jnp.concatenate([wr_hi, (wr - wr_hi.astype(F32)).astype(BF16)], axis=1)
    grp = lambda w: w.reshape(NG, PG, D, FH).transpose(0, 2, 1, 3).reshape(NG, D, PG * FH).astype(BF16)
    wd = w_down.reshape(NG, PG * FH, D).astype(BF16)
    ids = jnp.arange(tm)
    before = (ids[:, None] < ids[None, :]).astype(BF16)
    TP = MOE_TILES
    n_steps = NG + 2 * TP - 2
    vec = pl.BlockSpec((1, 1, D), lambda b, i, s: (b, 0, 0))
    row = pl.BlockSpec((1, D), lambda b, i, s: (0, 0))
    wspec = lambda k, n: pl.BlockSpec((1, k, n), lambda b, i, s: (jnp.clip(s - (TP - 1), 0, NG - 1), 0, 0))
    tokens_in = lambda n: pl.BlockSpec((1, tm, n), lambda b, i, s: (b, i * TP + jnp.minimum(s, TP - 1), 0))
    tokens_out = pl.BlockSpec((1, tm, D), lambda b, i, s: (b, i * TP + jnp.clip(s - (NG + TP - 2), 0, TP - 1), 0))
    const = lambda a: pl.BlockSpec(a.shape, lambda b, i, s: (0,) * a.ndim)
    return pl.pallas_call(
        functools.partial(_moe_kernel, n_in=n_in, final=final),
        out_shape=jax.ShapeDtypeStruct((B, S, D), F32),
        grid=(B, S // (tm * TP), n_steps),
        in_specs=[tokens_in(D), vec] + [tokens_in(a.shape[2]) for a in acts] + [const(w) for w in w_outs]
                 + [row, vec, vec, vec, const(wr), const(br), const(before),
                    wspec(D, PG * FH), wspec(D, PG * FH), wspec(PG * FH, D), row],
        out_specs=tokens_out,
        scratch_shapes=[pltpu.VMEM((TP, tm, D), F32), pltpu.VMEM((TP, tm + MOE_WIN, D), BF16),
                        pltpu.VMEM((TP, tm + MOE_WIN, LANES), F32), pltpu.VMEM((TP, tm + MOE_WIN, D), F32),
                        pltpu.VMEM((TP, tm, tm), BF16), pltpu.SMEM((TP, 2 * NG), jnp.int32)],
        compiler_params=_cparams(("parallel", "parallel", "arbitrary")),
        name="moe",
    )(x, mix_gate, *acts, *w_outs, g.reshape(1, D), shift, scale, gate, wr, br, before, grp(w_gate), grp(w_up), wd,
      final_g.reshape(1, D))


def _mlstm_s5_layer(x, g, shift, scale, w_in, conv_w, b_i, b_f, head_g, s5_params, w_out):
    H = MLSTM_HEADS
    A = MIX_A
    w_if = jnp.zeros((D_MODEL, LANES), F32).at[:, :2 * H].set(w_in[:, 4 * A:4 * A + 2 * H])
    weights = [w_in[:, :2 * A], w_in[:, 2 * A:4 * A], w_if, w_in[:, 4 * A + 2 * H:]]
    qk, vo, ifg, u = _norm_matmul(x, g, shift, scale, [w.astype(BF16) for w in weights], [BF16, BF16, F32, F32])
    gate_bias = jnp.zeros((1, LANES), F32).at[0, :H].set(b_i.astype(F32)).at[0, H:2 * H].set(b_f.astype(F32))
    hm = _mlstm(qk, vo, ifg, conv_w.astype(F32), gate_bias, head_g.reshape(1, A).astype(F32))
    ys = _s5s(u, _s5s_tables(*s5_params))
    w_out = w_out.astype(BF16)
    return [hm, ys], [w_out[:A], w_out[A:]]


def _nsa_layer(x, g, shift, scale, w_in, b_gate, cmp_pos, cmp_w1, cmp_b1, cmp_w2, cmp_b2, rel_bias, w_out):
    B, S, D = x.shape
    KV, R, DH = NSA_KV, NSA_R, NSA_DH
    w_g = jnp.zeros((D, KV, LANES), F32).at[:, :, :3 * R].set(w_in[:, D + 6 * KV_W:].reshape(D, KV, 3 * R))
    b_g = jnp.zeros((KV, LANES), F32).at[:, :3 * R].set(b_gate.reshape(KV, 3 * R).astype(F32))
    kv_cols = lambda i: w_in[:, D + i * KV_W:D + (i + 1) * KV_W]
    w_k = jnp.concatenate([kv_cols(0), kv_cols(2), kv_cols(4)], axis=1)
    w_v = jnp.concatenate([kv_cols(1), kv_cols(3), kv_cols(5)], axis=1)
    weights = [w_in[:, :D], w_k, w_v, w_g.reshape(D, KV * LANES)]
    q4, gv, kc, vc, ks, kw, vs_t, vw_t = _nsa_proj(x, g, shift, scale, [w.astype(BF16) for w in weights],
                                                   b_g.reshape(1, KV * LANES))
    grp = CMP_STRIDE
    xg = jnp.stack([kc, vc]).reshape(2, B, KV * S // grp, grp * DH)
    cmp = _compress(xg, cmp_pos, cmp_w1, cmp_b1, cmp_w2, cmp_b2).reshape(2, B, KV, S // grp, DH).astype(BF16)
    out = _nsa_t_attention(q4, gv, cmp[0], cmp[1].transpose(0, 1, 3, 2), ks, vs_t, kw, vw_t,
                           _nsa_t_tables(rel_bias, S))
    return [out], [w_out.astype(BF16)]


def kernel(x, c, rel_bias, ada_w, ada_b, norm_g, final_g,
           a_w_in, a_conv, a_b_i, a_b_f, a_head_g,
           s5_lam_re, s5_lam_im, s5_log_dt, s5_b_re, s5_b_im, s5_c_re, s5_c_im,
           s5_d, s5_glu_w, s5_glu_b, a_w_out,
           n_w_in, n_b_gate, n_cmp_pos, n_cmp_w1, n_cmp_b1, n_cmp_w2, n_cmp_b2, n_w_out,
           r_grp_w, r_grp_b, r_exp_w, r_exp_b, e_w_gate, e_w_up, e_w_down):
    B, S, D = x.shape
    mod = _ada_mod(c, ada_w, ada_b).reshape(DEPTH, 2, B, 1, 3 * D)
    split = lambda m: (m[..., :D], m[..., D:2 * D], m[..., 2 * D:])
    for layer in range(DEPTH):
        shift, scale, mix_gate = split(mod[layer, 0])
        j = layer // 2
        if layer % 2 == 0:
            s5_params = (s5_lam_re[j], s5_lam_im[j], s5_log_dt[j], s5_b_re[j], s5_b_im[j],
                         s5_c_re[j], s5_c_im[j], s5_d[j], s5_glu_w[j], s5_glu_b[j])
            acts, w_outs = _mlstm_s5_layer(x, norm_g[layer, 0], shift, scale, a_w_in[j], a_conv[j], a_b_i[j],
                                           a_b_f[j], a_head_g[j], s5_params, a_w_out[j])
        else:
            acts, w_outs = _nsa_layer(x, norm_g[layer, 0], shift, scale, n_w_in[j], n_b_gate[j], n_cmp_pos[j],
                                      n_cmp_w1[j], n_cmp_b1[j], n_cmp_w2[j], n_cmp_b2[j], rel_bias, n_w_out[j])
        shift, scale, gate = split(mod[layer, 1])
        x = _moe(x, mix_gate, acts, w_outs, norm_g[layer, 1], shift, scale, gate, r_grp_w[layer], r_grp_b[layer],
                 r_exp_w[layer], r_exp_b[layer], e_w_gate[layer], e_w_up[layer], e_w_down[layer], final_g,
                 final=(layer == DEPTH - 1))
    return x
```

```python
import functools
import math

import jax
import jax.numpy as jnp
from jax import lax
from jax.experimental import pallas as pl
from jax.experimental.pallas import tpu as pltpu

F32 = jnp.float32
BF16 = jnp.bfloat16
HIGHEST = lax.Precision.HIGHEST

D_MODEL = 1024
DEPTH = 2
MIX_A = 512
MLSTM_HEADS = 4
MLSTM_DH = MIX_A // MLSTM_HEADS
MLSTM_CHUNK = 128
CONV_K = 4
S5_GROUP = 16
S5_STATE = 64
S5_CHUNK = 16
NSA_HEADS = 16
NSA_KV = 4
NSA_R = NSA_HEADS // NSA_KV
NSA_DH = D_MODEL // NSA_HEADS
KV_W = NSA_KV * NSA_DH
CMP_BLOCK = 32
CMP_STRIDE = 16
CMP_HIDDEN = 256
SEL_BLOCK = 64
SEL_TOPK = 16
WINDOW = 512
FORCE = 1e9
REL_BUCKETS = 32
REL_MAX_DIST = 128
MOE_GROUPS = 4
MOE_PER_GROUP = 4
MOE_HIDDEN = 256
EPS = 1e-6
NEG = -1e30
BIG = 1e30
LOG2E = math.log2(math.e)

LANES = 128
SUBLANES = 8
ATT_TILE = 256
MOE_WIN = 160
MOE_ALIGN = 16
MOE_TILES = 2
VMEM_LIMIT = 56 * 1024 * 1024


def _cparams(sem):
    return pltpu.CompilerParams(dimension_semantics=sem, vmem_limit_bytes=VMEM_LIMIT)


def _dot(a, b, precision=None):
    return jnp.dot(a, b, preferred_element_type=F32, precision=precision)


def _dot_nt(a, b):
    return lax.dot_general(a, b, (((1,), (1,)), ((), ())), preferred_element_type=F32)


def _sigmoid(x):
    return 1.0 / (1.0 + jnp.exp(-x))


def _silu(x):
    return x * _sigmoid(x)


def _gelu_tanh(x):
    return 0.5 * x * (1.0 + jnp.tanh(math.sqrt(2.0 / math.pi) * (x + 0.044715 * (x * x * x))))


def _modulated_norm(x, g, shift, scale):
    y = x * lax.rsqrt(jnp.mean(x * x, axis=-1, keepdims=True) + EPS) * g
    return y * (1.0 + scale) + shift


def _ada_kernel(c_ref, w_ref, b_ref, o_ref):
    c = c_ref[...]
    o_ref[0] = _dot(_silu(c), w_ref[0]) + b_ref[0]


def _ada_mod(c, ada_w, ada_b):
    B, D = c.shape
    n_mod = ada_w.shape[0] * ada_w.shape[1]
    w = ada_w.reshape(n_mod, D, 3 * D)
    b = ada_b.reshape(n_mod, 1, 3 * D)
    tn = 1024
    return pl.pallas_call(
        _ada_kernel,
        out_shape=jax.ShapeDtypeStruct((n_mod, B, 3 * D), F32),
        grid=(n_mod, 3 * D // tn),
        in_specs=[pl.BlockSpec((B, D), lambda i, j: (0, 0)),
                  pl.BlockSpec((1, D, tn), lambda i, j: (i, 0, j)),
                  pl.BlockSpec((1, 1, tn), lambda i, j: (i, 0, j))],
        out_specs=pl.BlockSpec((1, B, tn), lambda i, j: (i, 0, j)),
        compiler_params=_cparams(("parallel", "parallel")),
        name="ada_mod",
    )(c, w, b)


def _norm_mm_kernel(*refs, n_w):
    x_ref, g_ref, sh_ref, sc_ref = refs[:4]
    w_refs = refs[4:4 + n_w]
    o_refs = refs[4 + n_w:]
    h = _modulated_norm(x_ref[0], g_ref[...], sh_ref[0], sc_ref[0]).astype(BF16)
    for w_ref, o_ref in zip(w_refs, o_refs):
        o_ref[0] = _dot(h, w_ref[...]).astype(o_ref.dtype)


def _norm_matmul(x, g, shift, scale, weights, out_dtypes, tm=512):
    B, S, D = x.shape
    n_w = len(weights)
    vec = pl.BlockSpec((1, 1, D), lambda b, i: (b, 0, 0))
    in_specs = [pl.BlockSpec((1, tm, D), lambda b, i: (b, i, 0)),
                pl.BlockSpec((1, D), lambda b, i: (0, 0)), vec, vec]
    in_specs += [pl.BlockSpec(w.shape, lambda b, i: (0, 0)) for w in weights]
    return pl.pallas_call(
        functools.partial(_norm_mm_kernel, n_w=n_w),
        out_shape=[jax.ShapeDtypeStruct((B, S, w.shape[1]), dt) for w, dt in zip(weights, out_dtypes)],
        grid=(B, S // tm),
        in_specs=in_specs,
        out_specs=[pl.BlockSpec((1, tm, w.shape[1]), lambda b, i: (b, i, 0)) for w in weights],
        compiler_params=_cparams(("parallel", "parallel")),
        name="norm_matmul",
    )(x, g.reshape(1, D), shift, scale, *weights)


def _mlstm_kernel(qk_ref, vo_ref, if_ref, cw_ref, gb_ref, hg_ref, tril_ref, o_ref,
                  xbuf, c_scr, n_scr, m_scr):
    pad = SUBLANES

    @pl.when(pl.program_id(1) == 0)
    def _():
        xbuf[:, 0:pad, :] = jnp.zeros((xbuf.shape[0], pad, 2 * MIX_A), F32)
        c_scr[...] = jnp.zeros_like(c_scr)
        n_scr[...] = jnp.zeros_like(n_scr)
        m_scr[...] = jnp.zeros_like(m_scr)

    for bb in range(qk_ref.shape[0]):
        _mlstm_chunk(qk_ref.at[bb], vo_ref.at[bb], if_ref.at[bb], cw_ref, gb_ref, hg_ref, tril_ref, o_ref.at[bb],
                     xbuf.at[bb], c_scr.at[bb], n_scr.at[bb], m_scr.at[bb])


def _mlstm_chunk(qk_ref, vo_ref, if_ref, cw_ref, gb_ref, hg_ref, tril_ref, o_ref, xbuf, c_scr, n_scr, m_scr):
    L, H, DH = MLSTM_CHUNK, MLSTM_HEADS, MLSTM_DH
    pad = SUBLANES
    xbuf[pad:pad + L, :] = qk_ref[...].astype(F32)
    cw = cw_ref[...]
    conv = None
    for j in range(CONV_K):
        lo = pad - (CONV_K - 1) + j
        t = xbuf[lo:lo + L, :] * cw[j:j + 1, :]
        conv = t if conv is None else conv + t
    xbuf[0:pad, :] = xbuf[L:L + pad, :]
    qk = _silu(conv)
    q = qk[:, :MIX_A]
    k = qk[:, MIX_A:] * (DH ** -0.5)
    vo = vo_ref[...].astype(F32)
    v = vo[:, :MIX_A]
    o_pre = vo[:, MIX_A:]

    ifb = if_ref[...] + gb_ref[...]
    lf = jnp.minimum(ifb, 0.0) - jnp.log1p(jnp.exp(-jnp.abs(ifb)))
    bcs = _dot(tril_ref[...], lf, precision=HIGHEST)
    ifb_t = ifb.T
    bcs_t = bcs.T
    row = lax.broadcasted_iota(jnp.int32, (L, L), 0)
    col = lax.broadcasted_iota(jnp.int32, (L, L), 1)
    causal = col <= row

    outs = []
    for h in range(H):
        sl = slice(h * DH, (h + 1) * DH)
        qh, kh, vh = q[:, sl], k[:, sl], v[:, sl]
        qb, kb = qh.astype(BF16), kh.astype(BF16)
        b_col = bcs[:, H + h:H + h + 1]
        b_row = bcs_t[H + h:H + h + 1, :]
        li_col = ifb[:, h:h + 1]
        li_row = ifb_t[h:h + 1, :]
        b_last = b_col[L - 1:L, :]
        m0 = m_scr[h][:, 0:1]
        c0 = c_scr[h]
        n0 = n_scr[h]

        log_d = jnp.where(causal, b_col - b_row + li_row, NEG)
        log_inter = b_col + m0
        m_t = jnp.maximum(log_inter, jnp.max(log_d, axis=1, keepdims=True))
        dmat = jnp.exp(log_d - m_t)
        a_inter = jnp.exp(log_inter - m_t)
        s = _dot_nt(qb, kb) * dmat
        num = _dot(s.astype(BF16), vh.astype(BF16)) + a_inter * _dot_nt(qb, c0.astype(BF16))
        den = jnp.sum(s, axis=1, keepdims=True) + a_inter * jnp.sum(qh * n0, axis=1, keepdims=True)
        hh = num / jnp.maximum(jnp.abs(den), jnp.exp(-m_t))

        w_col = b_last - b_col + li_col
        m_loc = jnp.max(w_col, axis=0, keepdims=True)
        e = jnp.exp(w_col - m_loc)
        c_loc = _dot((vh * e).T.astype(BF16), kb)
        n_loc = jnp.sum(kh * e, axis=0, keepdims=True)
        m_new = jnp.maximum(b_last + m0, m_loc)
        a = jnp.exp(b_last + m0 - m_new)
        sc = jnp.exp(m_loc - m_new)
        c_scr[h] = a * c0 + sc * c_loc
        n_scr[h] = a * n0 + sc * n_loc
        m_scr[h] = jnp.broadcast_to(m_new, (1, LANES))

        outs.append(hh * lax.rsqrt(jnp.mean(hh * hh, axis=1, keepdims=True) + EPS))
    hm = jnp.concatenate(outs, axis=1)
    o_ref[...] = (_sigmoid(o_pre) * (hm * hg_ref[...])).astype(o_ref.dtype)


def _mlstm(qk, vo, ifg, conv_w, gate_bias, head_g):
    B, S, _ = qk.shape
    rows = 1
    L, H, DH = MLSTM_CHUNK, MLSTM_HEADS, MLSTM_DH
    tril = jnp.tril(jnp.ones((L, L), F32))
    return pl.pallas_call(
        _mlstm_kernel,
        out_shape=jax.ShapeDtypeStruct((B, S, MIX_A), BF16),
        grid=(B // rows, S // L),
        in_specs=[pl.BlockSpec((rows, L, 2 * MIX_A), lambda b, c: (b, c, 0)),
                  pl.BlockSpec((rows, L, 2 * MIX_A), lambda b, c: (b, c, 0)),
                  pl.BlockSpec((rows, L, LANES), lambda b, c: (b, c, 0)),
                  pl.BlockSpec((CONV_K, 2 * MIX_A), lambda b, c: (0, 0)),
                  pl.BlockSpec((1, LANES), lambda b, c: (0, 0)),
                  pl.BlockSpec((1, MIX_A), lambda b, c: (0, 0)),
                  pl.BlockSpec((L, L), lambda b, c: (0, 0))],
        out_specs=pl.BlockSpec((rows, L, MIX_A), lambda b, c: (b, c, 0)),
        scratch_shapes=[pltpu.VMEM((rows, L + SUBLANES, 2 * MIX_A), F32),
                        pltpu.VMEM((rows, H, DH, DH), F32),
                        pltpu.VMEM((rows, H, 1, DH), F32),
                        pltpu.VMEM((rows, H, 1, LANES), F32)],
        compiler_params=_cparams(("parallel", "arbitrary")),
        name="mlstm",
    )(qk, vo, ifg, conv_w, gate_bias, head_g, tril)


S5_LT = LANES // S5_GROUP
S5_PAIRS = S5_CHUNK // 2


def _s5s_kernel(u_ref, h_ref, e_ref, kk_ref, are_ref, aim_ref, d_ref, gw_ref, gb_ref, o_ref, xl_scr, x0_scr):
    n_chunks = u_ref.shape[1] // S5_CHUNK
    half = S5_LT * S5_STATE
    tok = lambda s: u_ref[0, pl.ds(s, n_chunks, stride=S5_CHUNK), :]
    u2 = [jnp.concatenate([tok(2 * q), tok(2 * q + 1)], axis=1) for q in range(S5_PAIRS)]
    u2b = [v.astype(BF16) for v in u2]
    xl_scr[...] = functools.reduce(lambda a, b: a + b, [_dot(u2b[q], h_ref[0, q]) for q in range(S5_PAIRS)])
    a_re = are_ref[0]
    a_im = aim_ref[0]

    def body(a, carry):
        re, im = carry
        x0_scr[pl.ds(a, 1), 0:half] = re
        x0_scr[pl.ds(a, 1), half:2 * half] = im
        return (a_re * re - a_im * im + xl_scr[pl.ds(a, 1), 0:half],
                a_re * im + a_im * re + xl_scr[pl.ds(a, 1), half:2 * half])

    zero = jnp.zeros((1, half), F32)
    lax.fori_loop(0, n_chunks, body, (zero, zero), unroll=8)
    x0 = x0_scr[...].astype(BF16)
    for p in range(S5_PAIRS):
        y = _dot(x0, e_ref[0, p]) + u2[p] * d_ref[0]
        for q in range(p + 1):
            y = y + _dot(u2b[q], kk_ref[0, p - q])
        ys = _gelu_tanh(y)
        out = ys * _sigmoid(_dot(ys.astype(BF16), gw_ref[0]) + gb_ref[0])
        o_ref[0, pl.ds(2 * p, n_chunks, stride=S5_CHUNK), :] = out[:, :LANES].astype(o_ref.dtype)
        o_ref[0, pl.ds(2 * p + 1, n_chunks, stride=S5_CHUNK), :] = out[:, LANES:].astype(o_ref.dtype)


def _s5s_tables(lam_re, lam_im, log_dt, b_re, b_im, c_re, c_im, d_skip, glu_w, glu_b):
    T, C, P, LT = S5_CHUNK, S5_GROUP, S5_STATE, S5_LT
    G = lam_re.shape[0]
    NT = G // LT
    lam = lax.complex(lam_re.astype(F32), lam_im.astype(F32))
    dt = jnp.exp(log_dt.astype(F32))[:, None]
    lam_bar = jnp.exp(lam * dt)
    b_bar = ((lam_bar - 1.0) / lam)[..., None] * lax.complex(b_re.astype(F32), b_im.astype(F32))
    c_mat = lax.complex(c_re.astype(F32), c_im.astype(F32))
    taus = jnp.arange(T + 1, dtype=F32)
    pw = jnp.exp((lam * dt)[:, None, :] * taus[None, :, None])
    eye = jnp.eye(LT, dtype=F32)
    tiles = lambda a: a.reshape((NT, LT) + a.shape[1:])

    kern = jnp.einsum('gcp,gtp,gpd->gtdc', c_mat, pw[:, :T], b_bar, precision=HIGHEST).real
    kblk = jnp.einsum('nitdc,ij->ntidjc', tiles(kern), eye).reshape(NT, T, LANES, LANES)
    kblk = jnp.concatenate([jnp.zeros_like(kblk[:, :1]), kblk], axis=1)
    kk = jnp.stack([jnp.concatenate([jnp.concatenate([kblk[:, 2 * d + 1], kblk[:, 2 * d + 2]], axis=2),
                                     jnp.concatenate([kblk[:, 2 * d], kblk[:, 2 * d + 1]], axis=2)], axis=1)
                    for d in range(T // 2)], axis=1)

    hmat = pw[:, :T][:, ::-1, :, None] * b_bar[:, None]

    def state_cols(m):
        return jnp.einsum('nispc,ij->nsicjp', tiles(m), eye).reshape(NT, T, LANES, LT * P)

    h = jnp.concatenate([state_cols(hmat.real), state_cols(hmat.imag)], axis=3)
    h2 = h.reshape(NT, T // 2, 2 * LANES, 2 * LT * P)

    emat = c_mat[:, None] * pw[:, 1:][:, :, None, :]

    def state_rows(m):
        return jnp.einsum('nitcp,ij->ntjpic', tiles(m), eye).reshape(NT, T, LT * P, LANES)

    e = jnp.concatenate([state_rows(emat.real), state_rows(-emat.imag)], axis=2)
    e2 = e.reshape(NT, T // 2, 2, 2 * LT * P, LANES).transpose(0, 1, 3, 2, 4).reshape(NT, T // 2, 2 * LT * P, 2 * LANES)

    a_re = pw[:, T].real.reshape(NT, 1, LT * P)
    a_im = pw[:, T].imag.reshape(NT, 1, LT * P)
    pair = lambda v: jnp.tile(v.astype(F32).reshape(NT, 1, LANES), (1, 1, 2))
    gwb = jnp.einsum('nice,ij->nicje', tiles(glu_w.astype(F32)), eye).reshape(NT, LANES, LANES)
    zeros = jnp.zeros_like(gwb)
    gw2 = jnp.concatenate([jnp.concatenate([gwb, zeros], axis=2), jnp.concatenate([zeros, gwb], axis=2)], axis=1)
    return (h2.astype(BF16), e2.astype(BF16), kk.astype(BF16), a_re, a_im, pair(d_skip), gw2.astype(BF16), pair(glu_b))


def _s5s(u, tables):
    B, S, W = u.shape
    NT = W // LANES
    n_chunks = S // S5_CHUNK
    per_tile = lambda a: pl.BlockSpec((1,) + a.shape[1:], lambda j, b: (j,) + (0,) * (a.ndim - 1))
    return pl.pallas_call(
        _s5s_kernel,
        out_shape=jax.ShapeDtypeStruct((B, S, W), F32),
        grid=(NT, B),
        in_specs=[pl.BlockSpec((1, S, LANES), lambda j, b: (b, 0, j))] + [per_tile(t) for t in tables],
        out_specs=pl.BlockSpec((1, S, LANES), lambda j, b: (b, 0, j)),
        scratch_shapes=[pltpu.VMEM((n_chunks, 2 * S5_LT * S5_STATE), F32) for _ in range(2)],
        compiler_params=_cparams(("parallel", "parallel")),
        name="s5",
    )(u, *tables)


def _compress_kernel(x_ref, plo_ref, phi_ref, w1_ref, b1_ref, w2_ref, b2_ref, o_ref):
    x = x_ref[0, 0]
    half = x.shape[1]
    w1 = w1_ref[0]
    lo = _dot((x + plo_ref[0]).astype(BF16), w1[:half])
    hi = _dot((x + phi_ref[0]).astype(BF16), w1[half:])
    rows = x.shape[0]
    hid = _gelu_tanh(lo + pltpu.roll(hi, rows - 1, 0) + b1_ref[0])
    o_ref[0, 0] = _dot(hid.astype(BF16), w2_ref[0]) + b2_ref[0]


def _compress(xg, pos, w1, b1, w2, b2):
    _, B, rows, width = xg.shape
    pos_flat = pos.reshape(2, 2, 1, width).astype(F32)
    sel = lambda shape: pl.BlockSpec((1,) + shape, lambda j, b: (j, 0, 0))
    return pl.pallas_call(
        _compress_kernel,
        out_shape=jax.ShapeDtypeStruct((2, B, rows, NSA_DH), F32),
        grid=(2, B),
        in_specs=[pl.BlockSpec((1, 1, rows, width), lambda j, b: (j, b, 0, 0)),
                  sel((1, width)), sel((1, width)),
                  sel((2 * width, CMP_HIDDEN)), sel((1, CMP_HIDDEN)),
                  sel((CMP_HIDDEN, NSA_DH)), sel((1, NSA_DH))],
        out_specs=pl.BlockSpec((1, 1, rows, NSA_DH), lambda j, b: (j, b, 0, 0)),
        compiler_params=_cparams(("parallel", "parallel")),
        name="nsa_compress",
    )(xg, pos_flat[:, 0], pos_flat[:, 1], w1.astype(BF16), b1[:, None].astype(F32),
      w2.astype(BF16), b2[:, None].astype(F32))


def _t5_bucket(dist):
    dist = jnp.maximum(dist, 0)
    max_exact = REL_BUCKETS // 2
    log_ratio = jnp.log(jnp.maximum(dist, 1).astype(F32) / max_exact) / math.log(REL_MAX_DIST / max_exact)
    large = jnp.minimum(max_exact + (log_ratio * (REL_BUCKETS - max_exact)).astype(jnp.int32), REL_BUCKETS - 1)
    return jnp.where(dist < max_exact, dist, large)


def _nsa_proj_kernel(x_ref, g_ref, sh_ref, sc_ref, wq_ref, wk_ref, wv_ref, wg_ref, bg_ref,
                     q4_ref, gv_ref, kc_ref, vc_ref, ks_ref, kw_ref, vst_ref, vwt_ref):
    KV, R, DH, T = NSA_KV, NSA_R, NSA_DH, ATT_TILE
    h = _modulated_norm(x_ref[0], g_ref[...], sh_ref[0], sc_ref[0]).astype(BF16)
    q_t = (_dot(h, wq_ref[...]) * (DH ** -0.5 * LOG2E)).T.astype(BF16)
    gates_t = _sigmoid(_dot(h, wg_ref[...]) + bg_ref[...]).T
    row = lax.broadcasted_iota(jnp.int32, (SUBLANES, R * T), 0)
    for g in range(KV):
        q4_ref[0, g, 0] = jnp.concatenate([q_t[(g * R + r) * DH:(g * R + r + 1) * DH] for r in range(R)], axis=1)
        gv = jnp.zeros((SUBLANES, R * T), F32)
        for j in range(3):
            gj = jnp.concatenate([gates_t[g * LANES + 3 * r + j:g * LANES + 3 * r + j + 1] for r in range(R)], axis=1)
            gv = jnp.where(row == j, gj, gv)
        gv_ref[0, g, 0] = gv
    k3 = _dot(h, wk_ref[...])
    v3 = _dot(h, wv_ref[...])
    vs_t = v3[:, KV_W:2 * KV_W].T.astype(BF16)
    vw_t = v3[:, 2 * KV_W:].T.astype(BF16)
    for g in range(KV):
        cols = slice(g * DH, (g + 1) * DH)
        kc_ref[0, g] = k3[:, cols].astype(BF16)
        vc_ref[0, g] = v3[:, cols].astype(BF16)
        ks_ref[0, g] = k3[:, KV_W + g * DH:KV_W + (g + 1) * DH].astype(BF16)
        kw_ref[0, g] = k3[:, 2 * KV_W + g * DH:2 * KV_W + (g + 1) * DH].astype(BF16)
        vst_ref[0, g, 0] = vs_t[cols]
        vwt_ref[0, g, 0] = vw_t[cols]


def _nsa_proj(x, g, shift, scale, weights, b_gate):
    B, S, D = x.shape
    KV, R, DH, T = NSA_KV, NSA_R, NSA_DH, ATT_TILE
    vec = pl.BlockSpec((1, 1, D), lambda b, i: (b, 0, 0))
    keys = pl.BlockSpec((1, KV, T, DH), lambda b, i: (b, 0, i, 0))
    key_shape = jax.ShapeDtypeStruct((B, KV, S, DH), BF16)
    tile = lambda rows, width: pl.BlockSpec((1, KV, 1, rows, width), lambda b, i: (b, 0, i, 0, 0))
    tile_shape = lambda rows, width, dt: jax.ShapeDtypeStruct((B, KV, S // T, rows, width), dt)
    return pl.pallas_call(
        _nsa_proj_kernel,
        out_shape=[tile_shape(DH, R * T, BF16), tile_shape(SUBLANES, R * T, F32),
                   key_shape, key_shape, key_shape, key_shape,
                   tile_shape(DH, T, BF16), tile_shape(DH, T, BF16)],
        grid=(B, S // T),
        in_specs=[pl.BlockSpec((1, T, D), lambda b, i: (b, i, 0)),
                  pl.BlockSpec((1, D), lambda b, i: (0, 0)), vec, vec]
                 + [pl.BlockSpec(w.shape, lambda b, i: (0, 0)) for w in weights]
                 + [pl.BlockSpec(b_gate.shape, lambda b, i: (0, 0))],
        out_specs=[tile(DH, R * T), tile(SUBLANES, R * T), keys, keys, keys, keys, tile(DH, T), tile(DH, T)],
        compiler_params=_cparams(("parallel", "parallel")),
        name="nsa_proj",
    )(x, g.reshape(1, D), shift, scale, *weights, b_gate)


def _nsa_t_kernel(q4_ref, gv_ref, kc_ref, vct_ref, ks_ref, vst_ref, kw_ref, vwt_ref,
                  cfar_ref, band_ref, selb_ref, winb_ref, ovt_ref, o_ref, s_scr, sel_scr, sbuf):
    T = ATT_TILE
    R, DH = NSA_R, NSA_DH
    qi = pl.program_id(2)
    q0 = qi * T
    n_pad = kc_ref.shape[2]
    n_sel = ovt_ref.shape[0]
    n_far = selb_ref.shape[0] - 1
    n_win = winb_ref.shape[0] - 2
    band_rows = band_ref.shape[2] - T // CMP_STRIDE * 2

    q4 = q4_ref[0, 0, 0]
    t_lane = q0 + lax.broadcasted_iota(jnp.int32, (1, R * T), 1) % T

    ones_rows = DH // 2
    with_ones = lambda v_t: jnp.concatenate([v_t, jnp.ones((ones_rows, v_t.shape[1]), v_t.dtype)], axis=0)
    gvec = lambda j: gv_ref[0, 0, 0, j:j + 1, :]

    grp = T // CMP_STRIDE
    s_scr[0:n_pad, :] = _dot(kc_ref[0, 0], q4) + cfar_ref[0]
    s_scr[n_pad:n_pad + 2 * grp, :] = jnp.zeros((2 * grp, R * T), F32)
    r0 = jnp.maximum(qi * grp - 2 * grp, 0)
    x0 = r0 - (qi * grp - 2 * grp)
    r0 = pl.multiple_of(r0, SUBLANES)
    x0 = pl.multiple_of(x0, SUBLANES)
    s_scr[pl.ds(r0, band_rows), :] += band_ref[0, 0, pl.ds(x0, band_rows), :]
    lim = pl.multiple_of(qi * grp + 2 * grp, SUBLANES)
    s_scr[pl.ds(lim, n_pad), :] = jnp.full((n_pad, R * T), NEG, F32)

    w_subs, w_vals = [], []
    for d in range(n_win + 1):
        kt = jnp.maximum(qi - d, 0)
        off = pl.multiple_of(kt * T, T)
        tile = jnp.where(qi >= d, d, n_win + 1)
        w_subs.append((_dot(kw_ref[0, 0, pl.ds(off, T), :], q4) + winb_ref[tile, 0]).astype(BF16))
        w_vals.append(with_ones(vwt_ref[0, 0, kt]))

    s = s_scr[0:n_pad, :]
    e = jnp.exp2(s - jnp.max(s, axis=0, keepdims=True))
    inv = jnp.where(t_lane >= CMP_BLOCK - 1, 1.0 / jnp.sum(e, axis=0, keepdims=True), 0.0)
    p = e * inv
    o_cmp = _dot(vct_ref[0, 0], p.astype(BF16))
    psum = functools.reduce(lambda a, b: a + b, [p[:, r * T:(r + 1) * T] for r in range(R)])

    m_w = jnp.max(functools.reduce(jnp.maximum, w_subs), axis=0, keepdims=True)
    acc = functools.reduce(lambda a, b: a + b,
                           [_dot(vj, jnp.exp2(sj - m_w)) for sj, vj in zip(w_subs, w_vals)])
    o_win = acc[:DH] * (1.0 / acc[DH:DH + 1])
    out_t = gvec(0) * o_cmp + gvec(2) * o_win

    imp_t = _dot(ovt_ref[...], psum, precision=HIGHEST)
    jj = lax.broadcasted_iota(jnp.int32, (n_sel, T), 0)
    blk_t = (q0 + lax.broadcasted_iota(jnp.int32, (1, T), 1)) // SEL_BLOCK
    forced = (jj == 0) | (jj == blk_t) | (jj == blk_t - 1)
    score = jnp.where(forced, FORCE, jnp.where(jj <= blk_t, imp_t, -1.0))
    n_blk = n_sel // SUBLANES
    rows = [score[v * SUBLANES:(v + 1) * SUBLANES] for v in range(n_blk)]
    cnts = [jnp.zeros((SUBLANES, T), F32) for _ in range(n_blk)]
    sub = lax.broadcasted_iota(jnp.int32, (SUBLANES, T), 0)
    for j2 in range(n_sel):
        c2 = score[j2:j2 + 1, :]
        for v in range(n_blk):
            lo = v * SUBLANES
            if lo > j2:
                beats = c2 >= rows[v]
            elif lo + SUBLANES - 1 <= j2:
                beats = c2 > rows[v]
            else:
                beats = (c2 > rows[v]) | ((c2 >= rows[v]) & (sub > j2 - lo))
            cnts[v] = cnts[v] + jnp.where(beats, 1.0, 0.0)
    cnt = jnp.concatenate(cnts, axis=0)
    chosen = (cnt < float(min(SEL_TOPK, n_sel))) & (jj <= blk_t)
    sel_scr[...] = jnp.where(chosen, 0.0, -BIG)

    def block_mask(kt):
        per_tile = T // SEL_BLOCK
        parts = [jnp.broadcast_to(sel_scr[pl.ds(kt * per_tile + i, 1), :], (SEL_BLOCK, T)) for i in range(per_tile)]
        m1 = jnp.concatenate(parts, axis=0)
        return jnp.concatenate([m1] * R, axis=1)

    def sel_scores(slot, kc):
        off = pl.multiple_of(kc * T, T)
        s = _dot(ks_ref[0, 0, pl.ds(off, T), :], q4)
        s = (s + selb_ref[jnp.clip(qi - kc, 0, n_far), 0] + block_mask(kc)).astype(BF16)
        sbuf[slot] = s
        return jnp.max(s, axis=0, keepdims=True).astype(F32)

    def sel_weighted(slot, kc, m_new):
        return _dot(with_ones(vst_ref[0, 0, kc]), jnp.exp2(sbuf[slot] - m_new.astype(BF16)))

    last_tile = vst_ref.shape[2] - 1

    def sel_body(i, carry):
        m, acc, m_even = carry
        m_odd = sel_scores(1, 2 * i + 1)
        m_new = jnp.maximum(m, m_even)
        acc = jnp.exp2(m - m_new) * acc + sel_weighted(0, 2 * i, m_new)
        m_even = sel_scores(0, jnp.minimum(2 * i + 2, last_tile))
        m_fin = jnp.maximum(m_new, m_odd)
        acc = jnp.exp2(m_new - m_fin) * acc + sel_weighted(1, 2 * i + 1, m_fin)
        return m_fin, acc, m_even

    _, acc, _ = lax.fori_loop(0, qi // 2 + 1, sel_body,
                              (jnp.full((1, R * T), NEG, F32), jnp.zeros((DH + ones_rows, R * T), F32),
                               sel_scores(0, 0)))
    out_t = out_t + gvec(1) * (acc[:DH] * (1.0 / acc[DH:DH + 1]))
    for pr in range(R // 2):
        pair = jnp.concatenate([out_t[:, (2 * pr) * T:(2 * pr + 1) * T],
                                out_t[:, (2 * pr + 1) * T:(2 * pr + 2) * T]], axis=0)
        o_ref[0, :, pr * 2 * DH:(pr + 1) * 2 * DH] = pair.T.astype(o_ref.dtype)


def _bias_lookup(table, dist):
    onehot = (_t5_bucket(dist)[..., None] == jnp.arange(table.shape[0])).astype(F32)
    return jnp.einsum('...k,kh->...h', onehot, table, precision=HIGHEST)


def _nsa_t_tables(rel_bias, S):
    T, R, KV = ATT_TILE, NSA_R, NSA_KV
    table = rel_bias.astype(F32) * LOG2E
    ii = jnp.arange(T)
    delta = ii[None, :] - ii[:, None]

    def lanes(a):
        a = jnp.moveaxis(a, -1, 0)
        a = a.reshape((KV, R) + a.shape[1:])
        return jnp.moveaxis(a, 1, 2).reshape(KV, a.shape[2], R * a.shape[3])

    def tile(off):
        return lanes(_bias_lookup(table, off * T + delta))

    mask4 = lambda ok: jnp.tile(jnp.where(ok, 0.0, NEG), (1, R))[None]
    n_far = -(-REL_MAX_DIST // T) + 1
    selb = [tile(o) for o in range(n_far + 1)]
    selb[0] = selb[0] + mask4(delta >= 0)
    selb = jnp.stack(selb, axis=0)
    n_win = WINDOW // T
    winb = [tile(o) + mask4((o * T + delta >= 0) & (o * T + delta < WINDOW)) for o in range(n_win + 1)]
    winb.append(jnp.full_like(winb[0], NEG))
    winb = jnp.stack(winb, axis=0)

    grp = T // CMP_STRIDE
    far = _bias_lookup(table, jnp.asarray(2 * REL_MAX_DIST))
    xx = jnp.arange(4 * grp)
    bdist = ii[None, :] - CMP_STRIDE * (xx[:, None] - 2 * grp) - (CMP_BLOCK - 1)
    band = jnp.where((bdist >= 0)[..., None], _bias_lookup(table, bdist) - far, NEG)
    band = jnp.concatenate([lanes(band), jnp.zeros((KV, 2 * grp, R * T), F32)], axis=1)[:, None]
    cfar = jnp.repeat(far.reshape(KV, R), T, axis=1)[:, None]

    n_pad = S // CMP_STRIDE
    n_sel = S // SEL_BLOCK
    cmp_start = jnp.arange(n_pad) * CMP_STRIDE
    sel_start = jnp.arange(n_sel) * SEL_BLOCK
    overlap = jnp.clip(jnp.minimum(cmp_start[:, None] + CMP_BLOCK, sel_start[None] + SEL_BLOCK)
                       - jnp.maximum(cmp_start[:, None], sel_start[None]), 0).astype(F32) / CMP_BLOCK
    n_cmp = (S - CMP_BLOCK) // CMP_STRIDE + 1
    overlap_t = jnp.where((jnp.arange(n_pad) < n_cmp)[:, None], overlap, 0.0).T
    return cfar, band, selb, winb, overlap_t


def _nsa_t_attention(q4, gv, kcmp, vcmp_t, ks, vs_t, kw, vw_t, tables):
    B, KV, S, _ = kw.shape
    T = ATT_TILE
    cfar, band, selb, winb, overlap_t = tables
    gw = NSA_R * NSA_DH
    n_pad = kcmp.shape[2]
    seq = lambda a: pl.BlockSpec((1, 1) + a.shape[2:], lambda b, g, i: (b, g) + (0,) * (a.ndim - 2))
    qtile = lambda a: pl.BlockSpec((1, 1, 1) + a.shape[3:], lambda b, g, i: (b, g, i, 0, 0))
    grp = lambda a: pl.BlockSpec((1,) + a.shape[1:], lambda b, g, i: (g,) + (0,) * (a.ndim - 1))
    tiles = lambda a: pl.BlockSpec((a.shape[0], 1) + a.shape[2:], lambda b, g, i: (0, g, 0, 0))
    full = lambda a: pl.BlockSpec(a.shape, lambda b, g, i: (0,) * a.ndim)
    return pl.pallas_call(
        _nsa_t_kernel,
        out_shape=jax.ShapeDtypeStruct((B, S, KV * gw), BF16),
        grid=(B, KV, S // T),
        in_specs=[qtile(q4), qtile(gv),
                  seq(kcmp), seq(vcmp_t), seq(ks), seq(vs_t), seq(kw), seq(vw_t),
                  grp(cfar), grp(band), tiles(selb), tiles(winb), full(overlap_t)],
        out_specs=pl.BlockSpec((1, T, gw), lambda b, g, i: (b, i, g)),
        scratch_shapes=[pltpu.VMEM((2 * n_pad + 2 * (T // CMP_STRIDE), NSA_R * T), F32),
                        pltpu.VMEM((S // SEL_BLOCK, T), F32),
                        pltpu.VMEM((2, T, NSA_R * T), BF16)],
        compiler_params=_cparams(("parallel", "parallel", "arbitrary")),
        name="nsa_attention",
    )(q4, gv, kcmp, vcmp_t, ks, vs_t, kw, vw_t, cfar, band, selb, winb, overlap_t)


def _moe_kernel(*refs, n_in, final):
    x_ref, mgate_ref = refs[:2]
    a_refs = refs[2:2 + n_in]
    wo_refs = refs[2 + n_in:2 + 2 * n_in]
    (g_ref, sh_ref, sc_ref, gate_ref, wr_ref, br_ref, before_ref, wg_ref, wu_ref, wd_ref, fg_ref,
     o_ref, x_all, hs_all, rts_all, acc_all, perm_t_all, meta_all) = refs[2 + 2 * n_in:]
    NG, PG, FH = MOE_GROUPS, MOE_PER_GROUP, MOE_HIDDEN
    TP = x_all.shape[0]
    s = pl.program_id(2)

    def prologue(hh):
        x_scr, hs_scr, rts_scr, acc_scr, perm_t_scr = (r.at[hh] for r in (x_all, hs_all, rts_all, acc_all, perm_t_all))
        meta = meta_all.at[hh]
        mix = functools.reduce(lambda a, b: a + b,
                               [_dot(a_ref[0].astype(BF16), wo_ref[...]) for a_ref, wo_ref in zip(a_refs, wo_refs)])
        x = x_ref[0] + mgate_ref[0] * mix
        x_scr[...] = x
        h = _modulated_norm(x, g_ref[...], sh_ref[0], sc_ref[0])
        h_hi = h.astype(BF16)
        h_lo = (h - h_hi.astype(F32)).astype(BF16)
        both = _dot(h_hi, wr_ref[...])
        logits = (both[:, :LANES] + both[:, LANES:] + _dot(h_lo, wr_ref[:, :LANES]) + br_ref[...]).T
        gl = [logits[NG * PG + g:NG * PG + g + 1, :] for g in range(NG)]
        gmax = functools.reduce(jnp.maximum, gl)
        gtop = jnp.full_like(gmax, float(NG - 1))
        for g in reversed(range(NG - 1)):
            gtop = jnp.where(gl[g] == gmax, float(g), gtop)
        p_g = 1.0 / functools.reduce(lambda a, b: a + b, [jnp.exp(v - gmax) for v in gl])
        a = []
        for j in range(PG):
            v = logits[(NG - 1) * PG + j:(NG - 1) * PG + j + 1, :]
            for g in reversed(range(NG - 1)):
                v = jnp.where(gtop == float(g), logits[g * PG + j:g * PG + j + 1, :], v)
            a.append(v)

        def first_max(vals):
            vmax = functools.reduce(jnp.maximum, vals)
            taken = jnp.zeros_like(vmax) > 1.0
            hits = []
            for v in vals:
                hit = (v == vmax) & jnp.logical_not(taken)
                taken = taken | hit
                hits.append(hit)
            return vmax, hits

        v1, hit1 = first_max(a)
        rest = [jnp.where(hh, -jnp.inf, v) for hh, v in zip(hit1, a)]
        v2, hit2 = first_max(rest)
        e2 = jnp.exp(v2 - v1)
        w1 = p_g / (1.0 + e2)
        w2 = p_g * e2 / (1.0 + e2)
        tm = gtop.shape[1]
        row = lax.broadcasted_iota(jnp.int32, (SUBLANES, tm), 0)
        onehot = [jnp.where(gtop == float(g), 1.0, 0.0) for g in range(NG)]
        oh8 = jnp.zeros((SUBLANES, tm), F32)
        for g in range(NG):
            oh8 = jnp.where(row == g, onehot[g], oh8)
        before = _dot(oh8.astype(BF16), before_ref[...])
        pos = jnp.zeros_like(gtop)
        off = jnp.int32(0)
        for g in range(NG):
            cnt = jnp.sum(onehot[g]).astype(jnp.int32)
            meta[g] = off
            meta[NG + g] = cnt
            pos = pos + onehot[g] * (before[g:g + 1, :] + off.astype(F32))
            off = off + cnt
        rt = jnp.where(row == PG, gtop, jnp.where(row == PG + 1, pos, 0.0))
        for j in range(PG):
            wj = jnp.where(hit1[j], w1, jnp.where(hit2[j], w2, 0.0))
            rt = jnp.where(row == j, wj, rt)
        rt_tok = jnp.concatenate([rt, jnp.zeros((LANES - SUBLANES, tm), F32)], axis=0).T
        rid = lax.broadcasted_iota(jnp.int32, (tm, tm), 0).astype(F32)
        cid = lax.broadcasted_iota(jnp.int32, (tm, tm), 1).astype(F32)
        perm = jnp.where(rid == pos, 1.0, 0.0).astype(BF16)
        perm_t = jnp.where(rt_tok[:, PG + 1:PG + 2] == cid, 1.0, 0.0).astype(BF16)
        perm_t_scr[...] = perm_t
        pad = hs_scr.shape[0] - tm
        hs_scr[0:tm, :] = _dot(perm, h_hi).astype(BF16)
        hs_scr[tm:, :] = jnp.zeros((pad, h_hi.shape[1]), BF16)
        r1 = rt.astype(BF16)
        res = rt - r1.astype(F32)
        r2 = res.astype(BF16)
        r3 = (res - r2.astype(F32)).astype(BF16)
        rt_sorted = _dot(r1, perm_t) + _dot(r2, perm_t) + _dot(r3, perm_t)
        rts_scr[0:tm, :] = jnp.concatenate([rt_sorted, jnp.zeros((LANES - SUBLANES, tm), F32)], axis=0).T
        rts_scr[tm:, :] = jnp.full((pad, LANES), -1.0, F32)
        acc_scr[...] = jnp.zeros(acc_scr.shape, F32)

    tm = x_all.shape[1]
    WIN = hs_all.shape[1] - tm

    def experts(c):
        cf = c.astype(F32)
        bases, counts = [], []
        for hh in range(TP):
            off = meta_all[hh, c]
            cnt = meta_all[hh, NG + c]
            base = (off // MOE_ALIGN) * MOE_ALIGN
            bases.append(base)
            counts.append(jnp.where(cnt > 0, (off + cnt - base + WIN - 1) // WIN, 0))

        def win_body(w, carry):
            starts = [pl.multiple_of(jnp.where(w < counts[hh], bases[hh] + w * WIN, tm), MOE_ALIGN) for hh in range(TP)]
            hs = jnp.concatenate([hs_all[hh, pl.ds(starts[hh], WIN), :] for hh in range(TP)], axis=0)
            rt = jnp.concatenate([rts_all[hh, pl.ds(starts[hh], WIN), :] for hh in range(TP)], axis=0)
            in_group = rt[:, PG:PG + 1] == cf
            hid = _silu(_dot(hs, wg_ref[0])) * _dot(hs, wu_ref[0])
            parts = [hid[:, j * FH:(j + 1) * FH] * jnp.where(in_group, rt[:, j:j + 1], 0.0) for j in range(PG)]
            out = _dot(jnp.concatenate(parts, axis=1).astype(BF16), wd_ref[0])
            for hh in range(TP):
                acc_all[hh, pl.ds(starts[hh], WIN), :] += out[hh * WIN:(hh + 1) * WIN]
            return carry

        lax.fori_loop(0, functools.reduce(jnp.maximum, counts), win_body, 0)

    def epilogue(hh):
        ys = acc_all[hh, 0:tm, :]
        ys_hi = ys.astype(BF16)
        ys_lo = (ys - ys_hi.astype(F32)).astype(BF16)
        back = _dot(perm_t_all[hh], jnp.concatenate([ys_hi, ys_lo], axis=1))
        d = ys.shape[1]
        y = x_all[hh] + gate_ref[0] * (back[:, :d] + back[:, d:])
        if final:
            y = y * lax.rsqrt(jnp.mean(y * y, axis=-1, keepdims=True) + EPS) * fg_ref[...]
        o_ref[0] = y

    for hh in range(TP):
        pl.when(s == hh)(functools.partial(prologue, hh))
    pl.when((s >= TP - 1) & (s <= TP + NG - 2))(lambda: experts(s - (TP - 1)))
    for hh in range(TP):
        pl.when(s == NG + TP - 2 + hh)(functools.partial(epilogue, hh))


def _moe(x, mix_gate, acts, w_outs, g, shift, scale, gate, wg, bg, we, be, w_gate, w_up, w_down, final_g, final,
         tm=512):
    B, S, D = x.shape
    n_in = len(acts)
    NG, PG, FH = MOE_GROUPS, MOE_PER_GROUP, MOE_HIDDEN
    wr = jnp.zeros((D, LANES), F32)
    wr = wr.at[:, :NG * PG].set(we.reshape(D, NG * PG).astype(F32)).at[:, NG * PG:NG * PG + NG].set(wg.astype(F32))
    br = jnp.zeros((1, LANES), F32)
    br = br.at[0, :NG * PG].set(be.reshape(NG * PG).astype(F32)).at[0, NG * PG:NG * PG + NG].set(bg.astype(F32))
    wr_hi = wr.astype(BF16)
    wr = jnp.concatenate([wr_hi, (wr - wr_hi.astype(F32)).astype(BF16)], axis=1)
    grp = lambda w: w.reshape(NG, PG, D, FH).transpose(0, 2, 1, 3).reshape(NG, D, PG * FH).astype(BF16)
    wd = w_down.reshape(NG, PG * FH, D).astype(BF16)
    ids = jnp.arange(tm)
    before = (ids[:, None] < ids[None, :]).astype(BF16)
    TP = MOE_TILES
    n_steps = NG + 2 * TP - 2
    vec = pl.BlockSpec((1, 1, D), lambda b, i, s: (b, 0, 0))
    row = pl.BlockSpec((1, D), lambda b, i, s: (0, 0))
    wspec = lambda k, n: pl.BlockSpec((1, k, n), lambda b, i, s: (jnp.clip(s - (TP - 1), 0, NG - 1), 0, 0))
    tokens_in = lambda n: pl.BlockSpec((1, tm, n), lambda b, i, s: (b, i * TP + jnp.minimum(s, TP - 1), 0))
    tokens_out = pl.BlockSpec((1, tm, D), lambda b, i, s: (b, i * TP + jnp.clip(s - (NG + TP - 2), 0, TP - 1), 0))
    const = lambda a: pl.BlockSpec(a.shape, lambda b, i, s: (0,) * a.ndim)
    return pl.pallas_call(
        functools.partial(_moe_kernel, n_in=n_in, final=final),
        out_shape=jax.ShapeDtypeStruct((B, S, D), F32),
        grid=(B, S // (tm * TP), n_steps),
        in_specs=[tokens_in(D), vec] + [tokens_in(a.shape[2]) for a in acts] + [const(w) for w in w_outs]
                 + [row, vec, vec, vec, const(wr), const(br), const(before),
                    wspec(D, PG * FH), wspec(D, PG * FH), wspec(PG * FH, D), row],
        out_specs=tokens_out,
        scratch_shapes=[pltpu.VMEM((TP, tm, D), F32), pltpu.VMEM((TP, tm + MOE_WIN, D), BF16),
                        pltpu.VMEM((TP, tm + MOE_WIN, LANES), F32), pltpu.VMEM((TP, tm + MOE_WIN, D), F32),
                        pltpu.VMEM((TP, tm, tm), BF16), pltpu.SMEM((TP, 2 * NG), jnp.int32)],
        compiler_params=_cparams(("parallel", "parallel", "arbitrary")),
        name="moe",
    )(x, mix_gate, *acts, *w_outs, g.reshape(1, D), shift, scale, gate, wr, br, before, grp(w_gate), grp(w_up), wd,
      final_g.reshape(1, D))


def _mlstm_s5_layer(x, g, shift, scale, w_in, conv_w, b_i, b_f, head_g, s5_params, w_out):
    H = MLSTM_HEADS
    A = MIX_A
    w_if = jnp.zeros((D_MODEL, LANES), F32).at[:, :2 * H].set(w_in[:, 4 * A:4 * A + 2 * H])
    weights = [w_in[:, :2 * A], w_in[:, 2 * A:4 * A], w_if, w_in[:, 4 * A + 2 * H:]]
    qk, vo, ifg, u = _norm_matmul(x, g, shift, scale, [w.astype(BF16) for w in weights], [BF16, BF16, F32, F32])
    gate_bias = jnp.zeros((1, LANES), F32).at[0, :H].set(b_i.astype(F32)).at[0, H:2 * H].set(b_f.astype(F32))
    hm = _mlstm(qk, vo, ifg, conv_w.astype(F32), gate_bias, head_g.reshape(1, A).astype(F32))
    ys = _s5s(u, _s5s_tables(*s5_params))
    w_out = w_out.astype(BF16)
    return [hm, ys], [w_out[:A], w_out[A:]]


def _nsa_layer(x, g, shift, scale, w_in, b_gate, cmp_pos, cmp_w1, cmp_b1, cmp_w2, cmp_b2, rel_bias, w_out):
    B, S, D = x.shape
    KV, R, DH = NSA_KV, NSA_R, NSA_DH
    w_g = jnp.zeros((D, KV, LANES), F32).at[:, :, :3 * R].set(w_in[:, D + 6 * KV_W:].reshape(D, KV, 3 * R))
    b_g = jnp.zeros((KV, LANES), F32).at[:, :3 * R].set(b_gate.reshape(KV, 3 * R).astype(F32))
    kv_cols = lambda i: w_in[:, D + i * KV_W:D + (i + 1) * KV_W]
    w_k = jnp.concatenate([kv_cols(0), kv_cols(2), kv_cols(4)], axis=1)
    w_v = jnp.concatenate([kv_cols(1), kv_cols(3), kv_cols(5)], axis=1)
    weights = [w_in[:, :D], w_k, w_v, w_g.reshape(D, KV * LANES)]
    q4, gv, kc, vc, ks, kw, vs_t, vw_t = _nsa_proj(x, g, shift, scale, [w.astype(BF16) for w in weights],
                                                   b_g.reshape(1, KV * LANES))
    grp = CMP_STRIDE
    xg = jnp.stack([kc, vc]).reshape(2, B, KV * S // grp, grp * DH)
    cmp = _compress(xg, cmp_pos, cmp_w1, cmp_b1, cmp_w2, cmp_b2).reshape(2, B, KV, S // grp, DH).astype(BF16)
    out = _nsa_t_attention(q4, gv, cmp[0], cmp[1].transpose(0, 1, 3, 2), ks, vs_t, kw, vw_t,
                           _nsa_t_tables(rel_bias, S))
    return [out], [w_out.astype(BF16)]


def kernel(x, c, rel_bias, ada_w, ada_b, norm_g, final_g,
           a_w_in, a_conv, a_b_i, a_b_f, a_head_g,
           s5_lam_re, s5_lam_im, s5_log_dt, s5_b_re, s5_b_im, s5_c_re, s5_c_im,
           s5_d, s5_glu_w, s5_glu_b, a_w_out,
           n_w_in, n_b_gate, n_cmp_pos, n_cmp_w1, n_cmp_b1, n_cmp_w2, n_cmp_b2, n_w_out,
           r_grp_w, r_grp_b, r_exp_w, r_exp_b, e_w_gate, e_w_up, e_w_down):
    B, S, D = x.shape
    mod = _ada_mod(c, ada_w, ada_b).reshape(DEPTH, 2, B, 1, 3 * D)
    split = lambda m: (m[..., :D], m[..., D:2 * D], m[..., 2 * D:])
    for layer in range(DEPTH):
        shift, scale, mix_gate = split(mod[layer, 0])
        j = layer // 2
        if layer % 2 == 0:
            s5_params = (s5_lam_re[j], s5_lam_im[j], s5_log_dt[j], s5_b_re[j], s5_b_im[j],
                         s5_c_re[j], s5_c_im[j], s5_d[j], s5_glu_w[j], s5_glu_b[j])
            acts, w_outs = _mlstm_s5_layer(x, norm_g[layer, 0], shift, scale, a_w_in[j], a_conv[j], a_b_i[j],
                                           a_b_f[j], a_head_g[j], s5_params, a_w_out[j])
        else:
            acts, w_outs = _nsa_layer(x, norm_g[layer, 0], shift, scale, n_w_in[j], n_b_gate[j], n_cmp_pos[j],
                                      n_cmp_w1[j], n_cmp_b1[j], n_cmp_w2[j], n_cmp_b2[j], rel_bias, n_w_out[j])
        shift, scale, gate = split(mod[layer, 1])
        x = _moe(x, mix_gate, acts, w_outs, norm_g[layer, 1], shift, scale, gate, r_grp_w[layer], r_grp_b[layer],
                 r_exp_w[layer], r_exp_b[layer], e_w_gate[layer], e_w_up[layer], e_w_down[layer], final_g,
                 final=(layer == DEPTH - 1))
    return x
```

```python
import functools
import math

import jax
import jax.numpy as jnp
from jax import lax
from jax.experimental import pallas as pl
from jax.experimental.pallas import tpu as pltpu

F32 = jnp.float32
BF16 = jnp.bfloat16
HIGHEST = lax.Precision.HIGHEST

D_MODEL = 1024
DEPTH = 2
MIX_A = 512
MLSTM_HEADS = 4
MLSTM_DH = MIX_A // MLSTM_HEADS
MLSTM_CHUNK = 128
CONV_K = 4
S5_GROUP = 16
S5_STATE = 64
S5_CHUNK = 16
NSA_HEADS = 16
NSA_KV = 4
NSA_R = NSA_HEADS // NSA_KV
NSA_DH = D_MODEL // NSA_HEADS
KV_W = NSA_KV * NSA_DH
CMP_BLOCK = 32
CMP_STRIDE = 16
CMP_HIDDEN = 256
SEL_BLOCK = 64
SEL_TOPK = 16
WINDOW = 512
FORCE = 1e9
REL_BUCKETS = 32
REL_MAX_DIST = 128
MOE_GROUPS = 4
MOE_PER_GROUP = 4
MOE_HIDDEN = 256
EPS = 1e-6
NEG = -1e30
BIG = 1e30
LOG2E = math.log2(math.e)

LANES = 128
SUBLANES = 8
ATT_TILE = 256
MOE_WIN = 160
MOE_ALIGN = 16
MOE_TILES = 2
VMEM_LIMIT = 56 * 1024 * 1024


def _cparams(sem):
    return pltpu.CompilerParams(dimension_semantics=sem, vmem_limit_bytes=VMEM_LIMIT)


def _dot(a, b, precision=None):
    return jnp.dot(a, b, preferred_element_type=F32, precision=precision)


def _dot_nt(a, b):
    return lax.dot_general(a, b, (((1,), (1,)), ((), ())), preferred_element_type=F32)


def _sigmoid(x):
    return 1.0 / (1.0 + jnp.exp(-x))


def _silu(x):
    return x * _sigmoid(x)


def _gelu_tanh(x):
    return 0.5 * x * (1.0 + jnp.tanh(math.sqrt(2.0 / math.pi) * (x + 0.044715 * (x * x * x))))


def _modulated_norm(x, g, shift, scale):
    y = x * lax.rsqrt(jnp.mean(x * x, axis=-1, keepdims=True) + EPS) * g
    return y * (1.0 + scale) + shift


def _ada_kernel(c_ref, w_ref, b_ref, o_ref):
    c = c_ref[...]
    o_ref[0] = _dot(_silu(c), w_ref[0]) + b_ref[0]


def _ada_mod(c, ada_w, ada_b):
    B, D = c.shape
    n_mod = ada_w.shape[0] * ada_w.shape[1]
    w = ada_w.reshape(n_mod, D, 3 * D)
    b = ada_b.reshape(n_mod, 1, 3 * D)
    tn = 1024
    return pl.pallas_call(
        _ada_kernel,
        out_shape=jax.ShapeDtypeStruct((n_mod, B, 3 * D), F32),
        grid=(n_mod, 3 * D // tn),
        in_specs=[pl.BlockSpec((B, D), lambda i, j: (0, 0)),
                  pl.BlockSpec((1, D, tn), lambda i, j: (i, 0, j)),
                  pl.BlockSpec((1, 1, tn), lambda i, j: (i, 0, j))],
        out_specs=pl.BlockSpec((1, B, tn), lambda i, j: (i, 0, j)),
        compiler_params=_cparams(("parallel", "parallel")),
        name="ada_mod",
    )(c, w, b)


def _norm_mm_kernel(*refs, n_w):
    x_ref, g_ref, sh_ref, sc_ref = refs[:4]
    w_refs = refs[4:4 + n_w]
    o_refs = refs[4 + n_w:]
    h = _modulated_norm(x_ref[0], g_ref[...], sh_ref[0], sc_ref[0]).astype(BF16)
    for w_ref, o_ref in zip(w_refs, o_refs):
        o_ref[0] = _dot(h, w_ref[...]).astype(o_ref.dtype)


def _norm_matmul(x, g, shift, scale, weights, out_dtypes, tm=512):
    B, S, D = x.shape
    n_w = len(weights)
    vec = pl.BlockSpec((1, 1, D), lambda b, i: (b, 0, 0))
    in_specs = [pl.BlockSpec((1, tm, D), lambda b, i: (b, i, 0)),
                pl.BlockSpec((1, D), lambda b, i: (0, 0)), vec, vec]
    in_specs += [pl.BlockSpec(w.shape, lambda b, i: (0, 0)) for w in weights]
    return pl.pallas_call(
        functools.partial(_norm_mm_kernel, n_w=n_w),
        out_shape=[jax.ShapeDtypeStruct((B, S, w.shape[1]), dt) for w, dt in zip(weights, out_dtypes)],
        grid=(B, S // tm),
        in_specs=in_specs,
        out_specs=[pl.BlockSpec((1, tm, w.shape[1]), lambda b, i: (b, i, 0)) for w in weights],
        compiler_params=_cparams(("parallel", "parallel")),
        name="norm_matmul",
    )(x, g.reshape(1, D), shift, scale, *weights)


def _mlstm_kernel(qk_ref, vo_ref, if_ref, cw_ref, gb_ref, hg_ref, tril_ref, o_ref,
                  xbuf, c_scr, n_scr, m_scr):
    pad = SUBLANES

    @pl.when(pl.program_id(1) == 0)
    def _():
        xbuf[:, 0:pad, :] = jnp.zeros((xbuf.shape[0], pad, 2 * MIX_A), F32)
        c_scr[...] = jnp.zeros_like(c_scr)
        n_scr[...] = jnp.zeros_like(n_scr)
        m_scr[...] = jnp.zeros_like(m_scr)

    for bb in range(qk_ref.shape[0]):
        _mlstm_chunk(qk_ref.at[bb], vo_ref.at[bb], if_ref.at[bb], cw_ref, gb_ref, hg_ref, tril_ref, o_ref.at[bb],
                     xbuf.at[bb], c_scr.at[bb], n_scr.at[bb], m_scr.at[bb])


def _mlstm_chunk(qk_ref, vo_ref, if_ref, cw_ref, gb_ref, hg_ref, tril_ref, o_ref, xbuf, c_scr, n_scr, m_scr):
    L, H, DH = MLSTM_CHUNK, MLSTM_HEADS, MLSTM_DH
    pad = SUBLANES
    xbuf[pad:pad + L, :] = qk_ref[...].astype(F32)
    cw = cw_ref[...]
    conv = None
    for j in range(CONV_K):
        lo = pad - (CONV_K - 1) + j
        t = xbuf[lo:lo + L, :] * cw[j:j + 1, :]
        conv = t if conv is None else conv + t
    xbuf[0:pad, :] = xbuf[L:L + pad, :]
    qk = _silu(conv)
    q = qk[:, :MIX_A]
    k = qk[:, MIX_A:] * (DH ** -0.5)
    vo = vo_ref[...].astype(F32)
    v = vo[:, :MIX_A]
    o_pre = vo[:, MIX_A:]

    ifb = if_ref[...] + gb_ref[...]
    lf = jnp.minimum(ifb, 0.0) - jnp.log1p(jnp.exp(-jnp.abs(ifb)))
    bcs = _dot(tril_ref[...], lf, precision=HIGHEST)
    ifb_t = ifb.T
    bcs_t = bcs.T
    row = lax.broadcasted_iota(jnp.int32, (L, L), 0)
    col = lax.broadcasted_iota(jnp.int32, (L, L), 1)
    causal = col <= row

    outs = []
    for h in range(H):
        sl = slice(h * DH, (h + 1) * DH)
        qh, kh, vh = q[:, sl], k[:, sl], v[:, sl]
        qb, kb = qh.astype(BF16), kh.astype(BF16)
        b_col = bcs[:, H + h:H + h + 1]
        b_row = bcs_t[H + h:H + h + 1, :]
        li_col = ifb[:, h:h + 1]
        li_row = ifb_t[h:h + 1, :]
        b_last = b_col[L - 1:L, :]
        m0 = m_scr[h][:, 0:1]
        c0 = c_scr[h]
        n0 = n_scr[h]

        log_d = jnp.where(causal, b_col - b_row + li_row, NEG)
        log_inter = b_col + m0
        m_t = jnp.maximum(log_inter, jnp.max(log_d, axis=1, keepdims=True))
        dmat = jnp.exp(log_d - m_t)
        a_inter = jnp.exp(log_inter - m_t)
        s = _dot_nt(qb, kb) * dmat
        num = _dot(s.astype(BF16), vh.astype(BF16)) + a_inter * _dot_nt(qb, c0.astype(BF16))
        den = jnp.sum(s, axis=1, keepdims=True) + a_inter * jnp.sum(qh * n0, axis=1, keepdims=True)
        hh = num / jnp.maximum(jnp.abs(den), jnp.exp(-m_t))

        w_col = b_last - b_col + li_col
        m_loc = jnp.max(w_col, axis=0, keepdims=True)
        e = jnp.exp(w_col - m_loc)
        c_loc = _dot((vh * e).T.astype(BF16), kb)
        n_loc = jnp.sum(kh * e, axis=0, keepdims=True)
        m_new = jnp.maximum(b_last + m0, m_loc)
        a = jnp.exp(b_last + m0 - m_new)
        sc = jnp.exp(m_loc - m_new)
        c_scr[h] = a * c0 + sc * c_loc
        n_scr[h] = a * n0 + sc * n_loc
        m_scr[h] = jnp.broadcast_to(m_new, (1, LANES))

        outs.append(hh * lax.rsqrt(jnp.mean(hh * hh, axis=1, keepdims=True) + EPS))
    hm = jnp.concatenate(outs, axis=1)
    o_ref[...] = (_sigmoid(o_pre) * (hm * hg_ref[...])).astype(o_ref.dtype)


def _mlstm(qk, vo, ifg, conv_w, gate_bias, head_g):
    B, S, _ = qk.shape
    rows = 1
    L, H, DH = MLSTM_CHUNK, MLSTM_HEADS, MLSTM_DH
    tril = jnp.tril(jnp.ones((L, L), F32))
    return pl.pallas_call(
        _mlstm_kernel,
        out_shape=jax.ShapeDtypeStruct((B, S, MIX_A), BF16),
        grid=(B // rows, S // L),
        in_specs=[pl.BlockSpec((rows, L, 2 * MIX_A), lambda b, c: (b, c, 0)),
                  pl.BlockSpec((rows, L, 2 * MIX_A), lambda b, c: (b, c, 0)),
                  pl.BlockSpec((rows, L, LANES), lambda b, c: (b, c, 0)),
                  pl.BlockSpec((CONV_K, 2 * MIX_A), lambda b, c: (0, 0)),
                  pl.BlockSpec((1, LANES), lambda b, c: (0, 0)),
                  pl.BlockSpec((1, MIX_A), lambda b, c: (0, 0)),
                  pl.BlockSpec((L, L), lambda b, c: (0, 0))],
        out_specs=pl.BlockSpec((rows, L, MIX_A), lambda b, c: (b, c, 0)),
        scratch_shapes=[pltpu.VMEM((rows, L + SUBLANES, 2 * MIX_A), F32),
                        pltpu.VMEM((rows, H, DH, DH), F32),
                        pltpu.VMEM((rows, H, 1, DH), F32),
                        pltpu.VMEM((rows, H, 1, LANES), F32)],
        compiler_params=_cparams(("parallel", "arbitrary")),
        name="mlstm",
    )(qk, vo, ifg, conv_w, gate_bias, head_g, tril)


S5_LT = LANES // S5_GROUP
S5_PAIRS = S5_CHUNK // 2


def _s5s_kernel(u_ref, h_ref, e_ref, kk_ref, are_ref, aim_ref, d_ref, gw_ref, gb_ref, o_ref, xl_scr, x0_scr):
    n_chunks = u_ref.shape[1] // S5_CHUNK
    half = S5_LT * S5_STATE
    tok = lambda s: u_ref[0, pl.ds(s, n_chunks, stride=S5_CHUNK), :]
    u2 = [jnp.concatenate([tok(2 * q), tok(2 * q + 1)], axis=1) for q in range(S5_PAIRS)]
    u2b = [v.astype(BF16) for v in u2]
    xl_scr[...] = functools.reduce(lambda a, b: a + b, [_dot(u2b[q], h_ref[0, q]) for q in range(S5_PAIRS)])
    a_re = are_ref[0]
    a_im = aim_ref[0]

    def body(a, carry):
        re, im = carry
        x0_scr[pl.ds(a, 1), 0:half] = re
        x0_scr[pl.ds(a, 1), half:2 * half] = im
        return (a_re * re - a_im * im + xl_scr[pl.ds(a, 1), 0:half],
                a_re * im + a_im * re + xl_scr[pl.ds(a, 1), half:2 * half])

    zero = jnp.zeros((1, half), F32)
    lax.fori_loop(0, n_chunks, body, (zero, zero), unroll=8)
    x0 = x0_scr[...].astype(BF16)
    for p in range(S5_PAIRS):
        y = _dot(x0, e_ref[0, p]) + u2[p] * d_ref[0]
        for q in range(p + 1):
            y = y + _dot(u2b[q], kk_ref[0, p - q])
        ys = _gelu_tanh(y)
        out = ys * _sigmoid(_dot(ys.astype(BF16), gw_ref[0]) + gb_ref[0])
        o_ref[0, pl.ds(2 * p, n_chunks, stride=S5_CHUNK), :] = out[:, :LANES].astype(o_ref.dtype)
        o_ref[0, pl.ds(2 * p + 1, n_chunks, stride=S5_CHUNK), :] = out[:, LANES:].astype(o_ref.dtype)


def _s5s_tables(lam_re, lam_im, log_dt, b_re, b_im, c_re, c_im, d_skip, glu_w, glu_b):
    T, C, P, LT = S5_CHUNK, S5_GROUP, S5_STATE, S5_LT
    G = lam_re.shape[0]
    NT = G // LT
    lam = lax.complex(lam_re.astype(F32), lam_im.astype(F32))
    dt = jnp.exp(log_dt.astype(F32))[:, None]
    lam_bar = jnp.exp(lam * dt)
    b_bar = ((lam_bar - 1.0) / lam)[..., None] * lax.complex(b_re.astype(F32), b_im.astype(F32))
    c_mat = lax.complex(c_re.astype(F32), c_im.astype(F32))
    taus = jnp.arange(T + 1, dtype=F32)
    pw = jnp.exp((lam * dt)[:, None, :] * taus[None, :, None])
    eye = jnp.eye(LT, dtype=F32)
    tiles = lambda a: a.reshape((NT, LT) + a.shape[1:])

    kern = jnp.einsum('gcp,gtp,gpd->gtdc', c_mat, pw[:, :T], b_bar, precision=HIGHEST).real
    kblk = jnp.einsum('nitdc,ij->ntidjc', tiles(kern), eye).reshape(NT, T, LANES, LANES)
    kblk = jnp.concatenate([jnp.zeros_like(kblk[:, :1]), kblk], axis=1)
    kk = jnp.stack([jnp.concatenate([jnp.concatenate([kblk[:, 2 * d + 1], kblk[:, 2 * d + 2]], axis=2),
                                     jnp.concatenate([kblk[:, 2 * d], kblk[:, 2 * d + 1]], axis=2)], axis=1)
                    for d in range(T // 2)], axis=1)

    hmat = pw[:, :T][:, ::-1, :, None] * b_bar[:, None]

    def state_cols(m):
        return jnp.einsum('nispc,ij->nsicjp', tiles(m), eye).reshape(NT, T, LANES, LT * P)

    h = jnp.concatenate([state_cols(hmat.real), state_cols(hmat.imag)], axis=3)
    h2 = h.reshape(NT, T // 2, 2 * LANES, 2 * LT * P)

    emat = c_mat[:, None] * pw[:, 1:][:, :, None, :]

    def state_rows(m):
        return jnp.einsum('nitcp,ij->ntjpic', tiles(m), eye).reshape(NT, T, LT * P, LANES)

    e = jnp.concatenate([state_rows(emat.real), state_rows(-emat.imag)], axis=2)
    e2 = e.reshape(NT, T // 2, 2, 2 * LT * P, LANES).transpose(0, 1, 3, 2, 4).reshape(NT, T // 2, 2 * LT * P, 2 * LANES)

    a_re = pw[:, T].real.reshape(NT, 1, LT * P)
    a_im = pw[:, T].imag.reshape(NT, 1, LT * P)
    pair = lambda v: jnp.tile(v.astype(F32).reshape(NT, 1, LANES), (1, 1, 2))
    gwb = jnp.einsum('nice,ij->nicje', tiles(glu_w.astype(F32)), eye).reshape(NT, LANES, LANES)
    zeros = jnp.zeros_like(gwb)
    gw2 = jnp.concatenate([jnp.concatenate([gwb, zeros], axis=2), jnp.concatenate([zeros, gwb], axis=2)], axis=1)
    return (h2.astype(BF16), e2.astype(BF16), kk.astype(BF16), a_re, a_im, pair(d_skip), gw2.astype(BF16), pair(glu_b))


def _s5s(u, tables):
    B, S, W = u.shape
    NT = W // LANES
    n_chunks = S // S5_CHUNK
    per_tile = lambda a: pl.BlockSpec((1,) + a.shape[1:], lambda j, b: (j,) + (0,) * (a.ndim - 1))
    return pl.pallas_call(
        _s5s_kernel,
        out_shape=jax.ShapeDtypeStruct((B, S, W), F32),
        grid=(NT, B),
        in_specs=[pl.BlockSpec((1, S, LANES), lambda j, b: (b, 0, j))] + [per_tile(t) for t in tables],
        out_specs=pl.BlockSpec((1, S, LANES), lambda j, b: (b, 0, j)),
        scratch_shapes=[pltpu.VMEM((n_chunks, 2 * S5_LT * S5_STATE), F32) for _ in range(2)],
        compiler_params=_cparams(("parallel", "parallel")),
        name="s5",
    )(u, *tables)


def _compress_kernel(x_ref, plo_ref, phi_ref, w1_ref, b1_ref, w2_ref, b2_ref, o_ref):
    x = x_ref[0, 0]
    half = x.shape[1]
    w1 = w1_ref[0]
    lo = _dot((x + plo_ref[0]).astype(BF16), w1[:half])
    hi = _dot((x + phi_ref[0]).astype(BF16), w1[half:])
    rows = x.shape[0]
    hid = _gelu_tanh(lo + pltpu.roll(hi, rows - 1, 0) + b1_ref[0])
    o_ref[0, 0] = _dot(hid.astype(BF16), w2_ref[0]) + b2_ref[0]


def _compress(xg, pos, w1, b1, w2, b2):
    _, B, rows, width = xg.shape
    pos_flat = pos.reshape(2, 2, 1, width).astype(F32)
    sel = lambda shape: pl.BlockSpec((1,) + shape, lambda j, b: (j, 0, 0))
    return pl.pallas_call(
        _compress_kernel,
        out_shape=jax.ShapeDtypeStruct((2, B, rows, NSA_DH), F32),
        grid=(2, B),
        in_specs=[pl.BlockSpec((1, 1, rows, width), lambda j, b: (j, b, 0, 0)),
                  sel((1, width)), sel((1, width)),
                  sel((2 * width, CMP_HIDDEN)), sel((1, CMP_HIDDEN)),
                  sel((CMP_HIDDEN, NSA_DH)), sel((1, NSA_DH))],
        out_specs=pl.BlockSpec((1, 1, rows, NSA_DH), lambda j, b: (j, b, 0, 0)),
        compiler_params=_cparams(("parallel", "parallel")),
        name="nsa_compress",
    )(xg, pos_flat[:, 0], pos_flat[:, 1], w1.astype(BF16), b1[:, None].astype(F32),
      w2.astype(BF16), b2[:, None].astype(F32))


def _t5_bucket(dist):
    dist = jnp.maximum(dist, 0)
    max_exact = REL_BUCKETS // 2
    log_ratio = jnp.log(jnp.maximum(dist, 1).astype(F32) / max_exact) / math.log(REL_MAX_DIST / max_exact)
    large = jnp.minimum(max_exact + (log_ratio * (REL_BUCKETS - max_exact)).astype(jnp.int32), REL_BUCKETS - 1)
    return jnp.where(dist < max_exact, dist, large)


def _nsa_proj_kernel(x_ref, g_ref, sh_ref, sc_ref, wq_ref, wk_ref, wv_ref, wg_ref, bg_ref,
                     q4_ref, gv_ref, kc_ref, vc_ref, ks_ref, kw_ref, vst_ref, vwt_ref):
    KV, R, DH, T = NSA_KV, NSA_R, NSA_DH, ATT_TILE
    h = _modulated_norm(x_ref[0], g_ref[...], sh_ref[0], sc_ref[0]).astype(BF16)
    q_t = (_dot(h, wq_ref[...]) * (DH ** -0.5 * LOG2E)).T.astype(BF16)
    gates_t = _sigmoid(_dot(h, wg_ref[...]) + bg_ref[...]).T
    row = lax.broadcasted_iota(jnp.int32, (SUBLANES, R * T), 0)
    for g in range(KV):
        q4_ref[0, g, 0] = jnp.concatenate([q_t[(g * R + r) * DH:(g * R + r + 1) * DH] for r in range(R)], axis=1)
        gv = jnp.zeros((SUBLANES, R * T), F32)
        for j in range(3):
            gj = jnp.concatenate([gates_t[g * LANES + 3 * r + j:g * LANES + 3 * r + j + 1] for r in range(R)], axis=1)
            gv = jnp.where(row == j, gj, gv)
        gv_ref[0, g, 0] = gv
    k3 = _dot(h, wk_ref[...])
    v3 = _dot(h, wv_ref[...])
    vs_t = v3[:, KV_W:2 * KV_W].T.astype(BF16)
    vw_t = v3[:, 2 * KV_W:].T.astype(BF16)
    for g in range(KV):
        cols = slice(g * DH, (g + 1) * DH)
        kc_ref[0, g] = k3[:, cols].astype(BF16)
        vc_ref[0, g] = v3[:, cols].astype(BF16)
        ks_ref[0, g] = k3[:, KV_W + g * DH:KV_W + (g + 1) * DH].astype(BF16)
        kw_ref[0, g] = k3[:, 2 * KV_W + g * DH:2 * KV_W + (g + 1) * DH].astype(BF16)
        vst_ref[0, g, 0] = vs_t[cols]
        vwt_ref[0, g, 0] = vw_t[cols]


def _nsa_proj(x, g, shift, scale, weights, b_gate):
    B, S, D = x.shape
    KV, R, DH, T = NSA_KV, NSA_R, NSA_DH, ATT_TILE
    vec = pl.BlockSpec((1, 1, D), lambda b, i: (b, 0, 0))
    keys = pl.BlockSpec((1, KV, T, DH), lambda b, i: (b, 0, i, 0))
    key_shape = jax.ShapeDtypeStruct((B, KV, S, DH), BF16)
    tile = lambda rows, width: pl.BlockSpec((1, KV, 1, rows, width), lambda b, i: (b, 0, i, 0, 0))
    tile_shape = lambda rows, width, dt: jax.ShapeDtypeStruct((B, KV, S // T, rows, width), dt)
    return pl.pallas_call(
        _nsa_proj_kernel,
        out_shape=[tile_shape(DH, R * T, BF16), tile_shape(SUBLANES, R * T, F32),
                   key_shape, key_shape, key_shape, key_shape,
                   tile_shape(DH, T, BF16), tile_shape(DH, T, BF16)],
        grid=(B, S // T),
        in_specs=[pl.BlockSpec((1, T, D), lambda b, i: (b, i, 0)),
                  pl.BlockSpec((1, D), lambda b, i: (0, 0)), vec, vec]
                 + [pl.BlockSpec(w.shape, lambda b, i: (0, 0)) for w in weights]
                 + [pl.BlockSpec(b_gate.shape, lambda b, i: (0, 0))],
        out_specs=[tile(DH, R * T), tile(SUBLANES, R * T), keys, keys, keys, keys, tile(DH, T), tile(DH, T)],
        compiler_params=_cparams(("parallel", "parallel")),
        name="nsa_proj",
    )(x, g.reshape(1, D), shift, scale, *weights, b_gate)


def _nsa_t_kernel(q4_ref, gv_ref, kc_ref, vct_ref, ks_ref, vst_ref, kw_ref, vwt_ref,
                  cfar_ref, band_ref, selb_ref, winb_ref, ovt_ref, o_ref, s_scr, sel_scr, sbuf):
    T = ATT_TILE
    R, DH = NSA_R, NSA_DH
    qi = pl.program_id(2)
    q0 = qi * T
    n_pad = kc_ref.shape[2]
    n_sel = ovt_ref.shape[0]
    n_far = selb_ref.shape[0] - 1
    n_win = winb_ref.shape[0] - 2
    band_rows = band_ref.shape[2] - T // CMP_STRIDE * 2

    q4 = q4_ref[0, 0, 0]
    t_lane = q0 + lax.broadcasted_iota(jnp.int32, (1, R * T), 1) % T

    ones_rows = DH
    with_ones = lambda v_t: jnp.concatenate([v_t, jnp.ones((ones_rows, v_t.shape[1]), v_t.dtype)], axis=0)
    gvec = lambda j: gv_ref[0, 0, 0, j:j + 1, :]

    grp = T // CMP_STRIDE
    s_scr[0:n_pad, :] = _dot(kc_ref[0, 0], q4) + cfar_ref[0]
    s_scr[n_pad:n_pad + 2 * grp, :] = jnp.zeros((2 * grp, R * T), F32)
    r0 = jnp.maximum(qi * grp - 2 * grp, 0)
    x0 = r0 - (qi * grp - 2 * grp)
    r0 = pl.multiple_of(r0, SUBLANES)
    x0 = pl.multiple_of(x0, SUBLANES)
    s_scr[pl.ds(r0, band_rows), :] += band_ref[0, 0, pl.ds(x0, band_rows), :]
    lim = pl.multiple_of(qi * grp + 2 * grp, SUBLANES)
    s_scr[pl.ds(lim, n_pad), :] = jnp.full((n_pad, R * T), NEG, F32)

    w_subs, w_vals = [], []
    for d in range(n_win + 1):
        kt = jnp.maximum(qi - d, 0)
        off = pl.multiple_of(kt * T, T)
        tile = jnp.where(qi >= d, d, n_win + 1)
        w_subs.append((_dot(kw_ref[0, 0, pl.ds(off, T), :], q4) + winb_ref[tile, 0]).astype(BF16))
        w_vals.append(with_ones(vwt_ref[0, 0, kt]))

    s = s_scr[0:n_pad, :]
    e = jnp.exp2(s - jnp.max(s, axis=0, keepdims=True))
    inv = jnp.where(t_lane >= CMP_BLOCK - 1, 1.0 / jnp.sum(e, axis=0, keepdims=True), 0.0)
    p = e * inv
    o_cmp = _dot(vct_ref[0, 0], p.astype(BF16))
    psum = functools.reduce(lambda a, b: a + b, [p[:, r * T:(r + 1) * T] for r in range(R)])

    m_w = jnp.max(functools.reduce(jnp.maximum, w_subs), axis=0, keepdims=True)
    acc = functools.reduce(lambda a, b: a + b,
                           [_dot(vj, jnp.exp2(sj - m_w)) for sj, vj in zip(w_subs, w_vals)])
    o_win = acc[:DH] * (1.0 / acc[DH:DH + 1])
    out_t = gvec(0) * o_cmp + gvec(2) * o_win

    imp_t = _dot(ovt_ref[...], psum, precision=HIGHEST)
    jj = lax.broadcasted_iota(jnp.int32, (n_sel, T), 0)
    blk_t = (q0 + lax.broadcasted_iota(jnp.int32, (1, T), 1)) // SEL_BLOCK
    forced = (jj == 0) | (jj == blk_t) | (jj == blk_t - 1)
    score = jnp.where(forced, FORCE, jnp.where(jj <= blk_t, imp_t, -1.0))
    n_blk = n_sel // SUBLANES
    rows = [score[v * SUBLANES:(v + 1) * SUBLANES] for v in range(n_blk)]
    cnts = [jnp.zeros((SUBLANES, T), F32) for _ in range(n_blk)]
    sub = lax.broadcasted_iota(jnp.int32, (SUBLANES, T), 0)
    for j2 in range(n_sel):
        c2 = score[j2:j2 + 1, :]
        for v in range(n_blk):
            lo = v * SUBLANES
            if lo > j2:
                beats = c2 >= rows[v]
            elif lo + SUBLANES - 1 <= j2:
                beats = c2 > rows[v]
            else:
                beats = (c2 > rows[v]) | ((c2 >= rows[v]) & (sub > j2 - lo))
            cnts[v] = cnts[v] + jnp.where(beats, 1.0, 0.0)
    cnt = jnp.concatenate(cnts, axis=0)
    chosen = (cnt < float(min(SEL_TOPK, n_sel))) & (jj <= blk_t)
    sel_scr[...] = jnp.where(chosen, 0.0, -BIG)

    def block_mask(kt):
        per_tile = T // SEL_BLOCK
        parts = [jnp.broadcast_to(sel_scr[pl.ds(kt * per_tile + i, 1), :], (SEL_BLOCK, T)) for i in range(per_tile)]
        m1 = jnp.concatenate(parts, axis=0)
        return jnp.concatenate([m1] * R, axis=1)

    def sel_scores(slot, kc):
        off = pl.multiple_of(kc * T, T)
        s = _dot(ks_ref[0, 0, pl.ds(off, T), :], q4)
        s = (s + selb_ref[jnp.clip(qi - kc, 0, n_far), 0] + block_mask(kc)).astype(BF16)
        sbuf[slot] = s
        return jnp.max(s, axis=0, keepdims=True).astype(F32)

    def sel_weighted(slot, kc, m_new):
        return _dot(with_ones(vst_ref[0, 0, kc]), jnp.exp2(sbuf[slot] - m_new.astype(BF16)))

    last_tile = vst_ref.shape[2] - 1

    def sel_body(i, carry):
        m, acc, m_even = carry
        m_odd = sel_scores(1, 2 * i + 1)
        m_new = jnp.maximum(m, m_even)
        acc = jnp.exp2(m - m_new) * acc + sel_weighted(0, 2 * i, m_new)
        m_even = sel_scores(0, jnp.minimum(2 * i + 2, last_tile))
        m_fin = jnp.maximum(m_new, m_odd)
        acc = jnp.exp2(m_new - m_fin) * acc + sel_weighted(1, 2 * i + 1, m_fin)
        return m_fin, acc, m_even

    _, acc, _ = lax.fori_loop(0, qi // 2 + 1, sel_body,
                              (jnp.full((1, R * T), NEG, F32), jnp.zeros((DH + ones_rows, R * T), F32),
                               sel_scores(0, 0)))
    out_t = out_t + gvec(1) * (acc[:DH] * (1.0 / acc[DH:DH + 1]))
    for pr in range(R // 2):
        pair = jnp.concatenate([out_t[:, (2 * pr) * T:(2 * pr + 1) * T],
                                out_t[:, (2 * pr + 1) * T:(2 * pr + 2) * T]], axis=0)
        o_ref[0, :, pr * 2 * DH:(pr + 1) * 2 * DH] = pair.T.astype(o_ref.dtype)


def _bias_lookup(table, dist):
    onehot = (_t5_bucket(dist)[..., None] == jnp.arange(table.shape[0])).astype(F32)
    return jnp.einsum('...k,kh->...h', onehot, table, precision=HIGHEST)


def _nsa_t_tables(rel_bias, S):
    T, R, KV = ATT_TILE, NSA_R, NSA_KV
    table = rel_bias.astype(F32) * LOG2E
    ii = jnp.arange(T)
    delta = ii[None, :] - ii[:, None]

    def lanes(a):
        a = jnp.moveaxis(a, -1, 0)
        a = a.reshape((KV, R) + a.shape[1:])
        return jnp.moveaxis(a, 1, 2).reshape(KV, a.shape[2], R * a.shape[3])

    def tile(off):
        return lanes(_bias_lookup(table, off * T + delta))

    mask4 = lambda ok: jnp.tile(jnp.where(ok, 0.0, NEG), (1, R))[None]
    n_far = -(-REL_MAX_DIST // T) + 1
    selb = [tile(o) for o in range(n_far + 1)]
    selb[0] = selb[0] + mask4(delta >= 0)
    selb = jnp.stack(selb, axis=0)
    n_win = WINDOW // T
    winb = [tile(o) + mask4((o * T + delta >= 0) & (o * T + delta < WINDOW)) for o in range(n_win + 1)]
    winb.append(jnp.full_like(winb[0], NEG))
    winb = jnp.stack(winb, axis=0)

    grp = T // CMP_STRIDE
    far = _bias_lookup(table, jnp.asarray(2 * REL_MAX_DIST))
    xx = jnp.arange(4 * grp)
    bdist = ii[None, :] - CMP_STRIDE * (xx[:, None] - 2 * grp) - (CMP_BLOCK - 1)
    band = jnp.where((bdist >= 0)[..., None], _bias_lookup(table, bdist) - far, NEG)
    band = jnp.concatenate([lanes(band), jnp.zeros((KV, 2 * grp, R * T), F32)], axis=1)[:, None]
    cfar = jnp.repeat(far.reshape(KV, R), T, axis=1)[:, None]

    n_pad = S // CMP_STRIDE
    n_sel = S // SEL_BLOCK
    cmp_start = jnp.arange(n_pad) * CMP_STRIDE
    sel_start = jnp.arange(n_sel) * SEL_BLOCK
    overlap = jnp.clip(jnp.minimum(cmp_start[:, None] + CMP_BLOCK, sel_start[None] + SEL_BLOCK)
                       - jnp.maximum(cmp_start[:, None], sel_start[None]), 0).astype(F32) / CMP_BLOCK
    n_cmp = (S - CMP_BLOCK) // CMP_STRIDE + 1
    overlap_t = jnp.where((jnp.arange(n_pad) < n_cmp)[:, None], overlap, 0.0).T
    return cfar, band, selb, winb, overlap_t


def _nsa_t_attention(q4, gv, kcmp, vcmp_t, ks, vs_t, kw, vw_t, tables):
    B, KV, S, _ = kw.shape
    T = ATT_TILE
    cfar, band, selb, winb, overlap_t = tables
    gw = NSA_R * NSA_DH
    n_pad = kcmp.shape[2]
    seq = lambda a: pl.BlockSpec((1, 1) + a.shape[2:], lambda b, g, i: (b, g) + (0,) * (a.ndim - 2))
    qtile = lambda a: pl.BlockSpec((1, 1, 1) + a.shape[3:], lambda b, g, i: (b, g, i, 0, 0))
    grp = lambda a: pl.BlockSpec((1,) + a.shape[1:], lambda b, g, i: (g,) + (0,) * (a.ndim - 1))
    tiles = lambda a: pl.BlockSpec((a.shape[0], 1) + a.shape[2:], lambda b, g, i: (0, g, 0, 0))
    full = lambda a: pl.BlockSpec(a.shape, lambda b, g, i: (0,) * a.ndim)
    return pl.pallas_call(
        _nsa_t_kernel,
        out_shape=jax.ShapeDtypeStruct((B, S, KV * gw), BF16),
        grid=(B, KV, S // T),
        in_specs=[qtile(q4), qtile(gv),
                  seq(kcmp), seq(vcmp_t), seq(ks), seq(vs_t), seq(kw), seq(vw_t),
                  grp(cfar), grp(band), tiles(selb), tiles(winb), full(overlap_t)],
        out_specs=pl.BlockSpec((1, T, gw), lambda b, g, i: (b, i, g)),
        scratch_shapes=[pltpu.VMEM((2 * n_pad + 2 * (T // CMP_STRIDE), NSA_R * T), F32),
                        pltpu.VMEM((S // SEL_BLOCK, T), F32),
                        pltpu.VMEM((2, T, NSA_R * T), BF16)],
        compiler_params=_cparams(("parallel", "parallel", "arbitrary")),
        name="nsa_attention",
    )(q4, gv, kcmp, vcmp_t, ks, vs_t, kw, vw_t, cfar, band, selb, winb, overlap_t)


def _moe_kernel(*refs, n_in, final):
    x_ref, mgate_ref = refs[:2]
    a_refs = refs[2:2 + n_in]
    wo_refs = refs[2 + n_in:2 + 2 * n_in]
    (g_ref, sh_ref, sc_ref, gate_ref, wr_ref, br_ref, before_ref, wg_ref, wu_ref, wd_ref, fg_ref,
     o_ref, x_all, hs_all, rts_all, acc_all, perm_t_all, meta_all) = refs[2 + 2 * n_in:]
    NG, PG, FH = MOE_GROUPS, MOE_PER_GROUP, MOE_HIDDEN
    TP = x_all.shape[0]
    s = pl.program_id(2)

    def prologue(hh):
        x_scr, hs_scr, rts_scr, acc_scr, perm_t_scr = (r.at[hh] for r in (x_all, hs_all, rts_all, acc_all, perm_t_all))
        meta = meta_all.at[hh]
        mix = functools.reduce(lambda a, b: a + b,
                               [_dot(a_ref[0].astype(BF16), wo_ref[...]) for a_ref, wo_ref in zip(a_refs, wo_refs)])
        x = x_ref[0] + mgate_ref[0] * mix
        x_scr[...] = x
        h = _modulated_norm(x, g_ref[...], sh_ref[0], sc_ref[0])
        h_hi = h.astype(BF16)
        h_lo = (h - h_hi.astype(F32)).astype(BF16)
        both = _dot(h_hi, wr_ref[...])
        logits = (both[:, :LANES] + both[:, LANES:] + _dot(h_lo, wr_ref[:, :LANES]) + br_ref[...]).T
        gl = [logits[NG * PG + g:NG * PG + g + 1, :] for g in range(NG)]
        gmax = functools.reduce(jnp.maximum, gl)
        gtop = jnp.full_like(gmax, float(NG - 1))
        for g in reversed(range(NG - 1)):
            gtop = jnp.where(gl[g] == gmax, float(g), gtop)
        p_g = 1.0 / functools.reduce(lambda a, b: a + b, [jnp.exp(v - gmax) for v in gl])
        a = []
        for j in range(PG):
            v = logits[(NG - 1) * PG + j:(NG - 1) * PG + j + 1, :]
            for g in reversed(range(NG - 1)):
                v = jnp.where(gtop == float(g), logits[g * PG + j:g * PG + j + 1, :], v)
            a.append(v)

        def first_max(vals):
            vmax = functools.reduce(jnp.maximum, vals)
            taken = jnp.zeros_like(vmax) > 1.0
            hits = []
            for v in vals:
                hit = (v == vmax) & jnp.logical_not(taken)
                taken = taken | hit
                hits.append(hit)
            return vmax, hits

        v1, hit1 = first_max(a)
        rest = [jnp.where(hh, -jnp.inf, v) for hh, v in zip(hit1, a)]
        v2, hit2 = first_max(rest)
        e2 = jnp.exp(v2 - v1)
        w1 = p_g / (1.0 + e2)
        w2 = p_g * e2 / (1.0 + e2)
        tm = gtop.shape[1]
        row = lax.broadcasted_iota(jnp.int32, (SUBLANES, tm), 0)
        onehot = [jnp.where(gtop == float(g), 1.0, 0.0) for g in range(NG)]
        oh8 = jnp.zeros((SUBLANES, tm), F32)
        for g in range(NG):
            oh8 = jnp.where(row == g, onehot[g], oh8)
        before = _dot(oh8.astype(BF16), before_ref[...])
        pos = jnp.zeros_like(gtop)
        off = jnp.int32(0)
        for g in range(NG):
            cnt = jnp.sum(onehot[g]).astype(jnp.int32)
            meta[g] = off
            meta[NG + g] = cnt
            pos = pos + onehot[g] * (before[g:g + 1, :] + off.astype(F32))
            off = off + cnt
        rt = jnp.where(row == PG, gtop, jnp.where(row == PG + 1, pos, 0.0))
        for j in range(PG):
            wj = jnp.where(hit1[j], w1, jnp.where(hit2[j], w2, 0.0))
            rt = jnp.where(row == j, wj, rt)
        rt_tok = jnp.concatenate([rt, jnp.zeros((LANES - SUBLANES, tm), F32)], axis=0).T
        rid = lax.broadcasted_iota(jnp.int32, (tm, tm), 0).astype(F32)
        cid = lax.broadcasted_iota(jnp.int32, (tm, tm), 1).astype(F32)
        perm = jnp.where(rid == pos, 1.0, 0.0).astype(BF16)
        perm_t = jnp.where(rt_tok[:, PG + 1:PG + 2] == cid, 1.0, 0.0).astype(BF16)
        perm_t_scr[...] = perm_t
        pad = hs_scr.shape[0] - tm
        hs_scr[0:tm, :] = _dot(perm, h_hi).astype(BF16)
        hs_scr[tm:, :] = jnp.zeros((pad, h_hi.shape[1]), BF16)
        r1 = rt.astype(BF16)
        res = rt - r1.astype(F32)
        r2 = res.astype(BF16)
        r3 = (res - r2.astype(F32)).astype(BF16)
        rt_sorted = _dot(r1, perm_t) + _dot(r2, perm_t) + _dot(r3, perm_t)
        rts_scr[0:tm, :] = jnp.concatenate([rt_sorted, jnp.zeros((LANES - SUBLANES, tm), F32)], axis=0).T
        rts_scr[tm:, :] = jnp.full((pad, LANES), -1.0, F32)
        acc_scr[...] = jnp.zeros(acc_scr.shape, F32)

    tm = x_all.shape[1]
    WIN = hs_all.shape[1] - tm

    def experts(c):
        cf = c.astype(F32)
        bases, counts = [], []
        for hh in range(TP):
            off = meta_all[hh, c]
            cnt = meta_all[hh, NG + c]
            base = (off // MOE_ALIGN) * MOE_ALIGN
            bases.append(base)
            counts.append(jnp.where(cnt > 0, (off + cnt - base + WIN - 1) // WIN, 0))

        def win_body(w, carry):
            starts = [pl.multiple_of(jnp.where(w < counts[hh], bases[hh] + w * WIN, tm), MOE_ALIGN) for hh in range(TP)]
            hs = jnp.concatenate([hs_all[hh, pl.ds(starts[hh], WIN), :] for hh in range(TP)], axis=0)
            rt = jnp.concatenate([rts_all[hh, pl.ds(starts[hh], WIN), :] for hh in range(TP)], axis=0)
            in_group = rt[:, PG:PG + 1] == cf
            hid = _silu(_dot(hs, wg_ref[0])) * _dot(hs, wu_ref[0])
            parts = [hid[:, j * FH:(j + 1) * FH] * jnp.where(in_group, rt[:, j:j + 1], 0.0) for j in range(PG)]
            out = _dot(jnp.concatenate(parts, axis=1).astype(BF16), wd_ref[0])
            for hh in range(TP):
                acc_all[hh, pl.ds(starts[hh], WIN), :] += out[hh * WIN:(hh + 1) * WIN]
            return carry

        lax.fori_loop(0, functools.reduce(jnp.maximum, counts), win_body, 0)

    def epilogue(hh):
        ys = acc_all[hh, 0:tm, :]
        ys_hi = ys.astype(BF16)
        ys_lo = (ys - ys_hi.astype(F32)).astype(BF16)
        back = _dot(perm_t_all[hh], jnp.concatenate([ys_hi, ys_lo], axis=1))
        d = ys.shape[1]
        y = x_all[hh] + gate_ref[0] * (back[:, :d] + back[:, d:])
        if final:
            y = y * lax.rsqrt(jnp.mean(y * y, axis=-1, keepdims=True) + EPS) * fg_ref[...]
        o_ref[0] = y

    for hh in range(TP):
        pl.when(s == hh)(functools.partial(prologue, hh))
    pl.when((s >= TP - 1) & (s <= TP + NG - 2))(lambda: experts(s - (TP - 1)))
    for hh in range(TP):
        pl.when(s == NG + TP - 2 + hh)(functools.partial(epilogue, hh))


def _moe(x, mix_gate, acts, w_outs, g, shift, scale, gate, wg, bg, we, be, w_gate, w_up, w_down, final_g, final,
         tm=512):
    B, S, D = x.shape
    n_in = len(acts)
    NG, PG, FH = MOE_GROUPS, MOE_PER_GROUP, MOE_HIDDEN
    wr = jnp.zeros((D, LANES), F32)
    wr = wr.at[:, :NG * PG].set(we.reshape(D, NG * PG).astype(F32)).at[:, NG * PG:NG * PG + NG].set(wg.astype(F32))
    br = jnp.zeros((1, LANES), F32)
    br = br.at[0, :NG * PG].set(be.reshape(NG * PG).astype(F32)).at[0, NG * PG:NG * PG + NG].set(bg.astype(F32))
    wr_hi = wr.astype(BF16)
    wr = jnp.concatenate([wr_hi, (wr - wr_hi.astype(F32)).astype(BF16)], axis=1)
    grp = lambda w: w.reshape(NG, PG, D, FH).transpose(0, 2, 1, 3).reshape(NG, D, PG * FH).astype(BF16)
    wd = w_down.reshape(NG, PG * FH, D).astype(BF16)
    ids = jnp.arange(tm)
    before = (ids[:, None] < ids[None, :]).astype(BF16)
    TP = MOE_TILES
    n_steps = NG + 2 * TP - 2
    vec = pl.BlockSpec((1, 1, D), lambda b, i, s: (b, 0, 0))
    row = pl.BlockSpec((1, D), lambda b, i, s: (0, 0))
    wgroup = lambda s: jnp.where(s > TP + NG - 2, 0, jnp.clip(s - (TP - 1), 0, NG - 1))
    wspec = lambda k, n: pl.BlockSpec((1, k, n), lambda b, i, s: (wgroup(s), 0, 0))
    tokens_in = lambda n: pl.BlockSpec((1, tm, n), lambda b, i, s: (b, i * TP + jnp.minimum(s, TP - 1), 0))
    tokens_out = pl.BlockSpec((1, tm, D), lambda b, i, s: (b, i * TP + jnp.clip(s - (NG + TP - 2), 0, TP - 1), 0))
    const = lambda a: pl.BlockSpec(a.shape, lambda b, i, s: (0,) * a.ndim)
    return pl.pallas_call(
        functools.partial(_moe_kernel, n_in=n_in, final=final),
        out_shape=jax.ShapeDtypeStruct((B, S, D), F32),
        grid=(B, S // (tm * TP), n_steps),
        in_specs=[tokens_in(D), vec] + [tokens_in(a.shape[2]) for a in acts] + [const(w) for w in w_outs]
                 + [row, vec, vec, vec, const(wr), const(br), const(before),
                    wspec(D, PG * FH), wspec(D, PG * FH), wspec(PG * FH, D), row],
        out_specs=tokens_out,
        scratch_shapes=[pltpu.VMEM((TP, tm, D), F32), pltpu.VMEM((TP, tm + MOE_WIN, D), BF16),
                        pltpu.VMEM((TP, tm + MOE_WIN, LANES), F32), pltpu.VMEM((TP, tm + MOE_WIN, D), F32),
                        pltpu.VMEM((TP, tm, tm), BF16), pltpu.SMEM((TP, 2 * NG), jnp.int32)],
        compiler_params=_cparams(("parallel", "parallel", "arbitrary")),
        name="moe",
    )(x, mix_gate, *acts, *w_outs, g.reshape(1, D), shift, scale, gate, wr, br, before, grp(w_gate), grp(w_up), wd,
      final_g.reshape(1, D))


def _mlstm_s5_layer(x, g, shift, scale, w_in, conv_w, b_i, b_f, head_g, s5_params, w_out):
    H = MLSTM_HEADS
    A = MIX_A
    w_if = jnp.zeros((D_MODEL, LANES), F32).at[:, :2 * H].set(w_in[:, 4 * A:4 * A + 2 * H])
    weights = [w_in[:, :2 * A], w_in[:, 2 * A:4 * A], w_if, w_in[:, 4 * A + 2 * H:]]
    qk, vo, ifg, u = _norm_matmul(x, g, shift, scale, [w.astype(BF16) for w in weights], [BF16, BF16, F32, F32])
    gate_bias = jnp.zeros((1, LANES), F32).at[0, :H].set(b_i.astype(F32)).at[0, H:2 * H].set(b_f.astype(F32))
    hm = _mlstm(qk, vo, ifg, conv_w.astype(F32), gate_bias, head_g.reshape(1, A).astype(F32))
    ys = _s5s(u, _s5s_tables(*s5_params))
    w_out = w_out.astype(BF16)
    return [hm, ys], [w_out[:A], w_out[A:]]


def _nsa_layer(x, g, shift, scale, w_in, b_gate, cmp_pos, cmp_w1, cmp_b1, cmp_w2, cmp_b2, rel_bias, w_out):
    B, S, D = x.shape
    KV, R, DH = NSA_KV, NSA_R, NSA_DH
    w_g = jnp.zeros((D, KV, LANES), F32).at[:, :, :3 * R].set(w_in[:, D + 6 * KV_W:].reshape(D, KV, 3 * R))
    b_g = jnp.zeros((KV, LANES), F32).at[:, :3 * R].set(b_gate.reshape(KV, 3 * R).astype(F32))
    kv_cols = lambda i: w_in[:, D + i * KV_W:D + (i + 1) * KV_W]
    w_k = jnp.concatenate([kv_cols(0), kv_cols(2), kv_cols(4)], axis=1)
    w_v = jnp.concatenate([kv_cols(1), kv_cols(3), kv_cols(5)], axis=1)
    weights = [w_in[:, :D], w_k, w_v, w_g.reshape(D, KV * LANES)]
    q4, gv, kc, vc, ks, kw, vs_t, vw_t = _nsa_proj(x, g, shift, scale, [w.astype(BF16) for w in weights],
                                                   b_g.reshape(1, KV * LANES))
    grp = CMP_STRIDE
    xg = jnp.stack([kc, vc]).reshape(2, B, KV * S // grp, grp * DH)
    cmp = _compress(xg, cmp_pos, cmp_w1, cmp_b1, cmp_w2, cmp_b2).reshape(2, B, KV, S // grp, DH).astype(BF16)
    out = _nsa_t_attention(q4, gv, cmp[0], cmp[1].transpose(0, 1, 3, 2), ks, vs_t, kw, vw_t,
                           _nsa_t_tables(rel_bias, S))
    return [out], [w_out.astype(BF16)]


def kernel(x, c, rel_bias, ada_w, ada_b, norm_g, final_g,
           a_w_in, a_conv, a_b_i, a_b_f, a_head_g,
           s5_lam_re, s5_lam_im, s5_log_dt, s5_b_re, s5_b_im, s5_c_re, s5_c_im,
           s5_d, s5_glu_w, s5_glu_b, a_w_out,
           n_w_in, n_b_gate, n_cmp_pos, n_cmp_w1, n_cmp_b1, n_cmp_w2, n_cmp_b2, n_w_out,
           r_grp_w, r_grp_b, r_exp_w, r_exp_b, e_w_gate, e_w_up, e_w_down):
    B, S, D = x.shape
    mod = _ada_mod(c, ada_w, ada_b).reshape(DEPTH, 2, B, 1, 3 * D)
    split = lambda m: (m[..., :D], m[..., D:2 * D], m[..., 2 * D:])
    for layer in range(DEPTH):
        shift, scale, mix_gate = split(mod[layer, 0])
        j = layer // 2
        if layer % 2 == 0:
            s5_params = (s5_lam_re[j], s5_lam_im[j], s5_log_dt[j], s5_b_re[j], s5_b_im[j],
                         s5_c_re[j], s5_c_im[j], s5_d[j], s5_glu_w[j], s5_glu_b[j])
            acts, w_outs = _mlstm_s5_layer(x, norm_g[layer, 0], shift, scale, a_w_in[j], a_conv[j], a_b_i[j],
                                           a_b_f[j], a_head_g[j], s5_params, a_w_out[j])
        else:
            acts, w_outs = _nsa_layer(x, norm_g[layer, 0], shift, scale, n_w_in[j], n_b_gate[j], n_cmp_pos[j],
                                      n_cmp_w1[j], n_cmp_b1[j], n_cmp_w2[j], n_cmp_b2[j], rel_bias, n_w_out[j])
        shift, scale, gate = split(mod[layer, 1])
        x = _moe(x, mix_gate, acts, w_outs, norm_g[layer, 1], shift, scale, gate, r_grp_w[layer], r_grp_b[layer],
                 r_exp_w[layer], r_exp_b[layer], e_w_gate[layer], e_w_up[layer], e_w_down[layer], final_g,
                 final=(layer == DEPTH - 1))
    return x
```

```python
import functools
import math

import jax
import jax.numpy as jnp
from jax import lax
from jax.experimental import pallas as pl
from jax.experimental.pallas import tpu as pltpu

F32 = jnp.float32
BF16 = jnp.bfloat16
HIGHEST = lax.Precision.HIGHEST

D_MODEL = 1024
DEPTH = 2
MIX_A = 512
MLSTM_HEADS = 4
MLSTM_DH = MIX_A // MLSTM_HEADS
MLSTM_CHUNK = 128
CONV_K = 4
S5_GROUP = 16
S5_STATE = 64
S5_CHUNK = 16
NSA_HEADS = 16
NSA_KV = 4
NSA_R = NSA_HEADS // NSA_KV
NSA_DH = D_MODEL // NSA_HEADS
KV_W = NSA_KV * NSA_DH
CMP_BLOCK = 32
CMP_STRIDE = 16
CMP_HIDDEN = 256
SEL_BLOCK = 64
SEL_TOPK = 16
WINDOW = 512
FORCE = 1e9
REL_BUCKETS = 32
REL_MAX_DIST = 128
MOE_GROUPS = 4
MOE_PER_GROUP = 4
MOE_HIDDEN = 256
EPS = 1e-6
NEG = -1e30
BIG = 1e30
LOG2E = math.log2(math.e)

LANES = 128
SUBLANES = 8
ATT_TILE = 256
MOE_WIN = 160
MOE_ALIGN = 16
MOE_TILES = 2
VMEM_LIMIT = 56 * 1024 * 1024


def _cparams(sem):
    return pltpu.CompilerParams(dimension_semantics=sem, vmem_limit_bytes=VMEM_LIMIT)


def _dot(a, b, precision=None):
    return jnp.dot(a, b, preferred_element_type=F32, precision=precision)


def _dot_nt(a, b):
    return lax.dot_general(a, b, (((1,), (1,)), ((), ())), preferred_element_type=F32)


def _sigmoid(x):
    return 1.0 / (1.0 + jnp.exp(-x))


def _silu(x):
    return x * _sigmoid(x)


def _gelu_tanh(x):
    return 0.5 * x * (1.0 + jnp.tanh(math.sqrt(2.0 / math.pi) * (x + 0.044715 * (x * x * x))))


def _modulated_norm(x, g, shift, scale):
    y = x * lax.rsqrt(jnp.mean(x * x, axis=-1, keepdims=True) + EPS) * g
    return y * (1.0 + scale) + shift


def _ada_kernel(c_ref, w_ref, b_ref, o_ref):
    c = c_ref[...]
    o_ref[0] = _dot(_silu(c), w_ref[0]) + b_ref[0]


def _ada_mod(c, ada_w, ada_b):
    B, D = c.shape
    n_mod = ada_w.shape[0] * ada_w.shape[1]
    w = ada_w.reshape(n_mod, D, 3 * D)
    b = ada_b.reshape(n_mod, 1, 3 * D)
    tn = 1024
    return pl.pallas_call(
        _ada_kernel,
        out_shape=jax.ShapeDtypeStruct((n_mod, B, 3 * D), F32),
        grid=(n_mod, 3 * D // tn),
        in_specs=[pl.BlockSpec((B, D), lambda i, j: (0, 0)),
                  pl.BlockSpec((1, D, tn), lambda i, j: (i, 0, j)),
                  pl.BlockSpec((1, 1, tn), lambda i, j: (i, 0, j))],
        out_specs=pl.BlockSpec((1, B, tn), lambda i, j: (i, 0, j)),
        compiler_params=_cparams(("parallel", "parallel")),
        name="ada_mod",
    )(c, w, b)


def _norm_mm_kernel(*refs, n_w):
    x_ref, g_ref, sh_ref, sc_ref = refs[:4]
    w_refs = refs[4:4 + n_w]
    o_refs = refs[4 + n_w:]
    h = _modulated_norm(x_ref[0], g_ref[...], sh_ref[0], sc_ref[0]).astype(BF16)
    for w_ref, o_ref in zip(w_refs, o_refs):
        o_ref[0] = _dot(h, w_ref[...]).astype(o_ref.dtype)


def _norm_matmul(x, g, shift, scale, weights, out_dtypes, tm=512):
    B, S, D = x.shape
    n_w = len(weights)
    vec = pl.BlockSpec((1, 1, D), lambda b, i: (b, 0, 0))
    in_specs = [pl.BlockSpec((1, tm, D), lambda b, i: (b, i, 0)),
                pl.BlockSpec((1, D), lambda b, i: (0, 0)), vec, vec]
    in_specs += [pl.BlockSpec(w.shape, lambda b, i: (0, 0)) for w in weights]
    return pl.pallas_call(
        functools.partial(_norm_mm_kernel, n_w=n_w),
        out_shape=[jax.ShapeDtypeStruct((B, S, w.shape[1]), dt) for w, dt in zip(weights, out_dtypes)],
        grid=(B, S // tm),
        in_specs=in_specs,
        out_specs=[pl.BlockSpec((1, tm, w.shape[1]), lambda b, i: (b, i, 0)) for w in weights],
        compiler_params=_cparams(("parallel", "parallel")),
        name="norm_matmul",
    )(x, g.reshape(1, D), shift, scale, *weights)


def _mlstm_kernel(qk_ref, vo_ref, if_ref, cw_ref, gb_ref, hg_ref, tril_ref, o_ref,
                  xbuf, c_scr, n_scr, m_scr):
    pad = SUBLANES

    @pl.when(pl.program_id(1) == 0)
    def _():
        xbuf[:, 0:pad, :] = jnp.zeros((xbuf.shape[0], pad, 2 * MIX_A), F32)
        c_scr[...] = jnp.zeros_like(c_scr)
        n_scr[...] = jnp.zeros_like(n_scr)
        m_scr[...] = jnp.zeros_like(m_scr)

    for bb in range(qk_ref.shape[0]):
        _mlstm_chunk(qk_ref.at[bb], vo_ref.at[bb], if_ref.at[bb], cw_ref, gb_ref, hg_ref, tril_ref, o_ref.at[bb],
                     xbuf.at[bb], c_scr.at[bb], n_scr.at[bb], m_scr.at[bb])


def _mlstm_chunk(qk_ref, vo_ref, if_ref, cw_ref, gb_ref, hg_ref, tril_ref, o_ref, xbuf, c_scr, n_scr, m_scr):
    L, H, DH = MLSTM_CHUNK, MLSTM_HEADS, MLSTM_DH
    pad = SUBLANES
    xbuf[pad:pad + L, :] = qk_ref[...].astype(F32)
    cw = cw_ref[...]
    conv = None
    for j in range(CONV_K):
        lo = pad - (CONV_K - 1) + j
        t = xbuf[lo:lo + L, :] * cw[j:j + 1, :]
        conv = t if conv is None else conv + t
    xbuf[0:pad, :] = xbuf[L:L + pad, :]
    qk = _silu(conv)
    q = qk[:, :MIX_A]
    k = qk[:, MIX_A:] * (DH ** -0.5)
    vo = vo_ref[...].astype(F32)
    v = vo[:, :MIX_A]
    o_pre = vo[:, MIX_A:]

    ifb = if_ref[...] + gb_ref[...]
    lf = jnp.minimum(ifb, 0.0) - jnp.log1p(jnp.exp(-jnp.abs(ifb)))
    bcs = _dot(tril_ref[...], lf, precision=HIGHEST)
    ifb_t = ifb.T
    bcs_t = bcs.T
    row = lax.broadcasted_iota(jnp.int32, (L, L), 0)
    col = lax.broadcasted_iota(jnp.int32, (L, L), 1)
    causal = col <= row

    outs = []
    for h in range(H):
        sl = slice(h * DH, (h + 1) * DH)
        qh, kh, vh = q[:, sl], k[:, sl], v[:, sl]
        qb, kb = qh.astype(BF16), kh.astype(BF16)
        b_col = bcs[:, H + h:H + h + 1]
        b_row = bcs_t[H + h:H + h + 1, :]
        li_col = ifb[:, h:h + 1]
        li_row = ifb_t[h:h + 1, :]
        b_last = b_col[L - 1:L, :]
        m0 = m_scr[h][:, 0:1]
        c0 = c_scr[h]
        n0 = n_scr[h]

        log_d = jnp.where(causal, b_col - b_row + li_row, NEG)
        log_inter = b_col + m0
        m_t = jnp.maximum(log_inter, jnp.max(log_d, axis=1, keepdims=True))
        dmat = jnp.exp(log_d - m_t)
        a_inter = jnp.exp(log_inter - m_t)
        s = _dot_nt(qb, kb) * dmat
        num = _dot(s.astype(BF16), vh.astype(BF16)) + a_inter * _dot_nt(qb, c0.astype(BF16))
        den = jnp.sum(s, axis=1, keepdims=True) + a_inter * jnp.sum(qh * n0, axis=1, keepdims=True)
        hh = num / jnp.maximum(jnp.abs(den), jnp.exp(-m_t))

        w_col = b_last - b_col + li_col
        m_loc = jnp.max(w_col, axis=0, keepdims=True)
        e = jnp.exp(w_col - m_loc)
        c_loc = _dot((vh * e).T.astype(BF16), kb)
        n_loc = jnp.sum(kh * e, axis=0, keepdims=True)
        m_new = jnp.maximum(b_last + m0, m_loc)
        a = jnp.exp(b_last + m0 - m_new)
        sc = jnp.exp(m_loc - m_new)
        c_scr[h] = a * c0 + sc * c_loc
        n_scr[h] = a * n0 + sc * n_loc
        m_scr[h] = jnp.broadcast_to(m_new, (1, LANES))

        outs.append(hh * lax.rsqrt(jnp.mean(hh * hh, axis=1, keepdims=True) + EPS))
    hm = jnp.concatenate(outs, axis=1)
    o_ref[...] = (_sigmoid(o_pre) * (hm * hg_ref[...])).astype(o_ref.dtype)


def _mlstm(qk, vo, ifg, conv_w, gate_bias, head_g):
    B, S, _ = qk.shape
    rows = 1
    L, H, DH = MLSTM_CHUNK, MLSTM_HEADS, MLSTM_DH
    tril = jnp.tril(jnp.ones((L, L), F32))
    return pl.pallas_call(
        _mlstm_kernel,
        out_shape=jax.ShapeDtypeStruct((B, S, MIX_A), BF16),
        grid=(B // rows, S // L),
        in_specs=[pl.BlockSpec((rows, L, 2 * MIX_A), lambda b, c: (b, c, 0)),
                  pl.BlockSpec((rows, L, 2 * MIX_A), lambda b, c: (b, c, 0)),
                  pl.BlockSpec((rows, L, LANES), lambda b, c: (b, c, 0)),
                  pl.BlockSpec((CONV_K, 2 * MIX_A), lambda b, c: (0, 0)),
                  pl.BlockSpec((1, LANES), lambda b, c: (0, 0)),
                  pl.BlockSpec((1, MIX_A), lambda b, c: (0, 0)),
                  pl.BlockSpec((L, L), lambda b, c: (0, 0))],
        out_specs=pl.BlockSpec((rows, L, MIX_A), lambda b, c: (b, c, 0)),
        scratch_shapes=[pltpu.VMEM((rows, L + SUBLANES, 2 * MIX_A), F32),
                        pltpu.VMEM((rows, H, DH, DH), F32),
                        pltpu.VMEM((rows, H, 1, DH), F32),
                        pltpu.VMEM((rows, H, 1, LANES), F32)],
        compiler_params=_cparams(("parallel", "arbitrary")),
        name="mlstm",
    )(qk, vo, ifg, conv_w, gate_bias, head_g, tril)


S5_LT = LANES // S5_GROUP
S5_PAIRS = S5_CHUNK // 2


def _s5s_kernel(u_ref, h_ref, e_ref, kk_ref, are_ref, aim_ref, d_ref, gw_ref, gb_ref, o_ref, xl_scr, x0_scr):
    n_chunks = u_ref.shape[1] // S5_CHUNK
    half = S5_LT * S5_STATE
    tok = lambda s: u_ref[0, pl.ds(s, n_chunks, stride=S5_CHUNK), :]
    u2 = [jnp.concatenate([tok(2 * q), tok(2 * q + 1)], axis=1) for q in range(S5_PAIRS)]
    u2b = [v.astype(BF16) for v in u2]
    xl_scr[...] = functools.reduce(lambda a, b: a + b, [_dot(u2b[q], h_ref[0, q]) for q in range(S5_PAIRS)])
    a_re = are_ref[0]
    a_im = aim_ref[0]

    def body(a, carry):
        re, im = carry
        x0_scr[pl.ds(a, 1), 0:half] = re
        x0_scr[pl.ds(a, 1), half:2 * half] = im
        return (a_re * re - a_im * im + xl_scr[pl.ds(a, 1), 0:half],
                a_re * im + a_im * re + xl_scr[pl.ds(a, 1), half:2 * half])

    zero = jnp.zeros((1, half), F32)
    lax.fori_loop(0, n_chunks, body, (zero, zero), unroll=8)
    x0 = x0_scr[...].astype(BF16)
    for p in range(S5_PAIRS):
        y = _dot(x0, e_ref[0, p]) + u2[p] * d_ref[0]
        for q in range(p + 1):
            y = y + _dot(u2b[q], kk_ref[0, p - q])
        ys = _gelu_tanh(y)
        out = ys * _sigmoid(_dot(ys.astype(BF16), gw_ref[0]) + gb_ref[0])
        o_ref[0, pl.ds(2 * p, n_chunks, stride=S5_CHUNK), :] = out[:, :LANES].astype(o_ref.dtype)
        o_ref[0, pl.ds(2 * p + 1, n_chunks, stride=S5_CHUNK), :] = out[:, LANES:].astype(o_ref.dtype)


def _s5s_tables(lam_re, lam_im, log_dt, b_re, b_im, c_re, c_im, d_skip, glu_w, glu_b):
    T, C, P, LT = S5_CHUNK, S5_GROUP, S5_STATE, S5_LT
    G = lam_re.shape[0]
    NT = G // LT
    lam = lax.complex(lam_re.astype(F32), lam_im.astype(F32))
    dt = jnp.exp(log_dt.astype(F32))[:, None]
    lam_bar = jnp.exp(lam * dt)
    b_bar = ((lam_bar - 1.0) / lam)[..., None] * lax.complex(b_re.astype(F32), b_im.astype(F32))
    c_mat = lax.complex(c_re.astype(F32), c_im.astype(F32))
    taus = jnp.arange(T + 1, dtype=F32)
    pw = jnp.exp((lam * dt)[:, None, :] * taus[None, :, None])
    eye = jnp.eye(LT, dtype=F32)
    tiles = lambda a: a.reshape((NT, LT) + a.shape[1:])

    kern = jnp.einsum('gcp,gtp,gpd->gtdc', c_mat, pw[:, :T], b_bar, precision=HIGHEST).real
    kblk = jnp.einsum('nitdc,ij->ntidjc', tiles(kern), eye).reshape(NT, T, LANES, LANES)
    kblk = jnp.concatenate([jnp.zeros_like(kblk[:, :1]), kblk], axis=1)
    kk = jnp.stack([jnp.concatenate([jnp.concatenate([kblk[:, 2 * d + 1], kblk[:, 2 * d + 2]], axis=2),
                                     jnp.concatenate([kblk[:, 2 * d], kblk[:, 2 * d + 1]], axis=2)], axis=1)
                    for d in range(T // 2)], axis=1)

    hmat = pw[:, :T][:, ::-1, :, None] * b_bar[:, None]

    def state_cols(m):
        return jnp.einsum('nispc,ij->nsicjp', tiles(m), eye).reshape(NT, T, LANES, LT * P)

    h = jnp.concatenate([state_cols(hmat.real), state_cols(hmat.imag)], axis=3)
    h2 = h.reshape(NT, T // 2, 2 * LANES, 2 * LT * P)

    emat = c_mat[:, None] * pw[:, 1:][:, :, None, :]

    def state_rows(m):
        return jnp.einsum('nitcp,ij->ntjpic', tiles(m), eye).reshape(NT, T, LT * P, LANES)

    e = jnp.concatenate([state_rows(emat.real), state_rows(-emat.imag)], axis=2)
    e2 = e.reshape(NT, T // 2, 2, 2 * LT * P, LANES).transpose(0, 1, 3, 2, 4).reshape(NT, T // 2, 2 * LT * P, 2 * LANES)

    a_re = pw[:, T].real.reshape(NT, 1, LT * P)
    a_im = pw[:, T].imag.reshape(NT, 1, LT * P)
    pair = lambda v: jnp.tile(v.astype(F32).reshape(NT, 1, LANES), (1, 1, 2))
    gwb = jnp.einsum('nice,ij->nicje', tiles(glu_w.astype(F32)), eye).reshape(NT, LANES, LANES)
    zeros = jnp.zeros_like(gwb)
    gw2 = jnp.concatenate([jnp.concatenate([gwb, zeros], axis=2), jnp.concatenate([zeros, gwb], axis=2)], axis=1)
    return (h2.astype(BF16), e2.astype(BF16), kk.astype(BF16), a_re, a_im, pair(d_skip), gw2.astype(BF16), pair(glu_b))


def _s5s(u, tables):
    B, S, W = u.shape
    NT = W // LANES
    n_chunks = S // S5_CHUNK
    per_tile = lambda a: pl.BlockSpec((1,) + a.shape[1:], lambda j, b: (j,) + (0,) * (a.ndim - 1))
    return pl.pallas_call(
        _s5s_kernel,
        out_shape=jax.ShapeDtypeStruct((B, S, W), F32),
        grid=(NT, B),
        in_specs=[pl.BlockSpec((1, S, LANES), lambda j, b: (b, 0, j))] + [per_tile(t) for t in tables],
        out_specs=pl.BlockSpec((1, S, LANES), lambda j, b: (b, 0, j)),
        scratch_shapes=[pltpu.VMEM((n_chunks, 2 * S5_LT * S5_STATE), F32) for _ in range(2)],
        compiler_params=_cparams(("parallel", "parallel")),
        name="s5",
    )(u, *tables)


def _compress_kernel(x_ref, plo_ref, phi_ref, w1_ref, b1_ref, w2_ref, b2_ref, o_ref):
    x = x_ref[0, 0]
    half = x.shape[1]
    w1 = w1_ref[0]
    lo = _dot((x + plo_ref[0]).astype(BF16), w1[:half])
    hi = _dot((x + phi_ref[0]).astype(BF16), w1[half:])
    rows = x.shape[0]
    hid = _gelu_tanh(lo + pltpu.roll(hi, rows - 1, 0) + b1_ref[0])
    o_ref[0, 0] = _dot(hid.astype(BF16), w2_ref[0]) + b2_ref[0]


def _compress(xg, pos, w1, b1, w2, b2):
    _, B, rows, width = xg.shape
    pos_flat = pos.reshape(2, 2, 1, width).astype(F32)
    sel = lambda shape: pl.BlockSpec((1,) + shape, lambda j, b: (j, 0, 0))
    return pl.pallas_call(
        _compress_kernel,
        out_shape=jax.ShapeDtypeStruct((2, B, rows, NSA_DH), F32),
        grid=(2, B),
        in_specs=[pl.BlockSpec((1, 1, rows, width), lambda j, b: (j, b, 0, 0)),
                  sel((1, width)), sel((1, width)),
                  sel((2 * width, CMP_HIDDEN)), sel((1, CMP_HIDDEN)),
                  sel((CMP_HIDDEN, NSA_DH)), sel((1, NSA_DH))],
        out_specs=pl.BlockSpec((1, 1, rows, NSA_DH), lambda j, b: (j, b, 0, 0)),
        compiler_params=_cparams(("parallel", "parallel")),
        name="nsa_compress",
    )(xg, pos_flat[:, 0], pos_flat[:, 1], w1.astype(BF16), b1[:, None].astype(F32),
      w2.astype(BF16), b2[:, None].astype(F32))


def _t5_bucket(dist):
    dist = jnp.maximum(dist, 0)
    max_exact = REL_BUCKETS // 2
    log_ratio = jnp.log(jnp.maximum(dist, 1).astype(F32) / max_exact) / math.log(REL_MAX_DIST / max_exact)
    large = jnp.minimum(max_exact + (log_ratio * (REL_BUCKETS - max_exact)).astype(jnp.int32), REL_BUCKETS - 1)
    return jnp.where(dist < max_exact, dist, large)


def _nsa_proj_kernel(x_ref, g_ref, sh_ref, sc_ref, wq_ref, wk_ref, wv_ref, wg_ref, bg_ref,
                     q4_ref, gv_ref, kc_ref, vc_ref, ks_ref, kw_ref, vst_ref, vwt_ref):
    KV, R, DH, T = NSA_KV, NSA_R, NSA_DH, ATT_TILE
    h = _modulated_norm(x_ref[0], g_ref[...], sh_ref[0], sc_ref[0]).astype(BF16)
    q_t = (_dot(h, wq_ref[...]) * (DH ** -0.5 * LOG2E)).T.astype(BF16)
    gates_t = _sigmoid(_dot(h, wg_ref[...]) + bg_ref[...]).T
    row = lax.broadcasted_iota(jnp.int32, (SUBLANES, R * T), 0)
    for g in range(KV):
        q4_ref[0, g, 0] = jnp.concatenate([q_t[(g * R + r) * DH:(g * R + r + 1) * DH] for r in range(R)], axis=1)
        gv = jnp.zeros((SUBLANES, R * T), F32)
        for j in range(3):
            gj = jnp.concatenate([gates_t[g * LANES + 3 * r + j:g * LANES + 3 * r + j + 1] for r in range(R)], axis=1)
            gv = jnp.where(row == j, gj, gv)
        gv_ref[0, g, 0] = gv
    k3 = _dot(h, wk_ref[...])
    v3 = _dot(h, wv_ref[...])
    vs_t = v3[:, KV_W:2 * KV_W].T.astype(BF16)
    vw_t = v3[:, 2 * KV_W:].T.astype(BF16)
    for g in range(KV):
        cols = slice(g * DH, (g + 1) * DH)
        kc_ref[0, g] = k3[:, cols].astype(BF16)
        vc_ref[0, g] = v3[:, cols].astype(BF16)
        ks_ref[0, g] = k3[:, KV_W + g * DH:KV_W + (g + 1) * DH].astype(BF16)
        kw_ref[0, g] = k3[:, 2 * KV_W + g * DH:2 * KV_W + (g + 1) * DH].astype(BF16)
        vst_ref[0, g, 0] = vs_t[cols]
        vwt_ref[0, g, 0] = vw_t[cols]


def _nsa_proj(x, g, shift, scale, weights, b_gate):
    B, S, D = x.shape
    KV, R, DH, T = NSA_KV, NSA_R, NSA_DH, ATT_TILE
    vec = pl.BlockSpec((1, 1, D), lambda b, i: (b, 0, 0))
    keys = pl.BlockSpec((1, KV, T, DH), lambda b, i: (b, 0, i, 0))
    key_shape = jax.ShapeDtypeStruct((B, KV, S, DH), BF16)
    tile = lambda rows, width: pl.BlockSpec((1, KV, 1, rows, width), lambda b, i: (b, 0, i, 0, 0))
    tile_shape = lambda rows, width, dt: jax.ShapeDtypeStruct((B, KV, S // T, rows, width), dt)
    return pl.pallas_call(
        _nsa_proj_kernel,
        out_shape=[tile_shape(DH, R * T, BF16), tile_shape(SUBLANES, R * T, F32),
                   key_shape, key_shape, key_shape, key_shape,
                   tile_shape(DH, T, BF16), tile_shape(DH, T, BF16)],
        grid=(B, S // T),
        in_specs=[pl.BlockSpec((1, T, D), lambda b, i: (b, i, 0)),
                  pl.BlockSpec((1, D), lambda b, i: (0, 0)), vec, vec]
                 + [pl.BlockSpec(w.shape, lambda b, i: (0, 0)) for w in weights]
                 + [pl.BlockSpec(b_gate.shape, lambda b, i: (0, 0))],
        out_specs=[tile(DH, R * T), tile(SUBLANES, R * T), keys, keys, keys, keys, tile(DH, T), tile(DH, T)],
        compiler_params=_cparams(("parallel", "parallel")),
        name="nsa_proj",
    )(x, g.reshape(1, D), shift, scale, *weights, b_gate)


def _nsa_t_kernel(q4_ref, gv_ref, kc_ref, vct_ref, ks_ref, vst_ref, kw_ref, vwt_ref,
                  cfar_ref, band_ref, selb_ref, winb_ref, ovt_ref, o_ref, s_scr, sel_scr, sbuf):
    T = ATT_TILE
    R, DH = NSA_R, NSA_DH
    qi = pl.program_id(2)
    q0 = qi * T
    n_pad = kc_ref.shape[2]
    n_sel = ovt_ref.shape[0]
    n_far = selb_ref.shape[0] - 1
    n_win = winb_ref.shape[0] - 2
    band_rows = band_ref.shape[2] - T // CMP_STRIDE * 2

    q4 = q4_ref[0, 0, 0]
    t_lane = q0 + lax.broadcasted_iota(jnp.int32, (1, R * T), 1) % T

    ones_rows = DH
    with_ones = lambda v_t: jnp.concatenate([v_t, jnp.ones((ones_rows, v_t.shape[1]), v_t.dtype)], axis=0)
    gvec = lambda j: gv_ref[0, 0, 0, j:j + 1, :]

    grp = T // CMP_STRIDE
    s_scr[0:n_pad, :] = _dot(kc_ref[0, 0], q4) + cfar_ref[0]
    s_scr[n_pad:n_pad + 2 * grp, :] = jnp.zeros((2 * grp, R * T), F32)
    r0 = jnp.maximum(qi * grp - 2 * grp, 0)
    x0 = r0 - (qi * grp - 2 * grp)
    r0 = pl.multiple_of(r0, SUBLANES)
    x0 = pl.multiple_of(x0, SUBLANES)
    s_scr[pl.ds(r0, band_rows), :] += band_ref[0, 0, pl.ds(x0, band_rows), :]
    lim = pl.multiple_of(qi * grp + 2 * grp, SUBLANES)
    s_scr[pl.ds(lim, n_pad), :] = jnp.full((n_pad, R * T), NEG, F32)

    w_subs, w_vals = [], []
    for d in range(n_win + 1):
        kt = jnp.maximum(qi - d, 0)
        off = pl.multiple_of(kt * T, T)
        tile = jnp.where(qi >= d, d, n_win + 1)
        w_subs.append((_dot(kw_ref[0, 0, pl.ds(off, T), :], q4) + winb_ref[tile, 0]).astype(BF16))
        w_vals.append(with_ones(vwt_ref[0, 0, kt]))

    s = s_scr[0:n_pad, :]
    e = jnp.exp2(s - jnp.max(s, axis=0, keepdims=True))
    inv = jnp.where(t_lane >= CMP_BLOCK - 1, 1.0 / jnp.sum(e, axis=0, keepdims=True), 0.0)
    p = e * inv
    o_cmp = _dot(vct_ref[0, 0], p.astype(BF16))
    psum = functools.reduce(lambda a, b: a + b, [p[:, r * T:(r + 1) * T] for r in range(R)])

    m_w = jnp.max(functools.reduce(jnp.maximum, w_subs), axis=0, keepdims=True)
    acc = functools.reduce(lambda a, b: a + b,
                           [_dot(vj, jnp.exp2(sj - m_w)) for sj, vj in zip(w_subs, w_vals)])
    o_win = acc[:DH] * (1.0 / acc[DH:DH + 1])
    out_t = gvec(0) * o_cmp + gvec(2) * o_win

    imp_t = _dot(ovt_ref[...], psum, precision=HIGHEST)
    jj = lax.broadcasted_iota(jnp.int32, (n_sel, T), 0)
    blk_t = (q0 + lax.broadcasted_iota(jnp.int32, (1, T), 1)) // SEL_BLOCK
    forced = (jj == 0) | (jj == blk_t) | (jj == blk_t - 1)
    score = jnp.where(forced, FORCE, jnp.where(jj <= blk_t, imp_t, -1.0))
    n_blk = n_sel // SUBLANES
    rows = [score[v * SUBLANES:(v + 1) * SUBLANES] for v in range(n_blk)]
    cnts = [jnp.zeros((SUBLANES, T), F32) for _ in range(n_blk)]
    sub = lax.broadcasted_iota(jnp.int32, (SUBLANES, T), 0)
    for j2 in range(n_sel):
        c2 = score[j2:j2 + 1, :]
        for v in range(n_blk):
            lo = v * SUBLANES
            if lo > j2:
                beats = c2 >= rows[v]
            elif lo + SUBLANES - 1 <= j2:
                beats = c2 > rows[v]
            else:
                beats = (c2 > rows[v]) | ((c2 >= rows[v]) & (sub > j2 - lo))
            cnts[v] = cnts[v] + jnp.where(beats, 1.0, 0.0)
    cnt = jnp.concatenate(cnts, axis=0)
    chosen = (cnt < float(min(SEL_TOPK, n_sel))) & (jj <= blk_t)
    sel_scr[...] = jnp.where(chosen, 0.0, -BIG)

    def block_mask(kt):
        per_tile = T // SEL_BLOCK
        parts = [jnp.broadcast_to(sel_scr[pl.ds(kt * per_tile + i, 1), :], (SEL_BLOCK, T)) for i in range(per_tile)]
        m1 = jnp.concatenate(parts, axis=0)
        return jnp.concatenate([m1] * R, axis=1)

    def sel_scores(slot, kc):
        off = pl.multiple_of(kc * T, T)
        s = _dot(ks_ref[0, 0, pl.ds(off, T), :], q4)
        s = (s + selb_ref[jnp.clip(qi - kc, 0, n_far), 0] + block_mask(kc)).astype(BF16)
        sbuf[slot] = s
        return jnp.max(s, axis=0, keepdims=True).astype(F32)

    def sel_weighted(slot, kc, m_new):
        return _dot(with_ones(vst_ref[0, 0, kc]), jnp.exp2(sbuf[slot] - m_new.astype(BF16)))

    last_tile = vst_ref.shape[2] - 1

    def sel_body(i, carry):
        m, acc, m_even = carry
        m_odd = sel_scores(1, 2 * i + 1)
        m_new = jnp.maximum(m, m_even)
        acc = jnp.exp2(m - m_new) * acc + sel_weighted(0, 2 * i, m_new)
        m_even = sel_scores(0, jnp.minimum(2 * i + 2, last_tile))
        m_fin = jnp.maximum(m_new, m_odd)
        acc = jnp.exp2(m_new - m_fin) * acc + sel_weighted(1, 2 * i + 1, m_fin)
        return m_fin, acc, m_even

    _, acc, _ = lax.fori_loop(0, qi // 2 + 1, sel_body,
                              (jnp.full((1, R * T), NEG, F32), jnp.zeros((DH + ones_rows, R * T), F32),
                               sel_scores(0, 0)))
    out_t = out_t + gvec(1) * (acc[:DH] * (1.0 / acc[DH:DH + 1]))
    for pr in range(R // 2):
        pair = jnp.concatenate([out_t[:, (2 * pr) * T:(2 * pr + 1) * T],
                                out_t[:, (2 * pr + 1) * T:(2 * pr + 2) * T]], axis=0)
        o_ref[0, :, pr * 2 * DH:(pr + 1) * 2 * DH] = pair.T.astype(o_ref.dtype)


def _bias_lookup(table, dist):
    onehot = (_t5_bucket(dist)[..., None] == jnp.arange(table.shape[0])).astype(F32)
    return jnp.einsum('...k,kh->...h', onehot, table, precision=HIGHEST)


def _nsa_t_tables(rel_bias, S):
    T, R, KV = ATT_TILE, NSA_R, NSA_KV
    table = rel_bias.astype(F32) * LOG2E
    ii = jnp.arange(T)
    delta = ii[None, :] - ii[:, None]

    def lanes(a):
        a = jnp.moveaxis(a, -1, 0)
        a = a.reshape((KV, R) + a.shape[1:])
        return jnp.moveaxis(a, 1, 2).reshape(KV, a.shape[2], R * a.shape[3])

    def tile(off):
        return lanes(_bias_lookup(table, off * T + delta))

    mask4 = lambda ok: jnp.tile(jnp.where(ok, 0.0, NEG), (1, R))[None]
    n_far = -(-REL_MAX_DIST // T) + 1
    selb = [tile(o) for o in range(n_far + 1)]
    selb[0] = selb[0] + mask4(delta >= 0)
    selb = jnp.stack(selb, axis=0)
    n_win = WINDOW // T
    winb = [tile(o) + mask4((o * T + delta >= 0) & (o * T + delta < WINDOW)) for o in range(n_win + 1)]
    winb.append(jnp.full_like(winb[0], NEG))
    winb = jnp.stack(winb, axis=0)

    grp = T // CMP_STRIDE
    far = _bias_lookup(table, jnp.asarray(2 * REL_MAX_DIST))
    xx = jnp.arange(4 * grp)
    bdist = ii[None, :] - CMP_STRIDE * (xx[:, None] - 2 * grp) - (CMP_BLOCK - 1)
    band = jnp.where((bdist >= 0)[..., None], _bias_lookup(table, bdist) - far, NEG)
    band = jnp.concatenate([lanes(band), jnp.zeros((KV, 2 * grp, R * T), F32)], axis=1)[:, None]
    cfar = jnp.repeat(far.reshape(KV, R), T, axis=1)[:, None]

    n_pad = S // CMP_STRIDE
    n_sel = S // SEL_BLOCK
    cmp_start = jnp.arange(n_pad) * CMP_STRIDE
    sel_start = jnp.arange(n_sel) * SEL_BLOCK
    overlap = jnp.clip(jnp.minimum(cmp_start[:, None] + CMP_BLOCK, sel_start[None] + SEL_BLOCK)
                       - jnp.maximum(cmp_start[:, None], sel_start[None]), 0).astype(F32) / CMP_BLOCK
    n_cmp = (S - CMP_BLOCK) // CMP_STRIDE + 1
    overlap_t = jnp.where((jnp.arange(n_pad) < n_cmp)[:, None], overlap, 0.0).T
    return cfar, band, selb, winb, overlap_t


def _nsa_t_attention(q4, gv, kcmp, vcmp_t, ks, vs_t, kw, vw_t, tables):
    B, KV, S, _ = kw.shape
    T = ATT_TILE
    cfar, band, selb, winb, overlap_t = tables
    gw = NSA_R * NSA_DH
    n_pad = kcmp.shape[2]
    seq = lambda a: pl.BlockSpec((1, 1) + a.shape[2:], lambda b, g, i: (b, g) + (0,) * (a.ndim - 2))
    qtile = lambda a: pl.BlockSpec((1, 1, 1) + a.shape[3:], lambda b, g, i: (b, g, i, 0, 0))
    grp = lambda a: pl.BlockSpec((1,) + a.shape[1:], lambda b, g, i: (g,) + (0,) * (a.ndim - 1))
    tiles = lambda a: pl.BlockSpec((a.shape[0], 1) + a.shape[2:], lambda b, g, i: (0, g, 0, 0))
    full = lambda a: pl.BlockSpec(a.shape, lambda b, g, i: (0,) * a.ndim)
    return pl.pallas_call(
        _nsa_t_kernel,
        out_shape=jax.ShapeDtypeStruct((B, S, KV * gw), BF16),
        grid=(B, KV, S // T),
        in_specs=[qtile(q4), qtile(gv),
                  seq(kcmp), seq(vcmp_t), seq(ks), seq(vs_t), seq(kw), seq(vw_t),
                  grp(cfar), grp(band), tiles(selb), tiles(winb), full(overlap_t)],
        out_specs=pl.BlockSpec((1, T, gw), lambda b, g, i: (b, i, g)),
        scratch_shapes=[pltpu.VMEM((2 * n_pad + 2 * (T // CMP_STRIDE), NSA_R * T), F32),
                        pltpu.VMEM((S // SEL_BLOCK, T), F32),
                        pltpu.VMEM((2, T, NSA_R * T), BF16)],
        compiler_params=_cparams(("parallel", "parallel", "arbitrary")),
        name="nsa_attention",
    )(q4, gv, kcmp, vcmp_t, ks, vs_t, kw, vw_t, cfar, band, selb, winb, overlap_t)


def _moe_kernel(*refs, n_in, final):
    x_ref, mgate_ref = refs[:2]
    a_refs = refs[2:2 + n_in]
    wo_refs = refs[2 + n_in:2 + 2 * n_in]
    (g_ref, sh_ref, sc_ref, gate_ref, wr_ref, br_ref, before_ref, wg_ref, wu_ref, wd_ref, fg_ref,
     o_ref, x_all, hs_all, rts_all, acc_all, perm_t_all, meta_all) = refs[2 + 2 * n_in:]
    NG, PG, FH = MOE_GROUPS, MOE_PER_GROUP, MOE_HIDDEN
    TP = x_all.shape[0]
    s = pl.program_id(2)

    def prologue(hh):
        x_scr, hs_scr, rts_scr, acc_scr, perm_t_scr = (r.at[hh] for r in (x_all, hs_all, rts_all, acc_all, perm_t_all))
        meta = meta_all.at[hh]
        mix = functools.reduce(lambda a, b: a + b,
                               [_dot(a_ref[0].astype(BF16), wo_ref[...]) for a_ref, wo_ref in zip(a_refs, wo_refs)])
        x = x_ref[0] + mgate_ref[0] * mix
        x_scr[...] = x
        h = _modulated_norm(x, g_ref[...], sh_ref[0], sc_ref[0])
        h_hi = h.astype(BF16)
        h_lo = (h - h_hi.astype(F32)).astype(BF16)
        both = _dot(h_hi, wr_ref[...])
        logits = (both[:, :LANES] + both[:, LANES:] + _dot(h_lo, wr_ref[:, :LANES]) + br_ref[...]).T
        gl = [logits[NG * PG + g:NG * PG + g + 1, :] for g in range(NG)]
        gmax = functools.reduce(jnp.maximum, gl)
        gtop = jnp.full_like(gmax, float(NG - 1))
        for g in reversed(range(NG - 1)):
            gtop = jnp.where(gl[g] == gmax, float(g), gtop)
        p_g = 1.0 / functools.reduce(lambda a, b: a + b, [jnp.exp(v - gmax) for v in gl])
        a = []
        for j in range(PG):
            v = logits[(NG - 1) * PG + j:(NG - 1) * PG + j + 1, :]
            for g in reversed(range(NG - 1)):
                v = jnp.where(gtop == float(g), logits[g * PG + j:g * PG + j + 1, :], v)
            a.append(v)

        def first_max(vals):
            vmax = functools.reduce(jnp.maximum, vals)
            taken = jnp.zeros_like(vmax) > 1.0
            hits = []
            for v in vals:
                hit = (v == vmax) & jnp.logical_not(taken)
                taken = taken | hit
                hits.append(hit)
            return vmax, hits

        v1, hit1 = first_max(a)
        rest = [jnp.where(hh, -jnp.inf, v) for hh, v in zip(hit1, a)]
        v2, hit2 = first_max(rest)
        e2 = jnp.exp(v2 - v1)
        w1 = p_g / (1.0 + e2)
        w2 = p_g * e2 / (1.0 + e2)
        tm = gtop.shape[1]
        row = lax.broadcasted_iota(jnp.int32, (SUBLANES, tm), 0)
        onehot = [jnp.where(gtop == float(g), 1.0, 0.0) for g in range(NG)]
        oh8 = jnp.zeros((SUBLANES, tm), F32)
        for g in range(NG):
            oh8 = jnp.where(row == g, onehot[g], oh8)
        before = _dot(oh8.astype(BF16), before_ref[...])
        pos = jnp.zeros_like(gtop)
        off = jnp.int32(0)
        for g in range(NG):
            cnt = jnp.sum(onehot[g]).astype(jnp.int32)
            meta[g] = off
            meta[NG + g] = cnt
            pos = pos + onehot[g] * (before[g:g + 1, :] + off.astype(F32))
            off = off + cnt
        rt = jnp.where(row == PG, gtop, jnp.where(row == PG + 1, pos, 0.0))
        for j in range(PG):
            wj = jnp.where(hit1[j], w1, jnp.where(hit2[j], w2, 0.0))
            rt = jnp.where(row == j, wj, rt)
        rt_tok = jnp.concatenate([rt, jnp.zeros((LANES - SUBLANES, tm), F32)], axis=0).T
        rid = lax.broadcasted_iota(jnp.int32, (tm, tm), 0).astype(F32)
        cid = lax.broadcasted_iota(jnp.int32, (tm, tm), 1).astype(F32)
        perm = jnp.where(rid == pos, 1.0, 0.0).astype(BF16)
        perm_t = jnp.where(rt_tok[:, PG + 1:PG + 2] == cid, 1.0, 0.0).astype(BF16)
        perm_t_scr[...] = perm_t
        pad = hs_scr.shape[0] - tm
        hs_scr[0:tm, :] = _dot(perm, h_hi).astype(BF16)
        hs_scr[tm:, :] = jnp.zeros((pad, h_hi.shape[1]), BF16)
        r1 = rt.astype(BF16)
        res = rt - r1.astype(F32)
        r2 = res.astype(BF16)
        r3 = (res - r2.astype(F32)).astype(BF16)
        rt_sorted = _dot(r1, perm_t) + _dot(r2, perm_t) + _dot(r3, perm_t)
        rts_scr[0:tm, :] = jnp.concatenate([rt_sorted, jnp.zeros((LANES - SUBLANES, tm), F32)], axis=0).T
        rts_scr[tm:, :] = jnp.full((pad, LANES), -1.0, F32)
        acc_scr[...] = jnp.zeros(acc_scr.shape, F32)

    tm = x_all.shape[1]
    WIN = hs_all.shape[1] - tm

    def experts(c):
        cf = c.astype(F32)
        bases, counts = [], []
        for hh in range(TP):
            off = meta_all[hh, c]
            cnt = meta_all[hh, NG + c]
            base = (off // MOE_ALIGN) * MOE_ALIGN
            bases.append(base)
            counts.append(jnp.where(cnt > 0, (off + cnt - base + WIN - 1) // WIN, 0))

        def win_body(w, carry):
            starts = [pl.multiple_of(jnp.where(w < counts[hh], bases[hh] + w * WIN, tm), MOE_ALIGN) for hh in range(TP)]
            hs = jnp.concatenate([hs_all[hh, pl.ds(starts[hh], WIN), :] for hh in range(TP)], axis=0)
            rt = jnp.concatenate([rts_all[hh, pl.ds(starts[hh], WIN), :] for hh in range(TP)], axis=0)
            in_group = rt[:, PG:PG + 1] == cf
            hid = _silu(_dot(hs, wg_ref[0])) * _dot(hs, wu_ref[0])
            parts = [hid[:, j * FH:(j + 1) * FH] * jnp.where(in_group, rt[:, j:j + 1], 0.0) for j in range(PG)]
            out = _dot(jnp.concatenate(parts, axis=1).astype(BF16), wd_ref[0])
            for hh in range(TP):
                acc_all[hh, pl.ds(starts[hh], WIN), :] += out[hh * WIN:(hh + 1) * WIN]
            return carry

        lax.fori_loop(0, functools.reduce(jnp.maximum, counts), win_body, 0)

    def epilogue(hh):
        ys = acc_all[hh, 0:tm, :]
        ys_hi = ys.astype(BF16)
        ys_lo = (ys - ys_hi.astype(F32)).astype(BF16)
        back = _dot(perm_t_all[hh], jnp.concatenate([ys_hi, ys_lo], axis=1))
        d = ys.shape[1]
        y = x_all[hh] + gate_ref[0] * (back[:, :d] + back[:, d:])
        if final:
            y = y * lax.rsqrt(jnp.mean(y * y, axis=-1, keepdims=True) + EPS) * fg_ref[...]
        o_ref[0, hh * tm:(hh + 1) * tm, :] = y

    for hh in range(TP):
        pl.when(s == hh)(functools.partial(prologue, hh))
    pl.when((s >= TP - 1) & (s <= TP + NG - 2))(lambda: experts(s - (TP - 1)))
    for hh in range(TP):
        pl.when(s == NG + TP - 2)(functools.partial(epilogue, hh))


def _moe(x, mix_gate, acts, w_outs, g, shift, scale, gate, wg, bg, we, be, w_gate, w_up, w_down, final_g, final,
         tm=512):
    B, S, D = x.shape
    n_in = len(acts)
    NG, PG, FH = MOE_GROUPS, MOE_PER_GROUP, MOE_HIDDEN
    wr = jnp.zeros((D, LANES), F32)
    wr = wr.at[:, :NG * PG].set(we.reshape(D, NG * PG).astype(F32)).at[:, NG * PG:NG * PG + NG].set(wg.astype(F32))
    br = jnp.zeros((1, LANES), F32)
    br = br.at[0, :NG * PG].set(be.reshape(NG * PG).astype(F32)).at[0, NG * PG:NG * PG + NG].set(bg.astype(F32))
    wr_hi = wr.astype(BF16)
    wr = jnp.concatenate([wr_hi, (wr - wr_hi.astype(F32)).astype(BF16)], axis=1)
    grp = lambda w: w.reshape(NG, PG, D, FH).transpose(0, 2, 1, 3).reshape(NG, D, PG * FH).astype(BF16)
    wd = w_down.reshape(NG, PG * FH, D).astype(BF16)
    ids = jnp.arange(tm)
    before = (ids[:, None] < ids[None, :]).astype(BF16)
    TP = MOE_TILES
    n_steps = NG + TP - 1
    vec = pl.BlockSpec((1, 1, D), lambda b, i, s: (b, 0, 0))
    row = pl.BlockSpec((1, D), lambda b, i, s: (0, 0))
    wspec = lambda k, n: pl.BlockSpec((1, k, n), lambda b, i, s: (jnp.clip(s - (TP - 1), 0, NG - 1), 0, 0))
    tokens_in = lambda n: pl.BlockSpec((1, tm, n), lambda b, i, s: (b, i * TP + jnp.minimum(s, TP - 1), 0))
    tokens_out = pl.BlockSpec((1, TP * tm, D), lambda b, i, s: (b, i, 0))
    const = lambda a: pl.BlockSpec(a.shape, lambda b, i, s: (0,) * a.ndim)
    return pl.pallas_call(
        functools.partial(_moe_kernel, n_in=n_in, final=final),
        out_shape=jax.ShapeDtypeStruct((B, S, D), F32),
        grid=(B, S // (tm * TP), n_steps),
        in_specs=[tokens_in(D), vec] + [tokens_in(a.shape[2]) for a in acts] + [const(w) for w in w_outs]
                 + [row, vec, vec, vec, const(wr), const(br), const(before),
                    wspec(D, PG * FH), wspec(D, PG * FH), wspec(PG * FH, D), row],
        out_specs=tokens_out,
        scratch_shapes=[pltpu.VMEM((TP, tm, D), F32), pltpu.VMEM((TP, tm + MOE_WIN, D), BF16),
                        pltpu.VMEM((TP, tm + MOE_WIN, LANES), F32), pltpu.VMEM((TP, tm + MOE_WIN, D), F32),
                        pltpu.VMEM((TP, tm, tm), BF16), pltpu.SMEM((TP, 2 * NG), jnp.int32)],
        compiler_params=_cparams(("parallel", "parallel", "arbitrary")),
        name="moe",
    )(x, mix_gate, *acts, *w_outs, g.reshape(1, D), shift, scale, gate, wr, br, before, grp(w_gate), grp(w_up), wd,
      final_g.reshape(1, D))


def _mlstm_s5_layer(x, g, shift, scale, w_in, conv_w, b_i, b_f, head_g, s5_params, w_out):
    H = MLSTM_HEADS
    A = MIX_A
    w_if = jnp.zeros((D_MODEL, LANES), F32).at[:, :2 * H].set(w_in[:, 4 * A:4 * A + 2 * H])
    weights = [w_in[:, :2 * A], w_in[:, 2 * A:4 * A], w_if, w_in[:, 4 * A + 2 * H:]]
    qk, vo, ifg, u = _norm_matmul(x, g, shift, scale, [w.astype(BF16) for w in weights], [BF16, BF16, F32, F32])
    gate_bias = jnp.zeros((1, LANES), F32).at[0, :H].set(b_i.astype(F32)).at[0, H:2 * H].set(b_f.astype(F32))
    hm = _mlstm(qk, vo, ifg, conv_w.astype(F32), gate_bias, head_g.reshape(1, A).astype(F32))
    ys = _s5s(u, _s5s_tables(*s5_params))
    w_out = w_out.astype(BF16)
    return [hm, ys], [w_out[:A], w_out[A:]]


def _nsa_layer(x, g, shift, scale, w_in, b_gate, cmp_pos, cmp_w1, cmp_b1, cmp_w2, cmp_b2, rel_bias, w_out):
    B, S, D = x.shape
    KV, R, DH = NSA_KV, NSA_R, NSA_DH
    w_g = jnp.zeros((D, KV, LANES), F32).at[:, :, :3 * R].set(w_in[:, D + 6 * KV_W:].reshape(D, KV, 3 * R))
    b_g = jnp.zeros((KV, LANES), F32).at[:, :3 * R].set(b_gate.reshape(KV, 3 * R).astype(F32))
    kv_cols = lambda i: w_in[:, D + i * KV_W:D + (i + 1) * KV_W]
    w_k = jnp.concatenate([kv_cols(0), kv_cols(2), kv_cols(4)], axis=1)
    w_v = jnp.concatenate([kv_cols(1), kv_cols(3), kv_cols(5)], axis=1)
    weights = [w_in[:, :D], w_k, w_v, w_g.reshape(D, KV * LANES)]
    q4, gv, kc, vc, ks, kw, vs_t, vw_t = _nsa_proj(x, g, shift, scale, [w.astype(BF16) for w in weights],
                                                   b_g.reshape(1, KV * LANES))
    grp = CMP_STRIDE
    xg = jnp.stack([kc, vc]).reshape(2, B, KV * S // grp, grp * DH)
    cmp = _compress(xg, cmp_pos, cmp_w1, cmp_b1, cmp_w2, cmp_b2).reshape(2, B, KV, S // grp, DH).astype(BF16)
    out = _nsa_t_attention(q4, gv, cmp[0], cmp[1].transpose(0, 1, 3, 2), ks, vs_t, kw, vw_t,
                           _nsa_t_tables(rel_bias, S))
    return [out], [w_out.astype(BF16)]


def kernel(x, c, rel_bias, ada_w, ada_b, norm_g, final_g,
           a_w_in, a_conv, a_b_i, a_b_f, a_head_g,
           s5_lam_re, s5_lam_im, s5_log_dt, s5_b_re, s5_b_im, s5_c_re, s5_c_im,
           s5_d, s5_glu_w, s5_glu_b, a_w_out,
           n_w_in, n_b_gate, n_cmp_pos, n_cmp_w1, n_cmp_b1, n_cmp_w2, n_cmp_b2, n_w_out,
           r_grp_w, r_grp_b, r_exp_w, r_exp_b, e_w_gate, e_w_up, e_w_down):
    B, S, D = x.shape
    mod = _ada_mod(c, ada_w, ada_b).reshape(DEPTH, 2, B, 1, 3 * D)
    split = lambda m: (m[..., :D], m[..., D:2 * D], m[..., 2 * D:])
    for layer in range(DEPTH):
        shift, scale, mix_gate = split(mod[layer, 0])
        j = layer // 2
        if layer % 2 == 0:
            s5_params = (s5_lam_re[j], s5_lam_im[j], s5_log_dt[j], s5_b_re[j], s5_b_im[j],
                         s5_c_re[j], s5_c_im[j], s5_d[j], s5_glu_w[j], s5_glu_b[j])
            acts, w_outs = _mlstm_s5_layer(x, norm_g[layer, 0], shift, scale, a_w_in[j], a_conv[j], a_b_i[j],
                                           a_b_f[j], a_head_g[j], s5_params, a_w_out[j])
        else:
            acts, w_outs = _nsa_layer(x, norm_g[layer, 0], shift, scale, n_w_in[j], n_b_gate[j], n_cmp_pos[j],
                                      n_cmp_w1[j], n_cmp_b1[j], n_cmp_w2[j], n_cmp_b2[j], rel_bias, n_w_out[j])
        shift, scale, gate = split(mod[layer, 1])
        x = _moe(x, mix_gate, acts, w_outs, norm_g[layer, 1], shift, scale, gate, r_grp_w[layer], r_grp_b[layer],
                 r_exp_w[layer], r_exp_b[layer], e_w_gate[layer], e_w_up[layer], e_w_down[layer], final_g,
                 final=(layer == DEPTH - 1))
    return x
```

```python
import functools
import math

import jax
import jax.numpy as jnp
from jax import lax
from jax.experimental import pallas as pl
from jax.experimental.pallas import tpu as pltpu

F32 = jnp.float32
BF16 = jnp.bfloat16
HIGHEST = lax.Precision.HIGHEST

D_MODEL = 1024
DEPTH = 2
MIX_A = 512
MLSTM_HEADS = 4
MLSTM_DH = MIX_A // MLSTM_HEADS
MLSTM_CHUNK = 128
CONV_K = 4
S5_GROUP = 16
S5_STATE = 64
S5_CHUNK = 16
NSA_HEADS = 16
NSA_KV = 4
NSA_R = NSA_HEADS // NSA_KV
NSA_DH = D_MODEL // NSA_HEADS
KV_W = NSA_KV * NSA_DH
CMP_BLOCK = 32
CMP_STRIDE = 16
CMP_HIDDEN = 256
SEL_BLOCK = 64
SEL_TOPK = 16
WINDOW = 512
FORCE = 1e9
REL_BUCKETS = 32
REL_MAX_DIST = 128
MOE_GROUPS = 4
MOE_PER_GROUP = 4
MOE_HIDDEN = 256
EPS = 1e-6
NEG = -1e30
BIG = 1e30
LOG2E = math.log2(math.e)

LANES = 128
SUBLANES = 8
ATT_TILE = 256
MOE_WIN = 160
MOE_ALIGN = 16
MOE_TILES = 2
VMEM_LIMIT = 56 * 1024 * 1024


def _cparams(sem):
    return pltpu.CompilerParams(dimension_semantics=sem, vmem_limit_bytes=VMEM_LIMIT)


def _dot(a, b, precision=None):
    return jnp.dot(a, b, preferred_element_type=F32, precision=precision)


def _dot_nt(a, b):
    return lax.dot_general(a, b, (((1,), (1,)), ((), ())), preferred_element_type=F32)


def _sigmoid(x):
    return 1.0 / (1.0 + jnp.exp(-x))


def _silu(x):
    return x * _sigmoid(x)


def _gelu_tanh(x):
    return 0.5 * x * (1.0 + jnp.tanh(math.sqrt(2.0 / math.pi) * (x + 0.044715 * (x * x * x))))


def _modulated_norm(x, g, shift, scale):
    y = x * lax.rsqrt(jnp.mean(x * x, axis=-1, keepdims=True) + EPS) * g
    return y * (1.0 + scale) + shift


def _ada_kernel(c_ref, w_ref, b_ref, o_ref):
    c = c_ref[...]
    o_ref[0] = _dot(_silu(c), w_ref[0]) + b_ref[0]


def _ada_mod(c, ada_w, ada_b):
    B, D = c.shape
    n_mod = ada_w.shape[0] * ada_w.shape[1]
    w = ada_w.reshape(n_mod, D, 3 * D)
    b = ada_b.reshape(n_mod, 1, 3 * D)
    tn = 1024
    return pl.pallas_call(
        _ada_kernel,
        out_shape=jax.ShapeDtypeStruct((n_mod, B, 3 * D), F32),
        grid=(n_mod, 3 * D // tn),
        in_specs=[pl.BlockSpec((B, D), lambda i, j: (0, 0)),
                  pl.BlockSpec((1, D, tn), lambda i, j: (i, 0, j)),
                  pl.BlockSpec((1, 1, tn), lambda i, j: (i, 0, j))],
        out_specs=pl.BlockSpec((1, B, tn), lambda i, j: (i, 0, j)),
        compiler_params=_cparams(("parallel", "parallel")),
        name="ada_mod",
    )(c, w, b)


def _norm_mm_kernel(*refs, n_w):
    x_ref, g_ref, sh_ref, sc_ref = refs[:4]
    w_refs = refs[4:4 + n_w]
    o_refs = refs[4 + n_w:]
    h = _modulated_norm(x_ref[0], g_ref[...], sh_ref[0], sc_ref[0]).astype(BF16)
    for w_ref, o_ref in zip(w_refs, o_refs):
        o_ref[0] = _dot(h, w_ref[...]).astype(o_ref.dtype)


def _norm_matmul(x, g, shift, scale, weights, out_dtypes, tm=512):
    B, S, D = x.shape
    n_w = len(weights)
    vec = pl.BlockSpec((1, 1, D), lambda b, i: (b, 0, 0))
    in_specs = [pl.BlockSpec((1, tm, D), lambda b, i: (b, i, 0)),
                pl.BlockSpec((1, D), lambda b, i: (0, 0)), vec, vec]
    in_specs += [pl.BlockSpec(w.shape, lambda b, i: (0, 0)) for w in weights]
    return pl.pallas_call(
        functools.partial(_norm_mm_kernel, n_w=n_w),
        out_shape=[jax.ShapeDtypeStruct((B, S, w.shape[1]), dt) for w, dt in zip(weights, out_dtypes)],
        grid=(B, S // tm),
        in_specs=in_specs,
        out_specs=[pl.BlockSpec((1, tm, w.shape[1]), lambda b, i: (b, i, 0)) for w in weights],
        compiler_params=_cparams(("parallel", "parallel")),
        name="norm_matmul",
    )(x, g.reshape(1, D), shift, scale, *weights)


def _mlstm_kernel(qk_ref, vo_ref, if_ref, cw_ref, gb_ref, hg_ref, tril_ref, o_ref,
                  xbuf, c_scr, n_scr, m_scr):
    pad = SUBLANES

    @pl.when(pl.program_id(1) == 0)
    def _():
        xbuf[:, 0:pad, :] = jnp.zeros((xbuf.shape[0], pad, 2 * MIX_A), F32)
        c_scr[...] = jnp.zeros_like(c_scr)
        n_scr[...] = jnp.zeros_like(n_scr)
        m_scr[...] = jnp.zeros_like(m_scr)

    for bb in range(qk_ref.shape[0]):
        _mlstm_chunk(qk_ref.at[bb], vo_ref.at[bb], if_ref.at[bb], cw_ref, gb_ref, hg_ref, tril_ref, o_ref.at[bb],
                     xbuf.at[bb], c_scr.at[bb], n_scr.at[bb], m_scr.at[bb])


def _mlstm_chunk(qk_ref, vo_ref, if_ref, cw_ref, gb_ref, hg_ref, tril_ref, o_ref, xbuf, c_scr, n_scr, m_scr):
    L, H, DH = MLSTM_CHUNK, MLSTM_HEADS, MLSTM_DH
    pad = SUBLANES
    xbuf[pad:pad + L, :] = qk_ref[...].astype(F32)
    cw = cw_ref[...]
    conv = None
    for j in range(CONV_K):
        lo = pad - (CONV_K - 1) + j
        t = xbuf[lo:lo + L, :] * cw[j:j + 1, :]
        conv = t if conv is None else conv + t
    xbuf[0:pad, :] = xbuf[L:L + pad, :]
    qk = _silu(conv)
    q = qk[:, :MIX_A]
    k = qk[:, MIX_A:] * (DH ** -0.5)
    vo = vo_ref[...].astype(F32)
    v = vo[:, :MIX_A]
    o_pre = vo[:, MIX_A:]

    ifb = if_ref[...] + gb_ref[...]
    lf = jnp.minimum(ifb, 0.0) - jnp.log1p(jnp.exp(-jnp.abs(ifb)))
    bcs = _dot(tril_ref[...], lf, precision=HIGHEST)
    ifb_t = ifb.T
    bcs_t = bcs.T
    row = lax.broadcasted_iota(jnp.int32, (L, L), 0)
    col = lax.broadcasted_iota(jnp.int32, (L, L), 1)
    causal = col <= row

    outs = []
    for h in range(H):
        sl = slice(h * DH, (h + 1) * DH)
        qh, kh, vh = q[:, sl], k[:, sl], v[:, sl]
        qb, kb = qh.astype(BF16), kh.astype(BF16)
        b_col = bcs[:, H + h:H + h + 1]
        b_row = bcs_t[H + h:H + h + 1, :]
        li_col = ifb[:, h:h + 1]
        li_row = ifb_t[h:h + 1, :]
        b_last = b_col[L - 1:L, :]
        m0 = m_scr[h][:, 0:1]
        c0 = c_scr[h]
        n0 = n_scr[h]

        log_d = jnp.where(causal, b_col - b_row + li_row, NEG)
        log_inter = b_col + m0
        m_t = jnp.maximum(log_inter, jnp.max(log_d, axis=1, keepdims=True))
        dmat = jnp.exp(log_d - m_t)
        a_inter = jnp.exp(log_inter - m_t)
        s = _dot_nt(qb, kb) * dmat
        num = _dot(s.astype(BF16), vh.astype(BF16)) + a_inter * _dot_nt(qb, c0.astype(BF16))
        den = jnp.sum(s, axis=1, keepdims=True) + a_inter * jnp.sum(qh * n0, axis=1, keepdims=True)
        hh = num / jnp.maximum(jnp.abs(den), jnp.exp(-m_t))

        w_col = b_last - b_col + li_col
        m_loc = jnp.max(w_col, axis=0, keepdims=True)
        e = jnp.exp(w_col - m_loc)
        c_loc = _dot((vh * e).T.astype(BF16), kb)
        n_loc = jnp.sum(kh * e, axis=0, keepdims=True)
        m_new = jnp.maximum(b_last + m0, m_loc)
        a = jnp.exp(b_last + m0 - m_new)
        sc = jnp.exp(m_loc - m_new)
        c_scr[h] = a * c0 + sc * c_loc
        n_scr[h] = a * n0 + sc * n_loc
        m_scr[h] = jnp.broadcast_to(m_new, (1, LANES))

        outs.append(hh * lax.rsqrt(jnp.mean(hh * hh, axis=1, keepdims=True) + EPS))
    hm = jnp.concatenate(outs, axis=1)
    o_ref[...] = (_sigmoid(o_pre) * (hm * hg_ref[...])).astype(o_ref.dtype)


def _mlstm(qk, vo, ifg, conv_w, gate_bias, head_g):
    B, S, _ = qk.shape
    rows = 1
    L, H, DH = MLSTM_CHUNK, MLSTM_HEADS, MLSTM_DH
    tril = jnp.tril(jnp.ones((L, L), F32))
    return pl.pallas_call(
        _mlstm_kernel,
        out_shape=jax.ShapeDtypeStruct((B, S, MIX_A), BF16),
        grid=(B // rows, S // L),
        in_specs=[pl.BlockSpec((rows, L, 2 * MIX_A), lambda b, c: (b, c, 0)),
                  pl.BlockSpec((rows, L, 2 * MIX_A), lambda b, c: (b, c, 0)),
                  pl.BlockSpec((rows, L, LANES), lambda b, c: (b, c, 0)),
                  pl.BlockSpec((CONV_K, 2 * MIX_A), lambda b, c: (0, 0)),
                  pl.BlockSpec((1, LANES), lambda b, c: (0, 0)),
                  pl.BlockSpec((1, MIX_A), lambda b, c: (0, 0)),
                  pl.BlockSpec((L, L), lambda b, c: (0, 0))],
        out_specs=pl.BlockSpec((rows, L, MIX_A), lambda b, c: (b, c, 0)),
        scratch_shapes=[pltpu.VMEM((rows, L + SUBLANES, 2 * MIX_A), F32),
                        pltpu.VMEM((rows, H, DH, DH), F32),
                        pltpu.VMEM((rows, H, 1, DH), F32),
                        pltpu.VMEM((rows, H, 1, LANES), F32)],
        compiler_params=_cparams(("parallel", "arbitrary")),
        name="mlstm",
    )(qk, vo, ifg, conv_w, gate_bias, head_g, tril)


S5_LT = LANES // S5_GROUP
S5_PAIRS = S5_CHUNK // 2


def _s5s_kernel(u_ref, h_ref, e_ref, kk_ref, are_ref, aim_ref, d_ref, gw_ref, gb_ref, o_ref, xl_scr, x0_scr):
    n_chunks = u_ref.shape[1] // S5_CHUNK
    half = S5_LT * S5_STATE
    tok = lambda s: u_ref[0, pl.ds(s, n_chunks, stride=S5_CHUNK), :]
    u2 = [jnp.concatenate([tok(2 * q), tok(2 * q + 1)], axis=1) for q in range(S5_PAIRS)]
    u2b = [v.astype(BF16) for v in u2]
    xl_scr[...] = functools.reduce(lambda a, b: a + b, [_dot(u2b[q], h_ref[0, q]) for q in range(S5_PAIRS)])
    a_re = are_ref[0]
    a_im = aim_ref[0]

    def body(a, carry):
        re, im = carry
        x0_scr[pl.ds(a, 1), 0:half] = re
        x0_scr[pl.ds(a, 1), half:2 * half] = im
        return (a_re * re - a_im * im + xl_scr[pl.ds(a, 1), 0:half],
                a_re * im + a_im * re + xl_scr[pl.ds(a, 1), half:2 * half])

    zero = jnp.zeros((1, half), F32)
    lax.fori_loop(0, n_chunks, body, (zero, zero), unroll=8)
    x0 = x0_scr[...].astype(BF16)
    for p in range(S5_PAIRS):
        y = _dot(x0, e_ref[0, p]) + u2[p] * d_ref[0]
        for q in range(p + 1):
            y = y + _dot(u2b[q], kk_ref[0, p - q])
        ys = _gelu_tanh(y)
        out = ys * _sigmoid(_dot(ys.astype(BF16), gw_ref[0]) + gb_ref[0])
        o_ref[0, pl.ds(2 * p, n_chunks, stride=S5_CHUNK), :] = out[:, :LANES].astype(o_ref.dtype)
        o_ref[0, pl.ds(2 * p + 1, n_chunks, stride=S5_CHUNK), :] = out[:, LANES:].astype(o_ref.dtype)


def _s5s_tables(lam_re, lam_im, log_dt, b_re, b_im, c_re, c_im, d_skip, glu_w, glu_b):
    T, C, P, LT = S5_CHUNK, S5_GROUP, S5_STATE, S5_LT
    G = lam_re.shape[0]
    NT = G // LT
    lam = lax.complex(lam_re.astype(F32), lam_im.astype(F32))
    dt = jnp.exp(log_dt.astype(F32))[:, None]
    lam_bar = jnp.exp(lam * dt)
    b_bar = ((lam_bar - 1.0) / lam)[..., None] * lax.complex(b_re.astype(F32), b_im.astype(F32))
    c_mat = lax.complex(c_re.astype(F32), c_im.astype(F32))
    taus = jnp.arange(T + 1, dtype=F32)
    pw = jnp.exp((lam * dt)[:, None, :] * taus[None, :, None])
    eye = jnp.eye(LT, dtype=F32)
    tiles = lambda a: a.reshape((NT, LT) + a.shape[1:])

    kern = jnp.einsum('gcp,gtp,gpd->gtdc', c_mat, pw[:, :T], b_bar, precision=HIGHEST).real
    kblk = jnp.einsum('nitdc,ij->ntidjc', tiles(kern), eye).reshape(NT, T, LANES, LANES)
    kblk = jnp.concatenate([jnp.zeros_like(kblk[:, :1]), kblk], axis=1)
    kk = jnp.stack([jnp.concatenate([jnp.concatenate([kblk[:, 2 * d + 1], kblk[:, 2 * d + 2]], axis=2),
                                     jnp.concatenate([kblk[:, 2 * d], kblk[:, 2 * d + 1]], axis=2)], axis=1)
                    for d in range(T // 2)], axis=1)

    hmat = pw[:, :T][:, ::-1, :, None] * b_bar[:, None]

    def state_cols(m):
        return jnp.einsum('nispc,ij->nsicjp', tiles(m), eye).reshape(NT, T, LANES, LT * P)

    h = jnp.concatenate([state_cols(hmat.real), state_cols(hmat.imag)], axis=3)
    h2 = h.reshape(NT, T // 2, 2 * LANES, 2 * LT * P)

    emat = c_mat[:, None] * pw[:, 1:][:, :, None, :]

    def state_rows(m):
        return jnp.einsum('nitcp,ij->ntjpic', tiles(m), eye).reshape(NT, T, LT * P, LANES)

    e = jnp.concatenate([state_rows(emat.real), state_rows(-emat.imag)], axis=2)
    e2 = e.reshape(NT, T // 2, 2, 2 * LT * P, LANES).transpose(0, 1, 3, 2, 4).reshape(NT, T // 2, 2 * LT * P, 2 * LANES)

    a_re = pw[:, T].real.reshape(NT, 1, LT * P)
    a_im = pw[:, T].imag.reshape(NT, 1, LT * P)
    pair = lambda v: jnp.tile(v.astype(F32).reshape(NT, 1, LANES), (1, 1, 2))
    gwb = jnp.einsum('nice,ij->nicje', tiles(glu_w.astype(F32)), eye).reshape(NT, LANES, LANES)
    zeros = jnp.zeros_like(gwb)
    gw2 = jnp.concatenate([jnp.concatenate([gwb, zeros], axis=2), jnp.concatenate([zeros, gwb], axis=2)], axis=1)
    return (h2.astype(BF16), e2.astype(BF16), kk.astype(BF16), a_re, a_im, pair(d_skip), gw2.astype(BF16), pair(glu_b))


def _s5s(u, tables):
    B, S, W = u.shape
    NT = W // LANES
    n_chunks = S // S5_CHUNK
    per_tile = lambda a: pl.BlockSpec((1,) + a.shape[1:], lambda j, b: (j,) + (0,) * (a.ndim - 1))
    return pl.pallas_call(
        _s5s_kernel,
        out_shape=jax.ShapeDtypeStruct((B, S, W), F32),
        grid=(NT, B),
        in_specs=[pl.BlockSpec((1, S, LANES), lambda j, b: (b, 0, j))] + [per_tile(t) for t in tables],
        out_specs=pl.BlockSpec((1, S, LANES), lambda j, b: (b, 0, j)),
        scratch_shapes=[pltpu.VMEM((n_chunks, 2 * S5_LT * S5_STATE), F32) for _ in range(2)],
        compiler_params=_cparams(("parallel", "parallel")),
        name="s5",
    )(u, *tables)


def _compress_kernel(x_ref, plo_ref, phi_ref, w1_ref, b1_ref, w2_ref, b2_ref, o_ref):
    x = x_ref[0, 0]
    half = x.shape[1]
    w1 = w1_ref[0]
    lo = _dot((x + plo_ref[0]).astype(BF16), w1[:half])
    hi = _dot((x + phi_ref[0]).astype(BF16), w1[half:])
    rows = x.shape[0]
    hid = _gelu_tanh(lo + pltpu.roll(hi, rows - 1, 0) + b1_ref[0])
    o_ref[0, 0] = _dot(hid.astype(BF16), w2_ref[0]) + b2_ref[0]


def _compress(xg, pos, w1, b1, w2, b2):
    _, B, rows, width = xg.shape
    pos_flat = pos.reshape(2, 2, 1, width).astype(F32)
    sel = lambda shape: pl.BlockSpec((1,) + shape, lambda j, b: (j, 0, 0))
    return pl.pallas_call(
        _compress_kernel,
        out_shape=jax.ShapeDtypeStruct((2, B, rows, NSA_DH), F32),
        grid=(2, B),
        in_specs=[pl.BlockSpec((1, 1, rows, width), lambda j, b: (j, b, 0, 0)),
                  sel((1, width)), sel((1, width)),
                  sel((2 * width, CMP_HIDDEN)), sel((1, CMP_HIDDEN)),
                  sel((CMP_HIDDEN, NSA_DH)), sel((1, NSA_DH))],
        out_specs=pl.BlockSpec((1, 1, rows, NSA_DH), lambda j, b: (j, b, 0, 0)),
        compiler_params=_cparams(("parallel", "parallel")),
        name="nsa_compress",
    )(xg, pos_flat[:, 0], pos_flat[:, 1], w1.astype(BF16), b1[:, None].astype(F32),
      w2.astype(BF16), b2[:, None].astype(F32))


def _t5_bucket(dist):
    dist = jnp.maximum(dist, 0)
    max_exact = REL_BUCKETS // 2
    log_ratio = jnp.log(jnp.maximum(dist, 1).astype(F32) / max_exact) / math.log(REL_MAX_DIST / max_exact)
    large = jnp.minimum(max_exact + (log_ratio * (REL_BUCKETS - max_exact)).astype(jnp.int32), REL_BUCKETS - 1)
    return jnp.where(dist < max_exact, dist, large)


def _nsa_proj_kernel(x_ref, g_ref, sh_ref, sc_ref, wq_ref, wk_ref, wv_ref, wg_ref, bg_ref,
                     q4_ref, gv_ref, kc_ref, vc_ref, ks_ref, kw_ref, vst_ref, vwt_ref):
    KV, R, DH, T = NSA_KV, NSA_R, NSA_DH, ATT_TILE
    h = _modulated_norm(x_ref[0], g_ref[...], sh_ref[0], sc_ref[0]).astype(BF16)
    q_t = (_dot(h, wq_ref[...]) * (DH ** -0.5 * LOG2E)).T.astype(BF16)
    gates_t = _sigmoid(_dot(h, wg_ref[...]) + bg_ref[...]).T
    row = lax.broadcasted_iota(jnp.int32, (SUBLANES, R * T), 0)
    for g in range(KV):
        q4_ref[0, g, 0] = jnp.concatenate([q_t[(g * R + r) * DH:(g * R + r + 1) * DH] for r in range(R)], axis=1)
        gv = jnp.zeros((SUBLANES, R * T), F32)
        for j in range(3):
            gj = jnp.concatenate([gates_t[g * LANES + 3 * r + j:g * LANES + 3 * r + j + 1] for r in range(R)], axis=1)
            gv = jnp.where(row == j, gj, gv)
        gv_ref[0, g, 0] = gv
    k3 = _dot(h, wk_ref[...])
    v3 = _dot(h, wv_ref[...])
    vs_t = v3[:, KV_W:2 * KV_W].T.astype(BF16)
    vw_t = v3[:, 2 * KV_W:].T.astype(BF16)
    for g in range(KV):
        cols = slice(g * DH, (g + 1) * DH)
        kc_ref[0, g] = k3[:, cols].astype(BF16)
        vc_ref[0, g] = v3[:, cols].astype(BF16)
        ks_ref[0, g] = k3[:, KV_W + g * DH:KV_W + (g + 1) * DH].astype(BF16)
        kw_ref[0, g] = k3[:, 2 * KV_W + g * DH:2 * KV_W + (g + 1) * DH].astype(BF16)
        vst_ref[0, g, 0] = vs_t[cols]
        vwt_ref[0, g, 0] = vw_t[cols]


def _nsa_proj(x, g, shift, scale, weights, b_gate):
    B, S, D = x.shape
    KV, R, DH, T = NSA_KV, NSA_R, NSA_DH, ATT_TILE
    vec = pl.BlockSpec((1, 1, D), lambda b, i: (b, 0, 0))
    keys = pl.BlockSpec((1, KV, T, DH), lambda b, i: (b, 0, i, 0))
    key_shape = jax.ShapeDtypeStruct((B, KV, S, DH), BF16)
    tile = lambda rows, width: pl.BlockSpec((1, KV, 1, rows, width), lambda b, i: (b, 0, i, 0, 0))
    tile_shape = lambda rows, width, dt: jax.ShapeDtypeStruct((B, KV, S // T, rows, width), dt)
    return pl.pallas_call(
        _nsa_proj_kernel,
        out_shape=[tile_shape(DH, R * T, BF16), tile_shape(SUBLANES, R * T, F32),
                   key_shape, key_shape, key_shape, key_shape,
                   tile_shape(DH, T, BF16), tile_shape(DH, T, BF16)],
        grid=(B, S // T),
        in_specs=[pl.BlockSpec((1, T, D), lambda b, i: (b, i, 0)),
                  pl.BlockSpec((1, D), lambda b, i: (0, 0)), vec, vec]
                 + [pl.BlockSpec(w.shape, lambda b, i: (0, 0)) for w in weights]
                 + [pl.BlockSpec(b_gate.shape, lambda b, i: (0, 0))],
        out_specs=[tile(DH, R * T), tile(SUBLANES, R * T), keys, keys, keys, keys, tile(DH, T), tile(DH, T)],
        compiler_params=_cparams(("parallel", "parallel")),
        name="nsa_proj",
    )(x, g.reshape(1, D), shift, scale, *weights, b_gate)


def _nsa_t_kernel(q4_ref, gv_ref, kc_ref, vct_ref, ks_ref, vst_ref, kw_ref, vwt_ref,
                  cfar_ref, band_ref, selb_ref, winb_ref, ovt_ref, o_ref, s_scr, sel_scr, sbuf):
    T = ATT_TILE
    R, DH = NSA_R, NSA_DH
    qi = pl.program_id(2)
    q0 = qi * T
    n_pad = kc_ref.shape[2]
    n_sel = ovt_ref.shape[0]
    n_far = selb_ref.shape[0] - 1
    n_win = winb_ref.shape[0] - 2
    band_rows = band_ref.shape[2] - T // CMP_STRIDE * 2

    q4 = q4_ref[0, 0, 0]
    t_lane = q0 + lax.broadcasted_iota(jnp.int32, (1, R * T), 1) % T

    ones_rows = DH
    with_ones = lambda v_t: jnp.concatenate([v_t, jnp.ones((ones_rows, v_t.shape[1]), v_t.dtype)], axis=0)
    gvec = lambda j: gv_ref[0, 0, 0, j:j + 1, :]

    grp = T // CMP_STRIDE
    r0 = jnp.maximum(qi * grp - 2 * grp, 0)
    x0 = r0 - (qi * grp - 2 * grp)
    r0 = pl.multiple_of(r0, SUBLANES)
    x0 = pl.multiple_of(x0, SUBLANES)
    lim = pl.multiple_of(qi * grp + 2 * grp, SUBLANES)
    half = R * T // 2
    o_cmp, o_win, psum, w_all = [], [], None, []
    for hv in range(2):
        ln = slice(hv * half, (hv + 1) * half)
        q4h = q4[:, ln]
        s_scr[0:n_pad, ln] = _dot(kc_ref[0, 0], q4h) + cfar_ref[0, :, ln]
        s_scr[n_pad:n_pad + 2 * grp, ln] = jnp.zeros((2 * grp, half), F32)
        s_scr[pl.ds(r0, band_rows), ln] += band_ref[0, 0, pl.ds(x0, band_rows), ln]
        s_scr[pl.ds(lim, n_pad), ln] = jnp.full((n_pad, half), NEG, F32)
        w_subs, w_vals = [], []
        for d in range(n_win + 1):
            kt = jnp.maximum(qi - d, 0)
            off = pl.multiple_of(kt * T, T)
            tile = jnp.where(qi >= d, d, n_win + 1)
            w_subs.append((_dot(kw_ref[0, 0, pl.ds(off, T), :], q4h) + winb_ref[tile, 0, :, ln]).astype(BF16))
            w_vals.append(with_ones(vwt_ref[0, 0, kt]))
        w_all.append((w_subs, w_vals))
    for hv in range(2):
        ln = slice(hv * half, (hv + 1) * half)
        w_subs, w_vals = w_all[hv]
        s = s_scr[0:n_pad, ln]
        e = jnp.exp2(s - jnp.max(s, axis=0, keepdims=True))
        inv = jnp.where(t_lane[:, ln] >= CMP_BLOCK - 1, 1.0 / jnp.sum(e, axis=0, keepdims=True), 0.0)
        p = e * inv
        o_cmp.append(_dot(vct_ref[0, 0], p.astype(BF16)))
        for r in range(R // 2):
            pr = p[:, r * T:(r + 1) * T]
            psum = pr if psum is None else psum + pr
        m_w = jnp.max(functools.reduce(jnp.maximum, w_subs), axis=0, keepdims=True)
        acc = functools.reduce(lambda a, b: a + b,
                               [_dot(vj, jnp.exp2(sj - m_w)) for sj, vj in zip(w_subs, w_vals)])
        o_win.append(acc[:DH] * (1.0 / acc[DH:DH + 1]))
    out_t = gvec(0) * jnp.concatenate(o_cmp, axis=1) + gvec(2) * jnp.concatenate(o_win, axis=1)

    imp_t = _dot(ovt_ref[...], psum, precision=HIGHEST)
    jj = lax.broadcasted_iota(jnp.int32, (n_sel, T), 0)
    blk_t = (q0 + lax.broadcasted_iota(jnp.int32, (1, T), 1)) // SEL_BLOCK
    forced = (jj == 0) | (jj == blk_t) | (jj == blk_t - 1)
    score = jnp.where(forced, FORCE, jnp.where(jj <= blk_t, imp_t, -1.0))
    n_blk = n_sel // SUBLANES
    rows = [score[v * SUBLANES:(v + 1) * SUBLANES] for v in range(n_blk)]
    cnts = [jnp.zeros((SUBLANES, T), F32) for _ in range(n_blk)]
    sub = lax.broadcasted_iota(jnp.int32, (SUBLANES, T), 0)
    for j2 in range(n_sel):
        c2 = score[j2:j2 + 1, :]
        for v in range(n_blk):
            lo = v * SUBLANES
            if lo > j2:
                beats = c2 >= rows[v]
            elif lo + SUBLANES - 1 <= j2:
                beats = c2 > rows[v]
            else:
                beats = (c2 > rows[v]) | ((c2 >= rows[v]) & (sub > j2 - lo))
            cnts[v] = cnts[v] + jnp.where(beats, 1.0, 0.0)
    cnt = jnp.concatenate(cnts, axis=0)
    chosen = (cnt < float(min(SEL_TOPK, n_sel))) & (jj <= blk_t)
    sel_scr[...] = jnp.where(chosen, 0.0, -BIG)

    def block_mask(kt):
        per_tile = T // SEL_BLOCK
        parts = [jnp.broadcast_to(sel_scr[pl.ds(kt * per_tile + i, 1), :], (SEL_BLOCK, T)) for i in range(per_tile)]
        m1 = jnp.concatenate(parts, axis=0)
        return jnp.concatenate([m1] * R, axis=1)

    def sel_scores(slot, kc):
        off = pl.multiple_of(kc * T, T)
        s = _dot(ks_ref[0, 0, pl.ds(off, T), :], q4)
        s = (s + selb_ref[jnp.clip(qi - kc, 0, n_far), 0] + block_mask(kc)).astype(BF16)
        sbuf[slot] = s
        return jnp.max(s, axis=0, keepdims=True).astype(F32)

    def sel_weighted(slot, kc, m_new):
        return _dot(with_ones(vst_ref[0, 0, kc]), jnp.exp2(sbuf[slot] - m_new.astype(BF16)))

    last_tile = vst_ref.shape[2] - 1

    def sel_body(i, carry):
        m, acc, m_even = carry
        m_odd = sel_scores(1, 2 * i + 1)
        m_new = jnp.maximum(m, m_even)
        acc = jnp.exp2(m - m_new) * acc + sel_weighted(0, 2 * i, m_new)
        m_even = sel_scores(0, jnp.minimum(2 * i + 2, last_tile))
        m_fin = jnp.maximum(m_new, m_odd)
        acc = jnp.exp2(m_new - m_fin) * acc + sel_weighted(1, 2 * i + 1, m_fin)
        return m_fin, acc, m_even

    _, acc, _ = lax.fori_loop(0, qi // 2 + 1, sel_body,
                              (jnp.full((1, R * T), NEG, F32), jnp.zeros((DH + ones_rows, R * T), F32),
                               sel_scores(0, 0)))
    out_t = out_t + gvec(1) * (acc[:DH] * (1.0 / acc[DH:DH + 1]))
    for pr in range(R // 2):
        pair = jnp.concatenate([out_t[:, (2 * pr) * T:(2 * pr + 1) * T],
                                out_t[:, (2 * pr + 1) * T:(2 * pr + 2) * T]], axis=0)
        o_ref[0, :, pr * 2 * DH:(pr + 1) * 2 * DH] = pair.T.astype(o_ref.dtype)


def _bias_lookup(table, dist):
    onehot = (_t5_bucket(dist)[..., None] == jnp.arange(table.shape[0])).astype(F32)
    return jnp.einsum('...k,kh->...h', onehot, table, precision=HIGHEST)


def _nsa_t_tables(rel_bias, S):
    T, R, KV = ATT_TILE, NSA_R, NSA_KV
    table = rel_bias.astype(F32) * LOG2E
    ii = jnp.arange(T)
    delta = ii[None, :] - ii[:, None]

    def lanes(a):
        a = jnp.moveaxis(a, -1, 0)
        a = a.reshape((KV, R) + a.shape[1:])
        return jnp.moveaxis(a, 1, 2).reshape(KV, a.shape[2], R * a.shape[3])

    def tile(off):
        return lanes(_bias_lookup(table, off * T + delta))

    mask4 = lambda ok: jnp.tile(jnp.where(ok, 0.0, NEG), (1, R))[None]
    n_far = -(-REL_MAX_DIST // T) + 1
    selb = [tile(o) for o in range(n_far + 1)]
    selb[0] = selb[0] + mask4(delta >= 0)
    selb = jnp.stack(selb, axis=0)
    n_win = WINDOW // T
    winb = [tile(o) + mask4((o * T + delta >= 0) & (o * T + delta < WINDOW)) for o in range(n_win + 1)]
    winb.append(jnp.full_like(winb[0], NEG))
    winb = jnp.stack(winb, axis=0)

    grp = T // CMP_STRIDE
    far = _bias_lookup(table, jnp.asarray(2 * REL_MAX_DIST))
    xx = jnp.arange(4 * grp)
    bdist = ii[None, :] - CMP_STRIDE * (xx[:, None] - 2 * grp) - (CMP_BLOCK - 1)
    band = jnp.where((bdist >= 0)[..., None], _bias_lookup(table, bdist) - far, NEG)
    band = jnp.concatenate([lanes(band), jnp.zeros((KV, 2 * grp, R * T), F32)], axis=1)[:, None]
    cfar = jnp.repeat(far.reshape(KV, R), T, axis=1)[:, None]

    n_pad = S // CMP_STRIDE
    n_sel = S // SEL_BLOCK
    cmp_start = jnp.arange(n_pad) * CMP_STRIDE
    sel_start = jnp.arange(n_sel) * SEL_BLOCK
    overlap = jnp.clip(jnp.minimum(cmp_start[:, None] + CMP_BLOCK, sel_start[None] + SEL_BLOCK)
                       - jnp.maximum(cmp_start[:, None], sel_start[None]), 0).astype(F32) / CMP_BLOCK
    n_cmp = (S - CMP_BLOCK) // CMP_STRIDE + 1
    overlap_t = jnp.where((jnp.arange(n_pad) < n_cmp)[:, None], overlap, 0.0).T
    return cfar, band, selb, winb, overlap_t


def _nsa_t_attention(q4, gv, kcmp, vcmp_t, ks, vs_t, kw, vw_t, tables):
    B, KV, S, _ = kw.shape
    T = ATT_TILE
    cfar, band, selb, winb, overlap_t = tables
    gw = NSA_R * NSA_DH
    n_pad = kcmp.shape[2]
    seq = lambda a: pl.BlockSpec((1, 1) + a.shape[2:], lambda b, g, i: (b, g) + (0,) * (a.ndim - 2))
    qtile = lambda a: pl.BlockSpec((1, 1, 1) + a.shape[3:], lambda b, g, i: (b, g, i, 0, 0))
    grp = lambda a: pl.BlockSpec((1,) + a.shape[1:], lambda b, g, i: (g,) + (0,) * (a.ndim - 1))
    tiles = lambda a: pl.BlockSpec((a.shape[0], 1) + a.shape[2:], lambda b, g, i: (0, g, 0, 0))
    full = lambda a: pl.BlockSpec(a.shape, lambda b, g, i: (0,) * a.ndim)
    return pl.pallas_call(
        _nsa_t_kernel,
        out_shape=jax.ShapeDtypeStruct((B, S, KV * gw), BF16),
        grid=(B, KV, S // T),
        in_specs=[qtile(q4), qtile(gv),
                  seq(kcmp), seq(vcmp_t), seq(ks), seq(vs_t), seq(kw), seq(vw_t),
                  grp(cfar), grp(band), tiles(selb), tiles(winb), full(overlap_t)],
        out_specs=pl.BlockSpec((1, T, gw), lambda b, g, i: (b, i, g)),
        scratch_shapes=[pltpu.VMEM((2 * n_pad + 2 * (T // CMP_STRIDE), NSA_R * T), F32),
                        pltpu.VMEM((S // SEL_BLOCK, T), F32),
                        pltpu.VMEM((2, T, NSA_R * T), BF16)],
        compiler_params=_cparams(("parallel", "parallel", "arbitrary")),
        name="nsa_attention",
    )(q4, gv, kcmp, vcmp_t, ks, vs_t, kw, vw_t, cfar, band, selb, winb, overlap_t)


def _moe_kernel(*refs, n_in, final):
    x_ref, mgate_ref = refs[:2]
    a_refs = refs[2:2 + n_in]
    wo_refs = refs[2 + n_in:2 + 2 * n_in]
    (g_ref, sh_ref, sc_ref, gate_ref, wr_ref, br_ref, before_ref, wg_ref, wu_ref, wd_ref, fg_ref,
     o_ref, x_all, hs_all, rts_all, acc_all, perm_t_all, meta_all) = refs[2 + 2 * n_in:]
    NG, PG, FH = MOE_GROUPS, MOE_PER_GROUP, MOE_HIDDEN
    TP = x_all.shape[0]
    s = pl.program_id(2)

    def prologue(hh):
        x_scr, hs_scr, rts_scr, acc_scr, perm_t_scr = (r.at[hh] for r in (x_all, hs_all, rts_all, acc_all, perm_t_all))
        meta = meta_all.at[hh]
        mix = functools.reduce(lambda a, b: a + b,
                               [_dot(a_ref[0].astype(BF16), wo_ref[...]) for a_ref, wo_ref in zip(a_refs, wo_refs)])
        x = x_ref[0] + mgate_ref[0] * mix
        x_scr[...] = x
        h = _modulated_norm(x, g_ref[...], sh_ref[0], sc_ref[0])
        h_hi = h.astype(BF16)
        h_lo = (h - h_hi.astype(F32)).astype(BF16)
        both = _dot(h_hi, wr_ref[...])
        logits = (both[:, :LANES] + both[:, LANES:] + _dot(h_lo, wr_ref[:, :LANES]) + br_ref[...]).T
        gl = [logits[NG * PG + g:NG * PG + g + 1, :] for g in range(NG)]
        gmax = functools.reduce(jnp.maximum, gl)
        gtop = jnp.full_like(gmax, float(NG - 1))
        for g in reversed(range(NG - 1)):
            gtop = jnp.where(gl[g] == gmax, float(g), gtop)
        p_g = 1.0 / functools.reduce(lambda a, b: a + b, [jnp.exp(v - gmax) for v in gl])
        a = []
        for j in range(PG):
            v = logits[(NG - 1) * PG + j:(NG - 1) * PG + j + 1, :]
            for g in reversed(range(NG - 1)):
                v = jnp.where(gtop == float(g), logits[g * PG + j:g * PG + j + 1, :], v)
            a.append(v)

        def first_max(vals):
            vmax = functools.reduce(jnp.maximum, vals)
            taken = jnp.zeros_like(vmax) > 1.0
            hits = []
            for v in vals:
                hit = (v == vmax) & jnp.logical_not(taken)
                taken = taken | hit
                hits.append(hit)
            return vmax, hits

        v1, hit1 = first_max(a)
        rest = [jnp.where(hh, -jnp.inf, v) for hh, v in zip(hit1, a)]
        v2, hit2 = first_max(rest)
        e2 = jnp.exp(v2 - v1)
        w1 = p_g / (1.0 + e2)
        w2 = p_g * e2 / (1.0 + e2)
        tm = gtop.shape[1]
        row = lax.broadcasted_iota(jnp.int32, (SUBLANES, tm), 0)
        onehot = [jnp.where(gtop == float(g), 1.0, 0.0) for g in range(NG)]
        oh8 = jnp.zeros((SUBLANES, tm), F32)
        for g in range(NG):
            oh8 = jnp.where(row == g, onehot[g], oh8)
        before = _dot(oh8.astype(BF16), before_ref[...])
        pos = jnp.zeros_like(gtop)
        off = jnp.int32(0)
        for g in range(NG):
            cnt = jnp.sum(onehot[g]).astype(jnp.int32)
            meta[g] = off
            meta[NG + g] = cnt
            pos = pos + onehot[g] * (before[g:g + 1, :] + off.astype(F32))
            off = off + cnt
        rt = jnp.where(row == PG, gtop, jnp.where(row == PG + 1, pos, 0.0))
        for j in range(PG):
            wj = jnp.where(hit1[j], w1, jnp.where(hit2[j], w2, 0.0))
            rt = jnp.where(row == j, wj, rt)
        rt_tok = jnp.concatenate([rt, jnp.zeros((LANES - SUBLANES, tm), F32)], axis=0).T
        rid = lax.broadcasted_iota(jnp.int32, (tm, tm), 0).astype(F32)
        cid = lax.broadcasted_iota(jnp.int32, (tm, tm), 1).astype(F32)
        perm = jnp.where(rid == pos, 1.0, 0.0).astype(BF16)
        perm_t = jnp.where(rt_tok[:, PG + 1:PG + 2] == cid, 1.0, 0.0).astype(BF16)
        perm_t_scr[...] = perm_t
        pad = hs_scr.shape[0] - tm
        hs_scr[0:tm, :] = _dot(perm, h_hi).astype(BF16)
        hs_scr[tm:, :] = jnp.zeros((pad, h_hi.shape[1]), BF16)
        r1 = rt.astype(BF16)
        res = rt - r1.astype(F32)
        r2 = res.astype(BF16)
        r3 = (res - r2.astype(F32)).astype(BF16)
        rt_sorted = _dot(r1, perm_t) + _dot(r2, perm_t) + _dot(r3, perm_t)
        rts_scr[0:tm, :] = jnp.concatenate([rt_sorted, jnp.zeros((LANES - SUBLANES, tm), F32)], axis=0).T
        rts_scr[tm:, :] = jnp.full((pad, LANES), -1.0, F32)
        acc_scr[...] = jnp.zeros(acc_scr.shape, F32)

    tm = x_all.shape[1]
    WIN = hs_all.shape[1] - tm

    def experts(c):
        cf = c.astype(F32)
        bases, counts = [], []
        for hh in range(TP):
            off = meta_all[hh, c]
            cnt = meta_all[hh, NG + c]
            base = (off // MOE_ALIGN) * MOE_ALIGN
            bases.append(base)
            counts.append(jnp.where(cnt > 0, (off + cnt - base + WIN - 1) // WIN, 0))

        def win_body(w, carry):
            starts = [pl.multiple_of(jnp.where(w < counts[hh], bases[hh] + w * WIN, tm), MOE_ALIGN) for hh in range(TP)]
            hs = jnp.concatenate([hs_all[hh, pl.ds(starts[hh], WIN), :] for hh in range(TP)], axis=0)
            rt = jnp.concatenate([rts_all[hh, pl.ds(starts[hh], WIN), :] for hh in range(TP)], axis=0)
            in_group = rt[:, PG:PG + 1] == cf
            hid = _silu(_dot(hs, wg_ref[0])) * _dot(hs, wu_ref[0])
            parts = [hid[:, j * FH:(j + 1) * FH] * jnp.where(in_group, rt[:, j:j + 1], 0.0) for j in range(PG)]
            out = _dot(jnp.concatenate(parts, axis=1).astype(BF16), wd_ref[0])
            for hh in range(TP):
                acc_all[hh, pl.ds(starts[hh], WIN), :] += out[hh * WIN:(hh + 1) * WIN]
            return carry

        lax.fori_loop(0, functools.reduce(jnp.maximum, counts), win_body, 0)

    def epilogue(hh):
        ys = acc_all[hh, 0:tm, :]
        ys_hi = ys.astype(BF16)
        ys_lo = (ys - ys_hi.astype(F32)).astype(BF16)
        back = _dot(perm_t_all[hh], jnp.concatenate([ys_hi, ys_lo], axis=1))
        d = ys.shape[1]
        y = x_all[hh] + gate_ref[0] * (back[:, :d] + back[:, d:])
        if final:
            y = y * lax.rsqrt(jnp.mean(y * y, axis=-1, keepdims=True) + EPS) * fg_ref[...]
        o_ref[0, hh * tm:(hh + 1) * tm, :] = y

    for hh in range(TP):
        pl.when(s == hh)(functools.partial(prologue, hh))
    pl.when((s >= TP - 1) & (s <= TP + NG - 2))(lambda: experts(s - (TP - 1)))
    for hh in range(TP):
        pl.when(s == NG + TP - 2)(functools.partial(epilogue, hh))


def _moe(x, mix_gate, acts, w_outs, g, shift, scale, gate, wg, bg, we, be, w_gate, w_up, w_down, final_g, final,
         tm=512):
    B, S, D = x.shape
    n_in = len(acts)
    NG, PG, FH = MOE_GROUPS, MOE_PER_GROUP, MOE_HIDDEN
    wr = jnp.zeros((D, LANES), F32)
    wr = wr.at[:, :NG * PG].set(we.reshape(D, NG * PG).astype(F32)).at[:, NG * PG:NG * PG + NG].set(wg.astype(F32))
    br = jnp.zeros((1, LANES), F32)
    br = br.at[0, :NG * PG].set(be.reshape(NG * PG).astype(F32)).at[0, NG * PG:NG * PG + NG].set(bg.astype(F32))
    wr_hi = wr.astype(BF16)
    wr = jnp.concatenate([wr_hi, (wr - wr_hi.astype(F32)).astype(BF16)], axis=1)
    grp = lambda w: w.reshape(NG, PG, D, FH).transpose(0, 2, 1, 3).reshape(NG, D, PG * FH).astype(BF16)
    wd = w_down.reshape(NG, PG * FH, D).astype(BF16)
    ids = jnp.arange(tm)
    before = (ids[:, None] < ids[None, :]).astype(BF16)
    TP = MOE_TILES
    n_steps = NG + TP - 1
    vec = pl.BlockSpec((1, 1, D), lambda b, i, s: (b, 0, 0))
    row = pl.BlockSpec((1, D), lambda b, i, s: (0, 0))
    wspec = lambda k, n: pl.BlockSpec((1, k, n), lambda b, i, s: (jnp.clip(s - (TP - 1), 0, NG - 1), 0, 0))
    tokens_in = lambda n: pl.BlockSpec((1, tm, n), lambda b, i, s: (b, i * TP + jnp.minimum(s, TP - 1), 0))
    tokens_out = pl.BlockSpec((1, TP * tm, D), lambda b, i, s: (b, i, 0))
    const = lambda a: pl.BlockSpec(a.shape, lambda b, i, s: (0,) * a.ndim)
    return pl.pallas_call(
        functools.partial(_moe_kernel, n_in=n_in, final=final),
        out_shape=jax.ShapeDtypeStruct((B, S, D), F32),
        grid=(B, S // (tm * TP), n_steps),
        in_specs=[tokens_in(D), vec] + [tokens_in(a.shape[2]) for a in acts] + [const(w) for w in w_outs]
                 + [row, vec, vec, vec, const(wr), const(br), const(before),
                    wspec(D, PG * FH), wspec(D, PG * FH), wspec(PG * FH, D), row],
        out_specs=tokens_out,
        scratch_shapes=[pltpu.VMEM((TP, tm, D), F32), pltpu.VMEM((TP, tm + MOE_WIN, D), BF16),
                        pltpu.VMEM((TP, tm + MOE_WIN, LANES), F32), pltpu.VMEM((TP, tm + MOE_WIN, D), F32),
                        pltpu.VMEM((TP, tm, tm), BF16), pltpu.SMEM((TP, 2 * NG), jnp.int32)],
        compiler_params=_cparams(("parallel", "parallel", "arbitrary")),
        name="moe",
    )(x, mix_gate, *acts, *w_outs, g.reshape(1, D), shift, scale, gate, wr, br, before, grp(w_gate), grp(w_up), wd,
      final_g.reshape(1, D))


def _mlstm_s5_layer(x, g, shift, scale, w_in, conv_w, b_i, b_f, head_g, s5_params, w_out):
    H = MLSTM_HEADS
    A = MIX_A
    w_if = jnp.zeros((D_MODEL, LANES), F32).at[:, :2 * H].set(w_in[:, 4 * A:4 * A + 2 * H])
    weights = [w_in[:, :2 * A], w_in[:, 2 * A:4 * A], w_if, w_in[:, 4 * A + 2 * H:]]
    qk, vo, ifg, u = _norm_matmul(x, g, shift, scale, [w.astype(BF16) for w in weights], [BF16, BF16, F32, F32])
    gate_bias = jnp.zeros((1, LANES), F32).at[0, :H].set(b_i.astype(F32)).at[0, H:2 * H].set(b_f.astype(F32))
    hm = _mlstm(qk, vo, ifg, conv_w.astype(F32), gate_bias, head_g.reshape(1, A).astype(F32))
    ys = _s5s(u, _s5s_tables(*s5_params))
    w_out = w_out.astype(BF16)
    return [hm, ys], [w_out[:A], w_out[A:]]


def _nsa_layer(x, g, shift, scale, w_in, b_gate, cmp_pos, cmp_w1, cmp_b1, cmp_w2, cmp_b2, rel_bias, w_out):
    B, S, D = x.shape
    KV, R, DH = NSA_KV, NSA_R, NSA_DH
    w_g = jnp.zeros((D, KV, LANES), F32).at[:, :, :3 * R].set(w_in[:, D + 6 * KV_W:].reshape(D, KV, 3 * R))
    b_g = jnp.zeros((KV, LANES), F32).at[:, :3 * R].set(b_gate.reshape(KV, 3 * R).astype(F32))
    kv_cols = lambda i: w_in[:, D + i * KV_W:D + (i + 1) * KV_W]
    w_k = jnp.concatenate([kv_cols(0), kv_cols(2), kv_cols(4)], axis=1)
    w_v = jnp.concatenate([kv_cols(1), kv_cols(3), kv_cols(5)], axis=1)
    weights = [w_in[:, :D], w_k, w_v, w_g.reshape(D, KV * LANES)]
    q4, gv, kc, vc, ks, kw, vs_t, vw_t = _nsa_proj(x, g, shift, scale, [w.astype(BF16) for w in weights],
                                                   b_g.reshape(1, KV * LANES))
    grp = CMP_STRIDE
    xg = jnp.stack([kc, vc]).reshape(2, B, KV * S // grp, grp * DH)
    cmp = _compress(xg, cmp_pos, cmp_w1, cmp_b1, cmp_w2, cmp_b2).reshape(2, B, KV, S // grp, DH).astype(BF16)
    out = _nsa_t_attention(q4, gv, cmp[0], cmp[1].transpose(0, 1, 3, 2), ks, vs_t, kw, vw_t,
                           _nsa_t_tables(rel_bias, S))
    return [out], [w_out.astype(BF16)]


def kernel(x, c, rel_bias, ada_w, ada_b, norm_g, final_g,
           a_w_in, a_conv, a_b_i, a_b_f, a_head_g,
           s5_lam_re, s5_lam_im, s5_log_dt, s5_b_re, s5_b_im, s5_c_re, s5_c_im,
           s5_d, s5_glu_w, s5_glu_b, a_w_out,
           n_w_in, n_b_gate, n_cmp_pos, n_cmp_w1, n_cmp_b1, n_cmp_w2, n_cmp_b2, n_w_out,
           r_grp_w, r_grp_b, r_exp_w, r_exp_b, e_w_gate, e_w_up, e_w_down):
    B, S, D = x.shape
    mod = _ada_mod(c, ada_w, ada_b).reshape(DEPTH, 2, B, 1, 3 * D)
    split = lambda m: (m[..., :D], m[..., D:2 * D], m[..., 2 * D:])
    for layer in range(DEPTH):
        shift, scale, mix_gate = split(mod[layer, 0])
        j = layer // 2
        if layer % 2 == 0:
            s5_params = (s5_lam_re[j], s5_lam_im[j], s5_log_dt[j], s5_b_re[j], s5_b_im[j],
                         s5_c_re[j], s5_c_im[j], s5_d[j], s5_glu_w[j], s5_glu_b[j])
            acts, w_outs = _mlstm_s5_layer(x, norm_g[layer, 0], shift, scale, a_w_in[j], a_conv[j], a_b_i[j],
                                           a_b_f[j], a_head_g[j], s5_params, a_w_out[j])
        else:
            acts, w_outs = _nsa_layer(x, norm_g[layer, 0], shift, scale, n_w_in[j], n_b_gate[j], n_cmp_pos[j],
                                      n_cmp_w1[j], n_cmp_b1[j], n_cmp_w2[j], n_cmp_b2[j], rel_bias, n_w_out[j])
        shift, scale, gate = split(mod[layer, 1])
        x = _moe(x, mix_gate, acts, w_outs, norm_g[layer, 1], shift, scale, gate, r_grp_w[layer], r_grp_b[layer],
                 r_exp_w[layer], r_exp_b[layer], e_w_gate[layer], e_w_up[layer], e_w_down[layer], final_g,
                 final=(layer == DEPTH - 1))
    return x
```

```python
import functools
import math

import jax
import jax.numpy as jnp
from jax import lax
from jax.experimental import pallas as pl
from jax.experimental.pallas import tpu as pltpu

F32 = jnp.float32
BF16 = jnp.bfloat16
HIGHEST = lax.Precision.HIGHEST

D_MODEL = 1024
DEPTH = 2
MIX_A = 512
MLSTM_HEADS = 4
MLSTM_DH = MIX_A // MLSTM_HEADS
MLSTM_CHUNK = 128
CONV_K = 4
S5_GROUP = 16
S5_STATE = 64
S5_CHUNK = 16
NSA_HEADS = 16
NSA_KV = 4
NSA_R = NSA_HEADS // NSA_KV
NSA_DH = D_MODEL // NSA_HEADS
KV_W = NSA_KV * NSA_DH
CMP_BLOCK = 32
CMP_STRIDE = 16
CMP_HIDDEN = 256
SEL_BLOCK = 64
SEL_TOPK = 16
WINDOW = 512
FORCE = 1e9
REL_BUCKETS = 32
REL_MAX_DIST = 128
MOE_GROUPS = 4
MOE_PER_GROUP = 4
MOE_HIDDEN = 256
EPS = 1e-6
NEG = -1e30
BIG = 1e30
LOG2E = math.log2(math.e)

LANES = 128
SUBLANES = 8
ATT_TILE = 256
ATT_STEP_TILES = 2
MOE_WIN = 160
MOE_ALIGN = 16
MOE_TILES = 2
VMEM_LIMIT = 56 * 1024 * 1024


def _cparams(sem):
    return pltpu.CompilerParams(dimension_semantics=sem, vmem_limit_bytes=VMEM_LIMIT)


def _dot(a, b, precision=None):
    return jnp.dot(a, b, preferred_element_type=F32, precision=precision)


def _dot_nt(a, b):
    return lax.dot_general(a, b, (((1,), (1,)), ((), ())), preferred_element_type=F32)


def _sigmoid(x):
    return 1.0 / (1.0 + jnp.exp(-x))


def _silu(x):
    return x * _sigmoid(x)


def _gelu_tanh(x):
    return 0.5 * x * (1.0 + jnp.tanh(math.sqrt(2.0 / math.pi) * (x + 0.044715 * (x * x * x))))


def _modulated_norm(x, g, shift, scale):
    y = x * lax.rsqrt(jnp.mean(x * x, axis=-1, keepdims=True) + EPS) * g
    return y * (1.0 + scale) + shift


def _ada_kernel(c_ref, w_ref, b_ref, o_ref):
    c = c_ref[...]
    o_ref[0] = _dot(_silu(c), w_ref[0]) + b_ref[0]


def _ada_mod(c, ada_w, ada_b):
    B, D = c.shape
    n_mod = ada_w.shape[0] * ada_w.shape[1]
    w = ada_w.reshape(n_mod, D, 3 * D)
    b = ada_b.reshape(n_mod, 1, 3 * D)
    tn = 1024
    return pl.pallas_call(
        _ada_kernel,
        out_shape=jax.ShapeDtypeStruct((n_mod, B, 3 * D), F32),
        grid=(n_mod, 3 * D // tn),
        in_specs=[pl.BlockSpec((B, D), lambda i, j: (0, 0)),
                  pl.BlockSpec((1, D, tn), lambda i, j: (i, 0, j)),
                  pl.BlockSpec((1, 1, tn), lambda i, j: (i, 0, j))],
        out_specs=pl.BlockSpec((1, B, tn), lambda i, j: (i, 0, j)),
        compiler_params=_cparams(("parallel", "parallel")),
        name="ada_mod",
    )(c, w, b)


def _norm_mm_kernel(*refs, n_w):
    x_ref, g_ref, sh_ref, sc_ref = refs[:4]
    w_refs = refs[4:4 + n_w]
    o_refs = refs[4 + n_w:]
    h = _modulated_norm(x_ref[0], g_ref[...], sh_ref[0], sc_ref[0]).astype(BF16)
    for w_ref, o_ref in zip(w_refs, o_refs):
        o_ref[0] = _dot(h, w_ref[...]).astype(o_ref.dtype)


def _norm_matmul(x, g, shift, scale, weights, out_dtypes, tm=512):
    B, S, D = x.shape
    n_w = len(weights)
    vec = pl.BlockSpec((1, 1, D), lambda b, i: (b, 0, 0))
    in_specs = [pl.BlockSpec((1, tm, D), lambda b, i: (b, i, 0)),
                pl.BlockSpec((1, D), lambda b, i: (0, 0)), vec, vec]
    in_specs += [pl.BlockSpec(w.shape, lambda b, i: (0, 0)) for w in weights]
    return pl.pallas_call(
        functools.partial(_norm_mm_kernel, n_w=n_w),
        out_shape=[jax.ShapeDtypeStruct((B, S, w.shape[1]), dt) for w, dt in zip(weights, out_dtypes)],
        grid=(B, S // tm),
        in_specs=in_specs,
        out_specs=[pl.BlockSpec((1, tm, w.shape[1]), lambda b, i: (b, i, 0)) for w in weights],
        compiler_params=_cparams(("parallel", "parallel")),
        name="norm_matmul",
    )(x, g.reshape(1, D), shift, scale, *weights)


def _mlstm_kernel(qk_ref, vo_ref, if_ref, cw_ref, gb_ref, hg_ref, tril_ref, o_ref,
                  xbuf, c_scr, n_scr, m_scr):
    pad = SUBLANES

    @pl.when(pl.program_id(1) == 0)
    def _():
        xbuf[:, 0:pad, :] = jnp.zeros((xbuf.shape[0], pad, 2 * MIX_A), F32)
        c_scr[...] = jnp.zeros_like(c_scr)
        n_scr[...] = jnp.zeros_like(n_scr)
        m_scr[...] = jnp.zeros_like(m_scr)

    for bb in range(qk_ref.shape[0]):
        _mlstm_chunk(qk_ref.at[bb], vo_ref.at[bb], if_ref.at[bb], cw_ref, gb_ref, hg_ref, tril_ref, o_ref.at[bb],
                     xbuf.at[bb], c_scr.at[bb], n_scr.at[bb], m_scr.at[bb])


def _mlstm_chunk(qk_ref, vo_ref, if_ref, cw_ref, gb_ref, hg_ref, tril_ref, o_ref, xbuf, c_scr, n_scr, m_scr):
    L, H, DH = MLSTM_CHUNK, MLSTM_HEADS, MLSTM_DH
    pad = SUBLANES
    xbuf[pad:pad + L, :] = qk_ref[...].astype(F32)
    cw = cw_ref[...]
    conv = None
    for j in range(CONV_K):
        lo = pad - (CONV_K - 1) + j
        t = xbuf[lo:lo + L, :] * cw[j:j + 1, :]
        conv = t if conv is None else conv + t
    xbuf[0:pad, :] = xbuf[L:L + pad, :]
    qk = _silu(conv)
    q = qk[:, :MIX_A]
    k = qk[:, MIX_A:] * (DH ** -0.5)
    vo = vo_ref[...].astype(F32)
    v = vo[:, :MIX_A]
    o_pre = vo[:, MIX_A:]

    ifb = if_ref[...] + gb_ref[...]
    lf = jnp.minimum(ifb, 0.0) - jnp.log1p(jnp.exp(-jnp.abs(ifb)))
    bcs = _dot(tril_ref[...], lf, precision=HIGHEST)
    ifb_t = ifb.T
    bcs_t = bcs.T
    row = lax.broadcasted_iota(jnp.int32, (L, L), 0)
    col = lax.broadcasted_iota(jnp.int32, (L, L), 1)
    causal = col <= row

    outs = []
    for h in range(H):
        sl = slice(h * DH, (h + 1) * DH)
        qh, kh, vh = q[:, sl], k[:, sl], v[:, sl]
        qb, kb = qh.astype(BF16), kh.astype(BF16)
        b_col = bcs[:, H + h:H + h + 1]
        b_row = bcs_t[H + h:H + h + 1, :]
        li_col = ifb[:, h:h + 1]
        li_row = ifb_t[h:h + 1, :]
        b_last = b_col[L - 1:L, :]
        m0 = m_scr[h][:, 0:1]
        c0 = c_scr[h]
        n0 = n_scr[h]

        log_d = jnp.where(causal, b_col - b_row + li_row, NEG)
        log_inter = b_col + m0
        m_t = jnp.maximum(log_inter, jnp.max(log_d, axis=1, keepdims=True))
        dmat = jnp.exp(log_d - m_t)
        a_inter = jnp.exp(log_inter - m_t)
        s = _dot_nt(qb, kb) * dmat
        num = _dot(s.astype(BF16), vh.astype(BF16)) + a_inter * _dot_nt(qb, c0.astype(BF16))
        den = jnp.sum(s, axis=1, keepdims=True) + a_inter * jnp.sum(qh * n0, axis=1, keepdims=True)
        hh = num / jnp.maximum(jnp.abs(den), jnp.exp(-m_t))

        w_col = b_last - b_col + li_col
        m_loc = jnp.max(w_col, axis=0, keepdims=True)
        e = jnp.exp(w_col - m_loc)
        c_loc = _dot((vh * e).T.astype(BF16), kb)
        n_loc = jnp.sum(kh * e, axis=0, keepdims=True)
        m_new = jnp.maximum(b_last + m0, m_loc)
        a = jnp.exp(b_last + m0 - m_new)
        sc = jnp.exp(m_loc - m_new)
        c_scr[h] = a * c0 + sc * c_loc
        n_scr[h] = a * n0 + sc * n_loc
        m_scr[h] = jnp.broadcast_to(m_new, (1, LANES))

        outs.append(hh * lax.rsqrt(jnp.mean(hh * hh, axis=1, keepdims=True) + EPS))
    hm = jnp.concatenate(outs, axis=1)
    o_ref[...] = (_sigmoid(o_pre) * (hm * hg_ref[...])).astype(o_ref.dtype)


def _mlstm(qk, vo, ifg, conv_w, gate_bias, head_g):
    B, S, _ = qk.shape
    rows = 1
    L, H, DH = MLSTM_CHUNK, MLSTM_HEADS, MLSTM_DH
    tril = jnp.tril(jnp.ones((L, L), F32))
    return pl.pallas_call(
        _mlstm_kernel,
        out_shape=jax.ShapeDtypeStruct((B, S, MIX_A), BF16),
        grid=(B // rows, S // L),
        in_specs=[pl.BlockSpec((rows, L, 2 * MIX_A), lambda b, c: (b, c, 0)),
                  pl.BlockSpec((rows, L, 2 * MIX_A), lambda b, c: (b, c, 0)),
                  pl.BlockSpec((rows, L, LANES), lambda b, c: (b, c, 0)),
                  pl.BlockSpec((CONV_K, 2 * MIX_A), lambda b, c: (0, 0)),
                  pl.BlockSpec((1, LANES), lambda b, c: (0, 0)),
                  pl.BlockSpec((1, MIX_A), lambda b, c: (0, 0)),
                  pl.BlockSpec((L, L), lambda b, c: (0, 0))],
        out_specs=pl.BlockSpec((rows, L, MIX_A), lambda b, c: (b, c, 0)),
        scratch_shapes=[pltpu.VMEM((rows, L + SUBLANES, 2 * MIX_A), F32),
                        pltpu.VMEM((rows, H, DH, DH), F32),
                        pltpu.VMEM((rows, H, 1, DH), F32),
                        pltpu.VMEM((rows, H, 1, LANES), F32)],
        compiler_params=_cparams(("parallel", "arbitrary")),
        name="mlstm",
    )(qk, vo, ifg, conv_w, gate_bias, head_g, tril)


S5_LT = LANES // S5_GROUP
S5_PAIRS = S5_CHUNK // 2


def _s5s_kernel(u_ref, h_ref, e_ref, kk_ref, are_ref, aim_ref, d_ref, gw_ref, gb_ref, o_ref, xl_scr, x0_scr):
    n_chunks = u_ref.shape[1] // S5_CHUNK
    half = S5_LT * S5_STATE
    tok = lambda s: u_ref[0, pl.ds(s, n_chunks, stride=S5_CHUNK), :]
    u2 = [jnp.concatenate([tok(2 * q), tok(2 * q + 1)], axis=1) for q in range(S5_PAIRS)]
    u2b = [v.astype(BF16) for v in u2]
    xl_scr[...] = functools.reduce(lambda a, b: a + b, [_dot(u2b[q], h_ref[0, q]) for q in range(S5_PAIRS)])
    a_re = are_ref[0]
    a_im = aim_ref[0]

    def body(a, carry):
        re, im = carry
        x0_scr[pl.ds(a, 1), 0:half] = re
        x0_scr[pl.ds(a, 1), half:2 * half] = im
        return (a_re * re - a_im * im + xl_scr[pl.ds(a, 1), 0:half],
                a_re * im + a_im * re + xl_scr[pl.ds(a, 1), half:2 * half])

    zero = jnp.zeros((1, half), F32)
    lax.fori_loop(0, n_chunks, body, (zero, zero), unroll=8)
    x0 = x0_scr[...].astype(BF16)
    for p in range(S5_PAIRS):
        y = _dot(x0, e_ref[0, p]) + u2[p] * d_ref[0]
        for q in range(p + 1):
            y = y + _dot(u2b[q], kk_ref[0, p - q])
        ys = _gelu_tanh(y)
        out = ys * _sigmoid(_dot(ys.astype(BF16), gw_ref[0]) + gb_ref[0])
        o_ref[0, pl.ds(2 * p, n_chunks, stride=S5_CHUNK), :] = out[:, :LANES].astype(o_ref.dtype)
        o_ref[0, pl.ds(2 * p + 1, n_chunks, stride=S5_CHUNK), :] = out[:, LANES:].astype(o_ref.dtype)


def _s5s_tables(lam_re, lam_im, log_dt, b_re, b_im, c_re, c_im, d_skip, glu_w, glu_b):
    T, C, P, LT = S5_CHUNK, S5_GROUP, S5_STATE, S5_LT
    G = lam_re.shape[0]
    NT = G // LT
    lam = lax.complex(lam_re.astype(F32), lam_im.astype(F32))
    dt = jnp.exp(log_dt.astype(F32))[:, None]
    lam_bar = jnp.exp(lam * dt)
    b_bar = ((lam_bar - 1.0) / lam)[..., None] * lax.complex(b_re.astype(F32), b_im.astype(F32))
    c_mat = lax.complex(c_re.astype(F32), c_im.astype(F32))
    taus = jnp.arange(T + 1, dtype=F32)
    pw = jnp.exp((lam * dt)[:, None, :] * taus[None, :, None])
    eye = jnp.eye(LT, dtype=F32)
    tiles = lambda a: a.reshape((NT, LT) + a.shape[1:])

    kern = jnp.einsum('gcp,gtp,gpd->gtdc', c_mat, pw[:, :T], b_bar, precision=HIGHEST).real
    kblk = jnp.einsum('nitdc,ij->ntidjc', tiles(kern), eye).reshape(NT, T, LANES, LANES)
    kblk = jnp.concatenate([jnp.zeros_like(kblk[:, :1]), kblk], axis=1)
    kk = jnp.stack([jnp.concatenate([jnp.concatenate([kblk[:, 2 * d + 1], kblk[:, 2 * d + 2]], axis=2),
                                     jnp.concatenate([kblk[:, 2 * d], kblk[:, 2 * d + 1]], axis=2)], axis=1)
                    for d in range(T // 2)], axis=1)

    hmat = pw[:, :T][:, ::-1, :, None] * b_bar[:, None]

    def state_cols(m):
        return jnp.einsum('nispc,ij->nsicjp', tiles(m), eye).reshape(NT, T, LANES, LT * P)

    h = jnp.concatenate([state_cols(hmat.real), state_cols(hmat.imag)], axis=3)
    h2 = h.reshape(NT, T // 2, 2 * LANES, 2 * LT * P)

    emat = c_mat[:, None] * pw[:, 1:][:, :, None, :]

    def state_rows(m):
        return jnp.einsum('nitcp,ij->ntjpic', tiles(m), eye).reshape(NT, T, LT * P, LANES)

    e = jnp.concatenate([state_rows(emat.real), state_rows(-emat.imag)], axis=2)
    e2 = e.reshape(NT, T // 2, 2, 2 * LT * P, LANES).transpose(0, 1, 3, 2, 4).reshape(NT, T // 2, 2 * LT * P, 2 * LANES)

    a_re = pw[:, T].real.reshape(NT, 1, LT * P)
    a_im = pw[:, T].imag.reshape(NT, 1, LT * P)
    pair = lambda v: jnp.tile(v.astype(F32).reshape(NT, 1, LANES), (1, 1, 2))
    gwb = jnp.einsum('nice,ij->nicje', tiles(glu_w.astype(F32)), eye).reshape(NT, LANES, LANES)
    zeros = jnp.zeros_like(gwb)
    gw2 = jnp.concatenate([jnp.concatenate([gwb, zeros], axis=2), jnp.concatenate([zeros, gwb], axis=2)], axis=1)
    return (h2.astype(BF16), e2.astype(BF16), kk.astype(BF16), a_re, a_im, pair(d_skip), gw2.astype(BF16), pair(glu_b))


def _s5s(u, tables):
    B, S, W = u.shape
    NT = W // LANES
    n_chunks = S // S5_CHUNK
    per_tile = lambda a: pl.BlockSpec((1,) + a.shape[1:], lambda j, b: (j,) + (0,) * (a.ndim - 1))
    return pl.pallas_call(
        _s5s_kernel,
        out_shape=jax.ShapeDtypeStruct((B, S, W), F32),
        grid=(NT, B),
        in_specs=[pl.BlockSpec((1, S, LANES), lambda j, b: (b, 0, j))] + [per_tile(t) for t in tables],
        out_specs=pl.BlockSpec((1, S, LANES), lambda j, b: (b, 0, j)),
        scratch_shapes=[pltpu.VMEM((n_chunks, 2 * S5_LT * S5_STATE), F32) for _ in range(2)],
        compiler_params=_cparams(("parallel", "parallel")),
        name="s5",
    )(u, *tables)


def _compress_kernel(x_ref, plo_ref, phi_ref, w1_ref, b1_ref, w2_ref, b2_ref, o_ref):
    x = x_ref[0, 0]
    half = x.shape[1]
    w1 = w1_ref[0]
    lo = _dot((x + plo_ref[0]).astype(BF16), w1[:half])
    hi = _dot((x + phi_ref[0]).astype(BF16), w1[half:])
    rows = x.shape[0]
    hid = _gelu_tanh(lo + pltpu.roll(hi, rows - 1, 0) + b1_ref[0])
    o_ref[0, 0] = _dot(hid.astype(BF16), w2_ref[0]) + b2_ref[0]


def _compress(xg, pos, w1, b1, w2, b2):
    _, B, rows, width = xg.shape
    pos_flat = pos.reshape(2, 2, 1, width).astype(F32)
    sel = lambda shape: pl.BlockSpec((1,) + shape, lambda j, b: (j, 0, 0))
    return pl.pallas_call(
        _compress_kernel,
        out_shape=jax.ShapeDtypeStruct((2, B, rows, NSA_DH), F32),
        grid=(2, B),
        in_specs=[pl.BlockSpec((1, 1, rows, width), lambda j, b: (j, b, 0, 0)),
                  sel((1, width)), sel((1, width)),
                  sel((2 * width, CMP_HIDDEN)), sel((1, CMP_HIDDEN)),
                  sel((CMP_HIDDEN, NSA_DH)), sel((1, NSA_DH))],
        out_specs=pl.BlockSpec((1, 1, rows, NSA_DH), lambda j, b: (j, b, 0, 0)),
        compiler_params=_cparams(("parallel", "parallel")),
        name="nsa_compress",
    )(xg, pos_flat[:, 0], pos_flat[:, 1], w1.astype(BF16), b1[:, None].astype(F32),
      w2.astype(BF16), b2[:, None].astype(F32))


def _t5_bucket(dist):
    dist = jnp.maximum(dist, 0)
    max_exact = REL_BUCKETS // 2
    log_ratio = jnp.log(jnp.maximum(dist, 1).astype(F32) / max_exact) / math.log(REL_MAX_DIST / max_exact)
    large = jnp.minimum(max_exact + (log_ratio * (REL_BUCKETS - max_exact)).astype(jnp.int32), REL_BUCKETS - 1)
    return jnp.where(dist < max_exact, dist, large)


def _nsa_proj_kernel(x_ref, g_ref, sh_ref, sc_ref, wq_ref, wk_ref, wv_ref, wg_ref, bg_ref,
                     q4_ref, gv_ref, kc_ref, vc_ref, ks_ref, kw_ref, vst_ref, vwt_ref):
    KV, R, DH, T = NSA_KV, NSA_R, NSA_DH, ATT_TILE
    h = _modulated_norm(x_ref[0], g_ref[...], sh_ref[0], sc_ref[0]).astype(BF16)
    q_t = (_dot(h, wq_ref[...]) * (DH ** -0.5 * LOG2E)).T.astype(BF16)
    gates_t = _sigmoid(_dot(h, wg_ref[...]) + bg_ref[...]).T
    row = lax.broadcasted_iota(jnp.int32, (SUBLANES, R * T), 0)
    for g in range(KV):
        q4_ref[0, g, 0] = jnp.concatenate([q_t[(g * R + r) * DH:(g * R + r + 1) * DH] for r in range(R)], axis=1)
        gv = jnp.zeros((SUBLANES, R * T), F32)
        for j in range(3):
            gj = jnp.concatenate([gates_t[g * LANES + 3 * r + j:g * LANES + 3 * r + j + 1] for r in range(R)], axis=1)
            gv = jnp.where(row == j, gj, gv)
        gv_ref[0, g, 0] = gv
    k3 = _dot(h, wk_ref[...])
    v3 = _dot(h, wv_ref[...])
    vs_t = v3[:, KV_W:2 * KV_W].T.astype(BF16)
    vw_t = v3[:, 2 * KV_W:].T.astype(BF16)
    for g in range(KV):
        cols = slice(g * DH, (g + 1) * DH)
        kc_ref[0, g] = k3[:, cols].astype(BF16)
        vc_ref[0, g] = v3[:, cols].astype(BF16)
        ks_ref[0, g] = k3[:, KV_W + g * DH:KV_W + (g + 1) * DH].astype(BF16)
        kw_ref[0, g] = k3[:, 2 * KV_W + g * DH:2 * KV_W + (g + 1) * DH].astype(BF16)
        vst_ref[0, g, 0] = vs_t[cols]
        vwt_ref[0, g, 0] = vw_t[cols]


def _nsa_proj(x, g, shift, scale, weights, b_gate):
    B, S, D = x.shape
    KV, R, DH, T = NSA_KV, NSA_R, NSA_DH, ATT_TILE
    vec = pl.BlockSpec((1, 1, D), lambda b, i: (b, 0, 0))
    keys = pl.BlockSpec((1, KV, T, DH), lambda b, i: (b, 0, i, 0))
    key_shape = jax.ShapeDtypeStruct((B, KV, S, DH), BF16)
    tile = lambda rows, width: pl.BlockSpec((1, KV, 1, rows, width), lambda b, i: (b, 0, i, 0, 0))
    tile_shape = lambda rows, width, dt: jax.ShapeDtypeStruct((B, KV, S // T, rows, width), dt)
    return pl.pallas_call(
        _nsa_proj_kernel,
        out_shape=[tile_shape(DH, R * T, BF16), tile_shape(SUBLANES, R * T, F32),
                   key_shape, key_shape, key_shape, key_shape,
                   tile_shape(DH, T, BF16), tile_shape(DH, T, BF16)],
        grid=(B, S // T),
        in_specs=[pl.BlockSpec((1, T, D), lambda b, i: (b, i, 0)),
                  pl.BlockSpec((1, D), lambda b, i: (0, 0)), vec, vec]
                 + [pl.BlockSpec(w.shape, lambda b, i: (0, 0)) for w in weights]
                 + [pl.BlockSpec(b_gate.shape, lambda b, i: (0, 0))],
        out_specs=[tile(DH, R * T), tile(SUBLANES, R * T), keys, keys, keys, keys, tile(DH, T), tile(DH, T)],
        compiler_params=_cparams(("parallel", "parallel")),
        name="nsa_proj",
    )(x, g.reshape(1, D), shift, scale, *weights, b_gate)


def _nsa_t_kernel(q4_ref, gv_ref, *rest):
    o_ref = rest[11]
    for sub in range(ATT_STEP_TILES):
        _nsa_tile(pl.program_id(2) * ATT_STEP_TILES + sub, q4_ref.at[:, :, pl.ds(sub, 1)], gv_ref.at[:, :, pl.ds(sub, 1)],
                  *rest[:11], o_ref.at[:, pl.ds(sub * ATT_TILE, ATT_TILE), :], *rest[12:])


def _nsa_tile(qi, q4_ref, gv_ref, kc_ref, vct_ref, ks_ref, vst_ref, kw_ref, vwt_ref,
              cfar_ref, band_ref, selb_ref, winb_ref, ovt_ref, o_ref, s_scr, sel_scr, sbuf):
    T = ATT_TILE
    R, DH = NSA_R, NSA_DH
    q0 = qi * T
    n_pad = kc_ref.shape[2]
    n_sel = ovt_ref.shape[0]
    n_far = selb_ref.shape[0] - 1
    n_win = winb_ref.shape[0] - 2
    band_rows = band_ref.shape[2] - T // CMP_STRIDE * 2

    q4 = q4_ref[0, 0, 0]
    t_lane = q0 + lax.broadcasted_iota(jnp.int32, (1, R * T), 1) % T

    ones_rows = DH
    with_ones = lambda v_t: jnp.concatenate([v_t, jnp.ones((ones_rows, v_t.shape[1]), v_t.dtype)], axis=0)
    gvec = lambda j: gv_ref[0, 0, 0, j:j + 1, :]

    grp = T // CMP_STRIDE
    s_scr[0:n_pad, :] = _dot(kc_ref[0, 0], q4) + cfar_ref[0]
    s_scr[n_pad:n_pad + 2 * grp, :] = jnp.zeros((2 * grp, R * T), F32)
    r0 = jnp.maximum(qi * grp - 2 * grp, 0)
    x0 = r0 - (qi * grp - 2 * grp)
    r0 = pl.multiple_of(r0, SUBLANES)
    x0 = pl.multiple_of(x0, SUBLANES)
    s_scr[pl.ds(r0, band_rows), :] += band_ref[0, 0, pl.ds(x0, band_rows), :]
    lim = pl.multiple_of(qi * grp + 2 * grp, SUBLANES)
    s_scr[pl.ds(lim, n_pad), :] = jnp.full((n_pad, R * T), NEG, F32)

    w_subs, w_vals = [], []
    for d in range(n_win + 1):
        kt = jnp.maximum(qi - d, 0)
        off = pl.multiple_of(kt * T, T)
        tile = jnp.where(qi >= d, d, n_win + 1)
        w_subs.append((_dot(kw_ref[0, 0, pl.ds(off, T), :], q4) + winb_ref[tile, 0]).astype(BF16))
        w_vals.append(with_ones(vwt_ref[0, 0, kt]))

    s = s_scr[0:n_pad, :]
    e = jnp.exp2(s - jnp.max(s, axis=0, keepdims=True))
    inv = jnp.where(t_lane >= CMP_BLOCK - 1, 1.0 / jnp.sum(e, axis=0, keepdims=True), 0.0)
    p = e * inv
    o_cmp = _dot(vct_ref[0, 0], p.astype(BF16))
    psum = functools.reduce(lambda a, b: a + b, [p[:, r * T:(r + 1) * T] for r in range(R)])

    m_w = jnp.max(functools.reduce(jnp.maximum, w_subs), axis=0, keepdims=True)
    acc = functools.reduce(lambda a, b: a + b,
                           [_dot(vj, jnp.exp2(sj - m_w)) for sj, vj in zip(w_subs, w_vals)])
    o_win = acc[:DH] * (1.0 / acc[DH:DH + 1])
    out_t = gvec(0) * o_cmp + gvec(2) * o_win

    imp_t = _dot(ovt_ref[...], psum, precision=HIGHEST)
    jj = lax.broadcasted_iota(jnp.int32, (n_sel, T), 0)
    blk_t = (q0 + lax.broadcasted_iota(jnp.int32, (1, T), 1)) // SEL_BLOCK
    forced = (jj == 0) | (jj == blk_t) | (jj == blk_t - 1)
    score = jnp.where(forced, FORCE, jnp.where(jj <= blk_t, imp_t, -1.0))
    n_blk = n_sel // SUBLANES
    rows = [score[v * SUBLANES:(v + 1) * SUBLANES] for v in range(n_blk)]
    cnts = [jnp.zeros((SUBLANES, T), F32) for _ in range(n_blk)]
    sub = lax.broadcasted_iota(jnp.int32, (SUBLANES, T), 0)
    for j2 in range(n_sel):
        c2 = score[j2:j2 + 1, :]
        for v in range(n_blk):
            lo = v * SUBLANES
            if lo > j2:
                beats = c2 >= rows[v]
            elif lo + SUBLANES - 1 <= j2:
                beats = c2 > rows[v]
            else:
                beats = (c2 > rows[v]) | ((c2 >= rows[v]) & (sub > j2 - lo))
            cnts[v] = cnts[v] + jnp.where(beats, 1.0, 0.0)
    cnt = jnp.concatenate(cnts, axis=0)
    chosen = (cnt < float(min(SEL_TOPK, n_sel))) & (jj <= blk_t)
    sel_scr[...] = jnp.where(chosen, 0.0, -BIG)

    def block_mask(kt):
        per_tile = T // SEL_BLOCK
        parts = [jnp.broadcast_to(sel_scr[pl.ds(kt * per_tile + i, 1), :], (SEL_BLOCK, T)) for i in range(per_tile)]
        m1 = jnp.concatenate(parts, axis=0)
        return jnp.concatenate([m1] * R, axis=1)

    def sel_scores(slot, kc):
        off = pl.multiple_of(kc * T, T)
        s = _dot(ks_ref[0, 0, pl.ds(off, T), :], q4)
        s = (s + selb_ref[jnp.clip(qi - kc, 0, n_far), 0] + block_mask(kc)).astype(BF16)
        sbuf[slot] = s
        return jnp.max(s, axis=0, keepdims=True).astype(F32)

    def sel_weighted(slot, kc, m_new):
        return _dot(with_ones(vst_ref[0, 0, kc]), jnp.exp2(sbuf[slot] - m_new.astype(BF16)))

    last_tile = vst_ref.shape[2] - 1

    def sel_body(i, carry):
        m, acc, m_even = carry
        m_odd = sel_scores(1, 2 * i + 1)
        m_new = jnp.maximum(m, m_even)
        acc = jnp.exp2(m - m_new) * acc + sel_weighted(0, 2 * i, m_new)
        m_even = sel_scores(0, jnp.minimum(2 * i + 2, last_tile))
        m_fin = jnp.maximum(m_new, m_odd)
        acc = jnp.exp2(m_new - m_fin) * acc + sel_weighted(1, 2 * i + 1, m_fin)
        return m_fin, acc, m_even

    _, acc, _ = lax.fori_loop(0, qi // 2 + 1, sel_body,
                              (jnp.full((1, R * T), NEG, F32), jnp.zeros((DH + ones_rows, R * T), F32),
                               sel_scores(0, 0)))
    out_t = out_t + gvec(1) * (acc[:DH] * (1.0 / acc[DH:DH + 1]))
    for pr in range(R // 2):
        pair = jnp.concatenate([out_t[:, (2 * pr) * T:(2 * pr + 1) * T],
                                out_t[:, (2 * pr + 1) * T:(2 * pr + 2) * T]], axis=0)
        o_ref[0, :, pr * 2 * DH:(pr + 1) * 2 * DH] = pair.T.astype(o_ref.dtype)


def _bias_lookup(table, dist):
    onehot = (_t5_bucket(dist)[..., None] == jnp.arange(table.shape[0])).astype(F32)
    return jnp.einsum('...k,kh->...h', onehot, table, precision=HIGHEST)


def _nsa_t_tables(rel_bias, S):
    T, R, KV = ATT_TILE, NSA_R, NSA_KV
    table = rel_bias.astype(F32) * LOG2E
    ii = jnp.arange(T)
    delta = ii[None, :] - ii[:, None]

    def lanes(a):
        a = jnp.moveaxis(a, -1, 0)
        a = a.reshape((KV, R) + a.shape[1:])
        return jnp.moveaxis(a, 1, 2).reshape(KV, a.shape[2], R * a.shape[3])

    def tile(off):
        return lanes(_bias_lookup(table, off * T + delta))

    mask4 = lambda ok: jnp.tile(jnp.where(ok, 0.0, NEG), (1, R))[None]
    n_far = -(-REL_MAX_DIST // T) + 1
    selb = [tile(o) for o in range(n_far + 1)]
    selb[0] = selb[0] + mask4(delta >= 0)
    selb = jnp.stack(selb, axis=0)
    n_win = WINDOW // T
    winb = [tile(o) + mask4((o * T + delta >= 0) & (o * T + delta < WINDOW)) for o in range(n_win + 1)]
    winb.append(jnp.full_like(winb[0], NEG))
    winb = jnp.stack(winb, axis=0)

    grp = T // CMP_STRIDE
    far = _bias_lookup(table, jnp.asarray(2 * REL_MAX_DIST))
    xx = jnp.arange(4 * grp)
    bdist = ii[None, :] - CMP_STRIDE * (xx[:, None] - 2 * grp) - (CMP_BLOCK - 1)
    band = jnp.where((bdist >= 0)[..., None], _bias_lookup(table, bdist) - far, NEG)
    band = jnp.concatenate([lanes(band), jnp.zeros((KV, 2 * grp, R * T), F32)], axis=1)[:, None]
    cfar = jnp.repeat(far.reshape(KV, R), T, axis=1)[:, None]

    n_pad = S // CMP_STRIDE
    n_sel = S // SEL_BLOCK
    cmp_start = jnp.arange(n_pad) * CMP_STRIDE
    sel_start = jnp.arange(n_sel) * SEL_BLOCK
    overlap = jnp.clip(jnp.minimum(cmp_start[:, None] + CMP_BLOCK, sel_start[None] + SEL_BLOCK)
                       - jnp.maximum(cmp_start[:, None], sel_start[None]), 0).astype(F32) / CMP_BLOCK
    n_cmp = (S - CMP_BLOCK) // CMP_STRIDE + 1
    overlap_t = jnp.where((jnp.arange(n_pad) < n_cmp)[:, None], overlap, 0.0).T
    return cfar, band, selb, winb, overlap_t


def _nsa_t_attention(q4, gv, kcmp, vcmp_t, ks, vs_t, kw, vw_t, tables):
    B, KV, S, _ = kw.shape
    T = ATT_TILE
    cfar, band, selb, winb, overlap_t = tables
    gw = NSA_R * NSA_DH
    n_pad = kcmp.shape[2]
    seq = lambda a: pl.BlockSpec((1, 1) + a.shape[2:], lambda b, g, i: (b, g) + (0,) * (a.ndim - 2))
    qtile = lambda a: pl.BlockSpec((1, 1, ATT_STEP_TILES) + a.shape[3:], lambda b, g, i: (b, g, i, 0, 0))
    grp = lambda a: pl.BlockSpec((1,) + a.shape[1:], lambda b, g, i: (g,) + (0,) * (a.ndim - 1))
    tiles = lambda a: pl.BlockSpec((a.shape[0], 1) + a.shape[2:], lambda b, g, i: (0, g, 0, 0))
    full = lambda a: pl.BlockSpec(a.shape, lambda b, g, i: (0,) * a.ndim)
    return pl.pallas_call(
        _nsa_t_kernel,
        out_shape=jax.ShapeDtypeStruct((B, S, KV * gw), BF16),
        grid=(B, KV, S // (T * ATT_STEP_TILES)),
        in_specs=[qtile(q4), qtile(gv),
                  seq(kcmp), seq(vcmp_t), seq(ks), seq(vs_t), seq(kw), seq(vw_t),
                  grp(cfar), grp(band), tiles(selb), tiles(winb), full(overlap_t)],
        out_specs=pl.BlockSpec((1, T * ATT_STEP_TILES, gw), lambda b, g, i: (b, i, g)),
        scratch_shapes=[pltpu.VMEM((2 * n_pad + 2 * (T // CMP_STRIDE), NSA_R * T), F32),
                        pltpu.VMEM((S // SEL_BLOCK, T), F32),
                        pltpu.VMEM((2, T, NSA_R * T), BF16)],
        compiler_params=_cparams(("parallel", "parallel", "arbitrary")),
        name="nsa_attention",
    )(q4, gv, kcmp, vcmp_t, ks, vs_t, kw, vw_t, cfar, band, selb, winb, overlap_t)


def _moe_kernel(*refs, n_in, final):
    x_ref, mgate_ref = refs[:2]
    a_refs = refs[2:2 + n_in]
    wo_refs = refs[2 + n_in:2 + 2 * n_in]
    (g_ref, sh_ref, sc_ref, gate_ref, wr_ref, br_ref, before_ref, wg_ref, wu_ref, wd_ref, fg_ref,
     o_ref, x_all, hs_all, rts_all, acc_all, perm_t_all, meta_all) = refs[2 + 2 * n_in:]
    NG, PG, FH = MOE_GROUPS, MOE_PER_GROUP, MOE_HIDDEN
    TP = x_all.shape[0]
    s = pl.program_id(2)

    def prologue(hh):
        x_scr, hs_scr, rts_scr, acc_scr, perm_t_scr = (r.at[hh] for r in (x_all, hs_all, rts_all, acc_all, perm_t_all))
        meta = meta_all.at[hh]
        mix = functools.reduce(lambda a, b: a + b,
                               [_dot(a_ref[0].astype(BF16), wo_ref[...]) for a_ref, wo_ref in zip(a_refs, wo_refs)])
        x = x_ref[0] + mgate_ref[0] * mix
        x_scr[...] = x
        h = _modulated_norm(x, g_ref[...], sh_ref[0], sc_ref[0])
        h_hi = h.astype(BF16)
        h_lo = (h - h_hi.astype(F32)).astype(BF16)
        both = _dot(h_hi, wr_ref[...])
        logits = (both[:, :LANES] + both[:, LANES:] + _dot(h_lo, wr_ref[:, :LANES]) + br_ref[...]).T
        gl = [logits[NG * PG + g:NG * PG + g + 1, :] for g in range(NG)]
        gmax = functools.reduce(jnp.maximum, gl)
        gtop = jnp.full_like(gmax, float(NG - 1))
        for g in reversed(range(NG - 1)):
            gtop = jnp.where(gl[g] == gmax, float(g), gtop)
        p_g = 1.0 / functools.reduce(lambda a, b: a + b, [jnp.exp(v - gmax) for v in gl])
        a = []
        for j in range(PG):
            v = logits[(NG - 1) * PG + j:(NG - 1) * PG + j + 1, :]
            for g in reversed(range(NG - 1)):
                v = jnp.where(gtop == float(g), logits[g * PG + j:g * PG + j + 1, :], v)
            a.append(v)

        def first_max(vals):
            vmax = functools.reduce(jnp.maximum, vals)
            taken = jnp.zeros_like(vmax) > 1.0
            hits = []
            for v in vals:
                hit = (v == vmax) & jnp.logical_not(taken)
                taken = taken | hit
                hits.append(hit)
            return vmax, hits

        v1, hit1 = first_max(a)
        rest = [jnp.where(hh, -jnp.inf, v) for hh, v in zip(hit1, a)]
        v2, hit2 = first_max(rest)
        e2 = jnp.exp(v2 - v1)
        w1 = p_g / (1.0 + e2)
        w2 = p_g * e2 / (1.0 + e2)
        tm = gtop.shape[1]
        row = lax.broadcasted_iota(jnp.int32, (SUBLANES, tm), 0)
        onehot = [jnp.where(gtop == float(g), 1.0, 0.0) for g in range(NG)]
        oh8 = jnp.zeros((SUBLANES, tm), F32)
        for g in range(NG):
            oh8 = jnp.where(row == g, onehot[g], oh8)
        before = _dot(oh8.astype(BF16), before_ref[...])
        pos = jnp.zeros_like(gtop)
        off = jnp.int32(0)
        for g in range(NG):
            cnt = jnp.sum(onehot[g]).astype(jnp.int32)
            meta[g] = off
            meta[NG + g] = cnt
            pos = pos + onehot[g] * (before[g:g + 1, :] + off.astype(F32))
            off = off + cnt
        rt = jnp.where(row == PG, gtop, jnp.where(row == PG + 1, pos, 0.0))
        for j in range(PG):
            wj = jnp.where(hit1[j], w1, jnp.where(hit2[j], w2, 0.0))
            rt = jnp.where(row == j, wj, rt)
        rt_tok = jnp.concatenate([rt, jnp.zeros((LANES - SUBLANES, tm), F32)], axis=0).T
        rid = lax.broadcasted_iota(jnp.int32, (tm, tm), 0).astype(F32)
        cid = lax.broadcasted_iota(jnp.int32, (tm, tm), 1).astype(F32)
        perm = jnp.where(rid == pos, 1.0, 0.0).astype(BF16)
        perm_t = jnp.where(rt_tok[:, PG + 1:PG + 2] == cid, 1.0, 0.0).astype(BF16)
        perm_t_scr[...] = perm_t
        pad = hs_scr.shape[0] - tm
        hs_scr[0:tm, :] = _dot(perm, h_hi).astype(BF16)
        hs_scr[tm:, :] = jnp.zeros((pad, h_hi.shape[1]), BF16)
        r1 = rt.astype(BF16)
        res = rt - r1.astype(F32)
        r2 = res.astype(BF16)
        r3 = (res - r2.astype(F32)).astype(BF16)
        rt_sorted = _dot(r1, perm_t) + _dot(r2, perm_t) + _dot(r3, perm_t)
        rts_scr[0:tm, :] = jnp.concatenate([rt_sorted, jnp.zeros((LANES - SUBLANES, tm), F32)], axis=0).T
        rts_scr[tm:, :] = jnp.full((pad, LANES), -1.0, F32)
        acc_scr[...] = jnp.zeros(acc_scr.shape, F32)

    tm = x_all.shape[1]
    WIN = hs_all.shape[1] - tm

    def experts(c):
        cf = c.astype(F32)
        bases, counts = [], []
        for hh in range(TP):
            off = meta_all[hh, c]
            cnt = meta_all[hh, NG + c]
            base = (off // MOE_ALIGN) * MOE_ALIGN
            bases.append(base)
            counts.append(jnp.where(cnt > 0, (off + cnt - base + WIN - 1) // WIN, 0))

        def win_body(w, carry):
            starts = [pl.multiple_of(jnp.where(w < counts[hh], bases[hh] + w * WIN, tm), MOE_ALIGN) for hh in range(TP)]
            hs = jnp.concatenate([hs_all[hh, pl.ds(starts[hh], WIN), :] for hh in range(TP)], axis=0)
            rt = jnp.concatenate([rts_all[hh, pl.ds(starts[hh], WIN), :] for hh in range(TP)], axis=0)
            in_group = rt[:, PG:PG + 1] == cf
            hid = _silu(_dot(hs, wg_ref[0])) * _dot(hs, wu_ref[0])
            parts = [hid[:, j * FH:(j + 1) * FH] * jnp.where(in_group, rt[:, j:j + 1], 0.0) for j in range(PG)]
            out = _dot(jnp.concatenate(parts, axis=1).astype(BF16), wd_ref[0])
            for hh in range(TP):
                acc_all[hh, pl.ds(starts[hh], WIN), :] += out[hh * WIN:(hh + 1) * WIN]
            return carry

        lax.fori_loop(0, functools.reduce(jnp.maximum, counts), win_body, 0)

    def epilogue(hh):
        ys = acc_all[hh, 0:tm, :]
        ys_hi = ys.astype(BF16)
        ys_lo = (ys - ys_hi.astype(F32)).astype(BF16)
        back = _dot(perm_t_all[hh], jnp.concatenate([ys_hi, ys_lo], axis=1))
        d = ys.shape[1]
        y = x_all[hh] + gate_ref[0] * (back[:, :d] + back[:, d:])
        if final:
            y = y * lax.rsqrt(jnp.mean(y * y, axis=-1, keepdims=True) + EPS) * fg_ref[...]
        o_ref[0, hh * tm:(hh + 1) * tm, :] = y

    for hh in range(TP):
        pl.when(s == hh)(functools.partial(prologue, hh))
    pl.when((s >= TP - 1) & (s <= TP + NG - 2))(lambda: experts(s - (TP - 1)))
    for hh in range(TP):
        pl.when(s == NG + TP - 2)(functools.partial(epilogue, hh))


def _moe(x, mix_gate, acts, w_outs, g, shift, scale, gate, wg, bg, we, be, w_gate, w_up, w_down, final_g, final,
         tm=512):
    B, S, D = x.shape
    n_in = len(acts)
    NG, PG, FH = MOE_GROUPS, MOE_PER_GROUP, MOE_HIDDEN
    wr = jnp.zeros((D, LANES), F32)
    wr = wr.at[:, :NG * PG].set(we.reshape(D, NG * PG).astype(F32)).at[:, NG * PG:NG * PG + NG].set(wg.astype(F32))
    br = jnp.zeros((1, LANES), F32)
    br = br.at[0, :NG * PG].set(be.reshape(NG * PG).astype(F32)).at[0, NG * PG:NG * PG + NG].set(bg.astype(F32))
    wr_hi = wr.astype(BF16)
    wr = jnp.concatenate([wr_hi, (wr - wr_hi.astype(F32)).astype(BF16)], axis=1)
    grp = lambda w: w.reshape(NG, PG, D, FH).transpose(0, 2, 1, 3).reshape(NG, D, PG * FH).astype(BF16)
    wd = w_down.reshape(NG, PG * FH, D).astype(BF16)
    ids = jnp.arange(tm)
    before = (ids[:, None] < ids[None, :]).astype(BF16)
    TP = MOE_TILES
    n_steps = NG + TP - 1
    vec = pl.BlockSpec((1, 1, D), lambda b, i, s: (b, 0, 0))
    row = pl.BlockSpec((1, D), lambda b, i, s: (0, 0))
    wspec = lambda k, n: pl.BlockSpec((1, k, n), lambda b, i, s: (jnp.clip(s - (TP - 1), 0, NG - 1), 0, 0))
    tokens_in = lambda n: pl.BlockSpec((1, tm, n), lambda b, i, s: (b, i * TP + jnp.minimum(s, TP - 1), 0))
    tokens_out = pl.BlockSpec((1, TP * tm, D), lambda b, i, s: (b, i, 0))
    const = lambda a: pl.BlockSpec(a.shape, lambda b, i, s: (0,) * a.ndim)
    return pl.pallas_call(
        functools.partial(_moe_kernel, n_in=n_in, final=final),
        out_shape=jax.ShapeDtypeStruct((B, S, D), F32),
        grid=(B, S // (tm * TP), n_steps),
        in_specs=[tokens_in(D), vec] + [tokens_in(a.shape[2]) for a in acts] + [const(w) for w in w_outs]
                 + [row, vec, vec, vec, const(wr), const(br), const(before),
                    wspec(D, PG * FH), wspec(D, PG * FH), wspec(PG * FH, D), row],
        out_specs=tokens_out,
        scratch_shapes=[pltpu.VMEM((TP, tm, D), F32), pltpu.VMEM((TP, tm + MOE_WIN, D), BF16),
                        pltpu.VMEM((TP, tm + MOE_WIN, LANES), F32), pltpu.VMEM((TP, tm + MOE_WIN, D), F32),
                        pltpu.VMEM((TP, tm, tm), BF16), pltpu.SMEM((TP, 2 * NG), jnp.int32)],
        compiler_params=_cparams(("parallel", "parallel", "arbitrary")),
        name="moe",
    )(x, mix_gate, *acts, *w_outs, g.reshape(1, D), shift, scale, gate, wr, br, before, grp(w_gate), grp(w_up), wd,
      final_g.reshape(1, D))


def _mlstm_s5_layer(x, g, shift, scale, w_in, conv_w, b_i, b_f, head_g, s5_params, w_out):
    H = MLSTM_HEADS
    A = MIX_A
    w_if = jnp.zeros((D_MODEL, LANES), F32).at[:, :2 * H].set(w_in[:, 4 * A:4 * A + 2 * H])
    weights = [w_in[:, :2 * A], w_in[:, 2 * A:4 * A], w_if, w_in[:, 4 * A + 2 * H:]]
    qk, vo, ifg, u = _norm_matmul(x, g, shift, scale, [w.astype(BF16) for w in weights], [BF16, BF16, F32, F32])
    gate_bias = jnp.zeros((1, LANES), F32).at[0, :H].set(b_i.astype(F32)).at[0, H:2 * H].set(b_f.astype(F32))
    hm = _mlstm(qk, vo, ifg, conv_w.astype(F32), gate_bias, head_g.reshape(1, A).astype(F32))
    ys = _s5s(u, _s5s_tables(*s5_params))
    w_out = w_out.astype(BF16)
    return [hm, ys], [w_out[:A], w_out[A:]]


def _nsa_layer(x, g, shift, scale, w_in, b_gate, cmp_pos, cmp_w1, cmp_b1, cmp_w2, cmp_b2, rel_bias, w_out):
    B, S, D = x.shape
    KV, R, DH = NSA_KV, NSA_R, NSA_DH
    w_g = jnp.zeros((D, KV, LANES), F32).at[:, :, :3 * R].set(w_in[:, D + 6 * KV_W:].reshape(D, KV, 3 * R))
    b_g = jnp.zeros((KV, LANES), F32).at[:, :3 * R].set(b_gate.reshape(KV, 3 * R).astype(F32))
    kv_cols = lambda i: w_in[:, D + i * KV_W:D + (i + 1) * KV_W]
    w_k = jnp.concatenate([kv_cols(0), kv_cols(2), kv_cols(4)], axis=1)
    w_v = jnp.concatenate([kv_cols(1), kv_cols(3), kv_cols(5)], axis=1)
    weights = [w_in[:, :D], w_k, w_v, w_g.reshape(D, KV * LANES)]
    q4, gv, kc, vc, ks, kw, vs_t, vw_t = _nsa_proj(x, g, shift, scale, [w.astype(BF16) for w in weights],
                                                   b_g.reshape(1, KV * LANES))
    grp = CMP_STRIDE
    xg = jnp.stack([kc, vc]).reshape(2, B, KV * S // grp, grp * DH)
    cmp = _compress(xg, cmp_pos, cmp_w1, cmp_b1, cmp_w2, cmp_b2).reshape(2, B, KV, S // grp, DH).astype(BF16)
    out = _nsa_t_attention(q4, gv, cmp[0], cmp[1].transpose(0, 1, 3, 2), ks, vs_t, kw, vw_t,
                           _nsa_t_tables(rel_bias, S))
    return [out], [w_out.astype(BF16)]


def kernel(x, c, rel_bias, ada_w, ada_b, norm_g, final_g,
           a_w_in, a_conv, a_b_i, a_b_f, a_head_g,
           s5_lam_re, s5_lam_im, s5_log_dt, s5_b_re, s5_b_im, s5_c_re, s5_c_im,
           s5_d, s5_glu_w, s5_glu_b, a_w_out,
           n_w_in, n_b_gate, n_cmp_pos, n_cmp_w1, n_cmp_b1, n_cmp_w2, n_cmp_b2, n_w_out,
           r_grp_w, r_grp_b, r_exp_w, r_exp_b, e_w_gate, e_w_up, e_w_down):
    B, S, D = x.shape
    mod = _ada_mod(c, ada_w, ada_b).reshape(DEPTH, 2, B, 1, 3 * D)
    split = lambda m: (m[..., :D], m[..., D:2 * D], m[..., 2 * D:])
    for layer in range(DEPTH):
        shift, scale, mix_gate = split(mod[layer, 0])
        j = layer // 2
        if layer % 2 == 0:
            s5_params = (s5_lam_re[j], s5_lam_im[j], s5_log_dt[j], s5_b_re[j], s5_b_im[j],
                         s5_c_re[j], s5_c_im[j], s5_d[j], s5_glu_w[j], s5_glu_b[j])
            acts, w_outs = _mlstm_s5_layer(x, norm_g[layer, 0], shift, scale, a_w_in[j], a_conv[j], a_b_i[j],
                                           a_b_f[j], a_head_g[j], s5_params, a_w_out[j])
        else:
            acts, w_outs = _nsa_layer(x, norm_g[layer, 0], shift, scale, n_w_in[j], n_b_gate[j], n_cmp_pos[j],
                                      n_cmp_w1[j], n_cmp_b1[j], n_cmp_w2[j], n_cmp_b2[j], rel_bias, n_w_out[j])
        shift, scale, gate = split(mod[layer, 1])
        x = _moe(x, mix_gate, acts, w_outs, norm_g[layer, 1], shift, scale, gate, r_grp_w[layer], r_grp_b[layer],
                 r_exp_w[layer], r_exp_b[layer], e_w_gate[layer], e_w_up[layer], e_w_down[layer], final_g,
                 final=(layer == DEPTH - 1))
    return x
```

```python
import functools
import math

import jax
import jax.numpy as jnp
from jax import lax
from jax.experimental import pallas as pl
from jax.experimental.pallas import tpu as pltpu

F32 = jnp.float32
BF16 = jnp.bfloat16
HIGHEST = lax.Precision.HIGHEST

D_MODEL = 1024
DEPTH = 2
MIX_A = 512
MLSTM_HEADS = 4
MLSTM_DH = MIX_A // MLSTM_HEADS
MLSTM_CHUNK = 128
CONV_K = 4
S5_GROUP = 16
S5_STATE = 64
S5_CHUNK = 16
NSA_HEADS = 16
NSA_KV = 4
NSA_R = NSA_HEADS // NSA_KV
NSA_DH = D_MODEL // NSA_HEADS
KV_W = NSA_KV * NSA_DH
CMP_BLOCK = 32
CMP_STRIDE = 16
CMP_HIDDEN = 256
SEL_BLOCK = 64
SEL_TOPK = 16
WINDOW = 512
FORCE = 1e9
REL_BUCKETS = 32
REL_MAX_DIST = 128
MOE_GROUPS = 4
MOE_PER_GROUP = 4
MOE_HIDDEN = 256
EPS = 1e-6
NEG = -1e30
BIG = 1e30
LOG2E = math.log2(math.e)

LANES = 128
SUBLANES = 8
ATT_TILE = 256
ATT_STEP_TILES = 4
MOE_WIN = 160
MOE_ALIGN = 16
MOE_TILES = 2
VMEM_LIMIT = 56 * 1024 * 1024


def _cparams(sem):
    return pltpu.CompilerParams(dimension_semantics=sem, vmem_limit_bytes=VMEM_LIMIT)


def _dot(a, b, precision=None):
    return jnp.dot(a, b, preferred_element_type=F32, precision=precision)


def _dot_nt(a, b):
    return lax.dot_general(a, b, (((1,), (1,)), ((), ())), preferred_element_type=F32)


def _sigmoid(x):
    return 1.0 / (1.0 + jnp.exp(-x))


def _silu(x):
    return x * _sigmoid(x)


def _gelu_tanh(x):
    return 0.5 * x * (1.0 + jnp.tanh(math.sqrt(2.0 / math.pi) * (x + 0.044715 * (x * x * x))))


def _modulated_norm(x, g, shift, scale):
    y = x * lax.rsqrt(jnp.mean(x * x, axis=-1, keepdims=True) + EPS) * g
    return y * (1.0 + scale) + shift


def _ada_kernel(c_ref, w_ref, b_ref, o_ref):
    c = c_ref[...]
    o_ref[0] = _dot(_silu(c), w_ref[0]) + b_ref[0]


def _ada_mod(c, ada_w, ada_b):
    B, D = c.shape
    n_mod = ada_w.shape[0] * ada_w.shape[1]
    w = ada_w.reshape(n_mod, D, 3 * D)
    b = ada_b.reshape(n_mod, 1, 3 * D)
    tn = 1024
    return pl.pallas_call(
        _ada_kernel,
        out_shape=jax.ShapeDtypeStruct((n_mod, B, 3 * D), F32),
        grid=(n_mod, 3 * D // tn),
        in_specs=[pl.BlockSpec((B, D), lambda i, j: (0, 0)),
                  pl.BlockSpec((1, D, tn), lambda i, j: (i, 0, j)),
                  pl.BlockSpec((1, 1, tn), lambda i, j: (i, 0, j))],
        out_specs=pl.BlockSpec((1, B, tn), lambda i, j: (i, 0, j)),
        compiler_params=_cparams(("parallel", "parallel")),
        name="ada_mod",
    )(c, w, b)


def _norm_mm_kernel(*refs, n_w):
    x_ref, g_ref, sh_ref, sc_ref = refs[:4]
    w_refs = refs[4:4 + n_w]
    o_refs = refs[4 + n_w:]
    h = _modulated_norm(x_ref[0], g_ref[...], sh_ref[0], sc_ref[0]).astype(BF16)
    for w_ref, o_ref in zip(w_refs, o_refs):
        o_ref[0] = _dot(h, w_ref[...]).astype(o_ref.dtype)


def _norm_matmul(x, g, shift, scale, weights, out_dtypes, tm=512):
    B, S, D = x.shape
    n_w = len(weights)
    vec = pl.BlockSpec((1, 1, D), lambda b, i: (b, 0, 0))
    in_specs = [pl.BlockSpec((1, tm, D), lambda b, i: (b, i, 0)),
                pl.BlockSpec((1, D), lambda b, i: (0, 0)), vec, vec]
    in_specs += [pl.BlockSpec(w.shape, lambda b, i: (0, 0)) for w in weights]
    return pl.pallas_call(
        functools.partial(_norm_mm_kernel, n_w=n_w),
        out_shape=[jax.ShapeDtypeStruct((B, S, w.shape[1]), dt) for w, dt in zip(weights, out_dtypes)],
        grid=(B, S // tm),
        in_specs=in_specs,
        out_specs=[pl.BlockSpec((1, tm, w.shape[1]), lambda b, i: (b, i, 0)) for w in weights],
        compiler_params=_cparams(("parallel", "parallel")),
        name="norm_matmul",
    )(x, g.reshape(1, D), shift, scale, *weights)


def _mlstm_kernel(qk_ref, vo_ref, if_ref, cw_ref, gb_ref, hg_ref, tril_ref, o_ref,
                  xbuf, c_scr, n_scr, m_scr):
    pad = SUBLANES

    @pl.when(pl.program_id(1) == 0)
    def _():
        xbuf[:, 0:pad, :] = jnp.zeros((xbuf.shape[0], pad, 2 * MIX_A), F32)
        c_scr[...] = jnp.zeros_like(c_scr)
        n_scr[...] = jnp.zeros_like(n_scr)
        m_scr[...] = jnp.zeros_like(m_scr)

    for bb in range(qk_ref.shape[0]):
        _mlstm_chunk(qk_ref.at[bb], vo_ref.at[bb], if_ref.at[bb], cw_ref, gb_ref, hg_ref, tril_ref, o_ref.at[bb],
                     xbuf.at[bb], c_scr.at[bb], n_scr.at[bb], m_scr.at[bb])


def _mlstm_chunk(qk_ref, vo_ref, if_ref, cw_ref, gb_ref, hg_ref, tril_ref, o_ref, xbuf, c_scr, n_scr, m_scr):
    L, H, DH = MLSTM_CHUNK, MLSTM_HEADS, MLSTM_DH
    pad = SUBLANES
    xbuf[pad:pad + L, :] = qk_ref[...].astype(F32)
    cw = cw_ref[...]
    conv = None
    for j in range(CONV_K):
        lo = pad - (CONV_K - 1) + j
        t = xbuf[lo:lo + L, :] * cw[j:j + 1, :]
        conv = t if conv is None else conv + t
    xbuf[0:pad, :] = xbuf[L:L + pad, :]
    qk = _silu(conv)
    q = qk[:, :MIX_A]
    k = qk[:, MIX_A:] * (DH ** -0.5)
    vo = vo_ref[...].astype(F32)
    v = vo[:, :MIX_A]
    o_pre = vo[:, MIX_A:]

    ifb = if_ref[...] + gb_ref[...]
    lf = jnp.minimum(ifb, 0.0) - jnp.log1p(jnp.exp(-jnp.abs(ifb)))
    bcs = _dot(tril_ref[...], lf, precision=HIGHEST)
    ifb_t = ifb.T
    bcs_t = bcs.T
    row = lax.broadcasted_iota(jnp.int32, (L, L), 0)
    col = lax.broadcasted_iota(jnp.int32, (L, L), 1)
    causal = col <= row

    outs = []
    for h in range(H):
        sl = slice(h * DH, (h + 1) * DH)
        qh, kh, vh = q[:, sl], k[:, sl], v[:, sl]
        qb, kb = qh.astype(BF16), kh.astype(BF16)
        b_col = bcs[:, H + h:H + h + 1]
        b_row = bcs_t[H + h:H + h + 1, :]
        li_col = ifb[:, h:h + 1]
        li_row = ifb_t[h:h + 1, :]
        b_last = b_col[L - 1:L, :]
        m0 = m_scr[h][:, 0:1]
        c0 = c_scr[h]
        n0 = n_scr[h]

        log_d = jnp.where(causal, b_col - b_row + li_row, NEG)
        log_inter = b_col + m0
        m_t = jnp.maximum(log_inter, jnp.max(log_d, axis=1, keepdims=True))
        dmat = jnp.exp(log_d - m_t)
        a_inter = jnp.exp(log_inter - m_t)
        s = _dot_nt(qb, kb) * dmat
        num = _dot(s.astype(BF16), vh.astype(BF16)) + a_inter * _dot_nt(qb, c0.astype(BF16))
        den = jnp.sum(s, axis=1, keepdims=True) + a_inter * jnp.sum(qh * n0, axis=1, keepdims=True)
        hh = num / jnp.maximum(jnp.abs(den), jnp.exp(-m_t))

        w_col = b_last - b_col + li_col
        m_loc = jnp.max(w_col, axis=0, keepdims=True)
        e = jnp.exp(w_col - m_loc)
        c_loc = _dot((vh * e).T.astype(BF16), kb)
        n_loc = jnp.sum(kh * e, axis=0, keepdims=True)
        m_new = jnp.maximum(b_last + m0, m_loc)
        a = jnp.exp(b_last + m0 - m_new)
        sc = jnp.exp(m_loc - m_new)
        c_scr[h] = a * c0 + sc * c_loc
        n_scr[h] = a * n0 + sc * n_loc
        m_scr[h] = jnp.broadcast_to(m_new, (1, LANES))

        outs.append(hh * lax.rsqrt(jnp.mean(hh * hh, axis=1, keepdims=True) + EPS))
    hm = jnp.concatenate(outs, axis=1)
    o_ref[...] = (_sigmoid(o_pre) * (hm * hg_ref[...])).astype(o_ref.dtype)


def _mlstm(qk, vo, ifg, conv_w, gate_bias, head_g):
    B, S, _ = qk.shape
    rows = 1
    L, H, DH = MLSTM_CHUNK, MLSTM_HEADS, MLSTM_DH
    tril = jnp.tril(jnp.ones((L, L), F32))
    return pl.pallas_call(
        _mlstm_kernel,
        out_shape=jax.ShapeDtypeStruct((B, S, MIX_A), BF16),
        grid=(B // rows, S // L),
        in_specs=[pl.BlockSpec((rows, L, 2 * MIX_A), lambda b, c: (b, c, 0)),
                  pl.BlockSpec((rows, L, 2 * MIX_A), lambda b, c: (b, c, 0)),
                  pl.BlockSpec((rows, L, LANES), lambda b, c: (b, c, 0)),
                  pl.BlockSpec((CONV_K, 2 * MIX_A), lambda b, c: (0, 0)),
                  pl.BlockSpec((1, LANES), lambda b, c: (0, 0)),
                  pl.BlockSpec((1, MIX_A), lambda b, c: (0, 0)),
                  pl.BlockSpec((L, L), lambda b, c: (0, 0))],
        out_specs=pl.BlockSpec((rows, L, MIX_A), lambda b, c: (b, c, 0)),
        scratch_shapes=[pltpu.VMEM((rows, L + SUBLANES, 2 * MIX_A), F32),
                        pltpu.VMEM((rows, H, DH, DH), F32),
                        pltpu.VMEM((rows, H, 1, DH), F32),
                        pltpu.VMEM((rows, H, 1, LANES), F32)],
        compiler_params=_cparams(("parallel", "arbitrary")),
        name="mlstm",
    )(qk, vo, ifg, conv_w, gate_bias, head_g, tril)


S5_LT = LANES // S5_GROUP
S5_PAIRS = S5_CHUNK // 2


def _s5s_kernel(u_ref, h_ref, e_ref, kk_ref, are_ref, aim_ref, d_ref, gw_ref, gb_ref, o_ref, xl_scr, x0_scr):
    n_chunks = u_ref.shape[1] // S5_CHUNK
    half = S5_LT * S5_STATE
    tok = lambda s: u_ref[0, pl.ds(s, n_chunks, stride=S5_CHUNK), :]
    u2 = [jnp.concatenate([tok(2 * q), tok(2 * q + 1)], axis=1) for q in range(S5_PAIRS)]
    u2b = [v.astype(BF16) for v in u2]
    xl_scr[...] = functools.reduce(lambda a, b: a + b, [_dot(u2b[q], h_ref[0, q]) for q in range(S5_PAIRS)])
    a_re = are_ref[0]
    a_im = aim_ref[0]

    def body(a, carry):
        re, im = carry
        x0_scr[pl.ds(a, 1), 0:half] = re
        x0_scr[pl.ds(a, 1), half:2 * half] = im
        return (a_re * re - a_im * im + xl_scr[pl.ds(a, 1), 0:half],
                a_re * im + a_im * re + xl_scr[pl.ds(a, 1), half:2 * half])

    zero = jnp.zeros((1, half), F32)
    lax.fori_loop(0, n_chunks, body, (zero, zero), unroll=8)
    x0 = x0_scr[...].astype(BF16)
    for p in range(S5_PAIRS):
        y = _dot(x0, e_ref[0, p]) + u2[p] * d_ref[0]
        for q in range(p + 1):
            y = y + _dot(u2b[q], kk_ref[0, p - q])
        ys = _gelu_tanh(y)
        out = ys * _sigmoid(_dot(ys.astype(BF16), gw_ref[0]) + gb_ref[0])
        o_ref[0, pl.ds(2 * p, n_chunks, stride=S5_CHUNK), :] = out[:, :LANES].astype(o_ref.dtype)
        o_ref[0, pl.ds(2 * p + 1, n_chunks, stride=S5_CHUNK), :] = out[:, LANES:].astype(o_ref.dtype)


def _s5s_tables(lam_re, lam_im, log_dt, b_re, b_im, c_re, c_im, d_skip, glu_w, glu_b):
    T, C, P, LT = S5_CHUNK, S5_GROUP, S5_STATE, S5_LT
    G = lam_re.shape[0]
    NT = G // LT
    lam = lax.complex(lam_re.astype(F32), lam_im.astype(F32))
    dt = jnp.exp(log_dt.astype(F32))[:, None]
    lam_bar = jnp.exp(lam * dt)
    b_bar = ((lam_bar - 1.0) / lam)[..., None] * lax.complex(b_re.astype(F32), b_im.astype(F32))
    c_mat = lax.complex(c_re.astype(F32), c_im.astype(F32))
    taus = jnp.arange(T + 1, dtype=F32)
    pw = jnp.exp((lam * dt)[:, None, :] * taus[None, :, None])
    eye = jnp.eye(LT, dtype=F32)
    tiles = lambda a: a.reshape((NT, LT) + a.shape[1:])

    kern = jnp.einsum('gcp,gtp,gpd->gtdc', c_mat, pw[:, :T], b_bar, precision=HIGHEST).real
    kblk = jnp.einsum('nitdc,ij->ntidjc', tiles(kern), eye).reshape(NT, T, LANES, LANES)
    kblk = jnp.concatenate([jnp.zeros_like(kblk[:, :1]), kblk], axis=1)
    kk = jnp.stack([jnp.concatenate([jnp.concatenate([kblk[:, 2 * d + 1], kblk[:, 2 * d + 2]], axis=2),
                                     jnp.concatenate([kblk[:, 2 * d], kblk[:, 2 * d + 1]], axis=2)], axis=1)
                    for d in range(T // 2)], axis=1)

    hmat = pw[:, :T][:, ::-1, :, None] * b_bar[:, None]

    def state_cols(m):
        return jnp.einsum('nispc,ij->nsicjp', tiles(m), eye).reshape(NT, T, LANES, LT * P)

    h = jnp.concatenate([state_cols(hmat.real), state_cols(hmat.imag)], axis=3)
    h2 = h.reshape(NT, T // 2, 2 * LANES, 2 * LT * P)

    emat = c_mat[:, None] * pw[:, 1:][:, :, None, :]

    def state_rows(m):
        return jnp.einsum('nitcp,ij->ntjpic', tiles(m), eye).reshape(NT, T, LT * P, LANES)

    e = jnp.concatenate([state_rows(emat.real), state_rows(-emat.imag)], axis=2)
    e2 = e.reshape(NT, T // 2, 2, 2 * LT * P, LANES).transpose(0, 1, 3, 2, 4).reshape(NT, T // 2, 2 * LT * P, 2 * LANES)

    a_re = pw[:, T].real.reshape(NT, 1, LT * P)
    a_im = pw[:, T].imag.reshape(NT, 1, LT * P)
    pair = lambda v: jnp.tile(v.astype(F32).reshape(NT, 1, LANES), (1, 1, 2))
    gwb = jnp.einsum('nice,ij->nicje', tiles(glu_w.astype(F32)), eye).reshape(NT, LANES, LANES)
    zeros = jnp.zeros_like(gwb)
    gw2 = jnp.concatenate([jnp.concatenate([gwb, zeros], axis=2), jnp.concatenate([zeros, gwb], axis=2)], axis=1)
    return (h2.astype(BF16), e2.astype(BF16), kk.astype(BF16), a_re, a_im, pair(d_skip), gw2.astype(BF16), pair(glu_b))


def _s5s(u, tables):
    B, S, W = u.shape
    NT = W // LANES
    n_chunks = S // S5_CHUNK
    per_tile = lambda a: pl.BlockSpec((1,) + a.shape[1:], lambda j, b: (j,) + (0,) * (a.ndim - 1))
    return pl.pallas_call(
        _s5s_kernel,
        out_shape=jax.ShapeDtypeStruct((B, S, W), F32),
        grid=(NT, B),
        in_specs=[pl.BlockSpec((1, S, LANES), lambda j, b: (b, 0, j))] + [per_tile(t) for t in tables],
        out_specs=pl.BlockSpec((1, S, LANES), lambda j, b: (b, 0, j)),
        scratch_shapes=[pltpu.VMEM((n_chunks, 2 * S5_LT * S5_STATE), F32) for _ in range(2)],
        compiler_params=_cparams(("parallel", "parallel")),
        name="s5",
    )(u, *tables)


def _compress_kernel(x_ref, plo_ref, phi_ref, w1_ref, b1_ref, w2_ref, b2_ref, o_ref):
    x = x_ref[0, 0]
    half = x.shape[1]
    w1 = w1_ref[0]
    lo = _dot((x + plo_ref[0]).astype(BF16), w1[:half])
    hi = _dot((x + phi_ref[0]).astype(BF16), w1[half:])
    rows = x.shape[0]
    hid = _gelu_tanh(lo + pltpu.roll(hi, rows - 1, 0) + b1_ref[0])
    o_ref[0, 0] = _dot(hid.astype(BF16), w2_ref[0]) + b2_ref[0]


def _compress(xg, pos, w1, b1, w2, b2):
    _, B, rows, width = xg.shape
    pos_flat = pos.reshape(2, 2, 1, width).astype(F32)
    sel = lambda shape: pl.BlockSpec((1,) + shape, lambda j, b: (j, 0, 0))
    return pl.pallas_call(
        _compress_kernel,
        out_shape=jax.ShapeDtypeStruct((2, B, rows, NSA_DH), F32),
        grid=(2, B),
        in_specs=[pl.BlockSpec((1, 1, rows, width), lambda j, b: (j, b, 0, 0)),
                  sel((1, width)), sel((1, width)),
                  sel((2 * width, CMP_HIDDEN)), sel((1, CMP_HIDDEN)),
                  sel((CMP_HIDDEN, NSA_DH)), sel((1, NSA_DH))],
        out_specs=pl.BlockSpec((1, 1, rows, NSA_DH), lambda j, b: (j, b, 0, 0)),
        compiler_params=_cparams(("parallel", "parallel")),
        name="nsa_compress",
    )(xg, pos_flat[:, 0], pos_flat[:, 1], w1.astype(BF16), b1[:, None].astype(F32),
      w2.astype(BF16), b2[:, None].astype(F32))


def _t5_bucket(dist):
    dist = jnp.maximum(dist, 0)
    max_exact = REL_BUCKETS // 2
    log_ratio = jnp.log(jnp.maximum(dist, 1).astype(F32) / max_exact) / math.log(REL_MAX_DIST / max_exact)
    large = jnp.minimum(max_exact + (log_ratio * (REL_BUCKETS - max_exact)).astype(jnp.int32), REL_BUCKETS - 1)
    return jnp.where(dist < max_exact, dist, large)


def _nsa_proj_kernel(x_ref, g_ref, sh_ref, sc_ref, wq_ref, wk_ref, wv_ref, wg_ref, bg_ref,
                     q4_ref, gv_ref, kc_ref, vc_ref, ks_ref, kw_ref, vst_ref, vwt_ref):
    KV, R, DH, T = NSA_KV, NSA_R, NSA_DH, ATT_TILE
    h = _modulated_norm(x_ref[0], g_ref[...], sh_ref[0], sc_ref[0]).astype(BF16)
    q_t = (_dot(h, wq_ref[...]) * (DH ** -0.5 * LOG2E)).T.astype(BF16)
    gates_t = _sigmoid(_dot(h, wg_ref[...]) + bg_ref[...]).T
    row = lax.broadcasted_iota(jnp.int32, (SUBLANES, R * T), 0)
    for g in range(KV):
        q4_ref[0, g, 0] = jnp.concatenate([q_t[(g * R + r) * DH:(g * R + r + 1) * DH] for r in range(R)], axis=1)
        gv = jnp.zeros((SUBLANES, R * T), F32)
        for j in range(3):
            gj = jnp.concatenate([gates_t[g * LANES + 3 * r + j:g * LANES + 3 * r + j + 1] for r in range(R)], axis=1)
            gv = jnp.where(row == j, gj, gv)
        gv_ref[0, g, 0] = gv
    k3 = _dot(h, wk_ref[...])
    v3 = _dot(h, wv_ref[...])
    vs_t = v3[:, KV_W:2 * KV_W].T.astype(BF16)
    vw_t = v3[:, 2 * KV_W:].T.astype(BF16)
    for g in range(KV):
        cols = slice(g * DH, (g + 1) * DH)
        kc_ref[0, g] = k3[:, cols].astype(BF16)
        vc_ref[0, g] = v3[:, cols].astype(BF16)
        ks_ref[0, g] = k3[:, KV_W + g * DH:KV_W + (g + 1) * DH].astype(BF16)
        kw_ref[0, g] = k3[:, 2 * KV_W + g * DH:2 * KV_W + (g + 1) * DH].astype(BF16)
        vst_ref[0, g, 0] = vs_t[cols]
        vwt_ref[0, g, 0] = vw_t[cols]


def _nsa_proj(x, g, shift, scale, weights, b_gate):
    B, S, D = x.shape
    KV, R, DH, T = NSA_KV, NSA_R, NSA_DH, ATT_TILE
    vec = pl.BlockSpec((1, 1, D), lambda b, i: (b, 0, 0))
    keys = pl.BlockSpec((1, KV, T, DH), lambda b, i: (b, 0, i, 0))
    key_shape = jax.ShapeDtypeStruct((B, KV, S, DH), BF16)
    tile = lambda rows, width: pl.BlockSpec((1, KV, 1, rows, width), lambda b, i: (b, 0, i, 0, 0))
    tile_shape = lambda rows, width, dt: jax.ShapeDtypeStruct((B, KV, S // T, rows, width), dt)
    return pl.pallas_call(
        _nsa_proj_kernel,
        out_shape=[tile_shape(DH, R * T, BF16), tile_shape(SUBLANES, R * T, F32),
                   key_shape, key_shape, key_shape, key_shape,
                   tile_shape(DH, T, BF16), tile_shape(DH, T, BF16)],
        grid=(B, S // T),
        in_specs=[pl.BlockSpec((1, T, D), lambda b, i: (b, i, 0)),
                  pl.BlockSpec((1, D), lambda b, i: (0, 0)), vec, vec]
                 + [pl.BlockSpec(w.shape, lambda b, i: (0, 0)) for w in weights]
                 + [pl.BlockSpec(b_gate.shape, lambda b, i: (0, 0))],
        out_specs=[tile(DH, R * T), tile(SUBLANES, R * T), keys, keys, keys, keys, tile(DH, T), tile(DH, T)],
        compiler_params=_cparams(("parallel", "parallel")),
        name="nsa_proj",
    )(x, g.reshape(1, D), shift, scale, *weights, b_gate)


def _nsa_t_kernel(q4_ref, gv_ref, *rest):
    o_ref = rest[11]
    for sub in range(ATT_STEP_TILES):
        _nsa_tile(pl.program_id(2) * ATT_STEP_TILES + sub, q4_ref.at[:, :, pl.ds(sub, 1)], gv_ref.at[:, :, pl.ds(sub, 1)],
                  *rest[:11], o_ref.at[:, pl.ds(sub * ATT_TILE, ATT_TILE), :], *rest[12:])


def _nsa_tile(qi, q4_ref, gv_ref, kc_ref, vct_ref, ks_ref, vst_ref, kw_ref, vwt_ref,
              cfar_ref, band_ref, selb_ref, winb_ref, ovt_ref, o_ref, s_scr, sel_scr, sbuf):
    T = ATT_TILE
    R, DH = NSA_R, NSA_DH
    q0 = qi * T
    n_pad = kc_ref.shape[2]
    n_sel = ovt_ref.shape[0]
    n_far = selb_ref.shape[0] - 1
    n_win = winb_ref.shape[0] - 2
    band_rows = band_ref.shape[2] - T // CMP_STRIDE * 2

    q4 = q4_ref[0, 0, 0]
    t_lane = q0 + lax.broadcasted_iota(jnp.int32, (1, R * T), 1) % T

    ones_rows = DH
    with_ones = lambda v_t: jnp.concatenate([v_t, jnp.ones((ones_rows, v_t.shape[1]), v_t.dtype)], axis=0)
    gvec = lambda j: gv_ref[0, 0, 0, j:j + 1, :]

    grp = T // CMP_STRIDE
    s_scr[0:n_pad, :] = _dot(kc_ref[0, 0], q4) + cfar_ref[0]
    s_scr[n_pad:n_pad + 2 * grp, :] = jnp.zeros((2 * grp, R * T), F32)
    r0 = jnp.maximum(qi * grp - 2 * grp, 0)
    x0 = r0 - (qi * grp - 2 * grp)
    r0 = pl.multiple_of(r0, SUBLANES)
    x0 = pl.multiple_of(x0, SUBLANES)
    s_scr[pl.ds(r0, band_rows), :] += band_ref[0, 0, pl.ds(x0, band_rows), :]
    lim = pl.multiple_of(qi * grp + 2 * grp, SUBLANES)
    s_scr[pl.ds(lim, n_pad), :] = jnp.full((n_pad, R * T), NEG, F32)

    w_subs, w_vals = [], []
    for d in range(n_win + 1):
        kt = jnp.maximum(qi - d, 0)
        off = pl.multiple_of(kt * T, T)
        tile = jnp.where(qi >= d, d, n_win + 1)
        w_subs.append((_dot(kw_ref[0, 0, pl.ds(off, T), :], q4) + winb_ref[tile, 0]).astype(BF16))
        w_vals.append(with_ones(vwt_ref[0, 0, kt]))

    s = s_scr[0:n_pad, :]
    e = jnp.exp2(s - jnp.max(s, axis=0, keepdims=True))
    inv = jnp.where(t_lane >= CMP_BLOCK - 1, 1.0 / jnp.sum(e, axis=0, keepdims=True), 0.0)
    p = e * inv
    o_cmp = _dot(vct_ref[0, 0], p.astype(BF16))
    psum = functools.reduce(lambda a, b: a + b, [p[:, r * T:(r + 1) * T] for r in range(R)])

    m_w = jnp.max(functools.reduce(jnp.maximum, w_subs), axis=0, keepdims=True)
    acc = functools.reduce(lambda a, b: a + b,
                           [_dot(vj, jnp.exp2(sj - m_w)) for sj, vj in zip(w_subs, w_vals)])
    o_win = acc[:DH] * (1.0 / acc[DH:DH + 1])
    out_t = gvec(0) * o_cmp + gvec(2) * o_win

    imp_t = _dot(ovt_ref[...], psum, precision=HIGHEST)
    jj = lax.broadcasted_iota(jnp.int32, (n_sel, T), 0)
    blk_t = (q0 + lax.broadcasted_iota(jnp.int32, (1, T), 1)) // SEL_BLOCK
    forced = (jj == 0) | (jj == blk_t) | (jj == blk_t - 1)
    score = jnp.where(forced, FORCE, jnp.where(jj <= blk_t, imp_t, -1.0))
    n_blk = n_sel // SUBLANES
    rows = [score[v * SUBLANES:(v + 1) * SUBLANES] for v in range(n_blk)]
    cnts = [jnp.zeros((SUBLANES, T), F32) for _ in range(n_blk)]
    sub = lax.broadcasted_iota(jnp.int32, (SUBLANES, T), 0)
    for j2 in range(n_sel):
        c2 = score[j2:j2 + 1, :]
        for v in range(n_blk):
            lo = v * SUBLANES
            if lo > j2:
                beats = c2 >= rows[v]
            elif lo + SUBLANES - 1 <= j2:
                beats = c2 > rows[v]
            else:
                beats = (c2 > rows[v]) | ((c2 >= rows[v]) & (sub > j2 - lo))
            cnts[v] = cnts[v] + jnp.where(beats, 1.0, 0.0)
    cnt = jnp.concatenate(cnts, axis=0)
    chosen = (cnt < float(min(SEL_TOPK, n_sel))) & (jj <= blk_t)
    sel_scr[...] = jnp.where(chosen, 0.0, -BIG)

    def block_mask(kt):
        per_tile = T // SEL_BLOCK
        parts = [jnp.broadcast_to(sel_scr[pl.ds(kt * per_tile + i, 1), :], (SEL_BLOCK, T)) for i in range(per_tile)]
        m1 = jnp.concatenate(parts, axis=0)
        return jnp.concatenate([m1] * R, axis=1)

    def sel_scores(slot, kc):
        off = pl.multiple_of(kc * T, T)
        s = _dot(ks_ref[0, 0, pl.ds(off, T), :], q4)
        s = (s + selb_ref[jnp.clip(qi - kc, 0, n_far), 0] + block_mask(kc)).astype(BF16)
        sbuf[slot] = s
        return jnp.max(s, axis=0, keepdims=True).astype(F32)

    def sel_weighted(slot, kc, m_new):
        return _dot(with_ones(vst_ref[0, 0, kc]), jnp.exp2(sbuf[slot] - m_new.astype(BF16)))

    last_tile = vst_ref.shape[2] - 1

    def sel_body(i, carry):
        m, acc, m_even = carry
        m_odd = sel_scores(1, 2 * i + 1)
        m_new = jnp.maximum(m, m_even)
        acc = jnp.exp2(m - m_new) * acc + sel_weighted(0, 2 * i, m_new)
        m_even = sel_scores(0, jnp.minimum(2 * i + 2, last_tile))
        m_fin = jnp.maximum(m_new, m_odd)
        acc = jnp.exp2(m_new - m_fin) * acc + sel_weighted(1, 2 * i + 1, m_fin)
        return m_fin, acc, m_even

    _, acc, _ = lax.fori_loop(0, qi // 2 + 1, sel_body,
                              (jnp.full((1, R * T), NEG, F32), jnp.zeros((DH + ones_rows, R * T), F32),
                               sel_scores(0, 0)))
    out_t = out_t + gvec(1) * (acc[:DH] * (1.0 / acc[DH:DH + 1]))
    for pr in range(R // 2):
        pair = jnp.concatenate([out_t[:, (2 * pr) * T:(2 * pr + 1) * T],
                                out_t[:, (2 * pr + 1) * T:(2 * pr + 2) * T]], axis=0)
        o_ref[0, :, pr * 2 * DH:(pr + 1) * 2 * DH] = pair.T.astype(o_ref.dtype)


def _bias_lookup(table, dist):
    onehot = (_t5_bucket(dist)[..., None] == jnp.arange(table.shape[0])).astype(F32)
    return jnp.einsum('...k,kh->...h', onehot, table, precision=HIGHEST)


def _nsa_t_tables(rel_bias, S):
    T, R, KV = ATT_TILE, NSA_R, NSA_KV
    table = rel_bias.astype(F32) * LOG2E
    ii = jnp.arange(T)
    delta = ii[None, :] - ii[:, None]

    def lanes(a):
        a = jnp.moveaxis(a, -1, 0)
        a = a.reshape((KV, R) + a.shape[1:])
        return jnp.moveaxis(a, 1, 2).reshape(KV, a.shape[2], R * a.shape[3])

    def tile(off):
        return lanes(_bias_lookup(table, off * T + delta))

    mask4 = lambda ok: jnp.tile(jnp.where(ok, 0.0, NEG), (1, R))[None]
    n_far = -(-REL_MAX_DIST // T) + 1
    selb = [tile(o) for o in range(n_far + 1)]
    selb[0] = selb[0] + mask4(delta >= 0)
    selb = jnp.stack(selb, axis=0)
    n_win = WINDOW // T
    winb = [tile(o) + mask4((o * T + delta >= 0) & (o * T + delta < WINDOW)) for o in range(n_win + 1)]
    winb.append(jnp.full_like(winb[0], NEG))
    winb = jnp.stack(winb, axis=0)

    grp = T // CMP_STRIDE
    far = _bias_lookup(table, jnp.asarray(2 * REL_MAX_DIST))
    xx = jnp.arange(4 * grp)
    bdist = ii[None, :] - CMP_STRIDE * (xx[:, None] - 2 * grp) - (CMP_BLOCK - 1)
    band = jnp.where((bdist >= 0)[..., None], _bias_lookup(table, bdist) - far, NEG)
    band = jnp.concatenate([lanes(band), jnp.zeros((KV, 2 * grp, R * T), F32)], axis=1)[:, None]
    cfar = jnp.repeat(far.reshape(KV, R), T, axis=1)[:, None]

    n_pad = S // CMP_STRIDE
    n_sel = S // SEL_BLOCK
    cmp_start = jnp.arange(n_pad) * CMP_STRIDE
    sel_start = jnp.arange(n_sel) * SEL_BLOCK
    overlap = jnp.clip(jnp.minimum(cmp_start[:, None] + CMP_BLOCK, sel_start[None] + SEL_BLOCK)
                       - jnp.maximum(cmp_start[:, None], sel_start[None]), 0).astype(F32) / CMP_BLOCK
    n_cmp = (S - CMP_BLOCK) // CMP_STRIDE + 1
    overlap_t = jnp.where((jnp.arange(n_pad) < n_cmp)[:, None], overlap, 0.0).T
    return cfar, band, selb, winb, overlap_t


def _nsa_t_attention(q4, gv, kcmp, vcmp_t, ks, vs_t, kw, vw_t, tables):
    B, KV, S, _ = kw.shape
    T = ATT_TILE
    cfar, band, selb, winb, overlap_t = tables
    gw = NSA_R * NSA_DH
    n_pad = kcmp.shape[2]
    seq = lambda a: pl.BlockSpec((1, 1) + a.shape[2:], lambda b, g, i: (b, g) + (0,) * (a.ndim - 2))
    qtile = lambda a: pl.BlockSpec((1, 1, ATT_STEP_TILES) + a.shape[3:], lambda b, g, i: (b, g, i, 0, 0))
    grp = lambda a: pl.BlockSpec((1,) + a.shape[1:], lambda b, g, i: (g,) + (0,) * (a.ndim - 1))
    tiles = lambda a: pl.BlockSpec((a.shape[0], 1) + a.shape[2:], lambda b, g, i: (0, g, 0, 0))
    full = lambda a: pl.BlockSpec(a.shape, lambda b, g, i: (0,) * a.ndim)
    return pl.pallas_call(
        _nsa_t_kernel,
        out_shape=jax.ShapeDtypeStruct((B, S, KV * gw), BF16),
        grid=(B, KV, S // (T * ATT_STEP_TILES)),
        in_specs=[qtile(q4), qtile(gv),
                  seq(kcmp), seq(vcmp_t), seq(ks), seq(vs_t), seq(kw), seq(vw_t),
                  grp(cfar), grp(band), tiles(selb), tiles(winb), full(overlap_t)],
        out_specs=pl.BlockSpec((1, T * ATT_STEP_TILES, gw), lambda b, g, i: (b, i, g)),
        scratch_shapes=[pltpu.VMEM((2 * n_pad + 2 * (T // CMP_STRIDE), NSA_R * T), F32),
                        pltpu.VMEM((S // SEL_BLOCK, T), F32),
                        pltpu.VMEM((2, T, NSA_R * T), BF16)],
        compiler_params=_cparams(("parallel", "parallel", "arbitrary")),
        name="nsa_attention",
    )(q4, gv, kcmp, vcmp_t, ks, vs_t, kw, vw_t, cfar, band, selb, winb, overlap_t)


def _moe_kernel(*refs, n_in, final):
    x_ref, mgate_ref = refs[:2]
    a_refs = refs[2:2 + n_in]
    wo_refs = refs[2 + n_in:2 + 2 * n_in]
    (g_ref, sh_ref, sc_ref, gate_ref, wr_ref, br_ref, before_ref, wg_ref, wu_ref, wd_ref, fg_ref,
     o_ref, x_all, hs_all, rts_all, acc_all, perm_t_all, meta_all) = refs[2 + 2 * n_in:]
    NG, PG, FH = MOE_GROUPS, MOE_PER_GROUP, MOE_HIDDEN
    TP = x_all.shape[0]
    s = pl.program_id(2)

    def prologue(hh):
        x_scr, hs_scr, rts_scr, acc_scr, perm_t_scr = (r.at[hh] for r in (x_all, hs_all, rts_all, acc_all, perm_t_all))
        meta = meta_all.at[hh]
        mix = functools.reduce(lambda a, b: a + b,
                               [_dot(a_ref[0].astype(BF16), wo_ref[...]) for a_ref, wo_ref in zip(a_refs, wo_refs)])
        x = x_ref[0] + mgate_ref[0] * mix
        x_scr[...] = x
        h = _modulated_norm(x, g_ref[...], sh_ref[0], sc_ref[0])
        h_hi = h.astype(BF16)
        h_lo = (h - h_hi.astype(F32)).astype(BF16)
        both = _dot(h_hi, wr_ref[...])
        logits = (both[:, :LANES] + both[:, LANES:] + _dot(h_lo, wr_ref[:, :LANES]) + br_ref[...]).T
        gl = [logits[NG * PG + g:NG * PG + g + 1, :] for g in range(NG)]
        gmax = functools.reduce(jnp.maximum, gl)
        gtop = jnp.full_like(gmax, float(NG - 1))
        for g in reversed(range(NG - 1)):
            gtop = jnp.where(gl[g] == gmax, float(g), gtop)
        p_g = 1.0 / functools.reduce(lambda a, b: a + b, [jnp.exp(v - gmax) for v in gl])
        a = []
        for j in range(PG):
            v = logits[(NG - 1) * PG + j:(NG - 1) * PG + j + 1, :]
            for g in reversed(range(NG - 1)):
                v = jnp.where(gtop == float(g), logits[g * PG + j:g * PG + j + 1, :], v)
            a.append(v)

        def first_max(vals):
            vmax = functools.reduce(jnp.maximum, vals)
            taken = jnp.zeros_like(vmax) > 1.0
            hits = []
            for v in vals:
                hit = (v == vmax) & jnp.logical_not(taken)
                taken = taken | hit
                hits.append(hit)
            return vmax, hits

        v1, hit1 = first_max(a)
        rest = [jnp.where(hh, -jnp.inf, v) for hh, v in zip(hit1, a)]
        v2, hit2 = first_max(rest)
        e2 = jnp.exp(v2 - v1)
        w1 = p_g / (1.0 + e2)
        w2 = p_g * e2 / (1.0 + e2)
        tm = gtop.shape[1]
        row = lax.broadcasted_iota(jnp.int32, (SUBLANES, tm), 0)
        onehot = [jnp.where(gtop == float(g), 1.0, 0.0) for g in range(NG)]
        oh8 = jnp.zeros((SUBLANES, tm), F32)
        for g in range(NG):
            oh8 = jnp.where(row == g, onehot[g], oh8)
        before = _dot(oh8.astype(BF16), before_ref[...])
        pos = jnp.zeros_like(gtop)
        off = jnp.int32(0)
        for g in range(NG):
            cnt = jnp.sum(onehot[g]).astype(jnp.int32)
            meta[g] = off
            meta[NG + g] = cnt
            pos = pos + onehot[g] * (before[g:g + 1, :] + off.astype(F32))
            off = off + cnt
        rt = jnp.where(row == PG, gtop, jnp.where(row == PG + 1, pos, 0.0))
        for j in range(PG):
            wj = jnp.where(hit1[j], w1, jnp.where(hit2[j], w2, 0.0))
            rt = jnp.where(row == j, wj, rt)
        rt_tok = jnp.concatenate([rt, jnp.zeros((LANES - SUBLANES, tm), F32)], axis=0).T
        rid = lax.broadcasted_iota(jnp.int32, (tm, tm), 0).astype(F32)
        cid = lax.broadcasted_iota(jnp.int32, (tm, tm), 1).astype(F32)
        perm = jnp.where(rid == pos, 1.0, 0.0).astype(BF16)
        perm_t = jnp.where(rt_tok[:, PG + 1:PG + 2] == cid, 1.0, 0.0).astype(BF16)
        perm_t_scr[...] = perm_t
        pad = hs_scr.shape[0] - tm
        hs_scr[0:tm, :] = _dot(perm, h_hi).astype(BF16)
        hs_scr[tm:, :] = jnp.zeros((pad, h_hi.shape[1]), BF16)
        r1 = rt.astype(BF16)
        res = rt - r1.astype(F32)
        r2 = res.astype(BF16)
        r3 = (res - r2.astype(F32)).astype(BF16)
        rt_sorted = _dot(r1, perm_t) + _dot(r2, perm_t) + _dot(r3, perm_t)
        rts_scr[0:tm, :] = jnp.concatenate([rt_sorted, jnp.zeros((LANES - SUBLANES, tm), F32)], axis=0).T
        rts_scr[tm:, :] = jnp.full((pad, LANES), -1.0, F32)
        acc_scr[...] = jnp.zeros(acc_scr.shape, F32)

    tm = x_all.shape[1]
    WIN = hs_all.shape[1] - tm

    def experts(c):
        cf = c.astype(F32)
        bases, counts = [], []
        for hh in range(TP):
            off = meta_all[hh, c]
            cnt = meta_all[hh, NG + c]
            base = (off // MOE_ALIGN) * MOE_ALIGN
            bases.append(base)
            counts.append(jnp.where(cnt > 0, (off + cnt - base + WIN - 1) // WIN, 0))

        def win_body(w, carry):
            starts = [pl.multiple_of(jnp.where(w < counts[hh], bases[hh] + w * WIN, tm), MOE_ALIGN) for hh in range(TP)]
            hs = jnp.concatenate([hs_all[hh, pl.ds(starts[hh], WIN), :] for hh in range(TP)], axis=0)
            rt = jnp.concatenate([rts_all[hh, pl.ds(starts[hh], WIN), :] for hh in range(TP)], axis=0)
            in_group = rt[:, PG:PG + 1] == cf
            hid = _silu(_dot(hs, wg_ref[0])) * _dot(hs, wu_ref[0])
            parts = [hid[:, j * FH:(j + 1) * FH] * jnp.where(in_group, rt[:, j:j + 1], 0.0) for j in range(PG)]
            out = _dot(jnp.concatenate(parts, axis=1).astype(BF16), wd_ref[0])
            for hh in range(TP):
                acc_all[hh, pl.ds(starts[hh], WIN), :] += out[hh * WIN:(hh + 1) * WIN]
            return carry

        lax.fori_loop(0, functools.reduce(jnp.maximum, counts), win_body, 0)

    def epilogue(hh):
        ys = acc_all[hh, 0:tm, :]
        ys_hi = ys.astype(BF16)
        ys_lo = (ys - ys_hi.astype(F32)).astype(BF16)
        back = _dot(perm_t_all[hh], jnp.concatenate([ys_hi, ys_lo], axis=1))
        d = ys.shape[1]
        y = x_all[hh] + gate_ref[0] * (back[:, :d] + back[:, d:])
        if final:
            y = y * lax.rsqrt(jnp.mean(y * y, axis=-1, keepdims=True) + EPS) * fg_ref[...]
        o_ref[0, hh * tm:(hh + 1) * tm, :] = y

    for hh in range(TP):
        pl.when(s == hh)(functools.partial(prologue, hh))
    pl.when((s >= TP - 1) & (s <= TP + NG - 2))(lambda: experts(s - (TP - 1)))
    for hh in range(TP):
        pl.when(s == NG + TP - 2)(functools.partial(epilogue, hh))


def _moe(x, mix_gate, acts, w_outs, g, shift, scale, gate, wg, bg, we, be, w_gate, w_up, w_down, final_g, final,
         tm=512):
    B, S, D = x.shape
    n_in = len(acts)
    NG, PG, FH = MOE_GROUPS, MOE_PER_GROUP, MOE_HIDDEN
    wr = jnp.zeros((D, LANES), F32)
    wr = wr.at[:, :NG * PG].set(we.reshape(D, NG * PG).astype(F32)).at[:, NG * PG:NG * PG + NG].set(wg.astype(F32))
    br = jnp.zeros((1, LANES), F32)
    br = br.at[0, :NG * PG].set(be.reshape(NG * PG).astype(F32)).at[0, NG * PG:NG * PG + NG].set(bg.astype(F32))
    wr_hi = wr.astype(BF16)
    wr = jnp.concatenate([wr_hi, (wr - wr_hi.astype(F32)).astype(BF16)], axis=1)
    grp = lambda w: w.reshape(NG, PG, D, FH).transpose(0, 2, 1, 3).reshape(NG, D, PG * FH).astype(BF16)
    wd = w_down.reshape(NG, PG * FH, D).astype(BF16)
    ids = jnp.arange(tm)
    before = (ids[:, None] < ids[None, :]).astype(BF16)
    TP = MOE_TILES
    n_steps = NG + TP - 1
    vec = pl.BlockSpec((1, 1, D), lambda b, i, s: (b, 0, 0))
    row = pl.BlockSpec((1, D), lambda b, i, s: (0, 0))
    wspec = lambda k, n: pl.BlockSpec((1, k, n), lambda b, i, s: (jnp.clip(s - (TP - 1), 0, NG - 1), 0, 0))
    tokens_in = lambda n: pl.BlockSpec((1, tm, n), lambda b, i, s: (b, i * TP + jnp.minimum(s, TP - 1), 0))
    tokens_out = pl.BlockSpec((1, TP * tm, D), lambda b, i, s: (b, i, 0))
    const = lambda a: pl.BlockSpec(a.shape, lambda b, i, s: (0,) * a.ndim)
    return pl.pallas_call(
        functools.partial(_moe_kernel, n_in=n_in, final=final),
        out_shape=jax.ShapeDtypeStruct((B, S, D), F32),
        grid=(B, S // (tm * TP), n_steps),
        in_specs=[tokens_in(D), vec] + [tokens_in(a.shape[2]) for a in acts] + [const(w) for w in w_outs]
                 + [row, vec, vec, vec, const(wr), const(br), const(before),
                    wspec(D, PG * FH), wspec(D, PG * FH), wspec(PG * FH, D), row],
        out_specs=tokens_out,
        scratch_shapes=[pltpu.VMEM((TP, tm, D), F32), pltpu.VMEM((TP, tm + MOE_WIN, D), BF16),
                        pltpu.VMEM((TP, tm + MOE_WIN, LANES), F32), pltpu.VMEM((TP, tm + MOE_WIN, D), F32),
                        pltpu.VMEM((TP, tm, tm), BF16), pltpu.SMEM((TP, 2 * NG), jnp.int32)],
        compiler_params=_cparams(("parallel", "parallel", "arbitrary")),
        name="moe",
    )(x, mix_gate, *acts, *w_outs, g.reshape(1, D), shift, scale, gate, wr, br, before, grp(w_gate), grp(w_up), wd,
      final_g.reshape(1, D))


def _mlstm_s5_layer(x, g, shift, scale, w_in, conv_w, b_i, b_f, head_g, s5_params, w_out):
    H = MLSTM_HEADS
    A = MIX_A
    w_if = jnp.zeros((D_MODEL, LANES), F32).at[:, :2 * H].set(w_in[:, 4 * A:4 * A + 2 * H])
    weights = [w_in[:, :2 * A], w_in[:, 2 * A:4 * A], w_if, w_in[:, 4 * A + 2 * H:]]
    qk, vo, ifg, u = _norm_matmul(x, g, shift, scale, [w.astype(BF16) for w in weights], [BF16, BF16, F32, F32])
    gate_bias = jnp.zeros((1, LANES), F32).at[0, :H].set(b_i.astype(F32)).at[0, H:2 * H].set(b_f.astype(F32))
    hm = _mlstm(qk, vo, ifg, conv_w.astype(F32), gate_bias, head_g.reshape(1, A).astype(F32))
    ys = _s5s(u, _s5s_tables(*s5_params))
    w_out = w_out.astype(BF16)
    return [hm, ys], [w_out[:A], w_out[A:]]


def _nsa_layer(x, g, shift, scale, w_in, b_gate, cmp_pos, cmp_w1, cmp_b1, cmp_w2, cmp_b2, rel_bias, w_out):
    B, S, D = x.shape
    KV, R, DH = NSA_KV, NSA_R, NSA_DH
    w_g = jnp.zeros((D, KV, LANES), F32).at[:, :, :3 * R].set(w_in[:, D + 6 * KV_W:].reshape(D, KV, 3 * R))
    b_g = jnp.zeros((KV, LANES), F32).at[:, :3 * R].set(b_gate.reshape(KV, 3 * R).astype(F32))
    kv_cols = lambda i: w_in[:, D + i * KV_W:D + (i + 1) * KV_W]
    w_k = jnp.concatenate([kv_cols(0), kv_cols(2), kv_cols(4)], axis=1)
    w_v = jnp.concatenate([kv_cols(1), kv_cols(3), kv_cols(5)], axis=1)
    weights = [w_in[:, :D], w_k, w_v, w_g.reshape(D, KV * LANES)]
    q4, gv, kc, vc, ks, kw, vs_t, vw_t = _nsa_proj(x, g, shift, scale, [w.astype(BF16) for w in weights],
                                                   b_g.reshape(1, KV * LANES))
    grp = CMP_STRIDE
    xg = jnp.stack([kc, vc]).reshape(2, B, KV * S // grp, grp * DH)
    cmp = _compress(xg, cmp_pos, cmp_w1, cmp_b1, cmp_w2, cmp_b2).reshape(2, B, KV, S // grp, DH).astype(BF16)
    out = _nsa_t_attention(q4, gv, cmp[0], cmp[1].transpose(0, 1, 3, 2), ks, vs_t, kw, vw_t,
                           _nsa_t_tables(rel_bias, S))
    return [out], [w_out.astype(BF16)]


def kernel(x, c, rel_bias, ada_w, ada_b, norm_g, final_g,
           a_w_in, a_conv, a_b_i, a_b_f, a_head_g,
           s5_lam_re, s5_lam_im, s5_log_dt, s5_b_re, s5_b_im, s5_c_re, s5_c_im,
           s5_d, s5_glu_w, s5_glu_b, a_w_out,
           n_w_in, n_b_gate, n_cmp_pos, n_cmp_w1, n_cmp_b1, n_cmp_w2, n_cmp_b2, n_w_out,
           r_grp_w, r_grp_b, r_exp_w, r_exp_b, e_w_gate, e_w_up, e_w_down):
    B, S, D = x.shape
    mod = _ada_mod(c, ada_w, ada_b).reshape(DEPTH, 2, B, 1, 3 * D)
    split = lambda m: (m[..., :D], m[..., D:2 * D], m[..., 2 * D:])
    for layer in range(DEPTH):
        shift, scale, mix_gate = split(mod[layer, 0])
        j = layer // 2
        if layer % 2 == 0:
            s5_params = (s5_lam_re[j], s5_lam_im[j], s5_log_dt[j], s5_b_re[j], s5_b_im[j],
                         s5_c_re[j], s5_c_im[j], s5_d[j], s5_glu_w[j], s5_glu_b[j])
            acts, w_outs = _mlstm_s5_layer(x, norm_g[layer, 0], shift, scale, a_w_in[j], a_conv[j], a_b_i[j],
                                           a_b_f[j], a_head_g[j], s5_params, a_w_out[j])
        else:
            acts, w_outs = _nsa_layer(x, norm_g[layer, 0], shift, scale, n_w_in[j], n_b_gate[j], n_cmp_pos[j],
                                      n_cmp_w1[j], n_cmp_b1[j], n_cmp_w2[j], n_cmp_b2[j], rel_bias, n_w_out[j])
        shift, scale, gate = split(mod[layer, 1])
        x = _moe(x, mix_gate, acts, w_outs, norm_g[layer, 1], shift, scale, gate, r_grp_w[layer], r_grp_b[layer],
                 r_exp_w[layer], r_exp_b[layer], e_w_gate[layer], e_w_up[layer], e_w_down[layer], final_g,
                 final=(layer == DEPTH - 1))
    return x
```
